```python
import jax, jax.numpy as jnp
from jax import lax
import numpy as np

D_MODEL = 1024
BATCH = 16
SEQ = 256
DEPTH = 2
DEC_BATCH = 4
DEC_SEQ = 1024
PAST_LEN = 512

GRID_W = 64
MIX_WIDTH = 1024
MLA_HEADS = 4
NOPE_DIM = 128
ROPE_DIM = 64
V_DIM = 128
QK_DIM = NOPE_DIM + ROPE_DIM
MLA_WIDTH = MLA_HEADS * V_DIM
Q_LORA = 384
KV_LORA = 256
POOL_GROUPS = 4
POOL_GROUP_DIM = 64
POOL_WIDTH = POOL_GROUPS * POOL_GROUP_DIM
POOL_WINDOWS = (2, 4, 8, 16)
CONV_WIDTH = 256
CONV_K = 3
IN_SPLITS = (Q_LORA, KV_LORA, ROPE_DIM, MLA_WIDTH, POOL_WIDTH, POOL_WIDTH,
             CONV_WIDTH, CONV_WIDTH, CONV_WIDTH, CONV_WIDTH)
IN_WIDTH = sum(IN_SPLITS)
ROPE_BASE = 10000.0
AXIS_DIM = ROPE_DIM // 2
Q_BLOCK = 128
ATTN_SCALE = QK_DIM ** -0.5
EPS = 1e-6

kernel_name = "hybrid_mla_pool_conv_diffusion_step"


def rmsnorm(x, g):
    x32 = x.astype(jnp.float32)
    y = x32 * lax.rsqrt(jnp.mean(x32 * x32, axis=-1, keepdims=True) + EPS)
    return (y * g.astype(jnp.float32)).astype(x.dtype)


def axial_rope(L):
    rows = L // GRID_W
    row = jnp.repeat(jnp.arange(rows), GRID_W).astype(jnp.float32)
    col = jnp.tile(jnp.arange(GRID_W), rows).astype(jnp.float32)
    inv = 1.0 / (ROPE_BASE ** (jnp.arange(0, AXIS_DIM, 2, dtype=jnp.float32) / AXIS_DIM))
    ang = jnp.concatenate([row[:, None] * inv, col[:, None] * inv], axis=-1)
    return jnp.cos(ang), jnp.sin(ang)


def apply_rope(x, cos, sin):
    x32 = x.astype(jnp.float32)
    x1, x2 = x32[..., :AXIS_DIM], x32[..., AXIS_DIM:]
    return jnp.concatenate([x1 * cos - x2 * sin, x2 * cos + x1 * sin], axis=-1).astype(x.dtype)


def mla_attention(q_nope, q_rope, k_nope, k_rope, v):
    B, Lq, H, _ = q_nope.shape
    nb = Lq // Q_BLOCK

    def blocks(a):
        return jnp.moveaxis(a.reshape((B, nb, Q_BLOCK) + a.shape[2:]), 1, 0)

    def one(args):
        qn, qr = args
        s = (jnp.einsum('bqhd,bkhd->bhqk', qn, k_nope)
             + jnp.einsum('bqhr,bkr->bhqk', qr, k_rope)).astype(jnp.float32) * ATTN_SCALE
        p = jax.nn.softmax(s, axis=-1).astype(v.dtype)
        return jnp.einsum('bhqk,bkhd->bqhd', p, v)

    out = lax.map(one, (blocks(q_nope), blocks(q_rope)))
    return jnp.moveaxis(out, 0, 1).reshape(B, Lq, H * V_DIM)


def pool_mix(p, pool_w, pool_s):
    B, L, _ = p.shape
    cs = jnp.concatenate([jnp.zeros((B, 1, POOL_WIDTH), jnp.float32),
                          jnp.cumsum(p.astype(jnp.float32), axis=1)], axis=1)
    t = np.arange(L)
    outs = []
    for gi, w in enumerate(POOL_WINDOWS):
        lo = np.maximum(t - w // 2, 0)
        hi = np.minimum(t + (w - w // 2), L)
        cnt = (hi - lo).astype(np.float32)
        csg = cs[..., gi * POOL_GROUP_DIM:(gi + 1) * POOL_GROUP_DIM]
        outs.append((csg[:, hi] - csg[:, lo]) / cnt[None, :, None])
    pooled = (jnp.stack(outs, axis=2).astype(p.dtype)
              - p.reshape(B, L, POOL_GROUPS, POOL_GROUP_DIM))
    y = jnp.einsum('btgc,gcd->btgd', pooled, pool_w).reshape(B, L, POOL_WIDTH)
    return y * pool_s


def short_conv(z, conv_w):
    zp = jnp.pad(z, ((0, 0), (1, 1), (0, 0)))
    return zp[:, :-2] * conv_w[0] + zp[:, 1:-1] * conv_w[1] + zp[:, 2:] * conv_w[2]


def mixer_layer(x, mod, g_norm, w_in, g_q, w_uq, g_kv, w_ukv, pool_w, pool_s, conv_w, w_out,
                ctx=None, rope=None):
    B, L, _ = x.shape
    shift, scale, gate = jnp.split(mod, 3, axis=-1)
    h = rmsnorm(x, g_norm) * (1.0 + scale[:, None]) + shift[:, None]
    u = h @ w_in
    offs = [int(o) for o in np.cumsum(IN_SPLITS)[:-1]]
    cq, ckv_raw, kr, g_mla, px, g_pool, cb, cc, ch, g_conv = jnp.split(u, offs, axis=-1)

    q = (rmsnorm(cq, g_q) @ w_uq).reshape(B, L, MLA_HEADS, QK_DIM)
    q_nope, q_rope = q[..., :NOPE_DIM], q[..., NOPE_DIM:]
    ckv = rmsnorm(ckv_raw, g_kv)
    if rope is not None:
        cos, sin = rope
        q_rope = apply_rope(q_rope, cos[:, None], sin[:, None])
        kr = apply_rope(kr, cos, sin)
    if ctx is None:
        ckv_all, kr_all = ckv, kr
    else:
        ckv_all = jnp.concatenate([ctx[0], ckv], axis=1)
        kr_all = jnp.concatenate([ctx[1], kr], axis=1)
    kv = (ckv_all @ w_ukv).reshape(B, ckv_all.shape[1], MLA_HEADS, NOPE_DIM + V_DIM)
    k_nope, v = kv[..., :NOPE_DIM], kv[..., NOPE_DIM:]
    attn = mla_attention(q_nope, q_rope, k_nope, kr_all, v)

    pool = pool_mix(px, pool_w, pool_s)

    conv = cb * short_conv(cc * ch, conv_w)

    mixed = jnp.concatenate([jax.nn.silu(g_mla) * attn,
                             jax.nn.silu(g_pool) * pool,
                             jax.nn.silu(g_conv) * conv], axis=-1)
    y = x + gate[:, None] * (mixed @ w_out)
    return y, ckv, kr


def setup_inputs(seed: int = 0) -> dict:
    key = jax.random.key(seed)
    ks = jax.random.split(key, 24)
    f32 = jnp.float32
    nrm = lambda k, s, sc: jax.random.normal(k, s, f32) * sc
    gain = lambda k, s: 1.0 + 0.1 * jax.random.normal(k, s, f32)
    return {
        "x_prompt": nrm(ks[0], (BATCH, SEQ, D_MODEL), 1.0),
        "x_sample": nrm(ks[1], (DEC_BATCH, DEC_SEQ, D_MODEL), 1.0),
        "cache_ckv": nrm(ks[2], (DEC_BATCH, DEPTH, PAST_LEN, KV_LORA), 1.0),
        "cache_krope": nrm(ks[3], (DEC_BATCH, DEPTH, PAST_LEN, ROPE_DIM), 1.0),
        "c": nrm(ks[4], (DEC_BATCH, D_MODEL), 1.0),
        "c_ctx": nrm(ks[5], (D_MODEL,), 1.0),
        "w_mod": nrm(ks[6], (DEPTH, D_MODEL, 3 * D_MODEL), 0.5 * D_MODEL ** -0.5),
        "b_mod": nrm(ks[7], (DEPTH, 3 * D_MODEL), 0.02),
        "g_norm": gain(ks[8], (DEPTH, D_MODEL)),
        "w_in": nrm(ks[9], (DEPTH, D_MODEL, IN_WIDTH), D_MODEL ** -0.5),
        "g_q": gain(ks[10], (DEPTH, Q_LORA)),
        "w_uq": nrm(ks[11], (DEPTH, Q_LORA, MLA_HEADS * QK_DIM), Q_LORA ** -0.5),
        "g_kv": gain(ks[12], (DEPTH, KV_LORA)),
        "w_ukv": nrm(ks[13], (DEPTH, KV_LORA, MLA_HEADS * (NOPE_DIM + V_DIM)), KV_LORA ** -0.5),
        "pool_w": nrm(ks[14], (DEPTH, POOL_GROUPS, POOL_GROUP_DIM, POOL_GROUP_DIM), POOL_GROUP_DIM ** -0.5),
        "pool_s": gain(ks[15], (DEPTH, POOL_WIDTH)),
        "conv_w": nrm(ks[16], (DEPTH, CONV_K, CONV_WIDTH), CONV_K ** -0.5),
        "w_out": nrm(ks[17], (DEPTH, MIX_WIDTH, D_MODEL), MIX_WIDTH ** -0.5),
        "g_final": gain(ks[18], (D_MODEL,)),
    }


def reference(x_prompt, x_sample, cache_ckv, cache_krope, c, c_ctx, w_mod, b_mod, g_norm, w_in,
              g_q, w_uq, g_kv, w_ukv, pool_w, pool_s, conv_w, w_out, g_final):
    xp = x_prompt
    ckv_states, kr_states = [], []
    for l in range(DEPTH):
        mod_ctx = (jax.nn.silu(c_ctx)[None] @ w_mod[l] + b_mod[l])
        xp, ckv, kr = mixer_layer(xp, mod_ctx, g_norm[l], w_in[l], g_q[l], w_uq[l], g_kv[l], w_ukv[l],
                                  pool_w[l], pool_s[l], conv_w[l], w_out[l])
        ckv_states.append(ckv)
        kr_states.append(kr)
    y_prompt = rmsnorm(xp, g_final)
    state_ckv = jnp.stack(ckv_states, axis=1)
    state_krope = jnp.stack(kr_states, axis=1)

    rope = axial_rope(x_sample.shape[1])
    xs = x_sample
    for l in range(DEPTH):
        mod_lat = jax.nn.silu(c) @ w_mod[l] + b_mod[l]
        xs, _, _ = mixer_layer(xs, mod_lat, g_norm[l], w_in[l], g_q[l], w_uq[l], g_kv[l], w_ukv[l],
                               pool_w[l], pool_s[l], conv_w[l], w_out[l],
                               ctx=(cache_ckv[:, l], cache_krope[:, l]), rope=rope)
    y_sample = rmsnorm(xs, g_final)
    return (y_prompt, y_sample, state_ckv, state_krope)
```

```python
import functools

import jax
import jax.numpy as jnp
from jax import lax
from jax.experimental import pallas as pl
from jax.experimental.pallas import tpu as pltpu

D_MODEL = 1024
DEPTH = 2
GRID_W = 64
MLA_HEADS = 4
NOPE_DIM = 128
ROPE_DIM = 64
V_DIM = 128
QK_DIM = NOPE_DIM + ROPE_DIM
MLA_WIDTH = MLA_HEADS * V_DIM
Q_LORA = 384
KV_LORA = 256
POOL_GROUPS = 4
POOL_GROUP_DIM = 64
POOL_WIDTH = POOL_GROUPS * POOL_GROUP_DIM
CONV_WIDTH = 256
MIX_WIDTH = MLA_WIDTH + POOL_WIDTH + CONV_WIDTH
ROPE_BASE = 10000.0
AXIS_DIM = ROPE_DIM // 2
ATTN_SCALE = QK_DIM ** -0.5
EPS = 1e-6

_SPLITS = (Q_LORA, KV_LORA, ROPE_DIM, MLA_WIDTH, POOL_WIDTH, POOL_WIDTH,
           CONV_WIDTH, CONV_WIDTH, CONV_WIDTH, CONV_WIDTH)
_OFFS = [sum(_SPLITS[:i]) for i in range(len(_SPLITS) + 1)]

V7X_LANES = 128
V7X_MXU_DEPTH = 256
HEAD_PAD = V7X_MXU_DEPTH
ROPE_PAD = HEAD_PAD - NOPE_DIM

_C_A = 0
_C_B = _C_A + Q_LORA + KV_LORA
_C_C = _C_B + 2 * ROPE_PAD
_C_D = _C_C + MLA_WIDTH
_C_E = _C_D + 2 * POOL_WIDTH
_C_END = _C_E + 4 * CONV_WIDTH

ROWS_PER_STEP = 1024
Q_TILE = 512
VMEM_LIMIT_BYTES = 63 * 1024 * 1024

_BF = jnp.bfloat16
_F32 = jnp.float32


def _dot(a, b):
    return jnp.dot(a, b, preferred_element_type=_F32)


def _rms(x, g):
    return x * lax.rsqrt(jnp.mean(x * x, axis=-1, keepdims=True) + EPS) * g


def _silu(x):
    return x * jax.nn.sigmoid(x)


def _mod_kernel(c_ref, w_ref, b_ref, o_ref):
    s = _silu(c_ref[...])
    o_ref[...] = _dot(s.astype(_BF), w_ref[...].astype(_BF)) + b_ref[...]


def _modulation(c_all, w_mod, b_mod):
    tn = 1024
    return pl.pallas_call(
        _mod_kernel,
        grid=(DEPTH, 3 * D_MODEL // tn),
        in_specs=[
            pl.BlockSpec((8, D_MODEL), lambda l, j: (0, 0)),
            pl.BlockSpec((None, D_MODEL, tn), lambda l, j: (l, 0, j)),
            pl.BlockSpec((None, 1, tn), lambda l, j: (l, 0, j)),
        ],
        out_specs=pl.BlockSpec((None, 8, tn), lambda l, j: (l, 0, j)),
        out_shape=jax.ShapeDtypeStruct((DEPTH, 8, 3 * D_MODEL), _F32),
        name="modulation",
    )(c_all, w_mod, b_mod.reshape(DEPTH, 1, 3 * D_MODEL))


def _pool_mix(px, seq_len):
    m = px.shape[0]
    t = lax.broadcasted_iota(jnp.int32, (m, 1), 0) & (seq_len - 1)
    lane = lax.broadcasted_iota(jnp.int32, (1, POOL_WIDTH), 1)
    half = jnp.left_shift(1, lane >> 6)
    acc = px
    max_half = 1 << (POOL_GROUPS - 1)
    for d in range(-max_half, max_half):
        if d == 0:
            continue
        shifted = pltpu.roll(px, (-d) % m, axis=0)
        row_ok = (t + d >= 0) & (t + d < seq_len)
        lane_ok = (d >= -half) & (d < half)
        acc = acc + jnp.where(row_ok & lane_ok, shifted, 0.0)
    cnt = jnp.minimum(t + half, seq_len) - jnp.maximum(t - half, 0)
    return acc / cnt.astype(_F32) - px


def _short_conv(z, w, seq_len):
    m = z.shape[0]
    t = lax.broadcasted_iota(jnp.int32, (m, 1), 0) & (seq_len - 1)
    prev = jnp.where(t >= 1, pltpu.roll(z, 1, axis=0), 0.0)
    nxt = jnp.where(t <= seq_len - 2, pltpu.roll(z, m - 1, axis=0), 0.0)
    return prev * w[0:1, :] + z * w[1:2, :] + nxt * w[2:3, :]


def _layer_kernel(*refs, n_seq, seq_len, cache_len, has_rope, final_norm, emit_state):
    it = iter(refs)
    x_ref, mod_ref = next(it), next(it)
    if cache_len:
        cckv_ref, ckr_ref = next(it), next(it)
    if has_rope:
        cos_ref, sin_ref = next(it), next(it)
    gn_ref, win_ref, gq_ref, wuq_ref = next(it), next(it), next(it), next(it)
    if has_rope:
        wuqs_ref = next(it)
    gkv_ref, wukv_ref, poolw_ref, pools_ref, convw_ref, wout_ref, gfin_ref = (
        next(it), next(it), next(it), next(it), next(it), next(it), next(it))
    y_ref = next(it)
    if emit_state:
        ckv_out_ref, kr_out_ref = next(it), next(it)
    h_s, q_s, k_s, v_s, mixed_s = next(it), next(it), next(it), next(it), next(it)

    keys = cache_len + seq_len
    pair = 2 * (NOPE_DIM + V_DIM)

    shift, scale, gate = mod_ref[0:1, :], mod_ref[1:2, :], mod_ref[2:3, :]
    h_s[...] = (_rms(x_ref[...], gn_ref[...]) * (1.0 + scale) + shift).astype(_BF)

    a = _dot(h_s[...], win_ref[:, _C_A:_C_B])
    cq = _rms(a[:, :Q_LORA], gq_ref[...]).astype(_BF)
    ckv = _rms(a[:, Q_LORA:], gkv_ref[...])
    if emit_state:
        ckv_out_ref[...] = ckv.reshape(n_seq, seq_len, KV_LORA)
    ckv_bf = ckv.astype(_BF)
    b = _dot(h_s[...], win_ref[:, _C_B:_C_C])
    kr = b[:, :ROPE_PAD]
    if has_rope:
        kr = kr * cos_ref[...] + b[:, ROPE_PAD:] * sin_ref[...]
    if emit_state:
        kr_out_ref[...] = kr[:, :ROPE_DIM].reshape(n_seq, seq_len, ROPE_DIM)
    kr_bf = kr.astype(_BF)
    if cache_len:
        cckv_bf = cckv_ref[...].astype(_BF)
        ckr = ckr_ref[...]
        ckr_bf = jnp.concatenate([ckr, jnp.zeros_like(ckr)], axis=-1).astype(_BF)

    for hp in range(MLA_HEADS // 2):
        q2 = _dot(cq, wuq_ref[:, hp * 2 * HEAD_PAD:(hp + 1) * 2 * HEAD_PAD]) * ATTN_SCALE
        if has_rope:
            qs2 = _dot(cq, wuqs_ref[:, hp * 2 * ROPE_PAD:(hp + 1) * 2 * ROPE_PAD]) * ATTN_SCALE
        kv2 = _dot(ckv_bf, wukv_ref[:, hp * pair:(hp + 1) * pair])
        if cache_len:
            kvc2 = _dot(cckv_bf, wukv_ref[:, hp * pair:(hp + 1) * pair])
        for j in range(2):
            hd = 2 * hp + j
            c0 = j * HEAD_PAD
            q_s[hd, :, :NOPE_DIM] = q2[:, c0:c0 + NOPE_DIM].astype(_BF)
            qr = q2[:, c0 + NOPE_DIM:c0 + HEAD_PAD]
            if has_rope:
                qr = qr * cos_ref[...] + qs2[:, j * ROPE_PAD:(j + 1) * ROPE_PAD] * sin_ref[...]
            q_s[hd, :, NOPE_DIM:] = qr.astype(_BF)
            k0 = j * (NOPE_DIM + V_DIM)
            for s in range(n_seq):
                r0 = s * keys
                if cache_len:
                    k_s[hd, r0:r0 + cache_len, :NOPE_DIM] = kvc2[:, k0:k0 + NOPE_DIM].astype(_BF)
                    k_s[hd, r0:r0 + cache_len, NOPE_DIM:] = ckr_bf
                    v_s[hd, r0:r0 + cache_len, :] = kvc2[:, k0 + NOPE_DIM:k0 + NOPE_DIM + V_DIM].astype(_BF)
                rn = slice(r0 + cache_len, r0 + keys)
                rs = slice(s * seq_len, (s + 1) * seq_len)
                k_s[hd, rn, :NOPE_DIM] = kv2[rs, k0:k0 + NOPE_DIM].astype(_BF)
                k_s[hd, rn, NOPE_DIM:] = kr_bf[rs, :]
                v_s[hd, rn, :] = kv2[rs, k0 + NOPE_DIM:k0 + NOPE_DIM + V_DIM].astype(_BF)

    tq = min(Q_TILE, seq_len)
    for hp in range(MLA_HEADS // 2):
        g2 = _silu(_dot(h_s[...], win_ref[:, _C_C + hp * 2 * V_DIM:_C_C + (hp + 1) * 2 * V_DIM]))
        for j in range(2):
            hd = 2 * hp + j
            for s in range(n_seq):
                kh = k_s[hd, s * keys:(s + 1) * keys, :]
                vh = v_s[hd, s * keys:(s + 1) * keys, :]
                for qb in range(seq_len // tq):
                    rows = slice(s * seq_len + qb * tq, s * seq_len + (qb + 1) * tq)
                    sc = lax.dot_general(q_s[hd, rows, :], kh, (((1,), (1,)), ((), ())),
                                         preferred_element_type=_F32)
                    p = jnp.exp(sc - jnp.max(sc, axis=-1, keepdims=True))
                    o = _dot(p.astype(_BF), vh) / jnp.sum(p, axis=-1, keepdims=True)
                    mixed_s[rows, hd * V_DIM:(hd + 1) * V_DIM] = (
                        g2[rows, j * V_DIM:(j + 1) * V_DIM] * o).astype(_BF)

    dgrp = _dot(h_s[...], win_ref[:, _C_D:_C_E])
    pooled = _pool_mix(dgrp[:, :POOL_WIDTH], seq_len)
    pool = _dot(pooled.astype(_BF), poolw_ref[...]) * pools_ref[...]
    mixed_s[:, MLA_WIDTH:MLA_WIDTH + POOL_WIDTH] = (_silu(dgrp[:, POOL_WIDTH:]) * pool).astype(_BF)

    e1 = _dot(h_s[...], win_ref[:, _C_E:_C_E + 2 * CONV_WIDTH])
    conv = _short_conv(e1[:, :CONV_WIDTH] * e1[:, CONV_WIDTH:], convw_ref[...], seq_len)
    e2 = _dot(h_s[...], win_ref[:, _C_E + 2 * CONV_WIDTH:_C_END])
    mixed_s[:, MLA_WIDTH + POOL_WIDTH:] = (_silu(e2[:, CONV_WIDTH:]) * (e2[:, :CONV_WIDTH] * conv)).astype(_BF)

    y = x_ref[...] + gate * _dot(mixed_s[...], wout_ref[...])
    if final_norm:
        y = _rms(y, gfin_ref[...])
    y_ref[...] = y


def _mixer_layer(x2d, mod, wts, g_final, *, seq_len, cache=None, rope=None, final_norm, emit_state):
    rows = x2d.shape[0]
    m = ROWS_PER_STEP
    n_seq = m // seq_len
    n_steps = rows // m
    cache_len = 0 if cache is None else cache[0].shape[2]
    keys = cache_len + seq_len
    has_rope = rope is not None

    def const(shape):
        return pl.BlockSpec(shape, lambda i: (0,) * len(shape), pipeline_mode=pl.Buffered(1))

    mod_steps = mod.shape[0]
    args = [x2d, mod]
    in_specs = [
        pl.BlockSpec((m, D_MODEL), lambda i: (i, 0)),
        pl.BlockSpec((None, 3, D_MODEL), (lambda i: (i, 0, 0)) if mod_steps > 1 else (lambda i: (0, 0, 0))),
    ]
    if cache_len:
        layer = cache[2]
        args += [cache[0], cache[1]]
        in_specs += [pl.BlockSpec((None, None, cache_len, KV_LORA), lambda i: (i, layer, 0, 0)),
                     pl.BlockSpec((None, None, cache_len, ROPE_DIM), lambda i: (i, layer, 0, 0))]
    if has_rope:
        args += [rope[0], rope[1]]
        in_specs += [const((seq_len, ROPE_PAD)), const((seq_len, ROPE_PAD))]
    args += [wts["g_norm"], wts["w_in"], wts["g_q"], wts["w_uq"]]
    in_specs += [const((1, D_MODEL)), const((D_MODEL, _C_END)), const((1, Q_LORA)),
                 const((Q_LORA, MLA_HEADS * HEAD_PAD))]
    if has_rope:
        args += [wts["w_uq_rot"]]
        in_specs += [const((Q_LORA, MLA_HEADS * ROPE_PAD))]
    args += [wts["g_kv"], wts["w_ukv"], wts["pool_w"], wts["pool_s"], wts["conv_w"], wts["w_out"], g_final]
    in_specs += [const((1, KV_LORA)), const((KV_LORA, MLA_HEADS * (NOPE_DIM + V_DIM))),
                 const((POOL_WIDTH, POOL_WIDTH)), const((1, POOL_WIDTH)), const((3, CONV_WIDTH)),
                 const((MIX_WIDTH, D_MODEL)), const((1, D_MODEL))]

    out_shape = [jax.ShapeDtypeStruct((rows, D_MODEL), _F32)]
    out_specs = [pl.BlockSpec((m, D_MODEL), lambda i: (i, 0))]
    if emit_state:
        n_all = rows // seq_len
        out_shape += [jax.ShapeDtypeStruct((n_all, seq_len, KV_LORA), _F32),
                      jax.ShapeDtypeStruct((n_all, seq_len, ROPE_DIM), _F32)]
        out_specs += [pl.BlockSpec((n_seq, seq_len, KV_LORA), lambda i: (i, 0, 0)),
                      pl.BlockSpec((n_seq, seq_len, ROPE_DIM), lambda i: (i, 0, 0))]

    kern = functools.partial(_layer_kernel, n_seq=n_seq, seq_len=seq_len, cache_len=cache_len,
                             has_rope=has_rope, final_norm=final_norm, emit_state=emit_state)
    return pl.pallas_call(
        kern,
        grid=(n_steps,),
        in_specs=in_specs,
        out_specs=out_specs,
        out_shape=out_shape,
        scratch_shapes=[
            pltpu.VMEM((m, D_MODEL), _BF),
            pltpu.VMEM((MLA_HEADS, m, HEAD_PAD), _BF),
            pltpu.VMEM((MLA_HEADS, n_seq * keys, HEAD_PAD), _BF),
            pltpu.VMEM((MLA_HEADS, n_seq * keys, V_DIM), _BF),
            pltpu.VMEM((m, MIX_WIDTH), _BF),
        ],
        compiler_params=pltpu.CompilerParams(dimension_semantics=("arbitrary",),
                                             vmem_limit_bytes=VMEM_LIMIT_BYTES),
        name="mixer_layer_latent" if has_rope else "mixer_layer_context",
    )(*args)


def _prep_layer_weights(l, g_norm, w_in, g_q, w_uq, g_kv, w_ukv, pool_w, pool_s, conv_w, w_out):
    wi = w_in[l]
    col = lambda i: wi[:, _OFFS[i]:_OFFS[i + 1]]
    w_kr = col(2)
    zpad = jnp.zeros((D_MODEL, ROPE_PAD - ROPE_DIM), _F32)
    w_kr_rot = jnp.concatenate([-w_kr[:, AXIS_DIM:], w_kr[:, :AXIS_DIM]], axis=1)
    w_in_r = jnp.concatenate(
        [col(0), col(1), w_kr, zpad, w_kr_rot, zpad, col(3), col(4), col(5), col(7), col(8), col(6), col(9)],
        axis=1).astype(_BF)

    wq = w_uq[l].reshape(Q_LORA, MLA_HEADS, QK_DIM)
    zq = jnp.zeros((Q_LORA, MLA_HEADS, HEAD_PAD - QK_DIM), _F32)
    w_uq_r = jnp.concatenate([wq, zq], axis=2).reshape(Q_LORA, MLA_HEADS * HEAD_PAD).astype(_BF)
    wq_rope = wq[:, :, NOPE_DIM:]
    w_uq_rot = jnp.concatenate([-wq_rope[:, :, AXIS_DIM:], wq_rope[:, :, :AXIS_DIM], zq], axis=2)
    w_uq_rot = w_uq_rot.reshape(Q_LORA, MLA_HEADS * ROPE_PAD).astype(_BF)

    pw = jnp.zeros((POOL_WIDTH, POOL_WIDTH), _F32)
    for g in range(POOL_GROUPS):
        sl = slice(g * POOL_GROUP_DIM, (g + 1) * POOL_GROUP_DIM)
        pw = pw.at[sl, sl].set(pool_w[l, g])

    return {
        "g_norm": g_norm[l].reshape(1, D_MODEL),
        "w_in": w_in_r,
        "g_q": g_q[l].reshape(1, Q_LORA),
        "w_uq": w_uq_r,
        "w_uq_rot": w_uq_rot,
        "g_kv": g_kv[l].reshape(1, KV_LORA),
        "w_ukv": w_ukv[l].astype(_BF),
        "pool_w": pw.astype(_BF),
        "pool_s": pool_s[l].reshape(1, POOL_WIDTH),
        "conv_w": conv_w[l],
        "w_out": w_out[l].astype(_BF),
    }


def _rope_tables(seq_len):
    rows = seq_len // GRID_W
    row = jnp.repeat(jnp.arange(rows), GRID_W).astype(_F32)
    col = jnp.tile(jnp.arange(GRID_W), rows).astype(_F32)
    inv = 1.0 / (ROPE_BASE ** (jnp.arange(0, AXIS_DIM, 2, dtype=_F32) / AXIS_DIM))
    ang = jnp.concatenate([row[:, None] * inv, col[:, None] * inv], axis=-1)
    z = jnp.zeros((seq_len, ROPE_PAD - ROPE_DIM), _F32)
    cos, sin = jnp.cos(ang), jnp.sin(ang)
    return jnp.concatenate([cos, cos, z], axis=-1), jnp.concatenate([sin, sin, z], axis=-1)


def kernel(x_prompt, x_sample, cache_ckv, cache_krope, c, c_ctx, w_mod, b_mod, g_norm, w_in, g_q, w_uq,
           g_kv, w_ukv, pool_w, pool_s, conv_w, w_out, g_final):
    batch, seq, _ = x_prompt.shape
    dec_batch, dec_seq, _ = x_sample.shape

    c_all = jnp.concatenate([c_ctx[None], c, jnp.zeros((8 - 1 - dec_batch, D_MODEL), _F32)], axis=0)
    mod_all = _modulation(c_all, w_mod, b_mod)
    wts = [_prep_layer_weights(l, g_norm, w_in, g_q, w_uq, g_kv, w_ukv, pool_w, pool_s, conv_w, w_out)
           for l in range(DEPTH)]
    gfin = g_final.reshape(1, D_MODEL)

    xp = x_prompt.reshape(batch * seq, D_MODEL)
    ckv_states, kr_states = [], []
    for l in range(DEPTH):
        mod_ctx = mod_all[l, 0].reshape(1, 3, D_MODEL)
        xp, ckv, kr = _mixer_layer(xp, mod_ctx, wts[l], gfin, seq_len=seq,
                                   final_norm=(l == DEPTH - 1), emit_state=True)
        ckv_states.append(ckv)
        kr_states.append(kr)
    y_prompt = xp.reshape(batch, seq, D_MODEL)
    state_ckv = jnp.stack(ckv_states, axis=1)
    state_krope = jnp.stack(kr_states, axis=1)

    rope = _rope_tables(dec_seq)
    xs = x_sample.reshape(dec_batch * dec_seq, D_MODEL)
    for l in range(DEPTH):
        mod_lat = mod_all[l, 1:1 + dec_batch].reshape(dec_batch, 3, D_MODEL)
        (xs,) = _mixer_layer(xs, mod_lat, wts[l], gfin, seq_len=dec_seq,
                             cache=(cache_ckv, cache_krope, l), rope=rope,
                             final_norm=(l == DEPTH - 1), emit_state=False)
    y_sample = xs.reshape(dec_batch, dec_seq, D_MODEL)
    return (y_prompt, y_sample, state_ckv, state_krope)
```

```python
import functools

import numpy as np
import jax
import jax.numpy as jnp
from jax import lax
from jax.experimental import pallas as pl
from jax.experimental.pallas import tpu as pltpu

D_MODEL = 1024
DEPTH = 2
GRID_W = 64
MLA_HEADS = 4
NOPE_DIM = 128
ROPE_DIM = 64
V_DIM = 128
QK_DIM = NOPE_DIM + ROPE_DIM
MLA_WIDTH = MLA_HEADS * V_DIM
Q_LORA = 384
KV_LORA = 256
POOL_GROUPS = 4
POOL_GROUP_DIM = 64
POOL_WIDTH = POOL_GROUPS * POOL_GROUP_DIM
CONV_WIDTH = 256
MIX_WIDTH = MLA_WIDTH + POOL_WIDTH + CONV_WIDTH
ROPE_BASE = 10000.0
AXIS_DIM = ROPE_DIM // 2
ATTN_SCALE = QK_DIM ** -0.5
EPS = 1e-6

_SPLITS = (Q_LORA, KV_LORA, ROPE_DIM, MLA_WIDTH, POOL_WIDTH, POOL_WIDTH,
           CONV_WIDTH, CONV_WIDTH, CONV_WIDTH, CONV_WIDTH)
_OFFS = [sum(_SPLITS[:i]) for i in range(len(_SPLITS) + 1)]
IN_WIDTH = _OFFS[-1]

V7X_MXU_DEPTH = 256
HEAD_PAD = V7X_MXU_DEPTH
ROPE_PAD = HEAD_PAD - NOPE_DIM
KV_HEAD = NOPE_DIM + V_DIM

_C_A = 0
_C_B = _C_A + Q_LORA + KV_LORA
_C_C = _C_B + 2 * ROPE_PAD
_C_D = _C_C + MLA_WIDTH
_C_E = _C_D + 2 * POOL_WIDTH
_C_END = _C_E + 4 * CONV_WIDTH

MOD_ROWS = 8
ROWS_PER_STEP = 1024
Q_TILE = 512
PREP_ROWS = 256
VMEM_LIMIT_BYTES = 63 * 1024 * 1024

_BF = jnp.bfloat16
_F32 = jnp.float32


def _dot(a, b):
    return jnp.dot(a, b, preferred_element_type=_F32)


def _rms(x, g):
    return x * lax.rsqrt(jnp.mean(x * x, axis=-1, keepdims=True) + EPS) * g


def _silu(x):
    return x * jax.nn.sigmoid(x)


def _mod_kernel(c_ref, w_ref, b_ref, o_ref):
    s = _silu(c_ref[...])
    o_ref[...] = _dot(s.astype(_BF), w_ref[...].astype(_BF)) + b_ref[...]


def _modulation(c_all, w_mod, b_mod):
    return pl.pallas_call(
        _mod_kernel,
        grid=(DEPTH, 3),
        in_specs=[
            pl.BlockSpec((MOD_ROWS, D_MODEL), lambda l, j: (0, 0)),
            pl.BlockSpec((None, D_MODEL, D_MODEL), lambda l, j: (l, 0, j)),
            pl.BlockSpec((None, 1, D_MODEL), lambda l, j: (l, 0, j)),
        ],
        out_specs=pl.BlockSpec((None, None, MOD_ROWS, D_MODEL), lambda l, j: (l, j, 0, 0)),
        out_shape=jax.ShapeDtypeStruct((DEPTH, 3, MOD_ROWS, D_MODEL), _F32),
        name="modulation",
    )(c_all, w_mod, b_mod.reshape(DEPTH, 1, 3 * D_MODEL))


def _prep_kernel(win_ref, wout_ref, wuq_ref, wukv_ref, poolw_ref,
                 win_o, wout_o, wuq_o, wuqr_o, wukv_o, poolw_o):
    def put(dst, src, width, neg=False):
        v = win_ref[:, src:src + width]
        win_o[:, dst:dst + width] = (-v if neg else v).astype(_BF)

    rows = win_ref.shape[0]
    zpad = jnp.zeros((rows, ROPE_PAD - ROPE_DIM), _BF)
    kr0 = _OFFS[2]
    put(_C_A, _OFFS[0], Q_LORA + KV_LORA)
    put(_C_B, kr0, ROPE_DIM)
    win_o[:, _C_B + ROPE_DIM:_C_B + ROPE_PAD] = zpad
    put(_C_B + ROPE_PAD, kr0 + AXIS_DIM, AXIS_DIM, neg=True)
    put(_C_B + ROPE_PAD + AXIS_DIM, kr0, AXIS_DIM)
    win_o[:, _C_B + ROPE_PAD + ROPE_DIM:_C_C] = zpad
    put(_C_C, _OFFS[3], MLA_WIDTH + 2 * POOL_WIDTH)
    put(_C_E, _OFFS[7], 2 * CONV_WIDTH)
    put(_C_E + 2 * CONV_WIDTH, _OFFS[6], CONV_WIDTH)
    put(_C_E + 3 * CONV_WIDTH, _OFFS[9], CONV_WIDTH)
    wout_o[...] = wout_ref[...].astype(_BF)

    @pl.when(pl.program_id(1) == 0)
    def _():
        zq = jnp.zeros((Q_LORA, HEAD_PAD - QK_DIM), _BF)
        for hd in range(MLA_HEADS):
            s0 = hd * QK_DIM
            wuq_o[:, hd * HEAD_PAD:hd * HEAD_PAD + QK_DIM] = wuq_ref[:, s0:s0 + QK_DIM].astype(_BF)
            wuq_o[:, hd * HEAD_PAD + QK_DIM:(hd + 1) * HEAD_PAD] = zq
            r0 = s0 + NOPE_DIM
            d0 = hd * ROPE_PAD
            wuqr_o[:, d0:d0 + AXIS_DIM] = (-wuq_ref[:, r0 + AXIS_DIM:r0 + ROPE_DIM]).astype(_BF)
            wuqr_o[:, d0 + AXIS_DIM:d0 + ROPE_DIM] = wuq_ref[:, r0:r0 + AXIS_DIM].astype(_BF)
            wuqr_o[:, d0 + ROPE_DIM:d0 + ROPE_PAD] = zq
        wukv_o[...] = wukv_ref[...].astype(_BF)
        poolw_o[...] = jnp.zeros((POOL_WIDTH, POOL_WIDTH), _BF)
        for g in range(POOL_GROUPS):
            sl = slice(g * POOL_GROUP_DIM, (g + 1) * POOL_GROUP_DIM)
            poolw_o[sl, sl] = poolw_ref[g].astype(_BF)


def _prep_weights(w_in, w_out, w_uq, w_ukv, pool_w):
    per_layer = lambda *shape: pl.BlockSpec((None,) + shape, lambda l, j: (l,) + (0,) * len(shape))
    return pl.pallas_call(
        _prep_kernel,
        grid=(DEPTH, D_MODEL // PREP_ROWS),
        in_specs=[
            pl.BlockSpec((None, PREP_ROWS, IN_WIDTH), lambda l, j: (l, j, 0)),
            pl.BlockSpec((None, PREP_ROWS, D_MODEL), lambda l, j: (l, j, 0)),
            per_layer(Q_LORA, MLA_HEADS * QK_DIM),
            per_layer(KV_LORA, MLA_HEADS * KV_HEAD),
            per_layer(POOL_GROUPS, POOL_GROUP_DIM, POOL_GROUP_DIM),
        ],
        out_specs=[
            pl.BlockSpec((None, PREP_ROWS, _C_END), lambda l, j: (l, j, 0)),
            pl.BlockSpec((None, PREP_ROWS, D_MODEL), lambda l, j: (l, j, 0)),
            per_layer(Q_LORA, MLA_HEADS * HEAD_PAD),
            per_layer(Q_LORA, MLA_HEADS * ROPE_PAD),
            per_layer(KV_LORA, MLA_HEADS * KV_HEAD),
            per_layer(POOL_WIDTH, POOL_WIDTH),
        ],
        out_shape=[
            jax.ShapeDtypeStruct((DEPTH, D_MODEL, _C_END), _BF),
            jax.ShapeDtypeStruct((DEPTH, MIX_WIDTH, D_MODEL), _BF),
            jax.ShapeDtypeStruct((DEPTH, Q_LORA, MLA_HEADS * HEAD_PAD), _BF),
            jax.ShapeDtypeStruct((DEPTH, Q_LORA, MLA_HEADS * ROPE_PAD), _BF),
            jax.ShapeDtypeStruct((DEPTH, KV_LORA, MLA_HEADS * KV_HEAD), _BF),
            jax.ShapeDtypeStruct((DEPTH, POOL_WIDTH, POOL_WIDTH), _BF),
        ],
        compiler_params=pltpu.CompilerParams(dimension_semantics=("arbitrary", "arbitrary")),
        name="weight_prep",
    )(w_in, w_out, w_uq, w_ukv, pool_w)


def _pool_mix(px, seq_len):
    m = px.shape[0]
    t = lax.broadcasted_iota(jnp.int32, (m, 1), 0) & (seq_len - 1)
    lane = lax.broadcasted_iota(jnp.int32, (1, POOL_WIDTH), 1)
    half = jnp.left_shift(1, lane >> 6)
    acc = px
    max_half = 1 << (POOL_GROUPS - 1)
    for d in range(-max_half, max_half):
        if d == 0:
            continue
        shifted = pltpu.roll(px, (-d) % m, axis=0)
        row_ok = (t + d >= 0) & (t + d < seq_len)
        lane_ok = (d >= -half) & (d < half)
        acc = acc + jnp.where(row_ok & lane_ok, shifted, 0.0)
    cnt = jnp.minimum(t + half, seq_len) - jnp.maximum(t - half, 0)
    return acc / cnt.astype(_F32) - px


def _short_conv(z, w, seq_len):
    m = z.shape[0]
    t = lax.broadcasted_iota(jnp.int32, (m, 1), 0) & (seq_len - 1)
    prev = jnp.where(t >= 1, pltpu.roll(z, 1, axis=0), 0.0)
    nxt = jnp.where(t <= seq_len - 2, pltpu.roll(z, m - 1, axis=0), 0.0)
    return prev * w[0:1, :] + z * w[1:2, :] + nxt * w[2:3, :]


def _layer_kernel(*refs, n_seq, seq_len, cache_len, has_rope, final_norm, state_mode):
    it = iter(refs)
    x_ref, mod_ref = next(it), next(it)
    if cache_len:
        cckv_ref, ckr_ref = next(it), next(it)
    if has_rope:
        cos_ref, sin_ref = next(it), next(it)
    if state_mode == "stack":
        ckv_prev_ref, kr_prev_ref = next(it), next(it)
    gn_ref, win_ref, gq_ref, wuq_ref = next(it), next(it), next(it), next(it)
    if has_rope:
        wuqs_ref = next(it)
    gkv_ref, wukv_ref, poolw_ref, pools_ref, convw_ref, wout_ref, gfin_ref = (
        next(it), next(it), next(it), next(it), next(it), next(it), next(it))
    y_ref = next(it)
    if state_mode:
        ckv_out_ref, kr_out_ref = next(it), next(it)
    h_s, q_s, k_s, v_s, mixed_s = next(it), next(it), next(it), next(it), next(it)

    keys = cache_len + seq_len
    pair = 2 * KV_HEAD

    mrow = (1 + pl.program_id(0)) if has_rope else 0
    shift, scale, gate = (mod_ref[k, pl.ds(mrow, 1), :] for k in range(3))
    h_s[...] = (_rms(x_ref[...], gn_ref[...]) * (1.0 + scale) + shift).astype(_BF)

    a = _dot(h_s[...], win_ref[:, _C_A:_C_B])
    cq = _rms(a[:, :Q_LORA], gq_ref[...]).astype(_BF)
    ckv = _rms(a[:, Q_LORA:], gkv_ref[...])
    b = _dot(h_s[...], win_ref[:, _C_B:_C_C])
    kr = b[:, :ROPE_PAD]
    if has_rope:
        kr = kr * cos_ref[...] + b[:, ROPE_PAD:] * sin_ref[...]
    if state_mode == "own":
        ckv_out_ref[...] = ckv.reshape(n_seq, seq_len, KV_LORA)
        kr_out_ref[...] = kr[:, :ROPE_DIM].reshape(n_seq, seq_len, ROPE_DIM)
    elif state_mode == "stack":
        ckv_out_ref[:, 0] = ckv_prev_ref[...]
        kr_out_ref[:, 0] = kr_prev_ref[...]
        ckv_out_ref[:, 1] = ckv.reshape(n_seq, seq_len, KV_LORA)
        kr_out_ref[:, 1] = kr[:, :ROPE_DIM].reshape(n_seq, seq_len, ROPE_DIM)
    ckv_bf = ckv.astype(_BF)
    kr_bf = kr.astype(_BF)
    if cache_len:
        cckv_bf = cckv_ref[...].astype(_BF)
        ckr = ckr_ref[...]
        ckr_bf = jnp.concatenate([ckr, jnp.zeros_like(ckr)], axis=-1).astype(_BF)

    for hp in range(MLA_HEADS // 2):
        q2 = _dot(cq, wuq_ref[:, hp * 2 * HEAD_PAD:(hp + 1) * 2 * HEAD_PAD]) * ATTN_SCALE
        if has_rope:
            qs2 = _dot(cq, wuqs_ref[:, hp * 2 * ROPE_PAD:(hp + 1) * 2 * ROPE_PAD]) * ATTN_SCALE
        kv2 = _dot(ckv_bf, wukv_ref[:, hp * pair:(hp + 1) * pair])
        if cache_len:
            kvc2 = _dot(cckv_bf, wukv_ref[:, hp * pair:(hp + 1) * pair])
        for j in range(2):
            hd = 2 * hp + j
            c0 = j * HEAD_PAD
            q_s[hd, :, :NOPE_DIM] = q2[:, c0:c0 + NOPE_DIM].astype(_BF)
            qr = q2[:, c0 + NOPE_DIM:c0 + HEAD_PAD]
            if has_rope:
                qr = qr * cos_ref[...] + qs2[:, j * ROPE_PAD:(j + 1) * ROPE_PAD] * sin_ref[...]
            q_s[hd, :, NOPE_DIM:] = qr.astype(_BF)
            k0 = j * KV_HEAD
            for s in range(n_seq):
                r0 = s * keys
                if cache_len:
                    k_s[hd, r0:r0 + cache_len, :NOPE_DIM] = kvc2[:, k0:k0 + NOPE_DIM].astype(_BF)
                    k_s[hd, r0:r0 + cache_len, NOPE_DIM:] = ckr_bf
                    v_s[hd, r0:r0 + cache_len, :] = kvc2[:, k0 + NOPE_DIM:k0 + KV_HEAD].astype(_BF)
                rn = slice(r0 + cache_len, r0 + keys)
                rs = slice(s * seq_len, (s + 1) * seq_len)
                k_s[hd, rn, :NOPE_DIM] = kv2[rs, k0:k0 + NOPE_DIM].astype(_BF)
                k_s[hd, rn, NOPE_DIM:] = kr_bf[rs, :]
                v_s[hd, rn, :] = kv2[rs, k0 + NOPE_DIM:k0 + KV_HEAD].astype(_BF)

    tq = min(Q_TILE, seq_len)
    for hp in range(MLA_HEADS // 2):
        g2 = _silu(_dot(h_s[...], win_ref[:, _C_C + hp * 2 * V_DIM:_C_C + (hp + 1) * 2 * V_DIM]))
        for j in range(2):
            hd = 2 * hp + j
            for s in range(n_seq):
                kh = k_s[hd, s * keys:(s + 1) * keys, :]
                vh = v_s[hd, s * keys:(s + 1) * keys, :]
                for qb in range(seq_len // tq):
                    rows = slice(s * seq_len + qb * tq, s * seq_len + (qb + 1) * tq)
                    sc = lax.dot_general(q_s[hd, rows, :], kh, (((1,), (1,)), ((), ())),
                                         preferred_element_type=_F32)
                    p = jnp.exp(sc - jnp.max(sc, axis=-1, keepdims=True))
                    o = _dot(p.astype(_BF), vh) / jnp.sum(p, axis=-1, keepdims=True)
                    mixed_s[rows, hd * V_DIM:(hd + 1) * V_DIM] = (
                        g2[rows, j * V_DIM:(j + 1) * V_DIM] * o).astype(_BF)

    dgrp = _dot(h_s[...], win_ref[:, _C_D:_C_E])
    pooled = _pool_mix(dgrp[:, :POOL_WIDTH], seq_len)
    pool = _dot(pooled.astype(_BF), poolw_ref[...]) * pools_ref[...]
    mixed_s[:, MLA_WIDTH:MLA_WIDTH + POOL_WIDTH] = (_silu(dgrp[:, POOL_WIDTH:]) * pool).astype(_BF)

    e1 = _dot(h_s[...], win_ref[:, _C_E:_C_E + 2 * CONV_WIDTH])
    conv = _short_conv(e1[:, :CONV_WIDTH] * e1[:, CONV_WIDTH:], convw_ref[...], seq_len)
    e2 = _dot(h_s[...], win_ref[:, _C_E + 2 * CONV_WIDTH:_C_END])
    mixed_s[:, MLA_WIDTH + POOL_WIDTH:] = (_silu(e2[:, CONV_WIDTH:]) * (e2[:, :CONV_WIDTH] * conv)).astype(_BF)

    y = x_ref[...] + gate * _dot(mixed_s[...], wout_ref[...])
    if final_norm:
        y = _rms(y, gfin_ref[...])
    y_ref[...] = y


def _mixer_layer(x2d, mod_all, layer, wts, *, seq_len, cache=None, rope=None, prev_state=None,
                 final_norm, state_mode):
    rows = x2d.shape[0]
    m = ROWS_PER_STEP
    n_seq = m // seq_len
    n_steps = rows // m
    n_all = rows // seq_len
    cache_len = 0 if cache is None else cache[0].shape[2]
    keys = cache_len + seq_len
    has_rope = rope is not None

    def const(*shape):
        return pl.BlockSpec(shape, lambda i: (0,) * len(shape), pipeline_mode=pl.Buffered(1))

    def of_layer(*shape):
        return pl.BlockSpec((None,) + shape, lambda i: (layer,) + (0,) * len(shape),
                            pipeline_mode=pl.Buffered(1))

    args = [x2d, mod_all]
    in_specs = [pl.BlockSpec((m, D_MODEL), lambda i: (i, 0)), of_layer(3, MOD_ROWS, D_MODEL)]
    if cache_len:
        args += [cache[0], cache[1]]
        in_specs += [pl.BlockSpec((None, None, cache_len, KV_LORA), lambda i: (i, layer, 0, 0)),
                     pl.BlockSpec((None, None, cache_len, ROPE_DIM), lambda i: (i, layer, 0, 0))]
    if has_rope:
        args += [rope[0], rope[1]]
        in_specs += [const(seq_len, ROPE_PAD), const(seq_len, ROPE_PAD)]
    if state_mode == "stack":
        args += list(prev_state)
        in_specs += [pl.BlockSpec((n_seq, seq_len, KV_LORA), lambda i: (i, 0, 0)),
                     pl.BlockSpec((n_seq, seq_len, ROPE_DIM), lambda i: (i, 0, 0))]
    args += [wts["g_norm"], wts["w_in"], wts["g_q"], wts["w_uq"]]
    in_specs += [of_layer(1, D_MODEL), of_layer(D_MODEL, _C_END), of_layer(1, Q_LORA),
                 of_layer(Q_LORA, MLA_HEADS * HEAD_PAD)]
    if has_rope:
        args += [wts["w_uq_rot"]]
        in_specs += [of_layer(Q_LORA, MLA_HEADS * ROPE_PAD)]
    args += [wts["g_kv"], wts["w_ukv"], wts["pool_w"], wts["pool_s"], wts["conv_w"], wts["w_out"],
             wts["g_final"]]
    in_specs += [of_layer(1, KV_LORA), of_layer(KV_LORA, MLA_HEADS * KV_HEAD),
                 of_layer(POOL_WIDTH, POOL_WIDTH), of_layer(1, POOL_WIDTH), of_layer(3, CONV_WIDTH),
                 of_layer(MIX_WIDTH, D_MODEL), const(1, D_MODEL)]

    out_shape = [jax.ShapeDtypeStruct((rows, D_MODEL), _F32)]
    out_specs = [pl.BlockSpec((m, D_MODEL), lambda i: (i, 0))]
    if state_mode == "own":
        out_shape += [jax.ShapeDtypeStruct((n_all, seq_len, KV_LORA), _F32),
                      jax.ShapeDtypeStruct((n_all, seq_len, ROPE_DIM), _F32)]
        out_specs += [pl.BlockSpec((n_seq, seq_len, KV_LORA), lambda i: (i, 0, 0)),
                      pl.BlockSpec((n_seq, seq_len, ROPE_DIM), lambda i: (i, 0, 0))]
    elif state_mode == "stack":
        out_shape += [jax.ShapeDtypeStruct((n_all, DEPTH, seq_len, KV_LORA), _F32),
                      jax.ShapeDtypeStruct((n_all, DEPTH, seq_len, ROPE_DIM), _F32)]
        out_specs += [pl.BlockSpec((n_seq, DEPTH, seq_len, KV_LORA), lambda i: (i, 0, 0, 0)),
                      pl.BlockSpec((n_seq, DEPTH, seq_len, ROPE_DIM), lambda i: (i, 0, 0, 0))]

    kern = functools.partial(_layer_kernel, n_seq=n_seq, seq_len=seq_len, cache_len=cache_len,
                             has_rope=has_rope, final_norm=final_norm, state_mode=state_mode)
    return pl.pallas_call(
        kern,
        grid=(n_steps,),
        in_specs=in_specs,
        out_specs=out_specs,
        out_shape=out_shape,
        scratch_shapes=[
            pltpu.VMEM((m, D_MODEL), _BF),
            pltpu.VMEM((MLA_HEADS, m, HEAD_PAD), _BF),
            pltpu.VMEM((MLA_HEADS, n_seq * keys, HEAD_PAD), _BF),
            pltpu.VMEM((MLA_HEADS, n_seq * keys, V_DIM), _BF),
            pltpu.VMEM((m, MIX_WIDTH), _BF),
        ],
        compiler_params=pltpu.CompilerParams(dimension_semantics=("arbitrary",),
                                             vmem_limit_bytes=VMEM_LIMIT_BYTES),
        name="mixer_layer_latent" if has_rope else "mixer_layer_context",
    )(*args)


def _rope_tables(seq_len):
    rows = seq_len // GRID_W
    row = np.repeat(np.arange(rows), GRID_W).astype(np.float32)
    col = np.tile(np.arange(GRID_W), rows).astype(np.float32)
    inv = (1.0 / (np.float32(ROPE_BASE) ** (np.arange(0, AXIS_DIM, 2, dtype=np.float32) / np.float32(AXIS_DIM))))
    inv = inv.astype(np.float32)
    ang = np.concatenate([row[:, None] * inv, col[:, None] * inv], axis=-1).astype(np.float64)
    z = np.zeros((seq_len, ROPE_PAD - ROPE_DIM), np.float32)
    cos, sin = np.cos(ang).astype(np.float32), np.sin(ang).astype(np.float32)
    return (jnp.asarray(np.concatenate([cos, cos, z], axis=-1)),
            jnp.asarray(np.concatenate([sin, sin, z], axis=-1)))


def kernel(x_prompt, x_sample, cache_ckv, cache_krope, c, c_ctx, w_mod, b_mod, g_norm, w_in, g_q, w_uq,
           g_kv, w_ukv, pool_w, pool_s, conv_w, w_out, g_final):
    batch, seq, _ = x_prompt.shape
    dec_batch, dec_seq, _ = x_sample.shape
    assert 1 + dec_batch <= MOD_ROWS and dec_seq == ROWS_PER_STEP and ROWS_PER_STEP % seq == 0

    c_all = jnp.concatenate([c_ctx[None], c, jnp.zeros((MOD_ROWS - 1 - dec_batch, D_MODEL), _F32)], axis=0)
    mod_all = _modulation(c_all, w_mod, b_mod)
    w_in_r, w_out_r, w_uq_r, w_uq_rot, w_ukv_r, pool_w_r = _prep_weights(w_in, w_out, w_uq, w_ukv, pool_w)
    wts = {
        "g_norm": g_norm.reshape(DEPTH, 1, D_MODEL), "w_in": w_in_r,
        "g_q": g_q.reshape(DEPTH, 1, Q_LORA), "w_uq": w_uq_r, "w_uq_rot": w_uq_rot,
        "g_kv": g_kv.reshape(DEPTH, 1, KV_LORA), "w_ukv": w_ukv_r,
        "pool_w": pool_w_r, "pool_s": pool_s.reshape(DEPTH, 1, POOL_WIDTH), "conv_w": conv_w,
        "w_out": w_out_r, "g_final": g_final.reshape(1, D_MODEL),
    }

    xp = x_prompt.reshape(batch * seq, D_MODEL)
    xp, ckv0, kr0 = _mixer_layer(xp, mod_all, 0, wts, seq_len=seq, final_norm=False, state_mode="own")
    xp, state_ckv, state_krope = _mixer_layer(xp, mod_all, 1, wts, seq_len=seq, prev_state=(ckv0, kr0),
                                              final_norm=True, state_mode="stack")
    y_prompt = xp.reshape(batch, seq, D_MODEL)

    rope = _rope_tables(dec_seq)
    xs = x_sample.reshape(dec_batch * dec_seq, D_MODEL)
    for l in range(DEPTH):
        (xs,) = _mixer_layer(xs, mod_all, l, wts, seq_len=dec_seq, cache=(cache_ckv, cache_krope), rope=rope,
                             final_norm=(l == DEPTH - 1), state_mode=None)
    y_sample = xs.reshape(dec_batch, dec_seq, D_MODEL)
    return (y_prompt, y_sample, state_ckv, state_krope)
```

```python
import functools

import numpy as np
import jax
import jax.numpy as jnp
from jax import lax
from jax.experimental import pallas as pl
from jax.experimental.pallas import tpu as pltpu

D_MODEL = 1024
DEPTH = 2
GRID_W = 64
MLA_HEADS = 4
NOPE_DIM = 128
ROPE_DIM = 64
V_DIM = 128
QK_DIM = NOPE_DIM + ROPE_DIM
MLA_WIDTH = MLA_HEADS * V_DIM
Q_LORA = 384
KV_LORA = 256
POOL_GROUPS = 4
POOL_GROUP_DIM = 64
POOL_WIDTH = POOL_GROUPS * POOL_GROUP_DIM
CONV_WIDTH = 256
MIX_WIDTH = MLA_WIDTH + POOL_WIDTH + CONV_WIDTH
ROPE_BASE = 10000.0
AXIS_DIM = ROPE_DIM // 2
ATTN_SCALE = QK_DIM ** -0.5
EPS = 1e-6

_SPLITS = (Q_LORA, KV_LORA, ROPE_DIM, MLA_WIDTH, POOL_WIDTH, POOL_WIDTH,
           CONV_WIDTH, CONV_WIDTH, CONV_WIDTH, CONV_WIDTH)
_OFFS = [sum(_SPLITS[:i]) for i in range(len(_SPLITS) + 1)]
IN_WIDTH = _OFFS[-1]

V7X_MXU_DEPTH = 256
HEAD_PAD = V7X_MXU_DEPTH
ROPE_PAD = HEAD_PAD - NOPE_DIM
KV_HEAD = NOPE_DIM + V_DIM

_C_A = 0
_C_B = _C_A + Q_LORA + KV_LORA
_C_C = _C_B + 2 * ROPE_PAD
_C_D = _C_C + MLA_WIDTH
_C_E = _C_D + 2 * POOL_WIDTH
_C_END = _C_E + 4 * CONV_WIDTH

MOD_ROWS = 8
ROWS_PER_STEP = 1024
Q_TILE = 512
PREP_COLS = 256
VMEM_LIMIT_BYTES = 63 * 1024 * 1024

_BF = jnp.bfloat16
_F32 = jnp.float32


def _dot(a, b):
    return jnp.dot(a, b, preferred_element_type=_F32)


def _rms(x, g):
    return x * lax.rsqrt(jnp.mean(x * x, axis=-1, keepdims=True) + EPS) * g


def _silu(x):
    return x * jax.nn.sigmoid(x)


def _mod_kernel(c_ref, w_ref, b_ref, o_ref):
    s = _silu(c_ref[...])
    bias = b_ref[pl.ds(pl.program_id(0), 1), :]
    o_ref[...] = _dot(s.astype(_BF), w_ref[...].astype(_BF)) + bias


def _modulation(c_all, w_mod, b_mod):
    return pl.pallas_call(
        _mod_kernel,
        grid=(DEPTH, 3),
        in_specs=[
            pl.BlockSpec((MOD_ROWS, D_MODEL), lambda l, j: (0, 0)),
            pl.BlockSpec((None, D_MODEL, D_MODEL), lambda l, j: (l, 0, j)),
            pl.BlockSpec((DEPTH, D_MODEL), lambda l, j: (0, j)),
        ],
        out_specs=pl.BlockSpec((None, None, MOD_ROWS, D_MODEL), lambda l, j: (l, j, 0, 0)),
        out_shape=jax.ShapeDtypeStruct((DEPTH, 3, MOD_ROWS, D_MODEL), _F32),
        name="modulation",
    )(c_all, w_mod, b_mod)


def _prep_kernel(wint_ref, wout_ref, wuq_ref, wukv_ref, poolw_ref,
                 win_o, wout_o, wuq_o, wuqr_o, wukv_o, poolw_o):
    def put(dst, src, width):
        for c in range(0, width, PREP_COLS):
            n = min(PREP_COLS, width - c)
            win_o[:, dst + c:dst + c + n] = wint_ref[src + c:src + c + n, :].T.astype(_BF)

    put(_C_A, _OFFS[0], Q_LORA + KV_LORA)
    kr = wint_ref[_OFFS[2]:_OFFS[3], :]
    zpad = jnp.zeros((ROPE_PAD - ROPE_DIM, D_MODEL), _F32)
    krb = jnp.concatenate([kr, zpad, -kr[AXIS_DIM:], kr[:AXIS_DIM], zpad], axis=0)
    win_o[:, _C_B:_C_C] = krb.T.astype(_BF)
    put(_C_C, _OFFS[3], IN_WIDTH - _OFFS[3])
    wout_o[...] = wout_ref[...].astype(_BF)

    zq = jnp.zeros((Q_LORA, HEAD_PAD - QK_DIM), _BF)
    for hd in range(MLA_HEADS):
        s0 = hd * QK_DIM
        wuq_o[:, hd * HEAD_PAD:hd * HEAD_PAD + QK_DIM] = wuq_ref[:, s0:s0 + QK_DIM].astype(_BF)
        wuq_o[:, hd * HEAD_PAD + QK_DIM:(hd + 1) * HEAD_PAD] = zq
        r0 = s0 + NOPE_DIM
        d0 = hd * ROPE_PAD
        wuqr_o[:, d0:d0 + AXIS_DIM] = (-wuq_ref[:, r0 + AXIS_DIM:r0 + ROPE_DIM]).astype(_BF)
        wuqr_o[:, d0 + AXIS_DIM:d0 + ROPE_DIM] = wuq_ref[:, r0:r0 + AXIS_DIM].astype(_BF)
        wuqr_o[:, d0 + ROPE_DIM:d0 + ROPE_PAD] = zq
    wukv_o[...] = wukv_ref[...].astype(_BF)
    poolw_o[...] = jnp.zeros((POOL_WIDTH, POOL_WIDTH), _BF)
    for g in range(POOL_GROUPS):
        sl = slice(g * POOL_GROUP_DIM, (g + 1) * POOL_GROUP_DIM)
        poolw_o[sl, sl] = poolw_ref[g].astype(_BF)


def _prep_weights(w_in_t, w_out, w_uq, w_ukv, pool_w):
    per_layer = lambda *shape: pl.BlockSpec((None,) + shape, lambda l: (l,) + (0,) * len(shape))
    return pl.pallas_call(
        _prep_kernel,
        grid=(DEPTH,),
        in_specs=[
            per_layer(IN_WIDTH, D_MODEL),
            per_layer(MIX_WIDTH, D_MODEL),
            per_layer(Q_LORA, MLA_HEADS * QK_DIM),
            per_layer(KV_LORA, MLA_HEADS * KV_HEAD),
            per_layer(POOL_GROUPS, POOL_GROUP_DIM, POOL_GROUP_DIM),
        ],
        out_specs=[
            per_layer(D_MODEL, _C_END),
            per_layer(MIX_WIDTH, D_MODEL),
            per_layer(Q_LORA, MLA_HEADS * HEAD_PAD),
            per_layer(Q_LORA, MLA_HEADS * ROPE_PAD),
            per_layer(KV_LORA, MLA_HEADS * KV_HEAD),
            per_layer(POOL_WIDTH, POOL_WIDTH),
        ],
        out_shape=[
            jax.ShapeDtypeStruct((DEPTH, D_MODEL, _C_END), _BF),
            jax.ShapeDtypeStruct((DEPTH, MIX_WIDTH, D_MODEL), _BF),
            jax.ShapeDtypeStruct((DEPTH, Q_LORA, MLA_HEADS * HEAD_PAD), _BF),
            jax.ShapeDtypeStruct((DEPTH, Q_LORA, MLA_HEADS * ROPE_PAD), _BF),
            jax.ShapeDtypeStruct((DEPTH, KV_LORA, MLA_HEADS * KV_HEAD), _BF),
            jax.ShapeDtypeStruct((DEPTH, POOL_WIDTH, POOL_WIDTH), _BF),
        ],
        compiler_params=pltpu.CompilerParams(dimension_semantics=("arbitrary",),
                                             vmem_limit_bytes=VMEM_LIMIT_BYTES),
        name="weight_prep",
    )(w_in_t, w_out, w_uq, w_ukv, pool_w)


def _pool_mix(px, seq_len):
    m = px.shape[0]
    t = lax.broadcasted_iota(jnp.int32, (m, 1), 0) & (seq_len - 1)
    lane = lax.broadcasted_iota(jnp.int32, (1, POOL_WIDTH), 1)
    half = jnp.left_shift(1, lane >> 6)
    acc = px
    max_half = 1 << (POOL_GROUPS - 1)
    for d in range(-max_half, max_half):
        if d == 0:
            continue
        shifted = pltpu.roll(px, (-d) % m, axis=0)
        row_ok = (t + d >= 0) & (t + d < seq_len)
        lane_ok = (d >= -half) & (d < half)
        acc = acc + jnp.where(row_ok & lane_ok, shifted, 0.0)
    cnt = jnp.minimum(t + half, seq_len) - jnp.maximum(t - half, 0)
    return acc / cnt.astype(_F32) - px


def _short_conv(z, w, seq_len):
    m = z.shape[0]
    t = lax.broadcasted_iota(jnp.int32, (m, 1), 0) & (seq_len - 1)
    prev = jnp.where(t >= 1, pltpu.roll(z, 1, axis=0), 0.0)
    nxt = jnp.where(t <= seq_len - 2, pltpu.roll(z, m - 1, axis=0), 0.0)
    return prev * w[0] + z * w[1] + nxt * w[2]


def _layer_kernel(*refs, layer, n_seq, seq_len, cache_len, has_rope, final_norm, state_mode):
    it = iter(refs)
    x_ref, mod_ref = next(it), next(it)
    if cache_len:
        cckv_ref, ckr_ref = next(it), next(it)
    if has_rope:
        cos_ref, sin_ref = next(it), next(it)
    if state_mode == "stack":
        ckv_prev_ref, kr_prev_ref = next(it), next(it)
    gn_ref, win_ref, gq_ref, wuq_ref = next(it), next(it), next(it), next(it)
    if has_rope:
        wuqs_ref = next(it)
    gkv_ref, wukv_ref, poolw_ref, pools_ref, convw_ref, wout_ref, gfin_ref = (
        next(it), next(it), next(it), next(it), next(it), next(it), next(it))
    y_ref = next(it)
    if state_mode:
        ckv_out_ref, kr_out_ref = next(it), next(it)
    h_s, q_s, k_s, v_s, mixed_s = next(it), next(it), next(it), next(it), next(it)

    keys = cache_len + seq_len
    pair = 2 * KV_HEAD

    mrow = (1 + pl.program_id(0)) if has_rope else 0
    shift, scale, gate = (mod_ref[k, pl.ds(mrow, 1), :] for k in range(3))
    row = slice(layer, layer + 1)
    h_s[...] = (_rms(x_ref[...], gn_ref[row, :]) * (1.0 + scale) + shift).astype(_BF)

    a = _dot(h_s[...], win_ref[:, _C_A:_C_B])
    cq = _rms(a[:, :Q_LORA], gq_ref[row, :]).astype(_BF)
    ckv = _rms(a[:, Q_LORA:], gkv_ref[row, :])
    b = _dot(h_s[...], win_ref[:, _C_B:_C_C])
    kr = b[:, :ROPE_PAD]
    if has_rope:
        kr = kr * cos_ref[...] + b[:, ROPE_PAD:] * sin_ref[...]
    if state_mode:
        own = 0 if state_mode == "own" else layer
        if state_mode == "stack":
            ckv_out_ref[:, 0] = ckv_prev_ref[...]
            kr_out_ref[:, 0] = kr_prev_ref[...]
        for s in range(n_seq):
            rs = slice(s * seq_len, (s + 1) * seq_len)
            kr_t = kr[rs, :].T[:ROPE_DIM, :]
            if state_mode == "own":
                ckv_out_ref[s] = ckv[rs, :]
                kr_out_ref[s] = kr_t
            else:
                ckv_out_ref[s, own] = ckv[rs, :]
                kr_out_ref[s, own] = kr_t
    ckv_bf = ckv.astype(_BF)
    kr_bf = kr.astype(_BF)
    if cache_len:
        cckv_bf = cckv_ref[...].astype(_BF)
        ckr_t = ckr_ref[...]
        ckr_bf = jnp.concatenate([ckr_t, jnp.zeros_like(ckr_t)], axis=0).T.astype(_BF)

    for hp in range(MLA_HEADS // 2):
        q2 = _dot(cq, wuq_ref[:, hp * 2 * HEAD_PAD:(hp + 1) * 2 * HEAD_PAD]) * ATTN_SCALE
        if has_rope:
            qs2 = _dot(cq, wuqs_ref[:, hp * 2 * ROPE_PAD:(hp + 1) * 2 * ROPE_PAD]) * ATTN_SCALE
        kv2 = _dot(ckv_bf, wukv_ref[:, hp * pair:(hp + 1) * pair])
        if cache_len:
            kvc2 = _dot(cckv_bf, wukv_ref[:, hp * pair:(hp + 1) * pair])
        for j in range(2):
            hd = 2 * hp + j
            c0 = j * HEAD_PAD
            q_s[hd, :, :NOPE_DIM] = q2[:, c0:c0 + NOPE_DIM].astype(_BF)
            qr = q2[:, c0 + NOPE_DIM:c0 + HEAD_PAD]
            if has_rope:
                qr = qr * cos_ref[...] + qs2[:, j * ROPE_PAD:(j + 1) * ROPE_PAD] * sin_ref[...]
            q_s[hd, :, NOPE_DIM:] = qr.astype(_BF)
            k0 = j * KV_HEAD
            for s in range(n_seq):
                r0 = s * keys
                if cache_len:
                    k_s[hd, r0:r0 + cache_len, :NOPE_DIM] = kvc2[:, k0:k0 + NOPE_DIM].astype(_BF)
                    k_s[hd, r0:r0 + cache_len, NOPE_DIM:] = ckr_bf
                    v_s[hd, r0:r0 + cache_len, :] = kvc2[:, k0 + NOPE_DIM:k0 + KV_HEAD].astype(_BF)
                rn = slice(r0 + cache_len, r0 + keys)
                rs = slice(s * seq_len, (s + 1) * seq_len)
                k_s[hd, rn, :NOPE_DIM] = kv2[rs, k0:k0 + NOPE_DIM].astype(_BF)
                k_s[hd, rn, NOPE_DIM:] = kr_bf[rs, :]
                v_s[hd, rn, :] = kv2[rs, k0 + NOPE_DIM:k0 + KV_HEAD].astype(_BF)

    tq = min(Q_TILE, seq_len)
    for hp in range(MLA_HEADS // 2):
        g2 = _silu(_dot(h_s[...], win_ref[:, _C_C + hp * 2 * V_DIM:_C_C + (hp + 1) * 2 * V_DIM]))
        for j in range(2):
            hd = 2 * hp + j
            for s in range(n_seq):
                kh = k_s[hd, s * keys:(s + 1) * keys, :]
                vh = v_s[hd, s * keys:(s + 1) * keys, :]
                for qb in range(seq_len // tq):
                    rows = slice(s * seq_len + qb * tq, s * seq_len + (qb + 1) * tq)
                    sc = lax.dot_general(q_s[hd, rows, :], kh, (((1,), (1,)), ((), ())),
                                         preferred_element_type=_F32)
                    p = jnp.exp(sc - jnp.max(sc, axis=-1, keepdims=True))
                    o = _dot(p.astype(_BF), vh) / jnp.sum(p, axis=-1, keepdims=True)
                    mixed_s[rows, hd * V_DIM:(hd + 1) * V_DIM] = (
                        g2[rows, j * V_DIM:(j + 1) * V_DIM] * o).astype(_BF)

    dgrp = _dot(h_s[...], win_ref[:, _C_D:_C_E])
    pooled = _pool_mix(dgrp[:, :POOL_WIDTH], seq_len)
    pool = _dot(pooled.astype(_BF), poolw_ref[...]) * pools_ref[row, :]
    mixed_s[:, MLA_WIDTH:MLA_WIDTH + POOL_WIDTH] = (_silu(dgrp[:, POOL_WIDTH:]) * pool).astype(_BF)

    e1 = _dot(h_s[...], win_ref[:, _C_E + CONV_WIDTH:_C_E + 3 * CONV_WIDTH])
    convw = [convw_ref[k, layer:layer + 1, :] for k in range(3)]
    conv = _short_conv(e1[:, :CONV_WIDTH] * e1[:, CONV_WIDTH:], convw, seq_len)
    cb = _dot(h_s[...], win_ref[:, _C_E:_C_E + CONV_WIDTH])
    gc = _dot(h_s[...], win_ref[:, _C_E + 3 * CONV_WIDTH:_C_END])
    mixed_s[:, MLA_WIDTH + POOL_WIDTH:] = (_silu(gc) * (cb * conv)).astype(_BF)

    y = x_ref[...] + gate * _dot(mixed_s[...], wout_ref[...])
    if final_norm:
        y = _rms(y, gfin_ref[...])
    y_ref[...] = y


def _mixer_layer(x2d, mod_all, layer, wts, *, seq_len, cache=None, rope=None, prev_state=None,
                 final_norm, state_mode):
    rows = x2d.shape[0]
    m = ROWS_PER_STEP
    n_seq = m // seq_len
    n_steps = rows // m
    n_all = rows // seq_len
    cache_len = 0 if cache is None else cache[0].shape[2]
    keys = cache_len + seq_len
    has_rope = rope is not None

    def const(*shape):
        return pl.BlockSpec(shape, lambda i: (0,) * len(shape), pipeline_mode=pl.Buffered(1))

    def of_layer(*shape):
        return pl.BlockSpec((None,) + shape, lambda i: (layer,) + (0,) * len(shape),
                            pipeline_mode=pl.Buffered(1))

    args = [x2d, mod_all]
    in_specs = [pl.BlockSpec((m, D_MODEL), lambda i: (i, 0)), of_layer(3, MOD_ROWS, D_MODEL)]
    if cache_len:
        args += [cache[0], cache[1]]
        in_specs += [pl.BlockSpec((None, None, cache_len, KV_LORA), lambda i: (i, layer, 0, 0)),
                     pl.BlockSpec((None, None, ROPE_DIM, cache_len), lambda i: (i, layer, 0, 0))]
    if has_rope:
        args += [rope[0], rope[1]]
        in_specs += [const(seq_len, ROPE_PAD), const(seq_len, ROPE_PAD)]
    if state_mode == "stack":
        args += list(prev_state)
        in_specs += [pl.BlockSpec((n_seq, seq_len, KV_LORA), lambda i: (i, 0, 0)),
                     pl.BlockSpec((n_seq, ROPE_DIM, seq_len), lambda i: (i, 0, 0))]
    args += [wts["g_norm"], wts["w_in"], wts["g_q"], wts["w_uq"]]
    in_specs += [const(DEPTH, D_MODEL), of_layer(D_MODEL, _C_END), const(DEPTH, Q_LORA),
                 of_layer(Q_LORA, MLA_HEADS * HEAD_PAD)]
    if has_rope:
        args += [wts["w_uq_rot"]]
        in_specs += [of_layer(Q_LORA, MLA_HEADS * ROPE_PAD)]
    args += [wts["g_kv"], wts["w_ukv"], wts["pool_w"], wts["pool_s"], wts["conv_w"], wts["w_out"],
             wts["g_final"]]
    in_specs += [const(DEPTH, KV_LORA), of_layer(KV_LORA, MLA_HEADS * KV_HEAD),
                 of_layer(POOL_WIDTH, POOL_WIDTH), const(DEPTH, POOL_WIDTH), const(3, DEPTH, CONV_WIDTH),
                 of_layer(MIX_WIDTH, D_MODEL), const(1, D_MODEL)]

    out_shape = [jax.ShapeDtypeStruct((rows, D_MODEL), _F32)]
    out_specs = [pl.BlockSpec((m, D_MODEL), lambda i: (i, 0))]
    if state_mode == "own":
        out_shape += [jax.ShapeDtypeStruct((n_all, seq_len, KV_LORA), _F32),
                      jax.ShapeDtypeStruct((n_all, ROPE_DIM, seq_len), _F32)]
        out_specs += [pl.BlockSpec((n_seq, seq_len, KV_LORA), lambda i: (i, 0, 0)),
                      pl.BlockSpec((n_seq, ROPE_DIM, seq_len), lambda i: (i, 0, 0))]
    elif state_mode == "stack":
        out_shape += [jax.ShapeDtypeStruct((n_all, DEPTH, seq_len, KV_LORA), _F32),
                      jax.ShapeDtypeStruct((n_all, DEPTH, ROPE_DIM, seq_len), _F32)]
        out_specs += [pl.BlockSpec((n_seq, DEPTH, seq_len, KV_LORA), lambda i: (i, 0, 0, 0)),
                      pl.BlockSpec((n_seq, DEPTH, ROPE_DIM, seq_len), lambda i: (i, 0, 0, 0))]

    kern = functools.partial(_layer_kernel, layer=layer, n_seq=n_seq, seq_len=seq_len, cache_len=cache_len,
                             has_rope=has_rope, final_norm=final_norm, state_mode=state_mode)
    return pl.pallas_call(
        kern,
        grid=(n_steps,),
        in_specs=in_specs,
        out_specs=out_specs,
        out_shape=out_shape,
        scratch_shapes=[
            pltpu.VMEM((m, D_MODEL), _BF),
            pltpu.VMEM((MLA_HEADS, m, HEAD_PAD), _BF),
            pltpu.VMEM((MLA_HEADS, n_seq * keys, HEAD_PAD), _BF),
            pltpu.VMEM((MLA_HEADS, n_seq * keys, V_DIM), _BF),
            pltpu.VMEM((m, MIX_WIDTH), _BF),
        ],
        compiler_params=pltpu.CompilerParams(dimension_semantics=("arbitrary",),
                                             vmem_limit_bytes=VMEM_LIMIT_BYTES),
        name="mixer_layer_latent" if has_rope else "mixer_layer_context",
    )(*args)


def _rope_tables(seq_len):
    rows = seq_len // GRID_W
    row = np.repeat(np.arange(rows), GRID_W).astype(np.float32)
    col = np.tile(np.arange(GRID_W), rows).astype(np.float32)
    inv = (1.0 / (np.float32(ROPE_BASE) ** (np.arange(0, AXIS_DIM, 2, dtype=np.float32) / np.float32(AXIS_DIM))))
    inv = inv.astype(np.float32)
    ang = np.concatenate([row[:, None] * inv, col[:, None] * inv], axis=-1).astype(np.float64)
    z = np.zeros((seq_len, ROPE_PAD - ROPE_DIM), np.float32)
    cos, sin = np.cos(ang).astype(np.float32), np.sin(ang).astype(np.float32)
    return (jnp.asarray(np.concatenate([cos, cos, z], axis=-1)),
            jnp.asarray(np.concatenate([sin, sin, z], axis=-1)))


def kernel(x_prompt, x_sample, cache_ckv, cache_krope, c, c_ctx, w_mod, b_mod, g_norm, w_in, g_q, w_uq,
           g_kv, w_ukv, pool_w, pool_s, conv_w, w_out, g_final):
    batch, seq, _ = x_prompt.shape
    dec_batch, dec_seq, _ = x_sample.shape
    assert 1 + dec_batch <= MOD_ROWS and dec_seq == ROWS_PER_STEP and ROWS_PER_STEP % seq == 0

    c_all = jnp.concatenate([c_ctx[None], c, jnp.zeros((MOD_ROWS - 1 - dec_batch, D_MODEL), _F32)], axis=0)
    mod_all = _modulation(c_all, w_mod, b_mod)
    w_in_r, w_out_r, w_uq_r, w_uq_rot, w_ukv_r, pool_w_r = _prep_weights(
        jnp.transpose(w_in, (0, 2, 1)), w_out, w_uq, w_ukv, pool_w)
    wts = {
        "g_norm": g_norm, "w_in": w_in_r, "g_q": g_q, "w_uq": w_uq_r, "w_uq_rot": w_uq_rot,
        "g_kv": g_kv, "w_ukv": w_ukv_r, "pool_w": pool_w_r, "pool_s": pool_s,
        "conv_w": jnp.transpose(conv_w, (1, 0, 2)), "w_out": w_out_r, "g_final": g_final.reshape(1, D_MODEL),
    }

    xp = x_prompt.reshape(batch * seq, D_MODEL)
    xp, ckv0, kr0 = _mixer_layer(xp, mod_all, 0, wts, seq_len=seq, final_norm=False, state_mode="own")
    xp, state_ckv, state_krope_t = _mixer_layer(xp, mod_all, 1, wts, seq_len=seq, prev_state=(ckv0, kr0),
                                                final_norm=True, state_mode="stack")
    y_prompt = xp.reshape(batch, seq, D_MODEL)
    state_krope = jnp.transpose(state_krope_t, (0, 1, 3, 2))

    rope = _rope_tables(dec_seq)
    xs = x_sample.reshape(dec_batch * dec_seq, D_MODEL)
    cache = (cache_ckv, jnp.transpose(cache_krope, (0, 1, 3, 2)))
    for l in range(DEPTH):
        (xs,) = _mixer_layer(xs, mod_all, l, wts, seq_len=dec_seq, cache=cache, rope=rope,
                             final_norm=(l == DEPTH - 1), state_mode=None)
    y_sample = xs.reshape(dec_batch, dec_seq, D_MODEL)
    return (y_prompt, y_sample, state_ckv, state_krope)
```

```python
import functools

import numpy as np
import jax
import jax.numpy as jnp
from jax import lax
from jax.experimental import pallas as pl
from jax.experimental.pallas import tpu as pltpu

D_MODEL = 1024
DEPTH = 2
GRID_W = 64
MLA_HEADS = 4
NOPE_DIM = 128
ROPE_DIM = 64
V_DIM = 128
QK_DIM = NOPE_DIM + ROPE_DIM
MLA_WIDTH = MLA_HEADS * V_DIM
Q_LORA = 384
KV_LORA = 256
POOL_GROUPS = 4
POOL_GROUP_DIM = 64
POOL_WIDTH = POOL_GROUPS * POOL_GROUP_DIM
CONV_WIDTH = 256
MIX_WIDTH = MLA_WIDTH + POOL_WIDTH + CONV_WIDTH
ROPE_BASE = 10000.0
AXIS_DIM = ROPE_DIM // 2
ATTN_SCALE = QK_DIM ** -0.5
EPS = 1e-6

_SPLITS = (Q_LORA, KV_LORA, ROPE_DIM, MLA_WIDTH, POOL_WIDTH, POOL_WIDTH,
           CONV_WIDTH, CONV_WIDTH, CONV_WIDTH, CONV_WIDTH)
_OFFS = [sum(_SPLITS[:i]) for i in range(len(_SPLITS) + 1)]
IN_WIDTH = _OFFS[-1]

V7X_LANES = 128
V7X_SUBLANES = 8
V7X_MXU_DEPTH = 256
HEAD_PAD = V7X_MXU_DEPTH
ROPE_PAD = HEAD_PAD - NOPE_DIM
KV_HEAD = NOPE_DIM + V_DIM

_C_CQ = 0
_C_CKV = _C_CQ + Q_LORA
_C_KR = _C_CKV + KV_LORA
_C_GMLA = _C_KR + ROPE_PAD
_C_GPOOL = _C_GMLA + MLA_WIDTH
_C_PX = _C_GPOOL + POOL_WIDTH
_C_CC = _C_PX + POOL_WIDTH
_C_CB = _C_CC + 2 * CONV_WIDTH
_C_END = _C_CB + 2 * CONV_WIDTH

MOD_ROWS = 8
ROWS_PER_STEP = 1024
ROW_CHUNK = 256
HALO = V7X_SUBLANES
PREP_COLS = 256
VMEM_LIMIT_BYTES = 56 * 1024 * 1024

_BF = jnp.bfloat16
_F32 = jnp.float32


def _dot(a, b):
    return jnp.dot(a, b, preferred_element_type=_F32)


def _rms(x, g):
    return x * lax.rsqrt(jnp.mean(x * x, axis=-1, keepdims=True) + EPS) * g


def _silu(x):
    return x * jax.nn.sigmoid(x)


def _mod_kernel(c_ref, w_ref, b_ref, o_ref):
    s = _silu(c_ref[...])
    bias = b_ref[pl.ds(pl.program_id(0), 1), :]
    o_ref[...] = _dot(s.astype(_BF), w_ref[...].astype(_BF)) + bias


def _modulation(c_all, w_mod, b_mod):
    return pl.pallas_call(
        _mod_kernel,
        grid=(DEPTH, 3),
        in_specs=[
            pl.BlockSpec((MOD_ROWS, D_MODEL), lambda l, j: (0, 0)),
            pl.BlockSpec((None, D_MODEL, D_MODEL), lambda l, j: (l, 0, j)),
            pl.BlockSpec((DEPTH, D_MODEL), lambda l, j: (0, j)),
        ],
        out_specs=pl.BlockSpec((None, None, MOD_ROWS, D_MODEL), lambda l, j: (l, j, 0, 0)),
        out_shape=jax.ShapeDtypeStruct((DEPTH, 3, MOD_ROWS, D_MODEL), _F32),
        name="modulation",
    )(c_all, w_mod, b_mod)


def _prep_kernel(wint_ref, wout_ref, wuq_ref, wukv_ref, poolw_ref,
                 win_o, wout_o, wuq_o, wukv_o, poolw_o):
    def put(dst, src, width):
        for c in range(0, width, PREP_COLS):
            n = min(PREP_COLS, width - c)
            win_o[:, dst + c:dst + c + n] = wint_ref[src + c:src + c + n, :].T.astype(_BF)

    put(_C_CQ, _OFFS[0], Q_LORA + KV_LORA)
    kr = wint_ref[_OFFS[2]:_OFFS[3], :]
    krb = jnp.concatenate([kr, -kr[AXIS_DIM:], kr[:AXIS_DIM]], axis=0)
    win_o[:, _C_KR:_C_GMLA] = krb.T.astype(_BF)
    put(_C_GMLA, _OFFS[3], MLA_WIDTH)
    put(_C_GPOOL, _OFFS[5], POOL_WIDTH)
    put(_C_PX, _OFFS[4], POOL_WIDTH)
    put(_C_CC, _OFFS[7], 2 * CONV_WIDTH)
    put(_C_CB, _OFFS[6], CONV_WIDTH)
    put(_C_CB + CONV_WIDTH, _OFFS[9], CONV_WIDTH)
    wout_o[...] = wout_ref[...].astype(_BF)

    for hd in range(MLA_HEADS):
        s0, d0 = hd * QK_DIM, hd * HEAD_PAD
        r0 = s0 + NOPE_DIM
        wuq_o[:, d0:d0 + QK_DIM] = wuq_ref[:, s0:s0 + QK_DIM].astype(_BF)
        wuq_o[:, d0 + QK_DIM:d0 + QK_DIM + AXIS_DIM] = (-wuq_ref[:, r0 + AXIS_DIM:r0 + ROPE_DIM]).astype(_BF)
        wuq_o[:, d0 + QK_DIM + AXIS_DIM:d0 + HEAD_PAD] = wuq_ref[:, r0:r0 + AXIS_DIM].astype(_BF)
    wukv_o[...] = wukv_ref[...].astype(_BF)
    poolw_o[...] = jnp.zeros((POOL_WIDTH, POOL_WIDTH), _BF)
    for g in range(POOL_GROUPS):
        sl = slice(g * POOL_GROUP_DIM, (g + 1) * POOL_GROUP_DIM)
        poolw_o[sl, sl] = poolw_ref[g].astype(_BF)


def _prep_weights(w_in_t, w_out, w_uq, w_ukv, pool_w):
    per_layer = lambda *shape: pl.BlockSpec((None,) + shape, lambda l: (l,) + (0,) * len(shape))
    return pl.pallas_call(
        _prep_kernel,
        grid=(DEPTH,),
        in_specs=[
            per_layer(IN_WIDTH, D_MODEL),
            per_layer(MIX_WIDTH, D_MODEL),
            per_layer(Q_LORA, MLA_HEADS * QK_DIM),
            per_layer(KV_LORA, MLA_HEADS * KV_HEAD),
            per_layer(POOL_GROUPS, POOL_GROUP_DIM, POOL_GROUP_DIM),
        ],
        out_specs=[
            per_layer(D_MODEL, _C_END),
            per_layer(MIX_WIDTH, D_MODEL),
            per_layer(Q_LORA, MLA_HEADS * HEAD_PAD),
            per_layer(KV_LORA, MLA_HEADS * KV_HEAD),
            per_layer(POOL_WIDTH, POOL_WIDTH),
        ],
        out_shape=[
            jax.ShapeDtypeStruct((DEPTH, D_MODEL, _C_END), _BF),
            jax.ShapeDtypeStruct((DEPTH, MIX_WIDTH, D_MODEL), _BF),
            jax.ShapeDtypeStruct((DEPTH, Q_LORA, MLA_HEADS * HEAD_PAD), _BF),
            jax.ShapeDtypeStruct((DEPTH, KV_LORA, MLA_HEADS * KV_HEAD), _BF),
            jax.ShapeDtypeStruct((DEPTH, POOL_WIDTH, POOL_WIDTH), _BF),
        ],
        compiler_params=pltpu.CompilerParams(dimension_semantics=("arbitrary",),
                                             vmem_limit_bytes=VMEM_LIMIT_BYTES),
        name="weight_prep",
    )(w_in_t, w_out, w_uq, w_ukv, pool_w)


def _pool_mix(win, t, seq_len):
    rows = win.shape[0] - 2 * HALO
    lane = lax.broadcasted_iota(jnp.int32, (1, POOL_WIDTH), 1)
    half = jnp.left_shift(1, lane >> 6)
    px = win[HALO:HALO + rows]
    acc = px
    max_half = 1 << (POOL_GROUPS - 1)
    for d in range(-max_half, max_half):
        if d == 0:
            continue
        row_ok = (t + d >= 0) & (t + d < seq_len)
        lane_ok = (d >= -half) & (d < half)
        acc = acc + jnp.where(row_ok & lane_ok, win[HALO + d:HALO + d + rows], 0.0)
    cnt = jnp.minimum(t + half, seq_len) - jnp.maximum(t - half, 0)
    return acc / cnt.astype(_F32) - px


def _short_conv(win, w, t, seq_len):
    rows = win.shape[0] - 2 * HALO
    prev = jnp.where(t >= 1, win[HALO - 1:HALO - 1 + rows], 0.0)
    nxt = jnp.where(t <= seq_len - 2, win[HALO + 1:HALO + 1 + rows], 0.0)
    return prev * w[0] + win[HALO:HALO + rows] * w[1] + nxt * w[2]


def _layer_kernel(*refs, layer, n_seq, seq_len, cache_len, has_rope, final_norm, state_mode):
    it = iter(refs)
    x_ref, mod_ref = next(it), next(it)
    if cache_len:
        cckv_ref, ckr_ref = next(it), next(it)
    if has_rope:
        cs_ref = next(it)
    if state_mode == "stack":
        ckv_prev_ref, kr_prev_ref = next(it), next(it)
    (gn_ref, win_ref, gq_ref, wuq_ref, gkv_ref, wukv_ref, poolw_ref, pools_ref, convw_ref, wout_ref,
     gfin_ref) = (next(it) for _ in range(11))
    y_ref = next(it)
    if state_mode:
        ckv_out_ref, kr_out_ref = next(it), next(it)
    h_s, q_s, k_s, v_s, px_s, prod_s, mixed_s = (next(it) for _ in range(7))

    rc = ROW_CHUNK
    m = n_seq * seq_len
    keys = cache_len + seq_len
    pair = 2 * KV_HEAD
    row = slice(layer, layer + 1)
    lane = lax.broadcasted_iota(jnp.int32, (1, ROPE_PAD), 1)

    mrow = (1 + pl.program_id(0)) if has_rope else 0
    shift, scale, gate = (mod_ref[k, pl.ds(mrow, 1), :] for k in range(3))

    zeros_halo = jnp.zeros((HALO, POOL_WIDTH), _F32)
    for buf in (px_s, prod_s):
        buf[0:HALO, :] = zeros_halo
        buf[HALO + m:2 * HALO + m, :] = zeros_halo
    if state_mode == "stack":
        ckv_out_ref[:, 0] = ckv_prev_ref[...]
        kr_out_ref[:, 0] = kr_prev_ref[...]

    def rotate(v, r):
        v = v * cs_ref[pl.ds(r, rc), :]
        return v + pltpu.roll(v, ROPE_DIM, axis=1)

    def put_kv(kv2, kr_bf, hp, rows_k):
        for j in range(2):
            hd = 2 * hp + j
            k0 = j * KV_HEAD
            k_s[hd, rows_k, :NOPE_DIM] = kv2[:, k0:k0 + NOPE_DIM].astype(_BF)
            k_s[hd, rows_k, NOPE_DIM:] = kr_bf
            v_s[hd, rows_k, :] = kv2[:, k0 + NOPE_DIM:k0 + KV_HEAD].astype(_BF)

    if cache_len:
        ckr_t = ckr_ref[...]
        ckr_bf = jnp.concatenate([ckr_t, jnp.zeros_like(ckr_t)], axis=0).T.astype(_BF)
        for cc in range(cache_len // rc):
            rows_k = slice(cc * rc, (cc + 1) * rc)
            ck = cckv_ref[rows_k, :].astype(_BF)
            for hp in range(MLA_HEADS // 2):
                put_kv(_dot(ck, wukv_ref[:, hp * pair:(hp + 1) * pair]), ckr_bf[rows_k, :], hp, rows_k)

    def project(c, carry):
        r = pl.multiple_of(c * rc, rc)
        rows = pl.ds(r, rc)
        h = (_rms(x_ref[rows, :], gn_ref[row, :]) * (1.0 + scale) + shift).astype(_BF)
        h_s[rows, :] = h
        a = _dot(h, win_ref[:, _C_CQ:_C_GMLA])
        cq = _rms(a[:, :Q_LORA], gq_ref[row, :]).astype(_BF)
        ckv = _rms(a[:, _C_CKV:_C_KR], gkv_ref[row, :])
        kr = a[:, _C_KR:]
        if has_rope:
            kr = rotate(kr, r)
        kr = jnp.where(lane < ROPE_DIM, kr, 0.0)
        if state_mode:
            own = () if state_mode == "own" else (layer,)
            ckv_out_ref[(c,) + own] = ckv
            kr_out_ref[(c,) + own] = kr.T[:ROPE_DIM, :]
        ckv_bf = ckv.astype(_BF)
        kr_bf = kr.astype(_BF)
        rows_k = pl.ds(pl.multiple_of(cache_len + r, rc), rc)
        for hp in range(MLA_HEADS // 2):
            put_kv(_dot(ckv_bf, wukv_ref[:, hp * pair:(hp + 1) * pair]), kr_bf, hp, rows_k)
            q2 = _dot(cq, wuq_ref[:, hp * 2 * HEAD_PAD:(hp + 1) * 2 * HEAD_PAD]) * ATTN_SCALE
            for j in range(2):
                hd = 2 * hp + j
                c0 = j * HEAD_PAD
                q_s[hd, rows, :NOPE_DIM] = q2[:, c0:c0 + NOPE_DIM].astype(_BF)
                qr = q2[:, c0 + NOPE_DIM:c0 + HEAD_PAD]
                if has_rope:
                    qr = rotate(qr, r)
                q_s[hd, rows, NOPE_DIM:] = qr.astype(_BF)
        pe = _dot(h, win_ref[:, _C_PX:_C_CB])
        rows_h = pl.ds(pl.multiple_of(HALO + r, HALO), rc)
        px_s[rows_h, :] = pe[:, :POOL_WIDTH]
        prod_s[rows_h, :] = pe[:, POOL_WIDTH:POOL_WIDTH + CONV_WIDTH] * pe[:, POOL_WIDTH + CONV_WIDTH:]
        return carry

    lax.fori_loop(0, m // rc, project, 0)

    def mix(c, carry):
        r = pl.multiple_of(c * rc, rc)
        rows = pl.ds(r, rc)
        h = h_s[rows, :]
        t = (r + lax.broadcasted_iota(jnp.int32, (rc, 1), 0)) & (seq_len - 1)
        rows_k = pl.ds(0 if n_seq == 1 else r, keys)
        for hp in range(MLA_HEADS // 2):
            g2 = _silu(_dot(h, win_ref[:, _C_GMLA + hp * 2 * V_DIM:_C_GMLA + (hp + 1) * 2 * V_DIM]))
            for j in range(2):
                hd = 2 * hp + j
                sc = lax.dot_general(q_s[hd, rows, :], k_s[hd, rows_k, :], (((1,), (1,)), ((), ())),
                                     preferred_element_type=_F32)
                p = jnp.exp(sc - jnp.max(sc, axis=-1, keepdims=True))
                o = _dot(p.astype(_BF), v_s[hd, rows_k, :]) / jnp.sum(p, axis=-1, keepdims=True)
                mixed_s[rows, hd * V_DIM:(hd + 1) * V_DIM] = (g2[:, j * V_DIM:(j + 1) * V_DIM] * o).astype(_BF)

        rows_w = pl.ds(r, rc + 2 * HALO)
        pooled = _pool_mix(px_s[rows_w, :], t, seq_len)
        pool = _dot(pooled.astype(_BF), poolw_ref[...]) * pools_ref[row, :]
        gp = _silu(_dot(h, win_ref[:, _C_GPOOL:_C_PX]))
        mixed_s[rows, MLA_WIDTH:MLA_WIDTH + POOL_WIDTH] = (gp * pool).astype(_BF)

        convw = [convw_ref[k, row, :] for k in range(3)]
        conv = _short_conv(prod_s[rows_w, :], convw, t, seq_len)
        e2 = _dot(h, win_ref[:, _C_CB:_C_END])
        mixed_s[rows, MLA_WIDTH + POOL_WIDTH:] = (_silu(e2[:, CONV_WIDTH:]) * (e2[:, :CONV_WIDTH] * conv)).astype(_BF)

        y = x_ref[rows, :] + gate * _dot(mixed_s[rows, :], wout_ref[...])
        if final_norm:
            y = _rms(y, gfin_ref[...])
        y_ref[rows, :] = y
        return carry

    lax.fori_loop(0, m // rc, mix, 0)


def _mixer_layer(x2d, mod_all, layer, wts, *, seq_len, cache=None, rope=None, prev_state=None,
                 final_norm, state_mode):
    rows = x2d.shape[0]
    m = ROWS_PER_STEP
    n_seq = m // seq_len
    n_steps = rows // m
    n_all = rows // seq_len
    cache_len = 0 if cache is None else cache[0].shape[2]
    keys = cache_len + seq_len
    has_rope = rope is not None
    assert seq_len % ROW_CHUNK == 0 and cache_len % ROW_CHUNK == 0
    assert (seq_len == ROW_CHUNK and not cache_len) or (n_seq == 1 and not state_mode)

    def const(*shape):
        return pl.BlockSpec(shape, lambda i: (0,) * len(shape), pipeline_mode=pl.Buffered(1))

    def of_layer(*shape):
        return pl.BlockSpec((None,) + shape, lambda i: (layer,) + (0,) * len(shape),
                            pipeline_mode=pl.Buffered(1))

    args = [x2d, mod_all]
    in_specs = [pl.BlockSpec((m, D_MODEL), lambda i: (i, 0)), of_layer(3, MOD_ROWS, D_MODEL)]
    if cache_len:
        args += [cache[0], cache[1]]
        in_specs += [pl.BlockSpec((None, None, cache_len, KV_LORA), lambda i: (i, layer, 0, 0)),
                     pl.BlockSpec((None, None, ROPE_DIM, cache_len), lambda i: (i, layer, 0, 0))]
    if has_rope:
        args += [rope]
        in_specs += [const(seq_len, ROPE_PAD)]
    if state_mode == "stack":
        args += list(prev_state)
        in_specs += [pl.BlockSpec((n_seq, seq_len, KV_LORA), lambda i: (i, 0, 0)),
                     pl.BlockSpec((n_seq, ROPE_DIM, seq_len), lambda i: (i, 0, 0))]
    args += [wts["g_norm"], wts["w_in"], wts["g_q"], wts["w_uq"], wts["g_kv"], wts["w_ukv"], wts["pool_w"],
             wts["pool_s"], wts["conv_w"], wts["w_out"], wts["g_final"]]
    in_specs += [const(DEPTH, D_MODEL), of_layer(D_MODEL, _C_END), const(DEPTH, Q_LORA),
                 of_layer(Q_LORA, MLA_HEADS * HEAD_PAD), const(DEPTH, KV_LORA),
                 of_layer(KV_LORA, MLA_HEADS * KV_HEAD), of_layer(POOL_WIDTH, POOL_WIDTH),
                 const(DEPTH, POOL_WIDTH), const(3, DEPTH, CONV_WIDTH), of_layer(MIX_WIDTH, D_MODEL),
                 const(1, D_MODEL)]

    out_shape = [jax.ShapeDtypeStruct((rows, D_MODEL), _F32)]
    out_specs = [pl.BlockSpec((m, D_MODEL), lambda i: (i, 0))]
    if state_mode == "own":
        out_shape += [jax.ShapeDtypeStruct((n_all, seq_len, KV_LORA), _F32),
                      jax.ShapeDtypeStruct((n_all, ROPE_DIM, seq_len), _F32)]
        out_specs += [pl.BlockSpec((n_seq, seq_len, KV_LORA), lambda i: (i, 0, 0)),
                      pl.BlockSpec((n_seq, ROPE_DIM, seq_len), lambda i: (i, 0, 0))]
    elif state_mode == "stack":
        out_shape += [jax.ShapeDtypeStruct((n_all, DEPTH, seq_len, KV_LORA), _F32),
                      jax.ShapeDtypeStruct((n_all, DEPTH, ROPE_DIM, seq_len), _F32)]
        out_specs += [pl.BlockSpec((n_seq, DEPTH, seq_len, KV_LORA), lambda i: (i, 0, 0, 0)),
                      pl.BlockSpec((n_seq, DEPTH, ROPE_DIM, seq_len), lambda i: (i, 0, 0, 0))]

    kern = functools.partial(_layer_kernel, layer=layer, n_seq=n_seq, seq_len=seq_len, cache_len=cache_len,
                             has_rope=has_rope, final_norm=final_norm, state_mode=state_mode)
    return pl.pallas_call(
        kern,
        grid=(n_steps,),
        in_specs=in_specs,
        out_specs=out_specs,
        out_shape=out_shape,
        scratch_shapes=[
            pltpu.VMEM((m, D_MODEL), _BF),
            pltpu.VMEM((MLA_HEADS, m, HEAD_PAD), _BF),
            pltpu.VMEM((MLA_HEADS, n_seq * keys, HEAD_PAD), _BF),
            pltpu.VMEM((MLA_HEADS, n_seq * keys, V_DIM), _BF),
            pltpu.VMEM((m + 2 * HALO, POOL_WIDTH), _F32),
            pltpu.VMEM((m + 2 * HALO, CONV_WIDTH), _F32),
            pltpu.VMEM((m, MIX_WIDTH), _BF),
        ],
        compiler_params=pltpu.CompilerParams(dimension_semantics=("arbitrary",),
                                             vmem_limit_bytes=VMEM_LIMIT_BYTES),
        name="mixer_layer_latent" if has_rope else "mixer_layer_context",
    )(*args)


def _rope_table(seq_len):
    rows = seq_len // GRID_W
    row = np.repeat(np.arange(rows), GRID_W).astype(np.float32)
    col = np.tile(np.arange(GRID_W), rows).astype(np.float32)
    inv = (1.0 / (np.float32(ROPE_BASE) ** (np.arange(0, AXIS_DIM, 2, dtype=np.float32) / np.float32(AXIS_DIM))))
    inv = inv.astype(np.float32)
    ang = np.concatenate([row[:, None] * inv, col[:, None] * inv], axis=-1).astype(np.float64)
    cos, sin = np.cos(ang).astype(np.float32), np.sin(ang).astype(np.float32)
    return jnp.asarray(np.concatenate([cos, cos, sin, sin], axis=-1))


def kernel(x_prompt, x_sample, cache_ckv, cache_krope, c, c_ctx, w_mod, b_mod, g_norm, w_in, g_q, w_uq,
           g_kv, w_ukv, pool_w, pool_s, conv_w, w_out, g_final):
    batch, seq, _ = x_prompt.shape
    dec_batch, dec_seq, _ = x_sample.shape
    assert 1 + dec_batch <= MOD_ROWS and dec_seq == ROWS_PER_STEP and ROWS_PER_STEP % seq == 0 and DEPTH == 2

    c_all = jnp.concatenate([c_ctx[None], c, jnp.zeros((MOD_ROWS - 1 - dec_batch, D_MODEL), _F32)], axis=0)
    mod_all = _modulation(c_all, w_mod, b_mod)
    w_in_r, w_out_r, w_uq_r, w_ukv_r, pool_w_r = _prep_weights(
        jnp.transpose(w_in, (0, 2, 1)), w_out, w_uq, w_ukv, pool_w)
    wts = {
        "g_norm": g_norm, "w_in": w_in_r, "g_q": g_q, "w_uq": w_uq_r, "g_kv": g_kv, "w_ukv": w_ukv_r,
        "pool_w": pool_w_r, "pool_s": pool_s, "conv_w": jnp.transpose(conv_w, (1, 0, 2)), "w_out": w_out_r,
        "g_final": g_final.reshape(1, D_MODEL),
    }

    xp = x_prompt.reshape(batch * seq, D_MODEL)
    xp, ckv0, kr0 = _mixer_layer(xp, mod_all, 0, wts, seq_len=seq, final_norm=False, state_mode="own")
    xp, state_ckv, state_krope_t = _mixer_layer(xp, mod_all, 1, wts, seq_len=seq, prev_state=(ckv0, kr0),
                                                final_norm=True, state_mode="stack")
    y_prompt = xp.reshape(batch, seq, D_MODEL)
    state_krope = jnp.transpose(state_krope_t, (0, 1, 3, 2))

    rope = _rope_table(dec_seq)
    xs = x_sample.reshape(dec_batch * dec_seq, D_MODEL)
    cache = (cache_ckv, jnp.transpose(cache_krope, (0, 1, 3, 2)))
    for l in range(DEPTH):
        (xs,) = _mixer_layer(xs, mod_all, l, wts, seq_len=dec_seq, cache=cache, rope=rope,
                             final_norm=(l == DEPTH - 1), state_mode=None)
    y_sample = xs.reshape(dec_batch, dec_seq, D_MODEL)
    return (y_prompt, y_sample, state_ckv, state_krope)
```

```python
import functools

import numpy as np
import jax
import jax.numpy as jnp
from jax import lax
from jax.experimental import pallas as pl
from jax.experimental.pallas import tpu as pltpu

D_MODEL = 1024
DEPTH = 2
GRID_W = 64
MLA_HEADS = 4
NOPE_DIM = 128
ROPE_DIM = 64
V_DIM = 128
QK_DIM = NOPE_DIM + ROPE_DIM
MLA_WIDTH = MLA_HEADS * V_DIM
Q_LORA = 384
KV_LORA = 256
POOL_GROUPS = 4
POOL_GROUP_DIM = 64
POOL_WIDTH = POOL_GROUPS * POOL_GROUP_DIM
CONV_WIDTH = 256
MIX_WIDTH = MLA_WIDTH + POOL_WIDTH + CONV_WIDTH
ROPE_BASE = 10000.0
AXIS_DIM = ROPE_DIM // 2
ATTN_SCALE = QK_DIM ** -0.5
EPS = 1e-6

_SPLITS = (Q_LORA, KV_LORA, ROPE_DIM, MLA_WIDTH, POOL_WIDTH, POOL_WIDTH,
           CONV_WIDTH, CONV_WIDTH, CONV_WIDTH, CONV_WIDTH)
_OFFS = [sum(_SPLITS[:i]) for i in range(len(_SPLITS) + 1)]
IN_WIDTH = _OFFS[-1]

V7X_LANES = 128
V7X_SUBLANES = 8
V7X_MXU_DEPTH = 256
HEAD_PAD = V7X_MXU_DEPTH
ROPE_PAD = HEAD_PAD - NOPE_DIM
KV_HEAD = NOPE_DIM + V_DIM

_C_CQ = 0
_C_CKV = _C_CQ + Q_LORA
_C_KR = _C_CKV + KV_LORA
_C_GMLA = _C_KR + ROPE_PAD
_C_GPOOL = _C_GMLA + MLA_WIDTH
_C_PX = _C_GPOOL + POOL_WIDTH
_C_CC = _C_PX + POOL_WIDTH
_C_CB = _C_CC + 2 * CONV_WIDTH
_C_END = _C_CB + 2 * CONV_WIDTH

MOD_ROWS = 8
ROWS_PER_STEP = 1024
ROW_CHUNK = 256
HALO = V7X_SUBLANES
PREP_COLS = 256
VMEM_LIMIT_BYTES = 56 * 1024 * 1024

_BF = jnp.bfloat16
_F32 = jnp.float32


def _dot(a, b):
    return jnp.dot(a, b, preferred_element_type=_F32)


def _rms(x, g):
    return x * lax.rsqrt(jnp.mean(x * x, axis=-1, keepdims=True) + EPS) * g


def _silu(x):
    return x * jax.nn.sigmoid(x)


def _aligned(v, k):
    return v if isinstance(v, int) else pl.multiple_of(v, k)


def _for_chunks(n, body):
    for c in range(n):
        body(c, 0)


def _mod_kernel(c_ref, w_ref, b_ref, o_ref):
    s = _silu(c_ref[...])
    bias = b_ref[pl.ds(pl.program_id(0), 1), :]
    o_ref[...] = _dot(s.astype(_BF), w_ref[...].astype(_BF)) + bias


def _modulation(c_all, w_mod, b_mod):
    return pl.pallas_call(
        _mod_kernel,
        grid=(DEPTH, 3),
        in_specs=[
            pl.BlockSpec((MOD_ROWS, D_MODEL), lambda l, j: (0, 0)),
            pl.BlockSpec((None, D_MODEL, D_MODEL), lambda l, j: (l, 0, j)),
            pl.BlockSpec((DEPTH, D_MODEL), lambda l, j: (0, j)),
        ],
        out_specs=pl.BlockSpec((None, None, MOD_ROWS, D_MODEL), lambda l, j: (l, j, 0, 0)),
        out_shape=jax.ShapeDtypeStruct((DEPTH, 3, MOD_ROWS, D_MODEL), _F32),
        name="modulation",
    )(c_all, w_mod, b_mod)


def _prep_kernel(wint_ref, wout_ref, wuq_ref, wukv_ref, poolw_ref,
                 win_o, wout_o, wuq_o, wukv_o, poolw_o):
    def put(dst, src, width):
        for c in range(0, width, PREP_COLS):
            n = min(PREP_COLS, width - c)
            win_o[:, dst + c:dst + c + n] = wint_ref[src + c:src + c + n, :].T.astype(_BF)

    put(_C_CQ, _OFFS[0], Q_LORA + KV_LORA)
    kr = wint_ref[_OFFS[2]:_OFFS[3], :]
    krb = jnp.concatenate([kr, -kr[AXIS_DIM:], kr[:AXIS_DIM]], axis=0)
    win_o[:, _C_KR:_C_GMLA] = krb.T.astype(_BF)
    put(_C_GMLA, _OFFS[3], MLA_WIDTH)
    put(_C_GPOOL, _OFFS[5], POOL_WIDTH)
    put(_C_PX, _OFFS[4], POOL_WIDTH)
    put(_C_CC, _OFFS[7], 2 * CONV_WIDTH)
    put(_C_CB, _OFFS[6], CONV_WIDTH)
    put(_C_CB + CONV_WIDTH, _OFFS[9], CONV_WIDTH)
    wout_o[...] = wout_ref[...].astype(_BF)

    for hd in range(MLA_HEADS):
        s0, d0 = hd * QK_DIM, hd * HEAD_PAD
        r0 = s0 + NOPE_DIM
        wuq_o[:, d0:d0 + QK_DIM] = wuq_ref[:, s0:s0 + QK_DIM].astype(_BF)
        wuq_o[:, d0 + QK_DIM:d0 + QK_DIM + AXIS_DIM] = (-wuq_ref[:, r0 + AXIS_DIM:r0 + ROPE_DIM]).astype(_BF)
        wuq_o[:, d0 + QK_DIM + AXIS_DIM:d0 + HEAD_PAD] = wuq_ref[:, r0:r0 + AXIS_DIM].astype(_BF)
    wukv_o[...] = wukv_ref[...].astype(_BF)
    poolw_o[...] = jnp.zeros((POOL_WIDTH, POOL_WIDTH), _BF)
    for g in range(POOL_GROUPS):
        sl = slice(g * POOL_GROUP_DIM, (g + 1) * POOL_GROUP_DIM)
        poolw_o[sl, sl] = poolw_ref[g].astype(_BF)


def _prep_weights(w_in_t, w_out, w_uq, w_ukv, pool_w):
    per_layer = lambda *shape: pl.BlockSpec((None,) + shape, lambda l: (l,) + (0,) * len(shape))
    return pl.pallas_call(
        _prep_kernel,
        grid=(DEPTH,),
        in_specs=[
            per_layer(IN_WIDTH, D_MODEL),
            per_layer(MIX_WIDTH, D_MODEL),
            per_layer(Q_LORA, MLA_HEADS * QK_DIM),
            per_layer(KV_LORA, MLA_HEADS * KV_HEAD),
            per_layer(POOL_GROUPS, POOL_GROUP_DIM, POOL_GROUP_DIM),
        ],
        out_specs=[
            per_layer(D_MODEL, _C_END),
            per_layer(MIX_WIDTH, D_MODEL),
            per_layer(Q_LORA, MLA_HEADS * HEAD_PAD),
            per_layer(KV_LORA, MLA_HEADS * KV_HEAD),
            per_layer(POOL_WIDTH, POOL_WIDTH),
        ],
        out_shape=[
            jax.ShapeDtypeStruct((DEPTH, D_MODEL, _C_END), _BF),
            jax.ShapeDtypeStruct((DEPTH, MIX_WIDTH, D_MODEL), _BF),
            jax.ShapeDtypeStruct((DEPTH, Q_LORA, MLA_HEADS * HEAD_PAD), _BF),
            jax.ShapeDtypeStruct((DEPTH, KV_LORA, MLA_HEADS * KV_HEAD), _BF),
            jax.ShapeDtypeStruct((DEPTH, POOL_WIDTH, POOL_WIDTH), _BF),
        ],
        compiler_params=pltpu.CompilerParams(dimension_semantics=("arbitrary",),
                                             vmem_limit_bytes=VMEM_LIMIT_BYTES),
        name="weight_prep",
    )(w_in_t, w_out, w_uq, w_ukv, pool_w)


def _pool_mix(win, t, seq_len):
    rows = win.shape[0] - 2 * HALO
    lane = lax.broadcasted_iota(jnp.int32, (1, POOL_WIDTH), 1)
    half = jnp.left_shift(1, lane >> 6)
    px = win[HALO:HALO + rows]
    acc = px
    max_half = 1 << (POOL_GROUPS - 1)
    for d in range(-max_half, max_half):
        if d == 0:
            continue
        row_ok = (t + d >= 0) & (t + d < seq_len)
        lane_ok = (d >= -half) & (d < half)
        acc = acc + jnp.where(row_ok & lane_ok, win[HALO + d:HALO + d + rows], 0.0)
    cnt = jnp.minimum(t + half, seq_len) - jnp.maximum(t - half, 0)
    return acc / cnt.astype(_F32) - px


def _short_conv(win, w, t, seq_len):
    rows = win.shape[0] - 2 * HALO
    prev = jnp.where(t >= 1, win[HALO - 1:HALO - 1 + rows], 0.0)
    nxt = jnp.where(t <= seq_len - 2, win[HALO + 1:HALO + 1 + rows], 0.0)
    return prev * w[0] + win[HALO:HALO + rows] * w[1] + nxt * w[2]


def _layer_kernel(*refs, layer, n_seq, seq_len, cache_len, has_rope, final_norm, state_mode):
    it = iter(refs)
    x_ref, mod_ref = next(it), next(it)
    if cache_len:
        cckv_ref, ckr_ref = next(it), next(it)
    if has_rope:
        cs_ref = next(it)
    if state_mode == "stack":
        ckv_prev_ref, kr_prev_ref = next(it), next(it)
    (gn_ref, win_ref, gq_ref, wuq_ref, gkv_ref, wukv_ref, poolw_ref, pools_ref, convw_ref, wout_ref,
     gfin_ref) = (next(it) for _ in range(11))
    y_ref = next(it)
    if state_mode:
        ckv_out_ref, kr_out_ref = next(it), next(it)
    h_s, q_s, k_s, v_s, px_s, prod_s, mixed_s = (next(it) for _ in range(7))

    rc = ROW_CHUNK
    m = n_seq * seq_len
    keys = cache_len + seq_len
    pair = 2 * KV_HEAD
    row = slice(layer, layer + 1)
    lane = lax.broadcasted_iota(jnp.int32, (1, ROPE_PAD), 1)

    mrow = (1 + pl.program_id(0)) if has_rope else 0
    shift, scale, gate = (mod_ref[k, pl.ds(mrow, 1), :] for k in range(3))

    zeros_halo = jnp.zeros((HALO, POOL_WIDTH), _F32)
    for buf in (px_s, prod_s):
        buf[0:HALO, :] = zeros_halo
        buf[HALO + m:2 * HALO + m, :] = zeros_halo
    if state_mode == "stack":
        ckv_out_ref[:, 0] = ckv_prev_ref[...]
        kr_out_ref[:, 0] = kr_prev_ref[...]

    def rotate(v, r):
        v = v * cs_ref[pl.ds(r, rc), :]
        return v + pltpu.roll(v, ROPE_DIM, axis=1)

    def put_kv(kv2, kr_bf, hp, rows_k):
        for j in range(2):
            hd = 2 * hp + j
            k0 = j * KV_HEAD
            k_s[hd, rows_k, :NOPE_DIM] = kv2[:, k0:k0 + NOPE_DIM].astype(_BF)
            k_s[hd, rows_k, NOPE_DIM:] = kr_bf
            v_s[hd, rows_k, :] = kv2[:, k0 + NOPE_DIM:k0 + KV_HEAD].astype(_BF)

    if cache_len:
        ckr_t = ckr_ref[...]
        ckr_bf = jnp.concatenate([ckr_t, jnp.zeros_like(ckr_t)], axis=0).T.astype(_BF)
        for cc in range(cache_len // rc):
            rows_k = slice(cc * rc, (cc + 1) * rc)
            ck = cckv_ref[rows_k, :].astype(_BF)
            for hp in range(MLA_HEADS // 2):
                put_kv(_dot(ck, wukv_ref[:, hp * pair:(hp + 1) * pair]), ckr_bf[rows_k, :], hp, rows_k)

    def project(c, carry):
        r = _aligned(c * rc, rc)
        rows = pl.ds(r, rc)
        h = (_rms(x_ref[rows, :], gn_ref[row, :]) * (1.0 + scale) + shift).astype(_BF)
        h_s[rows, :] = h
        a = _dot(h, win_ref[:, _C_CQ:_C_GMLA])
        cq = _rms(a[:, :Q_LORA], gq_ref[row, :]).astype(_BF)
        ckv = _rms(a[:, _C_CKV:_C_KR], gkv_ref[row, :])
        kr = a[:, _C_KR:]
        if has_rope:
            kr = rotate(kr, r)
        kr = jnp.where(lane < ROPE_DIM, kr, 0.0)
        if state_mode:
            own = () if state_mode == "own" else (layer,)
            ckv_out_ref[(c,) + own] = ckv
            kr_out_ref[(c,) + own] = kr.T[:ROPE_DIM, :]
        ckv_bf = ckv.astype(_BF)
        kr_bf = kr.astype(_BF)
        rows_k = pl.ds(_aligned(cache_len + r, rc), rc)
        for hp in range(MLA_HEADS // 2):
            put_kv(_dot(ckv_bf, wukv_ref[:, hp * pair:(hp + 1) * pair]), kr_bf, hp, rows_k)
            q2 = _dot(cq, wuq_ref[:, hp * 2 * HEAD_PAD:(hp + 1) * 2 * HEAD_PAD]) * ATTN_SCALE
            for j in range(2):
                hd = 2 * hp + j
                c0 = j * HEAD_PAD
                q_s[hd, rows, :NOPE_DIM] = q2[:, c0:c0 + NOPE_DIM].astype(_BF)
                qr = q2[:, c0 + NOPE_DIM:c0 + HEAD_PAD]
                if has_rope:
                    qr = rotate(qr, r)
                q_s[hd, rows, NOPE_DIM:] = qr.astype(_BF)
        pe = _dot(h, win_ref[:, _C_PX:_C_CB])
        rows_h = pl.ds(_aligned(HALO + r, HALO), rc)
        px_s[rows_h, :] = pe[:, :POOL_WIDTH]
        prod_s[rows_h, :] = pe[:, POOL_WIDTH:POOL_WIDTH + CONV_WIDTH] * pe[:, POOL_WIDTH + CONV_WIDTH:]
        return carry

    _for_chunks(m // rc, project)

    def mix(c, carry):
        r = _aligned(c * rc, rc)
        rows = pl.ds(r, rc)
        h = h_s[rows, :]
        t = (r + lax.broadcasted_iota(jnp.int32, (rc, 1), 0)) & (seq_len - 1)
        rows_k = pl.ds(0 if n_seq == 1 else r, keys)
        for hp in range(MLA_HEADS // 2):
            g2 = _silu(_dot(h, win_ref[:, _C_GMLA + hp * 2 * V_DIM:_C_GMLA + (hp + 1) * 2 * V_DIM]))
            for j in range(2):
                hd = 2 * hp + j
                sc = lax.dot_general(q_s[hd, rows, :], k_s[hd, rows_k, :], (((1,), (1,)), ((), ())),
                                     preferred_element_type=_F32)
                p = jnp.exp(sc - jnp.max(sc, axis=-1, keepdims=True))
                o = _dot(p.astype(_BF), v_s[hd, rows_k, :]) / jnp.sum(p, axis=-1, keepdims=True)
                mixed_s[rows, hd * V_DIM:(hd + 1) * V_DIM] = (g2[:, j * V_DIM:(j + 1) * V_DIM] * o).astype(_BF)

        rows_w = pl.ds(r, rc + 2 * HALO)
        pooled = _pool_mix(px_s[rows_w, :], t, seq_len)
        pool = _dot(pooled.astype(_BF), poolw_ref[...]) * pools_ref[row, :]
        gp = _silu(_dot(h, win_ref[:, _C_GPOOL:_C_PX]))
        mixed_s[rows, MLA_WIDTH:MLA_WIDTH + POOL_WIDTH] = (gp * pool).astype(_BF)

        convw = [convw_ref[k, row, :] for k in range(3)]
        conv = _short_conv(prod_s[rows_w, :], convw, t, seq_len)
        e2 = _dot(h, win_ref[:, _C_CB:_C_END])
        mixed_s[rows, MLA_WIDTH + POOL_WIDTH:] = (_silu(e2[:, CONV_WIDTH:]) * (e2[:, :CONV_WIDTH] * conv)).astype(_BF)

        y = x_ref[rows, :] + gate * _dot(mixed_s[rows, :], wout_ref[...])
        if final_norm:
            y = _rms(y, gfin_ref[...])
        y_ref[rows, :] = y
        return carry

    _for_chunks(m // rc, mix)


def _mixer_layer(x2d, mod_all, layer, wts, *, seq_len, cache=None, rope=None, prev_state=None,
                 final_norm, state_mode):
    rows = x2d.shape[0]
    m = ROWS_PER_STEP
    n_seq = m // seq_len
    n_steps = rows // m
    n_all = rows // seq_len
    cache_len = 0 if cache is None else cache[0].shape[2]
    keys = cache_len + seq_len
    has_rope = rope is not None
    assert seq_len % ROW_CHUNK == 0 and cache_len % ROW_CHUNK == 0
    assert (seq_len == ROW_CHUNK and not cache_len) or (n_seq == 1 and not state_mode)

    def const(*shape):
        return pl.BlockSpec(shape, lambda i: (0,) * len(shape), pipeline_mode=pl.Buffered(1))

    def of_layer(*shape):
        return pl.BlockSpec((None,) + shape, lambda i: (layer,) + (0,) * len(shape),
                            pipeline_mode=pl.Buffered(1))

    args = [x2d, mod_all]
    in_specs = [pl.BlockSpec((m, D_MODEL), lambda i: (i, 0)), of_layer(3, MOD_ROWS, D_MODEL)]
    if cache_len:
        args += [cache[0], cache[1]]
        in_specs += [pl.BlockSpec((None, None, cache_len, KV_LORA), lambda i: (i, layer, 0, 0)),
                     pl.BlockSpec((None, None, ROPE_DIM, cache_len), lambda i: (i, layer, 0, 0))]
    if has_rope:
        args += [rope]
        in_specs += [const(seq_len, ROPE_PAD)]
    if state_mode == "stack":
        args += list(prev_state)
        in_specs += [pl.BlockSpec((n_seq, seq_len, KV_LORA), lambda i: (i, 0, 0)),
                     pl.BlockSpec((n_seq, ROPE_DIM, seq_len), lambda i: (i, 0, 0))]
    args += [wts["g_norm"], wts["w_in"], wts["g_q"], wts["w_uq"], wts["g_kv"], wts["w_ukv"], wts["pool_w"],
             wts["pool_s"], wts["conv_w"], wts["w_out"], wts["g_final"]]
    in_specs += [const(DEPTH, D_MODEL), of_layer(D_MODEL, _C_END), const(DEPTH, Q_LORA),
                 of_layer(Q_LORA, MLA_HEADS * HEAD_PAD), const(DEPTH, KV_LORA),
                 of_layer(KV_LORA, MLA_HEADS * KV_HEAD), of_layer(POOL_WIDTH, POOL_WIDTH),
                 const(DEPTH, POOL_WIDTH), const(3, DEPTH, CONV_WIDTH), of_layer(MIX_WIDTH, D_MODEL),
                 const(1, D_MODEL)]

    out_shape = [jax.ShapeDtypeStruct((rows, D_MODEL), _F32)]
    out_specs = [pl.BlockSpec((m, D_MODEL), lambda i: (i, 0))]
    if state_mode == "own":
        out_shape += [jax.ShapeDtypeStruct((n_all, seq_len, KV_LORA), _F32),
                      jax.ShapeDtypeStruct((n_all, ROPE_DIM, seq_len), _F32)]
        out_specs += [pl.BlockSpec((n_seq, seq_len, KV_LORA), lambda i: (i, 0, 0)),
                      pl.BlockSpec((n_seq, ROPE_DIM, seq_len), lambda i: (i, 0, 0))]
    elif state_mode == "stack":
        out_shape += [jax.ShapeDtypeStruct((n_all, DEPTH, seq_len, KV_LORA), _F32),
                      jax.ShapeDtypeStruct((n_all, DEPTH, ROPE_DIM, seq_len), _F32)]
        out_specs += [pl.BlockSpec((n_seq, DEPTH, seq_len, KV_LORA), lambda i: (i, 0, 0, 0)),
                      pl.BlockSpec((n_seq, DEPTH, ROPE_DIM, seq_len), lambda i: (i, 0, 0, 0))]

    kern = functools.partial(_layer_kernel, layer=layer, n_seq=n_seq, seq_len=seq_len, cache_len=cache_len,
                             has_rope=has_rope, final_norm=final_norm, state_mode=state_mode)
    return pl.pallas_call(
        kern,
        grid=(n_steps,),
        in_specs=in_specs,
        out_specs=out_specs,
        out_shape=out_shape,
        scratch_shapes=[
            pltpu.VMEM((m, D_MODEL), _BF),
            pltpu.VMEM((MLA_HEADS, m, HEAD_PAD), _BF),
            pltpu.VMEM((MLA_HEADS, n_seq * keys, HEAD_PAD), _BF),
            pltpu.VMEM((MLA_HEADS, n_seq * keys, V_DIM), _BF),
            pltpu.VMEM((m + 2 * HALO, POOL_WIDTH), _F32),
            pltpu.VMEM((m + 2 * HALO, CONV_WIDTH), _F32),
            pltpu.VMEM((m, MIX_WIDTH), _BF),
        ],
        compiler_params=pltpu.CompilerParams(dimension_semantics=("arbitrary",),
                                             vmem_limit_bytes=VMEM_LIMIT_BYTES),
        name="mixer_layer_latent" if has_rope else "mixer_layer_context",
    )(*args)


def _rope_table(seq_len):
    rows = seq_len // GRID_W
    row = np.repeat(np.arange(rows), GRID_W).astype(np.float32)
    col = np.tile(np.arange(GRID_W), rows).astype(np.float32)
    inv = (1.0 / (np.float32(ROPE_BASE) ** (np.arange(0, AXIS_DIM, 2, dtype=np.float32) / np.float32(AXIS_DIM))))
    inv = inv.astype(np.float32)
    ang = np.concatenate([row[:, None] * inv, col[:, None] * inv], axis=-1).astype(np.float64)
    cos, sin = np.cos(ang).astype(np.float32), np.sin(ang).astype(np.float32)
    return jnp.asarray(np.concatenate([cos, cos, sin, sin], axis=-1))


def kernel(x_prompt, x_sample, cache_ckv, cache_krope, c, c_ctx, w_mod, b_mod, g_norm, w_in, g_q, w_uq,
           g_kv, w_ukv, pool_w, pool_s, conv_w, w_out, g_final):
    batch, seq, _ = x_prompt.shape
    dec_batch, dec_seq, _ = x_sample.shape
    assert 1 + dec_batch <= MOD_ROWS and dec_seq == ROWS_PER_STEP and ROWS_PER_STEP % seq == 0 and DEPTH == 2

    c_all = jnp.concatenate([c_ctx[None], c, jnp.zeros((MOD_ROWS - 1 - dec_batch, D_MODEL), _F32)], axis=0)
    mod_all = _modulation(c_all, w_mod, b_mod)
    w_in_r, w_out_r, w_uq_r, w_ukv_r, pool_w_r = _prep_weights(
        jnp.transpose(w_in, (0, 2, 1)), w_out, w_uq, w_ukv, pool_w)
    wts = {
        "g_norm": g_norm, "w_in": w_in_r, "g_q": g_q, "w_uq": w_uq_r, "g_kv": g_kv, "w_ukv": w_ukv_r,
        "pool_w": pool_w_r, "pool_s": pool_s, "conv_w": jnp.transpose(conv_w, (1, 0, 2)), "w_out": w_out_r,
        "g_final": g_final.reshape(1, D_MODEL),
    }

    xp = x_prompt.reshape(batch * seq, D_MODEL)
    xp, ckv0, kr0 = _mixer_layer(xp, mod_all, 0, wts, seq_len=seq, final_norm=False, state_mode="own")
    xp, state_ckv, state_krope_t = _mixer_layer(xp, mod_all, 1, wts, seq_len=seq, prev_state=(ckv0, kr0),
                                                final_norm=True, state_mode="stack")
    y_prompt = xp.reshape(batch, seq, D_MODEL)
    state_krope = jnp.transpose(state_krope_t, (0, 1, 3, 2))

    rope = _rope_table(dec_seq)
    xs = x_sample.reshape(dec_batch * dec_seq, D_MODEL)
    cache = (cache_ckv, jnp.transpose(cache_krope, (0, 1, 3, 2)))
    for l in range(DEPTH):
        (xs,) = _mixer_layer(xs, mod_all, l, wts, seq_len=dec_seq, cache=cache, rope=rope,
                             final_norm=(l == DEPTH - 1), state_mode=None)
    y_sample = xs.reshape(dec_batch, dec_seq, D_MODEL)
    return (y_prompt, y_sample, state_ckv, state_krope)
```

```python
import functools

import numpy as np
import jax
import jax.numpy as jnp
from jax import lax
from jax.experimental import pallas as pl
from jax.experimental.pallas import tpu as pltpu

D_MODEL = 1024
DEPTH = 2
GRID_W = 64
MLA_HEADS = 4
NOPE_DIM = 128
ROPE_DIM = 64
V_DIM = 128
QK_DIM = NOPE_DIM + ROPE_DIM
MLA_WIDTH = MLA_HEADS * V_DIM
Q_LORA = 384
KV_LORA = 256
POOL_GROUPS = 4
POOL_GROUP_DIM = 64
POOL_WIDTH = POOL_GROUPS * POOL_GROUP_DIM
CONV_WIDTH = 256
MIX_WIDTH = MLA_WIDTH + POOL_WIDTH + CONV_WIDTH
ROPE_BASE = 10000.0
AXIS_DIM = ROPE_DIM // 2
ATTN_SCALE = QK_DIM ** -0.5
Q_SCALE = ATTN_SCALE * float(np.log2(np.e))
EPS = 1e-6

_SPLITS = (Q_LORA, KV_LORA, ROPE_DIM, MLA_WIDTH, POOL_WIDTH, POOL_WIDTH,
           CONV_WIDTH, CONV_WIDTH, CONV_WIDTH, CONV_WIDTH)
_OFFS = [sum(_SPLITS[:i]) for i in range(len(_SPLITS) + 1)]
IN_WIDTH = _OFFS[-1]

V7X_LANES = 128
V7X_SUBLANES = 8
V7X_MXU_DEPTH = 256
HEAD_PAD = V7X_MXU_DEPTH
ROPE_PAD = HEAD_PAD - NOPE_DIM
KV_HEAD = NOPE_DIM + V_DIM

_C_CQ = 0
_C_CKV = _C_CQ + Q_LORA
_C_KR = _C_CKV + KV_LORA
_C_GMLA = _C_KR + ROPE_PAD
_C_GPOOL = _C_GMLA + MLA_WIDTH
_C_PX = _C_GPOOL + POOL_WIDTH
_C_CC = _C_PX + POOL_WIDTH
_C_CB = _C_CC + 2 * CONV_WIDTH
_C_END = _C_CB + 2 * CONV_WIDTH

MOD_ROWS = 8
ROWS_PER_STEP = 1024
ROW_CHUNK = 256
HALO = V7X_SUBLANES
PREP_COLS = 256
VMEM_LIMIT_BYTES = 56 * 1024 * 1024

_BF = jnp.bfloat16
_F32 = jnp.float32


def _dot(a, b):
    return jnp.dot(a, b, preferred_element_type=_F32)


def _rms(x, g):
    return x * lax.rsqrt(jnp.mean(x * x, axis=-1, keepdims=True) + EPS) * g


def _silu(x):
    return x * jax.nn.sigmoid(x)


def _aligned(v, k):
    return v if isinstance(v, int) else pl.multiple_of(v, k)


def _for_chunks(n, body):
    for c in range(n):
        body(c, 0)


def _mod_kernel(c_ref, w_ref, b_ref, o_ref):
    s = _silu(c_ref[...])
    bias = b_ref[pl.ds(pl.program_id(0), 1), :]
    o_ref[...] = _dot(s.astype(_BF), w_ref[...].astype(_BF)) + bias


def _modulation(c_all, w_mod, b_mod):
    return pl.pallas_call(
        _mod_kernel,
        grid=(DEPTH, 3),
        in_specs=[
            pl.BlockSpec((MOD_ROWS, D_MODEL), lambda l, j: (0, 0)),
            pl.BlockSpec((None, D_MODEL, D_MODEL), lambda l, j: (l, 0, j)),
            pl.BlockSpec((DEPTH, D_MODEL), lambda l, j: (0, j)),
        ],
        out_specs=pl.BlockSpec((None, None, MOD_ROWS, D_MODEL), lambda l, j: (l, j, 0, 0)),
        out_shape=jax.ShapeDtypeStruct((DEPTH, 3, MOD_ROWS, D_MODEL), _F32),
        name="modulation",
    )(c_all, w_mod, b_mod)


def _prep_kernel(wint_ref, wout_ref, wuq_ref, wukv_ref, poolw_ref,
                 win_o, wout_o, wuq_o, wukt_o, wuv_o, poolw_o):
    def put(dst, src, width):
        for c in range(0, width, PREP_COLS):
            n = min(PREP_COLS, width - c)
            win_o[:, dst + c:dst + c + n] = wint_ref[src + c:src + c + n, :].T.astype(_BF)

    put(_C_CQ, _OFFS[0], Q_LORA + KV_LORA)
    kr = wint_ref[_OFFS[2]:_OFFS[3], :]
    krb = jnp.concatenate([kr, -kr[AXIS_DIM:], kr[:AXIS_DIM]], axis=0)
    win_o[:, _C_KR:_C_GMLA] = krb.T.astype(_BF)
    put(_C_GMLA, _OFFS[3], MLA_WIDTH)
    put(_C_GPOOL, _OFFS[5], POOL_WIDTH)
    put(_C_PX, _OFFS[4], POOL_WIDTH)
    put(_C_CC, _OFFS[7], 2 * CONV_WIDTH)
    put(_C_CB, _OFFS[6], CONV_WIDTH)
    put(_C_CB + CONV_WIDTH, _OFFS[9], CONV_WIDTH)
    wout_o[...] = wout_ref[...].astype(_BF)

    for hd in range(MLA_HEADS):
        s0, d0 = hd * QK_DIM, hd * HEAD_PAD
        r0 = s0 + NOPE_DIM
        wuq_o[:, d0:d0 + QK_DIM] = wuq_ref[:, s0:s0 + QK_DIM].astype(_BF)
        wuq_o[:, d0 + QK_DIM:d0 + QK_DIM + AXIS_DIM] = (-wuq_ref[:, r0 + AXIS_DIM:r0 + ROPE_DIM]).astype(_BF)
        wuq_o[:, d0 + QK_DIM + AXIS_DIM:d0 + HEAD_PAD] = wuq_ref[:, r0:r0 + AXIS_DIM].astype(_BF)
    for hd in range(MLA_HEADS):
        k0 = hd * KV_HEAD
        wukt_o[hd * NOPE_DIM:(hd + 1) * NOPE_DIM, :] = wukv_ref[:, k0:k0 + NOPE_DIM].T.astype(_BF)
        wuv_o[:, hd * V_DIM:(hd + 1) * V_DIM] = wukv_ref[:, k0 + NOPE_DIM:k0 + KV_HEAD].astype(_BF)
    poolw_o[...] = jnp.zeros((POOL_WIDTH, POOL_WIDTH), _BF)
    for g in range(POOL_GROUPS):
        sl = slice(g * POOL_GROUP_DIM, (g + 1) * POOL_GROUP_DIM)
        poolw_o[sl, sl] = poolw_ref[g].astype(_BF)


def _prep_weights(w_in_t, w_out, w_uq, w_ukv, pool_w):
    per_layer = lambda *shape: pl.BlockSpec((None,) + shape, lambda l: (l,) + (0,) * len(shape))
    return pl.pallas_call(
        _prep_kernel,
        grid=(DEPTH,),
        in_specs=[
            per_layer(IN_WIDTH, D_MODEL),
            per_layer(MIX_WIDTH, D_MODEL),
            per_layer(Q_LORA, MLA_HEADS * QK_DIM),
            per_layer(KV_LORA, MLA_HEADS * KV_HEAD),
            per_layer(POOL_GROUPS, POOL_GROUP_DIM, POOL_GROUP_DIM),
        ],
        out_specs=[
            per_layer(D_MODEL, _C_END),
            per_layer(MIX_WIDTH, D_MODEL),
            per_layer(Q_LORA, MLA_HEADS * HEAD_PAD),
            per_layer(MLA_HEADS * NOPE_DIM, KV_LORA),
            per_layer(KV_LORA, MLA_WIDTH),
            per_layer(POOL_WIDTH, POOL_WIDTH),
        ],
        out_shape=[
            jax.ShapeDtypeStruct((DEPTH, D_MODEL, _C_END), _BF),
            jax.ShapeDtypeStruct((DEPTH, MIX_WIDTH, D_MODEL), _BF),
            jax.ShapeDtypeStruct((DEPTH, Q_LORA, MLA_HEADS * HEAD_PAD), _BF),
            jax.ShapeDtypeStruct((DEPTH, MLA_HEADS * NOPE_DIM, KV_LORA), _BF),
            jax.ShapeDtypeStruct((DEPTH, KV_LORA, MLA_WIDTH), _BF),
            jax.ShapeDtypeStruct((DEPTH, POOL_WIDTH, POOL_WIDTH), _BF),
        ],
        compiler_params=pltpu.CompilerParams(dimension_semantics=("arbitrary",),
                                             vmem_limit_bytes=VMEM_LIMIT_BYTES),
        name="weight_prep",
    )(w_in_t, w_out, w_uq, w_ukv, pool_w)


def _pool_mix(win, rcnt):
    n = win.shape[0]
    rows = n - 2 * HALO
    lane = lax.broadcasted_iota(jnp.int32, (1, POOL_WIDTH), 1)
    sums = [win + pltpu.roll(win, 1, axis=0)]
    for k in (1, 2, 4):
        sums.append(pltpu.roll(sums[-1], k, axis=0) + pltpu.roll(sums[-1], n - k, axis=0))
    sel = sums[-1]
    for g in range(POOL_GROUPS - 2, -1, -1):
        sel = jnp.where(lane < (g + 1) * POOL_GROUP_DIM, sums[g], sel)
    return sel[HALO:HALO + rows] * rcnt - win[HALO:HALO + rows]


def _short_conv(win, w):
    rows = win.shape[0] - 2 * HALO
    return (win[HALO - 1:HALO - 1 + rows] * w[0] + win[HALO:HALO + rows] * w[1]
            + win[HALO + 1:HALO + 1 + rows] * w[2])


def _layer_kernel(*refs, layer, n_seq, seq_len, cache_len, has_rope, final_norm, state_mode):
    it = iter(refs)
    x_ref, mod_ref = next(it), next(it)
    if cache_len:
        cckv_ref, ckr_ref = next(it), next(it)
    if has_rope:
        cs_ref = next(it)
    if state_mode == "stack":
        ckv_prev_ref, kr_prev_ref = next(it), next(it)
    (rcnt_ref, gn_ref, win_ref, gq_ref, wuq_ref, gkv_ref, wukt_ref, wuv_ref, poolw_ref, pools_ref,
     convw_ref, wout_ref, gfin_ref) = (next(it) for _ in range(13))
    y_ref = next(it)
    if state_mode:
        ckv_out_ref, kr_out_ref = next(it), next(it)
    h_s, q_s, kt_s, v_s, px_s, prod_s, mixed_s = (next(it) for _ in range(7))

    rc = ROW_CHUNK
    m = n_seq * seq_len
    keys = cache_len + seq_len
    padded = seq_len + 2 * HALO
    row = slice(layer, layer + 1)
    lane = lax.broadcasted_iota(jnp.int32, (1, ROPE_PAD), 1)

    mrow = (1 + pl.program_id(0)) if has_rope else 0
    shift, scale, gate = (mod_ref[k, pl.ds(mrow, 1), :] for k in range(3))

    zeros_halo = jnp.zeros((HALO, POOL_WIDTH), _F32)
    for buf in (px_s, prod_s):
        for s in range(n_seq):
            buf[s * padded:s * padded + HALO, :] = zeros_halo
            buf[(s + 1) * padded - HALO:(s + 1) * padded, :] = zeros_halo
    ones_col = jnp.where(lax.broadcasted_iota(jnp.int32, (n_seq * keys, V_DIM), 1) == 0, 1.0, 0.0).astype(_BF)
    for hd in range(MLA_HEADS):
        v_s[hd, :, V_DIM:] = ones_col
    if state_mode == "stack":
        ckv_out_ref[:, 0] = ckv_prev_ref[...]
        kr_out_ref[:, 0] = kr_prev_ref[...]

    def rotate(v, r):
        v = v * cs_ref[pl.ds(r, rc), :]
        return v + pltpu.roll(v, ROPE_DIM, axis=1)

    def put_kv(ckv, kr_t, rows_k):
        knt = _dot(wukt_ref[...], ckv.T.astype(_BF))
        v4 = _dot(ckv.astype(_BF), wuv_ref[...])
        kr_bf = kr_t.astype(_BF)
        for hd in range(MLA_HEADS):
            kt_s[hd, :NOPE_DIM, rows_k] = knt[hd * NOPE_DIM:(hd + 1) * NOPE_DIM, :].astype(_BF)
            kt_s[hd, NOPE_DIM:, rows_k] = kr_bf
            v_s[hd, rows_k, :V_DIM] = v4[:, hd * V_DIM:(hd + 1) * V_DIM].astype(_BF)

    if cache_len:
        ckr_t = ckr_ref[...]
        ckr_pad = jnp.concatenate([ckr_t, jnp.zeros_like(ckr_t)], axis=0)
        for cc in range(cache_len // rc):
            rows_k = slice(cc * rc, (cc + 1) * rc)
            put_kv(cckv_ref[rows_k, :], ckr_pad[:, rows_k], rows_k)

    def project(c, carry):
        r = _aligned(c * rc, rc)
        rows = pl.ds(r, rc)
        h = (_rms(x_ref[rows, :], gn_ref[row, :]) * (1.0 + scale) + shift).astype(_BF)
        h_s[rows, :] = h
        a = _dot(h, win_ref[:, _C_CQ:_C_GMLA])
        cq = _rms(a[:, :Q_LORA], gq_ref[row, :]).astype(_BF)
        ckv = _rms(a[:, _C_CKV:_C_KR], gkv_ref[row, :])
        kr = a[:, _C_KR:]
        if has_rope:
            kr = rotate(kr, r)
        kr_t = jnp.where(lane < ROPE_DIM, kr, 0.0).T
        if state_mode:
            own = () if state_mode == "own" else (layer,)
            ckv_out_ref[(c,) + own] = ckv
            kr_out_ref[(c,) + own] = kr_t[:ROPE_DIM, :]
        put_kv(ckv, kr_t, pl.ds(_aligned(cache_len + r, rc), rc))
        for hp in range(MLA_HEADS // 2):
            q2 = _dot(cq, wuq_ref[:, hp * 2 * HEAD_PAD:(hp + 1) * 2 * HEAD_PAD]) * Q_SCALE
            for j in range(2):
                hd = 2 * hp + j
                c0 = j * HEAD_PAD
                q_s[hd, rows, :NOPE_DIM] = q2[:, c0:c0 + NOPE_DIM].astype(_BF)
                qr = q2[:, c0 + NOPE_DIM:c0 + HEAD_PAD]
                if has_rope:
                    qr = rotate(qr, r)
                q_s[hd, rows, NOPE_DIM:] = qr.astype(_BF)
        pe = _dot(h, win_ref[:, _C_PX:_C_CB])
        rows_h = pl.ds(_aligned((r // seq_len) * padded + HALO + r % seq_len, HALO), rc)
        px_s[rows_h, :] = pe[:, :POOL_WIDTH]
        prod_s[rows_h, :] = pe[:, POOL_WIDTH:POOL_WIDTH + CONV_WIDTH] * pe[:, POOL_WIDTH + CONV_WIDTH:]
        return carry

    _for_chunks(m // rc, project)

    def mix(c, carry):
        r = _aligned(c * rc, rc)
        rows = pl.ds(r, rc)
        h = h_s[rows, :]
        rows_k = pl.ds(0 if n_seq == 1 else r, keys)
        for hp in range(MLA_HEADS // 2):
            g2 = _silu(_dot(h, win_ref[:, _C_GMLA + hp * 2 * V_DIM:_C_GMLA + (hp + 1) * 2 * V_DIM]))
            for j in range(2):
                hd = 2 * hp + j
                sc = _dot(q_s[hd, rows, :], kt_s[hd, :, rows_k])
                p = jnp.exp2(sc - jnp.max(sc, axis=-1, keepdims=True))
                ov = _dot(p.astype(_BF), v_s[hd, rows_k, :])
                o = ov[:, :V_DIM] / ov[:, V_DIM:V_DIM + 1]
                mixed_s[rows, hd * V_DIM:(hd + 1) * V_DIM] = (g2[:, j * V_DIM:(j + 1) * V_DIM] * o).astype(_BF)

        rows_w = pl.ds(_aligned((r // seq_len) * padded + r % seq_len, HALO), rc + 2 * HALO)
        pooled = _pool_mix(px_s[rows_w, :], rcnt_ref[pl.ds(_aligned(r % seq_len, rc), rc), :])
        pool = _dot(pooled.astype(_BF), poolw_ref[...]) * pools_ref[row, :]
        gp = _silu(_dot(h, win_ref[:, _C_GPOOL:_C_PX]))
        mixed_s[rows, MLA_WIDTH:MLA_WIDTH + POOL_WIDTH] = (gp * pool).astype(_BF)

        convw = [convw_ref[k, row, :] for k in range(3)]
        conv = _short_conv(prod_s[rows_w, :], convw)
        e2 = _dot(h, win_ref[:, _C_CB:_C_END])
        mixed_s[rows, MLA_WIDTH + POOL_WIDTH:] = (_silu(e2[:, CONV_WIDTH:]) * (e2[:, :CONV_WIDTH] * conv)).astype(_BF)

        y = x_ref[rows, :] + gate * _dot(mixed_s[rows, :], wout_ref[...])
        if final_norm:
            y = _rms(y, gfin_ref[...])
        y_ref[rows, :] = y
        return carry

    _for_chunks(m // rc, mix)


def _mixer_layer(x2d, mod_all, layer, wts, *, seq_len, cache=None, rope=None, prev_state=None,
                 final_norm, state_mode):
    rows = x2d.shape[0]
    m = ROWS_PER_STEP
    n_seq = m // seq_len
    n_steps = rows // m
    n_all = rows // seq_len
    cache_len = 0 if cache is None else cache[0].shape[2]
    keys = cache_len + seq_len
    has_rope = rope is not None
    assert seq_len % ROW_CHUNK == 0 and cache_len % ROW_CHUNK == 0
    assert (seq_len == ROW_CHUNK and not cache_len) or (n_seq == 1 and not state_mode)

    def const(*shape):
        return pl.BlockSpec(shape, lambda i: (0,) * len(shape), pipeline_mode=pl.Buffered(1))

    def of_layer(*shape):
        return pl.BlockSpec((None,) + shape, lambda i: (layer,) + (0,) * len(shape),
                            pipeline_mode=pl.Buffered(1))

    args = [x2d, mod_all]
    in_specs = [pl.BlockSpec((m, D_MODEL), lambda i: (i, 0)), of_layer(3, MOD_ROWS, D_MODEL)]
    if cache_len:
        args += [cache[0], cache[1]]
        in_specs += [pl.BlockSpec((None, None, cache_len, KV_LORA), lambda i: (i, layer, 0, 0)),
                     pl.BlockSpec((None, None, ROPE_DIM, cache_len), lambda i: (i, layer, 0, 0))]
    if has_rope:
        args += [rope]
        in_specs += [const(seq_len, ROPE_PAD)]
    if state_mode == "stack":
        args += list(prev_state)
        in_specs += [pl.BlockSpec((n_seq, seq_len, KV_LORA), lambda i: (i, 0, 0)),
                     pl.BlockSpec((n_seq, ROPE_DIM, seq_len), lambda i: (i, 0, 0))]
    args += [_pool_rcnt(seq_len), wts["g_norm"], wts["w_in"], wts["g_q"], wts["w_uq"], wts["g_kv"],
             wts["w_ukt"], wts["w_uv"], wts["pool_w"], wts["pool_s"], wts["conv_w"], wts["w_out"],
             wts["g_final"]]
    in_specs += [const(seq_len, POOL_WIDTH), const(DEPTH, D_MODEL), of_layer(D_MODEL, _C_END),
                 const(DEPTH, Q_LORA), of_layer(Q_LORA, MLA_HEADS * HEAD_PAD), const(DEPTH, KV_LORA),
                 of_layer(MLA_HEADS * NOPE_DIM, KV_LORA), of_layer(KV_LORA, MLA_WIDTH),
                 of_layer(POOL_WIDTH, POOL_WIDTH), const(DEPTH, POOL_WIDTH), const(3, DEPTH, CONV_WIDTH),
                 of_layer(MIX_WIDTH, D_MODEL), const(1, D_MODEL)]

    out_shape = [jax.ShapeDtypeStruct((rows, D_MODEL), _F32)]
    out_specs = [pl.BlockSpec((m, D_MODEL), lambda i: (i, 0))]
    if state_mode == "own":
        out_shape += [jax.ShapeDtypeStruct((n_all, seq_len, KV_LORA), _F32),
                      jax.ShapeDtypeStruct((n_all, ROPE_DIM, seq_len), _F32)]
        out_specs += [pl.BlockSpec((n_seq, seq_len, KV_LORA), lambda i: (i, 0, 0)),
                      pl.BlockSpec((n_seq, ROPE_DIM, seq_len), lambda i: (i, 0, 0))]
    elif state_mode == "stack":
        out_shape += [jax.ShapeDtypeStruct((n_all, DEPTH, seq_len, KV_LORA), _F32),
                      jax.ShapeDtypeStruct((n_all, DEPTH, ROPE_DIM, seq_len), _F32)]
        out_specs += [pl.BlockSpec((n_seq, DEPTH, seq_len, KV_LORA), lambda i: (i, 0, 0, 0)),
                      pl.BlockSpec((n_seq, DEPTH, ROPE_DIM, seq_len), lambda i: (i, 0, 0, 0))]

    kern = functools.partial(_layer_kernel, layer=layer, n_seq=n_seq, seq_len=seq_len, cache_len=cache_len,
                             has_rope=has_rope, final_norm=final_norm, state_mode=state_mode)
    return pl.pallas_call(
        kern,
        grid=(n_steps,),
        in_specs=in_specs,
        out_specs=out_specs,
        out_shape=out_shape,
        scratch_shapes=[
            pltpu.VMEM((m, D_MODEL), _BF),
            pltpu.VMEM((MLA_HEADS, m, HEAD_PAD), _BF),
            pltpu.VMEM((MLA_HEADS, HEAD_PAD, n_seq * keys), _BF),
            pltpu.VMEM((MLA_HEADS, n_seq * keys, 2 * V_DIM), _BF),
            pltpu.VMEM((n_seq * (seq_len + 2 * HALO), POOL_WIDTH), _F32),
            pltpu.VMEM((n_seq * (seq_len + 2 * HALO), CONV_WIDTH), _F32),
            pltpu.VMEM((m, MIX_WIDTH), _BF),
        ],
        compiler_params=pltpu.CompilerParams(dimension_semantics=("arbitrary",),
                                             vmem_limit_bytes=VMEM_LIMIT_BYTES),
        name="mixer_layer_latent" if has_rope else "mixer_layer_context",
    )(*args)


def _pool_rcnt(seq_len):
    t = np.arange(seq_len)[:, None]
    half = np.repeat(1 << np.arange(POOL_GROUPS), POOL_GROUP_DIM)[None, :]
    cnt = np.minimum(t + half, seq_len) - np.maximum(t - half, 0)
    return jnp.asarray((1.0 / cnt).astype(np.float32))


def _rope_table(seq_len):
    rows = seq_len // GRID_W
    row = np.repeat(np.arange(rows), GRID_W).astype(np.float32)
    col = np.tile(np.arange(GRID_W), rows).astype(np.float32)
    inv = (1.0 / (np.float32(ROPE_BASE) ** (np.arange(0, AXIS_DIM, 2, dtype=np.float32) / np.float32(AXIS_DIM))))
    inv = inv.astype(np.float32)
    ang = np.concatenate([row[:, None] * inv, col[:, None] * inv], axis=-1).astype(np.float64)
    cos, sin = np.cos(ang).astype(np.float32), np.sin(ang).astype(np.float32)
    return jnp.asarray(np.concatenate([cos, cos, sin, sin], axis=-1))


def kernel(x_prompt, x_sample, cache_ckv, cache_krope, c, c_ctx, w_mod, b_mod, g_norm, w_in, g_q, w_uq,
           g_kv, w_ukv, pool_w, pool_s, conv_w, w_out, g_final):
    batch, seq, _ = x_prompt.shape
    dec_batch, dec_seq, _ = x_sample.shape
    assert 1 + dec_batch <= MOD_ROWS and dec_seq == ROWS_PER_STEP and ROWS_PER_STEP % seq == 0 and DEPTH == 2

    c_all = jnp.concatenate([c_ctx[None], c, jnp.zeros((MOD_ROWS - 1 - dec_batch, D_MODEL), _F32)], axis=0)
    mod_all = _modulation(c_all, w_mod, b_mod)
    w_in_r, w_out_r, w_uq_r, w_ukt, w_uv, pool_w_r = _prep_weights(
        jnp.transpose(w_in, (0, 2, 1)), w_out, w_uq, w_ukv, pool_w)
    wts = {
        "g_norm": g_norm, "w_in": w_in_r, "g_q": g_q, "w_uq": w_uq_r, "g_kv": g_kv, "w_ukt": w_ukt,
        "w_uv": w_uv, "pool_w": pool_w_r, "pool_s": pool_s, "conv_w": jnp.transpose(conv_w, (1, 0, 2)), "w_out": w_out_r,
        "g_final": g_final.reshape(1, D_MODEL),
    }

    xp = x_prompt.reshape(batch * seq, D_MODEL)
    xp, ckv0, kr0 = _mixer_layer(xp, mod_all, 0, wts, seq_len=seq, final_norm=False, state_mode="own")
    xp, state_ckv, state_krope_t = _mixer_layer(xp, mod_all, 1, wts, seq_len=seq, prev_state=(ckv0, kr0),
                                                final_norm=True, state_mode="stack")
    y_prompt = xp.reshape(batch, seq, D_MODEL)
    state_krope = jnp.transpose(state_krope_t, (0, 1, 3, 2))

    rope = _rope_table(dec_seq)
    xs = x_sample.reshape(dec_batch * dec_seq, D_MODEL)
    cache = (cache_ckv, jnp.transpose(cache_krope, (0, 1, 3, 2)))
    for l in range(DEPTH):
        (xs,) = _mixer_layer(xs, mod_all, l, wts, seq_len=dec_seq, cache=cache, rope=rope,
                             final_norm=(l == DEPTH - 1), state_mode=None)
    y_sample = xs.reshape(dec_batch, dec_seq, D_MODEL)
    return (y_prompt, y_sample, state_ckv, state_krope)
```

```python
import functools

import numpy as np
import jax
import jax.numpy as jnp
from jax import lax
from jax.experimental import pallas as pl
from jax.experimental.pallas import tpu as pltpu

D_MODEL = 1024
DEPTH = 2
GRID_W = 64
MLA_HEADS = 4
NOPE_DIM = 128
ROPE_DIM = 64
V_DIM = 128
QK_DIM = NOPE_DIM + ROPE_DIM
MLA_WIDTH = MLA_HEADS * V_DIM
Q_LORA = 384
KV_LORA = 256
POOL_GROUPS = 4
POOL_GROUP_DIM = 64
POOL_WIDTH = POOL_GROUPS * POOL_GROUP_DIM
CONV_WIDTH = 256
MIX_WIDTH = MLA_WIDTH + POOL_WIDTH + CONV_WIDTH
ROPE_BASE = 10000.0
AXIS_DIM = ROPE_DIM // 2
ATTN_SCALE = QK_DIM ** -0.5
Q_SCALE = ATTN_SCALE * float(np.log2(np.e))
EPS = 1e-6

_SPLITS = (Q_LORA, KV_LORA, ROPE_DIM, MLA_WIDTH, POOL_WIDTH, POOL_WIDTH,
           CONV_WIDTH, CONV_WIDTH, CONV_WIDTH, CONV_WIDTH)
_OFFS = [sum(_SPLITS[:i]) for i in range(len(_SPLITS) + 1)]
IN_WIDTH = _OFFS[-1]

V7X_LANES = 128
V7X_SUBLANES = 8
V7X_MXU_DEPTH = 256
HEAD_PAD = V7X_MXU_DEPTH
ROPE_PAD = HEAD_PAD - NOPE_DIM
KV_HEAD = NOPE_DIM + V_DIM

_C_CQ = 0
_C_CKV = _C_CQ + Q_LORA
_C_KR = _C_CKV + KV_LORA
_C_GMLA = _C_KR + ROPE_PAD
_C_GPOOL = _C_GMLA + MLA_WIDTH
_C_PX = _C_GPOOL + POOL_WIDTH
_C_CC = _C_PX + POOL_WIDTH
_C_CB = _C_CC + 2 * CONV_WIDTH
_C_END = _C_CB + 2 * CONV_WIDTH

MOD_ROWS = 8
ROWS_PER_STEP = 1024
ROW_CHUNK = 256
HALO = V7X_SUBLANES
PREP_COLS = 256
VMEM_LIMIT_BYTES = 56 * 1024 * 1024

_BF = jnp.bfloat16
_F32 = jnp.float32


def _dot(a, b):
    return jnp.dot(a, b, preferred_element_type=_F32)


def _rms(x, g):
    return x * lax.rsqrt(jnp.mean(x * x, axis=-1, keepdims=True) + EPS) * g


def _silu(x):
    return x * jax.nn.sigmoid(x)


def _aligned(v, k):
    return v if isinstance(v, int) else pl.multiple_of(v, k)


def _for_chunks(n, body):
    for c in range(n):
        body(c, 0)


def _mod_kernel(c_ref, w_ref, b_ref, o_ref):
    s = _silu(c_ref[...])
    bias = b_ref[pl.ds(pl.program_id(0), 1), :]
    o_ref[...] = _dot(s.astype(_BF), w_ref[...].astype(_BF)) + bias


def _modulation(c_all, w_mod, b_mod):
    return pl.pallas_call(
        _mod_kernel,
        grid=(DEPTH, 3),
        in_specs=[
            pl.BlockSpec((MOD_ROWS, D_MODEL), lambda l, j: (0, 0)),
            pl.BlockSpec((None, D_MODEL, D_MODEL), lambda l, j: (l, 0, j)),
            pl.BlockSpec((DEPTH, D_MODEL), lambda l, j: (0, j)),
        ],
        out_specs=pl.BlockSpec((None, None, MOD_ROWS, D_MODEL), lambda l, j: (l, j, 0, 0)),
        out_shape=jax.ShapeDtypeStruct((DEPTH, 3, MOD_ROWS, D_MODEL), _F32),
        name="modulation",
    )(c_all, w_mod, b_mod)


def _prep_kernel(wint_ref, wout_ref, wuq_ref, wukv_ref, poolw_ref,
                 win_o, wout_o, wuq_o, wukt_o, wuv_o, poolw_o):
    def put(dst, src, width):
        for c in range(0, width, PREP_COLS):
            n = min(PREP_COLS, width - c)
            win_o[:, dst + c:dst + c + n] = wint_ref[src + c:src + c + n, :].T.astype(_BF)

    put(_C_CQ, _OFFS[0], Q_LORA + KV_LORA)
    kr = wint_ref[_OFFS[2]:_OFFS[3], :]
    krb = jnp.concatenate([kr, -kr[AXIS_DIM:], kr[:AXIS_DIM]], axis=0)
    win_o[:, _C_KR:_C_GMLA] = krb.T.astype(_BF)
    put(_C_GMLA, _OFFS[3], MLA_WIDTH)
    put(_C_GPOOL, _OFFS[5], POOL_WIDTH)
    put(_C_PX, _OFFS[4], POOL_WIDTH)
    put(_C_CC, _OFFS[7], 2 * CONV_WIDTH)
    put(_C_CB, _OFFS[6], CONV_WIDTH)
    put(_C_CB + CONV_WIDTH, _OFFS[9], CONV_WIDTH)
    wout_o[...] = wout_ref[...].astype(_BF)

    for hd in range(MLA_HEADS):
        s0, d0 = hd * QK_DIM, hd * HEAD_PAD
        r0 = s0 + NOPE_DIM
        wuq_o[:, d0:d0 + QK_DIM] = wuq_ref[:, s0:s0 + QK_DIM].astype(_BF)
        wuq_o[:, d0 + QK_DIM:d0 + QK_DIM + AXIS_DIM] = (-wuq_ref[:, r0 + AXIS_DIM:r0 + ROPE_DIM]).astype(_BF)
        wuq_o[:, d0 + QK_DIM + AXIS_DIM:d0 + HEAD_PAD] = wuq_ref[:, r0:r0 + AXIS_DIM].astype(_BF)
    for hd in range(MLA_HEADS):
        k0 = hd * KV_HEAD
        wukt_o[hd * NOPE_DIM:(hd + 1) * NOPE_DIM, :] = wukv_ref[:, k0:k0 + NOPE_DIM].T.astype(_BF)
        wuv_o[:, hd * V_DIM:(hd + 1) * V_DIM] = wukv_ref[:, k0 + NOPE_DIM:k0 + KV_HEAD].astype(_BF)
    poolw_o[...] = jnp.zeros((POOL_WIDTH, POOL_WIDTH), _BF)
    for g in range(POOL_GROUPS):
        sl = slice(g * POOL_GROUP_DIM, (g + 1) * POOL_GROUP_DIM)
        poolw_o[sl, sl] = poolw_ref[g].astype(_BF)


def _prep_weights(w_in_t, w_out, w_uq, w_ukv, pool_w):
    per_layer = lambda *shape: pl.BlockSpec((None,) + shape, lambda l: (l,) + (0,) * len(shape))
    return pl.pallas_call(
        _prep_kernel,
        grid=(DEPTH,),
        in_specs=[
            per_layer(IN_WIDTH, D_MODEL),
            per_layer(MIX_WIDTH, D_MODEL),
            per_layer(Q_LORA, MLA_HEADS * QK_DIM),
            per_layer(KV_LORA, MLA_HEADS * KV_HEAD),
            per_layer(POOL_GROUPS, POOL_GROUP_DIM, POOL_GROUP_DIM),
        ],
        out_specs=[
            per_layer(D_MODEL, _C_END),
            per_layer(MIX_WIDTH, D_MODEL),
            per_layer(Q_LORA, MLA_HEADS * HEAD_PAD),
            per_layer(MLA_HEADS * NOPE_DIM, KV_LORA),
            per_layer(KV_LORA, MLA_WIDTH),
            per_layer(POOL_WIDTH, POOL_WIDTH),
        ],
        out_shape=[
            jax.ShapeDtypeStruct((DEPTH, D_MODEL, _C_END), _BF),
            jax.ShapeDtypeStruct((DEPTH, MIX_WIDTH, D_MODEL), _BF),
            jax.ShapeDtypeStruct((DEPTH, Q_LORA, MLA_HEADS * HEAD_PAD), _BF),
            jax.ShapeDtypeStruct((DEPTH, MLA_HEADS * NOPE_DIM, KV_LORA), _BF),
            jax.ShapeDtypeStruct((DEPTH, KV_LORA, MLA_WIDTH), _BF),
            jax.ShapeDtypeStruct((DEPTH, POOL_WIDTH, POOL_WIDTH), _BF),
        ],
        compiler_params=pltpu.CompilerParams(dimension_semantics=("arbitrary",),
                                             vmem_limit_bytes=VMEM_LIMIT_BYTES),
        name="weight_prep",
    )(w_in_t, w_out, w_uq, w_ukv, pool_w)


def _pool_mix(win, rcnt):
    n = win.shape[0]
    rows = n - 2 * HALO
    lane = lax.broadcasted_iota(jnp.int32, (1, POOL_WIDTH), 1)
    sums = [win + pltpu.roll(win, 1, axis=0)]
    for k in (1, 2, 4):
        sums.append(pltpu.roll(sums[-1], k, axis=0) + pltpu.roll(sums[-1], n - k, axis=0))
    sel = sums[-1]
    for g in range(POOL_GROUPS - 2, -1, -1):
        sel = jnp.where(lane < (g + 1) * POOL_GROUP_DIM, sums[g], sel)
    return sel[HALO:HALO + rows] * rcnt - win[HALO:HALO + rows]


def _short_conv(win, w):
    rows = win.shape[0] - 2 * HALO
    return (win[HALO - 1:HALO - 1 + rows] * w[0] + win[HALO:HALO + rows] * w[1]
            + win[HALO + 1:HALO + 1 + rows] * w[2])


def _layer_kernel(*refs, layer, n_seq, seq_len, cache_len, has_rope, final_norm, state_mode, ordered_after):
    it = iter(refs)
    x_ref, mod_ref = next(it), next(it)
    if ordered_after:
        next(it)
    if cache_len:
        cckv_ref, ckr_ref = next(it), next(it)
    if has_rope:
        cs_ref = next(it)
    if state_mode == "stack":
        ckv_prev_ref, kr_prev_ref = next(it), next(it)
    (rcnt_ref, gn_ref, win_ref, gq_ref, wuq_ref, gkv_ref, wukt_ref, wuv_ref, poolw_ref, pools_ref,
     convw_ref, wout_ref, gfin_ref) = (next(it) for _ in range(13))
    y_ref = next(it)
    if state_mode:
        ckv_out_ref, kr_out_ref = next(it), next(it)
    h_s, q_s, kt_s, v_s, px_s, prod_s, mixed_s = (next(it) for _ in range(7))

    rc = ROW_CHUNK
    m = n_seq * seq_len
    keys = cache_len + seq_len
    padded = seq_len + 2 * HALO
    row = slice(layer, layer + 1)
    lane = lax.broadcasted_iota(jnp.int32, (1, ROPE_PAD), 1)

    mrow = (1 + pl.program_id(0)) if has_rope else 0
    shift, scale, gate = (mod_ref[k, pl.ds(mrow, 1), :] for k in range(3))

    zeros_halo = jnp.zeros((HALO, POOL_WIDTH), _F32)
    for buf in (px_s, prod_s):
        for s in range(n_seq):
            buf[s * padded:s * padded + HALO, :] = zeros_halo
            buf[(s + 1) * padded - HALO:(s + 1) * padded, :] = zeros_halo
    ones_col = jnp.where(lax.broadcasted_iota(jnp.int32, (n_seq * keys, V_DIM), 1) == 0, 1.0, 0.0).astype(_BF)
    for hd in range(MLA_HEADS):
        v_s[hd, :, V_DIM:] = ones_col
    if state_mode == "stack":
        ckv_out_ref[:, 0] = ckv_prev_ref[...]
        kr_out_ref[:, 0] = kr_prev_ref[...]

    def rotate(v, r):
        v = v * cs_ref[pl.ds(r, rc), :]
        return v + pltpu.roll(v, ROPE_DIM, axis=1)

    def put_kv(ckv, kr_t, rows_k):
        knt = _dot(wukt_ref[...], ckv.T.astype(_BF))
        v4 = _dot(ckv.astype(_BF), wuv_ref[...])
        kr_bf = kr_t.astype(_BF)
        for hd in range(MLA_HEADS):
            kt_s[hd, :NOPE_DIM, rows_k] = knt[hd * NOPE_DIM:(hd + 1) * NOPE_DIM, :].astype(_BF)
            kt_s[hd, NOPE_DIM:, rows_k] = kr_bf
            v_s[hd, rows_k, :V_DIM] = v4[:, hd * V_DIM:(hd + 1) * V_DIM].astype(_BF)

    if cache_len:
        ckr_t = ckr_ref[...]
        ckr_pad = jnp.concatenate([ckr_t, jnp.zeros_like(ckr_t)], axis=0)
        for cc in range(cache_len // rc):
            rows_k = slice(cc * rc, (cc + 1) * rc)
            put_kv(cckv_ref[rows_k, :], ckr_pad[:, rows_k], rows_k)

    def project(c, carry):
        r = _aligned(c * rc, rc)
        rows = pl.ds(r, rc)
        h = (_rms(x_ref[rows, :], gn_ref[row, :]) * (1.0 + scale) + shift).astype(_BF)
        h_s[rows, :] = h
        a = _dot(h, win_ref[:, _C_CQ:_C_GMLA])
        cq = _rms(a[:, :Q_LORA], gq_ref[row, :]).astype(_BF)
        ckv = _rms(a[:, _C_CKV:_C_KR], gkv_ref[row, :])
        kr = a[:, _C_KR:]
        if has_rope:
            kr = rotate(kr, r)
        kr_t = jnp.where(lane < ROPE_DIM, kr, 0.0).T
        if state_mode:
            own = () if state_mode == "own" else (layer,)
            ckv_out_ref[(c,) + own] = ckv
            kr_out_ref[(c,) + own] = kr_t[:ROPE_DIM, :]
        put_kv(ckv, kr_t, pl.ds(_aligned(cache_len + r, rc), rc))
        for hp in range(MLA_HEADS // 2):
            q2 = _dot(cq, wuq_ref[:, hp * 2 * HEAD_PAD:(hp + 1) * 2 * HEAD_PAD]) * Q_SCALE
            for j in range(2):
                hd = 2 * hp + j
                c0 = j * HEAD_PAD
                q_s[hd, rows, :NOPE_DIM] = q2[:, c0:c0 + NOPE_DIM].astype(_BF)
                qr = q2[:, c0 + NOPE_DIM:c0 + HEAD_PAD]
                if has_rope:
                    qr = rotate(qr, r)
                q_s[hd, rows, NOPE_DIM:] = qr.astype(_BF)
        pe = _dot(h, win_ref[:, _C_PX:_C_CB])
        rows_h = pl.ds(_aligned((r // seq_len) * padded + HALO + r % seq_len, HALO), rc)
        px_s[rows_h, :] = pe[:, :POOL_WIDTH]
        prod_s[rows_h, :] = pe[:, POOL_WIDTH:POOL_WIDTH + CONV_WIDTH] * pe[:, POOL_WIDTH + CONV_WIDTH:]
        return carry

    _for_chunks(m // rc, project)

    def mix(c, carry):
        r = _aligned(c * rc, rc)
        rows = pl.ds(r, rc)
        h = h_s[rows, :]
        rows_k = pl.ds(0 if n_seq == 1 else r, keys)
        for hp in range(MLA_HEADS // 2):
            g2 = _silu(_dot(h, win_ref[:, _C_GMLA + hp * 2 * V_DIM:_C_GMLA + (hp + 1) * 2 * V_DIM]))
            for j in range(2):
                hd = 2 * hp + j
                sc = _dot(q_s[hd, rows, :], kt_s[hd, :, rows_k])
                p = jnp.exp2(sc - jnp.max(sc, axis=-1, keepdims=True))
                ov = _dot(p.astype(_BF), v_s[hd, rows_k, :])
                o = ov[:, :V_DIM] / ov[:, V_DIM:V_DIM + 1]
                mixed_s[rows, hd * V_DIM:(hd + 1) * V_DIM] = (g2[:, j * V_DIM:(j + 1) * V_DIM] * o).astype(_BF)

        rows_w = pl.ds(_aligned((r // seq_len) * padded + r % seq_len, HALO), rc + 2 * HALO)
        pooled = _pool_mix(px_s[rows_w, :], rcnt_ref[pl.ds(_aligned(r % seq_len, rc), rc), :])
        pool = _dot(pooled.astype(_BF), poolw_ref[...]) * pools_ref[row, :]
        gp = _silu(_dot(h, win_ref[:, _C_GPOOL:_C_PX]))
        mixed_s[rows, MLA_WIDTH:MLA_WIDTH + POOL_WIDTH] = (gp * pool).astype(_BF)

        convw = [convw_ref[k, row, :] for k in range(3)]
        conv = _short_conv(prod_s[rows_w, :], convw)
        e2 = _dot(h, win_ref[:, _C_CB:_C_END])
        mixed_s[rows, MLA_WIDTH + POOL_WIDTH:] = (_silu(e2[:, CONV_WIDTH:]) * (e2[:, :CONV_WIDTH] * conv)).astype(_BF)

        y = x_ref[rows, :] + gate * _dot(mixed_s[rows, :], wout_ref[...])
        if final_norm:
            y = _rms(y, gfin_ref[...])
        y_ref[rows, :] = y
        return carry

    _for_chunks(m // rc, mix)


def _mixer_layer(x2d, mod_all, layer, wts, *, seq_len, cache=None, rope=None, prev_state=None,
                 final_norm, state_mode, after=None):
    rows = x2d.shape[0]
    m = ROWS_PER_STEP
    n_seq = m // seq_len
    n_steps = rows // m
    n_all = rows // seq_len
    cache_len = 0 if cache is None else cache[0].shape[2]
    keys = cache_len + seq_len
    has_rope = rope is not None
    assert seq_len % ROW_CHUNK == 0 and cache_len % ROW_CHUNK == 0
    assert (seq_len == ROW_CHUNK and not cache_len) or (n_seq == 1 and not state_mode)

    def const(*shape):
        return pl.BlockSpec(shape, lambda i: (0,) * len(shape), pipeline_mode=pl.Buffered(1))

    def of_layer(*shape):
        return pl.BlockSpec((None,) + shape, lambda i: (layer,) + (0,) * len(shape),
                            pipeline_mode=pl.Buffered(1))

    args = [x2d, mod_all]
    in_specs = [pl.BlockSpec((m, D_MODEL), lambda i: (i, 0)), of_layer(3, MOD_ROWS, D_MODEL)]
    if after is not None:
        args += [after]
        in_specs += [pl.BlockSpec(memory_space=pl.ANY)]
    if cache_len:
        args += [cache[0], cache[1]]
        in_specs += [pl.BlockSpec((None, None, cache_len, KV_LORA), lambda i: (i, layer, 0, 0)),
                     pl.BlockSpec((None, None, ROPE_DIM, cache_len), lambda i: (i, layer, 0, 0))]
    if has_rope:
        args += [rope]
        in_specs += [const(seq_len, ROPE_PAD)]
    if state_mode == "stack":
        args += list(prev_state)
        in_specs += [pl.BlockSpec((n_seq, seq_len, KV_LORA), lambda i: (i, 0, 0)),
                     pl.BlockSpec((n_seq, ROPE_DIM, seq_len), lambda i: (i, 0, 0))]
    args += [_pool_rcnt(seq_len), wts["g_norm"], wts["w_in"], wts["g_q"], wts["w_uq"], wts["g_kv"],
             wts["w_ukt"], wts["w_uv"], wts["pool_w"], wts["pool_s"], wts["conv_w"], wts["w_out"],
             wts["g_final"]]
    in_specs += [const(seq_len, POOL_WIDTH), const(DEPTH, D_MODEL), of_layer(D_MODEL, _C_END),
                 const(DEPTH, Q_LORA), of_layer(Q_LORA, MLA_HEADS * HEAD_PAD), const(DEPTH, KV_LORA),
                 of_layer(MLA_HEADS * NOPE_DIM, KV_LORA), of_layer(KV_LORA, MLA_WIDTH),
                 of_layer(POOL_WIDTH, POOL_WIDTH), const(DEPTH, POOL_WIDTH), const(3, DEPTH, CONV_WIDTH),
                 of_layer(MIX_WIDTH, D_MODEL), const(1, D_MODEL)]

    out_shape = [jax.ShapeDtypeStruct((rows, D_MODEL), _F32)]
    out_specs = [pl.BlockSpec((m, D_MODEL), lambda i: (i, 0))]
    if state_mode == "own":
        out_shape += [jax.ShapeDtypeStruct((n_all, seq_len, KV_LORA), _F32),
                      jax.ShapeDtypeStruct((n_all, ROPE_DIM, seq_len), _F32)]
        out_specs += [pl.BlockSpec((n_seq, seq_len, KV_LORA), lambda i: (i, 0, 0)),
                      pl.BlockSpec((n_seq, ROPE_DIM, seq_len), lambda i: (i, 0, 0))]
    elif state_mode == "stack":
        out_shape += [jax.ShapeDtypeStruct((n_all, DEPTH, seq_len, KV_LORA), _F32),
                      jax.ShapeDtypeStruct((n_all, DEPTH, ROPE_DIM, seq_len), _F32)]
        out_specs += [pl.BlockSpec((n_seq, DEPTH, seq_len, KV_LORA), lambda i: (i, 0, 0, 0)),
                      pl.BlockSpec((n_seq, DEPTH, ROPE_DIM, seq_len), lambda i: (i, 0, 0, 0))]

    kern = functools.partial(_layer_kernel, layer=layer, n_seq=n_seq, seq_len=seq_len, cache_len=cache_len,
                             has_rope=has_rope, final_norm=final_norm, state_mode=state_mode,
                             ordered_after=after is not None)
    return pl.pallas_call(
        kern,
        grid=(n_steps,),
        in_specs=in_specs,
        out_specs=out_specs,
        out_shape=out_shape,
        scratch_shapes=[
            pltpu.VMEM((m, D_MODEL), _BF),
            pltpu.VMEM((MLA_HEADS, m, HEAD_PAD), _BF),
            pltpu.VMEM((MLA_HEADS, HEAD_PAD, n_seq * keys), _BF),
            pltpu.VMEM((MLA_HEADS, n_seq * keys, 2 * V_DIM), _BF),
            pltpu.VMEM((n_seq * (seq_len + 2 * HALO), POOL_WIDTH), _F32),
            pltpu.VMEM((n_seq * (seq_len + 2 * HALO), CONV_WIDTH), _F32),
            pltpu.VMEM((m, MIX_WIDTH), _BF),
        ],
        compiler_params=pltpu.CompilerParams(dimension_semantics=("arbitrary",),
                                             vmem_limit_bytes=VMEM_LIMIT_BYTES),
        name="mixer_layer_latent" if has_rope else "mixer_layer_context",
    )(*args)


def _pool_rcnt(seq_len):
    t = np.arange(seq_len)[:, None]
    half = np.repeat(1 << np.arange(POOL_GROUPS), POOL_GROUP_DIM)[None, :]
    cnt = np.minimum(t + half, seq_len) - np.maximum(t - half, 0)
    return jnp.asarray((1.0 / cnt).astype(np.float32))


def _rope_table(seq_len):
    rows = seq_len // GRID_W
    row = np.repeat(np.arange(rows), GRID_W).astype(np.float32)
    col = np.tile(np.arange(GRID_W), rows).astype(np.float32)
    inv = (1.0 / (np.float32(ROPE_BASE) ** (np.arange(0, AXIS_DIM, 2, dtype=np.float32) / np.float32(AXIS_DIM))))
    inv = inv.astype(np.float32)
    ang = np.concatenate([row[:, None] * inv, col[:, None] * inv], axis=-1).astype(np.float64)
    cos, sin = np.cos(ang).astype(np.float32), np.sin(ang).astype(np.float32)
    return jnp.asarray(np.concatenate([cos, cos, sin, sin], axis=-1))


def kernel(x_prompt, x_sample, cache_ckv, cache_krope, c, c_ctx, w_mod, b_mod, g_norm, w_in, g_q, w_uq,
           g_kv, w_ukv, pool_w, pool_s, conv_w, w_out, g_final):
    batch, seq, _ = x_prompt.shape
    dec_batch, dec_seq, _ = x_sample.shape
    assert 1 + dec_batch <= MOD_ROWS and dec_seq == ROWS_PER_STEP and ROWS_PER_STEP % seq == 0 and DEPTH == 2

    c_all = jnp.concatenate([c_ctx[None], c, jnp.zeros((MOD_ROWS - 1 - dec_batch, D_MODEL), _F32)], axis=0)
    mod_all = _modulation(c_all, w_mod, b_mod)
    w_in_r, w_out_r, w_uq_r, w_ukt, w_uv, pool_w_r = _prep_weights(
        jnp.transpose(w_in, (0, 2, 1)), w_out, w_uq, w_ukv, pool_w)
    wts = {
        "g_norm": g_norm, "w_in": w_in_r, "g_q": g_q, "w_uq": w_uq_r, "g_kv": g_kv, "w_ukt": w_ukt,
        "w_uv": w_uv, "pool_w": pool_w_r, "pool_s": pool_s, "conv_w": jnp.transpose(conv_w, (1, 0, 2)), "w_out": w_out_r,
        "g_final": g_final.reshape(1, D_MODEL),
    }

    rope = _rope_table(dec_seq)
    cache = (cache_ckv, jnp.transpose(cache_krope, (0, 1, 3, 2)))
    xp = x_prompt.reshape(batch * seq, D_MODEL)
    xs = x_sample.reshape(dec_batch * dec_seq, D_MODEL)
    xp, ckv0, kr0 = _mixer_layer(xp, mod_all, 0, wts, seq_len=seq, final_norm=False, state_mode="own")
    (xs,) = _mixer_layer(xs, mod_all, 0, wts, seq_len=dec_seq, cache=cache, rope=rope, final_norm=False,
                         state_mode=None, after=kr0)
    xp, state_ckv, state_krope_t = _mixer_layer(xp, mod_all, 1, wts, seq_len=seq, prev_state=(ckv0, kr0),
                                                final_norm=True, state_mode="stack", after=xs)
    (xs,) = _mixer_layer(xs, mod_all, 1, wts, seq_len=dec_seq, cache=cache, rope=rope, final_norm=True,
                         state_mode=None, after=state_krope_t)
    y_prompt = xp.reshape(batch, seq, D_MODEL)
    state_krope = jnp.transpose(state_krope_t, (0, 1, 3, 2))
    y_sample = xs.reshape(dec_batch, dec_seq, D_MODEL)
    return (y_prompt, y_sample, state_ckv, state_krope)
```

```python
import functools

import numpy as np
import jax
import jax.numpy as jnp
from jax import lax
from jax.experimental import pallas as pl
from jax.experimental.pallas import tpu as pltpu

D_MODEL = 1024
DEPTH = 2
GRID_W = 64
MLA_HEADS = 4
NOPE_DIM = 128
ROPE_DIM = 64
V_DIM = 128
QK_DIM = NOPE_DIM + ROPE_DIM
MLA_WIDTH = MLA_HEADS * V_DIM
Q_LORA = 384
KV_LORA = 256
POOL_GROUPS = 4
POOL_GROUP_DIM = 64
POOL_WIDTH = POOL_GROUPS * POOL_GROUP_DIM
CONV_WIDTH = 256
MIX_WIDTH = MLA_WIDTH + POOL_WIDTH + CONV_WIDTH
ROPE_BASE = 10000.0
AXIS_DIM = ROPE_DIM // 2
ATTN_SCALE = QK_DIM ** -0.5
Q_SCALE = ATTN_SCALE * float(np.log2(np.e))
EPS = 1e-6

_SPLITS = (Q_LORA, KV_LORA, ROPE_DIM, MLA_WIDTH, POOL_WIDTH, POOL_WIDTH,
           CONV_WIDTH, CONV_WIDTH, CONV_WIDTH, CONV_WIDTH)
_OFFS = [sum(_SPLITS[:i]) for i in range(len(_SPLITS) + 1)]
IN_WIDTH = _OFFS[-1]

V7X_LANES = 128
V7X_SUBLANES = 8
V7X_MXU_DEPTH = 256
HEAD_PAD = V7X_MXU_DEPTH
ROPE_PAD = HEAD_PAD - NOPE_DIM
KV_HEAD = NOPE_DIM + V_DIM

_C_CQ = 0
_C_CKV = _C_CQ + Q_LORA
_C_KR = _C_CKV + KV_LORA
_C_GMLA = _C_KR + ROPE_PAD
_C_GPOOL = _C_GMLA + MLA_WIDTH
_C_PX = _C_GPOOL + POOL_WIDTH
_C_CC = _C_PX + POOL_WIDTH
_C_CB = _C_CC + 2 * CONV_WIDTH
_C_END = _C_CB + 2 * CONV_WIDTH

MOD_ROWS = 8
ROWS_PER_STEP = 1024
ROW_CHUNK = 256
CHUNK_UNROLL = 2
HALO = V7X_SUBLANES
PREP_COLS = 256
VMEM_LIMIT_BYTES = 56 * 1024 * 1024

_BF = jnp.bfloat16
_F32 = jnp.float32


def _dot(a, b):
    return jnp.dot(a, b, preferred_element_type=_F32)


def _rms(x, g):
    return x * lax.rsqrt(jnp.mean(x * x, axis=-1, keepdims=True) + EPS) * g


def _silu(x):
    return x * jax.nn.sigmoid(x)


def _aligned(v, k):
    return v if isinstance(v, int) else pl.multiple_of(v, k)


def _for_chunks(n, unroll, body):
    if unroll >= n:
        for c in range(n):
            body(c)
        return

    def step(i, carry):
        for u in range(unroll):
            body(i * unroll + u)
        return carry

    lax.fori_loop(0, n // unroll, step, 0)


def _mod_kernel(c_ref, w_ref, b_ref, o_ref):
    s = _silu(c_ref[...])
    bias = b_ref[pl.ds(pl.program_id(0), 1), :]
    o_ref[...] = _dot(s.astype(_BF), w_ref[...].astype(_BF)) + bias


def _modulation(c_all, w_mod, b_mod):
    return pl.pallas_call(
        _mod_kernel,
        grid=(DEPTH, 3),
        in_specs=[
            pl.BlockSpec((MOD_ROWS, D_MODEL), lambda l, j: (0, 0)),
            pl.BlockSpec((None, D_MODEL, D_MODEL), lambda l, j: (l, 0, j)),
            pl.BlockSpec((DEPTH, D_MODEL), lambda l, j: (0, j)),
        ],
        out_specs=pl.BlockSpec((None, None, MOD_ROWS, D_MODEL), lambda l, j: (l, j, 0, 0)),
        out_shape=jax.ShapeDtypeStruct((DEPTH, 3, MOD_ROWS, D_MODEL), _F32),
        name="modulation",
    )(c_all, w_mod, b_mod)


def _prep_kernel(wint_ref, wout_ref, wuq_ref, wukv_ref, poolw_ref,
                 win_o, wout_o, wuq_o, wukt_o, wuv_o, poolw_o):
    def put(dst, src, width):
        for c in range(0, width, PREP_COLS):
            n = min(PREP_COLS, width - c)
            win_o[:, dst + c:dst + c + n] = wint_ref[src + c:src + c + n, :].T.astype(_BF)

    put(_C_CQ, _OFFS[0], Q_LORA + KV_LORA)
    kr = wint_ref[_OFFS[2]:_OFFS[3], :]
    krb = jnp.concatenate([kr, -kr[AXIS_DIM:], kr[:AXIS_DIM]], axis=0)
    win_o[:, _C_KR:_C_GMLA] = krb.T.astype(_BF)
    put(_C_GMLA, _OFFS[3], MLA_WIDTH)
    put(_C_GPOOL, _OFFS[5], POOL_WIDTH)
    put(_C_PX, _OFFS[4], POOL_WIDTH)
    put(_C_CC, _OFFS[7], 2 * CONV_WIDTH)
    put(_C_CB, _OFFS[6], CONV_WIDTH)
    put(_C_CB + CONV_WIDTH, _OFFS[9], CONV_WIDTH)
    wout_o[...] = wout_ref[...].astype(_BF)

    for hd in range(MLA_HEADS):
        s0, d0 = hd * QK_DIM, hd * HEAD_PAD
        r0 = s0 + NOPE_DIM
        wuq_o[:, d0:d0 + QK_DIM] = wuq_ref[:, s0:s0 + QK_DIM].astype(_BF)
        wuq_o[:, d0 + QK_DIM:d0 + QK_DIM + AXIS_DIM] = (-wuq_ref[:, r0 + AXIS_DIM:r0 + ROPE_DIM]).astype(_BF)
        wuq_o[:, d0 + QK_DIM + AXIS_DIM:d0 + HEAD_PAD] = wuq_ref[:, r0:r0 + AXIS_DIM].astype(_BF)
    for hd in range(MLA_HEADS):
        k0 = hd * KV_HEAD
        wukt_o[hd * NOPE_DIM:(hd + 1) * NOPE_DIM, :] = wukv_ref[:, k0:k0 + NOPE_DIM].T.astype(_BF)
        wuv_o[:, hd * V_DIM:(hd + 1) * V_DIM] = wukv_ref[:, k0 + NOPE_DIM:k0 + KV_HEAD].astype(_BF)
    poolw_o[...] = jnp.zeros((POOL_WIDTH, POOL_WIDTH), _BF)
    for g in range(POOL_GROUPS):
        sl = slice(g * POOL_GROUP_DIM, (g + 1) * POOL_GROUP_DIM)
        poolw_o[sl, sl] = poolw_ref[g].astype(_BF)


def _prep_weights(w_in_t, w_out, w_uq, w_ukv, pool_w):
    per_layer = lambda *shape: pl.BlockSpec((None,) + shape, lambda l: (l,) + (0,) * len(shape))
    return pl.pallas_call(
        _prep_kernel,
        grid=(DEPTH,),
        in_specs=[
            per_layer(IN_WIDTH, D_MODEL),
            per_layer(MIX_WIDTH, D_MODEL),
            per_layer(Q_LORA, MLA_HEADS * QK_DIM),
            per_layer(KV_LORA, MLA_HEADS * KV_HEAD),
            per_layer(POOL_GROUPS, POOL_GROUP_DIM, POOL_GROUP_DIM),
        ],
        out_specs=[
            per_layer(D_MODEL, _C_END),
            per_layer(MIX_WIDTH, D_MODEL),
            per_layer(Q_LORA, MLA_HEADS * HEAD_PAD),
            per_layer(MLA_HEADS * NOPE_DIM, KV_LORA),
            per_layer(KV_LORA, MLA_WIDTH),
            per_layer(POOL_WIDTH, POOL_WIDTH),
        ],
        out_shape=[
            jax.ShapeDtypeStruct((DEPTH, D_MODEL, _C_END), _BF),
            jax.ShapeDtypeStruct((DEPTH, MIX_WIDTH, D_MODEL), _BF),
            jax.ShapeDtypeStruct((DEPTH, Q_LORA, MLA_HEADS * HEAD_PAD), _BF),
            jax.ShapeDtypeStruct((DEPTH, MLA_HEADS * NOPE_DIM, KV_LORA), _BF),
            jax.ShapeDtypeStruct((DEPTH, KV_LORA, MLA_WIDTH), _BF),
            jax.ShapeDtypeStruct((DEPTH, POOL_WIDTH, POOL_WIDTH), _BF),
        ],
        compiler_params=pltpu.CompilerParams(dimension_semantics=("arbitrary",),
                                             vmem_limit_bytes=VMEM_LIMIT_BYTES),
        name="weight_prep",
    )(w_in_t, w_out, w_uq, w_ukv, pool_w)


def _pool_mix(win, rcnt):
    n = win.shape[0]
    rows = n - 2 * HALO
    lane = lax.broadcasted_iota(jnp.int32, (1, POOL_WIDTH), 1)
    sums = [win + pltpu.roll(win, 1, axis=0)]
    for k in (1, 2, 4):
        sums.append(pltpu.roll(sums[-1], k, axis=0) + pltpu.roll(sums[-1], n - k, axis=0))
    sel = sums[-1]
    for g in range(POOL_GROUPS - 2, -1, -1):
        sel = jnp.where(lane < (g + 1) * POOL_GROUP_DIM, sums[g], sel)
    return sel[HALO:HALO + rows] * rcnt - win[HALO:HALO + rows]


def _short_conv(win, w):
    rows = win.shape[0] - 2 * HALO
    return (win[HALO - 1:HALO - 1 + rows] * w[0] + win[HALO:HALO + rows] * w[1]
            + win[HALO + 1:HALO + 1 + rows] * w[2])


def _layer_kernel(*refs, layer, n_seq, seq_len, cache_len, has_rope, final_norm, state_mode, ordered_after,
                  unroll):
    it = iter(refs)
    x_ref, mod_ref = next(it), next(it)
    if ordered_after:
        next(it)
    if cache_len:
        cckv_ref, ckr_ref = next(it), next(it)
    if has_rope:
        cs_ref = next(it)
    if state_mode == "stack":
        ckv_prev_ref, kr_prev_ref = next(it), next(it)
    (rcnt_ref, gn_ref, win_ref, gq_ref, wuq_ref, gkv_ref, wukt_ref, wuv_ref, poolw_ref, pools_ref,
     convw_ref, wout_ref, gfin_ref) = (next(it) for _ in range(13))
    y_ref = next(it)
    if state_mode:
        ckv_out_ref, kr_out_ref = next(it), next(it)
    h_s, q_s, kt_s, v_s, px_s, prod_s, mixed_s = (next(it) for _ in range(7))

    rc = ROW_CHUNK
    m = n_seq * seq_len
    keys = cache_len + seq_len
    padded = seq_len + 2 * HALO
    row = slice(layer, layer + 1)
    lane = lax.broadcasted_iota(jnp.int32, (1, ROPE_PAD), 1)

    mrow = (1 + pl.program_id(0)) if has_rope else 0
    shift, scale, gate = (mod_ref[k, pl.ds(mrow, 1), :] for k in range(3))

    zeros_halo = jnp.zeros((HALO, POOL_WIDTH), _F32)
    for buf in (px_s, prod_s):
        for s in range(n_seq):
            buf[s * padded:s * padded + HALO, :] = zeros_halo
            buf[(s + 1) * padded - HALO:(s + 1) * padded, :] = zeros_halo
    ones_col = jnp.where(lax.broadcasted_iota(jnp.int32, (n_seq * keys, V_DIM), 1) == 0, 1.0, 0.0).astype(_BF)
    for hd in range(MLA_HEADS):
        v_s[hd, :, V_DIM:] = ones_col
    if state_mode == "stack":
        ckv_out_ref[:, 0] = ckv_prev_ref[...]
        kr_out_ref[:, 0] = kr_prev_ref[...]

    def rotate(v, r):
        v = v * cs_ref[pl.ds(r, rc), :]
        return v + pltpu.roll(v, ROPE_DIM, axis=1)

    def put_kv(ckv, kr_t, kc):
        knt = _dot(wukt_ref[...], ckv.T.astype(_BF))
        v4 = _dot(ckv.astype(_BF), wuv_ref[...])
        kr_bf = kr_t.astype(_BF)
        rows_k = pl.ds(_aligned(kc * rc, rc), rc)
        for hd in range(MLA_HEADS):
            kt_s[hd, kc, :NOPE_DIM, :] = knt[hd * NOPE_DIM:(hd + 1) * NOPE_DIM, :].astype(_BF)
            kt_s[hd, kc, NOPE_DIM:, :] = kr_bf
            v_s[hd, rows_k, :V_DIM] = v4[:, hd * V_DIM:(hd + 1) * V_DIM].astype(_BF)

    if cache_len:
        ckr_t = ckr_ref[...]
        ckr_pad = jnp.concatenate([ckr_t, jnp.zeros_like(ckr_t)], axis=0)
        for cc in range(cache_len // rc):
            rows_k = slice(cc * rc, (cc + 1) * rc)
            put_kv(cckv_ref[rows_k, :], ckr_pad[:, rows_k], cc)

    def halo_base(c):
        return _aligned(c * padded if seq_len == rc else c * rc, HALO)

    def project(c):
        r = _aligned(c * rc, rc)
        rows = pl.ds(r, rc)
        h = (_rms(x_ref[rows, :], gn_ref[row, :]) * (1.0 + scale) + shift).astype(_BF)
        h_s[rows, :] = h
        a = _dot(h, win_ref[:, _C_CQ:_C_GMLA])
        cq = _rms(a[:, :Q_LORA], gq_ref[row, :]).astype(_BF)
        ckv = _rms(a[:, _C_CKV:_C_KR], gkv_ref[row, :])
        kr = a[:, _C_KR:]
        if has_rope:
            kr = rotate(kr, r)
        kr_t = jnp.where(lane < ROPE_DIM, kr, 0.0).T
        if state_mode:
            own = () if state_mode == "own" else (layer,)
            ckv_out_ref[(c,) + own] = ckv
            kr_out_ref[(c,) + own] = kr_t[:ROPE_DIM, :]
        put_kv(ckv, kr_t, cache_len // rc + c)
        for hp in range(MLA_HEADS // 2):
            q2 = _dot(cq, wuq_ref[:, hp * 2 * HEAD_PAD:(hp + 1) * 2 * HEAD_PAD]) * Q_SCALE
            for j in range(2):
                hd = 2 * hp + j
                c0 = j * HEAD_PAD
                q_s[hd, rows, :NOPE_DIM] = q2[:, c0:c0 + NOPE_DIM].astype(_BF)
                qr = q2[:, c0 + NOPE_DIM:c0 + HEAD_PAD]
                if has_rope:
                    qr = rotate(qr, r)
                q_s[hd, rows, NOPE_DIM:] = qr.astype(_BF)
        pe = _dot(h, win_ref[:, _C_PX:_C_CB])
        rows_h = pl.ds(_aligned(halo_base(c) + HALO, HALO), rc)
        px_s[rows_h, :] = pe[:, :POOL_WIDTH]
        prod_s[rows_h, :] = pe[:, POOL_WIDTH:POOL_WIDTH + CONV_WIDTH] * pe[:, POOL_WIDTH + CONV_WIDTH:]

    _for_chunks(m // rc, unroll, project)

    def mix(c):
        r = _aligned(c * rc, rc)
        rows = pl.ds(r, rc)
        h = h_s[rows, :]
        key_chunks = range(keys // rc) if n_seq == 1 else [c]
        rows_k = pl.ds(0 if n_seq == 1 else r, keys)
        for hp in range(MLA_HEADS // 2):
            g2 = _silu(_dot(h, win_ref[:, _C_GMLA + hp * 2 * V_DIM:_C_GMLA + (hp + 1) * 2 * V_DIM]))
            for j in range(2):
                hd = 2 * hp + j
                q = q_s[hd, rows, :]
                sc = jnp.concatenate([_dot(q, kt_s[hd, kc]) for kc in key_chunks], axis=-1)
                p = jnp.exp2(sc - jnp.max(sc, axis=-1, keepdims=True))
                ov = _dot(p.astype(_BF), v_s[hd, rows_k, :])
                o = ov[:, :V_DIM] / ov[:, V_DIM:V_DIM + 1]
                mixed_s[rows, hd * V_DIM:(hd + 1) * V_DIM] = (g2[:, j * V_DIM:(j + 1) * V_DIM] * o).astype(_BF)

        rows_w = pl.ds(halo_base(c), rc + 2 * HALO)
        pooled = _pool_mix(px_s[rows_w, :], rcnt_ref[pl.ds(0 if seq_len == rc else r, rc), :])
        pool = _dot(pooled.astype(_BF), poolw_ref[...]) * pools_ref[row, :]
        gp = _silu(_dot(h, win_ref[:, _C_GPOOL:_C_PX]))
        mixed_s[rows, MLA_WIDTH:MLA_WIDTH + POOL_WIDTH] = (gp * pool).astype(_BF)

        convw = [convw_ref[k, row, :] for k in range(3)]
        conv = _short_conv(prod_s[rows_w, :], convw)
        e2 = _dot(h, win_ref[:, _C_CB:_C_END])
        mixed_s[rows, MLA_WIDTH + POOL_WIDTH:] = (_silu(e2[:, CONV_WIDTH:]) * (e2[:, :CONV_WIDTH] * conv)).astype(_BF)

        y = x_ref[rows, :] + gate * _dot(mixed_s[rows, :], wout_ref[...])
        if final_norm:
            y = _rms(y, gfin_ref[...])
        y_ref[rows, :] = y

    _for_chunks(m // rc, unroll, mix)


def _mixer_layer(x2d, mod_all, layer, wts, *, seq_len, cache=None, rope=None, prev_state=None,
                 final_norm, state_mode, after=None):
    rows = x2d.shape[0]
    m = ROWS_PER_STEP
    n_seq = m // seq_len
    n_steps = rows // m
    n_all = rows // seq_len
    cache_len = 0 if cache is None else cache[0].shape[2]
    keys = cache_len + seq_len
    has_rope = rope is not None
    assert seq_len % ROW_CHUNK == 0 and cache_len % ROW_CHUNK == 0
    assert (seq_len == ROW_CHUNK and not cache_len) or (n_seq == 1 and not state_mode)

    def const(*shape):
        return pl.BlockSpec(shape, lambda i: (0,) * len(shape), pipeline_mode=pl.Buffered(1))

    def of_layer(*shape):
        return pl.BlockSpec((None,) + shape, lambda i: (layer,) + (0,) * len(shape),
                            pipeline_mode=pl.Buffered(1))

    args = [x2d, mod_all]
    in_specs = [pl.BlockSpec((m, D_MODEL), lambda i: (i, 0)), of_layer(3, MOD_ROWS, D_MODEL)]
    if after is not None:
        args += [after]
        in_specs += [pl.BlockSpec(memory_space=pl.ANY)]
    if cache_len:
        args += [cache[0], cache[1]]
        in_specs += [pl.BlockSpec((None, None, cache_len, KV_LORA), lambda i: (i, layer, 0, 0)),
                     pl.BlockSpec((None, None, ROPE_DIM, cache_len), lambda i: (i, layer, 0, 0))]
    if has_rope:
        args += [rope]
        in_specs += [const(seq_len, ROPE_PAD)]
    if state_mode == "stack":
        args += list(prev_state)
        in_specs += [pl.BlockSpec((n_seq, seq_len, KV_LORA), lambda i: (i, 0, 0)),
                     pl.BlockSpec((n_seq, ROPE_DIM, seq_len), lambda i: (i, 0, 0))]
    args += [_pool_rcnt(seq_len), wts["g_norm"], wts["w_in"], wts["g_q"], wts["w_uq"], wts["g_kv"],
             wts["w_ukt"], wts["w_uv"], wts["pool_w"], wts["pool_s"], wts["conv_w"], wts["w_out"],
             wts["g_final"]]
    in_specs += [const(seq_len, POOL_WIDTH), const(DEPTH, D_MODEL), of_layer(D_MODEL, _C_END),
                 const(DEPTH, Q_LORA), of_layer(Q_LORA, MLA_HEADS * HEAD_PAD), const(DEPTH, KV_LORA),
                 of_layer(MLA_HEADS * NOPE_DIM, KV_LORA), of_layer(KV_LORA, MLA_WIDTH),
                 of_layer(POOL_WIDTH, POOL_WIDTH), const(DEPTH, POOL_WIDTH), const(3, DEPTH, CONV_WIDTH),
                 of_layer(MIX_WIDTH, D_MODEL), const(1, D_MODEL)]

    out_shape = [jax.ShapeDtypeStruct((rows, D_MODEL), _F32)]
    out_specs = [pl.BlockSpec((m, D_MODEL), lambda i: (i, 0))]
    if state_mode == "own":
        out_shape += [jax.ShapeDtypeStruct((n_all, seq_len, KV_LORA), _F32),
                      jax.ShapeDtypeStruct((n_all, ROPE_DIM, seq_len), _F32)]
        out_specs += [pl.BlockSpec((n_seq, seq_len, KV_LORA), lambda i: (i, 0, 0)),
                      pl.BlockSpec((n_seq, ROPE_DIM, seq_len), lambda i: (i, 0, 0))]
    elif state_mode == "stack":
        out_shape += [jax.ShapeDtypeStruct((n_all, DEPTH, seq_len, KV_LORA), _F32),
                      jax.ShapeDtypeStruct((n_all, DEPTH, ROPE_DIM, seq_len), _F32)]
        out_specs += [pl.BlockSpec((n_seq, DEPTH, seq_len, KV_LORA), lambda i: (i, 0, 0, 0)),
                      pl.BlockSpec((n_seq, DEPTH, ROPE_DIM, seq_len), lambda i: (i, 0, 0, 0))]

    kern = functools.partial(_layer_kernel, layer=layer, n_seq=n_seq, seq_len=seq_len, cache_len=cache_len,
                             has_rope=has_rope, final_norm=final_norm, state_mode=state_mode,
                             ordered_after=after is not None, unroll=CHUNK_UNROLL)
    return pl.pallas_call(
        kern,
        grid=(n_steps,),
        in_specs=in_specs,
        out_specs=out_specs,
        out_shape=out_shape,
        scratch_shapes=[
            pltpu.VMEM((m, D_MODEL), _BF),
            pltpu.VMEM((MLA_HEADS, m, HEAD_PAD), _BF),
            pltpu.VMEM((MLA_HEADS, n_seq * keys // ROW_CHUNK, HEAD_PAD, ROW_CHUNK), _BF),
            pltpu.VMEM((MLA_HEADS, n_seq * keys, 2 * V_DIM), _BF),
            pltpu.VMEM((n_seq * (seq_len + 2 * HALO), POOL_WIDTH), _F32),
            pltpu.VMEM((n_seq * (seq_len + 2 * HALO), CONV_WIDTH), _F32),
            pltpu.VMEM((m, MIX_WIDTH), _BF),
        ],
        compiler_params=pltpu.CompilerParams(dimension_semantics=("arbitrary",),
                                             vmem_limit_bytes=VMEM_LIMIT_BYTES),
        name="mixer_layer_latent" if has_rope else "mixer_layer_context",
    )(*args)


def _pool_rcnt(seq_len):
    t = np.arange(seq_len)[:, None]
    half = np.repeat(1 << np.arange(POOL_GROUPS), POOL_GROUP_DIM)[None, :]
    cnt = np.minimum(t + half, seq_len) - np.maximum(t - half, 0)
    return jnp.asarray((1.0 / cnt).astype(np.float32))


def _rope_table(seq_len):
    rows = seq_len // GRID_W
    row = np.repeat(np.arange(rows), GRID_W).astype(np.float32)
    col = np.tile(np.arange(GRID_W), rows).astype(np.float32)
    inv = (1.0 / (np.float32(ROPE_BASE) ** (np.arange(0, AXIS_DIM, 2, dtype=np.float32) / np.float32(AXIS_DIM))))
    inv = inv.astype(np.float32)
    ang = np.concatenate([row[:, None] * inv, col[:, None] * inv], axis=-1).astype(np.float64)
    cos, sin = np.cos(ang).astype(np.float32), np.sin(ang).astype(np.float32)
    return jnp.asarray(np.concatenate([cos, cos, sin, sin], axis=-1))


def kernel(x_prompt, x_sample, cache_ckv, cache_krope, c, c_ctx, w_mod, b_mod, g_norm, w_in, g_q, w_uq,
           g_kv, w_ukv, pool_w, pool_s, conv_w, w_out, g_final):
    batch, seq, _ = x_prompt.shape
    dec_batch, dec_seq, _ = x_sample.shape
    assert 1 + dec_batch <= MOD_ROWS and dec_seq == ROWS_PER_STEP and ROWS_PER_STEP % seq == 0 and DEPTH == 2

    c_all = jnp.concatenate([c_ctx[None], c, jnp.zeros((MOD_ROWS - 1 - dec_batch, D_MODEL), _F32)], axis=0)
    mod_all = _modulation(c_all, w_mod, b_mod)
    w_in_r, w_out_r, w_uq_r, w_ukt, w_uv, pool_w_r = _prep_weights(
        jnp.transpose(w_in, (0, 2, 1)), w_out, w_uq, w_ukv, pool_w)
    wts = {
        "g_norm": g_norm, "w_in": w_in_r, "g_q": g_q, "w_uq": w_uq_r, "g_kv": g_kv, "w_ukt": w_ukt,
        "w_uv": w_uv, "pool_w": pool_w_r, "pool_s": pool_s, "conv_w": jnp.transpose(conv_w, (1, 0, 2)), "w_out": w_out_r,
        "g_final": g_final.reshape(1, D_MODEL),
    }

    rope = _rope_table(dec_seq)
    cache = (cache_ckv, jnp.transpose(cache_krope, (0, 1, 3, 2)))
    xp = x_prompt.reshape(batch * seq, D_MODEL)
    xs = x_sample.reshape(dec_batch * dec_seq, D_MODEL)
    xp, ckv0, kr0 = _mixer_layer(xp, mod_all, 0, wts, seq_len=seq, final_norm=False, state_mode="own")
    (xs,) = _mixer_layer(xs, mod_all, 0, wts, seq_len=dec_seq, cache=cache, rope=rope, final_norm=False,
                         state_mode=None, after=kr0)
    xp, state_ckv, state_krope_t = _mixer_layer(xp, mod_all, 1, wts, seq_len=seq, prev_state=(ckv0, kr0),
                                                final_norm=True, state_mode="stack", after=xs)
    (xs,) = _mixer_layer(xs, mod_all, 1, wts, seq_len=dec_seq, cache=cache, rope=rope, final_norm=True,
                         state_mode=None, after=state_krope_t)
    y_prompt = xp.reshape(batch, seq, D_MODEL)
    state_krope = jnp.transpose(state_krope_t, (0, 1, 3, 2))
    y_sample = xs.reshape(dec_batch, dec_seq, D_MODEL)
    return (y_prompt, y_sample, state_ckv, state_krope)
```

```python
import functools

import numpy as np
import jax
import jax.numpy as jnp
from jax import lax
from jax.experimental import pallas as pl
from jax.experimental.pallas import tpu as pltpu

D_MODEL = 1024
DEPTH = 2
GRID_W = 64
MLA_HEADS = 4
NOPE_DIM = 128
ROPE_DIM = 64
V_DIM = 128
QK_DIM = NOPE_DIM + ROPE_DIM
MLA_WIDTH = MLA_HEADS * V_DIM
Q_LORA = 384
KV_LORA = 256
POOL_GROUPS = 4
POOL_GROUP_DIM = 64
POOL_WIDTH = POOL_GROUPS * POOL_GROUP_DIM
CONV_WIDTH = 256
MIX_WIDTH = MLA_WIDTH + POOL_WIDTH + CONV_WIDTH
ROPE_BASE = 10000.0
AXIS_DIM = ROPE_DIM // 2
ATTN_SCALE = QK_DIM ** -0.5
Q_SCALE = ATTN_SCALE * float(np.log2(np.e))
EPS = 1e-6

_SPLITS = (Q_LORA, KV_LORA, ROPE_DIM, MLA_WIDTH, POOL_WIDTH, POOL_WIDTH,
           CONV_WIDTH, CONV_WIDTH, CONV_WIDTH, CONV_WIDTH)
_OFFS = [sum(_SPLITS[:i]) for i in range(len(_SPLITS) + 1)]
IN_WIDTH = _OFFS[-1]

V7X_LANES = 128
V7X_SUBLANES = 8
V7X_MXU_DEPTH = 256
HEAD_PAD = V7X_MXU_DEPTH
ROPE_PAD = HEAD_PAD - NOPE_DIM
KV_HEAD = NOPE_DIM + V_DIM

_C_CQ = 0
_C_CKV = _C_CQ + Q_LORA
_C_KR = _C_CKV + KV_LORA
_C_GMLA = _C_KR + ROPE_PAD
_C_GPOOL = _C_GMLA + MLA_WIDTH
_C_PX = _C_GPOOL + POOL_WIDTH
_C_CC = _C_PX + POOL_WIDTH
_C_CB = _C_CC + 2 * CONV_WIDTH
_C_END = _C_CB + 2 * CONV_WIDTH

MOD_ROWS = 8
ROWS_PER_STEP = 1024
MAX_ROW_CHUNK = 512
CHUNK_UNROLL = 4
HALO = V7X_SUBLANES
PREP_COLS = 256
VMEM_LIMIT_BYTES = 56 * 1024 * 1024

_BF = jnp.bfloat16
_F32 = jnp.float32


def _dot(a, b):
    return jnp.dot(a, b, preferred_element_type=_F32)


def _rms(x, g):
    return x * lax.rsqrt(jnp.mean(x * x, axis=-1, keepdims=True) + EPS) * g


def _silu(x):
    return x * jax.nn.sigmoid(x)


def _aligned(v, k):
    return v if isinstance(v, int) else pl.multiple_of(v, k)


def _for_chunks(n, unroll, body):
    if unroll >= n:
        for c in range(n):
            body(c)
        return

    def step(i, carry):
        for u in range(unroll):
            body(i * unroll + u)
        return carry

    lax.fori_loop(0, n // unroll, step, 0)


def _mod_kernel(c_ref, w_ref, b_ref, o_ref):
    s = _silu(c_ref[...])
    bias = b_ref[pl.ds(pl.program_id(0), 1), :]
    o_ref[...] = _dot(s.astype(_BF), w_ref[...].astype(_BF)) + bias


def _modulation(c_all, w_mod, b_mod):
    return pl.pallas_call(
        _mod_kernel,
        grid=(DEPTH, 3),
        in_specs=[
            pl.BlockSpec((MOD_ROWS, D_MODEL), lambda l, j: (0, 0)),
            pl.BlockSpec((None, D_MODEL, D_MODEL), lambda l, j: (l, 0, j)),
            pl.BlockSpec((DEPTH, D_MODEL), lambda l, j: (0, j)),
        ],
        out_specs=pl.BlockSpec((None, None, MOD_ROWS, D_MODEL), lambda l, j: (l, j, 0, 0)),
        out_shape=jax.ShapeDtypeStruct((DEPTH, 3, MOD_ROWS, D_MODEL), _F32),
        name="modulation",
    )(c_all, w_mod, b_mod)


def _prep_kernel(wint_ref, wout_ref, wuq_ref, wukv_ref, poolw_ref,
                 win_o, wout_o, wuq_o, wukt_o, wuv_o, poolw_o):
    def put(dst, src, width):
        for c in range(0, width, PREP_COLS):
            n = min(PREP_COLS, width - c)
            win_o[:, dst + c:dst + c + n] = wint_ref[src + c:src + c + n, :].T.astype(_BF)

    put(_C_CQ, _OFFS[0], Q_LORA + KV_LORA)
    kr = wint_ref[_OFFS[2]:_OFFS[3], :]
    krb = jnp.concatenate([kr, -kr[AXIS_DIM:], kr[:AXIS_DIM]], axis=0)
    win_o[:, _C_KR:_C_GMLA] = krb.T.astype(_BF)
    put(_C_GMLA, _OFFS[3], MLA_WIDTH)
    put(_C_GPOOL, _OFFS[5], POOL_WIDTH)
    put(_C_PX, _OFFS[4], POOL_WIDTH)
    put(_C_CC, _OFFS[7], 2 * CONV_WIDTH)
    put(_C_CB, _OFFS[6], CONV_WIDTH)
    put(_C_CB + CONV_WIDTH, _OFFS[9], CONV_WIDTH)
    wout_o[...] = wout_ref[...].astype(_BF)

    for hd in range(MLA_HEADS):
        s0, d0 = hd * QK_DIM, hd * HEAD_PAD
        r0 = s0 + NOPE_DIM
        wuq_o[:, d0:d0 + QK_DIM] = wuq_ref[:, s0:s0 + QK_DIM].astype(_BF)
        wuq_o[:, d0 + QK_DIM:d0 + QK_DIM + AXIS_DIM] = (-wuq_ref[:, r0 + AXIS_DIM:r0 + ROPE_DIM]).astype(_BF)
        wuq_o[:, d0 + QK_DIM + AXIS_DIM:d0 + HEAD_PAD] = wuq_ref[:, r0:r0 + AXIS_DIM].astype(_BF)
    for hd in range(MLA_HEADS):
        k0 = hd * KV_HEAD
        wukt_o[hd * NOPE_DIM:(hd + 1) * NOPE_DIM, :] = wukv_ref[:, k0:k0 + NOPE_DIM].T.astype(_BF)
        wuv_o[:, hd * V_DIM:(hd + 1) * V_DIM] = wukv_ref[:, k0 + NOPE_DIM:k0 + KV_HEAD].astype(_BF)
    poolw_o[...] = jnp.zeros((POOL_WIDTH, POOL_WIDTH), _BF)
    for g in range(POOL_GROUPS):
        sl = slice(g * POOL_GROUP_DIM, (g + 1) * POOL_GROUP_DIM)
        poolw_o[sl, sl] = poolw_ref[g].astype(_BF)


def _prep_weights(w_in_t, w_out, w_uq, w_ukv, pool_w):
    per_layer = lambda *shape: pl.BlockSpec((None,) + shape, lambda l: (l,) + (0,) * len(shape))
    return pl.pallas_call(
        _prep_kernel,
        grid=(DEPTH,),
        in_specs=[
            per_layer(IN_WIDTH, D_MODEL),
            per_layer(MIX_WIDTH, D_MODEL),
            per_layer(Q_LORA, MLA_HEADS * QK_DIM),
            per_layer(KV_LORA, MLA_HEADS * KV_HEAD),
            per_layer(POOL_GROUPS, POOL_GROUP_DIM, POOL_GROUP_DIM),
        ],
        out_specs=[
            per_layer(D_MODEL, _C_END),
            per_layer(MIX_WIDTH, D_MODEL),
            per_layer(Q_LORA, MLA_HEADS * HEAD_PAD),
            per_layer(MLA_HEADS * NOPE_DIM, KV_LORA),
            per_layer(KV_LORA, MLA_WIDTH),
            per_layer(POOL_WIDTH, POOL_WIDTH),
        ],
        out_shape=[
            jax.ShapeDtypeStruct((DEPTH, D_MODEL, _C_END), _BF),
            jax.ShapeDtypeStruct((DEPTH, MIX_WIDTH, D_MODEL), _BF),
            jax.ShapeDtypeStruct((DEPTH, Q_LORA, MLA_HEADS * HEAD_PAD), _BF),
            jax.ShapeDtypeStruct((DEPTH, MLA_HEADS * NOPE_DIM, KV_LORA), _BF),
            jax.ShapeDtypeStruct((DEPTH, KV_LORA, MLA_WIDTH), _BF),
            jax.ShapeDtypeStruct((DEPTH, POOL_WIDTH, POOL_WIDTH), _BF),
        ],
        compiler_params=pltpu.CompilerParams(dimension_semantics=("arbitrary",),
                                             vmem_limit_bytes=VMEM_LIMIT_BYTES),
        name="weight_prep",
    )(w_in_t, w_out, w_uq, w_ukv, pool_w)


def _pool_mix(win, rcnt):
    n = win.shape[0]
    rows = n - 2 * HALO
    lane = lax.broadcasted_iota(jnp.int32, (1, POOL_WIDTH), 1)
    sums = [win + pltpu.roll(win, 1, axis=0)]
    for k in (1, 2, 4):
        sums.append(pltpu.roll(sums[-1], k, axis=0) + pltpu.roll(sums[-1], n - k, axis=0))
    sel = sums[-1]
    for g in range(POOL_GROUPS - 2, -1, -1):
        sel = jnp.where(lane < (g + 1) * POOL_GROUP_DIM, sums[g], sel)
    return sel[HALO:HALO + rows] * rcnt - win[HALO:HALO + rows]


def _short_conv(win, w):
    rows = win.shape[0] - 2 * HALO
    return (win[HALO - 1:HALO - 1 + rows] * w[0] + win[HALO:HALO + rows] * w[1]
            + win[HALO + 1:HALO + 1 + rows] * w[2])


def _layer_kernel(*refs, layer, n_seq, seq_len, cache_len, has_rope, final_norm, state_mode, ordered_after,
                  unroll, rc):
    it = iter(refs)
    x_ref, mod_ref = next(it), next(it)
    if ordered_after:
        next(it)
    if cache_len:
        cckv_ref, ckr_ref = next(it), next(it)
    if has_rope:
        cs_ref = next(it)
    if state_mode == "stack":
        ckv_prev_ref, kr_prev_ref = next(it), next(it)
    (rcnt_ref, gn_ref, win_ref, gq_ref, wuq_ref, gkv_ref, wukt_ref, wuv_ref, poolw_ref, pools_ref,
     convw_ref, wout_ref, gfin_ref) = (next(it) for _ in range(13))
    y_ref = next(it)
    if state_mode:
        ckv_out_ref, kr_out_ref = next(it), next(it)
    h_s, q_s, kt_s, v_s, px_s, prod_s, mixed_s = (next(it) for _ in range(7))

    m = n_seq * seq_len
    keys = cache_len + seq_len
    padded = seq_len + 2 * HALO
    row = slice(layer, layer + 1)
    lane = lax.broadcasted_iota(jnp.int32, (1, ROPE_PAD), 1)

    mrow = (1 + pl.program_id(0)) if has_rope else 0
    shift, scale, gate = (mod_ref[k, pl.ds(mrow, 1), :] for k in range(3))

    zeros_halo = jnp.zeros((HALO, POOL_WIDTH), _F32)
    for buf in (px_s, prod_s):
        for s in range(n_seq):
            buf[s * padded:s * padded + HALO, :] = zeros_halo
            buf[(s + 1) * padded - HALO:(s + 1) * padded, :] = zeros_halo
    ones_col = jnp.where(lax.broadcasted_iota(jnp.int32, (n_seq * keys, V_DIM), 1) == 0, 1.0, 0.0).astype(_BF)
    for hd in range(MLA_HEADS):
        v_s[hd, :, V_DIM:] = ones_col
    if state_mode == "stack":
        ckv_out_ref[:, 0] = ckv_prev_ref[...]
        kr_out_ref[:, 0] = kr_prev_ref[...]

    def rotate(v, r):
        v = v * cs_ref[pl.ds(r, rc), :]
        return v + pltpu.roll(v, ROPE_DIM, axis=1)

    def put_kv(ckv, kr_t, kc):
        knt = _dot(wukt_ref[...], ckv.T.astype(_BF))
        v4 = _dot(ckv.astype(_BF), wuv_ref[...])
        kr_bf = kr_t.astype(_BF)
        rows_k = pl.ds(_aligned(kc * rc, rc), rc)
        for hd in range(MLA_HEADS):
            kt_s[hd, kc, :NOPE_DIM, :] = knt[hd * NOPE_DIM:(hd + 1) * NOPE_DIM, :].astype(_BF)
            kt_s[hd, kc, NOPE_DIM:, :] = kr_bf
            v_s[hd, rows_k, :V_DIM] = v4[:, hd * V_DIM:(hd + 1) * V_DIM].astype(_BF)

    if cache_len:
        ckr_t = ckr_ref[...]
        ckr_pad = jnp.concatenate([ckr_t, jnp.zeros_like(ckr_t)], axis=0)
        for cc in range(cache_len // rc):
            rows_k = slice(cc * rc, (cc + 1) * rc)
            put_kv(cckv_ref[rows_k, :], ckr_pad[:, rows_k], cc)

    def halo_base(c):
        return _aligned(c * padded if seq_len == rc else c * rc, HALO)

    def project(c):
        r = _aligned(c * rc, rc)
        rows = pl.ds(r, rc)
        h = (_rms(x_ref[rows, :], gn_ref[row, :]) * (1.0 + scale) + shift).astype(_BF)
        h_s[rows, :] = h
        a = _dot(h, win_ref[:, _C_CQ:_C_GMLA])
        cq = _rms(a[:, :Q_LORA], gq_ref[row, :]).astype(_BF)
        ckv = _rms(a[:, _C_CKV:_C_KR], gkv_ref[row, :])
        kr = a[:, _C_KR:]
        if has_rope:
            kr = rotate(kr, r)
        kr_t = jnp.where(lane < ROPE_DIM, kr, 0.0).T
        if state_mode:
            own = () if state_mode == "own" else (layer,)
            ckv_out_ref[(c,) + own] = ckv
            kr_out_ref[(c,) + own] = kr_t[:ROPE_DIM, :]
        put_kv(ckv, kr_t, cache_len // rc + c)
        for hp in range(MLA_HEADS // 2):
            q2 = _dot(cq, wuq_ref[:, hp * 2 * HEAD_PAD:(hp + 1) * 2 * HEAD_PAD]) * Q_SCALE
            for j in range(2):
                hd = 2 * hp + j
                c0 = j * HEAD_PAD
                q_s[hd, rows, :NOPE_DIM] = q2[:, c0:c0 + NOPE_DIM].astype(_BF)
                qr = q2[:, c0 + NOPE_DIM:c0 + HEAD_PAD]
                if has_rope:
                    qr = rotate(qr, r)
                q_s[hd, rows, NOPE_DIM:] = qr.astype(_BF)
        pe = _dot(h, win_ref[:, _C_PX:_C_CB])
        rows_h = pl.ds(_aligned(halo_base(c) + HALO, HALO), rc)
        px_s[rows_h, :] = pe[:, :POOL_WIDTH]
        prod_s[rows_h, :] = pe[:, POOL_WIDTH:POOL_WIDTH + CONV_WIDTH] * pe[:, POOL_WIDTH + CONV_WIDTH:]

    _for_chunks(m // rc, unroll, project)

    def mix(c):
        r = _aligned(c * rc, rc)
        rows = pl.ds(r, rc)
        h = h_s[rows, :]
        key_chunks = range(keys // rc) if n_seq == 1 else [c]
        rows_k = pl.ds(0 if n_seq == 1 else r, keys)
        for hp in range(MLA_HEADS // 2):
            g2 = _silu(_dot(h, win_ref[:, _C_GMLA + hp * 2 * V_DIM:_C_GMLA + (hp + 1) * 2 * V_DIM]))
            for j in range(2):
                hd = 2 * hp + j
                q = q_s[hd, rows, :]
                sc = jnp.concatenate([_dot(q, kt_s[hd, kc]) for kc in key_chunks], axis=-1)
                p = jnp.exp2(sc - jnp.max(sc, axis=-1, keepdims=True))
                ov = _dot(p.astype(_BF), v_s[hd, rows_k, :])
                o = ov[:, :V_DIM] / ov[:, V_DIM:V_DIM + 1]
                mixed_s[rows, hd * V_DIM:(hd + 1) * V_DIM] = (g2[:, j * V_DIM:(j + 1) * V_DIM] * o).astype(_BF)

        rows_w = pl.ds(halo_base(c), rc + 2 * HALO)
        pooled = _pool_mix(px_s[rows_w, :], rcnt_ref[pl.ds(0 if seq_len == rc else r, rc), :])
        pool = _dot(pooled.astype(_BF), poolw_ref[...]) * pools_ref[row, :]
        gp = _silu(_dot(h, win_ref[:, _C_GPOOL:_C_PX]))
        mixed_s[rows, MLA_WIDTH:MLA_WIDTH + POOL_WIDTH] = (gp * pool).astype(_BF)

        convw = [convw_ref[k, row, :] for k in range(3)]
        conv = _short_conv(prod_s[rows_w, :], convw)
        e2 = _dot(h, win_ref[:, _C_CB:_C_END])
        mixed_s[rows, MLA_WIDTH + POOL_WIDTH:] = (_silu(e2[:, CONV_WIDTH:]) * (e2[:, :CONV_WIDTH] * conv)).astype(_BF)

        y = x_ref[rows, :] + gate * _dot(mixed_s[rows, :], wout_ref[...])
        if final_norm:
            y = _rms(y, gfin_ref[...])
        y_ref[rows, :] = y

    _for_chunks(m // rc, unroll, mix)


def _mixer_layer(x2d, mod_all, layer, wts, *, seq_len, cache=None, rope=None, prev_state=None,
                 final_norm, state_mode, after=None):
    rows = x2d.shape[0]
    m = ROWS_PER_STEP
    n_seq = m // seq_len
    n_steps = rows // m
    n_all = rows // seq_len
    cache_len = 0 if cache is None else cache[0].shape[2]
    keys = cache_len + seq_len
    has_rope = rope is not None
    rc = min(seq_len, MAX_ROW_CHUNK)
    assert seq_len % rc == 0 and cache_len % rc == 0
    assert (seq_len == rc and not cache_len) or (n_seq == 1 and not state_mode)

    def const(*shape):
        return pl.BlockSpec(shape, lambda i: (0,) * len(shape), pipeline_mode=pl.Buffered(1))

    def of_layer(*shape):
        return pl.BlockSpec((None,) + shape, lambda i: (layer,) + (0,) * len(shape),
                            pipeline_mode=pl.Buffered(1))

    args = [x2d, mod_all]
    in_specs = [pl.BlockSpec((m, D_MODEL), lambda i: (i, 0)), of_layer(3, MOD_ROWS, D_MODEL)]
    if after is not None:
        args += [after]
        in_specs += [pl.BlockSpec(memory_space=pl.ANY)]
    if cache_len:
        args += [cache[0], cache[1]]
        in_specs += [pl.BlockSpec((None, None, cache_len, KV_LORA), lambda i: (i, layer, 0, 0)),
                     pl.BlockSpec((None, None, ROPE_DIM, cache_len), lambda i: (i, layer, 0, 0))]
    if has_rope:
        args += [rope]
        in_specs += [const(seq_len, ROPE_PAD)]
    if state_mode == "stack":
        args += list(prev_state)
        in_specs += [pl.BlockSpec((n_seq, seq_len, KV_LORA), lambda i: (i, 0, 0)),
                     pl.BlockSpec((n_seq, ROPE_DIM, seq_len), lambda i: (i, 0, 0))]
    args += [_pool_rcnt(seq_len), wts["g_norm"], wts["w_in"], wts["g_q"], wts["w_uq"], wts["g_kv"],
             wts["w_ukt"], wts["w_uv"], wts["pool_w"], wts["pool_s"], wts["conv_w"], wts["w_out"],
             wts["g_final"]]
    in_specs += [const(seq_len, POOL_WIDTH), const(DEPTH, D_MODEL), of_layer(D_MODEL, _C_END),
                 const(DEPTH, Q_LORA), of_layer(Q_LORA, MLA_HEADS * HEAD_PAD), const(DEPTH, KV_LORA),
                 of_layer(MLA_HEADS * NOPE_DIM, KV_LORA), of_layer(KV_LORA, MLA_WIDTH),
                 of_layer(POOL_WIDTH, POOL_WIDTH), const(DEPTH, POOL_WIDTH), const(3, DEPTH, CONV_WIDTH),
                 of_layer(MIX_WIDTH, D_MODEL), const(1, D_MODEL)]

    out_shape = [jax.ShapeDtypeStruct((rows, D_MODEL), _F32)]
    out_specs = [pl.BlockSpec((m, D_MODEL), lambda i: (i, 0))]
    if state_mode == "own":
        out_shape += [jax.ShapeDtypeStruct((n_all, seq_len, KV_LORA), _F32),
                      jax.ShapeDtypeStruct((n_all, ROPE_DIM, seq_len), _F32)]
        out_specs += [pl.BlockSpec((n_seq, seq_len, KV_LORA), lambda i: (i, 0, 0)),
                      pl.BlockSpec((n_seq, ROPE_DIM, seq_len), lambda i: (i, 0, 0))]
    elif state_mode == "stack":
        out_shape += [jax.ShapeDtypeStruct((n_all, DEPTH, seq_len, KV_LORA), _F32),
                      jax.ShapeDtypeStruct((n_all, DEPTH, ROPE_DIM, seq_len), _F32)]
        out_specs += [pl.BlockSpec((n_seq, DEPTH, seq_len, KV_LORA), lambda i: (i, 0, 0, 0)),
                      pl.BlockSpec((n_seq, DEPTH, ROPE_DIM, seq_len), lambda i: (i, 0, 0, 0))]

    kern = functools.partial(_layer_kernel, layer=layer, n_seq=n_seq, seq_len=seq_len, cache_len=cache_len,
                             has_rope=has_rope, final_norm=final_norm, state_mode=state_mode,
                             ordered_after=after is not None, unroll=CHUNK_UNROLL, rc=rc)
    return pl.pallas_call(
        kern,
        grid=(n_steps,),
        in_specs=in_specs,
        out_specs=out_specs,
        out_shape=out_shape,
        scratch_shapes=[
            pltpu.VMEM((m, D_MODEL), _BF),
            pltpu.VMEM((MLA_HEADS, m, HEAD_PAD), _BF),
            pltpu.VMEM((MLA_HEADS, n_seq * keys // rc, HEAD_PAD, rc), _BF),
            pltpu.VMEM((MLA_HEADS, n_seq * keys, 2 * V_DIM), _BF),
            pltpu.VMEM((n_seq * (seq_len + 2 * HALO), POOL_WIDTH), _F32),
            pltpu.VMEM((n_seq * (seq_len + 2 * HALO), CONV_WIDTH), _F32),
            pltpu.VMEM((m, MIX_WIDTH), _BF),
        ],
        compiler_params=pltpu.CompilerParams(dimension_semantics=("arbitrary",),
                                             vmem_limit_bytes=VMEM_LIMIT_BYTES),
        name="mixer_layer_latent" if has_rope else "mixer_layer_context",
    )(*args)


def _pool_rcnt(seq_len):
    t = np.arange(seq_len)[:, None]
    half = np.repeat(1 << np.arange(POOL_GROUPS), POOL_GROUP_DIM)[None, :]
    cnt = np.minimum(t + half, seq_len) - np.maximum(t - half, 0)
    return jnp.asarray((1.0 / cnt).astype(np.float32))


def _rope_table(seq_len):
    rows = seq_len // GRID_W
    row = np.repeat(np.arange(rows), GRID_W).astype(np.float32)
    col = np.tile(np.arange(GRID_W), rows).astype(np.float32)
    inv = (1.0 / (np.float32(ROPE_BASE) ** (np.arange(0, AXIS_DIM, 2, dtype=np.float32) / np.float32(AXIS_DIM))))
    inv = inv.astype(np.float32)
    ang = np.concatenate([row[:, None] * inv, col[:, None] * inv], axis=-1).astype(np.float64)
    cos, sin = np.cos(ang).astype(np.float32), np.sin(ang).astype(np.float32)
    return jnp.asarray(np.concatenate([cos, cos, sin, sin], axis=-1))


def kernel(x_prompt, x_sample, cache_ckv, cache_krope, c, c_ctx, w_mod, b_mod, g_norm, w_in, g_q, w_uq,
           g_kv, w_ukv, pool_w, pool_s, conv_w, w_out, g_final):
    batch, seq, _ = x_prompt.shape
    dec_batch, dec_seq, _ = x_sample.shape
    assert 1 + dec_batch <= MOD_ROWS and dec_seq == ROWS_PER_STEP and ROWS_PER_STEP % seq == 0 and DEPTH == 2

    c_all = jnp.concatenate([c_ctx[None], c, jnp.zeros((MOD_ROWS - 1 - dec_batch, D_MODEL), _F32)], axis=0)
    mod_all = _modulation(c_all, w_mod, b_mod)
    w_in_r, w_out_r, w_uq_r, w_ukt, w_uv, pool_w_r = _prep_weights(
        jnp.transpose(w_in, (0, 2, 1)), w_out, w_uq, w_ukv, pool_w)
    wts = {
        "g_norm": g_norm, "w_in": w_in_r, "g_q": g_q, "w_uq": w_uq_r, "g_kv": g_kv, "w_ukt": w_ukt,
        "w_uv": w_uv, "pool_w": pool_w_r, "pool_s": pool_s, "conv_w": jnp.transpose(conv_w, (1, 0, 2)), "w_out": w_out_r,
        "g_final": g_final.reshape(1, D_MODEL),
    }

    rope = _rope_table(dec_seq)
    cache = (cache_ckv, jnp.transpose(cache_krope, (0, 1, 3, 2)))
    xp = x_prompt.reshape(batch * seq, D_MODEL)
    xs = x_sample.reshape(dec_batch * dec_seq, D_MODEL)
    xp, ckv0, kr0 = _mixer_layer(xp, mod_all, 0, wts, seq_len=seq, final_norm=False, state_mode="own")
    (xs,) = _mixer_layer(xs, mod_all, 0, wts, seq_len=dec_seq, cache=cache, rope=rope, final_norm=False,
                         state_mode=None, after=kr0)
    xp, state_ckv, state_krope_t = _mixer_layer(xp, mod_all, 1, wts, seq_len=seq, prev_state=(ckv0, kr0),
                                                final_norm=True, state_mode="stack", after=xs)
    (xs,) = _mixer_layer(xs, mod_all, 1, wts, seq_len=dec_seq, cache=cache, rope=rope, final_norm=True,
                         state_mode=None, after=state_krope_t)
    y_prompt = xp.reshape(batch, seq, D_MODEL)
    state_krope = jnp.transpose(state_krope_t, (0, 1, 3, 2))
    y_sample = xs.reshape(dec_batch, dec_seq, D_MODEL)
    return (y_prompt, y_sample, state_ckv, state_krope)
```

```python
import functools

import numpy as np
import jax
import jax.numpy as jnp
from jax import lax
from jax.experimental import pallas as pl
from jax.experimental.pallas import tpu as pltpu

D_MODEL = 1024
DEPTH = 2
GRID_W = 64
MLA_HEADS = 4
NOPE_DIM = 128
ROPE_DIM = 64
V_DIM = 128
QK_DIM = NOPE_DIM + ROPE_DIM
MLA_WIDTH = MLA_HEADS * V_DIM
Q_LORA = 384
KV_LORA = 256
POOL_GROUPS = 4
POOL_GROUP_DIM = 64
POOL_WIDTH = POOL_GROUPS * POOL_GROUP_DIM
CONV_WIDTH = 256
MIX_WIDTH = MLA_WIDTH + POOL_WIDTH + CONV_WIDTH
ROPE_BASE = 10000.0
AXIS_DIM = ROPE_DIM // 2
ATTN_SCALE = QK_DIM ** -0.5
Q_SCALE = ATTN_SCALE * float(np.log2(np.e))
EPS = 1e-6

_SPLITS = (Q_LORA, KV_LORA, ROPE_DIM, MLA_WIDTH, POOL_WIDTH, POOL_WIDTH,
           CONV_WIDTH, CONV_WIDTH, CONV_WIDTH, CONV_WIDTH)
_OFFS = [sum(_SPLITS[:i]) for i in range(len(_SPLITS) + 1)]
IN_WIDTH = _OFFS[-1]

V7X_LANES = 128
V7X_SUBLANES = 8
V7X_MXU_DEPTH = 256
HEAD_PAD = V7X_MXU_DEPTH
ROPE_PAD = HEAD_PAD - NOPE_DIM
KV_HEAD = NOPE_DIM + V_DIM

_C_CQ = 0
_C_CKV = _C_CQ + Q_LORA
_C_KR = _C_CKV + KV_LORA
_C_GMLA = _C_KR + ROPE_PAD
_C_GPOOL = _C_GMLA + MLA_WIDTH
_C_PX = _C_GPOOL + POOL_WIDTH
_C_CC = _C_PX + POOL_WIDTH
_C_CB = _C_CC + 2 * CONV_WIDTH
_C_END = _C_CB + 2 * CONV_WIDTH

MOD_ROWS = 8
ROWS_PER_STEP = 1024
MAX_ROW_CHUNK = 512
CHUNK_UNROLL = 4
HALO = V7X_SUBLANES
PREP_COLS = 256
VMEM_LIMIT_BYTES = 60 * 1024 * 1024

_BF = jnp.bfloat16
_F32 = jnp.float32


def _dot(a, b):
    return jnp.dot(a, b, preferred_element_type=_F32)


def _rms(x, g):
    return x * lax.rsqrt(jnp.mean(x * x, axis=-1, keepdims=True) + EPS) * g


def _silu(x):
    return x * jax.nn.sigmoid(x)


def _aligned(v, k):
    return v if isinstance(v, int) else pl.multiple_of(v, k)


def _for_chunks(n, unroll, body):
    if unroll >= n:
        for c in range(n):
            body(c)
        return

    def step(i, carry):
        for u in range(unroll):
            body(i * unroll + u)
        return carry

    lax.fori_loop(0, n // unroll, step, 0)


def _mod_kernel(c_ref, w_ref, b_ref, o_ref):
    s = _silu(c_ref[...])
    bias = b_ref[pl.ds(pl.program_id(0), 1), :]
    o_ref[...] = _dot(s.astype(_BF), w_ref[...].astype(_BF)) + bias


def _modulation(c_all, w_mod, b_mod):
    return pl.pallas_call(
        _mod_kernel,
        grid=(DEPTH, 3),
        in_specs=[
            pl.BlockSpec((MOD_ROWS, D_MODEL), lambda l, j: (0, 0)),
            pl.BlockSpec((None, D_MODEL, D_MODEL), lambda l, j: (l, 0, j)),
            pl.BlockSpec((DEPTH, D_MODEL), lambda l, j: (0, j)),
        ],
        out_specs=pl.BlockSpec((None, None, MOD_ROWS, D_MODEL), lambda l, j: (l, j, 0, 0)),
        out_shape=jax.ShapeDtypeStruct((DEPTH, 3, MOD_ROWS, D_MODEL), _F32),
        name="modulation",
    )(c_all, w_mod, b_mod)


def _prep_kernel(wint_ref, wout_ref, wuq_ref, wukv_ref, poolw_ref,
                 win_o, wout_o, wuq_o, wukt_o, wuv_o, poolw_o):
    def put(dst, src, width):
        for c in range(0, width, PREP_COLS):
            n = min(PREP_COLS, width - c)
            win_o[:, dst + c:dst + c + n] = wint_ref[src + c:src + c + n, :].T.astype(_BF)

    put(_C_CQ, _OFFS[0], Q_LORA + KV_LORA)
    kr = wint_ref[_OFFS[2]:_OFFS[3], :]
    krb = jnp.concatenate([kr, -kr[AXIS_DIM:], kr[:AXIS_DIM]], axis=0)
    win_o[:, _C_KR:_C_GMLA] = krb.T.astype(_BF)
    put(_C_GMLA, _OFFS[3], MLA_WIDTH)
    put(_C_GPOOL, _OFFS[5], POOL_WIDTH)
    put(_C_PX, _OFFS[4], POOL_WIDTH)
    put(_C_CC, _OFFS[7], 2 * CONV_WIDTH)
    put(_C_CB, _OFFS[6], CONV_WIDTH)
    put(_C_CB + CONV_WIDTH, _OFFS[9], CONV_WIDTH)
    wout_o[...] = wout_ref[...].astype(_BF)

    for hd in range(MLA_HEADS):
        s0, d0 = hd * QK_DIM, hd * HEAD_PAD
        r0 = s0 + NOPE_DIM
        wuq_o[:, d0:d0 + QK_DIM] = wuq_ref[:, s0:s0 + QK_DIM].astype(_BF)
        wuq_o[:, d0 + QK_DIM:d0 + QK_DIM + AXIS_DIM] = (-wuq_ref[:, r0 + AXIS_DIM:r0 + ROPE_DIM]).astype(_BF)
        wuq_o[:, d0 + QK_DIM + AXIS_DIM:d0 + HEAD_PAD] = wuq_ref[:, r0:r0 + AXIS_DIM].astype(_BF)
    for hd in range(MLA_HEADS):
        k0 = hd * KV_HEAD
        wukt_o[hd * NOPE_DIM:(hd + 1) * NOPE_DIM, :] = wukv_ref[:, k0:k0 + NOPE_DIM].T.astype(_BF)
        wuv_o[:, hd * V_DIM:(hd + 1) * V_DIM] = wukv_ref[:, k0 + NOPE_DIM:k0 + KV_HEAD].astype(_BF)
    poolw_o[...] = jnp.zeros((POOL_WIDTH, POOL_WIDTH), _BF)
    for g in range(POOL_GROUPS):
        sl = slice(g * POOL_GROUP_DIM, (g + 1) * POOL_GROUP_DIM)
        poolw_o[sl, sl] = poolw_ref[g].astype(_BF)


def _prep_weights(w_in_t, w_out, w_uq, w_ukv, pool_w):
    per_layer = lambda *shape: pl.BlockSpec((None,) + shape, lambda l: (l,) + (0,) * len(shape))
    return pl.pallas_call(
        _prep_kernel,
        grid=(DEPTH,),
        in_specs=[
            per_layer(IN_WIDTH, D_MODEL),
            per_layer(MIX_WIDTH, D_MODEL),
            per_layer(Q_LORA, MLA_HEADS * QK_DIM),
            per_layer(KV_LORA, MLA_HEADS * KV_HEAD),
            per_layer(POOL_GROUPS, POOL_GROUP_DIM, POOL_GROUP_DIM),
        ],
        out_specs=[
            per_layer(D_MODEL, _C_END),
            per_layer(MIX_WIDTH, D_MODEL),
            per_layer(Q_LORA, MLA_HEADS * HEAD_PAD),
            per_layer(MLA_HEADS * NOPE_DIM, KV_LORA),
            per_layer(KV_LORA, MLA_WIDTH),
            per_layer(POOL_WIDTH, POOL_WIDTH),
        ],
        out_shape=[
            jax.ShapeDtypeStruct((DEPTH, D_MODEL, _C_END), _BF),
            jax.ShapeDtypeStruct((DEPTH, MIX_WIDTH, D_MODEL), _BF),
            jax.ShapeDtypeStruct((DEPTH, Q_LORA, MLA_HEADS * HEAD_PAD), _BF),
            jax.ShapeDtypeStruct((DEPTH, MLA_HEADS * NOPE_DIM, KV_LORA), _BF),
            jax.ShapeDtypeStruct((DEPTH, KV_LORA, MLA_WIDTH), _BF),
            jax.ShapeDtypeStruct((DEPTH, POOL_WIDTH, POOL_WIDTH), _BF),
        ],
        compiler_params=pltpu.CompilerParams(dimension_semantics=("arbitrary",),
                                             vmem_limit_bytes=VMEM_LIMIT_BYTES),
        name="weight_prep",
    )(w_in_t, w_out, w_uq, w_ukv, pool_w)


def _pool_mix(win, rcnt):
    n = win.shape[0]
    rows = n - 2 * HALO
    lane = lax.broadcasted_iota(jnp.int32, (1, POOL_WIDTH), 1)
    sums = [win + pltpu.roll(win, 1, axis=0)]
    for k in (1, 2, 4):
        sums.append(pltpu.roll(sums[-1], k, axis=0) + pltpu.roll(sums[-1], n - k, axis=0))
    sel = sums[-1]
    for g in range(POOL_GROUPS - 2, -1, -1):
        sel = jnp.where(lane < (g + 1) * POOL_GROUP_DIM, sums[g], sel)
    return sel[HALO:HALO + rows] * rcnt - win[HALO:HALO + rows]


def _short_conv(win, w):
    rows = win.shape[0] - 2 * HALO
    return (win[HALO - 1:HALO - 1 + rows] * w[0] + win[HALO:HALO + rows] * w[1]
            + win[HALO + 1:HALO + 1 + rows] * w[2])


def _pass_kernel(*refs, n_seq, seq_len, cache_len, has_rope, emit_state, unroll, rc):
    it = iter(refs)
    x_ref, mod_ref = next(it), next(it)
    if cache_len:
        cckv_all_ref, ckr_all_ref = next(it), next(it)
    if has_rope:
        cs_ref = next(it)
    (rcnt_ref, gn_ref, win_all_ref, gq_ref, wuq_all_ref, gkv_ref, wukt_all_ref, wuv_all_ref, poolw_all_ref,
     pools_ref, convw_ref, wout_all_ref, gfin_ref) = (next(it) for _ in range(13))
    y_ref = next(it)
    if emit_state:
        ckv_out_ref, kr_out_ref = next(it), next(it)
    h_s, q_s, kt_s, v_s, px_s, prod_s, mixed_s = (next(it) for _ in range(7))

    m = n_seq * seq_len
    keys = cache_len + seq_len
    padded = seq_len + 2 * HALO
    lane = lax.broadcasted_iota(jnp.int32, (1, ROPE_PAD), 1)
    mrow = (1 + pl.program_id(0)) if has_rope else 0

    for c in range(m // rc):
        y_ref[c * rc:(c + 1) * rc, :] = x_ref[c * rc:(c + 1) * rc, :]

    zeros_halo = jnp.zeros((HALO, POOL_WIDTH), _F32)
    for buf in (px_s, prod_s):
        for s in range(n_seq):
            buf[s * padded:s * padded + HALO, :] = zeros_halo
            buf[(s + 1) * padded - HALO:(s + 1) * padded, :] = zeros_halo
    ones_col = jnp.where(lax.broadcasted_iota(jnp.int32, (n_seq * keys, V_DIM), 1) == 0, 1.0, 0.0).astype(_BF)
    for hd in range(MLA_HEADS):
        v_s[hd, :, V_DIM:] = ones_col

    def layer_view(l):
        lw = dict(l=l, row=pl.ds(l, 1), win=win_all_ref.at[l], wuq=wuq_all_ref.at[l], wukt=wukt_all_ref.at[l],
                  wuv=wuv_all_ref.at[l], poolw=poolw_all_ref.at[l], wout=wout_all_ref.at[l])
        lw["shift"], lw["scale"], lw["gate"] = (mod_ref[l, k, pl.ds(mrow, 1), :] for k in range(3))
        if cache_len:
            lw["cckv"], lw["ckr"] = cckv_all_ref.at[l], ckr_all_ref.at[l]
        return lw

    def rotate(v, r):
        v = v * cs_ref[pl.ds(r, rc), :]
        return v + pltpu.roll(v, ROPE_DIM, axis=1)

    def put_kv(lw, ckv, kr_t, kc):
        knt = _dot(lw["wukt"][...], ckv.T.astype(_BF))
        v4 = _dot(ckv.astype(_BF), lw["wuv"][...])
        kr_bf = kr_t.astype(_BF)
        rows_k = pl.ds(_aligned(kc * rc, rc), rc)
        for hd in range(MLA_HEADS):
            kt_s[hd, kc, :NOPE_DIM, :] = knt[hd * NOPE_DIM:(hd + 1) * NOPE_DIM, :].astype(_BF)
            kt_s[hd, kc, NOPE_DIM:, :] = kr_bf
            v_s[hd, rows_k, :V_DIM] = v4[:, hd * V_DIM:(hd + 1) * V_DIM].astype(_BF)

    def cached_keys(lw):
        ckr_t = lw["ckr"][...]
        ckr_pad = jnp.concatenate([ckr_t, jnp.zeros_like(ckr_t)], axis=0)
        for cc in range(cache_len // rc):
            rows_k = slice(cc * rc, (cc + 1) * rc)
            put_kv(lw, lw["cckv"][rows_k, :], ckr_pad[:, rows_k], cc)

    def halo_base(c):
        return _aligned(c * padded if seq_len == rc else c * rc, HALO)

    def project(lw, c):
        r = _aligned(c * rc, rc)
        rows = pl.ds(r, rc)
        row, win_ref = lw["row"], lw["win"]
        h = (_rms(y_ref[rows, :], gn_ref[row, :]) * (1.0 + lw["scale"]) + lw["shift"]).astype(_BF)
        h_s[rows, :] = h
        a = _dot(h, win_ref[:, _C_CQ:_C_GMLA])
        cq = _rms(a[:, :Q_LORA], gq_ref[row, :]).astype(_BF)
        ckv = _rms(a[:, _C_CKV:_C_KR], gkv_ref[row, :])
        kr = a[:, _C_KR:]
        if has_rope:
            kr = rotate(kr, r)
        kr_t = jnp.where(lane < ROPE_DIM, kr, 0.0).T
        if emit_state:
            ckv_out_ref[c, lw["l"]] = ckv
            kr_out_ref[c, lw["l"]] = kr_t[:ROPE_DIM, :]
        put_kv(lw, ckv, kr_t, cache_len // rc + c)
        for hp in range(MLA_HEADS // 2):
            q2 = _dot(cq, lw["wuq"][:, hp * 2 * HEAD_PAD:(hp + 1) * 2 * HEAD_PAD]) * Q_SCALE
            for j in range(2):
                hd = 2 * hp + j
                c0 = j * HEAD_PAD
                q_s[hd, rows, :NOPE_DIM] = q2[:, c0:c0 + NOPE_DIM].astype(_BF)
                qr = q2[:, c0 + NOPE_DIM:c0 + HEAD_PAD]
                if has_rope:
                    qr = rotate(qr, r)
                q_s[hd, rows, NOPE_DIM:] = qr.astype(_BF)
        pe = _dot(h, win_ref[:, _C_PX:_C_CB])
        rows_h = pl.ds(_aligned(halo_base(c) + HALO, HALO), rc)
        px_s[rows_h, :] = pe[:, :POOL_WIDTH]
        prod_s[rows_h, :] = pe[:, POOL_WIDTH:POOL_WIDTH + CONV_WIDTH] * pe[:, POOL_WIDTH + CONV_WIDTH:]

    def mix(lw, c):
        r = _aligned(c * rc, rc)
        rows = pl.ds(r, rc)
        row, win_ref = lw["row"], lw["win"]
        h = h_s[rows, :]
        key_chunks = range(keys // rc) if n_seq == 1 else [c]
        rows_k = pl.ds(0 if n_seq == 1 else r, keys)
        for hp in range(MLA_HEADS // 2):
            g2 = _silu(_dot(h, win_ref[:, _C_GMLA + hp * 2 * V_DIM:_C_GMLA + (hp + 1) * 2 * V_DIM]))
            for j in range(2):
                hd = 2 * hp + j
                q = q_s[hd, rows, :]
                sc = jnp.concatenate([_dot(q, kt_s[hd, kc]) for kc in key_chunks], axis=-1)
                p = jnp.exp2(sc - jnp.max(sc, axis=-1, keepdims=True))
                ov = _dot(p.astype(_BF), v_s[hd, rows_k, :])
                o = ov[:, :V_DIM] / ov[:, V_DIM:V_DIM + 1]
                mixed_s[rows, hd * V_DIM:(hd + 1) * V_DIM] = (g2[:, j * V_DIM:(j + 1) * V_DIM] * o).astype(_BF)

        rows_w = pl.ds(halo_base(c), rc + 2 * HALO)
        pooled = _pool_mix(px_s[rows_w, :], rcnt_ref[pl.ds(0 if seq_len == rc else r, rc), :])
        pool = _dot(pooled.astype(_BF), lw["poolw"][...]) * pools_ref[row, :]
        gp = _silu(_dot(h, win_ref[:, _C_GPOOL:_C_PX]))
        mixed_s[rows, MLA_WIDTH:MLA_WIDTH + POOL_WIDTH] = (gp * pool).astype(_BF)

        convw = [convw_ref[k, row, :] for k in range(3)]
        conv = _short_conv(prod_s[rows_w, :], convw)
        e2 = _dot(h, win_ref[:, _C_CB:_C_END])
        mixed_s[rows, MLA_WIDTH + POOL_WIDTH:] = (_silu(e2[:, CONV_WIDTH:]) * (e2[:, :CONV_WIDTH] * conv)).astype(_BF)

        y_ref[rows, :] = y_ref[rows, :] + lw["gate"] * _dot(mixed_s[rows, :], lw["wout"][...])

    def one_layer(l, carry):
        lw = layer_view(l)
        if cache_len:
            cached_keys(lw)
        _for_chunks(m // rc, unroll, functools.partial(project, lw))
        _for_chunks(m // rc, unroll, functools.partial(mix, lw))
        return carry

    lax.fori_loop(0, DEPTH, one_layer, 0)
    for c in range(m // rc):
        rows = slice(c * rc, (c + 1) * rc)
        y_ref[rows, :] = _rms(y_ref[rows, :], gfin_ref[...])


def _mixer_pass(x2d, mod_all, wts, *, seq_len, cache=None, rope=None, emit_state):
    rows = x2d.shape[0]
    m = ROWS_PER_STEP
    n_seq = m // seq_len
    n_steps = rows // m
    n_all = rows // seq_len
    cache_len = 0 if cache is None else cache[0].shape[2]
    keys = cache_len + seq_len
    has_rope = rope is not None
    rc = min(seq_len, MAX_ROW_CHUNK)
    assert seq_len % rc == 0 and cache_len % rc == 0
    assert (seq_len == rc and not cache_len) or (n_seq == 1 and not emit_state)

    def const(*shape):
        return pl.BlockSpec(shape, lambda i: (0,) * len(shape), pipeline_mode=pl.Buffered(1))

    args = [x2d, mod_all]
    in_specs = [pl.BlockSpec((m, D_MODEL), lambda i: (i, 0)), const(DEPTH, 3, MOD_ROWS, D_MODEL)]
    if cache_len:
        args += [cache[0], cache[1]]
        in_specs += [pl.BlockSpec((None, DEPTH, cache_len, KV_LORA), lambda i: (i, 0, 0, 0)),
                     pl.BlockSpec((None, DEPTH, ROPE_DIM, cache_len), lambda i: (i, 0, 0, 0))]
    if has_rope:
        args += [rope]
        in_specs += [const(seq_len, ROPE_PAD)]
    args += [_pool_rcnt(seq_len), wts["g_norm"], wts["w_in"], wts["g_q"], wts["w_uq"], wts["g_kv"],
             wts["w_ukt"], wts["w_uv"], wts["pool_w"], wts["pool_s"], wts["conv_w"], wts["w_out"],
             wts["g_final"]]
    in_specs += [const(seq_len, POOL_WIDTH), const(DEPTH, D_MODEL), const(DEPTH, D_MODEL, _C_END),
                 const(DEPTH, Q_LORA), const(DEPTH, Q_LORA, MLA_HEADS * HEAD_PAD), const(DEPTH, KV_LORA),
                 const(DEPTH, MLA_HEADS * NOPE_DIM, KV_LORA), const(DEPTH, KV_LORA, MLA_WIDTH),
                 const(DEPTH, POOL_WIDTH, POOL_WIDTH), const(DEPTH, POOL_WIDTH), const(3, DEPTH, CONV_WIDTH),
                 const(DEPTH, MIX_WIDTH, D_MODEL), const(1, D_MODEL)]

    out_shape = [jax.ShapeDtypeStruct((rows, D_MODEL), _F32)]
    out_specs = [pl.BlockSpec((m, D_MODEL), lambda i: (i, 0))]
    if emit_state:
        out_shape += [jax.ShapeDtypeStruct((n_all, DEPTH, seq_len, KV_LORA), _F32),
                      jax.ShapeDtypeStruct((n_all, DEPTH, ROPE_DIM, seq_len), _F32)]
        out_specs += [pl.BlockSpec((n_seq, DEPTH, seq_len, KV_LORA), lambda i: (i, 0, 0, 0)),
                      pl.BlockSpec((n_seq, DEPTH, ROPE_DIM, seq_len), lambda i: (i, 0, 0, 0))]

    kern = functools.partial(_pass_kernel, n_seq=n_seq, seq_len=seq_len, cache_len=cache_len,
                             has_rope=has_rope, emit_state=emit_state, unroll=CHUNK_UNROLL, rc=rc)
    return pl.pallas_call(
        kern,
        grid=(n_steps,),
        in_specs=in_specs,
        out_specs=out_specs,
        out_shape=out_shape,
        scratch_shapes=[
            pltpu.VMEM((m, D_MODEL), _BF),
            pltpu.VMEM((MLA_HEADS, m, HEAD_PAD), _BF),
            pltpu.VMEM((MLA_HEADS, n_seq * keys // rc, HEAD_PAD, rc), _BF),
            pltpu.VMEM((MLA_HEADS, n_seq * keys, 2 * V_DIM), _BF),
            pltpu.VMEM((n_seq * (seq_len + 2 * HALO), POOL_WIDTH), _F32),
            pltpu.VMEM((n_seq * (seq_len + 2 * HALO), CONV_WIDTH), _F32),
            pltpu.VMEM((m, MIX_WIDTH), _BF),
        ],
        compiler_params=pltpu.CompilerParams(dimension_semantics=("arbitrary",),
                                             vmem_limit_bytes=VMEM_LIMIT_BYTES),
        name="mixer_pass_latent" if has_rope else "mixer_pass_context",
    )(*args)


def _pool_rcnt(seq_len):
    t = np.arange(seq_len)[:, None]
    half = np.repeat(1 << np.arange(POOL_GROUPS), POOL_GROUP_DIM)[None, :]
    cnt = np.minimum(t + half, seq_len) - np.maximum(t - half, 0)
    return jnp.asarray((1.0 / cnt).astype(np.float32))


def _rope_table(seq_len):
    rows = seq_len // GRID_W
    row = np.repeat(np.arange(rows), GRID_W).astype(np.float32)
    col = np.tile(np.arange(GRID_W), rows).astype(np.float32)
    inv = (1.0 / (np.float32(ROPE_BASE) ** (np.arange(0, AXIS_DIM, 2, dtype=np.float32) / np.float32(AXIS_DIM))))
    inv = inv.astype(np.float32)
    ang = np.concatenate([row[:, None] * inv, col[:, None] * inv], axis=-1).astype(np.float64)
    cos, sin = np.cos(ang).astype(np.float32), np.sin(ang).astype(np.float32)
    return jnp.asarray(np.concatenate([cos, cos, sin, sin], axis=-1))


def kernel(x_prompt, x_sample, cache_ckv, cache_krope, c, c_ctx, w_mod, b_mod, g_norm, w_in, g_q, w_uq,
           g_kv, w_ukv, pool_w, pool_s, conv_w, w_out, g_final):
    batch, seq, _ = x_prompt.shape
    dec_batch, dec_seq, _ = x_sample.shape
    assert 1 + dec_batch <= MOD_ROWS and dec_seq == ROWS_PER_STEP and ROWS_PER_STEP % seq == 0 and DEPTH == 2

    c_all = jnp.concatenate([c_ctx[None], c, jnp.zeros((MOD_ROWS - 1 - dec_batch, D_MODEL), _F32)], axis=0)
    mod_all = _modulation(c_all, w_mod, b_mod)
    w_in_r, w_out_r, w_uq_r, w_ukt, w_uv, pool_w_r = _prep_weights(
        jnp.transpose(w_in, (0, 2, 1)), w_out, w_uq, w_ukv, pool_w)
    wts = {
        "g_norm": g_norm, "w_in": w_in_r, "g_q": g_q, "w_uq": w_uq_r, "g_kv": g_kv, "w_ukt": w_ukt,
        "w_uv": w_uv, "pool_w": pool_w_r, "pool_s": pool_s, "conv_w": jnp.transpose(conv_w, (1, 0, 2)), "w_out": w_out_r,
        "g_final": g_final.reshape(1, D_MODEL),
    }

    rope = _rope_table(dec_seq)
    cache = (cache_ckv, jnp.transpose(cache_krope, (0, 1, 3, 2)))
    xp = x_prompt.reshape(batch * seq, D_MODEL)
    xs = x_sample.reshape(dec_batch * dec_seq, D_MODEL)
    xp, state_ckv, state_krope_t = _mixer_pass(xp, mod_all, wts, seq_len=seq, emit_state=True)
    (xs,) = _mixer_pass(xs, mod_all, wts, seq_len=dec_seq, cache=cache, rope=rope, emit_state=False)
    y_prompt = xp.reshape(batch, seq, D_MODEL)
    state_krope = jnp.transpose(state_krope_t, (0, 1, 3, 2))
    y_sample = xs.reshape(dec_batch, dec_seq, D_MODEL)
    return (y_prompt, y_sample, state_ckv, state_krope)
```

```python
import functools

import numpy as np
import jax
import jax.numpy as jnp
from jax import lax
from jax.experimental import pallas as pl
from jax.experimental.pallas import tpu as pltpu

D_MODEL = 1024
DEPTH = 2
GRID_W = 64
MLA_HEADS = 4
NOPE_DIM = 128
ROPE_DIM = 64
V_DIM = 128
QK_DIM = NOPE_DIM + ROPE_DIM
MLA_WIDTH = MLA_HEADS * V_DIM
Q_LORA = 384
KV_LORA = 256
POOL_GROUPS = 4
POOL_GROUP_DIM = 64
POOL_WIDTH = POOL_GROUPS * POOL_GROUP_DIM
CONV_WIDTH = 256
MIX_WIDTH = MLA_WIDTH + POOL_WIDTH + CONV_WIDTH
ROPE_BASE = 10000.0
AXIS_DIM = ROPE_DIM // 2
ATTN_SCALE = QK_DIM ** -0.5
Q_SCALE = ATTN_SCALE * float(np.log2(np.e))
EPS = 1e-6

_SPLITS = (Q_LORA, KV_LORA, ROPE_DIM, MLA_WIDTH, POOL_WIDTH, POOL_WIDTH,
           CONV_WIDTH, CONV_WIDTH, CONV_WIDTH, CONV_WIDTH)
_OFFS = [sum(_SPLITS[:i]) for i in range(len(_SPLITS) + 1)]
IN_WIDTH = _OFFS[-1]

V7X_LANES = 128
V7X_SUBLANES = 8
V7X_MXU_DEPTH = 256
HEAD_PAD = V7X_MXU_DEPTH
ROPE_PAD = HEAD_PAD - NOPE_DIM
KV_HEAD = NOPE_DIM + V_DIM

_C_CQ = 0
_C_CKV = _C_CQ + Q_LORA
_C_KR = _C_CKV + KV_LORA
_C_GMLA = _C_KR + ROPE_PAD
_C_GPOOL = _C_GMLA + MLA_WIDTH
_C_PX = _C_GPOOL + POOL_WIDTH
_C_CC = _C_PX + POOL_WIDTH
_C_CB = _C_CC + 2 * CONV_WIDTH
_C_END = _C_CB + 2 * CONV_WIDTH

MOD_ROWS = 8
ROWS_PER_STEP = 1024
MAX_ROW_CHUNK = 512
CHUNK_UNROLL = 4
HALO = V7X_SUBLANES
PREP_COLS = 256
VMEM_LIMIT_BYTES = 60 * 1024 * 1024

_BF = jnp.bfloat16
_F32 = jnp.float32


def _dot(a, b):
    return jnp.dot(a, b, preferred_element_type=_F32)


def _rms(x, g):
    return x * lax.rsqrt(jnp.mean(x * x, axis=-1, keepdims=True) + EPS) * g


def _silu(x):
    return x * jax.nn.sigmoid(x)


def _aligned(v, k):
    return v if isinstance(v, int) else pl.multiple_of(v, k)


def _for_chunks(n, unroll, body):
    if unroll >= n:
        for c in range(n):
            body(c)
        return

    def step(i, carry):
        for u in range(unroll):
            body(i * unroll + u)
        return carry

    lax.fori_loop(0, n // unroll, step, 0)


def _mod_kernel(c_ref, w_ref, b_ref, o_ref):
    s = _silu(c_ref[...])
    bias = b_ref[pl.ds(pl.program_id(0), 1), :]
    o_ref[...] = _dot(s.astype(_BF), w_ref[...].astype(_BF)) + bias


def _modulation(c_all, w_mod, b_mod):
    return pl.pallas_call(
        _mod_kernel,
        grid=(DEPTH, 3),
        in_specs=[
            pl.BlockSpec((MOD_ROWS, D_MODEL), lambda l, j: (0, 0)),
            pl.BlockSpec((None, D_MODEL, D_MODEL), lambda l, j: (l, 0, j)),
            pl.BlockSpec((DEPTH, D_MODEL), lambda l, j: (0, j)),
        ],
        out_specs=pl.BlockSpec((None, None, MOD_ROWS, D_MODEL), lambda l, j: (l, j, 0, 0)),
        out_shape=jax.ShapeDtypeStruct((DEPTH, 3, MOD_ROWS, D_MODEL), _F32),
        name="modulation",
    )(c_all, w_mod, b_mod)


def _prep_kernel(wint_ref, wout_ref, wuq_ref, wukv_ref, poolw_ref,
                 win_o, wout_o, wuq_o, wukt_o, wuv_o, poolw_o):
    def put(dst, src, width):
        for c in range(0, width, PREP_COLS):
            n = min(PREP_COLS, width - c)
            win_o[:, dst + c:dst + c + n] = wint_ref[src + c:src + c + n, :].T.astype(_BF)

    put(_C_CQ, _OFFS[0], Q_LORA + KV_LORA)
    kr = wint_ref[_OFFS[2]:_OFFS[3], :]
    krb = jnp.concatenate([kr, -kr[AXIS_DIM:], kr[:AXIS_DIM]], axis=0)
    win_o[:, _C_KR:_C_GMLA] = krb.T.astype(_BF)
    put(_C_GMLA, _OFFS[3], MLA_WIDTH)
    put(_C_GPOOL, _OFFS[5], POOL_WIDTH)
    put(_C_PX, _OFFS[4], POOL_WIDTH)
    put(_C_CC, _OFFS[7], 2 * CONV_WIDTH)
    put(_C_CB, _OFFS[6], CONV_WIDTH)
    put(_C_CB + CONV_WIDTH, _OFFS[9], CONV_WIDTH)
    wout_o[...] = wout_ref[...].astype(_BF)

    for hd in range(MLA_HEADS):
        s0, d0 = hd * QK_DIM, hd * HEAD_PAD
        r0 = s0 + NOPE_DIM
        wuq_o[:, d0:d0 + QK_DIM] = wuq_ref[:, s0:s0 + QK_DIM].astype(_BF)
        wuq_o[:, d0 + QK_DIM:d0 + QK_DIM + AXIS_DIM] = (-wuq_ref[:, r0 + AXIS_DIM:r0 + ROPE_DIM]).astype(_BF)
        wuq_o[:, d0 + QK_DIM + AXIS_DIM:d0 + HEAD_PAD] = wuq_ref[:, r0:r0 + AXIS_DIM].astype(_BF)
    for hd in range(MLA_HEADS):
        k0 = hd * KV_HEAD
        wukt_o[hd * NOPE_DIM:(hd + 1) * NOPE_DIM, :] = wukv_ref[:, k0:k0 + NOPE_DIM].T.astype(_BF)
        wuv_o[:, hd * V_DIM:(hd + 1) * V_DIM] = wukv_ref[:, k0 + NOPE_DIM:k0 + KV_HEAD].astype(_BF)
    poolw_o[...] = jnp.zeros((POOL_WIDTH, POOL_WIDTH), _BF)
    for g in range(POOL_GROUPS):
        sl = slice(g * POOL_GROUP_DIM, (g + 1) * POOL_GROUP_DIM)
        poolw_o[sl, sl] = poolw_ref[g].astype(_BF)


def _prep_weights(w_in_t, w_out, w_uq, w_ukv, pool_w):
    per_layer = lambda *shape: pl.BlockSpec((None,) + shape, lambda l: (l,) + (0,) * len(shape))
    return pl.pallas_call(
        _prep_kernel,
        grid=(DEPTH,),
        in_specs=[
            per_layer(IN_WIDTH, D_MODEL),
            per_layer(MIX_WIDTH, D_MODEL),
            per_layer(Q_LORA, MLA_HEADS * QK_DIM),
            per_layer(KV_LORA, MLA_HEADS * KV_HEAD),
            per_layer(POOL_GROUPS, POOL_GROUP_DIM, POOL_GROUP_DIM),
        ],
        out_specs=[
            per_layer(D_MODEL, _C_END),
            per_layer(MIX_WIDTH, D_MODEL),
            per_layer(Q_LORA, MLA_HEADS * HEAD_PAD),
            per_layer(MLA_HEADS * NOPE_DIM, KV_LORA),
            per_layer(KV_LORA, MLA_WIDTH),
            per_layer(POOL_WIDTH, POOL_WIDTH),
        ],
        out_shape=[
            jax.ShapeDtypeStruct((DEPTH, D_MODEL, _C_END), _BF),
            jax.ShapeDtypeStruct((DEPTH, MIX_WIDTH, D_MODEL), _BF),
            jax.ShapeDtypeStruct((DEPTH, Q_LORA, MLA_HEADS * HEAD_PAD), _BF),
            jax.ShapeDtypeStruct((DEPTH, MLA_HEADS * NOPE_DIM, KV_LORA), _BF),
            jax.ShapeDtypeStruct((DEPTH, KV_LORA, MLA_WIDTH), _BF),
            jax.ShapeDtypeStruct((DEPTH, POOL_WIDTH, POOL_WIDTH), _BF),
        ],
        compiler_params=pltpu.CompilerParams(dimension_semantics=("arbitrary",),
                                             vmem_limit_bytes=VMEM_LIMIT_BYTES),
        name="weight_prep",
    )(w_in_t, w_out, w_uq, w_ukv, pool_w)


def _pool_mix(win, rcnt):
    n = win.shape[0]
    rows = n - 2 * HALO
    lane = lax.broadcasted_iota(jnp.int32, (1, POOL_WIDTH), 1)
    sums = [win + pltpu.roll(win, 1, axis=0)]
    for k in (1, 2, 4):
        sums.append(pltpu.roll(sums[-1], k, axis=0) + pltpu.roll(sums[-1], n - k, axis=0))
    sel = sums[-1]
    for g in range(POOL_GROUPS - 2, -1, -1):
        sel = jnp.where(lane < (g + 1) * POOL_GROUP_DIM, sums[g], sel)
    return sel[HALO:HALO + rows] * rcnt - win[HALO:HALO + rows]


def _short_conv(win, w):
    rows = win.shape[0] - 2 * HALO
    return (win[HALO - 1:HALO - 1 + rows] * w[0] + win[HALO:HALO + rows] * w[1]
            + win[HALO + 1:HALO + 1 + rows] * w[2])


def _pass_kernel(*refs, n_seq, seq_len, cache_len, has_rope, emit_state, unroll, rc):
    it = iter(refs)
    x_ref, mod_ref = next(it), next(it)
    if cache_len:
        cckv_all_ref, ckr_all_ref = next(it), next(it)
    if has_rope:
        cs_ref = next(it)
    (rcnt_ref, gn_ref, win_all_ref, gq_ref, wuq_all_ref, gkv_ref, wukt_all_ref, wuv_all_ref, poolw_all_ref,
     pools_ref, convw_ref, wout_all_ref, gfin_ref) = (next(it) for _ in range(13))
    y_ref = next(it)
    if emit_state:
        ckv_out_ref, kr_out_ref = next(it), next(it)
    h_s, q_s, kt_s, v_s, px_s, prod_s, mixed_s = (next(it) for _ in range(7))

    m = n_seq * seq_len
    keys = cache_len + seq_len
    padded = seq_len + 2 * HALO
    lane = lax.broadcasted_iota(jnp.int32, (1, ROPE_PAD), 1)
    mrow = (1 + pl.program_id(0)) if has_rope else 0

    zeros_halo = jnp.zeros((HALO, POOL_WIDTH), _F32)
    for buf in (px_s, prod_s):
        for s in range(n_seq):
            buf[s * padded:s * padded + HALO, :] = zeros_halo
            buf[(s + 1) * padded - HALO:(s + 1) * padded, :] = zeros_halo
    ones_col = jnp.where(lax.broadcasted_iota(jnp.int32, (n_seq * keys, V_DIM), 1) == 0, 1.0, 0.0).astype(_BF)
    for hd in range(MLA_HEADS):
        v_s[hd, :, V_DIM:] = ones_col

    def layer_view(l):
        lw = dict(l=l, src=x_ref if l == 0 else y_ref, row=pl.ds(l, 1), win=win_all_ref.at[l], wuq=wuq_all_ref.at[l], wukt=wukt_all_ref.at[l],
                  wuv=wuv_all_ref.at[l], poolw=poolw_all_ref.at[l], wout=wout_all_ref.at[l])
        lw["shift"], lw["scale"], lw["gate"] = (mod_ref[l, k, pl.ds(mrow, 1), :] for k in range(3))
        if cache_len:
            lw["cckv"], lw["ckr"] = cckv_all_ref.at[l], ckr_all_ref.at[l]
        return lw

    def rotate(v, r):
        v = v * cs_ref[pl.ds(r, rc), :]
        return v + pltpu.roll(v, ROPE_DIM, axis=1)

    def put_kv(lw, ckv, kr_t, kc):
        knt = _dot(lw["wukt"][...], ckv.T.astype(_BF))
        v4 = _dot(ckv.astype(_BF), lw["wuv"][...])
        kr_bf = kr_t.astype(_BF)
        rows_k = pl.ds(_aligned(kc * rc, rc), rc)
        for hd in range(MLA_HEADS):
            kt_s[hd, kc, :NOPE_DIM, :] = knt[hd * NOPE_DIM:(hd + 1) * NOPE_DIM, :].astype(_BF)
            kt_s[hd, kc, NOPE_DIM:, :] = kr_bf
            v_s[hd, rows_k, :V_DIM] = v4[:, hd * V_DIM:(hd + 1) * V_DIM].astype(_BF)

    def cached_keys(lw):
        ckr_t = lw["ckr"][...]
        ckr_pad = jnp.concatenate([ckr_t, jnp.zeros_like(ckr_t)], axis=0)
        for cc in range(cache_len // rc):
            rows_k = slice(cc * rc, (cc + 1) * rc)
            put_kv(lw, lw["cckv"][rows_k, :], ckr_pad[:, rows_k], cc)

    def halo_base(c):
        return _aligned(c * padded if seq_len == rc else c * rc, HALO)

    def project(lw, c):
        r = _aligned(c * rc, rc)
        rows = pl.ds(r, rc)
        row, win_ref = lw["row"], lw["win"]
        h = (_rms(lw["src"][rows, :], gn_ref[row, :]) * (1.0 + lw["scale"]) + lw["shift"]).astype(_BF)
        h_s[rows, :] = h
        a = _dot(h, win_ref[:, _C_CQ:_C_GMLA])
        cq = _rms(a[:, :Q_LORA], gq_ref[row, :]).astype(_BF)
        ckv = _rms(a[:, _C_CKV:_C_KR], gkv_ref[row, :])
        kr = a[:, _C_KR:]
        if has_rope:
            kr = rotate(kr, r)
        kr_t = jnp.where(lane < ROPE_DIM, kr, 0.0).T
        if emit_state:
            ckv_out_ref[c, lw["l"]] = ckv
            kr_out_ref[c, lw["l"]] = kr_t[:ROPE_DIM, :]
        put_kv(lw, ckv, kr_t, cache_len // rc + c)
        for hp in range(MLA_HEADS // 2):
            q2 = _dot(cq, lw["wuq"][:, hp * 2 * HEAD_PAD:(hp + 1) * 2 * HEAD_PAD]) * Q_SCALE
            for j in range(2):
                hd = 2 * hp + j
                c0 = j * HEAD_PAD
                q_s[hd, rows, :NOPE_DIM] = q2[:, c0:c0 + NOPE_DIM].astype(_BF)
                qr = q2[:, c0 + NOPE_DIM:c0 + HEAD_PAD]
                if has_rope:
                    qr = rotate(qr, r)
                q_s[hd, rows, NOPE_DIM:] = qr.astype(_BF)
        pe = _dot(h, win_ref[:, _C_PX:_C_CB])
        rows_h = pl.ds(_aligned(halo_base(c) + HALO, HALO), rc)
        px_s[rows_h, :] = pe[:, :POOL_WIDTH]
        prod_s[rows_h, :] = pe[:, POOL_WIDTH:POOL_WIDTH + CONV_WIDTH] * pe[:, POOL_WIDTH + CONV_WIDTH:]

    def mix(lw, c):
        r = _aligned(c * rc, rc)
        rows = pl.ds(r, rc)
        row, win_ref = lw["row"], lw["win"]
        h = h_s[rows, :]
        key_chunks = range(keys // rc) if n_seq == 1 else [c]
        rows_k = pl.ds(0 if n_seq == 1 else r, keys)
        for hp in range(MLA_HEADS // 2):
            g2 = _silu(_dot(h, win_ref[:, _C_GMLA + hp * 2 * V_DIM:_C_GMLA + (hp + 1) * 2 * V_DIM]))
            for j in range(2):
                hd = 2 * hp + j
                q = q_s[hd, rows, :]
                sc = jnp.concatenate([_dot(q, kt_s[hd, kc]) for kc in key_chunks], axis=-1)
                p = jnp.exp2(sc - jnp.max(sc, axis=-1, keepdims=True))
                ov = _dot(p.astype(_BF), v_s[hd, rows_k, :])
                o = ov[:, :V_DIM] / ov[:, V_DIM:V_DIM + 1]
                mixed_s[rows, hd * V_DIM:(hd + 1) * V_DIM] = (g2[:, j * V_DIM:(j + 1) * V_DIM] * o).astype(_BF)

        rows_w = pl.ds(halo_base(c), rc + 2 * HALO)
        pooled = _pool_mix(px_s[rows_w, :], rcnt_ref[pl.ds(0 if seq_len == rc else r, rc), :])
        pool = _dot(pooled.astype(_BF), lw["poolw"][...]) * pools_ref[row, :]
        gp = _silu(_dot(h, win_ref[:, _C_GPOOL:_C_PX]))
        mixed_s[rows, MLA_WIDTH:MLA_WIDTH + POOL_WIDTH] = (gp * pool).astype(_BF)

        convw = [convw_ref[k, row, :] for k in range(3)]
        conv = _short_conv(prod_s[rows_w, :], convw)
        e2 = _dot(h, win_ref[:, _C_CB:_C_END])
        mixed_s[rows, MLA_WIDTH + POOL_WIDTH:] = (_silu(e2[:, CONV_WIDTH:]) * (e2[:, :CONV_WIDTH] * conv)).astype(_BF)

        y = lw["src"][rows, :] + lw["gate"] * _dot(mixed_s[rows, :], lw["wout"][...])
        if lw["l"] == DEPTH - 1:
            y = _rms(y, gfin_ref[...])
        y_ref[rows, :] = y

    def one_layer(l, carry):
        lw = layer_view(l)
        if cache_len:
            cached_keys(lw)
        _for_chunks(m // rc, unroll, functools.partial(project, lw))
        _for_chunks(m // rc, unroll, functools.partial(mix, lw))
        return carry

    for l in range(DEPTH):
        one_layer(l, 0)


def _mixer_pass(x2d, mod_all, wts, *, seq_len, cache=None, rope=None, emit_state):
    rows = x2d.shape[0]
    m = ROWS_PER_STEP
    n_seq = m // seq_len
    n_steps = rows // m
    n_all = rows // seq_len
    cache_len = 0 if cache is None else cache[0].shape[2]
    keys = cache_len + seq_len
    has_rope = rope is not None
    rc = min(seq_len, MAX_ROW_CHUNK)
    assert seq_len % rc == 0 and cache_len % rc == 0
    assert (seq_len == rc and not cache_len) or (n_seq == 1 and not emit_state)

    def const(*shape):
        return pl.BlockSpec(shape, lambda i: (0,) * len(shape), pipeline_mode=pl.Buffered(1))

    args = [x2d, mod_all]
    in_specs = [pl.BlockSpec((m, D_MODEL), lambda i: (i, 0)), const(DEPTH, 3, MOD_ROWS, D_MODEL)]
    if cache_len:
        args += [cache[0], cache[1]]
        in_specs += [pl.BlockSpec((None, DEPTH, cache_len, KV_LORA), lambda i: (i, 0, 0, 0)),
                     pl.BlockSpec((None, DEPTH, ROPE_DIM, cache_len), lambda i: (i, 0, 0, 0))]
    if has_rope:
        args += [rope]
        in_specs += [const(seq_len, ROPE_PAD)]
    args += [_pool_rcnt(seq_len), wts["g_norm"], wts["w_in"], wts["g_q"], wts["w_uq"], wts["g_kv"],
             wts["w_ukt"], wts["w_uv"], wts["pool_w"], wts["pool_s"], wts["conv_w"], wts["w_out"],
             wts["g_final"]]
    in_specs += [const(seq_len, POOL_WIDTH), const(DEPTH, D_MODEL), const(DEPTH, D_MODEL, _C_END),
                 const(DEPTH, Q_LORA), const(DEPTH, Q_LORA, MLA_HEADS * HEAD_PAD), const(DEPTH, KV_LORA),
                 const(DEPTH, MLA_HEADS * NOPE_DIM, KV_LORA), const(DEPTH, KV_LORA, MLA_WIDTH),
                 const(DEPTH, POOL_WIDTH, POOL_WIDTH), const(DEPTH, POOL_WIDTH), const(3, DEPTH, CONV_WIDTH),
                 const(DEPTH, MIX_WIDTH, D_MODEL), const(1, D_MODEL)]

    out_shape = [jax.ShapeDtypeStruct((rows, D_MODEL), _F32)]
    out_specs = [pl.BlockSpec((m, D_MODEL), lambda i: (i, 0))]
    if emit_state:
        out_shape += [jax.ShapeDtypeStruct((n_all, DEPTH, seq_len, KV_LORA), _F32),
                      jax.ShapeDtypeStruct((n_all, DEPTH, ROPE_DIM, seq_len), _F32)]
        out_specs += [pl.BlockSpec((n_seq, DEPTH, seq_len, KV_LORA), lambda i: (i, 0, 0, 0)),
                      pl.BlockSpec((n_seq, DEPTH, ROPE_DIM, seq_len), lambda i: (i, 0, 0, 0))]

    kern = functools.partial(_pass_kernel, n_seq=n_seq, seq_len=seq_len, cache_len=cache_len,
                             has_rope=has_rope, emit_state=emit_state, unroll=CHUNK_UNROLL, rc=rc)
    return pl.pallas_call(
        kern,
        grid=(n_steps,),
        in_specs=in_specs,
        out_specs=out_specs,
        out_shape=out_shape,
        scratch_shapes=[
            pltpu.VMEM((m, D_MODEL), _BF),
            pltpu.VMEM((MLA_HEADS, m, HEAD_PAD), _BF),
            pltpu.VMEM((MLA_HEADS, n_seq * keys // rc, HEAD_PAD, rc), _BF),
            pltpu.VMEM((MLA_HEADS, n_seq * keys, 2 * V_DIM), _BF),
            pltpu.VMEM((n_seq * (seq_len + 2 * HALO), POOL_WIDTH), _F32),
            pltpu.VMEM((n_seq * (seq_len + 2 * HALO), CONV_WIDTH), _F32),
            pltpu.VMEM((m, MIX_WIDTH), _BF),
        ],
        compiler_params=pltpu.CompilerParams(dimension_semantics=("arbitrary",),
                                             vmem_limit_bytes=VMEM_LIMIT_BYTES),
        name="mixer_pass_latent" if has_rope else "mixer_pass_context",
    )(*args)


def _pool_rcnt(seq_len):
    t = np.arange(seq_len)[:, None]
    half = np.repeat(1 << np.arange(POOL_GROUPS), POOL_GROUP_DIM)[None, :]
    cnt = np.minimum(t + half, seq_len) - np.maximum(t - half, 0)
    return jnp.asarray((1.0 / cnt).astype(np.float32))


def _rope_table(seq_len):
    rows = seq_len // GRID_W
    row = np.repeat(np.arange(rows), GRID_W).astype(np.float32)
    col = np.tile(np.arange(GRID_W), rows).astype(np.float32)
    inv = (1.0 / (np.float32(ROPE_BASE) ** (np.arange(0, AXIS_DIM, 2, dtype=np.float32) / np.float32(AXIS_DIM))))
    inv = inv.astype(np.float32)
    ang = np.concatenate([row[:, None] * inv, col[:, None] * inv], axis=-1).astype(np.float64)
    cos, sin = np.cos(ang).astype(np.float32), np.sin(ang).astype(np.float32)
    return jnp.asarray(np.concatenate([cos, cos, sin, sin], axis=-1))


def kernel(x_prompt, x_sample, cache_ckv, cache_krope, c, c_ctx, w_mod, b_mod, g_norm, w_in, g_q, w_uq,
           g_kv, w_ukv, pool_w, pool_s, conv_w, w_out, g_final):
    batch, seq, _ = x_prompt.shape
    dec_batch, dec_seq, _ = x_sample.shape
    assert 1 + dec_batch <= MOD_ROWS and dec_seq == ROWS_PER_STEP and ROWS_PER_STEP % seq == 0 and DEPTH == 2

    c_all = jnp.concatenate([c_ctx[None], c, jnp.zeros((MOD_ROWS - 1 - dec_batch, D_MODEL), _F32)], axis=0)
    mod_all = _modulation(c_all, w_mod, b_mod)
    w_in_r, w_out_r, w_uq_r, w_ukt, w_uv, pool_w_r = _prep_weights(
        jnp.transpose(w_in, (0, 2, 1)), w_out, w_uq, w_ukv, pool_w)
    wts = {
        "g_norm": g_norm, "w_in": w_in_r, "g_q": g_q, "w_uq": w_uq_r, "g_kv": g_kv, "w_ukt": w_ukt,
        "w_uv": w_uv, "pool_w": pool_w_r, "pool_s": pool_s, "conv_w": jnp.transpose(conv_w, (1, 0, 2)), "w_out": w_out_r,
        "g_final": g_final.reshape(1, D_MODEL),
    }

    rope = _rope_table(dec_seq)
    cache = (cache_ckv, jnp.transpose(cache_krope, (0, 1, 3, 2)))
    xp = x_prompt.reshape(batch * seq, D_MODEL)
    xs = x_sample.reshape(dec_batch * dec_seq, D_MODEL)
    xp, state_ckv, state_krope_t = _mixer_pass(xp, mod_all, wts, seq_len=seq, emit_state=True)
    (xs,) = _mixer_pass(xs, mod_all, wts, seq_len=dec_seq, cache=cache, rope=rope, emit_state=False)
    y_prompt = xp.reshape(batch, seq, D_MODEL)
    state_krope = jnp.transpose(state_krope_t, (0, 1, 3, 2))
    y_sample = xs.reshape(dec_batch, dec_seq, D_MODEL)
    return (y_prompt, y_sample, state_ckv, state_krope)
```

```python
import functools

import numpy as np
import jax
import jax.numpy as jnp
from jax import lax
from jax.experimental import pallas as pl
from jax.experimental.pallas import tpu as pltpu

D_MODEL = 1024
DEPTH = 2
GRID_W = 64
MLA_HEADS = 4
NOPE_DIM = 128
ROPE_DIM = 64
V_DIM = 128
QK_DIM = NOPE_DIM + ROPE_DIM
MLA_WIDTH = MLA_HEADS * V_DIM
Q_LORA = 384
KV_LORA = 256
POOL_GROUPS = 4
POOL_GROUP_DIM = 64
POOL_WIDTH = POOL_GROUPS * POOL_GROUP_DIM
CONV_WIDTH = 256
MIX_WIDTH = MLA_WIDTH + POOL_WIDTH + CONV_WIDTH
ROPE_BASE = 10000.0
AXIS_DIM = ROPE_DIM // 2
ATTN_SCALE = QK_DIM ** -0.5
Q_SCALE = ATTN_SCALE * float(np.log2(np.e))
EPS = 1e-6

_SPLITS = (Q_LORA, KV_LORA, ROPE_DIM, MLA_WIDTH, POOL_WIDTH, POOL_WIDTH,
           CONV_WIDTH, CONV_WIDTH, CONV_WIDTH, CONV_WIDTH)
_OFFS = [sum(_SPLITS[:i]) for i in range(len(_SPLITS) + 1)]
IN_WIDTH = _OFFS[-1]

V7X_LANES = 128
V7X_SUBLANES = 8
V7X_MXU_DEPTH = 256
HEAD_PAD = V7X_MXU_DEPTH
ROPE_PAD = HEAD_PAD - NOPE_DIM
KV_HEAD = NOPE_DIM + V_DIM

_C_CQ = 0
_C_CKV = _C_CQ + Q_LORA
_C_KR = _C_CKV + KV_LORA
_C_GMLA = _C_KR + ROPE_PAD
_C_GPOOL = _C_GMLA + MLA_WIDTH
_C_PX = _C_GPOOL + POOL_WIDTH
_C_CC = _C_PX + POOL_WIDTH
_C_CB = _C_CC + 2 * CONV_WIDTH
_C_END = _C_CB + 2 * CONV_WIDTH

MOD_ROWS = 8
ROWS_PER_STEP = 1024
ROW_CHUNK = 512
CHUNK_UNROLL = 4
HALO = V7X_SUBLANES
PREP_COLS = 256
VMEM_LIMIT_BYTES = 60 * 1024 * 1024

_BF = jnp.bfloat16
_F32 = jnp.float32


def _dot(a, b):
    return jnp.dot(a, b, preferred_element_type=_F32)


def _rms(x, g):
    return x * lax.rsqrt(jnp.mean(x * x, axis=-1, keepdims=True) + EPS) * g


def _silu(x):
    return x * jax.nn.sigmoid(x)


def _aligned(v, k):
    return v if isinstance(v, int) else pl.multiple_of(v, k)


def _for_chunks(n, unroll, body):
    if unroll >= n:
        for c in range(n):
            body(c)
        return

    def step(i, carry):
        for u in range(unroll):
            body(i * unroll + u)
        return carry

    lax.fori_loop(0, n // unroll, step, 0)


def _mod_kernel(c_ref, w_ref, b_ref, o_ref):
    s = _silu(c_ref[...])
    bias = b_ref[pl.ds(pl.program_id(0), 1), :]
    o_ref[...] = _dot(s.astype(_BF), w_ref[...].astype(_BF)) + bias


def _modulation(c_all, w_mod, b_mod):
    return pl.pallas_call(
        _mod_kernel,
        grid=(DEPTH, 3),
        in_specs=[
            pl.BlockSpec((MOD_ROWS, D_MODEL), lambda l, j: (0, 0)),
            pl.BlockSpec((None, D_MODEL, D_MODEL), lambda l, j: (l, 0, j)),
            pl.BlockSpec((DEPTH, D_MODEL), lambda l, j: (0, j)),
        ],
        out_specs=pl.BlockSpec((None, None, MOD_ROWS, D_MODEL), lambda l, j: (l, j, 0, 0)),
        out_shape=jax.ShapeDtypeStruct((DEPTH, 3, MOD_ROWS, D_MODEL), _F32),
        name="modulation",
    )(c_all, w_mod, b_mod)


def _prep_kernel(wint_ref, wout_ref, wuq_ref, wukv_ref, poolw_ref,
                 win_o, wout_o, wuq_o, wukt_o, wuv_o, poolw_o):
    def put(dst, src, width):
        for c in range(0, width, PREP_COLS):
            n = min(PREP_COLS, width - c)
            win_o[:, dst + c:dst + c + n] = wint_ref[src + c:src + c + n, :].T.astype(_BF)

    put(_C_CQ, _OFFS[0], Q_LORA + KV_LORA)
    kr = wint_ref[_OFFS[2]:_OFFS[3], :]
    krb = jnp.concatenate([kr, -kr[AXIS_DIM:], kr[:AXIS_DIM]], axis=0)
    win_o[:, _C_KR:_C_GMLA] = krb.T.astype(_BF)
    put(_C_GMLA, _OFFS[3], MLA_WIDTH)
    put(_C_GPOOL, _OFFS[5], POOL_WIDTH)
    put(_C_PX, _OFFS[4], POOL_WIDTH)
    put(_C_CC, _OFFS[7], 2 * CONV_WIDTH)
    put(_C_CB, _OFFS[6], CONV_WIDTH)
    put(_C_CB + CONV_WIDTH, _OFFS[9], CONV_WIDTH)
    wout_o[...] = wout_ref[...].astype(_BF)

    for hd in range(MLA_HEADS):
        s0, d0 = hd * QK_DIM, hd * HEAD_PAD
        r0 = s0 + NOPE_DIM
        wuq_o[:, d0:d0 + QK_DIM] = wuq_ref[:, s0:s0 + QK_DIM].astype(_BF)
        wuq_o[:, d0 + QK_DIM:d0 + QK_DIM + AXIS_DIM] = (-wuq_ref[:, r0 + AXIS_DIM:r0 + ROPE_DIM]).astype(_BF)
        wuq_o[:, d0 + QK_DIM + AXIS_DIM:d0 + HEAD_PAD] = wuq_ref[:, r0:r0 + AXIS_DIM].astype(_BF)
    for hd in range(MLA_HEADS):
        k0 = hd * KV_HEAD
        wukt_o[hd * NOPE_DIM:(hd + 1) * NOPE_DIM, :] = wukv_ref[:, k0:k0 + NOPE_DIM].T.astype(_BF)
        wuv_o[:, hd * V_DIM:(hd + 1) * V_DIM] = wukv_ref[:, k0 + NOPE_DIM:k0 + KV_HEAD].astype(_BF)
    poolw_o[...] = jnp.zeros((POOL_WIDTH, POOL_WIDTH), _BF)
    for g in range(POOL_GROUPS):
        sl = slice(g * POOL_GROUP_DIM, (g + 1) * POOL_GROUP_DIM)
        poolw_o[sl, sl] = poolw_ref[g].astype(_BF)


def _prep_weights(w_in_t, w_out, w_uq, w_ukv, pool_w):
    per_layer = lambda *shape: pl.BlockSpec((None,) + shape, lambda l: (l,) + (0,) * len(shape))
    return pl.pallas_call(
        _prep_kernel,
        grid=(DEPTH,),
        in_specs=[
            per_layer(IN_WIDTH, D_MODEL),
            per_layer(MIX_WIDTH, D_MODEL),
            per_layer(Q_LORA, MLA_HEADS * QK_DIM),
            per_layer(KV_LORA, MLA_HEADS * KV_HEAD),
            per_layer(POOL_GROUPS, POOL_GROUP_DIM, POOL_GROUP_DIM),
        ],
        out_specs=[
            per_layer(D_MODEL, _C_END),
            per_layer(MIX_WIDTH, D_MODEL),
            per_layer(Q_LORA, MLA_HEADS * HEAD_PAD),
            per_layer(MLA_HEADS * NOPE_DIM, KV_LORA),
            per_layer(KV_LORA, MLA_WIDTH),
            per_layer(POOL_WIDTH, POOL_WIDTH),
        ],
        out_shape=[
            jax.ShapeDtypeStruct((DEPTH, D_MODEL, _C_END), _BF),
            jax.ShapeDtypeStruct((DEPTH, MIX_WIDTH, D_MODEL), _BF),
            jax.ShapeDtypeStruct((DEPTH, Q_LORA, MLA_HEADS * HEAD_PAD), _BF),
            jax.ShapeDtypeStruct((DEPTH, MLA_HEADS * NOPE_DIM, KV_LORA), _BF),
            jax.ShapeDtypeStruct((DEPTH, KV_LORA, MLA_WIDTH), _BF),
            jax.ShapeDtypeStruct((DEPTH, POOL_WIDTH, POOL_WIDTH), _BF),
        ],
        compiler_params=pltpu.CompilerParams(dimension_semantics=("arbitrary",),
                                             vmem_limit_bytes=VMEM_LIMIT_BYTES),
        name="weight_prep",
    )(w_in_t, w_out, w_uq, w_ukv, pool_w)


def _pool_mix(win, rcnt):
    n = win.shape[0]
    rows = n - 2 * HALO
    lane = lax.broadcasted_iota(jnp.int32, (1, POOL_WIDTH), 1)
    sums = [win + pltpu.roll(win, 1, axis=0)]
    for k in (1, 2, 4):
        sums.append(pltpu.roll(sums[-1], k, axis=0) + pltpu.roll(sums[-1], n - k, axis=0))
    sel = sums[-1]
    for g in range(POOL_GROUPS - 2, -1, -1):
        sel = jnp.where(lane < (g + 1) * POOL_GROUP_DIM, sums[g], sel)
    return sel[HALO:HALO + rows] * rcnt - win[HALO:HALO + rows]


def _short_conv(win, w):
    rows = win.shape[0] - 2 * HALO
    return (win[HALO - 1:HALO - 1 + rows] * w[0] + win[HALO:HALO + rows] * w[1]
            + win[HALO + 1:HALO + 1 + rows] * w[2])


def _pass_kernel(*refs, n_seq, seq_len, cache_len, has_rope, emit_state, unroll, rc):
    it = iter(refs)
    x_ref, mod_ref = next(it), next(it)
    if cache_len:
        cckv_all_ref, ckr_all_ref = next(it), next(it)
    if has_rope:
        cs_ref = next(it)
    (rcnt_ref, gn_ref, win_all_ref, gq_ref, wuq_all_ref, gkv_ref, wukt_all_ref, wuv_all_ref, poolw_all_ref,
     pools_ref, convw_ref, wout_all_ref, gfin_ref) = (next(it) for _ in range(13))
    y_ref = next(it)
    if emit_state:
        ckv_out_ref, kr_out_ref = next(it), next(it)
    h_s, q_s, kt_s, v_s, px_s, prod_s, mixed_s = (next(it) for _ in range(7))

    m = n_seq * seq_len
    keys = cache_len + seq_len
    padded = seq_len + 2 * HALO
    lane = lax.broadcasted_iota(jnp.int32, (1, ROPE_PAD), 1)
    mrow = (1 + pl.program_id(0)) if has_rope else 0

    for c in range(m // rc):
        y_ref[c * rc:(c + 1) * rc, :] = x_ref[c * rc:(c + 1) * rc, :]

    zeros_halo = jnp.zeros((HALO, POOL_WIDTH), _F32)
    for buf in (px_s, prod_s):
        for s in range(n_seq):
            buf[s * padded:s * padded + HALO, :] = zeros_halo
            buf[(s + 1) * padded - HALO:(s + 1) * padded, :] = zeros_halo
    ones_col = jnp.where(lax.broadcasted_iota(jnp.int32, (n_seq * keys, V_DIM), 1) == 0, 1.0, 0.0).astype(_BF)
    for hd in range(MLA_HEADS):
        v_s[hd, :, V_DIM:] = ones_col

    def layer_view(l):
        lw = dict(l=l, row=pl.ds(l, 1), win=win_all_ref.at[l], wuq=wuq_all_ref.at[l], wukt=wukt_all_ref.at[l],
                  wuv=wuv_all_ref.at[l], poolw=poolw_all_ref.at[l], wout=wout_all_ref.at[l])
        lw["shift"], lw["scale"], lw["gate"] = (mod_ref[l, k, pl.ds(mrow, 1), :] for k in range(3))
        if cache_len:
            lw["cckv"], lw["ckr"] = cckv_all_ref.at[l], ckr_all_ref.at[l]
        return lw

    def rotate(v, r):
        v = v * cs_ref[pl.ds(r, rc), :]
        return v + pltpu.roll(v, ROPE_DIM, axis=1)

    def put_kv(lw, ckv, kr_t, kc):
        knt = _dot(lw["wukt"][...], ckv.T.astype(_BF))
        v4 = _dot(ckv.astype(_BF), lw["wuv"][...])
        kr_bf = kr_t.astype(_BF)
        rows_k = pl.ds(_aligned(kc * rc, rc), rc)
        for hd in range(MLA_HEADS):
            kt_s[hd, kc, :NOPE_DIM, :] = knt[hd * NOPE_DIM:(hd + 1) * NOPE_DIM, :].astype(_BF)
            kt_s[hd, kc, NOPE_DIM:, :] = kr_bf
            v_s[hd, rows_k, :V_DIM] = v4[:, hd * V_DIM:(hd + 1) * V_DIM].astype(_BF)

    def cached_keys(lw):
        ckr_t = lw["ckr"][...]
        ckr_pad = jnp.concatenate([ckr_t, jnp.zeros_like(ckr_t)], axis=0)
        for cc in range(cache_len // rc):
            rows_k = slice(cc * rc, (cc + 1) * rc)
            put_kv(lw, lw["cckv"][rows_k, :], ckr_pad[:, rows_k], cc)

    piece = min(seq_len, rc)
    n_sub = rc // piece
    pieces = [(s, slice(s * piece, (s + 1) * piece)) for s in range(n_sub)]

    def halo_base(c, s):
        return (c * n_sub + s) * padded if seq_len <= rc else c * rc

    def project(lw, c):
        r = _aligned(c * rc, rc)
        rows = pl.ds(r, rc)
        row, win_ref = lw["row"], lw["win"]
        h = (_rms(y_ref[rows, :], gn_ref[row, :]) * (1.0 + lw["scale"]) + lw["shift"]).astype(_BF)
        h_s[rows, :] = h
        a = _dot(h, win_ref[:, _C_CQ:_C_GMLA])
        cq = _rms(a[:, :Q_LORA], gq_ref[row, :]).astype(_BF)
        ckv = _rms(a[:, _C_CKV:_C_KR], gkv_ref[row, :])
        kr = a[:, _C_KR:]
        if has_rope:
            kr = rotate(kr, r)
        kr_t = jnp.where(lane < ROPE_DIM, kr, 0.0).T
        if emit_state:
            for s, ps in pieces:
                ckv_out_ref[c * n_sub + s, lw["l"]] = ckv[ps, :]
                kr_out_ref[c * n_sub + s, lw["l"]] = kr_t[:ROPE_DIM, ps]
        put_kv(lw, ckv, kr_t, cache_len // rc + c)
        for hp in range(MLA_HEADS // 2):
            q2 = _dot(cq, lw["wuq"][:, hp * 2 * HEAD_PAD:(hp + 1) * 2 * HEAD_PAD]) * Q_SCALE
            for j in range(2):
                hd = 2 * hp + j
                c0 = j * HEAD_PAD
                q_s[hd, rows, :NOPE_DIM] = q2[:, c0:c0 + NOPE_DIM].astype(_BF)
                qr = q2[:, c0 + NOPE_DIM:c0 + HEAD_PAD]
                if has_rope:
                    qr = rotate(qr, r)
                q_s[hd, rows, NOPE_DIM:] = qr.astype(_BF)
        pe = _dot(h, win_ref[:, _C_PX:_C_CB])
        for s, ps in pieces:
            rows_h = pl.ds(halo_base(c, s) + HALO, piece)
            px_s[rows_h, :] = pe[ps, :POOL_WIDTH]
            prod_s[rows_h, :] = pe[ps, POOL_WIDTH:POOL_WIDTH + CONV_WIDTH] * pe[ps, POOL_WIDTH + CONV_WIDTH:]

    def mix(lw, c):
        r = _aligned(c * rc, rc)
        rows = pl.ds(r, rc)
        row, win_ref = lw["row"], lw["win"]
        h = h_s[rows, :]
        for hp in range(MLA_HEADS // 2):
            g2 = _silu(_dot(h, win_ref[:, _C_GMLA + hp * 2 * V_DIM:_C_GMLA + (hp + 1) * 2 * V_DIM]))
            for j in range(2):
                hd = 2 * hp + j
                for s, ps in pieces:
                    rows_q = pl.ds(r + s * piece, piece)
                    if n_seq == 1:
                        ks = [kt_s[hd, kc] for kc in range(keys // rc)]
                        rows_k = pl.ds(0, keys)
                    else:
                        ks = [kt_s[hd, c, :, ps]]
                        rows_k = rows_q
                    q = q_s[hd, rows_q, :]
                    sc = jnp.concatenate([_dot(q, k) for k in ks], axis=-1)
                    p = jnp.exp2(sc - jnp.max(sc, axis=-1, keepdims=True))
                    ov = _dot(p.astype(_BF), v_s[hd, rows_k, :])
                    o = ov[:, :V_DIM] / ov[:, V_DIM:V_DIM + 1]
                    mixed_s[rows_q, hd * V_DIM:(hd + 1) * V_DIM] = (g2[ps, j * V_DIM:(j + 1) * V_DIM] * o).astype(_BF)

        windows = [pl.ds(halo_base(c, s), piece + 2 * HALO) for s, _ in pieces]
        rcnt = rcnt_ref[pl.ds(0 if seq_len <= rc else r, piece), :]
        pooled = jnp.concatenate([_pool_mix(px_s[w, :], rcnt) for w in windows], axis=0)
        pool = _dot(pooled.astype(_BF), lw["poolw"][...]) * pools_ref[row, :]
        gp = _silu(_dot(h, win_ref[:, _C_GPOOL:_C_PX]))
        mixed_s[rows, MLA_WIDTH:MLA_WIDTH + POOL_WIDTH] = (gp * pool).astype(_BF)

        convw = [convw_ref[k, row, :] for k in range(3)]
        conv = jnp.concatenate([_short_conv(prod_s[w, :], convw) for w in windows], axis=0)
        e2 = _dot(h, win_ref[:, _C_CB:_C_END])
        mixed_s[rows, MLA_WIDTH + POOL_WIDTH:] = (_silu(e2[:, CONV_WIDTH:]) * (e2[:, :CONV_WIDTH] * conv)).astype(_BF)

        y_ref[rows, :] = y_ref[rows, :] + lw["gate"] * _dot(mixed_s[rows, :], lw["wout"][...])

    def one_layer(l, carry):
        lw = layer_view(l)
        if cache_len:
            cached_keys(lw)
        _for_chunks(m // rc, unroll, functools.partial(project, lw))
        _for_chunks(m // rc, unroll, functools.partial(mix, lw))
        return carry

    lax.fori_loop(0, DEPTH, one_layer, 0)
    for c in range(m // rc):
        rows = slice(c * rc, (c + 1) * rc)
        y_ref[rows, :] = _rms(y_ref[rows, :], gfin_ref[...])


def _mixer_pass(x2d, mod_all, wts, *, seq_len, cache=None, rope=None, emit_state):
    rows = x2d.shape[0]
    m = ROWS_PER_STEP
    n_seq = m // seq_len
    n_steps = rows // m
    n_all = rows // seq_len
    cache_len = 0 if cache is None else cache[0].shape[2]
    keys = cache_len + seq_len
    has_rope = rope is not None
    rc = ROW_CHUNK
    assert m % rc == 0 and cache_len % rc == 0
    assert (rc % seq_len == 0 and not cache_len) or (n_seq == 1 and seq_len % rc == 0 and not emit_state)

    def const(*shape):
        return pl.BlockSpec(shape, lambda i: (0,) * len(shape), pipeline_mode=pl.Buffered(1))

    args = [x2d, mod_all]
    in_specs = [pl.BlockSpec((m, D_MODEL), lambda i: (i, 0)), const(DEPTH, 3, MOD_ROWS, D_MODEL)]
    if cache_len:
        args += [cache[0], cache[1]]
        in_specs += [pl.BlockSpec((None, DEPTH, cache_len, KV_LORA), lambda i: (i, 0, 0, 0)),
                     pl.BlockSpec((None, DEPTH, ROPE_DIM, cache_len), lambda i: (i, 0, 0, 0))]
    if has_rope:
        args += [rope]
        in_specs += [const(seq_len, ROPE_PAD)]
    args += [_pool_rcnt(seq_len), wts["g_norm"], wts["w_in"], wts["g_q"], wts["w_uq"], wts["g_kv"],
             wts["w_ukt"], wts["w_uv"], wts["pool_w"], wts["pool_s"], wts["conv_w"], wts["w_out"],
             wts["g_final"]]
    in_specs += [const(seq_len, POOL_WIDTH), const(DEPTH, D_MODEL), const(DEPTH, D_MODEL, _C_END),
                 const(DEPTH, Q_LORA), const(DEPTH, Q_LORA, MLA_HEADS * HEAD_PAD), const(DEPTH, KV_LORA),
                 const(DEPTH, MLA_HEADS * NOPE_DIM, KV_LORA), const(DEPTH, KV_LORA, MLA_WIDTH),
                 const(DEPTH, POOL_WIDTH, POOL_WIDTH), const(DEPTH, POOL_WIDTH), const(3, DEPTH, CONV_WIDTH),
                 const(DEPTH, MIX_WIDTH, D_MODEL), const(1, D_MODEL)]

    out_shape = [jax.ShapeDtypeStruct((rows, D_MODEL), _F32)]
    out_specs = [pl.BlockSpec((m, D_MODEL), lambda i: (i, 0))]
    if emit_state:
        out_shape += [jax.ShapeDtypeStruct((n_all, DEPTH, seq_len, KV_LORA), _F32),
                      jax.ShapeDtypeStruct((n_all, DEPTH, ROPE_DIM, seq_len), _F32)]
        out_specs += [pl.BlockSpec((n_seq, DEPTH, seq_len, KV_LORA), lambda i: (i, 0, 0, 0)),
                      pl.BlockSpec((n_seq, DEPTH, ROPE_DIM, seq_len), lambda i: (i, 0, 0, 0))]

    kern = functools.partial(_pass_kernel, n_seq=n_seq, seq_len=seq_len, cache_len=cache_len,
                             has_rope=has_rope, emit_state=emit_state, unroll=CHUNK_UNROLL, rc=rc)
    return pl.pallas_call(
        kern,
        grid=(n_steps,),
        in_specs=in_specs,
        out_specs=out_specs,
        out_shape=out_shape,
        scratch_shapes=[
            pltpu.VMEM((m, D_MODEL), _BF),
            pltpu.VMEM((MLA_HEADS, m, HEAD_PAD), _BF),
            pltpu.VMEM((MLA_HEADS, n_seq * keys // rc, HEAD_PAD, rc), _BF),
            pltpu.VMEM((MLA_HEADS, n_seq * keys, 2 * V_DIM), _BF),
            pltpu.VMEM((n_seq * (seq_len + 2 * HALO), POOL_WIDTH), _F32),
            pltpu.VMEM((n_seq * (seq_len + 2 * HALO), CONV_WIDTH), _F32),
            pltpu.VMEM((m, MIX_WIDTH), _BF),
        ],
        compiler_params=pltpu.CompilerParams(dimension_semantics=("arbitrary",),
                                             vmem_limit_bytes=VMEM_LIMIT_BYTES),
        name="mixer_pass_latent" if has_rope else "mixer_pass_context",
    )(*args)


def _pool_rcnt(seq_len):
    t = np.arange(seq_len)[:, None]
    half = np.repeat(1 << np.arange(POOL_GROUPS), POOL_GROUP_DIM)[None, :]
    cnt = np.minimum(t + half, seq_len) - np.maximum(t - half, 0)
    return jnp.asarray((1.0 / cnt).astype(np.float32))


def _rope_table(seq_len):
    rows = seq_len // GRID_W
    row = np.repeat(np.arange(rows), GRID_W).astype(np.float32)
    col = np.tile(np.arange(GRID_W), rows).astype(np.float32)
    inv = (1.0 / (np.float32(ROPE_BASE) ** (np.arange(0, AXIS_DIM, 2, dtype=np.float32) / np.float32(AXIS_DIM))))
    inv = inv.astype(np.float32)
    ang = np.concatenate([row[:, None] * inv, col[:, None] * inv], axis=-1).astype(np.float64)
    cos, sin = np.cos(ang).astype(np.float32), np.sin(ang).astype(np.float32)
    return jnp.asarray(np.concatenate([cos, cos, sin, sin], axis=-1))


def kernel(x_prompt, x_sample, cache_ckv, cache_krope, c, c_ctx, w_mod, b_mod, g_norm, w_in, g_q, w_uq,
           g_kv, w_ukv, pool_w, pool_s, conv_w, w_out, g_final):
    batch, seq, _ = x_prompt.shape
    dec_batch, dec_seq, _ = x_sample.shape
    assert 1 + dec_batch <= MOD_ROWS and dec_seq == ROWS_PER_STEP and ROWS_PER_STEP % seq == 0 and DEPTH == 2

    c_all = jnp.concatenate([c_ctx[None], c, jnp.zeros((MOD_ROWS - 1 - dec_batch, D_MODEL), _F32)], axis=0)
    mod_all = _modulation(c_all, w_mod, b_mod)
    w_in_r, w_out_r, w_uq_r, w_ukt, w_uv, pool_w_r = _prep_weights(
        jnp.transpose(w_in, (0, 2, 1)), w_out, w_uq, w_ukv, pool_w)
    wts = {
        "g_norm": g_norm, "w_in": w_in_r, "g_q": g_q, "w_uq": w_uq_r, "g_kv": g_kv, "w_ukt": w_ukt,
        "w_uv": w_uv, "pool_w": pool_w_r, "pool_s": pool_s, "conv_w": jnp.transpose(conv_w, (1, 0, 2)), "w_out": w_out_r,
        "g_final": g_final.reshape(1, D_MODEL),
    }

    rope = _rope_table(dec_seq)
    cache = (cache_ckv, jnp.transpose(cache_krope, (0, 1, 3, 2)))
    xp = x_prompt.reshape(batch * seq, D_MODEL)
    xs = x_sample.reshape(dec_batch * dec_seq, D_MODEL)
    xp, state_ckv, state_krope_t = _mixer_pass(xp, mod_all, wts, seq_len=seq, emit_state=True)
    (xs,) = _mixer_pass(xs, mod_all, wts, seq_len=dec_seq, cache=cache, rope=rope, emit_state=False)
    y_prompt = xp.reshape(batch, seq, D_MODEL)
    state_krope = jnp.transpose(state_krope_t, (0, 1, 3, 2))
    y_sample = xs.reshape(dec_batch, dec_seq, D_MODEL)
    return (y_prompt, y_sample, state_ckv, state_krope)
```

```python
import functools

import numpy as np
import jax
import jax.numpy as jnp
from jax import lax
from jax.experimental import pallas as pl
from jax.experimental.pallas import tpu as pltpu

D_MODEL = 1024
DEPTH = 2
GRID_W = 64
MLA_HEADS = 4
NOPE_DIM = 128
ROPE_DIM = 64
V_DIM = 128
QK_DIM = NOPE_DIM + ROPE_DIM
MLA_WIDTH = MLA_HEADS * V_DIM
Q_LORA = 384
KV_LORA = 256
POOL_GROUPS = 4
POOL_GROUP_DIM = 64
POOL_WIDTH = POOL_GROUPS * POOL_GROUP_DIM
CONV_WIDTH = 256
MIX_WIDTH = MLA_WIDTH + POOL_WIDTH + CONV_WIDTH
ROPE_BASE = 10000.0
AXIS_DIM = ROPE_DIM // 2
ATTN_SCALE = QK_DIM ** -0.5
Q_SCALE = ATTN_SCALE * float(np.log2(np.e))
EPS = 1e-6

_SPLITS = (Q_LORA, KV_LORA, ROPE_DIM, MLA_WIDTH, POOL_WIDTH, POOL_WIDTH,
           CONV_WIDTH, CONV_WIDTH, CONV_WIDTH, CONV_WIDTH)
_OFFS = [sum(_SPLITS[:i]) for i in range(len(_SPLITS) + 1)]
IN_WIDTH = _OFFS[-1]

V7X_LANES = 128
V7X_SUBLANES = 8
V7X_MXU_DEPTH = 256
HEAD_PAD = V7X_MXU_DEPTH
ROPE_PAD = HEAD_PAD - NOPE_DIM
KV_HEAD = NOPE_DIM + V_DIM

_C_CQ = 0
_C_CKV = _C_CQ + Q_LORA
_C_KR = _C_CKV + KV_LORA
_C_GMLA = _C_KR + ROPE_PAD
_C_GPOOL = _C_GMLA + MLA_WIDTH
_C_PX = _C_GPOOL + POOL_WIDTH
_C_CC = _C_PX + POOL_WIDTH
_C_CB = _C_CC + 2 * CONV_WIDTH
_C_END = _C_CB + 2 * CONV_WIDTH

MOD_ROWS = 8
ROWS_PER_STEP = 1024
ROW_CHUNK = 512
CHUNK_UNROLL = 4
HALO = V7X_SUBLANES
PREP_COLS = 256
VMEM_LIMIT_BYTES = 60 * 1024 * 1024

_BF = jnp.bfloat16
_F32 = jnp.float32


def _dot(a, b):
    return jnp.dot(a, b, preferred_element_type=_F32)


def _rms(x, g):
    return x * lax.rsqrt(jnp.mean(x * x, axis=-1, keepdims=True) + EPS) * g


def _silu(x):
    return x * jax.nn.sigmoid(x)


def _aligned(v, k):
    return v if isinstance(v, int) else pl.multiple_of(v, k)


def _for_chunks(n, unroll, body):
    if unroll >= n:
        for c in range(n):
            body(c)
        return

    def step(i, carry):
        for u in range(unroll):
            body(i * unroll + u)
        return carry

    lax.fori_loop(0, n // unroll, step, 0)


def _mod_kernel(c_ref, w_ref, b_ref, o_ref):
    s = _silu(c_ref[...])
    bias = b_ref[pl.ds(pl.program_id(0), 1), :]
    o_ref[...] = _dot(s.astype(_BF), w_ref[...].astype(_BF)) + bias


def _modulation(c_all, w_mod, b_mod):
    return pl.pallas_call(
        _mod_kernel,
        grid=(DEPTH, 3),
        in_specs=[
            pl.BlockSpec((MOD_ROWS, D_MODEL), lambda l, j: (0, 0)),
            pl.BlockSpec((None, D_MODEL, D_MODEL), lambda l, j: (l, 0, j)),
            pl.BlockSpec((DEPTH, D_MODEL), lambda l, j: (0, j)),
        ],
        out_specs=pl.BlockSpec((None, None, MOD_ROWS, D_MODEL), lambda l, j: (l, j, 0, 0)),
        out_shape=jax.ShapeDtypeStruct((DEPTH, 3, MOD_ROWS, D_MODEL), _F32),
        name="modulation",
    )(c_all, w_mod, b_mod)


def _prep_kernel(wint_ref, wout_ref, wuq_ref, wukv_ref, poolw_ref,
                 win_o, wout_o, wuq_o, wukt_o, wuv_o, poolw_o):
    def put(dst, src, width):
        for c in range(0, width, PREP_COLS):
            n = min(PREP_COLS, width - c)
            win_o[:, dst + c:dst + c + n] = wint_ref[src + c:src + c + n, :].T.astype(_BF)

    put(_C_CQ, _OFFS[0], Q_LORA + KV_LORA)
    kr = wint_ref[_OFFS[2]:_OFFS[3], :]
    krb = jnp.concatenate([kr, -kr[AXIS_DIM:], kr[:AXIS_DIM]], axis=0)
    win_o[:, _C_KR:_C_GMLA] = krb.T.astype(_BF)
    put(_C_GMLA, _OFFS[3], MLA_WIDTH)
    put(_C_GPOOL, _OFFS[5], POOL_WIDTH)
    put(_C_PX, _OFFS[4], POOL_WIDTH)
    put(_C_CC, _OFFS[7], 2 * CONV_WIDTH)
    put(_C_CB, _OFFS[6], CONV_WIDTH)
    put(_C_CB + CONV_WIDTH, _OFFS[9], CONV_WIDTH)
    wout_o[...] = wout_ref[...].astype(_BF)

    for hd in range(MLA_HEADS):
        s0, d0 = hd * QK_DIM, hd * HEAD_PAD
        r0 = s0 + NOPE_DIM
        wuq_o[:, d0:d0 + QK_DIM] = wuq_ref[:, s0:s0 + QK_DIM].astype(_BF)
        wuq_o[:, d0 + QK_DIM:d0 + QK_DIM + AXIS_DIM] = (-wuq_ref[:, r0 + AXIS_DIM:r0 + ROPE_DIM]).astype(_BF)
        wuq_o[:, d0 + QK_DIM + AXIS_DIM:d0 + HEAD_PAD] = wuq_ref[:, r0:r0 + AXIS_DIM].astype(_BF)
    for hd in range(MLA_HEADS):
        k0 = hd * KV_HEAD
        wukt_o[hd * NOPE_DIM:(hd + 1) * NOPE_DIM, :] = wukv_ref[:, k0:k0 + NOPE_DIM].T.astype(_BF)
        wuv_o[:, hd * V_DIM:(hd + 1) * V_DIM] = wukv_ref[:, k0 + NOPE_DIM:k0 + KV_HEAD].astype(_BF)
    poolw_o[...] = jnp.zeros((POOL_WIDTH, POOL_WIDTH), _BF)
    for g in range(POOL_GROUPS):
        sl = slice(g * POOL_GROUP_DIM, (g + 1) * POOL_GROUP_DIM)
        poolw_o[sl, sl] = poolw_ref[g].astype(_BF)


def _prep_weights(w_in_t, w_out, w_uq, w_ukv, pool_w):
    per_layer = lambda *shape: pl.BlockSpec((None,) + shape, lambda l: (l,) + (0,) * len(shape))
    return pl.pallas_call(
        _prep_kernel,
        grid=(DEPTH,),
        in_specs=[
            per_layer(IN_WIDTH, D_MODEL),
            per_layer(MIX_WIDTH, D_MODEL),
            per_layer(Q_LORA, MLA_HEADS * QK_DIM),
            per_layer(KV_LORA, MLA_HEADS * KV_HEAD),
            per_layer(POOL_GROUPS, POOL_GROUP_DIM, POOL_GROUP_DIM),
        ],
        out_specs=[
            per_layer(D_MODEL, _C_END),
            per_layer(MIX_WIDTH, D_MODEL),
            per_layer(Q_LORA, MLA_HEADS * HEAD_PAD),
            per_layer(MLA_HEADS * NOPE_DIM, KV_LORA),
            per_layer(KV_LORA, MLA_WIDTH),
            per_layer(POOL_WIDTH, POOL_WIDTH),
        ],
        out_shape=[
            jax.ShapeDtypeStruct((DEPTH, D_MODEL, _C_END), _BF),
            jax.ShapeDtypeStruct((DEPTH, MIX_WIDTH, D_MODEL), _BF),
            jax.ShapeDtypeStruct((DEPTH, Q_LORA, MLA_HEADS * HEAD_PAD), _BF),
            jax.ShapeDtypeStruct((DEPTH, MLA_HEADS * NOPE_DIM, KV_LORA), _BF),
            jax.ShapeDtypeStruct((DEPTH, KV_LORA, MLA_WIDTH), _BF),
            jax.ShapeDtypeStruct((DEPTH, POOL_WIDTH, POOL_WIDTH), _BF),
        ],
        compiler_params=pltpu.CompilerParams(dimension_semantics=("arbitrary",),
                                             vmem_limit_bytes=VMEM_LIMIT_BYTES),
        name="weight_prep",
    )(w_in_t, w_out, w_uq, w_ukv, pool_w)


def _pool_mix(win, rcnt):
    n = win.shape[0]
    rows = n - 2 * HALO
    lane = lax.broadcasted_iota(jnp.int32, (1, POOL_WIDTH), 1)
    sums = [win + pltpu.roll(win, 1, axis=0)]
    for k in (1, 2, 4):
        sums.append(pltpu.roll(sums[-1], k, axis=0) + pltpu.roll(sums[-1], n - k, axis=0))
    sel = sums[-1]
    for g in range(POOL_GROUPS - 2, -1, -1):
        sel = jnp.where(lane < (g + 1) * POOL_GROUP_DIM, sums[g], sel)
    return sel[HALO:HALO + rows] * rcnt - win[HALO:HALO + rows]


def _short_conv(win, w):
    rows = win.shape[0] - 2 * HALO
    return (win[HALO - 1:HALO - 1 + rows] * w[0] + win[HALO:HALO + rows] * w[1]
            + win[HALO + 1:HALO + 1 + rows] * w[2])


def _pass_kernel(*refs, n_seq, seq_len, cache_len, has_rope, emit_state, unroll, rc):
    it = iter(refs)
    x_ref, mod_ref = next(it), next(it)
    if cache_len:
        cckv_all_ref, ckr_all_ref = next(it), next(it)
    if has_rope:
        cs_ref = next(it)
    (rcnt_ref, gn_ref, win_all_ref, gq_ref, wuq_all_ref, gkv_ref, wukt_all_ref, wuv_all_ref, poolw_all_ref,
     pools_ref, convw_ref, wout_all_ref, gfin_ref) = (next(it) for _ in range(13))
    y_ref = next(it)
    if emit_state:
        ckv_out_ref, kr_out_ref = next(it), next(it)
    h_s, q_s, kt_s, v_s, px_s, prod_s, mixed_s = (next(it) for _ in range(7))

    m = n_seq * seq_len
    keys = cache_len + seq_len
    padded = seq_len + 2 * HALO
    lane = lax.broadcasted_iota(jnp.int32, (1, ROPE_PAD), 1)
    mrow = (1 + pl.program_id(0)) if has_rope else 0

    for c in range(m // rc):
        y_ref[c * rc:(c + 1) * rc, :] = x_ref[c * rc:(c + 1) * rc, :]

    zeros_halo = jnp.zeros((HALO, POOL_WIDTH), _F32)
    for buf in (px_s, prod_s):
        for s in range(n_seq):
            buf[s * padded:s * padded + HALO, :] = zeros_halo
            buf[(s + 1) * padded - HALO:(s + 1) * padded, :] = zeros_halo
    ones_col = jnp.where(lax.broadcasted_iota(jnp.int32, (n_seq * keys, V_DIM), 1) == 0, 1.0, 0.0).astype(_BF)
    for hd in range(MLA_HEADS):
        v_s[hd, :, V_DIM:] = ones_col

    def layer_view(l):
        lw = dict(l=l, row=pl.ds(l, 1), win=win_all_ref.at[l], wuq=wuq_all_ref.at[l], wukt=wukt_all_ref.at[l],
                  wuv=wuv_all_ref.at[l], poolw=poolw_all_ref.at[l], wout=wout_all_ref.at[l])
        lw["shift"], lw["scale"], lw["gate"] = (mod_ref[l, k, pl.ds(mrow, 1), :] for k in range(3))
        if cache_len:
            lw["cckv"], lw["ckr"] = cckv_all_ref.at[l], ckr_all_ref.at[l]
        return lw

    def rotate(v, r):
        v = v * cs_ref[pl.ds(r, rc), :]
        return v + pltpu.roll(v, ROPE_DIM, axis=1)

    def put_kv(lw, ckv, kr_t, kc):
        knt = _dot(lw["wukt"][...], ckv.T.astype(_BF))
        v4 = _dot(ckv.astype(_BF), lw["wuv"][...])
        kr_bf = kr_t.astype(_BF)
        rows_k = pl.ds(_aligned(kc * rc, rc), rc)
        for hd in range(MLA_HEADS):
            kt_s[hd, kc, :NOPE_DIM, :] = knt[hd * NOPE_DIM:(hd + 1) * NOPE_DIM, :].astype(_BF)
            kt_s[hd, kc, NOPE_DIM:, :] = kr_bf
            v_s[hd, rows_k, :V_DIM] = v4[:, hd * V_DIM:(hd + 1) * V_DIM].astype(_BF)

    def cached_keys(lw):
        ckr_t = lw["ckr"][...]
        ckr_pad = jnp.concatenate([ckr_t, jnp.zeros_like(ckr_t)], axis=0)
        for cc in range(cache_len // rc):
            rows_k = slice(cc * rc, (cc + 1) * rc)
            put_kv(lw, lw["cckv"][rows_k, :], ckr_pad[:, rows_k], cc)

    piece = min(seq_len, rc)
    n_sub = rc // piece
    pieces = [(s, slice(s * piece, (s + 1) * piece)) for s in range(n_sub)]

    def halo_base(c, s):
        return (c * n_sub + s) * padded if seq_len <= rc else c * rc

    def project(lw, c):
        r = _aligned(c * rc, rc)
        rows = pl.ds(r, rc)
        row, win_ref = lw["row"], lw["win"]
        h = (_rms(y_ref[rows, :], gn_ref[row, :]) * (1.0 + lw["scale"]) + lw["shift"]).astype(_BF)
        h_s[rows, :] = h
        a = _dot(h, win_ref[:, _C_CQ:_C_GMLA])
        cq = _rms(a[:, :Q_LORA], gq_ref[row, :]).astype(_BF)
        ckv = _rms(a[:, _C_CKV:_C_KR], gkv_ref[row, :])
        kr = a[:, _C_KR:]
        if has_rope:
            kr = rotate(kr, r)
        kr_t = jnp.where(lane < ROPE_DIM, kr, 0.0).T
        if emit_state:
            for s, ps in pieces:
                ckv_out_ref[c * n_sub + s, lw["l"]] = ckv[ps, :]
                kr_out_ref[c * n_sub + s, lw["l"]] = kr_t[:ROPE_DIM, ps]
        put_kv(lw, ckv, kr_t, cache_len // rc + c)
        for hp in range(MLA_HEADS // 2):
            q2 = _dot(cq, lw["wuq"][:, hp * 2 * HEAD_PAD:(hp + 1) * 2 * HEAD_PAD]) * Q_SCALE
            for j in range(2):
                hd = 2 * hp + j
                c0 = j * HEAD_PAD
                q_s[hd, rows, :NOPE_DIM] = q2[:, c0:c0 + NOPE_DIM].astype(_BF)
                qr = q2[:, c0 + NOPE_DIM:c0 + HEAD_PAD]
                if has_rope:
                    qr = rotate(qr, r)
                q_s[hd, rows, NOPE_DIM:] = qr.astype(_BF)
        pe = _dot(h, win_ref[:, _C_PX:_C_CB])
        for s, ps in pieces:
            rows_h = pl.ds(halo_base(c, s) + HALO, piece)
            px_s[rows_h, :] = pe[ps, :POOL_WIDTH]
            prod_s[rows_h, :] = pe[ps, POOL_WIDTH:POOL_WIDTH + CONV_WIDTH] * pe[ps, POOL_WIDTH + CONV_WIDTH:]

    def mix(lw, c):
        r = _aligned(c * rc, rc)
        rows = pl.ds(r, rc)
        row, win_ref = lw["row"], lw["win"]
        h = h_s[rows, :]
        for hp in range(MLA_HEADS // 2):
            g2 = _silu(_dot(h, win_ref[:, _C_GMLA + hp * 2 * V_DIM:_C_GMLA + (hp + 1) * 2 * V_DIM]))
            for j in range(2):
                hd = 2 * hp + j
                for s, ps in pieces:
                    rows_q = pl.ds(r + s * piece, piece)
                    if n_seq == 1:
                        ks = [kt_s[hd, kc] for kc in range(keys // rc)]
                        rows_k = pl.ds(0, keys)
                    else:
                        ks = [kt_s[hd, c, :, ps]]
                        rows_k = rows_q
                    q = q_s[hd, rows_q, :]
                    sc = jnp.concatenate([_dot(q, k) for k in ks], axis=-1)
                    p = jnp.exp2(sc - jnp.max(sc, axis=-1, keepdims=True))
                    ov = _dot(p.astype(_BF), v_s[hd, rows_k, :])
                    o = ov[:, :V_DIM] / ov[:, V_DIM:V_DIM + 1]
                    mixed_s[rows_q, hd * V_DIM:(hd + 1) * V_DIM] = (g2[ps, j * V_DIM:(j + 1) * V_DIM] * o).astype(_BF)

        windows = [pl.ds(halo_base(c, s), piece + 2 * HALO) for s, _ in pieces]
        rcnt = rcnt_ref[pl.ds(0 if seq_len <= rc else r, piece), :]
        pooled = jnp.concatenate([_pool_mix(px_s[w, :], rcnt) for w in windows], axis=0)
        pool = _dot(pooled.astype(_BF), lw["poolw"][...]) * pools_ref[row, :]
        gp = _silu(_dot(h, win_ref[:, _C_GPOOL:_C_PX]))
        mixed_s[rows, MLA_WIDTH:MLA_WIDTH + POOL_WIDTH] = (gp * pool).astype(_BF)

        convw = [convw_ref[k, row, :] for k in range(3)]
        conv = jnp.concatenate([_short_conv(prod_s[w, :], convw) for w in windows], axis=0)
        e2 = _dot(h, win_ref[:, _C_CB:_C_END])
        mixed_s[rows, MLA_WIDTH + POOL_WIDTH:] = (_silu(e2[:, CONV_WIDTH:]) * (e2[:, :CONV_WIDTH] * conv)).astype(_BF)

        y_ref[rows, :] = y_ref[rows, :] + lw["gate"] * _dot(mixed_s[rows, :], lw["wout"][...])

    def one_layer(l, carry):
        lw = layer_view(l)
        if cache_len:
            cached_keys(lw)
        _for_chunks(m // rc, unroll, functools.partial(project, lw))
        _for_chunks(m // rc, unroll, functools.partial(mix, lw))
        return carry

    lax.fori_loop(0, DEPTH, one_layer, 0)
    for c in range(m // rc):
        rows = slice(c * rc, (c + 1) * rc)
        y_ref[rows, :] = _rms(y_ref[rows, :], gfin_ref[...])


def _mixer_pass(x2d, mod_all, wts, *, seq_len, cache=None, rope=None, emit_state):
    rows = x2d.shape[0]
    m = ROWS_PER_STEP
    n_seq = m // seq_len
    n_steps = rows // m
    n_all = rows // seq_len
    cache_len = 0 if cache is None else cache[0].shape[2]
    keys = cache_len + seq_len
    has_rope = rope is not None
    rc = ROW_CHUNK if cache_len else m
    assert m % rc == 0 and cache_len % rc == 0
    assert (rc % seq_len == 0 and not cache_len) or (n_seq == 1 and seq_len % rc == 0 and not emit_state)

    def const(*shape):
        return pl.BlockSpec(shape, lambda i: (0,) * len(shape), pipeline_mode=pl.Buffered(1))

    args = [x2d, mod_all]
    in_specs = [pl.BlockSpec((m, D_MODEL), lambda i: (i, 0)), const(DEPTH, 3, MOD_ROWS, D_MODEL)]
    if cache_len:
        args += [cache[0], cache[1]]
        in_specs += [pl.BlockSpec((None, DEPTH, cache_len, KV_LORA), lambda i: (i, 0, 0, 0)),
                     pl.BlockSpec((None, DEPTH, ROPE_DIM, cache_len), lambda i: (i, 0, 0, 0))]
    if has_rope:
        args += [rope]
        in_specs += [const(seq_len, ROPE_PAD)]
    args += [_pool_rcnt(seq_len), wts["g_norm"], wts["w_in"], wts["g_q"], wts["w_uq"], wts["g_kv"],
             wts["w_ukt"], wts["w_uv"], wts["pool_w"], wts["pool_s"], wts["conv_w"], wts["w_out"],
             wts["g_final"]]
    in_specs += [const(seq_len, POOL_WIDTH), const(DEPTH, D_MODEL), const(DEPTH, D_MODEL, _C_END),
                 const(DEPTH, Q_LORA), const(DEPTH, Q_LORA, MLA_HEADS * HEAD_PAD), const(DEPTH, KV_LORA),
                 const(DEPTH, MLA_HEADS * NOPE_DIM, KV_LORA), const(DEPTH, KV_LORA, MLA_WIDTH),
                 const(DEPTH, POOL_WIDTH, POOL_WIDTH), const(DEPTH, POOL_WIDTH), const(3, DEPTH, CONV_WIDTH),
                 const(DEPTH, MIX_WIDTH, D_MODEL), const(1, D_MODEL)]

    out_shape = [jax.ShapeDtypeStruct((rows, D_MODEL), _F32)]
    out_specs = [pl.BlockSpec((m, D_MODEL), lambda i: (i, 0))]
    if emit_state:
        out_shape += [jax.ShapeDtypeStruct((n_all, DEPTH, seq_len, KV_LORA), _F32),
                      jax.ShapeDtypeStruct((n_all, DEPTH, ROPE_DIM, seq_len), _F32)]
        out_specs += [pl.BlockSpec((n_seq, DEPTH, seq_len, KV_LORA), lambda i: (i, 0, 0, 0)),
                      pl.BlockSpec((n_seq, DEPTH, ROPE_DIM, seq_len), lambda i: (i, 0, 0, 0))]

    kern = functools.partial(_pass_kernel, n_seq=n_seq, seq_len=seq_len, cache_len=cache_len,
                             has_rope=has_rope, emit_state=emit_state, unroll=CHUNK_UNROLL, rc=rc)
    return pl.pallas_call(
        kern,
        grid=(n_steps,),
        in_specs=in_specs,
        out_specs=out_specs,
        out_shape=out_shape,
        scratch_shapes=[
            pltpu.VMEM((m, D_MODEL), _BF),
            pltpu.VMEM((MLA_HEADS, m, HEAD_PAD), _BF),
            pltpu.VMEM((MLA_HEADS, n_seq * keys // rc, HEAD_PAD, rc), _BF),
            pltpu.VMEM((MLA_HEADS, n_seq * keys, 2 * V_DIM), _BF),
            pltpu.VMEM((n_seq * (seq_len + 2 * HALO), POOL_WIDTH), _F32),
            pltpu.VMEM((n_seq * (seq_len + 2 * HALO), CONV_WIDTH), _F32),
            pltpu.VMEM((m, MIX_WIDTH), _BF),
        ],
        compiler_params=pltpu.CompilerParams(dimension_semantics=("arbitrary",),
                                             vmem_limit_bytes=VMEM_LIMIT_BYTES),
        name="mixer_pass_latent" if has_rope else "mixer_pass_context",
    )(*args)


def _pool_rcnt(seq_len):
    t = np.arange(seq_len)[:, None]
    half = np.repeat(1 << np.arange(POOL_GROUPS), POOL_GROUP_DIM)[None, :]
    cnt = np.minimum(t + half, seq_len) - np.maximum(t - half, 0)
    return jnp.asarray((1.0 / cnt).astype(np.float32))


def _rope_table(seq_len):
    rows = seq_len // GRID_W
    row = np.repeat(np.arange(rows), GRID_W).astype(np.float32)
    col = np.tile(np.arange(GRID_W), rows).astype(np.float32)
    inv = (1.0 / (np.float32(ROPE_BASE) ** (np.arange(0, AXIS_DIM, 2, dtype=np.float32) / np.float32(AXIS_DIM))))
    inv = inv.astype(np.float32)
    ang = np.concatenate([row[:, None] * inv, col[:, None] * inv], axis=-1).astype(np.float64)
    cos, sin = np.cos(ang).astype(np.float32), np.sin(ang).astype(np.float32)
    return jnp.asarray(np.concatenate([cos, cos, sin, sin], axis=-1))


def kernel(x_prompt, x_sample, cache_ckv, cache_krope, c, c_ctx, w_mod, b_mod, g_norm, w_in, g_q, w_uq,
           g_kv, w_ukv, pool_w, pool_s, conv_w, w_out, g_final):
    batch, seq, _ = x_prompt.shape
    dec_batch, dec_seq, _ = x_sample.shape
    assert 1 + dec_batch <= MOD_ROWS and dec_seq == ROWS_PER_STEP and ROWS_PER_STEP % seq == 0 and DEPTH == 2

    c_all = jnp.concatenate([c_ctx[None], c, jnp.zeros((MOD_ROWS - 1 - dec_batch, D_MODEL), _F32)], axis=0)
    mod_all = _modulation(c_all, w_mod, b_mod)
    w_in_r, w_out_r, w_uq_r, w_ukt, w_uv, pool_w_r = _prep_weights(
        jnp.transpose(w_in, (0, 2, 1)), w_out, w_uq, w_ukv, pool_w)
    wts = {
        "g_norm": g_norm, "w_in": w_in_r, "g_q": g_q, "w_uq": w_uq_r, "g_kv": g_kv, "w_ukt": w_ukt,
        "w_uv": w_uv, "pool_w": pool_w_r, "pool_s": pool_s, "conv_w": jnp.transpose(conv_w, (1, 0, 2)), "w_out": w_out_r,
        "g_final": g_final.reshape(1, D_MODEL),
    }

    rope = _rope_table(dec_seq)
    cache = (cache_ckv, jnp.transpose(cache_krope, (0, 1, 3, 2)))
    xp = x_prompt.reshape(batch * seq, D_MODEL)
    xs = x_sample.reshape(dec_batch * dec_seq, D_MODEL)
    xp, state_ckv, state_krope_t = _mixer_pass(xp, mod_all, wts, seq_len=seq, emit_state=True)
    (xs,) = _mixer_pass(xs, mod_all, wts, seq_len=dec_seq, cache=cache, rope=rope, emit_state=False)
    y_prompt = xp.reshape(batch, seq, D_MODEL)
    state_krope = jnp.transpose(state_krope_t, (0, 1, 3, 2))
    y_sample = xs.reshape(dec_batch, dec_seq, D_MODEL)
    return (y_prompt, y_sample, state_ckv, state_krope)
```

```python
import functools

import numpy as np
import jax
import jax.numpy as jnp
from jax import lax
from jax.experimental import pallas as pl
from jax.experimental.pallas import tpu as pltpu

D_MODEL = 1024
DEPTH = 2
GRID_W = 64
MLA_HEADS = 4
NOPE_DIM = 128
ROPE_DIM = 64
V_DIM = 128
QK_DIM = NOPE_DIM + ROPE_DIM
MLA_WIDTH = MLA_HEADS * V_DIM
Q_LORA = 384
KV_LORA = 256
POOL_GROUPS = 4
POOL_GROUP_DIM = 64
POOL_WIDTH = POOL_GROUPS * POOL_GROUP_DIM
CONV_WIDTH = 256
MIX_WIDTH = MLA_WIDTH + POOL_WIDTH + CONV_WIDTH
ROPE_BASE = 10000.0
AXIS_DIM = ROPE_DIM // 2
ATTN_SCALE = QK_DIM ** -0.5
Q_SCALE = ATTN_SCALE * float(np.log2(np.e))
EPS = 1e-6

_SPLITS = (Q_LORA, KV_LORA, ROPE_DIM, MLA_WIDTH, POOL_WIDTH, POOL_WIDTH,
           CONV_WIDTH, CONV_WIDTH, CONV_WIDTH, CONV_WIDTH)
_OFFS = [sum(_SPLITS[:i]) for i in range(len(_SPLITS) + 1)]
IN_WIDTH = _OFFS[-1]

V7X_LANES = 128
V7X_SUBLANES = 8
V7X_MXU_DEPTH = 256
HEAD_PAD = V7X_MXU_DEPTH
ROPE_PAD = HEAD_PAD - NOPE_DIM
KV_HEAD = NOPE_DIM + V_DIM

_C_CQ = 0
_C_CKV = _C_CQ + Q_LORA
_C_KR = _C_CKV + KV_LORA
_C_GMLA = _C_KR + ROPE_PAD
_C_GPOOL = _C_GMLA + MLA_WIDTH
_C_PX = _C_GPOOL + POOL_WIDTH
_C_CC = _C_PX + POOL_WIDTH
_C_CB = _C_CC + 2 * CONV_WIDTH
_C_END = _C_CB + 2 * CONV_WIDTH

MOD_ROWS = 8
ROWS_PER_STEP = 1024
ROW_CHUNK = 512
CHUNK_UNROLL = 4
HALO = V7X_SUBLANES
PREP_COLS = 256
VMEM_LIMIT_BYTES = 60 * 1024 * 1024

_BF = jnp.bfloat16
_F32 = jnp.float32


def _dot(a, b):
    return jnp.dot(a, b, preferred_element_type=_F32)


def _rms(x, g):
    return x * lax.rsqrt(jnp.mean(x * x, axis=-1, keepdims=True) + EPS) * g


def _silu(x):
    return x * jax.nn.sigmoid(x)


def _aligned(v, k):
    return v if isinstance(v, int) else pl.multiple_of(v, k)


def _for_chunks(n, unroll, body):
    if unroll >= n:
        for c in range(n):
            body(c)
        return

    def step(i, carry):
        for u in range(unroll):
            body(i * unroll + u)
        return carry

    lax.fori_loop(0, n // unroll, step, 0)


def _mod_kernel(c_ref, w_ref, b_ref, o_ref):
    s = _silu(c_ref[...])
    bias = b_ref[pl.ds(pl.program_id(0), 1), :]
    o_ref[...] = _dot(s.astype(_BF), w_ref[...].astype(_BF)) + bias


def _modulation(c_all, w_mod, b_mod):
    return pl.pallas_call(
        _mod_kernel,
        grid=(DEPTH, 3),
        in_specs=[
            pl.BlockSpec((MOD_ROWS, D_MODEL), lambda l, j: (0, 0)),
            pl.BlockSpec((None, D_MODEL, D_MODEL), lambda l, j: (l, 0, j)),
            pl.BlockSpec((DEPTH, D_MODEL), lambda l, j: (0, j)),
        ],
        out_specs=pl.BlockSpec((None, None, MOD_ROWS, D_MODEL), lambda l, j: (l, j, 0, 0)),
        out_shape=jax.ShapeDtypeStruct((DEPTH, 3, MOD_ROWS, D_MODEL), _F32),
        name="modulation",
    )(c_all, w_mod, b_mod)


def _prep_kernel(wint_ref, wout_ref, wuq_ref, wukv_ref, poolw_ref,
                 win_o, wout_o, wuq_o, wukt_o, wuv_o, poolw_o):
    def put(dst, src, width):
        for c in range(0, width, PREP_COLS):
            n = min(PREP_COLS, width - c)
            win_o[:, dst + c:dst + c + n] = wint_ref[src + c:src + c + n, :].T.astype(_BF)

    put(_C_CQ, _OFFS[0], Q_LORA + KV_LORA)
    kr = wint_ref[_OFFS[2]:_OFFS[3], :]
    krb = jnp.concatenate([kr, -kr[AXIS_DIM:], kr[:AXIS_DIM]], axis=0)
    win_o[:, _C_KR:_C_GMLA] = krb.T.astype(_BF)
    put(_C_GMLA, _OFFS[3], MLA_WIDTH)
    put(_C_GPOOL, _OFFS[5], POOL_WIDTH)
    put(_C_PX, _OFFS[4], POOL_WIDTH)
    put(_C_CC, _OFFS[7], 2 * CONV_WIDTH)
    put(_C_CB, _OFFS[6], CONV_WIDTH)
    put(_C_CB + CONV_WIDTH, _OFFS[9], CONV_WIDTH)
    wout_o[...] = wout_ref[...].astype(_BF)

    for hd in range(MLA_HEADS):
        s0, d0 = hd * QK_DIM, hd * HEAD_PAD
        r0 = s0 + NOPE_DIM
        wuq_o[:, d0:d0 + QK_DIM] = wuq_ref[:, s0:s0 + QK_DIM].astype(_BF)
        wuq_o[:, d0 + QK_DIM:d0 + QK_DIM + AXIS_DIM] = (-wuq_ref[:, r0 + AXIS_DIM:r0 + ROPE_DIM]).astype(_BF)
        wuq_o[:, d0 + QK_DIM + AXIS_DIM:d0 + HEAD_PAD] = wuq_ref[:, r0:r0 + AXIS_DIM].astype(_BF)
    for hd in range(MLA_HEADS):
        k0 = hd * KV_HEAD
        wukt_o[hd * NOPE_DIM:(hd + 1) * NOPE_DIM, :] = wukv_ref[:, k0:k0 + NOPE_DIM].T.astype(_BF)
        wuv_o[:, hd * V_DIM:(hd + 1) * V_DIM] = wukv_ref[:, k0 + NOPE_DIM:k0 + KV_HEAD].astype(_BF)
    poolw_o[...] = jnp.zeros((POOL_WIDTH, POOL_WIDTH), _BF)
    for g in range(POOL_GROUPS):
        sl = slice(g * POOL_GROUP_DIM, (g + 1) * POOL_GROUP_DIM)
        poolw_o[sl, sl] = poolw_ref[g].astype(_BF)


def _prep_weights(w_in_t, w_out, w_uq, w_ukv, pool_w):
    per_layer = lambda *shape: pl.BlockSpec((None,) + shape, lambda l: (l,) + (0,) * len(shape))
    return pl.pallas_call(
        _prep_kernel,
        grid=(DEPTH,),
        in_specs=[
            per_layer(IN_WIDTH, D_MODEL),
            per_layer(MIX_WIDTH, D_MODEL),
            per_layer(Q_LORA, MLA_HEADS * QK_DIM),
            per_layer(KV_LORA, MLA_HEADS * KV_HEAD),
            per_layer(POOL_GROUPS, POOL_GROUP_DIM, POOL_GROUP_DIM),
        ],
        out_specs=[
            per_layer(D_MODEL, _C_END),
            per_layer(MIX_WIDTH, D_MODEL),
            per_layer(Q_LORA, MLA_HEADS * HEAD_PAD),
            per_layer(MLA_HEADS * NOPE_DIM, KV_LORA),
            per_layer(KV_LORA, MLA_WIDTH),
            per_layer(POOL_WIDTH, POOL_WIDTH),
        ],
        out_shape=[
            jax.ShapeDtypeStruct((DEPTH, D_MODEL, _C_END), _BF),
            jax.ShapeDtypeStruct((DEPTH, MIX_WIDTH, D_MODEL), _BF),
            jax.ShapeDtypeStruct((DEPTH, Q_LORA, MLA_HEADS * HEAD_PAD), _BF),
            jax.ShapeDtypeStruct((DEPTH, MLA_HEADS * NOPE_DIM, KV_LORA), _BF),
            jax.ShapeDtypeStruct((DEPTH, KV_LORA, MLA_WIDTH), _BF),
            jax.ShapeDtypeStruct((DEPTH, POOL_WIDTH, POOL_WIDTH), _BF),
        ],
        compiler_params=pltpu.CompilerParams(dimension_semantics=("arbitrary",),
                                             vmem_limit_bytes=VMEM_LIMIT_BYTES),
        name="weight_prep",
    )(w_in_t, w_out, w_uq, w_ukv, pool_w)


def _pool_mix(win, rcnt):
    n = win.shape[0]
    rows = n - 2 * HALO
    lane = lax.broadcasted_iota(jnp.int32, (1, POOL_WIDTH), 1)
    sums = [win + pltpu.roll(win, 1, axis=0)]
    for k in (1, 2, 4):
        sums.append(pltpu.roll(sums[-1], k, axis=0) + pltpu.roll(sums[-1], n - k, axis=0))
    sel = sums[-1]
    for g in range(POOL_GROUPS - 2, -1, -1):
        sel = jnp.where(lane < (g + 1) * POOL_GROUP_DIM, sums[g], sel)
    return sel[HALO:HALO + rows] * rcnt - win[HALO:HALO + rows]


def _short_conv(win, w):
    rows = win.shape[0] - 2 * HALO
    return (win[HALO - 1:HALO - 1 + rows] * w[0] + win[HALO:HALO + rows] * w[1]
            + win[HALO + 1:HALO + 1 + rows] * w[2])


def _pass_kernel(*refs, n_seq, seq_len, cache_len, has_rope, emit_state, unroll, rc):
    it = iter(refs)
    x_ref, mod_ref = next(it), next(it)
    if cache_len:
        cckv_all_ref, ckr_all_ref = next(it), next(it)
    if has_rope:
        cs_ref = next(it)
    (rcnt_ref, gn_ref, win_all_ref, gq_ref, wuq_all_ref, gkv_ref, wukt_all_ref, wuv_all_ref, poolw_all_ref,
     pools_ref, convw_ref, wout_all_ref, gfin_ref) = (next(it) for _ in range(13))
    y_ref = next(it)
    if emit_state:
        ckv_out_ref, kr_out_ref = next(it), next(it)
    h_s, q_s, kt_s, v_s, px_s, prod_s, mixed_s = (next(it) for _ in range(7))

    m = n_seq * seq_len
    keys = cache_len + seq_len
    padded = seq_len + 2 * HALO
    lane = lax.broadcasted_iota(jnp.int32, (1, ROPE_PAD), 1)
    mrow = (1 + pl.program_id(0)) if has_rope else 0

    for c in range(m // rc):
        y_ref[c * rc:(c + 1) * rc, :] = x_ref[c * rc:(c + 1) * rc, :]

    zeros_halo = jnp.zeros((HALO, POOL_WIDTH), _F32)
    for buf in (px_s, prod_s):
        for s in range(n_seq):
            buf[s * padded:s * padded + HALO, :] = zeros_halo
            buf[(s + 1) * padded - HALO:(s + 1) * padded, :] = zeros_halo
    ones_col = jnp.where(lax.broadcasted_iota(jnp.int32, (n_seq * keys, V_DIM), 1) == 0, 1.0, 0.0).astype(_BF)
    for hd in range(MLA_HEADS):
        v_s[hd, :, V_DIM:] = ones_col

    def layer_view(l):
        lw = dict(l=l, row=pl.ds(l, 1), win=win_all_ref.at[l], wuq=wuq_all_ref.at[l], wukt=wukt_all_ref.at[l],
                  wuv=wuv_all_ref.at[l], poolw=poolw_all_ref.at[l], wout=wout_all_ref.at[l])
        lw["shift"], lw["scale"], lw["gate"] = (mod_ref[l, k, pl.ds(mrow, 1), :] for k in range(3))
        if cache_len:
            lw["cckv"], lw["ckr"] = cckv_all_ref.at[l], ckr_all_ref.at[l]
        return lw

    def rotate(v, r):
        v = v * cs_ref[pl.ds(r, rc), :]
        return v + pltpu.roll(v, ROPE_DIM, axis=1)

    def put_kv(lw, ckv, kr_t, kc):
        knt = _dot(lw["wukt"][...], ckv.T.astype(_BF))
        v4 = _dot(ckv.astype(_BF), lw["wuv"][...])
        kr_bf = kr_t.astype(_BF)
        rows_k = pl.ds(_aligned(kc * rc, rc), rc)
        for hd in range(MLA_HEADS):
            kt_s[hd, kc, :NOPE_DIM, :] = knt[hd * NOPE_DIM:(hd + 1) * NOPE_DIM, :].astype(_BF)
            kt_s[hd, kc, NOPE_DIM:, :] = kr_bf
            v_s[hd, rows_k, :V_DIM] = v4[:, hd * V_DIM:(hd + 1) * V_DIM].astype(_BF)

    def cached_keys(lw):
        ckr_t = lw["ckr"][...]
        ckr_pad = jnp.concatenate([ckr_t, jnp.zeros_like(ckr_t)], axis=0)
        for cc in range(cache_len // rc):
            rows_k = slice(cc * rc, (cc + 1) * rc)
            put_kv(lw, lw["cckv"][rows_k, :], ckr_pad[:, rows_k], cc)

    piece = min(seq_len, rc)
    n_sub = rc // piece
    pieces = [(s, slice(s * piece, (s + 1) * piece)) for s in range(n_sub)]

    def halo_base(c, s):
        return (c * n_sub + s) * padded if seq_len <= rc else c * rc

    def project(lw, c):
        r = _aligned(c * rc, rc)
        rows = pl.ds(r, rc)
        row, win_ref = lw["row"], lw["win"]
        h = (_rms(y_ref[rows, :], gn_ref[row, :]) * (1.0 + lw["scale"]) + lw["shift"]).astype(_BF)
        h_s[rows, :] = h
        a = _dot(h, win_ref[:, _C_CQ:_C_GMLA])
        cq = _rms(a[:, :Q_LORA], gq_ref[row, :]).astype(_BF)
        ckv = _rms(a[:, _C_CKV:_C_KR], gkv_ref[row, :])
        kr = a[:, _C_KR:]
        if has_rope:
            kr = rotate(kr, r)
        kr_t = jnp.where(lane < ROPE_DIM, kr, 0.0).T
        if emit_state:
            for s, ps in pieces:
                ckv_out_ref[c * n_sub + s, lw["l"]] = ckv[ps, :]
                kr_out_ref[c * n_sub + s, lw["l"]] = kr_t[:ROPE_DIM, ps]
        put_kv(lw, ckv, kr_t, cache_len // rc + c)
        for hp in range(MLA_HEADS // 2):
            q2 = _dot(cq, lw["wuq"][:, hp * 2 * HEAD_PAD:(hp + 1) * 2 * HEAD_PAD]) * Q_SCALE
            for j in range(2):
                hd = 2 * hp + j
                c0 = j * HEAD_PAD
                q_s[hd, rows, :NOPE_DIM] = q2[:, c0:c0 + NOPE_DIM].astype(_BF)
                qr = q2[:, c0 + NOPE_DIM:c0 + HEAD_PAD]
                if has_rope:
                    qr = rotate(qr, r)
                q_s[hd, rows, NOPE_DIM:] = qr.astype(_BF)
        pe = _dot(h, win_ref[:, _C_PX:_C_CB])
        for s, ps in pieces:
            rows_h = pl.ds(halo_base(c, s) + HALO, piece)
            px_s[rows_h, :] = pe[ps, :POOL_WIDTH]
            prod_s[rows_h, :] = pe[ps, POOL_WIDTH:POOL_WIDTH + CONV_WIDTH] * pe[ps, POOL_WIDTH + CONV_WIDTH:]

    def mix(lw, c):
        r = _aligned(c * rc, rc)
        rows = pl.ds(r, rc)
        row, win_ref = lw["row"], lw["win"]
        h = h_s[rows, :]
        for hp in range(MLA_HEADS // 2):
            g2 = _silu(_dot(h, win_ref[:, _C_GMLA + hp * 2 * V_DIM:_C_GMLA + (hp + 1) * 2 * V_DIM]))
            for j in range(2):
                hd = 2 * hp + j
                for s, ps in pieces:
                    rows_q = pl.ds(r + s * piece, piece)
                    if n_seq == 1:
                        ks = [kt_s[hd, kc] for kc in range(keys // rc)]
                        rows_k = pl.ds(0, keys)
                    else:
                        ks = [kt_s[hd, c, :, ps]]
                        rows_k = rows_q
                    q = q_s[hd, rows_q, :]
                    sc = jnp.concatenate([_dot(q, k) for k in ks], axis=-1)
                    p = jnp.exp2(sc - jnp.max(sc, axis=-1, keepdims=True))
                    ov = _dot(p.astype(_BF), v_s[hd, rows_k, :])
                    o = ov[:, :V_DIM] / ov[:, V_DIM:V_DIM + 1]
                    mixed_s[rows_q, hd * V_DIM:(hd + 1) * V_DIM] = (g2[ps, j * V_DIM:(j + 1) * V_DIM] * o).astype(_BF)

        windows = [pl.ds(halo_base(c, s), piece + 2 * HALO) for s, _ in pieces]
        rcnt = rcnt_ref[pl.ds(0 if seq_len <= rc else r, piece), :]
        pooled = jnp.concatenate([_pool_mix(px_s[w, :], rcnt) for w in windows], axis=0)
        pool = _dot(pooled.astype(_BF), lw["poolw"][...]) * pools_ref[row, :]
        gp = _silu(_dot(h, win_ref[:, _C_GPOOL:_C_PX]))
        mixed_s[rows, MLA_WIDTH:MLA_WIDTH + POOL_WIDTH] = (gp * pool).astype(_BF)

        convw = [convw_ref[k, row, :] for k in range(3)]
        conv = jnp.concatenate([_short_conv(prod_s[w, :], convw) for w in windows], axis=0)
        e2 = _dot(h, win_ref[:, _C_CB:_C_END])
        mixed_s[rows, MLA_WIDTH + POOL_WIDTH:] = (_silu(e2[:, CONV_WIDTH:]) * (e2[:, :CONV_WIDTH] * conv)).astype(_BF)

        y = y_ref[rows, :] + lw["gate"] * _dot(mixed_s[rows, :], lw["wout"][...])
        y_ref[rows, :] = jnp.where(lw["l"] == DEPTH - 1, _rms(y, gfin_ref[...]), y)

    def one_layer(l, carry):
        lw = layer_view(l)
        if cache_len:
            cached_keys(lw)
        _for_chunks(m // rc, unroll, functools.partial(project, lw))
        _for_chunks(m // rc, unroll, functools.partial(mix, lw))
        return carry

    lax.fori_loop(0, DEPTH, one_layer, 0)


def _mixer_pass(x2d, mod_all, wts, *, seq_len, cache=None, rope=None, emit_state):
    rows = x2d.shape[0]
    m = ROWS_PER_STEP
    n_seq = m // seq_len
    n_steps = rows // m
    n_all = rows // seq_len
    cache_len = 0 if cache is None else cache[0].shape[2]
    keys = cache_len + seq_len
    has_rope = rope is not None
    rc = ROW_CHUNK
    assert m % rc == 0 and cache_len % rc == 0
    assert (rc % seq_len == 0 and not cache_len) or (n_seq == 1 and seq_len % rc == 0 and not emit_state)

    def const(*shape):
        return pl.BlockSpec(shape, lambda i: (0,) * len(shape), pipeline_mode=pl.Buffered(1))

    args = [x2d, mod_all]
    in_specs = [pl.BlockSpec((m, D_MODEL), lambda i: (i, 0)), const(DEPTH, 3, MOD_ROWS, D_MODEL)]
    if cache_len:
        args += [cache[0], cache[1]]
        in_specs += [pl.BlockSpec((None, DEPTH, cache_len, KV_LORA), lambda i: (i, 0, 0, 0)),
                     pl.BlockSpec((None, DEPTH, ROPE_DIM, cache_len), lambda i: (i, 0, 0, 0))]
    if has_rope:
        args += [rope]
        in_specs += [const(seq_len, ROPE_PAD)]
    args += [_pool_rcnt(seq_len), wts["g_norm"], wts["w_in"], wts["g_q"], wts["w_uq"], wts["g_kv"],
             wts["w_ukt"], wts["w_uv"], wts["pool_w"], wts["pool_s"], wts["conv_w"], wts["w_out"],
             wts["g_final"]]
    in_specs += [const(seq_len, POOL_WIDTH), const(DEPTH, D_MODEL), const(DEPTH, D_MODEL, _C_END),
                 const(DEPTH, Q_LORA), const(DEPTH, Q_LORA, MLA_HEADS * HEAD_PAD), const(DEPTH, KV_LORA),
                 const(DEPTH, MLA_HEADS * NOPE_DIM, KV_LORA), const(DEPTH, KV_LORA, MLA_WIDTH),
                 const(DEPTH, POOL_WIDTH, POOL_WIDTH), const(DEPTH, POOL_WIDTH), const(3, DEPTH, CONV_WIDTH),
                 const(DEPTH, MIX_WIDTH, D_MODEL), const(1, D_MODEL)]

    out_shape = [jax.ShapeDtypeStruct((rows, D_MODEL), _F32)]
    out_specs = [pl.BlockSpec((m, D_MODEL), lambda i: (i, 0))]
    if emit_state:
        out_shape += [jax.ShapeDtypeStruct((n_all, DEPTH, seq_len, KV_LORA), _F32),
                      jax.ShapeDtypeStruct((n_all, DEPTH, ROPE_DIM, seq_len), _F32)]
        out_specs += [pl.BlockSpec((n_seq, DEPTH, seq_len, KV_LORA), lambda i: (i, 0, 0, 0)),
                      pl.BlockSpec((n_seq, DEPTH, ROPE_DIM, seq_len), lambda i: (i, 0, 0, 0))]

    kern = functools.partial(_pass_kernel, n_seq=n_seq, seq_len=seq_len, cache_len=cache_len,
                             has_rope=has_rope, emit_state=emit_state, unroll=CHUNK_UNROLL, rc=rc)
    return pl.pallas_call(
        kern,
        grid=(n_steps,),
        in_specs=in_specs,
        out_specs=out_specs,
        out_shape=out_shape,
        scratch_shapes=[
            pltpu.VMEM((m, D_MODEL), _BF),
            pltpu.VMEM((MLA_HEADS, m, HEAD_PAD), _BF),
            pltpu.VMEM((MLA_HEADS, n_seq * keys // rc, HEAD_PAD, rc), _BF),
            pltpu.VMEM((MLA_HEADS, n_seq * keys, 2 * V_DIM), _BF),
            pltpu.VMEM((n_seq * (seq_len + 2 * HALO), POOL_WIDTH), _F32),
            pltpu.VMEM((n_seq * (seq_len + 2 * HALO), CONV_WIDTH), _F32),
            pltpu.VMEM((m, MIX_WIDTH), _BF),
        ],
        compiler_params=pltpu.CompilerParams(dimension_semantics=("arbitrary",),
                                             vmem_limit_bytes=VMEM_LIMIT_BYTES),
        name="mixer_pass_latent" if has_rope else "mixer_pass_context",
    )(*args)


def _pool_rcnt(seq_len):
    t = np.arange(seq_len)[:, None]
    half = np.repeat(1 << np.arange(POOL_GROUPS), POOL_GROUP_DIM)[None, :]
    cnt = np.minimum(t + half, seq_len) - np.maximum(t - half, 0)
    return jnp.asarray((1.0 / cnt).astype(np.float32))


def _rope_table(seq_len):
    rows = seq_len // GRID_W
    row = np.repeat(np.arange(rows), GRID_W).astype(np.float32)
    col = np.tile(np.arange(GRID_W), rows).astype(np.float32)
    inv = (1.0 / (np.float32(ROPE_BASE) ** (np.arange(0, AXIS_DIM, 2, dtype=np.float32) / np.float32(AXIS_DIM))))
    inv = inv.astype(np.float32)
    ang = np.concatenate([row[:, None] * inv, col[:, None] * inv], axis=-1).astype(np.float64)
    cos, sin = np.cos(ang).astype(np.float32), np.sin(ang).astype(np.float32)
    return jnp.asarray(np.concatenate([cos, cos, sin, sin], axis=-1))


def kernel(x_prompt, x_sample, cache_ckv, cache_krope, c, c_ctx, w_mod, b_mod, g_norm, w_in, g_q, w_uq,
           g_kv, w_ukv, pool_w, pool_s, conv_w, w_out, g_final):
    batch, seq, _ = x_prompt.shape
    dec_batch, dec_seq, _ = x_sample.shape
    assert 1 + dec_batch <= MOD_ROWS and dec_seq == ROWS_PER_STEP and ROWS_PER_STEP % seq == 0 and DEPTH == 2

    c_all = jnp.concatenate([c_ctx[None], c, jnp.zeros((MOD_ROWS - 1 - dec_batch, D_MODEL), _F32)], axis=0)
    mod_all = _modulation(c_all, w_mod, b_mod)
    w_in_r, w_out_r, w_uq_r, w_ukt, w_uv, pool_w_r = _prep_weights(
        jnp.transpose(w_in, (0, 2, 1)), w_out, w_uq, w_ukv, pool_w)
    wts = {
        "g_norm": g_norm, "w_in": w_in_r, "g_q": g_q, "w_uq": w_uq_r, "g_kv": g_kv, "w_ukt": w_ukt,
        "w_uv": w_uv, "pool_w": pool_w_r, "pool_s": pool_s, "conv_w": jnp.transpose(conv_w, (1, 0, 2)), "w_out": w_out_r,
        "g_final": g_final.reshape(1, D_MODEL),
    }

    rope = _rope_table(dec_seq)
    cache = (cache_ckv, jnp.transpose(cache_krope, (0, 1, 3, 2)))
    xp = x_prompt.reshape(batch * seq, D_MODEL)
    xs = x_sample.reshape(dec_batch * dec_seq, D_MODEL)
    xp, state_ckv, state_krope_t = _mixer_pass(xp, mod_all, wts, seq_len=seq, emit_state=True)
    (xs,) = _mixer_pass(xs, mod_all, wts, seq_len=dec_seq, cache=cache, rope=rope, emit_state=False)
    y_prompt = xp.reshape(batch, seq, D_MODEL)
    state_krope = jnp.transpose(state_krope_t, (0, 1, 3, 2))
    y_sample = xs.reshape(dec_batch, dec_seq, D_MODEL)
    return (y_prompt, y_sample, state_ckv, state_krope)
```

```python
import functools

import numpy as np
import jax
import jax.numpy as jnp
from jax import lax
from jax.experimental import pallas as pl
from jax.experimental.pallas import tpu as pltpu

D_MODEL = 1024
DEPTH = 2
GRID_W = 64
MLA_HEADS = 4
NOPE_DIM = 128
ROPE_DIM = 64
V_DIM = 128
QK_DIM = NOPE_DIM + ROPE_DIM
MLA_WIDTH = MLA_HEADS * V_DIM
Q_LORA = 384
KV_LORA = 256
POOL_GROUPS = 4
POOL_GROUP_DIM = 64
POOL_WIDTH = POOL_GROUPS * POOL_GROUP_DIM
CONV_WIDTH = 256
MIX_WIDTH = MLA_WIDTH + POOL_WIDTH + CONV_WIDTH
ROPE_BASE = 10000.0
AXIS_DIM = ROPE_DIM // 2
ATTN_SCALE = QK_DIM ** -0.5
Q_SCALE = ATTN_SCALE * float(np.log2(np.e))
EPS = 1e-6

_SPLITS = (Q_LORA, KV_LORA, ROPE_DIM, MLA_WIDTH, POOL_WIDTH, POOL_WIDTH,
           CONV_WIDTH, CONV_WIDTH, CONV_WIDTH, CONV_WIDTH)
_OFFS = [sum(_SPLITS[:i]) for i in range(len(_SPLITS) + 1)]
IN_WIDTH = _OFFS[-1]

V7X_LANES = 128
V7X_SUBLANES = 8
V7X_MXU_DEPTH = 256
HEAD_PAD = V7X_MXU_DEPTH
ROPE_PAD = HEAD_PAD - NOPE_DIM
KV_HEAD = NOPE_DIM + V_DIM

_C_CQ = 0
_C_CKV = _C_CQ + Q_LORA
_C_KR = _C_CKV + KV_LORA
_C_GMLA = _C_KR + ROPE_PAD
_C_GPOOL = _C_GMLA + MLA_WIDTH
_C_PX = _C_GPOOL + POOL_WIDTH
_C_CC = _C_PX + POOL_WIDTH
_C_CB = _C_CC + 2 * CONV_WIDTH
_C_END = _C_CB + 2 * CONV_WIDTH

MOD_ROWS = 8
ROWS_PER_STEP = 1024
ROW_CHUNK = 512
CHUNK_UNROLL = 4
HALO = V7X_SUBLANES
PREP_COLS = 256
VMEM_LIMIT_BYTES = 60 * 1024 * 1024

_BF = jnp.bfloat16
_F32 = jnp.float32


def _dot(a, b):
    return jnp.dot(a, b, preferred_element_type=_F32)


def _rms(x, g):
    return x * lax.rsqrt(jnp.mean(x * x, axis=-1, keepdims=True) + EPS) * g


def _silu(x):
    return x * jax.nn.sigmoid(x)


def _aligned(v, k):
    return v if isinstance(v, int) else pl.multiple_of(v, k)


def _for_chunks(n, unroll, body):
    if unroll >= n:
        for c in range(n):
            body(c)
        return

    def step(i, carry):
        for u in range(unroll):
            body(i * unroll + u)
        return carry

    lax.fori_loop(0, n // unroll, step, 0)


def _mod_kernel(c_ref, w_ref, b_ref, o_ref):
    s = _silu(c_ref[...]).astype(_BF)
    for k in range(3):
        cols = slice(k * D_MODEL, (k + 1) * D_MODEL)
        bias = b_ref[pl.ds(pl.program_id(0), 1), cols]
        o_ref[k] = _dot(s, w_ref[:, cols].astype(_BF)) + bias


def _modulation(c_all, w_mod, b_mod):
    return pl.pallas_call(
        _mod_kernel,
        grid=(DEPTH,),
        in_specs=[
            pl.BlockSpec((MOD_ROWS, D_MODEL), lambda l: (0, 0)),
            pl.BlockSpec((None, D_MODEL, 3 * D_MODEL), lambda l: (l, 0, 0)),
            pl.BlockSpec((DEPTH, 3 * D_MODEL), lambda l: (0, 0)),
        ],
        out_specs=pl.BlockSpec((None, 3, MOD_ROWS, D_MODEL), lambda l: (l, 0, 0, 0)),
        out_shape=jax.ShapeDtypeStruct((DEPTH, 3, MOD_ROWS, D_MODEL), _F32),
        compiler_params=pltpu.CompilerParams(dimension_semantics=("arbitrary",),
                                             vmem_limit_bytes=VMEM_LIMIT_BYTES),
        name="modulation",
    )(c_all, w_mod, b_mod)


def _prep_kernel(wint_ref, wout_ref, wuq_ref, wukv_ref, poolw_ref,
                 win_o, wout_o, wuq_o, wukt_o, wuv_o, poolw_o):
    def put(dst, src, width):
        for c in range(0, width, PREP_COLS):
            n = min(PREP_COLS, width - c)
            win_o[:, dst + c:dst + c + n] = wint_ref[src + c:src + c + n, :].T.astype(_BF)

    put(_C_CQ, _OFFS[0], Q_LORA + KV_LORA)
    kr = wint_ref[_OFFS[2]:_OFFS[3], :]
    krb = jnp.concatenate([kr, -kr[AXIS_DIM:], kr[:AXIS_DIM]], axis=0)
    win_o[:, _C_KR:_C_GMLA] = krb.T.astype(_BF)
    put(_C_GMLA, _OFFS[3], MLA_WIDTH)
    put(_C_GPOOL, _OFFS[5], POOL_WIDTH)
    put(_C_PX, _OFFS[4], POOL_WIDTH)
    put(_C_CC, _OFFS[7], 2 * CONV_WIDTH)
    put(_C_CB, _OFFS[6], CONV_WIDTH)
    put(_C_CB + CONV_WIDTH, _OFFS[9], CONV_WIDTH)
    wout_o[...] = wout_ref[...].astype(_BF)

    for hd in range(MLA_HEADS):
        s0, d0 = hd * QK_DIM, hd * HEAD_PAD
        r0 = s0 + NOPE_DIM
        wuq_o[:, d0:d0 + QK_DIM] = wuq_ref[:, s0:s0 + QK_DIM].astype(_BF)
        wuq_o[:, d0 + QK_DIM:d0 + QK_DIM + AXIS_DIM] = (-wuq_ref[:, r0 + AXIS_DIM:r0 + ROPE_DIM]).astype(_BF)
        wuq_o[:, d0 + QK_DIM + AXIS_DIM:d0 + HEAD_PAD] = wuq_ref[:, r0:r0 + AXIS_DIM].astype(_BF)
    for hd in range(MLA_HEADS):
        k0 = hd * KV_HEAD
        wukt_o[hd * NOPE_DIM:(hd + 1) * NOPE_DIM, :] = wukv_ref[:, k0:k0 + NOPE_DIM].T.astype(_BF)
        wuv_o[:, hd * V_DIM:(hd + 1) * V_DIM] = wukv_ref[:, k0 + NOPE_DIM:k0 + KV_HEAD].astype(_BF)
    poolw_o[...] = jnp.zeros((POOL_WIDTH, POOL_WIDTH), _BF)
    for g in range(POOL_GROUPS):
        sl = slice(g * POOL_GROUP_DIM, (g + 1) * POOL_GROUP_DIM)
        poolw_o[sl, sl] = poolw_ref[g].astype(_BF)


def _prep_weights(w_in_t, w_out, w_uq, w_ukv, pool_w):
    per_layer = lambda *shape: pl.BlockSpec((None,) + shape, lambda l: (l,) + (0,) * len(shape))
    return pl.pallas_call(
        _prep_kernel,
        grid=(DEPTH,),
        in_specs=[
            per_layer(IN_WIDTH, D_MODEL),
            per_layer(MIX_WIDTH, D_MODEL),
            per_layer(Q_LORA, MLA_HEADS * QK_DIM),
            per_layer(KV_LORA, MLA_HEADS * KV_HEAD),
            per_layer(POOL_GROUPS, POOL_GROUP_DIM, POOL_GROUP_DIM),
        ],
        out_specs=[
            per_layer(D_MODEL, _C_END),
            per_layer(MIX_WIDTH, D_MODEL),
            per_layer(Q_LORA, MLA_HEADS * HEAD_PAD),
            per_layer(MLA_HEADS * NOPE_DIM, KV_LORA),
            per_layer(KV_LORA, MLA_WIDTH),
            per_layer(POOL_WIDTH, POOL_WIDTH),
        ],
        out_shape=[
            jax.ShapeDtypeStruct((DEPTH, D_MODEL, _C_END), _BF),
            jax.ShapeDtypeStruct((DEPTH, MIX_WIDTH, D_MODEL), _BF),
            jax.ShapeDtypeStruct((DEPTH, Q_LORA, MLA_HEADS * HEAD_PAD), _BF),
            jax.ShapeDtypeStruct((DEPTH, MLA_HEADS * NOPE_DIM, KV_LORA), _BF),
            jax.ShapeDtypeStruct((DEPTH, KV_LORA, MLA_WIDTH), _BF),
            jax.ShapeDtypeStruct((DEPTH, POOL_WIDTH, POOL_WIDTH), _BF),
        ],
        compiler_params=pltpu.CompilerParams(dimension_semantics=("arbitrary",),
                                             vmem_limit_bytes=VMEM_LIMIT_BYTES),
        name="weight_prep",
    )(w_in_t, w_out, w_uq, w_ukv, pool_w)


def _pool_mix(win, rcnt):
    n = win.shape[0]
    rows = n - 2 * HALO
    lane = lax.broadcasted_iota(jnp.int32, (1, POOL_WIDTH), 1)
    sums = [win + pltpu.roll(win, 1, axis=0)]
    for k in (1, 2, 4):
        sums.append(pltpu.roll(sums[-1], k, axis=0) + pltpu.roll(sums[-1], n - k, axis=0))
    sel = sums[-1]
    for g in range(POOL_GROUPS - 2, -1, -1):
        sel = jnp.where(lane < (g + 1) * POOL_GROUP_DIM, sums[g], sel)
    return sel[HALO:HALO + rows] * rcnt - win[HALO:HALO + rows]


def _short_conv(win, w):
    rows = win.shape[0] - 2 * HALO
    return (win[HALO - 1:HALO - 1 + rows] * w[0] + win[HALO:HALO + rows] * w[1]
            + win[HALO + 1:HALO + 1 + rows] * w[2])


def _pass_kernel(*refs, n_seq, seq_len, cache_len, has_rope, emit_state, unroll, rc):
    it = iter(refs)
    x_ref, mod_ref = next(it), next(it)
    if cache_len:
        cckv_all_ref, ckr_all_ref = next(it), next(it)
    if has_rope:
        cs_ref = next(it)
    (rcnt_ref, gn_ref, win_all_ref, gq_ref, wuq_all_ref, gkv_ref, wukt_all_ref, wuv_all_ref, poolw_all_ref,
     pools_ref, convw_ref, wout_all_ref, gfin_ref) = (next(it) for _ in range(13))
    y_ref = next(it)
    if emit_state:
        ckv_out_ref, kr_out_ref = next(it), next(it)
    h_s, q_s, kt_s, v_s, px_s, prod_s, mixed_s = (next(it) for _ in range(7))

    m = n_seq * seq_len
    keys = cache_len + seq_len
    padded = seq_len + 2 * HALO
    lane = lax.broadcasted_iota(jnp.int32, (1, ROPE_PAD), 1)
    mrow = (1 + pl.program_id(0)) if has_rope else 0

    for c in range(m // rc):
        y_ref[c * rc:(c + 1) * rc, :] = x_ref[c * rc:(c + 1) * rc, :]

    zeros_halo = jnp.zeros((HALO, POOL_WIDTH), _F32)
    for buf in (px_s, prod_s):
        for s in range(n_seq):
            buf[s * padded:s * padded + HALO, :] = zeros_halo
            buf[(s + 1) * padded - HALO:(s + 1) * padded, :] = zeros_halo
    ones_col = jnp.where(lax.broadcasted_iota(jnp.int32, (n_seq * keys, V_DIM), 1) == 0, 1.0, 0.0).astype(_BF)
    for hd in range(MLA_HEADS):
        v_s[hd, :, V_DIM:] = ones_col

    def layer_view(l):
        lw = dict(l=l, row=pl.ds(l, 1), win=win_all_ref.at[l], wuq=wuq_all_ref.at[l], wukt=wukt_all_ref.at[l],
                  wuv=wuv_all_ref.at[l], poolw=poolw_all_ref.at[l], wout=wout_all_ref.at[l])
        lw["shift"], lw["scale"], lw["gate"] = (mod_ref[l, k, pl.ds(mrow, 1), :] for k in range(3))
        if cache_len:
            lw["cckv"], lw["ckr"] = cckv_all_ref.at[l], ckr_all_ref.at[l]
        return lw

    def rotate(v, r):
        v = v * cs_ref[pl.ds(r, rc), :]
        return v + pltpu.roll(v, ROPE_DIM, axis=1)

    def put_kv(lw, ckv, kr_t, kc):
        knt = _dot(lw["wukt"][...], ckv.T.astype(_BF))
        v4 = _dot(ckv.astype(_BF), lw["wuv"][...])
        kr_bf = kr_t.astype(_BF)
        rows_k = pl.ds(_aligned(kc * rc, rc), rc)
        for hd in range(MLA_HEADS):
            kt_s[hd, kc, :NOPE_DIM, :] = knt[hd * NOPE_DIM:(hd + 1) * NOPE_DIM, :].astype(_BF)
            kt_s[hd, kc, NOPE_DIM:, :] = kr_bf
            v_s[hd, rows_k, :V_DIM] = v4[:, hd * V_DIM:(hd + 1) * V_DIM].astype(_BF)

    def cached_keys(lw):
        ckr_t = lw["ckr"][...]
        ckr_pad = jnp.concatenate([ckr_t, jnp.zeros_like(ckr_t)], axis=0)
        for cc in range(cache_len // rc):
            rows_k = slice(cc * rc, (cc + 1) * rc)
            put_kv(lw, lw["cckv"][rows_k, :], ckr_pad[:, rows_k], cc)

    piece = min(seq_len, rc)
    n_sub = rc // piece
    pieces = [(s, slice(s * piece, (s + 1) * piece)) for s in range(n_sub)]

    def halo_base(c, s):
        return (c * n_sub + s) * padded if seq_len <= rc else c * rc

    def project(lw, c):
        r = _aligned(c * rc, rc)
        rows = pl.ds(r, rc)
        row, win_ref = lw["row"], lw["win"]
        h = (_rms(y_ref[rows, :], gn_ref[row, :]) * (1.0 + lw["scale"]) + lw["shift"]).astype(_BF)
        h_s[rows, :] = h
        a = _dot(h, win_ref[:, _C_CQ:_C_GMLA])
        cq = _rms(a[:, :Q_LORA], gq_ref[row, :]).astype(_BF)
        ckv = _rms(a[:, _C_CKV:_C_KR], gkv_ref[row, :])
        kr = a[:, _C_KR:]
        if has_rope:
            kr = rotate(kr, r)
        kr_t = jnp.where(lane < ROPE_DIM, kr, 0.0).T
        if emit_state:
            for s, ps in pieces:
                ckv_out_ref[c * n_sub + s, lw["l"]] = ckv[ps, :]
                kr_out_ref[c * n_sub + s, lw["l"]] = kr_t[:ROPE_DIM, ps]
        put_kv(lw, ckv, kr_t, cache_len // rc + c)
        for hp in range(MLA_HEADS // 2):
            q2 = _dot(cq, lw["wuq"][:, hp * 2 * HEAD_PAD:(hp + 1) * 2 * HEAD_PAD]) * Q_SCALE
            for j in range(2):
                hd = 2 * hp + j
                c0 = j * HEAD_PAD
                q_s[hd, rows, :NOPE_DIM] = q2[:, c0:c0 + NOPE_DIM].astype(_BF)
                qr = q2[:, c0 + NOPE_DIM:c0 + HEAD_PAD]
                if has_rope:
                    qr = rotate(qr, r)
                q_s[hd, rows, NOPE_DIM:] = qr.astype(_BF)
        pe = _dot(h, win_ref[:, _C_PX:_C_CB])
        for s, ps in pieces:
            rows_h = pl.ds(halo_base(c, s) + HALO, piece)
            px_s[rows_h, :] = pe[ps, :POOL_WIDTH]
            prod_s[rows_h, :] = pe[ps, POOL_WIDTH:POOL_WIDTH + CONV_WIDTH] * pe[ps, POOL_WIDTH + CONV_WIDTH:]

    def mix(lw, c):
        r = _aligned(c * rc, rc)
        rows = pl.ds(r, rc)
        row, win_ref = lw["row"], lw["win"]
        h = h_s[rows, :]
        for hp in range(MLA_HEADS // 2):
            g2 = _silu(_dot(h, win_ref[:, _C_GMLA + hp * 2 * V_DIM:_C_GMLA + (hp + 1) * 2 * V_DIM]))
            for j in range(2):
                hd = 2 * hp + j
                for s, ps in pieces:
                    rows_q = pl.ds(r + s * piece, piece)
                    if n_seq == 1:
                        ks = [kt_s[hd, kc] for kc in range(keys // rc)]
                        rows_k = pl.ds(0, keys)
                    else:
                        ks = [kt_s[hd, c, :, ps]]
                        rows_k = rows_q
                    q = q_s[hd, rows_q, :]
                    sc = jnp.concatenate([_dot(q, k) for k in ks], axis=-1)
                    p = jnp.exp2(sc - jnp.max(sc, axis=-1, keepdims=True))
                    ov = _dot(p.astype(_BF), v_s[hd, rows_k, :])
                    o = ov[:, :V_DIM] / ov[:, V_DIM:V_DIM + 1]
                    mixed_s[rows_q, hd * V_DIM:(hd + 1) * V_DIM] = (g2[ps, j * V_DIM:(j + 1) * V_DIM] * o).astype(_BF)

        windows = [pl.ds(halo_base(c, s), piece + 2 * HALO) for s, _ in pieces]
        rcnt = rcnt_ref[pl.ds(0 if seq_len <= rc else r, piece), :]
        pooled = jnp.concatenate([_pool_mix(px_s[w, :], rcnt) for w in windows], axis=0)
        pool = _dot(pooled.astype(_BF), lw["poolw"][...]) * pools_ref[row, :]
        gp = _silu(_dot(h, win_ref[:, _C_GPOOL:_C_PX]))
        mixed_s[rows, MLA_WIDTH:MLA_WIDTH + POOL_WIDTH] = (gp * pool).astype(_BF)

        convw = [convw_ref[k, row, :] for k in range(3)]
        conv = jnp.concatenate([_short_conv(prod_s[w, :], convw) for w in windows], axis=0)
        e2 = _dot(h, win_ref[:, _C_CB:_C_END])
        mixed_s[rows, MLA_WIDTH + POOL_WIDTH:] = (_silu(e2[:, CONV_WIDTH:]) * (e2[:, :CONV_WIDTH] * conv)).astype(_BF)

        y_ref[rows, :] = y_ref[rows, :] + lw["gate"] * _dot(mixed_s[rows, :], lw["wout"][...])

    def one_layer(l, carry):
        lw = layer_view(l)
        if cache_len:
            cached_keys(lw)
        _for_chunks(m // rc, unroll, functools.partial(project, lw))
        _for_chunks(m // rc, unroll, functools.partial(mix, lw))
        return carry

    lax.fori_loop(0, DEPTH, one_layer, 0)
    for c in range(m // rc):
        rows = slice(c * rc, (c + 1) * rc)
        y_ref[rows, :] = _rms(y_ref[rows, :], gfin_ref[...])


def _mixer_pass(x2d, mod_all, wts, *, seq_len, cache=None, rope=None, emit_state):
    rows = x2d.shape[0]
    m = ROWS_PER_STEP
    n_seq = m // seq_len
    n_steps = rows // m
    n_all = rows // seq_len
    cache_len = 0 if cache is None else cache[0].shape[2]
    keys = cache_len + seq_len
    has_rope = rope is not None
    rc = ROW_CHUNK
    assert m % rc == 0 and cache_len % rc == 0
    assert (rc % seq_len == 0 and not cache_len) or (n_seq == 1 and seq_len % rc == 0 and not emit_state)

    def const(*shape):
        return pl.BlockSpec(shape, lambda i: (0,) * len(shape), pipeline_mode=pl.Buffered(1))

    args = [x2d, mod_all]
    in_specs = [pl.BlockSpec((m, D_MODEL), lambda i: (i, 0)), const(DEPTH, 3, MOD_ROWS, D_MODEL)]
    if cache_len:
        args += [cache[0], cache[1]]
        in_specs += [pl.BlockSpec((None, DEPTH, cache_len, KV_LORA), lambda i: (i, 0, 0, 0)),
                     pl.BlockSpec((None, DEPTH, ROPE_DIM, cache_len), lambda i: (i, 0, 0, 0))]
    if has_rope:
        args += [rope]
        in_specs += [const(seq_len, ROPE_PAD)]
    args += [_pool_rcnt(seq_len), wts["g_norm"], wts["w_in"], wts["g_q"], wts["w_uq"], wts["g_kv"],
             wts["w_ukt"], wts["w_uv"], wts["pool_w"], wts["pool_s"], wts["conv_w"], wts["w_out"],
             wts["g_final"]]
    in_specs += [const(seq_len, POOL_WIDTH), const(DEPTH, D_MODEL), const(DEPTH, D_MODEL, _C_END),
                 const(DEPTH, Q_LORA), const(DEPTH, Q_LORA, MLA_HEADS * HEAD_PAD), const(DEPTH, KV_LORA),
                 const(DEPTH, MLA_HEADS * NOPE_DIM, KV_LORA), const(DEPTH, KV_LORA, MLA_WIDTH),
                 const(DEPTH, POOL_WIDTH, POOL_WIDTH), const(DEPTH, POOL_WIDTH), const(3, DEPTH, CONV_WIDTH),
                 const(DEPTH, MIX_WIDTH, D_MODEL), const(1, D_MODEL)]

    out_shape = [jax.ShapeDtypeStruct((rows, D_MODEL), _F32)]
    out_specs = [pl.BlockSpec((m, D_MODEL), lambda i: (i, 0))]
    if emit_state:
        out_shape += [jax.ShapeDtypeStruct((n_all, DEPTH, seq_len, KV_LORA), _F32),
                      jax.ShapeDtypeStruct((n_all, DEPTH, ROPE_DIM, seq_len), _F32)]
        out_specs += [pl.BlockSpec((n_seq, DEPTH, seq_len, KV_LORA), lambda i: (i, 0, 0, 0)),
                      pl.BlockSpec((n_seq, DEPTH, ROPE_DIM, seq_len), lambda i: (i, 0, 0, 0))]

    kern = functools.partial(_pass_kernel, n_seq=n_seq, seq_len=seq_len, cache_len=cache_len,
                             has_rope=has_rope, emit_state=emit_state, unroll=CHUNK_UNROLL, rc=rc)
    return pl.pallas_call(
        kern,
        grid=(n_steps,),
        in_specs=in_specs,
        out_specs=out_specs,
        out_shape=out_shape,
        scratch_shapes=[
            pltpu.VMEM((m, D_MODEL), _BF),
            pltpu.VMEM((MLA_HEADS, m, HEAD_PAD), _BF),
            pltpu.VMEM((MLA_HEADS, n_seq * keys // rc, HEAD_PAD, rc), _BF),
            pltpu.VMEM((MLA_HEADS, n_seq * keys, 2 * V_DIM), _BF),
            pltpu.VMEM((n_seq * (seq_len + 2 * HALO), POOL_WIDTH), _F32),
            pltpu.VMEM((n_seq * (seq_len + 2 * HALO), CONV_WIDTH), _F32),
            pltpu.VMEM((m, MIX_WIDTH), _BF),
        ],
        compiler_params=pltpu.CompilerParams(dimension_semantics=("arbitrary",),
                                             vmem_limit_bytes=VMEM_LIMIT_BYTES),
        name="mixer_pass_latent" if has_rope else "mixer_pass_context",
    )(*args)


def _pool_rcnt(seq_len):
    t = np.arange(seq_len)[:, None]
    half = np.repeat(1 << np.arange(POOL_GROUPS), POOL_GROUP_DIM)[None, :]
    cnt = np.minimum(t + half, seq_len) - np.maximum(t - half, 0)
    return jnp.asarray((1.0 / cnt).astype(np.float32))


def _rope_table(seq_len):
    rows = seq_len // GRID_W
    row = np.repeat(np.arange(rows), GRID_W).astype(np.float32)
    col = np.tile(np.arange(GRID_W), rows).astype(np.float32)
    inv = (1.0 / (np.float32(ROPE_BASE) ** (np.arange(0, AXIS_DIM, 2, dtype=np.float32) / np.float32(AXIS_DIM))))
    inv = inv.astype(np.float32)
    ang = np.concatenate([row[:, None] * inv, col[:, None] * inv], axis=-1).astype(np.float64)
    cos, sin = np.cos(ang).astype(np.float32), np.sin(ang).astype(np.float32)
    return jnp.asarray(np.concatenate([cos, cos, sin, sin], axis=-1))


def kernel(x_prompt, x_sample, cache_ckv, cache_krope, c, c_ctx, w_mod, b_mod, g_norm, w_in, g_q, w_uq,
           g_kv, w_ukv, pool_w, pool_s, conv_w, w_out, g_final):
    batch, seq, _ = x_prompt.shape
    dec_batch, dec_seq, _ = x_sample.shape
    assert 1 + dec_batch <= MOD_ROWS and dec_seq == ROWS_PER_STEP and ROWS_PER_STEP % seq == 0 and DEPTH == 2

    c_all = jnp.concatenate([c_ctx[None], c, jnp.zeros((MOD_ROWS - 1 - dec_batch, D_MODEL), _F32)], axis=0)
    mod_all = _modulation(c_all, w_mod, b_mod)
    w_in_r, w_out_r, w_uq_r, w_ukt, w_uv, pool_w_r = _prep_weights(
        jnp.transpose(w_in, (0, 2, 1)), w_out, w_uq, w_ukv, pool_w)
    wts = {
        "g_norm": g_norm, "w_in": w_in_r, "g_q": g_q, "w_uq": w_uq_r, "g_kv": g_kv, "w_ukt": w_ukt,
        "w_uv": w_uv, "pool_w": pool_w_r, "pool_s": pool_s, "conv_w": jnp.transpose(conv_w, (1, 0, 2)), "w_out": w_out_r,
        "g_final": g_final.reshape(1, D_MODEL),
    }

    rope = _rope_table(dec_seq)
    cache = (cache_ckv, jnp.transpose(cache_krope, (0, 1, 3, 2)))
    xp = x_prompt.reshape(batch * seq, D_MODEL)
    xs = x_sample.reshape(dec_batch * dec_seq, D_MODEL)
    xp, state_ckv, state_krope_t = _mixer_pass(xp, mod_all, wts, seq_len=seq, emit_state=True)
    (xs,) = _mixer_pass(xs, mod_all, wts, seq_len=dec_seq, cache=cache, rope=rope, emit_state=False)
    y_prompt = xp.reshape(batch, seq, D_MODEL)
    state_krope = jnp.transpose(state_krope_t, (0, 1, 3, 2))
    y_sample = xs.reshape(dec_batch, dec_seq, D_MODEL)
    return (y_prompt, y_sample, state_ckv, state_krope)
```

```python
import functools

import numpy as np
import jax
import jax.numpy as jnp
from jax import lax
from jax.experimental import pallas as pl
from jax.experimental.pallas import tpu as pltpu

D_MODEL = 1024
DEPTH = 2
GRID_W = 64
MLA_HEADS = 4
NOPE_DIM = 128
ROPE_DIM = 64
V_DIM = 128
QK_DIM = NOPE_DIM + ROPE_DIM
MLA_WIDTH = MLA_HEADS * V_DIM
Q_LORA = 384
KV_LORA = 256
POOL_GROUPS = 4
POOL_GROUP_DIM = 64
POOL_WIDTH = POOL_GROUPS * POOL_GROUP_DIM
CONV_WIDTH = 256
MIX_WIDTH = MLA_WIDTH + POOL_WIDTH + CONV_WIDTH
ROPE_BASE = 10000.0
AXIS_DIM = ROPE_DIM // 2
ATTN_SCALE = QK_DIM ** -0.5
Q_SCALE = ATTN_SCALE * float(np.log2(np.e))
EPS = 1e-6

_SPLITS = (Q_LORA, KV_LORA, ROPE_DIM, MLA_WIDTH, POOL_WIDTH, POOL_WIDTH,
           CONV_WIDTH, CONV_WIDTH, CONV_WIDTH, CONV_WIDTH)
_OFFS = [sum(_SPLITS[:i]) for i in range(len(_SPLITS) + 1)]
IN_WIDTH = _OFFS[-1]

V7X_LANES = 128
V7X_SUBLANES = 8
V7X_MXU_DEPTH = 256
HEAD_PAD = V7X_MXU_DEPTH
ROPE_PAD = HEAD_PAD - NOPE_DIM
KV_HEAD = NOPE_DIM + V_DIM
V7X_BF16_SUBLANES = 16
V_ROWS = V_DIM + V7X_BF16_SUBLANES

_C_CQ = 0
_C_CKV = _C_CQ + Q_LORA
_C_KR = _C_CKV + KV_LORA
_C_GMLA = _C_KR + ROPE_PAD
_C_GPOOL = _C_GMLA + MLA_WIDTH
_C_PX = _C_GPOOL + POOL_WIDTH
_C_CC = _C_PX + POOL_WIDTH
_C_CB = _C_CC + 2 * CONV_WIDTH
_C_END = _C_CB + 2 * CONV_WIDTH

MOD_ROWS = 8
ROWS_PER_STEP = 1024
ROW_CHUNK = 512
CHUNK_UNROLL = 4
HALO = V7X_SUBLANES
PREP_COLS = 256
VMEM_LIMIT_BYTES = 60 * 1024 * 1024

_BF = jnp.bfloat16
_F32 = jnp.float32


def _dot(a, b):
    return jnp.dot(a, b, preferred_element_type=_F32)


def _rms(x, g):
    return x * lax.rsqrt(jnp.mean(x * x, axis=-1, keepdims=True) + EPS) * g


def _silu(x):
    return x * jax.nn.sigmoid(x)


def _aligned(v, k):
    return v if isinstance(v, int) else pl.multiple_of(v, k)


def _for_chunks(n, unroll, body):
    if unroll >= n:
        for c in range(n):
            body(c)
        return

    def step(i, carry):
        for u in range(unroll):
            body(i * unroll + u)
        return carry

    lax.fori_loop(0, n // unroll, step, 0)


def _mod_kernel(c_ref, w_ref, b_ref, o_ref):
    s = _silu(c_ref[...]).astype(_BF)
    for k in range(3):
        cols = slice(k * D_MODEL, (k + 1) * D_MODEL)
        bias = b_ref[pl.ds(pl.program_id(0), 1), cols]
        o_ref[k] = _dot(s, w_ref[:, cols].astype(_BF)) + bias


def _modulation(c_all, w_mod, b_mod):
    return pl.pallas_call(
        _mod_kernel,
        grid=(DEPTH,),
        in_specs=[
            pl.BlockSpec((MOD_ROWS, D_MODEL), lambda l: (0, 0)),
            pl.BlockSpec((None, D_MODEL, 3 * D_MODEL), lambda l: (l, 0, 0)),
            pl.BlockSpec((DEPTH, 3 * D_MODEL), lambda l: (0, 0)),
        ],
        out_specs=pl.BlockSpec((None, 3, MOD_ROWS, D_MODEL), lambda l: (l, 0, 0, 0)),
        out_shape=jax.ShapeDtypeStruct((DEPTH, 3, MOD_ROWS, D_MODEL), _F32),
        compiler_params=pltpu.CompilerParams(dimension_semantics=("arbitrary",),
                                             vmem_limit_bytes=VMEM_LIMIT_BYTES),
        name="modulation",
    )(c_all, w_mod, b_mod)


def _prep_kernel(wint_ref, wout_ref, wuq_ref, wukv_ref, poolw_ref,
                 win_o, wout_o, wuqt_o, wuk_o, wuvt_o, poolw_o, wuq_s):
    def put(dst, src, width):
        for c in range(0, width, PREP_COLS):
            n = min(PREP_COLS, width - c)
            win_o[:, dst + c:dst + c + n] = wint_ref[src + c:src + c + n, :].T.astype(_BF)

    put(_C_CQ, _OFFS[0], Q_LORA + KV_LORA)
    kr = wint_ref[_OFFS[2]:_OFFS[3], :]
    krb = jnp.concatenate([kr, -kr[AXIS_DIM:], kr[:AXIS_DIM]], axis=0)
    win_o[:, _C_KR:_C_GMLA] = krb.T.astype(_BF)
    put(_C_GMLA, _OFFS[3], MLA_WIDTH)
    put(_C_GPOOL, _OFFS[5], POOL_WIDTH)
    put(_C_PX, _OFFS[4], POOL_WIDTH)
    put(_C_CC, _OFFS[7], 2 * CONV_WIDTH)
    put(_C_CB, _OFFS[6], CONV_WIDTH)
    put(_C_CB + CONV_WIDTH, _OFFS[9], CONV_WIDTH)
    wout_o[...] = wout_ref[...].astype(_BF)

    for hd in range(MLA_HEADS):
        s0, d0 = hd * QK_DIM, hd * HEAD_PAD
        r0 = s0 + NOPE_DIM
        wuq_s[:, d0:d0 + QK_DIM] = wuq_ref[:, s0:s0 + QK_DIM]
        wuq_s[:, d0 + QK_DIM:d0 + QK_DIM + AXIS_DIM] = -wuq_ref[:, r0 + AXIS_DIM:r0 + ROPE_DIM]
        wuq_s[:, d0 + QK_DIM + AXIS_DIM:d0 + HEAD_PAD] = wuq_ref[:, r0:r0 + AXIS_DIM]
        wuqt_o[d0:d0 + HEAD_PAD, :] = wuq_s[:, d0:d0 + HEAD_PAD].T.astype(_BF)
    for hd in range(MLA_HEADS):
        k0 = hd * KV_HEAD
        wuk_o[:, hd * NOPE_DIM:(hd + 1) * NOPE_DIM] = wukv_ref[:, k0:k0 + NOPE_DIM].astype(_BF)
        wuvt_o[hd * V_DIM:(hd + 1) * V_DIM, :] = wukv_ref[:, k0 + NOPE_DIM:k0 + KV_HEAD].T.astype(_BF)
    poolw_o[...] = jnp.zeros((POOL_WIDTH, POOL_WIDTH), _BF)
    for g in range(POOL_GROUPS):
        sl = slice(g * POOL_GROUP_DIM, (g + 1) * POOL_GROUP_DIM)
        poolw_o[sl, sl] = poolw_ref[g].astype(_BF)


def _prep_weights(w_in_t, w_out, w_uq, w_ukv, pool_w):
    per_layer = lambda *shape: pl.BlockSpec((None,) + shape, lambda l: (l,) + (0,) * len(shape))
    return pl.pallas_call(
        _prep_kernel,
        grid=(DEPTH,),
        in_specs=[
            per_layer(IN_WIDTH, D_MODEL),
            per_layer(MIX_WIDTH, D_MODEL),
            per_layer(Q_LORA, MLA_HEADS * QK_DIM),
            per_layer(KV_LORA, MLA_HEADS * KV_HEAD),
            per_layer(POOL_GROUPS, POOL_GROUP_DIM, POOL_GROUP_DIM),
        ],
        out_specs=[
            per_layer(D_MODEL, _C_END),
            per_layer(MIX_WIDTH, D_MODEL),
            per_layer(MLA_HEADS * HEAD_PAD, Q_LORA),
            per_layer(KV_LORA, MLA_HEADS * NOPE_DIM),
            per_layer(MLA_WIDTH, KV_LORA),
            per_layer(POOL_WIDTH, POOL_WIDTH),
        ],
        out_shape=[
            jax.ShapeDtypeStruct((DEPTH, D_MODEL, _C_END), _BF),
            jax.ShapeDtypeStruct((DEPTH, MIX_WIDTH, D_MODEL), _BF),
            jax.ShapeDtypeStruct((DEPTH, MLA_HEADS * HEAD_PAD, Q_LORA), _BF),
            jax.ShapeDtypeStruct((DEPTH, KV_LORA, MLA_HEADS * NOPE_DIM), _BF),
            jax.ShapeDtypeStruct((DEPTH, MLA_WIDTH, KV_LORA), _BF),
            jax.ShapeDtypeStruct((DEPTH, POOL_WIDTH, POOL_WIDTH), _BF),
        ],
        scratch_shapes=[pltpu.VMEM((Q_LORA, MLA_HEADS * HEAD_PAD), _F32)],
        compiler_params=pltpu.CompilerParams(dimension_semantics=("arbitrary",),
                                             vmem_limit_bytes=VMEM_LIMIT_BYTES),
        name="weight_prep",
    )(w_in_t, w_out, w_uq, w_ukv, pool_w)


def _pool_mix(win, rcnt):
    n = win.shape[0]
    rows = n - 2 * HALO
    lane = lax.broadcasted_iota(jnp.int32, (1, POOL_WIDTH), 1)
    sums = [win + pltpu.roll(win, 1, axis=0)]
    for k in (1, 2, 4):
        sums.append(pltpu.roll(sums[-1], k, axis=0) + pltpu.roll(sums[-1], n - k, axis=0))
    sel = sums[-1]
    for g in range(POOL_GROUPS - 2, -1, -1):
        sel = jnp.where(lane < (g + 1) * POOL_GROUP_DIM, sums[g], sel)
    return sel[HALO:HALO + rows] * rcnt - win[HALO:HALO + rows]


def _short_conv(win, w):
    rows = win.shape[0] - 2 * HALO
    return (win[HALO - 1:HALO - 1 + rows] * w[0] + win[HALO:HALO + rows] * w[1]
            + win[HALO + 1:HALO + 1 + rows] * w[2])


def _pass_kernel(*refs, n_seq, seq_len, cache_len, has_rope, emit_state, unroll, rc):
    it = iter(refs)
    x_ref, mod_ref = next(it), next(it)
    if cache_len:
        cckv_all_ref, ckr_all_ref = next(it), next(it)
    if has_rope:
        cs_ref, cst_ref = next(it), next(it)
    (rcnt_ref, gn_ref, win_all_ref, gq_ref, wuqt_all_ref, gkv_ref, wuk_all_ref, wuvt_all_ref, poolw_all_ref,
     pools_ref, convw_ref, wout_all_ref, gfin_ref) = (next(it) for _ in range(13))
    y_ref = next(it)
    if emit_state:
        ckv_out_ref, kr_out_ref = next(it), next(it)
    h_s, qt_s, k_s, vt_s, px_s, prod_s, mixed_s = (next(it) for _ in range(7))

    m = n_seq * seq_len
    keys = cache_len + seq_len
    padded = seq_len + 2 * HALO
    lane = lax.broadcasted_iota(jnp.int32, (1, ROPE_PAD), 1)
    mrow = (1 + pl.program_id(0)) if has_rope else 0

    for c in range(m // rc):
        y_ref[c * rc:(c + 1) * rc, :] = x_ref[c * rc:(c + 1) * rc, :]

    zeros_halo = jnp.zeros((HALO, POOL_WIDTH), _F32)
    for buf in (px_s, prod_s):
        for s in range(n_seq):
            buf[s * padded:s * padded + HALO, :] = zeros_halo
            buf[(s + 1) * padded - HALO:(s + 1) * padded, :] = zeros_halo
    ones_row = jnp.where(lax.broadcasted_iota(jnp.int32, (V_ROWS - V_DIM, n_seq * keys), 0) == 0, 1.0, 0.0)
    for hd in range(MLA_HEADS):
        vt_s[hd, V_DIM:, :] = ones_row.astype(_BF)

    def layer_view(l):
        lw = dict(l=l, row=pl.ds(l, 1), win=win_all_ref.at[l], wuqt=wuqt_all_ref.at[l], wuk=wuk_all_ref.at[l],
                  wuvt=wuvt_all_ref.at[l], poolw=poolw_all_ref.at[l], wout=wout_all_ref.at[l])
        lw["shift"], lw["scale"], lw["gate"] = (mod_ref[l, k, pl.ds(mrow, 1), :] for k in range(3))
        if cache_len:
            lw["cckv"], lw["ckr"] = cckv_all_ref.at[l], ckr_all_ref.at[l]
        return lw

    def rotate(v, r):
        v = v * cs_ref[pl.ds(r, rc), :]
        return v + pltpu.roll(v, ROPE_DIM, axis=1)

    def put_kv(lw, ckv, kr, kc):
        kn = _dot(ckv.astype(_BF), lw["wuk"][...])
        vt = _dot(lw["wuvt"][...], ckv.T.astype(_BF))
        kr_bf = kr.astype(_BF)
        rows_k = pl.ds(kc * rc, rc)
        for hd in range(MLA_HEADS):
            k_s[hd, rows_k, :NOPE_DIM] = kn[:, hd * NOPE_DIM:(hd + 1) * NOPE_DIM].astype(_BF)
            k_s[hd, rows_k, NOPE_DIM:] = kr_bf
            vt_s[hd, :V_DIM, rows_k] = vt[hd * V_DIM:(hd + 1) * V_DIM, :].astype(_BF)

    def cached_keys(lw):
        ckr_t = lw["ckr"][...]
        ckr_pad = jnp.concatenate([ckr_t, jnp.zeros_like(ckr_t)], axis=0).T
        for cc in range(cache_len // rc):
            rows_k = slice(cc * rc, (cc + 1) * rc)
            put_kv(lw, lw["cckv"][rows_k, :], ckr_pad[rows_k, :], cc)

    piece = min(seq_len, rc)
    n_sub = rc // piece
    pieces = [(s, slice(s * piece, (s + 1) * piece)) for s in range(n_sub)]

    def halo_base(c, s):
        return (c * n_sub + s) * padded if seq_len <= rc else c * rc

    def project(lw, c):
        r = _aligned(c * rc, rc)
        rows = pl.ds(r, rc)
        row, win_ref = lw["row"], lw["win"]
        h = (_rms(y_ref[rows, :], gn_ref[row, :]) * (1.0 + lw["scale"]) + lw["shift"]).astype(_BF)
        h_s[rows, :] = h
        a = _dot(h, win_ref[:, _C_CQ:_C_GMLA])
        cq_t = _rms(a[:, :Q_LORA], gq_ref[row, :]).T.astype(_BF)
        ckv = _rms(a[:, _C_CKV:_C_KR], gkv_ref[row, :])
        kr = a[:, _C_KR:]
        if has_rope:
            kr = rotate(kr, r)
        kr = jnp.where(lane < ROPE_DIM, kr, 0.0)
        if emit_state:
            kr_t = kr.T
            for s, ps in pieces:
                ckv_out_ref[c * n_sub + s, lw["l"]] = ckv[ps, :]
                kr_out_ref[c * n_sub + s, lw["l"]] = kr_t[:ROPE_DIM, ps]
        put_kv(lw, ckv, kr, cache_len // rc + c)
        for hp in range(MLA_HEADS // 2):
            qt2 = _dot(lw["wuqt"][hp * 2 * HEAD_PAD:(hp + 1) * 2 * HEAD_PAD, :], cq_t) * Q_SCALE
            for j in range(2):
                hd = 2 * hp + j
                r0 = j * HEAD_PAD
                qt_s[hd, c, :NOPE_DIM, :] = qt2[r0:r0 + NOPE_DIM, :].astype(_BF)
                qr = qt2[r0 + NOPE_DIM:r0 + HEAD_PAD, :]
                if has_rope:
                    qr = qr * cst_ref[:, pl.ds(r, rc)]
                    qr = qr + pltpu.roll(qr, ROPE_DIM, axis=0)
                qt_s[hd, c, NOPE_DIM:, :] = qr.astype(_BF)
        pe = _dot(h, win_ref[:, _C_PX:_C_CB])
        for s, ps in pieces:
            rows_h = pl.ds(halo_base(c, s) + HALO, piece)
            px_s[rows_h, :] = pe[ps, :POOL_WIDTH]
            prod_s[rows_h, :] = pe[ps, POOL_WIDTH:POOL_WIDTH + CONV_WIDTH] * pe[ps, POOL_WIDTH + CONV_WIDTH:]

    def mix(lw, c):
        r = _aligned(c * rc, rc)
        rows = pl.ds(r, rc)
        row, win_ref = lw["row"], lw["win"]
        h = h_s[rows, :]
        for hp in range(MLA_HEADS // 2):
            g2 = _silu(_dot(h, win_ref[:, _C_GMLA + hp * 2 * V_DIM:_C_GMLA + (hp + 1) * 2 * V_DIM]))
            for j in range(2):
                hd = 2 * hp + j
                for s, ps in pieces:
                    rows_q = pl.ds(r + s * piece, piece)
                    rows_k = pl.ds(0, keys) if n_seq == 1 else rows_q
                    st = _dot(k_s[hd, rows_k, :], qt_s[hd, c, :, ps])
                    p = jnp.exp2(st - jnp.max(st, axis=0, keepdims=True))
                    ovt = _dot(vt_s[hd, :, rows_k], p.astype(_BF))
                    o = (ovt[:V_DIM, :] / ovt[V_DIM:V_DIM + 1, :]).T
                    mixed_s[rows_q, hd * V_DIM:(hd + 1) * V_DIM] = (g2[ps, j * V_DIM:(j + 1) * V_DIM] * o).astype(_BF)

        windows = [pl.ds(halo_base(c, s), piece + 2 * HALO) for s, _ in pieces]
        rcnt = rcnt_ref[pl.ds(0 if seq_len <= rc else r, piece), :]
        pooled = jnp.concatenate([_pool_mix(px_s[w, :], rcnt) for w in windows], axis=0)
        pool = _dot(pooled.astype(_BF), lw["poolw"][...]) * pools_ref[row, :]
        gp = _silu(_dot(h, win_ref[:, _C_GPOOL:_C_PX]))
        mixed_s[rows, MLA_WIDTH:MLA_WIDTH + POOL_WIDTH] = (gp * pool).astype(_BF)

        convw = [convw_ref[k, row, :] for k in range(3)]
        conv = jnp.concatenate([_short_conv(prod_s[w, :], convw) for w in windows], axis=0)
        e2 = _dot(h, win_ref[:, _C_CB:_C_END])
        mixed_s[rows, MLA_WIDTH + POOL_WIDTH:] = (_silu(e2[:, CONV_WIDTH:]) * (e2[:, :CONV_WIDTH] * conv)).astype(_BF)

        y_ref[rows, :] = y_ref[rows, :] + lw["gate"] * _dot(mixed_s[rows, :], lw["wout"][...])

    def one_layer(l, carry):
        lw = layer_view(l)
        if cache_len:
            cached_keys(lw)
        _for_chunks(m // rc, unroll, functools.partial(project, lw))
        _for_chunks(m // rc, unroll, functools.partial(mix, lw))
        return carry

    lax.fori_loop(0, DEPTH, one_layer, 0)
    for c in range(m // rc):
        rows = slice(c * rc, (c + 1) * rc)
        y_ref[rows, :] = _rms(y_ref[rows, :], gfin_ref[...])


def _mixer_pass(x2d, mod_all, wts, *, seq_len, cache=None, rope=None, emit_state):
    rows = x2d.shape[0]
    m = ROWS_PER_STEP
    n_seq = m // seq_len
    n_steps = rows // m
    n_all = rows // seq_len
    cache_len = 0 if cache is None else cache[0].shape[2]
    keys = cache_len + seq_len
    has_rope = rope is not None
    rc = ROW_CHUNK
    assert m % rc == 0 and cache_len % rc == 0
    assert (rc % seq_len == 0 and not cache_len) or (n_seq == 1 and seq_len % rc == 0 and not emit_state)

    def const(*shape):
        return pl.BlockSpec(shape, lambda i: (0,) * len(shape), pipeline_mode=pl.Buffered(1))

    args = [x2d, mod_all]
    in_specs = [pl.BlockSpec((m, D_MODEL), lambda i: (i, 0)), const(DEPTH, 3, MOD_ROWS, D_MODEL)]
    if cache_len:
        args += [cache[0], cache[1]]
        in_specs += [pl.BlockSpec((None, DEPTH, cache_len, KV_LORA), lambda i: (i, 0, 0, 0)),
                     pl.BlockSpec((None, DEPTH, ROPE_DIM, cache_len), lambda i: (i, 0, 0, 0))]
    if has_rope:
        args += [rope, rope.T]
        in_specs += [const(seq_len, ROPE_PAD), const(ROPE_PAD, seq_len)]
    args += [_pool_rcnt(seq_len), wts["g_norm"], wts["w_in"], wts["g_q"], wts["w_uqt"], wts["g_kv"],
             wts["w_uk"], wts["w_uvt"], wts["pool_w"], wts["pool_s"], wts["conv_w"], wts["w_out"],
             wts["g_final"]]
    in_specs += [const(seq_len, POOL_WIDTH), const(DEPTH, D_MODEL), const(DEPTH, D_MODEL, _C_END),
                 const(DEPTH, Q_LORA), const(DEPTH, MLA_HEADS * HEAD_PAD, Q_LORA), const(DEPTH, KV_LORA),
                 const(DEPTH, KV_LORA, MLA_HEADS * NOPE_DIM), const(DEPTH, MLA_WIDTH, KV_LORA),
                 const(DEPTH, POOL_WIDTH, POOL_WIDTH), const(DEPTH, POOL_WIDTH), const(3, DEPTH, CONV_WIDTH),
                 const(DEPTH, MIX_WIDTH, D_MODEL), const(1, D_MODEL)]

    out_shape = [jax.ShapeDtypeStruct((rows, D_MODEL), _F32)]
    out_specs = [pl.BlockSpec((m, D_MODEL), lambda i: (i, 0))]
    if emit_state:
        out_shape += [jax.ShapeDtypeStruct((n_all, DEPTH, seq_len, KV_LORA), _F32),
                      jax.ShapeDtypeStruct((n_all, DEPTH, ROPE_DIM, seq_len), _F32)]
        out_specs += [pl.BlockSpec((n_seq, DEPTH, seq_len, KV_LORA), lambda i: (i, 0, 0, 0)),
                      pl.BlockSpec((n_seq, DEPTH, ROPE_DIM, seq_len), lambda i: (i, 0, 0, 0))]

    kern = functools.partial(_pass_kernel, n_seq=n_seq, seq_len=seq_len, cache_len=cache_len,
                             has_rope=has_rope, emit_state=emit_state, unroll=CHUNK_UNROLL, rc=rc)
    return pl.pallas_call(
        kern,
        grid=(n_steps,),
        in_specs=in_specs,
        out_specs=out_specs,
        out_shape=out_shape,
        scratch_shapes=[
            pltpu.VMEM((m, D_MODEL), _BF),
            pltpu.VMEM((MLA_HEADS, m // rc, HEAD_PAD, rc), _BF),
            pltpu.VMEM((MLA_HEADS, n_seq * keys, HEAD_PAD), _BF),
            pltpu.VMEM((MLA_HEADS, V_ROWS, n_seq * keys), _BF),
            pltpu.VMEM((n_seq * (seq_len + 2 * HALO), POOL_WIDTH), _F32),
            pltpu.VMEM((n_seq * (seq_len + 2 * HALO), CONV_WIDTH), _F32),
            pltpu.VMEM((m, MIX_WIDTH), _BF),
        ],
        compiler_params=pltpu.CompilerParams(dimension_semantics=("arbitrary",),
                                             vmem_limit_bytes=VMEM_LIMIT_BYTES),
        name="mixer_pass_latent" if has_rope else "mixer_pass_context",
    )(*args)


def _pool_rcnt(seq_len):
    t = np.arange(seq_len)[:, None]
    half = np.repeat(1 << np.arange(POOL_GROUPS), POOL_GROUP_DIM)[None, :]
    cnt = np.minimum(t + half, seq_len) - np.maximum(t - half, 0)
    return jnp.asarray((1.0 / cnt).astype(np.float32))


def _rope_table(seq_len):
    rows = seq_len // GRID_W
    row = np.repeat(np.arange(rows), GRID_W).astype(np.float32)
    col = np.tile(np.arange(GRID_W), rows).astype(np.float32)
    inv = (1.0 / (np.float32(ROPE_BASE) ** (np.arange(0, AXIS_DIM, 2, dtype=np.float32) / np.float32(AXIS_DIM))))
    inv = inv.astype(np.float32)
    ang = np.concatenate([row[:, None] * inv, col[:, None] * inv], axis=-1).astype(np.float64)
    cos, sin = np.cos(ang).astype(np.float32), np.sin(ang).astype(np.float32)
    return jnp.asarray(np.concatenate([cos, cos, sin, sin], axis=-1))


def kernel(x_prompt, x_sample, cache_ckv, cache_krope, c, c_ctx, w_mod, b_mod, g_norm, w_in, g_q, w_uq,
           g_kv, w_ukv, pool_w, pool_s, conv_w, w_out, g_final):
    batch, seq, _ = x_prompt.shape
    dec_batch, dec_seq, _ = x_sample.shape
    assert 1 + dec_batch <= MOD_ROWS and dec_seq == ROWS_PER_STEP and ROWS_PER_STEP % seq == 0 and DEPTH == 2

    c_all = jnp.concatenate([c_ctx[None], c, jnp.zeros((MOD_ROWS - 1 - dec_batch, D_MODEL), _F32)], axis=0)
    mod_all = _modulation(c_all, w_mod, b_mod)
    w_in_r, w_out_r, w_uqt, w_uk, w_uvt, pool_w_r = _prep_weights(
        jnp.transpose(w_in, (0, 2, 1)), w_out, w_uq, w_ukv, pool_w)
    wts = {
        "g_norm": g_norm, "w_in": w_in_r, "g_q": g_q, "w_uqt": w_uqt, "g_kv": g_kv, "w_uk": w_uk,
        "w_uvt": w_uvt, "pool_w": pool_w_r, "pool_s": pool_s, "conv_w": jnp.transpose(conv_w, (1, 0, 2)), "w_out": w_out_r,
        "g_final": g_final.reshape(1, D_MODEL),
    }

    rope = _rope_table(dec_seq)
    cache = (cache_ckv, jnp.transpose(cache_krope, (0, 1, 3, 2)))
    xp = x_prompt.reshape(batch * seq, D_MODEL)
    xs = x_sample.reshape(dec_batch * dec_seq, D_MODEL)
    xp, state_ckv, state_krope_t = _mixer_pass(xp, mod_all, wts, seq_len=seq, emit_state=True)
    (xs,) = _mixer_pass(xs, mod_all, wts, seq_len=dec_seq, cache=cache, rope=rope, emit_state=False)
    y_prompt = xp.reshape(batch, seq, D_MODEL)
    state_krope = jnp.transpose(state_krope_t, (0, 1, 3, 2))
    y_sample = xs.reshape(dec_batch, dec_seq, D_MODEL)
    return (y_prompt, y_sample, state_ckv, state_krope)
```

```python
import functools

import numpy as np
import jax
import jax.numpy as jnp
from jax import lax
from jax.experimental import pallas as pl
from jax.experimental.pallas import tpu as pltpu

D_MODEL = 1024
DEPTH = 2
GRID_W = 64
MLA_HEADS = 4
NOPE_DIM = 128
ROPE_DIM = 64
V_DIM = 128
QK_DIM = NOPE_DIM + ROPE_DIM
MLA_WIDTH = MLA_HEADS * V_DIM
Q_LORA = 384
KV_LORA = 256
POOL_GROUPS = 4
POOL_GROUP_DIM = 64
POOL_WIDTH = POOL_GROUPS * POOL_GROUP_DIM
CONV_WIDTH = 256
MIX_WIDTH = MLA_WIDTH + POOL_WIDTH + CONV_WIDTH
ROPE_BASE = 10000.0
AXIS_DIM = ROPE_DIM // 2
ATTN_SCALE = QK_DIM ** -0.5
Q_SCALE = ATTN_SCALE * float(np.log2(np.e))
EPS = 1e-6

_SPLITS = (Q_LORA, KV_LORA, ROPE_DIM, MLA_WIDTH, POOL_WIDTH, POOL_WIDTH,
           CONV_WIDTH, CONV_WIDTH, CONV_WIDTH, CONV_WIDTH)
_OFFS = [sum(_SPLITS[:i]) for i in range(len(_SPLITS) + 1)]
IN_WIDTH = _OFFS[-1]

V7X_SUBLANES = 8
V7X_MXU_DEPTH = 256
V7X_VMEM_BYTES = 64 * 1024 * 1024
HEAD_PAD = V7X_MXU_DEPTH
ROPE_PAD = HEAD_PAD - NOPE_DIM
KV_HEAD = NOPE_DIM + V_DIM

_C_CQ = 0
_C_CKV = _C_CQ + Q_LORA
_C_KR = _C_CKV + KV_LORA
_C_GMLA = _C_KR + ROPE_PAD
_C_GPOOL = _C_GMLA + MLA_WIDTH
_C_PX = _C_GPOOL + POOL_WIDTH
_C_CC = _C_PX + POOL_WIDTH
_C_CB = _C_CC + 2 * CONV_WIDTH
_C_END = _C_CB + 2 * CONV_WIDTH

MOD_ROWS = 8
ROWS_PER_STEP = 1024
ROW_CHUNK = 512
HALO = V7X_SUBLANES
PREP_COLS = 256
VMEM_LIMIT_BYTES = V7X_VMEM_BYTES - 4 * 1024 * 1024

_BF = jnp.bfloat16
_F32 = jnp.float32


def _dot(a, b):
    return jnp.dot(a, b, preferred_element_type=_F32)


def _rms(x, g):
    return x * lax.rsqrt(jnp.mean(x * x, axis=-1, keepdims=True) + EPS) * g


def _silu(x):
    return x * jax.nn.sigmoid(x)


def _mod_kernel(c_ref, w_ref, b_ref, o_ref):
    s = _silu(c_ref[...]).astype(_BF)
    for k in range(3):
        cols = slice(k * D_MODEL, (k + 1) * D_MODEL)
        bias = b_ref[pl.ds(pl.program_id(0), 1), cols]
        o_ref[k] = _dot(s, w_ref[:, cols].astype(_BF)) + bias


def _modulation(c_all, w_mod, b_mod):
    return pl.pallas_call(
        _mod_kernel,
        grid=(DEPTH,),
        in_specs=[
            pl.BlockSpec((MOD_ROWS, D_MODEL), lambda l: (0, 0)),
            pl.BlockSpec((None, D_MODEL, 3 * D_MODEL), lambda l: (l, 0, 0)),
            pl.BlockSpec((DEPTH, 3 * D_MODEL), lambda l: (0, 0)),
        ],
        out_specs=pl.BlockSpec((None, 3, MOD_ROWS, D_MODEL), lambda l: (l, 0, 0, 0)),
        out_shape=jax.ShapeDtypeStruct((DEPTH, 3, MOD_ROWS, D_MODEL), _F32),
        compiler_params=pltpu.CompilerParams(dimension_semantics=("arbitrary",),
                                             vmem_limit_bytes=VMEM_LIMIT_BYTES),
        name="modulation",
    )(c_all, w_mod, b_mod)


def _prep_kernel(wint_ref, wout_ref, wuq_ref, wukv_ref, poolw_ref,
                 win_o, wout_o, wuq_o, wukt_o, wuv_o, poolw_o):
    def put(dst, src, width):
        for c in range(0, width, PREP_COLS):
            n = min(PREP_COLS, width - c)
            win_o[:, dst + c:dst + c + n] = wint_ref[src + c:src + c + n, :].T.astype(_BF)

    put(_C_CQ, _OFFS[0], Q_LORA + KV_LORA)
    kr = wint_ref[_OFFS[2]:_OFFS[3], :]
    krb = jnp.concatenate([kr, -kr[AXIS_DIM:], kr[:AXIS_DIM]], axis=0)
    win_o[:, _C_KR:_C_GMLA] = krb.T.astype(_BF)
    put(_C_GMLA, _OFFS[3], MLA_WIDTH)
    put(_C_GPOOL, _OFFS[5], POOL_WIDTH)
    put(_C_PX, _OFFS[4], POOL_WIDTH)
    put(_C_CC, _OFFS[7], 2 * CONV_WIDTH)
    put(_C_CB, _OFFS[6], CONV_WIDTH)
    put(_C_CB + CONV_WIDTH, _OFFS[9], CONV_WIDTH)
    wout_o[...] = wout_ref[...].astype(_BF)

    for hd in range(MLA_HEADS):
        s0, d0 = hd * QK_DIM, hd * HEAD_PAD
        r0 = s0 + NOPE_DIM
        wuq_o[:, d0:d0 + QK_DIM] = wuq_ref[:, s0:s0 + QK_DIM].astype(_BF)
        wuq_o[:, d0 + QK_DIM:d0 + QK_DIM + AXIS_DIM] = (-wuq_ref[:, r0 + AXIS_DIM:r0 + ROPE_DIM]).astype(_BF)
        wuq_o[:, d0 + QK_DIM + AXIS_DIM:d0 + HEAD_PAD] = wuq_ref[:, r0:r0 + AXIS_DIM].astype(_BF)
    for hd in range(MLA_HEADS):
        k0 = hd * KV_HEAD
        wukt_o[hd * NOPE_DIM:(hd + 1) * NOPE_DIM, :] = wukv_ref[:, k0:k0 + NOPE_DIM].T.astype(_BF)
        wuv_o[:, hd * V_DIM:(hd + 1) * V_DIM] = wukv_ref[:, k0 + NOPE_DIM:k0 + KV_HEAD].astype(_BF)
    poolw_o[...] = jnp.zeros((POOL_WIDTH, POOL_WIDTH), _BF)
    for g in range(POOL_GROUPS):
        sl = slice(g * POOL_GROUP_DIM, (g + 1) * POOL_GROUP_DIM)
        poolw_o[sl, sl] = poolw_ref[g].astype(_BF)


def _prep_weights(w_in_t, w_out, w_uq, w_ukv, pool_w):
    per_layer = lambda *shape: pl.BlockSpec((None,) + shape, lambda l: (l,) + (0,) * len(shape))
    return pl.pallas_call(
        _prep_kernel,
        grid=(DEPTH,),
        in_specs=[
            per_layer(IN_WIDTH, D_MODEL),
            per_layer(MIX_WIDTH, D_MODEL),
            per_layer(Q_LORA, MLA_HEADS * QK_DIM),
            per_layer(KV_LORA, MLA_HEADS * KV_HEAD),
            per_layer(POOL_GROUPS, POOL_GROUP_DIM, POOL_GROUP_DIM),
        ],
        out_specs=[
            per_layer(D_MODEL, _C_END),
            per_layer(MIX_WIDTH, D_MODEL),
            per_layer(Q_LORA, MLA_HEADS * HEAD_PAD),
            per_layer(MLA_HEADS * NOPE_DIM, KV_LORA),
            per_layer(KV_LORA, MLA_WIDTH),
            per_layer(POOL_WIDTH, POOL_WIDTH),
        ],
        out_shape=[
            jax.ShapeDtypeStruct((DEPTH, D_MODEL, _C_END), _BF),
            jax.ShapeDtypeStruct((DEPTH, MIX_WIDTH, D_MODEL), _BF),
            jax.ShapeDtypeStruct((DEPTH, Q_LORA, MLA_HEADS * HEAD_PAD), _BF),
            jax.ShapeDtypeStruct((DEPTH, MLA_HEADS * NOPE_DIM, KV_LORA), _BF),
            jax.ShapeDtypeStruct((DEPTH, KV_LORA, MLA_WIDTH), _BF),
            jax.ShapeDtypeStruct((DEPTH, POOL_WIDTH, POOL_WIDTH), _BF),
        ],
        compiler_params=pltpu.CompilerParams(dimension_semantics=("arbitrary",),
                                             vmem_limit_bytes=VMEM_LIMIT_BYTES),
        name="weight_prep",
    )(w_in_t, w_out, w_uq, w_ukv, pool_w)


def _pool_mix(win, rcnt):
    n = win.shape[0]
    rows = n - 2 * HALO
    lane = lax.broadcasted_iota(jnp.int32, (1, POOL_WIDTH), 1)
    sums = [win + pltpu.roll(win, 1, axis=0)]
    for k in (1, 2, 4):
        sums.append(pltpu.roll(sums[-1], k, axis=0) + pltpu.roll(sums[-1], n - k, axis=0))
    sel = sums[-1]
    for g in range(POOL_GROUPS - 2, -1, -1):
        sel = jnp.where(lane < (g + 1) * POOL_GROUP_DIM, sums[g], sel)
    return sel[HALO:HALO + rows] * rcnt - win[HALO:HALO + rows]


def _short_conv(win, w):
    rows = win.shape[0] - 2 * HALO
    return (win[HALO - 1:HALO - 1 + rows] * w[0] + win[HALO:HALO + rows] * w[1]
            + win[HALO + 1:HALO + 1 + rows] * w[2])


def _pass_kernel(*refs, n_seq, seq_len, cache_len, has_rope, emit_state, rc):
    it = iter(refs)
    x_ref, mod_ref = next(it), next(it)
    if cache_len:
        cckv_all_ref, ckr_all_ref = next(it), next(it)
    if has_rope:
        cs_ref = next(it)
    (rcnt_ref, gn_ref, win_all_ref, gq_ref, wuq_all_ref, gkv_ref, wukt_all_ref, wuv_all_ref, poolw_all_ref,
     pools_ref, convw_ref, wout_all_ref, gfin_ref) = (next(it) for _ in range(13))
    y_ref = next(it)
    if emit_state:
        ckv_out_ref, kr_out_ref = next(it), next(it)
    h_s, q_s, kt_s, v_s, px_s, prod_s, mixed_s = (next(it) for _ in range(7))

    m = n_seq * seq_len
    keys = cache_len + seq_len
    padded = seq_len + 2 * HALO
    lane = lax.broadcasted_iota(jnp.int32, (1, ROPE_PAD), 1)
    mrow = (1 + pl.program_id(0)) if has_rope else 0
    piece = min(seq_len, rc)
    n_sub = rc // piece
    pieces = [(s, slice(s * piece, (s + 1) * piece)) for s in range(n_sub)]
    chunks = range(m // rc)

    for c in chunks:
        y_ref[c * rc:(c + 1) * rc, :] = x_ref[c * rc:(c + 1) * rc, :]

    zeros_halo = jnp.zeros((HALO, POOL_WIDTH), _F32)
    for buf in (px_s, prod_s):
        for s in range(n_seq):
            buf[s * padded:s * padded + HALO, :] = zeros_halo
            buf[(s + 1) * padded - HALO:(s + 1) * padded, :] = zeros_halo
    ones_col = jnp.where(lax.broadcasted_iota(jnp.int32, (n_seq * keys, V_DIM), 1) == 0, 1.0, 0.0).astype(_BF)
    for hd in range(MLA_HEADS):
        v_s[hd, :, V_DIM:] = ones_col

    def layer_view(l):
        lw = dict(l=l, row=pl.ds(l, 1), win=win_all_ref.at[l], wuq=wuq_all_ref.at[l], wukt=wukt_all_ref.at[l],
                  wuv=wuv_all_ref.at[l], poolw=poolw_all_ref.at[l], wout=wout_all_ref.at[l])
        lw["shift"], lw["scale"], lw["gate"] = (mod_ref[l, k, pl.ds(mrow, 1), :] for k in range(3))
        if cache_len:
            lw["cckv"], lw["ckr"] = cckv_all_ref.at[l], ckr_all_ref.at[l]
        return lw

    def rotate(v, r):
        v = v * cs_ref[pl.ds(r, rc), :]
        return v + pltpu.roll(v, ROPE_DIM, axis=1)

    def put_kv(lw, ckv, kr_t, kc):
        knt = _dot(lw["wukt"][...], ckv.T.astype(_BF))
        v4 = _dot(ckv.astype(_BF), lw["wuv"][...])
        kr_bf = kr_t.astype(_BF)
        rows_k = pl.ds(kc * rc, rc)
        for hd in range(MLA_HEADS):
            kt_s[hd, kc, :NOPE_DIM, :] = knt[hd * NOPE_DIM:(hd + 1) * NOPE_DIM, :].astype(_BF)
            kt_s[hd, kc, NOPE_DIM:, :] = kr_bf
            v_s[hd, rows_k, :V_DIM] = v4[:, hd * V_DIM:(hd + 1) * V_DIM].astype(_BF)

    def cached_keys(lw):
        ckr_t = lw["ckr"][...]
        ckr_pad = jnp.concatenate([ckr_t, jnp.zeros_like(ckr_t)], axis=0)
        for cc in range(cache_len // rc):
            rows_k = slice(cc * rc, (cc + 1) * rc)
            put_kv(lw, lw["cckv"][rows_k, :], ckr_pad[:, rows_k], cc)

    def halo_base(c, s):
        return (c * n_sub + s) * padded if seq_len <= rc else c * rc

    def project(lw, c):
        r = c * rc
        rows = pl.ds(r, rc)
        row, win_ref = lw["row"], lw["win"]
        h = (_rms(y_ref[rows, :], gn_ref[row, :]) * (1.0 + lw["scale"]) + lw["shift"]).astype(_BF)
        h_s[rows, :] = h
        a = _dot(h, win_ref[:, _C_CQ:_C_GMLA])
        cq = _rms(a[:, :Q_LORA], gq_ref[row, :]).astype(_BF)
        ckv = _rms(a[:, _C_CKV:_C_KR], gkv_ref[row, :])
        kr = a[:, _C_KR:]
        if has_rope:
            kr = rotate(kr, r)
        kr_t = jnp.where(lane < ROPE_DIM, kr, 0.0).T
        if emit_state:
            for s, ps in pieces:
                ckv_out_ref[c * n_sub + s, lw["l"]] = ckv[ps, :]
                kr_out_ref[c * n_sub + s, lw["l"]] = kr_t[:ROPE_DIM, ps]
        put_kv(lw, ckv, kr_t, cache_len // rc + c)
        for hp in range(MLA_HEADS // 2):
            q2 = _dot(cq, lw["wuq"][:, hp * 2 * HEAD_PAD:(hp + 1) * 2 * HEAD_PAD]) * Q_SCALE
            for j in range(2):
                hd = 2 * hp + j
                c0 = j * HEAD_PAD
                q_s[hd, rows, :NOPE_DIM] = q2[:, c0:c0 + NOPE_DIM].astype(_BF)
                qr = q2[:, c0 + NOPE_DIM:c0 + HEAD_PAD]
                if has_rope:
                    qr = rotate(qr, r)
                q_s[hd, rows, NOPE_DIM:] = qr.astype(_BF)
        pe = _dot(h, win_ref[:, _C_PX:_C_CB])
        for s, ps in pieces:
            rows_h = pl.ds(halo_base(c, s) + HALO, piece)
            px_s[rows_h, :] = pe[ps, :POOL_WIDTH]
            prod_s[rows_h, :] = pe[ps, POOL_WIDTH:POOL_WIDTH + CONV_WIDTH] * pe[ps, POOL_WIDTH + CONV_WIDTH:]

    def mix(lw, c):
        r = c * rc
        rows = pl.ds(r, rc)
        row, win_ref = lw["row"], lw["win"]
        h = h_s[rows, :]
        for hp in range(MLA_HEADS // 2):
            g2 = _silu(_dot(h, win_ref[:, _C_GMLA + hp * 2 * V_DIM:_C_GMLA + (hp + 1) * 2 * V_DIM]))
            for j in range(2):
                hd = 2 * hp + j
                for s, ps in pieces:
                    rows_q = pl.ds(r + s * piece, piece)
                    if n_seq == 1:
                        ks = [kt_s[hd, kc] for kc in range(keys // rc)]
                        rows_k = pl.ds(0, keys)
                    else:
                        ks = [kt_s[hd, c, :, ps]]
                        rows_k = rows_q
                    q = q_s[hd, rows_q, :]
                    sc = jnp.concatenate([_dot(q, k) for k in ks], axis=-1)
                    p = jnp.exp2(sc - jnp.max(sc, axis=-1, keepdims=True))
                    ov = _dot(p.astype(_BF), v_s[hd, rows_k, :])
                    o = ov[:, :V_DIM] / ov[:, V_DIM:V_DIM + 1]
                    mixed_s[rows_q, hd * V_DIM:(hd + 1) * V_DIM] = (g2[ps, j * V_DIM:(j + 1) * V_DIM] * o).astype(_BF)

        windows = [pl.ds(halo_base(c, s), piece + 2 * HALO) for s, _ in pieces]
        rcnt = rcnt_ref[pl.ds(0 if seq_len <= rc else r, piece), :]
        pooled = jnp.concatenate([_pool_mix(px_s[w, :], rcnt) for w in windows], axis=0)
        pool = _dot(pooled.astype(_BF), lw["poolw"][...]) * pools_ref[row, :]
        gp = _silu(_dot(h, win_ref[:, _C_GPOOL:_C_PX]))
        mixed_s[rows, MLA_WIDTH:MLA_WIDTH + POOL_WIDTH] = (gp * pool).astype(_BF)

        convw = [convw_ref[k, row, :] for k in range(3)]
        conv = jnp.concatenate([_short_conv(prod_s[w, :], convw) for w in windows], axis=0)
        e2 = _dot(h, win_ref[:, _C_CB:_C_END])
        mixed_s[rows, MLA_WIDTH + POOL_WIDTH:] = (_silu(e2[:, CONV_WIDTH:]) * (e2[:, :CONV_WIDTH] * conv)).astype(_BF)

        y_ref[rows, :] = y_ref[rows, :] + lw["gate"] * _dot(mixed_s[rows, :], lw["wout"][...])

    def one_layer(l, carry):
        lw = layer_view(l)
        if cache_len:
            cached_keys(lw)
        for c in chunks:
            project(lw, c)
        for c in chunks:
            mix(lw, c)
        return carry

    lax.fori_loop(0, DEPTH, one_layer, 0)
    for c in chunks:
        rows = slice(c * rc, (c + 1) * rc)
        y_ref[rows, :] = _rms(y_ref[rows, :], gfin_ref[...])


def _mixer_pass(x2d, mod_all, wts, *, seq_len, cache=None, rope=None, emit_state):
    rows = x2d.shape[0]
    m = ROWS_PER_STEP
    n_seq = m // seq_len
    n_steps = rows // m
    n_all = rows // seq_len
    cache_len = 0 if cache is None else cache[0].shape[2]
    keys = cache_len + seq_len
    has_rope = rope is not None
    rc = ROW_CHUNK
    assert m % rc == 0 and cache_len % rc == 0
    assert (rc % seq_len == 0 and not cache_len) or (n_seq == 1 and seq_len % rc == 0 and not emit_state)

    def const(*shape):
        return pl.BlockSpec(shape, lambda i: (0,) * len(shape), pipeline_mode=pl.Buffered(1))

    args = [x2d, mod_all]
    in_specs = [pl.BlockSpec((m, D_MODEL), lambda i: (i, 0)), const(DEPTH, 3, MOD_ROWS, D_MODEL)]
    if cache_len:
        args += [cache[0], cache[1]]
        in_specs += [pl.BlockSpec((None, DEPTH, cache_len, KV_LORA), lambda i: (i, 0, 0, 0)),
                     pl.BlockSpec((None, DEPTH, ROPE_DIM, cache_len), lambda i: (i, 0, 0, 0))]
    if has_rope:
        args += [rope]
        in_specs += [const(seq_len, ROPE_PAD)]
    args += [_pool_rcnt(seq_len), wts["g_norm"], wts["w_in"], wts["g_q"], wts["w_uq"], wts["g_kv"],
             wts["w_ukt"], wts["w_uv"], wts["pool_w"], wts["pool_s"], wts["conv_w"], wts["w_out"],
             wts["g_final"]]
    in_specs += [const(seq_len, POOL_WIDTH), const(DEPTH, D_MODEL), const(DEPTH, D_MODEL, _C_END),
                 const(DEPTH, Q_LORA), const(DEPTH, Q_LORA, MLA_HEADS * HEAD_PAD), const(DEPTH, KV_LORA),
                 const(DEPTH, MLA_HEADS * NOPE_DIM, KV_LORA), const(DEPTH, KV_LORA, MLA_WIDTH),
                 const(DEPTH, POOL_WIDTH, POOL_WIDTH), const(DEPTH, POOL_WIDTH), const(3, DEPTH, CONV_WIDTH),
                 const(DEPTH, MIX_WIDTH, D_MODEL), const(1, D_MODEL)]

    out_shape = [jax.ShapeDtypeStruct((rows, D_MODEL), _F32)]
    out_specs = [pl.BlockSpec((m, D_MODEL), lambda i: (i, 0))]
    if emit_state:
        out_shape += [jax.ShapeDtypeStruct((n_all, DEPTH, seq_len, KV_LORA), _F32),
                      jax.ShapeDtypeStruct((n_all, DEPTH, ROPE_DIM, seq_len), _F32)]
        out_specs += [pl.BlockSpec((n_seq, DEPTH, seq_len, KV_LORA), lambda i: (i, 0, 0, 0)),
                      pl.BlockSpec((n_seq, DEPTH, ROPE_DIM, seq_len), lambda i: (i, 0, 0, 0))]

    kern = functools.partial(_pass_kernel, n_seq=n_seq, seq_len=seq_len, cache_len=cache_len,
                             has_rope=has_rope, emit_state=emit_state, rc=rc)
    return pl.pallas_call(
        kern,
        grid=(n_steps,),
        in_specs=in_specs,
        out_specs=out_specs,
        out_shape=out_shape,
        scratch_shapes=[
            pltpu.VMEM((m, D_MODEL), _BF),
            pltpu.VMEM((MLA_HEADS, m, HEAD_PAD), _BF),
            pltpu.VMEM((MLA_HEADS, n_seq * keys // rc, HEAD_PAD, rc), _BF),
            pltpu.VMEM((MLA_HEADS, n_seq * keys, 2 * V_DIM), _BF),
            pltpu.VMEM((n_seq * (seq_len + 2 * HALO), POOL_WIDTH), _F32),
            pltpu.VMEM((n_seq * (seq_len + 2 * HALO), CONV_WIDTH), _F32),
            pltpu.VMEM((m, MIX_WIDTH), _BF),
        ],
        compiler_params=pltpu.CompilerParams(dimension_semantics=("arbitrary",),
                                             vmem_limit_bytes=VMEM_LIMIT_BYTES),
        name="mixer_pass_latent" if has_rope else "mixer_pass_context",
    )(*args)


def _pool_rcnt(seq_len):
    t = np.arange(seq_len)[:, None]
    half = np.repeat(1 << np.arange(POOL_GROUPS), POOL_GROUP_DIM)[None, :]
    cnt = np.minimum(t + half, seq_len) - np.maximum(t - half, 0)
    return jnp.asarray((1.0 / cnt).astype(np.float32))


def _rope_table(seq_len):
    rows = seq_len // GRID_W
    row = np.repeat(np.arange(rows), GRID_W).astype(np.float32)
    col = np.tile(np.arange(GRID_W), rows).astype(np.float32)
    inv = (1.0 / (np.float32(ROPE_BASE) ** (np.arange(0, AXIS_DIM, 2, dtype=np.float32) / np.float32(AXIS_DIM))))
    inv = inv.astype(np.float32)
    ang = np.concatenate([row[:, None] * inv, col[:, None] * inv], axis=-1).astype(np.float64)
    cos, sin = np.cos(ang).astype(np.float32), np.sin(ang).astype(np.float32)
    return jnp.asarray(np.concatenate([cos, cos, sin, sin], axis=-1))


def kernel(x_prompt, x_sample, cache_ckv, cache_krope, c, c_ctx, w_mod, b_mod, g_norm, w_in, g_q, w_uq,
           g_kv, w_ukv, pool_w, pool_s, conv_w, w_out, g_final):
    batch, seq, _ = x_prompt.shape
    dec_batch, dec_seq, _ = x_sample.shape
    assert 1 + dec_batch <= MOD_ROWS and dec_seq == ROWS_PER_STEP and ROWS_PER_STEP % seq == 0

    c_all = jnp.concatenate([c_ctx[None], c, jnp.zeros((MOD_ROWS - 1 - dec_batch, D_MODEL), _F32)], axis=0)
    mod_all = _modulation(c_all, w_mod, b_mod)
    w_in_r, w_out_r, w_uq_r, w_ukt, w_uv, pool_w_r = _prep_weights(
        jnp.transpose(w_in, (0, 2, 1)), w_out, w_uq, w_ukv, pool_w)
    wts = {
        "g_norm": g_norm, "w_in": w_in_r, "g_q": g_q, "w_uq": w_uq_r, "g_kv": g_kv, "w_ukt": w_ukt,
        "w_uv": w_uv, "pool_w": pool_w_r, "pool_s": pool_s, "conv_w": jnp.transpose(conv_w, (1, 0, 2)),
        "w_out": w_out_r, "g_final": g_final.reshape(1, D_MODEL),
    }

    rope = _rope_table(dec_seq)
    cache = (cache_ckv, jnp.transpose(cache_krope, (0, 1, 3, 2)))
    xp = x_prompt.reshape(batch * seq, D_MODEL)
    xs = x_sample.reshape(dec_batch * dec_seq, D_MODEL)
    xp, state_ckv, state_krope_t = _mixer_pass(xp, mod_all, wts, seq_len=seq, emit_state=True)
    (xs,) = _mixer_pass(xs, mod_all, wts, seq_len=dec_seq, cache=cache, rope=rope, emit_state=False)
    y_prompt = xp.reshape(batch, seq, D_MODEL)
    state_krope = jnp.transpose(state_krope_t, (0, 1, 3, 2))
    y_sample = xs.reshape(dec_batch, dec_seq, D_MODEL)
    return (y_prompt, y_sample, state_ckv, state_krope)
```

```python
import functools

import numpy as np
import jax
import jax.numpy as jnp
from jax import lax
from jax.experimental import pallas as pl
from jax.experimental.pallas import tpu as pltpu

D_MODEL = 1024
DEPTH = 2
GRID_W = 64
MLA_HEADS = 4
NOPE_DIM = 128
ROPE_DIM = 64
V_DIM = 128
QK_DIM = NOPE_DIM + ROPE_DIM
MLA_WIDTH = MLA_HEADS * V_DIM
Q_LORA = 384
KV_LORA = 256
POOL_GROUPS = 4
POOL_GROUP_DIM = 64
POOL_WIDTH = POOL_GROUPS * POOL_GROUP_DIM
CONV_WIDTH = 256
MIX_WIDTH = MLA_WIDTH + POOL_WIDTH + CONV_WIDTH
ROPE_BASE = 10000.0
AXIS_DIM = ROPE_DIM // 2
ATTN_SCALE = QK_DIM ** -0.5
Q_SCALE = ATTN_SCALE * float(np.log2(np.e))
EPS = 1e-6

_SPLITS = (Q_LORA, KV_LORA, ROPE_DIM, MLA_WIDTH, POOL_WIDTH, POOL_WIDTH,
           CONV_WIDTH, CONV_WIDTH, CONV_WIDTH, CONV_WIDTH)
_OFFS = [sum(_SPLITS[:i]) for i in range(len(_SPLITS) + 1)]
IN_WIDTH = _OFFS[-1]

V7X_SUBLANES = 8
V7X_MXU_DEPTH = 256
V7X_VMEM_BYTES = 64 * 1024 * 1024
HEAD_PAD = V7X_MXU_DEPTH
ROPE_PAD = HEAD_PAD - NOPE_DIM
KV_HEAD = NOPE_DIM + V_DIM

_C_CQ = 0
_C_CKV = _C_CQ + Q_LORA
_C_KR = _C_CKV + KV_LORA
_C_GMLA = _C_KR + ROPE_PAD
_C_GPOOL = _C_GMLA + MLA_WIDTH
_C_PX = _C_GPOOL + POOL_WIDTH
_C_CC = _C_PX + POOL_WIDTH
_C_CB = _C_CC + 2 * CONV_WIDTH
_C_END = _C_CB + 2 * CONV_WIDTH

MOD_ROWS = 8
ROWS_PER_STEP = 1024
ROW_CHUNK = 512
ATT_ROWS = 256
HALO = V7X_SUBLANES
PREP_COLS = 256
VMEM_LIMIT_BYTES = V7X_VMEM_BYTES - 4 * 1024 * 1024

_BF = jnp.bfloat16
_F32 = jnp.float32


def _dot(a, b):
    return jnp.dot(a, b, preferred_element_type=_F32)


def _rms(x, g):
    return x * lax.rsqrt(jnp.mean(x * x, axis=-1, keepdims=True) + EPS) * g


def _silu(x):
    return x * jax.nn.sigmoid(x)


def _mod_kernel(c_ref, w_ref, b_ref, o_ref):
    s = _silu(c_ref[...]).astype(_BF)
    for k in range(3):
        cols = slice(k * D_MODEL, (k + 1) * D_MODEL)
        bias = b_ref[pl.ds(pl.program_id(0), 1), cols]
        o_ref[k] = _dot(s, w_ref[:, cols].astype(_BF)) + bias


def _modulation(c_all, w_mod, b_mod):
    return pl.pallas_call(
        _mod_kernel,
        grid=(DEPTH,),
        in_specs=[
            pl.BlockSpec((MOD_ROWS, D_MODEL), lambda l: (0, 0)),
            pl.BlockSpec((None, D_MODEL, 3 * D_MODEL), lambda l: (l, 0, 0)),
            pl.BlockSpec((DEPTH, 3 * D_MODEL), lambda l: (0, 0)),
        ],
        out_specs=pl.BlockSpec((None, 3, MOD_ROWS, D_MODEL), lambda l: (l, 0, 0, 0)),
        out_shape=jax.ShapeDtypeStruct((DEPTH, 3, MOD_ROWS, D_MODEL), _F32),
        compiler_params=pltpu.CompilerParams(dimension_semantics=("arbitrary",),
                                             vmem_limit_bytes=VMEM_LIMIT_BYTES),
        name="modulation",
    )(c_all, w_mod, b_mod)


def _prep_kernel(wint_ref, wout_ref, wuq_ref, wukv_ref, poolw_ref,
                 win_o, wout_o, wuq_o, wukt_o, wuv_o, poolw_o):
    def put(dst, src, width):
        for c in range(0, width, PREP_COLS):
            n = min(PREP_COLS, width - c)
            win_o[:, dst + c:dst + c + n] = wint_ref[src + c:src + c + n, :].T.astype(_BF)

    put(_C_CQ, _OFFS[0], Q_LORA + KV_LORA)
    kr = wint_ref[_OFFS[2]:_OFFS[3], :]
    krb = jnp.concatenate([kr, -kr[AXIS_DIM:], kr[:AXIS_DIM]], axis=0)
    win_o[:, _C_KR:_C_GMLA] = krb.T.astype(_BF)
    put(_C_GMLA, _OFFS[3], MLA_WIDTH)
    put(_C_GPOOL, _OFFS[5], POOL_WIDTH)
    put(_C_PX, _OFFS[4], POOL_WIDTH)
    put(_C_CC, _OFFS[7], 2 * CONV_WIDTH)
    put(_C_CB, _OFFS[6], CONV_WIDTH)
    put(_C_CB + CONV_WIDTH, _OFFS[9], CONV_WIDTH)
    wout_o[...] = wout_ref[...].astype(_BF)

    for hd in range(MLA_HEADS):
        s0, d0 = hd * QK_DIM, hd * HEAD_PAD
        r0 = s0 + NOPE_DIM
        wuq_o[:, d0:d0 + QK_DIM] = wuq_ref[:, s0:s0 + QK_DIM].astype(_BF)
        wuq_o[:, d0 + QK_DIM:d0 + QK_DIM + AXIS_DIM] = (-wuq_ref[:, r0 + AXIS_DIM:r0 + ROPE_DIM]).astype(_BF)
        wuq_o[:, d0 + QK_DIM + AXIS_DIM:d0 + HEAD_PAD] = wuq_ref[:, r0:r0 + AXIS_DIM].astype(_BF)
    for hd in range(MLA_HEADS):
        k0 = hd * KV_HEAD
        wukt_o[hd * NOPE_DIM:(hd + 1) * NOPE_DIM, :] = wukv_ref[:, k0:k0 + NOPE_DIM].T.astype(_BF)
        wuv_o[:, hd * V_DIM:(hd + 1) * V_DIM] = wukv_ref[:, k0 + NOPE_DIM:k0 + KV_HEAD].astype(_BF)
    poolw_o[...] = jnp.zeros((POOL_WIDTH, POOL_WIDTH), _BF)
    for g in range(POOL_GROUPS):
        sl = slice(g * POOL_GROUP_DIM, (g + 1) * POOL_GROUP_DIM)
        poolw_o[sl, sl] = poolw_ref[g].astype(_BF)


def _prep_weights(w_in_t, w_out, w_uq, w_ukv, pool_w):
    per_layer = lambda *shape: pl.BlockSpec((None,) + shape, lambda l: (l,) + (0,) * len(shape))
    return pl.pallas_call(
        _prep_kernel,
        grid=(DEPTH,),
        in_specs=[
            per_layer(IN_WIDTH, D_MODEL),
            per_layer(MIX_WIDTH, D_MODEL),
            per_layer(Q_LORA, MLA_HEADS * QK_DIM),
            per_layer(KV_LORA, MLA_HEADS * KV_HEAD),
            per_layer(POOL_GROUPS, POOL_GROUP_DIM, POOL_GROUP_DIM),
        ],
        out_specs=[
            per_layer(D_MODEL, _C_END),
            per_layer(MIX_WIDTH, D_MODEL),
            per_layer(Q_LORA, MLA_HEADS * HEAD_PAD),
            per_layer(MLA_HEADS * NOPE_DIM, KV_LORA),
            per_layer(KV_LORA, MLA_WIDTH),
            per_layer(POOL_WIDTH, POOL_WIDTH),
        ],
        out_shape=[
            jax.ShapeDtypeStruct((DEPTH, D_MODEL, _C_END), _BF),
            jax.ShapeDtypeStruct((DEPTH, MIX_WIDTH, D_MODEL), _BF),
            jax.ShapeDtypeStruct((DEPTH, Q_LORA, MLA_HEADS * HEAD_PAD), _BF),
            jax.ShapeDtypeStruct((DEPTH, MLA_HEADS * NOPE_DIM, KV_LORA), _BF),
            jax.ShapeDtypeStruct((DEPTH, KV_LORA, MLA_WIDTH), _BF),
            jax.ShapeDtypeStruct((DEPTH, POOL_WIDTH, POOL_WIDTH), _BF),
        ],
        compiler_params=pltpu.CompilerParams(dimension_semantics=("arbitrary",),
                                             vmem_limit_bytes=VMEM_LIMIT_BYTES),
        name="weight_prep",
    )(w_in_t, w_out, w_uq, w_ukv, pool_w)


def _pool_mix(win, rcnt):
    n = win.shape[0]
    rows = n - 2 * HALO
    lane = lax.broadcasted_iota(jnp.int32, (1, POOL_WIDTH), 1)
    sums = [win + pltpu.roll(win, 1, axis=0)]
    for k in (1, 2, 4):
        sums.append(pltpu.roll(sums[-1], k, axis=0) + pltpu.roll(sums[-1], n - k, axis=0))
    sel = sums[-1]
    for g in range(POOL_GROUPS - 2, -1, -1):
        sel = jnp.where(lane < (g + 1) * POOL_GROUP_DIM, sums[g], sel)
    return sel[HALO:HALO + rows] * rcnt - win[HALO:HALO + rows]


def _short_conv(win, w):
    rows = win.shape[0] - 2 * HALO
    return (win[HALO - 1:HALO - 1 + rows] * w[0] + win[HALO:HALO + rows] * w[1]
            + win[HALO + 1:HALO + 1 + rows] * w[2])


def _pass_kernel(*refs, n_seq, seq_len, cache_len, has_rope, emit_state, rc):
    it = iter(refs)
    x_ref, mod_ref = next(it), next(it)
    if cache_len:
        cckv_all_ref, ckr_all_ref = next(it), next(it)
    if has_rope:
        cs_ref = next(it)
    (rcnt_ref, gn_ref, win_all_ref, gq_ref, wuq_all_ref, gkv_ref, wukt_all_ref, wuv_all_ref, poolw_all_ref,
     pools_ref, convw_ref, wout_all_ref, gfin_ref) = (next(it) for _ in range(13))
    y_ref = next(it)
    if emit_state:
        ckv_out_ref, kr_out_ref = next(it), next(it)
    h_s, q_s, kt_s, v_s, px_s, prod_s, mixed_s = (next(it) for _ in range(7))

    m = n_seq * seq_len
    keys = cache_len + seq_len
    padded = seq_len + 2 * HALO
    lane = lax.broadcasted_iota(jnp.int32, (1, ROPE_PAD), 1)
    mrow = (1 + pl.program_id(0)) if has_rope else 0
    piece = min(seq_len, rc)
    n_sub = rc // piece
    pieces = [(s, slice(s * piece, (s + 1) * piece)) for s in range(n_sub)]
    att_tiles = [(s, slice(s * ATT_ROWS, (s + 1) * ATT_ROWS)) for s in range(rc // ATT_ROWS)]
    chunks = range(m // rc)

    for c in chunks:
        y_ref[c * rc:(c + 1) * rc, :] = x_ref[c * rc:(c + 1) * rc, :]

    zeros_halo = jnp.zeros((HALO, POOL_WIDTH), _F32)
    for buf in (px_s, prod_s):
        for s in range(n_seq):
            buf[s * padded:s * padded + HALO, :] = zeros_halo
            buf[(s + 1) * padded - HALO:(s + 1) * padded, :] = zeros_halo
    ones_col = jnp.where(lax.broadcasted_iota(jnp.int32, (n_seq * keys, V_DIM), 1) == 0, 1.0, 0.0).astype(_BF)
    for hd in range(MLA_HEADS):
        v_s[hd, :, V_DIM:] = ones_col

    def layer_view(l):
        lw = dict(l=l, row=pl.ds(l, 1), win=win_all_ref.at[l], wuq=wuq_all_ref.at[l], wukt=wukt_all_ref.at[l],
                  wuv=wuv_all_ref.at[l], poolw=poolw_all_ref.at[l], wout=wout_all_ref.at[l])
        lw["shift"], lw["scale"], lw["gate"] = (mod_ref[l, k, pl.ds(mrow, 1), :] for k in range(3))
        if cache_len:
            lw["cckv"], lw["ckr"] = cckv_all_ref.at[l], ckr_all_ref.at[l]
        return lw

    def rotate(v, r):
        v = v * cs_ref[pl.ds(r, rc), :]
        return v + pltpu.roll(v, ROPE_DIM, axis=1)

    def put_kv(lw, ckv, kr_t, kc):
        knt = _dot(lw["wukt"][...], ckv.T.astype(_BF))
        v4 = _dot(ckv.astype(_BF), lw["wuv"][...])
        kr_bf = kr_t.astype(_BF)
        rows_k = pl.ds(kc * rc, rc)
        for hd in range(MLA_HEADS):
            kt_s[hd, kc, :NOPE_DIM, :] = knt[hd * NOPE_DIM:(hd + 1) * NOPE_DIM, :].astype(_BF)
            kt_s[hd, kc, NOPE_DIM:, :] = kr_bf
            v_s[hd, rows_k, :V_DIM] = v4[:, hd * V_DIM:(hd + 1) * V_DIM].astype(_BF)

    def cached_keys(lw):
        ckr_t = lw["ckr"][...]
        ckr_pad = jnp.concatenate([ckr_t, jnp.zeros_like(ckr_t)], axis=0)
        for cc in range(cache_len // rc):
            rows_k = slice(cc * rc, (cc + 1) * rc)
            put_kv(lw, lw["cckv"][rows_k, :], ckr_pad[:, rows_k], cc)

    def halo_base(c, s):
        return (c * n_sub + s) * padded if seq_len <= rc else c * rc

    def project(lw, c):
        r = c * rc
        rows = pl.ds(r, rc)
        row, win_ref = lw["row"], lw["win"]
        h = (_rms(y_ref[rows, :], gn_ref[row, :]) * (1.0 + lw["scale"]) + lw["shift"]).astype(_BF)
        h_s[rows, :] = h
        a = _dot(h, win_ref[:, _C_CQ:_C_GMLA])
        cq = _rms(a[:, :Q_LORA], gq_ref[row, :]).astype(_BF)
        ckv = _rms(a[:, _C_CKV:_C_KR], gkv_ref[row, :])
        kr = a[:, _C_KR:]
        if has_rope:
            kr = rotate(kr, r)
        kr_t = jnp.where(lane < ROPE_DIM, kr, 0.0).T
        if emit_state:
            for s, ps in pieces:
                ckv_out_ref[c * n_sub + s, lw["l"]] = ckv[ps, :]
                kr_out_ref[c * n_sub + s, lw["l"]] = kr_t[:ROPE_DIM, ps]
        put_kv(lw, ckv, kr_t, cache_len // rc + c)
        for hp in range(MLA_HEADS // 2):
            q2 = _dot(cq, lw["wuq"][:, hp * 2 * HEAD_PAD:(hp + 1) * 2 * HEAD_PAD]) * Q_SCALE
            for j in range(2):
                hd = 2 * hp + j
                c0 = j * HEAD_PAD
                q_s[hd, rows, :NOPE_DIM] = q2[:, c0:c0 + NOPE_DIM].astype(_BF)
                qr = q2[:, c0 + NOPE_DIM:c0 + HEAD_PAD]
                if has_rope:
                    qr = rotate(qr, r)
                q_s[hd, rows, NOPE_DIM:] = qr.astype(_BF)
        pe = _dot(h, win_ref[:, _C_PX:_C_CB])
        for s, ps in pieces:
            rows_h = pl.ds(halo_base(c, s) + HALO, piece)
            px_s[rows_h, :] = pe[ps, :POOL_WIDTH]
            prod_s[rows_h, :] = pe[ps, POOL_WIDTH:POOL_WIDTH + CONV_WIDTH] * pe[ps, POOL_WIDTH + CONV_WIDTH:]

    def mix(lw, c):
        r = c * rc
        rows = pl.ds(r, rc)
        row, win_ref = lw["row"], lw["win"]
        h = h_s[rows, :]
        for hp in range(MLA_HEADS // 2):
            g2 = _silu(_dot(h, win_ref[:, _C_GMLA + hp * 2 * V_DIM:_C_GMLA + (hp + 1) * 2 * V_DIM]))
            for j in range(2):
                hd = 2 * hp + j
                for s, ps in att_tiles:
                    rows_q = pl.ds(r + ps.start, ATT_ROWS)
                    if n_seq == 1:
                        ks = [kt_s[hd, kc] for kc in range(keys // rc)]
                        rows_k = pl.ds(0, keys)
                    else:
                        ks = [kt_s[hd, c, :, ps]]
                        rows_k = rows_q
                    q = q_s[hd, rows_q, :]
                    sc = jnp.concatenate([_dot(q, k) for k in ks], axis=-1)
                    p = jnp.exp2(sc - jnp.max(sc, axis=-1, keepdims=True))
                    ov = _dot(p.astype(_BF), v_s[hd, rows_k, :])
                    o = ov[:, :V_DIM] / ov[:, V_DIM:V_DIM + 1]
                    mixed_s[rows_q, hd * V_DIM:(hd + 1) * V_DIM] = (g2[ps, j * V_DIM:(j + 1) * V_DIM] * o).astype(_BF)

        windows = [pl.ds(halo_base(c, s), piece + 2 * HALO) for s, _ in pieces]
        rcnt = rcnt_ref[pl.ds(0 if seq_len <= rc else r, piece), :]
        pooled = jnp.concatenate([_pool_mix(px_s[w, :], rcnt) for w in windows], axis=0)
        pool = _dot(pooled.astype(_BF), lw["poolw"][...]) * pools_ref[row, :]
        gp = _silu(_dot(h, win_ref[:, _C_GPOOL:_C_PX]))
        mixed_s[rows, MLA_WIDTH:MLA_WIDTH + POOL_WIDTH] = (gp * pool).astype(_BF)

        convw = [convw_ref[k, row, :] for k in range(3)]
        conv = jnp.concatenate([_short_conv(prod_s[w, :], convw) for w in windows], axis=0)
        e2 = _dot(h, win_ref[:, _C_CB:_C_END])
        mixed_s[rows, MLA_WIDTH + POOL_WIDTH:] = (_silu(e2[:, CONV_WIDTH:]) * (e2[:, :CONV_WIDTH] * conv)).astype(_BF)

        y_ref[rows, :] = y_ref[rows, :] + lw["gate"] * _dot(mixed_s[rows, :], lw["wout"][...])

    def one_layer(l, carry):
        lw = layer_view(l)
        if cache_len:
            cached_keys(lw)
        for c in chunks:
            project(lw, c)
        for c in chunks:
            mix(lw, c)
        return carry

    lax.fori_loop(0, DEPTH, one_layer, 0)
    for c in chunks:
        rows = slice(c * rc, (c + 1) * rc)
        y_ref[rows, :] = _rms(y_ref[rows, :], gfin_ref[...])


def _mixer_pass(x2d, mod_all, wts, *, seq_len, cache=None, rope=None, emit_state):
    rows = x2d.shape[0]
    m = ROWS_PER_STEP
    n_seq = m // seq_len
    n_steps = rows // m
    n_all = rows // seq_len
    cache_len = 0 if cache is None else cache[0].shape[2]
    keys = cache_len + seq_len
    has_rope = rope is not None
    rc = ROW_CHUNK
    assert m % rc == 0 and cache_len % rc == 0
    assert (rc % seq_len == 0 and not cache_len) or (n_seq == 1 and seq_len % rc == 0 and not emit_state)

    def const(*shape):
        return pl.BlockSpec(shape, lambda i: (0,) * len(shape), pipeline_mode=pl.Buffered(1))

    args = [x2d, mod_all]
    in_specs = [pl.BlockSpec((m, D_MODEL), lambda i: (i, 0)), const(DEPTH, 3, MOD_ROWS, D_MODEL)]
    if cache_len:
        args += [cache[0], cache[1]]
        in_specs += [pl.BlockSpec((None, DEPTH, cache_len, KV_LORA), lambda i: (i, 0, 0, 0)),
                     pl.BlockSpec((None, DEPTH, ROPE_DIM, cache_len), lambda i: (i, 0, 0, 0))]
    if has_rope:
        args += [rope]
        in_specs += [const(seq_len, ROPE_PAD)]
    args += [_pool_rcnt(seq_len), wts["g_norm"], wts["w_in"], wts["g_q"], wts["w_uq"], wts["g_kv"],
             wts["w_ukt"], wts["w_uv"], wts["pool_w"], wts["pool_s"], wts["conv_w"], wts["w_out"],
             wts["g_final"]]
    in_specs += [const(seq_len, POOL_WIDTH), const(DEPTH, D_MODEL), const(DEPTH, D_MODEL, _C_END),
                 const(DEPTH, Q_LORA), const(DEPTH, Q_LORA, MLA_HEADS * HEAD_PAD), const(DEPTH, KV_LORA),
                 const(DEPTH, MLA_HEADS * NOPE_DIM, KV_LORA), const(DEPTH, KV_LORA, MLA_WIDTH),
                 const(DEPTH, POOL_WIDTH, POOL_WIDTH), const(DEPTH, POOL_WIDTH), const(3, DEPTH, CONV_WIDTH),
                 const(DEPTH, MIX_WIDTH, D_MODEL), const(1, D_MODEL)]

    out_shape = [jax.ShapeDtypeStruct((rows, D_MODEL), _F32)]
    out_specs = [pl.BlockSpec((m, D_MODEL), lambda i: (i, 0))]
    if emit_state:
        out_shape += [jax.ShapeDtypeStruct((n_all, DEPTH, seq_len, KV_LORA), _F32),
                      jax.ShapeDtypeStruct((n_all, DEPTH, ROPE_DIM, seq_len), _F32)]
        out_specs += [pl.BlockSpec((n_seq, DEPTH, seq_len, KV_LORA), lambda i: (i, 0, 0, 0)),
                      pl.BlockSpec((n_seq, DEPTH, ROPE_DIM, seq_len), lambda i: (i, 0, 0, 0))]

    kern = functools.partial(_pass_kernel, n_seq=n_seq, seq_len=seq_len, cache_len=cache_len,
                             has_rope=has_rope, emit_state=emit_state, rc=rc)
    return pl.pallas_call(
        kern,
        grid=(n_steps,),
        in_specs=in_specs,
        out_specs=out_specs,
        out_shape=out_shape,
        scratch_shapes=[
            pltpu.VMEM((m, D_MODEL), _BF),
            pltpu.VMEM((MLA_HEADS, m, HEAD_PAD), _BF),
            pltpu.VMEM((MLA_HEADS, n_seq * keys // rc, HEAD_PAD, rc), _BF),
            pltpu.VMEM((MLA_HEADS, n_seq * keys, 2 * V_DIM), _BF),
            pltpu.VMEM((n_seq * (seq_len + 2 * HALO), POOL_WIDTH), _F32),
            pltpu.VMEM((n_seq * (seq_len + 2 * HALO), CONV_WIDTH), _F32),
            pltpu.VMEM((m, MIX_WIDTH), _BF),
        ],
        compiler_params=pltpu.CompilerParams(dimension_semantics=("arbitrary",),
                                             vmem_limit_bytes=VMEM_LIMIT_BYTES),
        name="mixer_pass_latent" if has_rope else "mixer_pass_context",
    )(*args)


def _pool_rcnt(seq_len):
    t = np.arange(seq_len)[:, None]
    half = np.repeat(1 << np.arange(POOL_GROUPS), POOL_GROUP_DIM)[None, :]
    cnt = np.minimum(t + half, seq_len) - np.maximum(t - half, 0)
    return jnp.asarray((1.0 / cnt).astype(np.float32))


def _rope_table(seq_len):
    rows = seq_len // GRID_W
    row = np.repeat(np.arange(rows), GRID_W).astype(np.float32)
    col = np.tile(np.arange(GRID_W), rows).astype(np.float32)
    inv = (1.0 / (np.float32(ROPE_BASE) ** (np.arange(0, AXIS_DIM, 2, dtype=np.float32) / np.float32(AXIS_DIM))))
    inv = inv.astype(np.float32)
    ang = np.concatenate([row[:, None] * inv, col[:, None] * inv], axis=-1).astype(np.float64)
    cos, sin = np.cos(ang).astype(np.float32), np.sin(ang).astype(np.float32)
    return jnp.asarray(np.concatenate([cos, cos, sin, sin], axis=-1))


def kernel(x_prompt, x_sample, cache_ckv, cache_krope, c, c_ctx, w_mod, b_mod, g_norm, w_in, g_q, w_uq,
           g_kv, w_ukv, pool_w, pool_s, conv_w, w_out, g_final):
    batch, seq, _ = x_prompt.shape
    dec_batch, dec_seq, _ = x_sample.shape
    assert 1 + dec_batch <= MOD_ROWS and dec_seq == ROWS_PER_STEP and ROWS_PER_STEP % seq == 0

    c_all = jnp.concatenate([c_ctx[None], c, jnp.zeros((MOD_ROWS - 1 - dec_batch, D_MODEL), _F32)], axis=0)
    mod_all = _modulation(c_all, w_mod, b_mod)
    w_in_r, w_out_r, w_uq_r, w_ukt, w_uv, pool_w_r = _prep_weights(
        jnp.transpose(w_in, (0, 2, 1)), w_out, w_uq, w_ukv, pool_w)
    wts = {
        "g_norm": g_norm, "w_in": w_in_r, "g_q": g_q, "w_uq": w_uq_r, "g_kv": g_kv, "w_ukt": w_ukt,
        "w_uv": w_uv, "pool_w": pool_w_r, "pool_s": pool_s, "conv_w": jnp.transpose(conv_w, (1, 0, 2)),
        "w_out": w_out_r, "g_final": g_final.reshape(1, D_MODEL),
    }

    rope = _rope_table(dec_seq)
    cache = (cache_ckv, jnp.transpose(cache_krope, (0, 1, 3, 2)))
    xp = x_prompt.reshape(batch * seq, D_MODEL)
    xs = x_sample.reshape(dec_batch * dec_seq, D_MODEL)
    xp, state_ckv, state_krope_t = _mixer_pass(xp, mod_all, wts, seq_len=seq, emit_state=True)
    (xs,) = _mixer_pass(xs, mod_all, wts, seq_len=dec_seq, cache=cache, rope=rope, emit_state=False)
    y_prompt = xp.reshape(batch, seq, D_MODEL)
    state_krope = jnp.transpose(state_krope_t, (0, 1, 3, 2))
    y_sample = xs.reshape(dec_batch, dec_seq, D_MODEL)
    return (y_prompt, y_sample, state_ckv, state_krope)
```

```python
import functools

import numpy as np
import jax
import jax.numpy as jnp
from jax import lax
from jax.experimental import pallas as pl
from jax.experimental.pallas import tpu as pltpu

D_MODEL = 1024
DEPTH = 2
GRID_W = 64
MLA_HEADS = 4
NOPE_DIM = 128
ROPE_DIM = 64
V_DIM = 128
QK_DIM = NOPE_DIM + ROPE_DIM
MLA_WIDTH = MLA_HEADS * V_DIM
Q_LORA = 384
KV_LORA = 256
POOL_GROUPS = 4
POOL_GROUP_DIM = 64
POOL_WIDTH = POOL_GROUPS * POOL_GROUP_DIM
CONV_WIDTH = 256
MIX_WIDTH = MLA_WIDTH + POOL_WIDTH + CONV_WIDTH
ROPE_BASE = 10000.0
AXIS_DIM = ROPE_DIM // 2
ATTN_SCALE = QK_DIM ** -0.5
Q_SCALE = ATTN_SCALE * float(np.log2(np.e))
EPS = 1e-6

_SPLITS = (Q_LORA, KV_LORA, ROPE_DIM, MLA_WIDTH, POOL_WIDTH, POOL_WIDTH,
           CONV_WIDTH, CONV_WIDTH, CONV_WIDTH, CONV_WIDTH)
_OFFS = [sum(_SPLITS[:i]) for i in range(len(_SPLITS) + 1)]
IN_WIDTH = _OFFS[-1]

V7X_SUBLANES = 8
V7X_MXU_DEPTH = 256
V7X_VMEM_BYTES = 64 * 1024 * 1024
HEAD_PAD = V7X_MXU_DEPTH
ROPE_PAD = HEAD_PAD - NOPE_DIM
KV_HEAD = NOPE_DIM + V_DIM

_C_CQ = 0
_C_CKV = _C_CQ + Q_LORA
_C_KR = _C_CKV + KV_LORA
_C_GMLA = _C_KR + ROPE_PAD
_C_GPOOL = _C_GMLA + MLA_WIDTH
_C_PX = _C_GPOOL + POOL_WIDTH
_C_CC = _C_PX + POOL_WIDTH
_C_CB = _C_CC + 2 * CONV_WIDTH
_C_END = _C_CB + 2 * CONV_WIDTH

MOD_ROWS = 8
ROWS_PER_STEP = 1024
ROW_CHUNK = 512
HALO = V7X_SUBLANES
PREP_COLS = 256
VMEM_LIMIT_BYTES = V7X_VMEM_BYTES - 4 * 1024 * 1024

_BF = jnp.bfloat16
_F32 = jnp.float32


def _dot(a, b):
    return jnp.dot(a, b, preferred_element_type=_F32)


def _rms(x, g):
    return x * lax.rsqrt(jnp.mean(x * x, axis=-1, keepdims=True) + EPS) * g


def _silu(x):
    return x * jax.nn.sigmoid(x)


def _mod_kernel(c_ref, w_ref, b_ref, o_ref):
    s = _silu(c_ref[...]).astype(_BF)
    for k in range(3):
        cols = slice(k * D_MODEL, (k + 1) * D_MODEL)
        bias = b_ref[pl.ds(pl.program_id(0), 1), cols]
        o_ref[k] = _dot(s, w_ref[:, cols].astype(_BF)) + bias


def _modulation(c_all, w_mod, b_mod):
    return pl.pallas_call(
        _mod_kernel,
        grid=(DEPTH,),
        in_specs=[
            pl.BlockSpec((MOD_ROWS, D_MODEL), lambda l: (0, 0)),
            pl.BlockSpec((None, D_MODEL, 3 * D_MODEL), lambda l: (l, 0, 0)),
            pl.BlockSpec((DEPTH, 3 * D_MODEL), lambda l: (0, 0)),
        ],
        out_specs=pl.BlockSpec((None, 3, MOD_ROWS, D_MODEL), lambda l: (l, 0, 0, 0)),
        out_shape=jax.ShapeDtypeStruct((DEPTH, 3, MOD_ROWS, D_MODEL), _F32),
        compiler_params=pltpu.CompilerParams(dimension_semantics=("arbitrary",),
                                             vmem_limit_bytes=VMEM_LIMIT_BYTES),
        name="modulation",
    )(c_all, w_mod, b_mod)


def _prep_kernel(wint_ref, wout_ref, wuq_ref, wukv_ref, poolw_ref,
                 win_o, wout_o, wuq_o, wukt_o, wuv_o, poolw_o):
    def put(dst, src, width):
        for c in range(0, width, PREP_COLS):
            n = min(PREP_COLS, width - c)
            win_o[:, dst + c:dst + c + n] = wint_ref[src + c:src + c + n, :].T.astype(_BF)

    put(_C_CQ, _OFFS[0], Q_LORA + KV_LORA)
    kr = wint_ref[_OFFS[2]:_OFFS[3], :]
    krb = jnp.concatenate([kr, -kr[AXIS_DIM:], kr[:AXIS_DIM]], axis=0)
    win_o[:, _C_KR:_C_GMLA] = krb.T.astype(_BF)
    put(_C_GMLA, _OFFS[3], MLA_WIDTH)
    put(_C_GPOOL, _OFFS[5], POOL_WIDTH)
    put(_C_PX, _OFFS[4], POOL_WIDTH)
    put(_C_CC, _OFFS[7], 2 * CONV_WIDTH)
    put(_C_CB, _OFFS[6], CONV_WIDTH)
    put(_C_CB + CONV_WIDTH, _OFFS[9], CONV_WIDTH)
    wout_o[...] = wout_ref[...].astype(_BF)

    for hd in range(MLA_HEADS):
        s0, d0 = hd * QK_DIM, hd * HEAD_PAD
        r0 = s0 + NOPE_DIM
        wuq_o[:, d0:d0 + QK_DIM] = wuq_ref[:, s0:s0 + QK_DIM].astype(_BF)
        wuq_o[:, d0 + QK_DIM:d0 + QK_DIM + AXIS_DIM] = (-wuq_ref[:, r0 + AXIS_DIM:r0 + ROPE_DIM]).astype(_BF)
        wuq_o[:, d0 + QK_DIM + AXIS_DIM:d0 + HEAD_PAD] = wuq_ref[:, r0:r0 + AXIS_DIM].astype(_BF)
    for hd in range(MLA_HEADS):
        k0 = hd * KV_HEAD
        wukt_o[hd * NOPE_DIM:(hd + 1) * NOPE_DIM, :] = wukv_ref[:, k0:k0 + NOPE_DIM].T.astype(_BF)
        wuv_o[:, hd * V_DIM:(hd + 1) * V_DIM] = wukv_ref[:, k0 + NOPE_DIM:k0 + KV_HEAD].astype(_BF)
    poolw_o[...] = jnp.zeros((POOL_WIDTH, POOL_WIDTH), _BF)
    for g in range(POOL_GROUPS):
        sl = slice(g * POOL_GROUP_DIM, (g + 1) * POOL_GROUP_DIM)
        poolw_o[sl, sl] = poolw_ref[g].astype(_BF)


def _prep_weights(w_in_t, w_out, w_uq, w_ukv, pool_w):
    per_layer = lambda *shape: pl.BlockSpec((None,) + shape, lambda l: (l,) + (0,) * len(shape))
    return pl.pallas_call(
        _prep_kernel,
        grid=(DEPTH,),
        in_specs=[
            per_layer(IN_WIDTH, D_MODEL),
            per_layer(MIX_WIDTH, D_MODEL),
            per_layer(Q_LORA, MLA_HEADS * QK_DIM),
            per_layer(KV_LORA, MLA_HEADS * KV_HEAD),
            per_layer(POOL_GROUPS, POOL_GROUP_DIM, POOL_GROUP_DIM),
        ],
        out_specs=[
            per_layer(D_MODEL, _C_END),
            per_layer(MIX_WIDTH, D_MODEL),
            per_layer(Q_LORA, MLA_HEADS * HEAD_PAD),
            per_layer(MLA_HEADS * NOPE_DIM, KV_LORA),
            per_layer(KV_LORA, MLA_WIDTH),
            per_layer(POOL_WIDTH, POOL_WIDTH),
        ],
        out_shape=[
            jax.ShapeDtypeStruct((DEPTH, D_MODEL, _C_END), _BF),
            jax.ShapeDtypeStruct((DEPTH, MIX_WIDTH, D_MODEL), _BF),
            jax.ShapeDtypeStruct((DEPTH, Q_LORA, MLA_HEADS * HEAD_PAD), _BF),
            jax.ShapeDtypeStruct((DEPTH, MLA_HEADS * NOPE_DIM, KV_LORA), _BF),
            jax.ShapeDtypeStruct((DEPTH, KV_LORA, MLA_WIDTH), _BF),
            jax.ShapeDtypeStruct((DEPTH, POOL_WIDTH, POOL_WIDTH), _BF),
        ],
        compiler_params=pltpu.CompilerParams(dimension_semantics=("arbitrary",),
                                             vmem_limit_bytes=VMEM_LIMIT_BYTES),
        name="weight_prep",
    )(w_in_t, w_out, w_uq, w_ukv, pool_w)


def _pool_mix(win, rcnt):
    n = win.shape[0]
    rows = n - 2 * HALO
    lane = lax.broadcasted_iota(jnp.int32, (1, POOL_WIDTH), 1)
    sums = [win + pltpu.roll(win, 1, axis=0)]
    for k in (1, 2, 4):
        sums.append(pltpu.roll(sums[-1], k, axis=0) + pltpu.roll(sums[-1], n - k, axis=0))
    sel = sums[-1]
    for g in range(POOL_GROUPS - 2, -1, -1):
        sel = jnp.where(lane < (g + 1) * POOL_GROUP_DIM, sums[g], sel)
    return sel[HALO:HALO + rows] * rcnt - win[HALO:HALO + rows]


def _short_conv(win, w):
    rows = win.shape[0] - 2 * HALO
    return (win[HALO - 1:HALO - 1 + rows] * w[0] + win[HALO:HALO + rows] * w[1]
            + win[HALO + 1:HALO + 1 + rows] * w[2])


def _pass_kernel(*refs, n_seq, seq_len, cache_len, has_rope, emit_state, rc, ordered):
    it = iter(refs)
    x_ref, mod_ref = next(it), next(it)
    if ordered:
        next(it)
    if cache_len:
        cckv_all_ref, ckr_all_ref = next(it), next(it)
    if has_rope:
        cs_ref = next(it)
    (rcnt_ref, gn_ref, win_all_ref, gq_ref, wuq_all_ref, gkv_ref, wukt_all_ref, wuv_all_ref, poolw_all_ref,
     pools_ref, convw_ref, wout_all_ref, gfin_ref) = (next(it) for _ in range(13))
    y_ref = next(it)
    if emit_state:
        ckv_out_ref, kr_out_ref = next(it), next(it)
    h_s, q_s, kt_s, v_s, px_s, prod_s, mixed_s = (next(it) for _ in range(7))

    m = n_seq * seq_len
    keys = cache_len + seq_len
    padded = seq_len + 2 * HALO
    lane = lax.broadcasted_iota(jnp.int32, (1, ROPE_PAD), 1)
    mrow = (1 + pl.program_id(0)) if has_rope else 0
    piece = min(seq_len, rc)
    n_sub = rc // piece
    pieces = [(s, slice(s * piece, (s + 1) * piece)) for s in range(n_sub)]
    chunks = range(m // rc)

    for c in chunks:
        y_ref[c * rc:(c + 1) * rc, :] = x_ref[c * rc:(c + 1) * rc, :]

    zeros_halo = jnp.zeros((HALO, POOL_WIDTH), _F32)
    for buf in (px_s, prod_s):
        for s in range(n_seq):
            buf[s * padded:s * padded + HALO, :] = zeros_halo
            buf[(s + 1) * padded - HALO:(s + 1) * padded, :] = zeros_halo
    ones_col = jnp.where(lax.broadcasted_iota(jnp.int32, (n_seq * keys, V_DIM), 1) == 0, 1.0, 0.0).astype(_BF)
    for hd in range(MLA_HEADS):
        v_s[hd, :, V_DIM:] = ones_col

    def layer_view(l):
        lw = dict(l=l, row=pl.ds(l, 1), win=win_all_ref.at[l], wuq=wuq_all_ref.at[l], wukt=wukt_all_ref.at[l],
                  wuv=wuv_all_ref.at[l], poolw=poolw_all_ref.at[l], wout=wout_all_ref.at[l])
        lw["shift"], lw["scale"], lw["gate"] = (mod_ref[l, k, pl.ds(mrow, 1), :] for k in range(3))
        if cache_len:
            lw["cckv"], lw["ckr"] = cckv_all_ref.at[l], ckr_all_ref.at[l]
        return lw

    def rotate(v, r):
        v = v * cs_ref[pl.ds(r, rc), :]
        return v + pltpu.roll(v, ROPE_DIM, axis=1)

    def put_kv(lw, ckv, kr_t, kc):
        knt = _dot(lw["wukt"][...], ckv.T.astype(_BF))
        v4 = _dot(ckv.astype(_BF), lw["wuv"][...])
        kr_bf = kr_t.astype(_BF)
        rows_k = pl.ds(kc * rc, rc)
        for hd in range(MLA_HEADS):
            kt_s[hd, kc, :NOPE_DIM, :] = knt[hd * NOPE_DIM:(hd + 1) * NOPE_DIM, :].astype(_BF)
            kt_s[hd, kc, NOPE_DIM:, :] = kr_bf
            v_s[hd, rows_k, :V_DIM] = v4[:, hd * V_DIM:(hd + 1) * V_DIM].astype(_BF)

    def cached_keys(lw):
        ckr_t = lw["ckr"][...]
        ckr_pad = jnp.concatenate([ckr_t, jnp.zeros_like(ckr_t)], axis=0)
        for cc in range(cache_len // rc):
            rows_k = slice(cc * rc, (cc + 1) * rc)
            put_kv(lw, lw["cckv"][rows_k, :], ckr_pad[:, rows_k], cc)

    def halo_base(c, s):
        return (c * n_sub + s) * padded if seq_len <= rc else c * rc

    def project(lw, c):
        r = c * rc
        rows = pl.ds(r, rc)
        row, win_ref = lw["row"], lw["win"]
        h = (_rms(y_ref[rows, :], gn_ref[row, :]) * (1.0 + lw["scale"]) + lw["shift"]).astype(_BF)
        h_s[rows, :] = h
        a = _dot(h, win_ref[:, _C_CQ:_C_GMLA])
        cq = _rms(a[:, :Q_LORA], gq_ref[row, :]).astype(_BF)
        ckv = _rms(a[:, _C_CKV:_C_KR], gkv_ref[row, :])
        kr = a[:, _C_KR:]
        if has_rope:
            kr = rotate(kr, r)
        kr_t = jnp.where(lane < ROPE_DIM, kr, 0.0).T
        if emit_state:
            for s, ps in pieces:
                ckv_out_ref[c * n_sub + s, lw["l"]] = ckv[ps, :]
                kr_out_ref[c * n_sub + s, lw["l"]] = kr_t[:ROPE_DIM, ps]
        put_kv(lw, ckv, kr_t, cache_len // rc + c)
        for hp in range(MLA_HEADS // 2):
            q2 = _dot(cq, lw["wuq"][:, hp * 2 * HEAD_PAD:(hp + 1) * 2 * HEAD_PAD]) * Q_SCALE
            for j in range(2):
                hd = 2 * hp + j
                c0 = j * HEAD_PAD
                q_s[hd, rows, :NOPE_DIM] = q2[:, c0:c0 + NOPE_DIM].astype(_BF)
                qr = q2[:, c0 + NOPE_DIM:c0 + HEAD_PAD]
                if has_rope:
                    qr = rotate(qr, r)
                q_s[hd, rows, NOPE_DIM:] = qr.astype(_BF)
        pe = _dot(h, win_ref[:, _C_PX:_C_CB])
        for s, ps in pieces:
            rows_h = pl.ds(halo_base(c, s) + HALO, piece)
            px_s[rows_h, :] = pe[ps, :POOL_WIDTH]
            prod_s[rows_h, :] = pe[ps, POOL_WIDTH:POOL_WIDTH + CONV_WIDTH] * pe[ps, POOL_WIDTH + CONV_WIDTH:]

    def mix(lw, c):
        r = c * rc
        rows = pl.ds(r, rc)
        row, win_ref = lw["row"], lw["win"]
        h = h_s[rows, :]
        for hp in range(MLA_HEADS // 2):
            g2 = _silu(_dot(h, win_ref[:, _C_GMLA + hp * 2 * V_DIM:_C_GMLA + (hp + 1) * 2 * V_DIM]))
            for j in range(2):
                hd = 2 * hp + j
                for s, ps in pieces:
                    rows_q = pl.ds(r + s * piece, piece)
                    if n_seq == 1:
                        ks = [kt_s[hd, kc] for kc in range(keys // rc)]
                        rows_k = pl.ds(0, keys)
                    else:
                        ks = [kt_s[hd, c, :, ps]]
                        rows_k = rows_q
                    q = q_s[hd, rows_q, :]
                    sc = jnp.concatenate([_dot(q, k) for k in ks], axis=-1)
                    p = jnp.exp2(sc - jnp.max(sc, axis=-1, keepdims=True))
                    ov = _dot(p.astype(_BF), v_s[hd, rows_k, :])
                    o = ov[:, :V_DIM] / ov[:, V_DIM:V_DIM + 1]
                    mixed_s[rows_q, hd * V_DIM:(hd + 1) * V_DIM] = (g2[ps, j * V_DIM:(j + 1) * V_DIM] * o).astype(_BF)

        windows = [pl.ds(halo_base(c, s), piece + 2 * HALO) for s, _ in pieces]
        rcnt = rcnt_ref[pl.ds(0 if seq_len <= rc else r, piece), :]
        pooled = jnp.concatenate([_pool_mix(px_s[w, :], rcnt) for w in windows], axis=0)
        pool = _dot(pooled.astype(_BF), lw["poolw"][...]) * pools_ref[row, :]
        gp = _silu(_dot(h, win_ref[:, _C_GPOOL:_C_PX]))
        mixed_s[rows, MLA_WIDTH:MLA_WIDTH + POOL_WIDTH] = (gp * pool).astype(_BF)

        convw = [convw_ref[k, row, :] for k in range(3)]
        conv = jnp.concatenate([_short_conv(prod_s[w, :], convw) for w in windows], axis=0)
        e2 = _dot(h, win_ref[:, _C_CB:_C_END])
        mixed_s[rows, MLA_WIDTH + POOL_WIDTH:] = (_silu(e2[:, CONV_WIDTH:]) * (e2[:, :CONV_WIDTH] * conv)).astype(_BF)

        y_ref[rows, :] = y_ref[rows, :] + lw["gate"] * _dot(mixed_s[rows, :], lw["wout"][...])

    def one_layer(l, carry):
        lw = layer_view(l)
        if cache_len:
            cached_keys(lw)
        for c in chunks:
            project(lw, c)
        for c in chunks:
            mix(lw, c)
        return carry

    lax.fori_loop(0, DEPTH, one_layer, 0)
    for c in chunks:
        rows = slice(c * rc, (c + 1) * rc)
        y_ref[rows, :] = _rms(y_ref[rows, :], gfin_ref[...])


def _mixer_pass(x2d, mod_all, wts, *, seq_len, cache=None, rope=None, emit_state, after=None):
    rows = x2d.shape[0]
    m = ROWS_PER_STEP
    n_seq = m // seq_len
    n_steps = rows // m
    n_all = rows // seq_len
    cache_len = 0 if cache is None else cache[0].shape[2]
    keys = cache_len + seq_len
    has_rope = rope is not None
    rc = ROW_CHUNK
    assert m % rc == 0 and cache_len % rc == 0
    assert (rc % seq_len == 0 and not cache_len) or (n_seq == 1 and seq_len % rc == 0 and not emit_state)

    def const(*shape):
        return pl.BlockSpec(shape, lambda i: (0,) * len(shape), pipeline_mode=pl.Buffered(1))

    args = [x2d, mod_all]
    in_specs = [pl.BlockSpec((m, D_MODEL), lambda i: (i, 0)), const(DEPTH, 3, MOD_ROWS, D_MODEL)]
    if after is not None:
        args += [after]
        in_specs += [pl.BlockSpec(memory_space=pl.ANY)]
    if cache_len:
        args += [cache[0], cache[1]]
        in_specs += [pl.BlockSpec((None, DEPTH, cache_len, KV_LORA), lambda i: (i, 0, 0, 0)),
                     pl.BlockSpec((None, DEPTH, ROPE_DIM, cache_len), lambda i: (i, 0, 0, 0))]
    if has_rope:
        args += [rope]
        in_specs += [const(seq_len, ROPE_PAD)]
    args += [_pool_rcnt(seq_len), wts["g_norm"], wts["w_in"], wts["g_q"], wts["w_uq"], wts["g_kv"],
             wts["w_ukt"], wts["w_uv"], wts["pool_w"], wts["pool_s"], wts["conv_w"], wts["w_out"],
             wts["g_final"]]
    in_specs += [const(seq_len, POOL_WIDTH), const(DEPTH, D_MODEL), const(DEPTH, D_MODEL, _C_END),
                 const(DEPTH, Q_LORA), const(DEPTH, Q_LORA, MLA_HEADS * HEAD_PAD), const(DEPTH, KV_LORA),
                 const(DEPTH, MLA_HEADS * NOPE_DIM, KV_LORA), const(DEPTH, KV_LORA, MLA_WIDTH),
                 const(DEPTH, POOL_WIDTH, POOL_WIDTH), const(DEPTH, POOL_WIDTH), const(3, DEPTH, CONV_WIDTH),
                 const(DEPTH, MIX_WIDTH, D_MODEL), const(1, D_MODEL)]

    out_shape = [jax.ShapeDtypeStruct((rows, D_MODEL), _F32)]
    out_specs = [pl.BlockSpec((m, D_MODEL), lambda i: (i, 0))]
    if emit_state:
        out_shape += [jax.ShapeDtypeStruct((n_all, DEPTH, seq_len, KV_LORA), _F32),
                      jax.ShapeDtypeStruct((n_all, DEPTH, ROPE_DIM, seq_len), _F32)]
        out_specs += [pl.BlockSpec((n_seq, DEPTH, seq_len, KV_LORA), lambda i: (i, 0, 0, 0)),
                      pl.BlockSpec((n_seq, DEPTH, ROPE_DIM, seq_len), lambda i: (i, 0, 0, 0))]

    kern = functools.partial(_pass_kernel, n_seq=n_seq, seq_len=seq_len, cache_len=cache_len,
                             has_rope=has_rope, emit_state=emit_state, rc=rc, ordered=after is not None)
    return pl.pallas_call(
        kern,
        grid=(n_steps,),
        in_specs=in_specs,
        out_specs=out_specs,
        out_shape=out_shape,
        scratch_shapes=[
            pltpu.VMEM((m, D_MODEL), _BF),
            pltpu.VMEM((MLA_HEADS, m, HEAD_PAD), _BF),
            pltpu.VMEM((MLA_HEADS, n_seq * keys // rc, HEAD_PAD, rc), _BF),
            pltpu.VMEM((MLA_HEADS, n_seq * keys, 2 * V_DIM), _BF),
            pltpu.VMEM((n_seq * (seq_len + 2 * HALO), POOL_WIDTH), _F32),
            pltpu.VMEM((n_seq * (seq_len + 2 * HALO), CONV_WIDTH), _F32),
            pltpu.VMEM((m, MIX_WIDTH), _BF),
        ],
        compiler_params=pltpu.CompilerParams(dimension_semantics=("arbitrary",),
                                             vmem_limit_bytes=VMEM_LIMIT_BYTES),
        name="mixer_pass_latent" if has_rope else "mixer_pass_context",
    )(*args)


def _pool_rcnt(seq_len):
    t = np.arange(seq_len)[:, None]
    half = np.repeat(1 << np.arange(POOL_GROUPS), POOL_GROUP_DIM)[None, :]
    cnt = np.minimum(t + half, seq_len) - np.maximum(t - half, 0)
    return jnp.asarray((1.0 / cnt).astype(np.float32))


def _rope_table(seq_len):
    rows = seq_len // GRID_W
    row = np.repeat(np.arange(rows), GRID_W).astype(np.float32)
    col = np.tile(np.arange(GRID_W), rows).astype(np.float32)
    inv = (1.0 / (np.float32(ROPE_BASE) ** (np.arange(0, AXIS_DIM, 2, dtype=np.float32) / np.float32(AXIS_DIM))))
    inv = inv.astype(np.float32)
    ang = np.concatenate([row[:, None] * inv, col[:, None] * inv], axis=-1).astype(np.float64)
    cos, sin = np.cos(ang).astype(np.float32), np.sin(ang).astype(np.float32)
    return jnp.asarray(np.concatenate([cos, cos, sin, sin], axis=-1))


def kernel(x_prompt, x_sample, cache_ckv, cache_krope, c, c_ctx, w_mod, b_mod, g_norm, w_in, g_q, w_uq,
           g_kv, w_ukv, pool_w, pool_s, conv_w, w_out, g_final):
    batch, seq, _ = x_prompt.shape
    dec_batch, dec_seq, _ = x_sample.shape
    assert 1 + dec_batch <= MOD_ROWS and dec_seq == ROWS_PER_STEP and ROWS_PER_STEP % seq == 0

    c_all = jnp.concatenate([c_ctx[None], c, jnp.zeros((MOD_ROWS - 1 - dec_batch, D_MODEL), _F32)], axis=0)
    mod_all = _modulation(c_all, w_mod, b_mod)
    w_in_r, w_out_r, w_uq_r, w_ukt, w_uv, pool_w_r = _prep_weights(
        jnp.transpose(w_in, (0, 2, 1)), w_out, w_uq, w_ukv, pool_w)
    wts = {
        "g_norm": g_norm, "w_in": w_in_r, "g_q": g_q, "w_uq": w_uq_r, "g_kv": g_kv, "w_ukt": w_ukt,
        "w_uv": w_uv, "pool_w": pool_w_r, "pool_s": pool_s, "conv_w": jnp.transpose(conv_w, (1, 0, 2)),
        "w_out": w_out_r, "g_final": g_final.reshape(1, D_MODEL),
    }

    rope = _rope_table(dec_seq)
    cache = (cache_ckv, jnp.transpose(cache_krope, (0, 1, 3, 2)))
    xp = x_prompt.reshape(batch * seq, D_MODEL)
    xs = x_sample.reshape(dec_batch * dec_seq, D_MODEL)
    (xs,) = _mixer_pass(xs, mod_all, wts, seq_len=dec_seq, cache=cache, rope=rope, emit_state=False)
    xp, state_ckv, state_krope_t = _mixer_pass(xp, mod_all, wts, seq_len=seq, emit_state=True, after=xs)
    y_prompt = xp.reshape(batch, seq, D_MODEL)
    state_krope = jnp.transpose(state_krope_t, (0, 1, 3, 2))
    y_sample = xs.reshape(dec_batch, dec_seq, D_MODEL)
    return (y_prompt, y_sample, state_ckv, state_krope)
```

```python
import functools

import numpy as np
import jax
import jax.numpy as jnp
from jax import lax
from jax.experimental import pallas as pl
from jax.experimental.pallas import tpu as pltpu

D_MODEL = 1024
DEPTH = 2
GRID_W = 64
MLA_HEADS = 4
NOPE_DIM = 128
ROPE_DIM = 64
V_DIM = 128
QK_DIM = NOPE_DIM + ROPE_DIM
MLA_WIDTH = MLA_HEADS * V_DIM
Q_LORA = 384
KV_LORA = 256
POOL_GROUPS = 4
POOL_GROUP_DIM = 64
POOL_WIDTH = POOL_GROUPS * POOL_GROUP_DIM
CONV_WIDTH = 256
MIX_WIDTH = MLA_WIDTH + POOL_WIDTH + CONV_WIDTH
ROPE_BASE = 10000.0
AXIS_DIM = ROPE_DIM // 2
ATTN_SCALE = QK_DIM ** -0.5
Q_SCALE = ATTN_SCALE * float(np.log2(np.e))
EPS = 1e-6

_SPLITS = (Q_LORA, KV_LORA, ROPE_DIM, MLA_WIDTH, POOL_WIDTH, POOL_WIDTH,
           CONV_WIDTH, CONV_WIDTH, CONV_WIDTH, CONV_WIDTH)
_OFFS = [sum(_SPLITS[:i]) for i in range(len(_SPLITS) + 1)]
IN_WIDTH = _OFFS[-1]

V7X_SUBLANES = 8
V7X_MXU_DEPTH = 256
V7X_VMEM_BYTES = 64 * 1024 * 1024
HEAD_PAD = V7X_MXU_DEPTH
ROPE_PAD = HEAD_PAD - NOPE_DIM
KV_HEAD = NOPE_DIM + V_DIM

_C_CQ = 0
_C_CKV = _C_CQ + Q_LORA
_C_KR = _C_CKV + KV_LORA
_C_GMLA = _C_KR + ROPE_PAD
_C_GPOOL = _C_GMLA + MLA_WIDTH
_C_PX = _C_GPOOL + POOL_WIDTH
_C_CC = _C_PX + POOL_WIDTH
_C_CB = _C_CC + 2 * CONV_WIDTH
_C_END = _C_CB + 2 * CONV_WIDTH

MOD_ROWS = 8
ROWS_PER_STEP = 1024
ROW_CHUNK = 512
HALO = V7X_SUBLANES
PREP_COLS = 256
VMEM_LIMIT_BYTES = V7X_VMEM_BYTES - 4 * 1024 * 1024

_WEIGHTS = ("w_in", "w_uq", "w_ukt", "w_uv", "pool_w", "w_out")

_BF = jnp.bfloat16
_F32 = jnp.float32


def _dot(a, b):
    return jnp.dot(a, b, preferred_element_type=_F32)


def _rms(x, g):
    return x * lax.rsqrt(jnp.mean(x * x, axis=-1, keepdims=True) + EPS) * g


def _silu(x):
    return x * jax.nn.sigmoid(x)


def _mod_kernel(c_ref, w_ref, b_ref, o_ref):
    s = _silu(c_ref[...]).astype(_BF)
    for k in range(3):
        cols = slice(k * D_MODEL, (k + 1) * D_MODEL)
        bias = b_ref[pl.ds(pl.program_id(0), 1), cols]
        o_ref[k] = _dot(s, w_ref[:, cols].astype(_BF)) + bias


def _modulation(c_all, w_mod, b_mod):
    return pl.pallas_call(
        _mod_kernel,
        grid=(DEPTH,),
        in_specs=[
            pl.BlockSpec((MOD_ROWS, D_MODEL), lambda l: (0, 0)),
            pl.BlockSpec((None, D_MODEL, 3 * D_MODEL), lambda l: (l, 0, 0)),
            pl.BlockSpec((DEPTH, 3 * D_MODEL), lambda l: (0, 0)),
        ],
        out_specs=pl.BlockSpec((None, 3, MOD_ROWS, D_MODEL), lambda l: (l, 0, 0, 0)),
        out_shape=jax.ShapeDtypeStruct((DEPTH, 3, MOD_ROWS, D_MODEL), _F32),
        compiler_params=pltpu.CompilerParams(dimension_semantics=("arbitrary",),
                                             vmem_limit_bytes=VMEM_LIMIT_BYTES),
        name="modulation",
    )(c_all, w_mod, b_mod)


def _prep_kernel(wint_ref, wout_ref, wuq_ref, wukv_ref, poolw_ref,
                 win_o, wout_o, wuq_o, wukt_o, wuv_o, poolw_o):
    def put(dst, src, width):
        for c in range(0, width, PREP_COLS):
            n = min(PREP_COLS, width - c)
            win_o[:, dst + c:dst + c + n] = wint_ref[src + c:src + c + n, :].T.astype(_BF)

    put(_C_CQ, _OFFS[0], Q_LORA + KV_LORA)
    kr = wint_ref[_OFFS[2]:_OFFS[3], :]
    krb = jnp.concatenate([kr, -kr[AXIS_DIM:], kr[:AXIS_DIM]], axis=0)
    win_o[:, _C_KR:_C_GMLA] = krb.T.astype(_BF)
    put(_C_GMLA, _OFFS[3], MLA_WIDTH)
    put(_C_GPOOL, _OFFS[5], POOL_WIDTH)
    put(_C_PX, _OFFS[4], POOL_WIDTH)
    put(_C_CC, _OFFS[7], 2 * CONV_WIDTH)
    put(_C_CB, _OFFS[6], CONV_WIDTH)
    put(_C_CB + CONV_WIDTH, _OFFS[9], CONV_WIDTH)
    wout_o[...] = wout_ref[...].astype(_BF)

    for hd in range(MLA_HEADS):
        s0, d0 = hd * QK_DIM, hd * HEAD_PAD
        r0 = s0 + NOPE_DIM
        wuq_o[:, d0:d0 + QK_DIM] = wuq_ref[:, s0:s0 + QK_DIM].astype(_BF)
        wuq_o[:, d0 + QK_DIM:d0 + QK_DIM + AXIS_DIM] = (-wuq_ref[:, r0 + AXIS_DIM:r0 + ROPE_DIM]).astype(_BF)
        wuq_o[:, d0 + QK_DIM + AXIS_DIM:d0 + HEAD_PAD] = wuq_ref[:, r0:r0 + AXIS_DIM].astype(_BF)
    for hd in range(MLA_HEADS):
        k0 = hd * KV_HEAD
        wukt_o[hd * NOPE_DIM:(hd + 1) * NOPE_DIM, :] = wukv_ref[:, k0:k0 + NOPE_DIM].T.astype(_BF)
        wuv_o[:, hd * V_DIM:(hd + 1) * V_DIM] = wukv_ref[:, k0 + NOPE_DIM:k0 + KV_HEAD].astype(_BF)
    poolw_o[...] = jnp.zeros((POOL_WIDTH, POOL_WIDTH), _BF)
    for g in range(POOL_GROUPS):
        sl = slice(g * POOL_GROUP_DIM, (g + 1) * POOL_GROUP_DIM)
        poolw_o[sl, sl] = poolw_ref[g].astype(_BF)


def _prep_weights(w_in_t, w_out, w_uq, w_ukv, pool_w):
    per_layer = lambda *shape: pl.BlockSpec((None,) + shape, lambda l: (l,) + (0,) * len(shape))
    return pl.pallas_call(
        _prep_kernel,
        grid=(DEPTH,),
        in_specs=[
            per_layer(IN_WIDTH, D_MODEL),
            per_layer(MIX_WIDTH, D_MODEL),
            per_layer(Q_LORA, MLA_HEADS * QK_DIM),
            per_layer(KV_LORA, MLA_HEADS * KV_HEAD),
            per_layer(POOL_GROUPS, POOL_GROUP_DIM, POOL_GROUP_DIM),
        ],
        out_specs=[
            per_layer(D_MODEL, _C_END),
            per_layer(MIX_WIDTH, D_MODEL),
            per_layer(Q_LORA, MLA_HEADS * HEAD_PAD),
            per_layer(MLA_HEADS * NOPE_DIM, KV_LORA),
            per_layer(KV_LORA, MLA_WIDTH),
            per_layer(POOL_WIDTH, POOL_WIDTH),
        ],
        out_shape=[
            jax.ShapeDtypeStruct((DEPTH, D_MODEL, _C_END), _BF),
            jax.ShapeDtypeStruct((DEPTH, MIX_WIDTH, D_MODEL), _BF),
            jax.ShapeDtypeStruct((DEPTH, Q_LORA, MLA_HEADS * HEAD_PAD), _BF),
            jax.ShapeDtypeStruct((DEPTH, MLA_HEADS * NOPE_DIM, KV_LORA), _BF),
            jax.ShapeDtypeStruct((DEPTH, KV_LORA, MLA_WIDTH), _BF),
            jax.ShapeDtypeStruct((DEPTH, POOL_WIDTH, POOL_WIDTH), _BF),
        ],
        compiler_params=pltpu.CompilerParams(dimension_semantics=("arbitrary",),
                                             vmem_limit_bytes=VMEM_LIMIT_BYTES),
        name="weight_prep",
    )(w_in_t, w_out, w_uq, w_ukv, pool_w)


def _pool_mix(win, rcnt):
    n = win.shape[0]
    rows = n - 2 * HALO
    lane = lax.broadcasted_iota(jnp.int32, (1, POOL_WIDTH), 1)
    sums = [win + pltpu.roll(win, 1, axis=0)]
    for k in (1, 2, 4):
        sums.append(pltpu.roll(sums[-1], k, axis=0) + pltpu.roll(sums[-1], n - k, axis=0))
    sel = sums[-1]
    for g in range(POOL_GROUPS - 2, -1, -1):
        sel = jnp.where(lane < (g + 1) * POOL_GROUP_DIM, sums[g], sel)
    return sel[HALO:HALO + rows] * rcnt - win[HALO:HALO + rows]


def _short_conv(win, w):
    rows = win.shape[0] - 2 * HALO
    return (win[HALO - 1:HALO - 1 + rows] * w[0] + win[HALO:HALO + rows] * w[1]
            + win[HALO + 1:HALO + 1 + rows] * w[2])


def _pass_kernel(*refs, n_seq, seq_len, cache_len, has_rope, emit_state, rc):
    it = iter(refs)
    x_ref, mod_ref = next(it), next(it)
    if cache_len:
        cckv_all_ref, ckr_all_ref = next(it), next(it)
    if has_rope:
        cs_ref = next(it)
    rcnt_ref, gn_ref, gq_ref, gkv_ref, pools_ref, convw_ref, gfin_ref = (next(it) for _ in range(7))
    w_hbm = [next(it) for _ in range(len(_WEIGHTS))]
    y_ref = next(it)
    if emit_state:
        ckv_out_ref, kr_out_ref = next(it), next(it)
    h_s, q_s, kt_s, v_s, px_s, prod_s, mixed_s = (next(it) for _ in range(7))
    w_vm = [next(it) for _ in range(len(_WEIGHTS))]
    w_sem = next(it)
    win_all_ref, wuq_all_ref, wukt_all_ref, wuv_all_ref, poolw_all_ref, wout_all_ref = w_vm

    def weight_copy(l, k):
        return pltpu.make_async_copy(w_hbm[k].at[l], w_vm[k].at[l], w_sem.at[l, k])

    first_step = pl.program_id(0) == 0

    @pl.when(first_step)
    def _():
        for l in range(DEPTH):
            for k in range(len(_WEIGHTS)):
                weight_copy(l, k).start()

    m = n_seq * seq_len
    keys = cache_len + seq_len
    padded = seq_len + 2 * HALO
    lane = lax.broadcasted_iota(jnp.int32, (1, ROPE_PAD), 1)
    mrow = (1 + pl.program_id(0)) if has_rope else 0
    piece = min(seq_len, rc)
    n_sub = rc // piece
    pieces = [(s, slice(s * piece, (s + 1) * piece)) for s in range(n_sub)]
    chunks = range(m // rc)

    for c in chunks:
        y_ref[c * rc:(c + 1) * rc, :] = x_ref[c * rc:(c + 1) * rc, :]

    zeros_halo = jnp.zeros((HALO, POOL_WIDTH), _F32)
    for buf in (px_s, prod_s):
        for s in range(n_seq):
            buf[s * padded:s * padded + HALO, :] = zeros_halo
            buf[(s + 1) * padded - HALO:(s + 1) * padded, :] = zeros_halo
    ones_col = jnp.where(lax.broadcasted_iota(jnp.int32, (n_seq * keys, V_DIM), 1) == 0, 1.0, 0.0).astype(_BF)
    for hd in range(MLA_HEADS):
        v_s[hd, :, V_DIM:] = ones_col

    def layer_view(l):
        lw = dict(l=l, row=pl.ds(l, 1), win=win_all_ref.at[l], wuq=wuq_all_ref.at[l], wukt=wukt_all_ref.at[l],
                  wuv=wuv_all_ref.at[l], poolw=poolw_all_ref.at[l], wout=wout_all_ref.at[l])
        lw["shift"], lw["scale"], lw["gate"] = (mod_ref[l, k, pl.ds(mrow, 1), :] for k in range(3))
        if cache_len:
            lw["cckv"], lw["ckr"] = cckv_all_ref.at[l], ckr_all_ref.at[l]
        return lw

    def rotate(v, r):
        v = v * cs_ref[pl.ds(r, rc), :]
        return v + pltpu.roll(v, ROPE_DIM, axis=1)

    def put_kv(lw, ckv, kr_t, kc):
        knt = _dot(lw["wukt"][...], ckv.T.astype(_BF))
        v4 = _dot(ckv.astype(_BF), lw["wuv"][...])
        kr_bf = kr_t.astype(_BF)
        rows_k = pl.ds(kc * rc, rc)
        for hd in range(MLA_HEADS):
            kt_s[hd, kc, :NOPE_DIM, :] = knt[hd * NOPE_DIM:(hd + 1) * NOPE_DIM, :].astype(_BF)
            kt_s[hd, kc, NOPE_DIM:, :] = kr_bf
            v_s[hd, rows_k, :V_DIM] = v4[:, hd * V_DIM:(hd + 1) * V_DIM].astype(_BF)

    def cached_keys(lw):
        ckr_t = lw["ckr"][...]
        ckr_pad = jnp.concatenate([ckr_t, jnp.zeros_like(ckr_t)], axis=0)
        for cc in range(cache_len // rc):
            rows_k = slice(cc * rc, (cc + 1) * rc)
            put_kv(lw, lw["cckv"][rows_k, :], ckr_pad[:, rows_k], cc)

    def halo_base(c, s):
        return (c * n_sub + s) * padded if seq_len <= rc else c * rc

    def project(lw, c):
        r = c * rc
        rows = pl.ds(r, rc)
        row, win_ref = lw["row"], lw["win"]
        h = (_rms(y_ref[rows, :], gn_ref[row, :]) * (1.0 + lw["scale"]) + lw["shift"]).astype(_BF)
        h_s[rows, :] = h
        a = _dot(h, win_ref[:, _C_CQ:_C_GMLA])
        cq = _rms(a[:, :Q_LORA], gq_ref[row, :]).astype(_BF)
        ckv = _rms(a[:, _C_CKV:_C_KR], gkv_ref[row, :])
        kr = a[:, _C_KR:]
        if has_rope:
            kr = rotate(kr, r)
        kr_t = jnp.where(lane < ROPE_DIM, kr, 0.0).T
        if emit_state:
            for s, ps in pieces:
                ckv_out_ref[c * n_sub + s, lw["l"]] = ckv[ps, :]
                kr_out_ref[c * n_sub + s, lw["l"]] = kr_t[:ROPE_DIM, ps]
        put_kv(lw, ckv, kr_t, cache_len // rc + c)
        for hp in range(MLA_HEADS // 2):
            q2 = _dot(cq, lw["wuq"][:, hp * 2 * HEAD_PAD:(hp + 1) * 2 * HEAD_PAD]) * Q_SCALE
            for j in range(2):
                hd = 2 * hp + j
                c0 = j * HEAD_PAD
                q_s[hd, rows, :NOPE_DIM] = q2[:, c0:c0 + NOPE_DIM].astype(_BF)
                qr = q2[:, c0 + NOPE_DIM:c0 + HEAD_PAD]
                if has_rope:
                    qr = rotate(qr, r)
                q_s[hd, rows, NOPE_DIM:] = qr.astype(_BF)
        pe = _dot(h, win_ref[:, _C_PX:_C_CB])
        for s, ps in pieces:
            rows_h = pl.ds(halo_base(c, s) + HALO, piece)
            px_s[rows_h, :] = pe[ps, :POOL_WIDTH]
            prod_s[rows_h, :] = pe[ps, POOL_WIDTH:POOL_WIDTH + CONV_WIDTH] * pe[ps, POOL_WIDTH + CONV_WIDTH:]

    def mix(lw, c):
        r = c * rc
        rows = pl.ds(r, rc)
        row, win_ref = lw["row"], lw["win"]
        h = h_s[rows, :]
        for hp in range(MLA_HEADS // 2):
            g2 = _silu(_dot(h, win_ref[:, _C_GMLA + hp * 2 * V_DIM:_C_GMLA + (hp + 1) * 2 * V_DIM]))
            for j in range(2):
                hd = 2 * hp + j
                for s, ps in pieces:
                    rows_q = pl.ds(r + s * piece, piece)
                    if n_seq == 1:
                        ks = [kt_s[hd, kc] for kc in range(keys // rc)]
                        rows_k = pl.ds(0, keys)
                    else:
                        ks = [kt_s[hd, c, :, ps]]
                        rows_k = rows_q
                    q = q_s[hd, rows_q, :]
                    sc = jnp.concatenate([_dot(q, k) for k in ks], axis=-1)
                    p = jnp.exp2(sc - jnp.max(sc, axis=-1, keepdims=True))
                    ov = _dot(p.astype(_BF), v_s[hd, rows_k, :])
                    o = ov[:, :V_DIM] / ov[:, V_DIM:V_DIM + 1]
                    mixed_s[rows_q, hd * V_DIM:(hd + 1) * V_DIM] = (g2[ps, j * V_DIM:(j + 1) * V_DIM] * o).astype(_BF)

        windows = [pl.ds(halo_base(c, s), piece + 2 * HALO) for s, _ in pieces]
        rcnt = rcnt_ref[pl.ds(0 if seq_len <= rc else r, piece), :]
        pooled = jnp.concatenate([_pool_mix(px_s[w, :], rcnt) for w in windows], axis=0)
        pool = _dot(pooled.astype(_BF), lw["poolw"][...]) * pools_ref[row, :]
        gp = _silu(_dot(h, win_ref[:, _C_GPOOL:_C_PX]))
        mixed_s[rows, MLA_WIDTH:MLA_WIDTH + POOL_WIDTH] = (gp * pool).astype(_BF)

        convw = [convw_ref[k, row, :] for k in range(3)]
        conv = jnp.concatenate([_short_conv(prod_s[w, :], convw) for w in windows], axis=0)
        e2 = _dot(h, win_ref[:, _C_CB:_C_END])
        mixed_s[rows, MLA_WIDTH + POOL_WIDTH:] = (_silu(e2[:, CONV_WIDTH:]) * (e2[:, :CONV_WIDTH] * conv)).astype(_BF)

        y_ref[rows, :] = y_ref[rows, :] + lw["gate"] * _dot(mixed_s[rows, :], lw["wout"][...])

    def one_layer(l, carry):
        @pl.when(first_step)
        def _():
            for k in range(len(_WEIGHTS)):
                weight_copy(l, k).wait()

        lw = layer_view(l)
        if cache_len:
            cached_keys(lw)
        for c in chunks:
            project(lw, c)
        for c in chunks:
            mix(lw, c)
        return carry

    lax.fori_loop(0, DEPTH, one_layer, 0)
    for c in chunks:
        rows = slice(c * rc, (c + 1) * rc)
        y_ref[rows, :] = _rms(y_ref[rows, :], gfin_ref[...])


def _mixer_pass(x2d, mod_all, wts, *, seq_len, cache=None, rope=None, emit_state):
    rows = x2d.shape[0]
    m = ROWS_PER_STEP
    n_seq = m // seq_len
    n_steps = rows // m
    n_all = rows // seq_len
    cache_len = 0 if cache is None else cache[0].shape[2]
    keys = cache_len + seq_len
    has_rope = rope is not None
    rc = ROW_CHUNK
    assert m % rc == 0 and cache_len % rc == 0
    assert (rc % seq_len == 0 and not cache_len) or (n_seq == 1 and seq_len % rc == 0 and not emit_state)

    def const(*shape):
        return pl.BlockSpec(shape, lambda i: (0,) * len(shape), pipeline_mode=pl.Buffered(1))

    args = [x2d, mod_all]
    in_specs = [pl.BlockSpec((m, D_MODEL), lambda i: (i, 0)), const(DEPTH, 3, MOD_ROWS, D_MODEL)]
    if cache_len:
        args += [cache[0], cache[1]]
        in_specs += [pl.BlockSpec((None, DEPTH, cache_len, KV_LORA), lambda i: (i, 0, 0, 0)),
                     pl.BlockSpec((None, DEPTH, ROPE_DIM, cache_len), lambda i: (i, 0, 0, 0))]
    if has_rope:
        args += [rope]
        in_specs += [const(seq_len, ROPE_PAD)]
    args += [_pool_rcnt(seq_len), wts["g_norm"], wts["g_q"], wts["g_kv"], wts["pool_s"], wts["conv_w"],
             wts["g_final"]]
    in_specs += [const(seq_len, POOL_WIDTH), const(DEPTH, D_MODEL), const(DEPTH, Q_LORA), const(DEPTH, KV_LORA),
                 const(DEPTH, POOL_WIDTH), const(3, DEPTH, CONV_WIDTH), const(1, D_MODEL)]
    weights = [wts[name] for name in _WEIGHTS]
    args += weights
    in_specs += [pl.BlockSpec(memory_space=pl.ANY)] * len(weights)

    out_shape = [jax.ShapeDtypeStruct((rows, D_MODEL), _F32)]
    out_specs = [pl.BlockSpec((m, D_MODEL), lambda i: (i, 0))]
    if emit_state:
        out_shape += [jax.ShapeDtypeStruct((n_all, DEPTH, seq_len, KV_LORA), _F32),
                      jax.ShapeDtypeStruct((n_all, DEPTH, ROPE_DIM, seq_len), _F32)]
        out_specs += [pl.BlockSpec((n_seq, DEPTH, seq_len, KV_LORA), lambda i: (i, 0, 0, 0)),
                      pl.BlockSpec((n_seq, DEPTH, ROPE_DIM, seq_len), lambda i: (i, 0, 0, 0))]

    kern = functools.partial(_pass_kernel, n_seq=n_seq, seq_len=seq_len, cache_len=cache_len,
                             has_rope=has_rope, emit_state=emit_state, rc=rc)
    return pl.pallas_call(
        kern,
        grid=(n_steps,),
        in_specs=in_specs,
        out_specs=out_specs,
        out_shape=out_shape,
        scratch_shapes=[
            pltpu.VMEM((m, D_MODEL), _BF),
            pltpu.VMEM((MLA_HEADS, m, HEAD_PAD), _BF),
            pltpu.VMEM((MLA_HEADS, n_seq * keys // rc, HEAD_PAD, rc), _BF),
            pltpu.VMEM((MLA_HEADS, n_seq * keys, 2 * V_DIM), _BF),
            pltpu.VMEM((n_seq * (seq_len + 2 * HALO), POOL_WIDTH), _F32),
            pltpu.VMEM((n_seq * (seq_len + 2 * HALO), CONV_WIDTH), _F32),
            pltpu.VMEM((m, MIX_WIDTH), _BF),
        ] + [pltpu.VMEM(w.shape, w.dtype) for w in weights] + [pltpu.SemaphoreType.DMA((DEPTH, len(weights)))],
        compiler_params=pltpu.CompilerParams(dimension_semantics=("arbitrary",),
                                             vmem_limit_bytes=VMEM_LIMIT_BYTES),
        name="mixer_pass_latent" if has_rope else "mixer_pass_context",
    )(*args)


def _pool_rcnt(seq_len):
    t = np.arange(seq_len)[:, None]
    half = np.repeat(1 << np.arange(POOL_GROUPS), POOL_GROUP_DIM)[None, :]
    cnt = np.minimum(t + half, seq_len) - np.maximum(t - half, 0)
    return jnp.asarray((1.0 / cnt).astype(np.float32))


def _rope_table(seq_len):
    rows = seq_len // GRID_W
    row = np.repeat(np.arange(rows), GRID_W).astype(np.float32)
    col = np.tile(np.arange(GRID_W), rows).astype(np.float32)
    inv = (1.0 / (np.float32(ROPE_BASE) ** (np.arange(0, AXIS_DIM, 2, dtype=np.float32) / np.float32(AXIS_DIM))))
    inv = inv.astype(np.float32)
    ang = np.concatenate([row[:, None] * inv, col[:, None] * inv], axis=-1).astype(np.float64)
    cos, sin = np.cos(ang).astype(np.float32), np.sin(ang).astype(np.float32)
    return jnp.asarray(np.concatenate([cos, cos, sin, sin], axis=-1))


def kernel(x_prompt, x_sample, cache_ckv, cache_krope, c, c_ctx, w_mod, b_mod, g_norm, w_in, g_q, w_uq,
           g_kv, w_ukv, pool_w, pool_s, conv_w, w_out, g_final):
    batch, seq, _ = x_prompt.shape
    dec_batch, dec_seq, _ = x_sample.shape
    assert 1 + dec_batch <= MOD_ROWS and dec_seq == ROWS_PER_STEP and ROWS_PER_STEP % seq == 0

    c_all = jnp.concatenate([c_ctx[None], c, jnp.zeros((MOD_ROWS - 1 - dec_batch, D_MODEL), _F32)], axis=0)
    mod_all = _modulation(c_all, w_mod, b_mod)
    w_in_r, w_out_r, w_uq_r, w_ukt, w_uv, pool_w_r = _prep_weights(
        jnp.transpose(w_in, (0, 2, 1)), w_out, w_uq, w_ukv, pool_w)
    wts = {
        "g_norm": g_norm, "w_in": w_in_r, "g_q": g_q, "w_uq": w_uq_r, "g_kv": g_kv, "w_ukt": w_ukt,
        "w_uv": w_uv, "pool_w": pool_w_r, "pool_s": pool_s, "conv_w": jnp.transpose(conv_w, (1, 0, 2)),
        "w_out": w_out_r, "g_final": g_final.reshape(1, D_MODEL),
    }

    rope = _rope_table(dec_seq)
    cache = (cache_ckv, jnp.transpose(cache_krope, (0, 1, 3, 2)))
    xp = x_prompt.reshape(batch * seq, D_MODEL)
    xs = x_sample.reshape(dec_batch * dec_seq, D_MODEL)
    xp, state_ckv, state_krope_t = _mixer_pass(xp, mod_all, wts, seq_len=seq, emit_state=True)
    (xs,) = _mixer_pass(xs, mod_all, wts, seq_len=dec_seq, cache=cache, rope=rope, emit_state=False)
    y_prompt = xp.reshape(batch, seq, D_MODEL)
    state_krope = jnp.transpose(state_krope_t, (0, 1, 3, 2))
    y_sample = xs.reshape(dec_batch, dec_seq, D_MODEL)
    return (y_prompt, y_sample, state_ckv, state_krope)
```

```python
import functools

import numpy as np
import jax
import jax.numpy as jnp
from jax import lax
from jax.experimental import pallas as pl
from jax.experimental.pallas import tpu as pltpu

D_MODEL = 1024
DEPTH = 2
GRID_W = 64
MLA_HEADS = 4
NOPE_DIM = 128
ROPE_DIM = 64
V_DIM = 128
QK_DIM = NOPE_DIM + ROPE_DIM
MLA_WIDTH = MLA_HEADS * V_DIM
Q_LORA = 384
KV_LORA = 256
POOL_GROUPS = 4
POOL_GROUP_DIM = 64
POOL_WIDTH = POOL_GROUPS * POOL_GROUP_DIM
CONV_WIDTH = 256
MIX_WIDTH = MLA_WIDTH + POOL_WIDTH + CONV_WIDTH
ROPE_BASE = 10000.0
AXIS_DIM = ROPE_DIM // 2
ATTN_SCALE = QK_DIM ** -0.5
Q_SCALE = ATTN_SCALE * float(np.log2(np.e))
EPS = 1e-6

_SPLITS = (Q_LORA, KV_LORA, ROPE_DIM, MLA_WIDTH, POOL_WIDTH, POOL_WIDTH,
           CONV_WIDTH, CONV_WIDTH, CONV_WIDTH, CONV_WIDTH)
_OFFS = [sum(_SPLITS[:i]) for i in range(len(_SPLITS) + 1)]
IN_WIDTH = _OFFS[-1]

V7X_SUBLANES = 8
V7X_MXU_DEPTH = 256
V7X_VMEM_BYTES = 64 * 1024 * 1024
HEAD_PAD = V7X_MXU_DEPTH
ROPE_PAD = HEAD_PAD - NOPE_DIM
KV_HEAD = NOPE_DIM + V_DIM

_C_CQ = 0
_C_CKV = _C_CQ + Q_LORA
_C_KR = _C_CKV + KV_LORA
_C_GMLA = _C_KR + ROPE_PAD
_C_GPOOL = _C_GMLA + MLA_WIDTH
_C_PX = _C_GPOOL + POOL_WIDTH
_C_CC = _C_PX + POOL_WIDTH
_C_CB = _C_CC + 2 * CONV_WIDTH
_C_END = _C_CB + 2 * CONV_WIDTH

MOD_ROWS = 8
ROWS_PER_STEP = 1024
ROW_CHUNK = 512
HALO = V7X_SUBLANES
PREP_COLS = 256
VMEM_LIMIT_BYTES = V7X_VMEM_BYTES - 4 * 1024 * 1024

_WEIGHTS = ("w_in", "w_uq", "w_ukt", "w_uv", "pool_w", "w_out")

_BF = jnp.bfloat16
_F32 = jnp.float32


def _dot(a, b):
    return jnp.dot(a, b, preferred_element_type=_F32)


def _rms(x, g):
    return x * lax.rsqrt(jnp.mean(x * x, axis=-1, keepdims=True) + EPS) * g


def _silu(x):
    return x * jax.nn.sigmoid(x)


def _hint(v, k):
    return v if isinstance(v, int) else pl.multiple_of(v, k)


def _mod_kernel(c_ref, w_ref, b_ref, o_ref):
    s = _silu(c_ref[...]).astype(_BF)
    for k in range(3):
        cols = slice(k * D_MODEL, (k + 1) * D_MODEL)
        bias = b_ref[pl.ds(pl.program_id(0), 1), cols]
        o_ref[k] = _dot(s, w_ref[:, cols].astype(_BF)) + bias


def _modulation(c_all, w_mod, b_mod):
    return pl.pallas_call(
        _mod_kernel,
        grid=(DEPTH,),
        in_specs=[
            pl.BlockSpec((MOD_ROWS, D_MODEL), lambda l: (0, 0)),
            pl.BlockSpec((None, D_MODEL, 3 * D_MODEL), lambda l: (l, 0, 0)),
            pl.BlockSpec((DEPTH, 3 * D_MODEL), lambda l: (0, 0)),
        ],
        out_specs=pl.BlockSpec((None, 3, MOD_ROWS, D_MODEL), lambda l: (l, 0, 0, 0)),
        out_shape=jax.ShapeDtypeStruct((DEPTH, 3, MOD_ROWS, D_MODEL), _F32),
        compiler_params=pltpu.CompilerParams(dimension_semantics=("arbitrary",),
                                             vmem_limit_bytes=VMEM_LIMIT_BYTES),
        name="modulation",
    )(c_all, w_mod, b_mod)


def _prep_kernel(wint_ref, wout_ref, wuq_ref, wukv_ref, poolw_ref,
                 win_o, wout_o, wuq_o, wukt_o, wuv_o, poolw_o):
    def put(dst, src, width):
        for c in range(0, width, PREP_COLS):
            n = min(PREP_COLS, width - c)
            win_o[:, dst + c:dst + c + n] = wint_ref[src + c:src + c + n, :].T.astype(_BF)

    put(_C_CQ, _OFFS[0], Q_LORA + KV_LORA)
    kr = wint_ref[_OFFS[2]:_OFFS[3], :]
    krb = jnp.concatenate([kr, -kr[AXIS_DIM:], kr[:AXIS_DIM]], axis=0)
    win_o[:, _C_KR:_C_GMLA] = krb.T.astype(_BF)
    put(_C_GMLA, _OFFS[3], MLA_WIDTH)
    put(_C_GPOOL, _OFFS[5], POOL_WIDTH)
    put(_C_PX, _OFFS[4], POOL_WIDTH)
    put(_C_CC, _OFFS[7], 2 * CONV_WIDTH)
    put(_C_CB, _OFFS[6], CONV_WIDTH)
    put(_C_CB + CONV_WIDTH, _OFFS[9], CONV_WIDTH)
    wout_o[...] = wout_ref[...].astype(_BF)

    for hd in range(MLA_HEADS):
        s0, d0 = hd * QK_DIM, hd * HEAD_PAD
        r0 = s0 + NOPE_DIM
        wuq_o[:, d0:d0 + QK_DIM] = wuq_ref[:, s0:s0 + QK_DIM].astype(_BF)
        wuq_o[:, d0 + QK_DIM:d0 + QK_DIM + AXIS_DIM] = (-wuq_ref[:, r0 + AXIS_DIM:r0 + ROPE_DIM]).astype(_BF)
        wuq_o[:, d0 + QK_DIM + AXIS_DIM:d0 + HEAD_PAD] = wuq_ref[:, r0:r0 + AXIS_DIM].astype(_BF)
    for hd in range(MLA_HEADS):
        k0 = hd * KV_HEAD
        wukt_o[hd * NOPE_DIM:(hd + 1) * NOPE_DIM, :] = wukv_ref[:, k0:k0 + NOPE_DIM].T.astype(_BF)
        wuv_o[:, hd * V_DIM:(hd + 1) * V_DIM] = wukv_ref[:, k0 + NOPE_DIM:k0 + KV_HEAD].astype(_BF)
    poolw_o[...] = jnp.zeros((POOL_WIDTH, POOL_WIDTH), _BF)
    for g in range(POOL_GROUPS):
        sl = slice(g * POOL_GROUP_DIM, (g + 1) * POOL_GROUP_DIM)
        poolw_o[sl, sl] = poolw_ref[g].astype(_BF)


def _prep_weights(w_in_t, w_out, w_uq, w_ukv, pool_w):
    per_layer = lambda *shape: pl.BlockSpec((None,) + shape, lambda l: (l,) + (0,) * len(shape))
    return pl.pallas_call(
        _prep_kernel,
        grid=(DEPTH,),
        in_specs=[
            per_layer(IN_WIDTH, D_MODEL),
            per_layer(MIX_WIDTH, D_MODEL),
            per_layer(Q_LORA, MLA_HEADS * QK_DIM),
            per_layer(KV_LORA, MLA_HEADS * KV_HEAD),
            per_layer(POOL_GROUPS, POOL_GROUP_DIM, POOL_GROUP_DIM),
        ],
        out_specs=[
            per_layer(D_MODEL, _C_END),
            per_layer(MIX_WIDTH, D_MODEL),
            per_layer(Q_LORA, MLA_HEADS * HEAD_PAD),
            per_layer(MLA_HEADS * NOPE_DIM, KV_LORA),
            per_layer(KV_LORA, MLA_WIDTH),
            per_layer(POOL_WIDTH, POOL_WIDTH),
        ],
        out_shape=[
            jax.ShapeDtypeStruct((DEPTH, D_MODEL, _C_END), _BF),
            jax.ShapeDtypeStruct((DEPTH, MIX_WIDTH, D_MODEL), _BF),
            jax.ShapeDtypeStruct((DEPTH, Q_LORA, MLA_HEADS * HEAD_PAD), _BF),
            jax.ShapeDtypeStruct((DEPTH, MLA_HEADS * NOPE_DIM, KV_LORA), _BF),
            jax.ShapeDtypeStruct((DEPTH, KV_LORA, MLA_WIDTH), _BF),
            jax.ShapeDtypeStruct((DEPTH, POOL_WIDTH, POOL_WIDTH), _BF),
        ],
        compiler_params=pltpu.CompilerParams(dimension_semantics=("arbitrary",),
                                             vmem_limit_bytes=VMEM_LIMIT_BYTES),
        name="weight_prep",
    )(w_in_t, w_out, w_uq, w_ukv, pool_w)


def _pool_mix(win, rcnt):
    n = win.shape[0]
    rows = n - 2 * HALO
    lane = lax.broadcasted_iota(jnp.int32, (1, POOL_WIDTH), 1)
    sums = [win + pltpu.roll(win, 1, axis=0)]
    for k in (1, 2, 4):
        sums.append(pltpu.roll(sums[-1], k, axis=0) + pltpu.roll(sums[-1], n - k, axis=0))
    sel = sums[-1]
    for g in range(POOL_GROUPS - 2, -1, -1):
        sel = jnp.where(lane < (g + 1) * POOL_GROUP_DIM, sums[g], sel)
    return sel[HALO:HALO + rows] * rcnt - win[HALO:HALO + rows]


def _short_conv(win, w):
    rows = win.shape[0] - 2 * HALO
    return (win[HALO - 1:HALO - 1 + rows] * w[0] + win[HALO:HALO + rows] * w[1]
            + win[HALO + 1:HALO + 1 + rows] * w[2])


def _pass_kernel(*refs, n_seq, seq_len, cache_len, has_rope, emit_state, rc, roll_chunks):
    it = iter(refs)
    x_ref, mod_ref = next(it), next(it)
    if cache_len:
        cckv_all_ref, ckr_all_ref = next(it), next(it)
    if has_rope:
        cs_ref = next(it)
    rcnt_ref, gn_ref, gq_ref, gkv_ref, pools_ref, convw_ref, gfin_ref = (next(it) for _ in range(7))
    w_hbm = [next(it) for _ in range(len(_WEIGHTS))]
    y_ref = next(it)
    if emit_state:
        ckv_out_ref, kr_out_ref = next(it), next(it)
    h_s, q_s, kt_s, v_s, px_s, prod_s, mixed_s = (next(it) for _ in range(7))
    w_vm = [next(it) for _ in range(len(_WEIGHTS))]
    w_sem = next(it)
    win_all_ref, wuq_all_ref, wukt_all_ref, wuv_all_ref, poolw_all_ref, wout_all_ref = w_vm

    def weight_copy(l, k):
        return pltpu.make_async_copy(w_hbm[k].at[l], w_vm[k].at[l], w_sem.at[l, k])

    first_step = pl.program_id(0) == 0

    @pl.when(first_step)
    def _():
        for l in range(DEPTH):
            for k in range(len(_WEIGHTS)):
                weight_copy(l, k).start()

    m = n_seq * seq_len
    keys = cache_len + seq_len
    padded = seq_len + 2 * HALO
    lane = lax.broadcasted_iota(jnp.int32, (1, ROPE_PAD), 1)
    mrow = (1 + pl.program_id(0)) if has_rope else 0
    piece = min(seq_len, rc)
    n_sub = rc // piece
    pieces = [(s, slice(s * piece, (s + 1) * piece)) for s in range(n_sub)]
    chunks = range(m // rc)

    for c in chunks:
        y_ref[c * rc:(c + 1) * rc, :] = x_ref[c * rc:(c + 1) * rc, :]

    zeros_halo = jnp.zeros((HALO, POOL_WIDTH), _F32)
    for buf in (px_s, prod_s):
        for s in range(n_seq):
            buf[s * padded:s * padded + HALO, :] = zeros_halo
            buf[(s + 1) * padded - HALO:(s + 1) * padded, :] = zeros_halo
    ones_col = jnp.where(lax.broadcasted_iota(jnp.int32, (n_seq * keys, V_DIM), 1) == 0, 1.0, 0.0).astype(_BF)
    for hd in range(MLA_HEADS):
        v_s[hd, :, V_DIM:] = ones_col

    def layer_view(l):
        lw = dict(l=l, row=pl.ds(l, 1), win=win_all_ref.at[l], wuq=wuq_all_ref.at[l], wukt=wukt_all_ref.at[l],
                  wuv=wuv_all_ref.at[l], poolw=poolw_all_ref.at[l], wout=wout_all_ref.at[l])
        lw["shift"], lw["scale"], lw["gate"] = (mod_ref[l, k, pl.ds(mrow, 1), :] for k in range(3))
        if cache_len:
            lw["cckv"], lw["ckr"] = cckv_all_ref.at[l], ckr_all_ref.at[l]
        return lw

    def rotate(v, r):
        v = v * cs_ref[pl.ds(r, rc), :]
        return v + pltpu.roll(v, ROPE_DIM, axis=1)

    def put_kv(lw, ckv, kr_t, kc):
        knt = _dot(lw["wukt"][...], ckv.T.astype(_BF))
        v4 = _dot(ckv.astype(_BF), lw["wuv"][...])
        kr_bf = kr_t.astype(_BF)
        rows_k = pl.ds(_hint(kc * rc, rc), rc)
        for hd in range(MLA_HEADS):
            kt_s[hd, kc, :NOPE_DIM, :] = knt[hd * NOPE_DIM:(hd + 1) * NOPE_DIM, :].astype(_BF)
            kt_s[hd, kc, NOPE_DIM:, :] = kr_bf
            v_s[hd, rows_k, :V_DIM] = v4[:, hd * V_DIM:(hd + 1) * V_DIM].astype(_BF)

    def cached_keys(lw):
        ckr_t = lw["ckr"][...]
        ckr_pad = jnp.concatenate([ckr_t, jnp.zeros_like(ckr_t)], axis=0)
        for cc in range(cache_len // rc):
            rows_k = slice(cc * rc, (cc + 1) * rc)
            put_kv(lw, lw["cckv"][rows_k, :], ckr_pad[:, rows_k], cc)

    def halo_base(c, s):
        return _hint((c * n_sub + s) * padded if seq_len <= rc else c * rc, HALO)

    def project(lw, c):
        r = _hint(c * rc, rc)
        rows = pl.ds(r, rc)
        row, win_ref = lw["row"], lw["win"]
        h = (_rms(y_ref[rows, :], gn_ref[row, :]) * (1.0 + lw["scale"]) + lw["shift"]).astype(_BF)
        h_s[rows, :] = h
        a = _dot(h, win_ref[:, _C_CQ:_C_GMLA])
        cq = _rms(a[:, :Q_LORA], gq_ref[row, :]).astype(_BF)
        ckv = _rms(a[:, _C_CKV:_C_KR], gkv_ref[row, :])
        kr = a[:, _C_KR:]
        if has_rope:
            kr = rotate(kr, r)
        kr_t = jnp.where(lane < ROPE_DIM, kr, 0.0).T
        if emit_state:
            for s, ps in pieces:
                ckv_out_ref[c * n_sub + s, lw["l"]] = ckv[ps, :]
                kr_out_ref[c * n_sub + s, lw["l"]] = kr_t[:ROPE_DIM, ps]
        put_kv(lw, ckv, kr_t, cache_len // rc + c)
        for hp in range(MLA_HEADS // 2):
            q2 = _dot(cq, lw["wuq"][:, hp * 2 * HEAD_PAD:(hp + 1) * 2 * HEAD_PAD]) * Q_SCALE
            for j in range(2):
                hd = 2 * hp + j
                c0 = j * HEAD_PAD
                q_s[hd, rows, :NOPE_DIM] = q2[:, c0:c0 + NOPE_DIM].astype(_BF)
                qr = q2[:, c0 + NOPE_DIM:c0 + HEAD_PAD]
                if has_rope:
                    qr = rotate(qr, r)
                q_s[hd, rows, NOPE_DIM:] = qr.astype(_BF)
        pe = _dot(h, win_ref[:, _C_PX:_C_CB])
        for s, ps in pieces:
            rows_h = pl.ds(_hint(halo_base(c, s) + HALO, HALO), piece)
            px_s[rows_h, :] = pe[ps, :POOL_WIDTH]
            prod_s[rows_h, :] = pe[ps, POOL_WIDTH:POOL_WIDTH + CONV_WIDTH] * pe[ps, POOL_WIDTH + CONV_WIDTH:]

    def mix(lw, c):
        r = _hint(c * rc, rc)
        rows = pl.ds(r, rc)
        row, win_ref = lw["row"], lw["win"]
        h = h_s[rows, :]
        for hp in range(MLA_HEADS // 2):
            g2 = _silu(_dot(h, win_ref[:, _C_GMLA + hp * 2 * V_DIM:_C_GMLA + (hp + 1) * 2 * V_DIM]))
            for j in range(2):
                hd = 2 * hp + j
                for s, ps in pieces:
                    rows_q = pl.ds(_hint(r + s * piece, piece), piece)
                    if n_seq == 1:
                        ks = [kt_s[hd, kc] for kc in range(keys // rc)]
                        rows_k = pl.ds(0, keys)
                    else:
                        ks = [kt_s[hd, c, :, ps]]
                        rows_k = rows_q
                    q = q_s[hd, rows_q, :]
                    sc = jnp.concatenate([_dot(q, k) for k in ks], axis=-1)
                    p = jnp.exp2(sc - jnp.max(sc, axis=-1, keepdims=True))
                    ov = _dot(p.astype(_BF), v_s[hd, rows_k, :])
                    o = ov[:, :V_DIM] / ov[:, V_DIM:V_DIM + 1]
                    mixed_s[rows_q, hd * V_DIM:(hd + 1) * V_DIM] = (g2[ps, j * V_DIM:(j + 1) * V_DIM] * o).astype(_BF)

        windows = [pl.ds(halo_base(c, s), piece + 2 * HALO) for s, _ in pieces]
        rcnt = rcnt_ref[pl.ds(0 if seq_len <= rc else r, piece), :]
        pooled = jnp.concatenate([_pool_mix(px_s[w, :], rcnt) for w in windows], axis=0)
        pool = _dot(pooled.astype(_BF), lw["poolw"][...]) * pools_ref[row, :]
        gp = _silu(_dot(h, win_ref[:, _C_GPOOL:_C_PX]))
        mixed_s[rows, MLA_WIDTH:MLA_WIDTH + POOL_WIDTH] = (gp * pool).astype(_BF)

        convw = [convw_ref[k, row, :] for k in range(3)]
        conv = jnp.concatenate([_short_conv(prod_s[w, :], convw) for w in windows], axis=0)
        e2 = _dot(h, win_ref[:, _C_CB:_C_END])
        mixed_s[rows, MLA_WIDTH + POOL_WIDTH:] = (_silu(e2[:, CONV_WIDTH:]) * (e2[:, :CONV_WIDTH] * conv)).astype(_BF)

        y_ref[rows, :] = y_ref[rows, :] + lw["gate"] * _dot(mixed_s[rows, :], lw["wout"][...])

    def one_layer(l, carry):
        @pl.when(first_step)
        def _():
            for k in range(len(_WEIGHTS)):
                weight_copy(l, k).wait()

        lw = layer_view(l)
        if cache_len:
            cached_keys(lw)
        if roll_chunks:
            lax.fori_loop(0, len(chunks), lambda c, _: project(lw, c), None)
            lax.fori_loop(0, len(chunks), lambda c, _: mix(lw, c), None)
        else:
            for c in chunks:
                project(lw, c)
            for c in chunks:
                mix(lw, c)
        return carry

    lax.fori_loop(0, DEPTH, one_layer, 0)
    for c in chunks:
        rows = slice(c * rc, (c + 1) * rc)
        y_ref[rows, :] = _rms(y_ref[rows, :], gfin_ref[...])


def _mixer_pass(x2d, mod_all, wts, *, seq_len, cache=None, rope=None, emit_state):
    rows = x2d.shape[0]
    m = ROWS_PER_STEP
    n_seq = m // seq_len
    n_steps = rows // m
    n_all = rows // seq_len
    cache_len = 0 if cache is None else cache[0].shape[2]
    keys = cache_len + seq_len
    has_rope = rope is not None
    rc = ROW_CHUNK
    assert m % rc == 0 and cache_len % rc == 0
    assert (rc % seq_len == 0 and not cache_len) or (n_seq == 1 and seq_len % rc == 0 and not emit_state)

    def const(*shape):
        return pl.BlockSpec(shape, lambda i: (0,) * len(shape), pipeline_mode=pl.Buffered(1))

    args = [x2d, mod_all]
    in_specs = [pl.BlockSpec((m, D_MODEL), lambda i: (i, 0)), const(DEPTH, 3, MOD_ROWS, D_MODEL)]
    if cache_len:
        args += [cache[0], cache[1]]
        in_specs += [pl.BlockSpec((None, DEPTH, cache_len, KV_LORA), lambda i: (i, 0, 0, 0)),
                     pl.BlockSpec((None, DEPTH, ROPE_DIM, cache_len), lambda i: (i, 0, 0, 0))]
    if has_rope:
        args += [rope]
        in_specs += [const(seq_len, ROPE_PAD)]
    args += [_pool_rcnt(seq_len), wts["g_norm"], wts["g_q"], wts["g_kv"], wts["pool_s"], wts["conv_w"],
             wts["g_final"]]
    in_specs += [const(seq_len, POOL_WIDTH), const(DEPTH, D_MODEL), const(DEPTH, Q_LORA), const(DEPTH, KV_LORA),
                 const(DEPTH, POOL_WIDTH), const(3, DEPTH, CONV_WIDTH), const(1, D_MODEL)]
    weights = [wts[name] for name in _WEIGHTS]
    args += weights
    in_specs += [pl.BlockSpec(memory_space=pl.ANY)] * len(weights)

    out_shape = [jax.ShapeDtypeStruct((rows, D_MODEL), _F32)]
    out_specs = [pl.BlockSpec((m, D_MODEL), lambda i: (i, 0))]
    if emit_state:
        out_shape += [jax.ShapeDtypeStruct((n_all, DEPTH, seq_len, KV_LORA), _F32),
                      jax.ShapeDtypeStruct((n_all, DEPTH, ROPE_DIM, seq_len), _F32)]
        out_specs += [pl.BlockSpec((n_seq, DEPTH, seq_len, KV_LORA), lambda i: (i, 0, 0, 0)),
                      pl.BlockSpec((n_seq, DEPTH, ROPE_DIM, seq_len), lambda i: (i, 0, 0, 0))]

    kern = functools.partial(_pass_kernel, n_seq=n_seq, seq_len=seq_len, cache_len=cache_len,
                             has_rope=has_rope, emit_state=emit_state, rc=rc, roll_chunks=n_seq > 1)
    return pl.pallas_call(
        kern,
        grid=(n_steps,),
        in_specs=in_specs,
        out_specs=out_specs,
        out_shape=out_shape,
        scratch_shapes=[
            pltpu.VMEM((m, D_MODEL), _BF),
            pltpu.VMEM((MLA_HEADS, m, HEAD_PAD), _BF),
            pltpu.VMEM((MLA_HEADS, n_seq * keys // rc, HEAD_PAD, rc), _BF),
            pltpu.VMEM((MLA_HEADS, n_seq * keys, 2 * V_DIM), _BF),
            pltpu.VMEM((n_seq * (seq_len + 2 * HALO), POOL_WIDTH), _F32),
            pltpu.VMEM((n_seq * (seq_len + 2 * HALO), CONV_WIDTH), _F32),
            pltpu.VMEM((m, MIX_WIDTH), _BF),
        ] + [pltpu.VMEM(w.shape, w.dtype) for w in weights] + [pltpu.SemaphoreType.DMA((DEPTH, len(weights)))],
        compiler_params=pltpu.CompilerParams(dimension_semantics=("arbitrary",),
                                             vmem_limit_bytes=VMEM_LIMIT_BYTES),
        name="mixer_pass_latent" if has_rope else "mixer_pass_context",
    )(*args)


def _pool_rcnt(seq_len):
    t = np.arange(seq_len)[:, None]
    half = np.repeat(1 << np.arange(POOL_GROUPS), POOL_GROUP_DIM)[None, :]
    cnt = np.minimum(t + half, seq_len) - np.maximum(t - half, 0)
    return jnp.asarray((1.0 / cnt).astype(np.float32))


def _rope_table(seq_len):
    rows = seq_len // GRID_W
    row = np.repeat(np.arange(rows), GRID_W).astype(np.float32)
    col = np.tile(np.arange(GRID_W), rows).astype(np.float32)
    inv = (1.0 / (np.float32(ROPE_BASE) ** (np.arange(0, AXIS_DIM, 2, dtype=np.float32) / np.float32(AXIS_DIM))))
    inv = inv.astype(np.float32)
    ang = np.concatenate([row[:, None] * inv, col[:, None] * inv], axis=-1).astype(np.float64)
    cos, sin = np.cos(ang).astype(np.float32), np.sin(ang).astype(np.float32)
    return jnp.asarray(np.concatenate([cos, cos, sin, sin], axis=-1))


def kernel(x_prompt, x_sample, cache_ckv, cache_krope, c, c_ctx, w_mod, b_mod, g_norm, w_in, g_q, w_uq,
           g_kv, w_ukv, pool_w, pool_s, conv_w, w_out, g_final):
    batch, seq, _ = x_prompt.shape
    dec_batch, dec_seq, _ = x_sample.shape
    assert 1 + dec_batch <= MOD_ROWS and dec_seq == ROWS_PER_STEP and ROWS_PER_STEP % seq == 0

    c_all = jnp.concatenate([c_ctx[None], c, jnp.zeros((MOD_ROWS - 1 - dec_batch, D_MODEL), _F32)], axis=0)
    mod_all = _modulation(c_all, w_mod, b_mod)
    w_in_r, w_out_r, w_uq_r, w_ukt, w_uv, pool_w_r = _prep_weights(
        jnp.transpose(w_in, (0, 2, 1)), w_out, w_uq, w_ukv, pool_w)
    wts = {
        "g_norm": g_norm, "w_in": w_in_r, "g_q": g_q, "w_uq": w_uq_r, "g_kv": g_kv, "w_ukt": w_ukt,
        "w_uv": w_uv, "pool_w": pool_w_r, "pool_s": pool_s, "conv_w": jnp.transpose(conv_w, (1, 0, 2)),
        "w_out": w_out_r, "g_final": g_final.reshape(1, D_MODEL),
    }

    rope = _rope_table(dec_seq)
    cache = (cache_ckv, jnp.transpose(cache_krope, (0, 1, 3, 2)))
    xp = x_prompt.reshape(batch * seq, D_MODEL)
    xs = x_sample.reshape(dec_batch * dec_seq, D_MODEL)
    xp, state_ckv, state_krope_t = _mixer_pass(xp, mod_all, wts, seq_len=seq, emit_state=True)
    (xs,) = _mixer_pass(xs, mod_all, wts, seq_len=dec_seq, cache=cache, rope=rope, emit_state=False)
    y_prompt = xp.reshape(batch, seq, D_MODEL)
    state_krope = jnp.transpose(state_krope_t, (0, 1, 3, 2))
    y_sample = xs.reshape(dec_batch, dec_seq, D_MODEL)
    return (y_prompt, y_sample, state_ckv, state_krope)
```

```python
import functools

import numpy as np
import jax
import jax.numpy as jnp
from jax import lax
from jax.experimental import pallas as pl
from jax.experimental.pallas import tpu as pltpu

D_MODEL = 1024
DEPTH = 2
GRID_W = 64
MLA_HEADS = 4
NOPE_DIM = 128
ROPE_DIM = 64
V_DIM = 128
QK_DIM = NOPE_DIM + ROPE_DIM
MLA_WIDTH = MLA_HEADS * V_DIM
Q_LORA = 384
KV_LORA = 256
POOL_GROUPS = 4
POOL_GROUP_DIM = 64
POOL_WIDTH = POOL_GROUPS * POOL_GROUP_DIM
CONV_WIDTH = 256
MIX_WIDTH = MLA_WIDTH + POOL_WIDTH + CONV_WIDTH
ROPE_BASE = 10000.0
AXIS_DIM = ROPE_DIM // 2
ATTN_SCALE = QK_DIM ** -0.5
Q_SCALE = ATTN_SCALE * float(np.log2(np.e))
EPS = 1e-6

_SPLITS = (Q_LORA, KV_LORA, ROPE_DIM, MLA_WIDTH, POOL_WIDTH, POOL_WIDTH,
           CONV_WIDTH, CONV_WIDTH, CONV_WIDTH, CONV_WIDTH)
_OFFS = [sum(_SPLITS[:i]) for i in range(len(_SPLITS) + 1)]
IN_WIDTH = _OFFS[-1]

V7X_SUBLANES = 8
V7X_MXU_DEPTH = 256
V7X_VMEM_BYTES = 64 * 1024 * 1024
HEAD_PAD = V7X_MXU_DEPTH
ROPE_PAD = HEAD_PAD - NOPE_DIM
KV_HEAD = NOPE_DIM + V_DIM

_C_CQ = 0
_C_CKV = _C_CQ + Q_LORA
_C_KR = _C_CKV + KV_LORA
_C_GMLA = _C_KR + ROPE_PAD
_C_GPOOL = _C_GMLA + MLA_WIDTH
_C_PX = _C_GPOOL + POOL_WIDTH
_C_CC = _C_PX + POOL_WIDTH
_C_CB = _C_CC + 2 * CONV_WIDTH
_C_END = _C_CB + 2 * CONV_WIDTH

MOD_ROWS = 8
ROWS_PER_STEP = 1024
ROW_CHUNK = 512
HALO = V7X_SUBLANES
PREP_COLS = 256
VMEM_LIMIT_BYTES = V7X_VMEM_BYTES - 4 * 1024 * 1024

_WEIGHTS = ("w_in", "w_uq", "w_ukt", "w_uv", "pool_w", "w_out")

_BF = jnp.bfloat16
_F32 = jnp.float32


def _dot(a, b):
    return jnp.dot(a, b, preferred_element_type=_F32)


def _rms(x, g):
    return x * lax.rsqrt(jnp.mean(x * x, axis=-1, keepdims=True) + EPS) * g


def _silu(x):
    return x * jax.nn.sigmoid(x)


def _hint(v, k):
    return v if isinstance(v, int) else pl.multiple_of(v, k)


def _mod_kernel(c_ref, w_ref, b_ref, o_ref):
    s = _silu(c_ref[...]).astype(_BF)
    for k in range(3):
        cols = slice(k * D_MODEL, (k + 1) * D_MODEL)
        bias = b_ref[pl.ds(pl.program_id(0), 1), cols]
        o_ref[k] = _dot(s, w_ref[:, cols].astype(_BF)) + bias


def _modulation(c_all, w_mod, b_mod):
    return pl.pallas_call(
        _mod_kernel,
        grid=(DEPTH,),
        in_specs=[
            pl.BlockSpec((MOD_ROWS, D_MODEL), lambda l: (0, 0)),
            pl.BlockSpec((None, D_MODEL, 3 * D_MODEL), lambda l: (l, 0, 0)),
            pl.BlockSpec((DEPTH, 3 * D_MODEL), lambda l: (0, 0)),
        ],
        out_specs=pl.BlockSpec((None, 3, MOD_ROWS, D_MODEL), lambda l: (l, 0, 0, 0)),
        out_shape=jax.ShapeDtypeStruct((DEPTH, 3, MOD_ROWS, D_MODEL), _F32),
        compiler_params=pltpu.CompilerParams(dimension_semantics=("arbitrary",),
                                             vmem_limit_bytes=VMEM_LIMIT_BYTES),
        name="modulation",
    )(c_all, w_mod, b_mod)


def _prep_kernel(wint_ref, wout_ref, wuq_ref, wukv_ref, poolw_ref,
                 win_o, wout_o, wuq_o, wukt_o, wuv_o, poolw_o):
    def put(dst, src, width):
        for c in range(0, width, PREP_COLS):
            n = min(PREP_COLS, width - c)
            win_o[:, dst + c:dst + c + n] = wint_ref[src + c:src + c + n, :].T.astype(_BF)

    put(_C_CQ, _OFFS[0], Q_LORA + KV_LORA)
    kr = wint_ref[_OFFS[2]:_OFFS[3], :]
    krb = jnp.concatenate([kr, -kr[AXIS_DIM:], kr[:AXIS_DIM]], axis=0)
    win_o[:, _C_KR:_C_GMLA] = krb.T.astype(_BF)
    put(_C_GMLA, _OFFS[3], MLA_WIDTH)
    put(_C_GPOOL, _OFFS[5], POOL_WIDTH)
    put(_C_PX, _OFFS[4], POOL_WIDTH)
    put(_C_CC, _OFFS[7], 2 * CONV_WIDTH)
    put(_C_CB, _OFFS[6], CONV_WIDTH)
    put(_C_CB + CONV_WIDTH, _OFFS[9], CONV_WIDTH)
    wout_o[...] = wout_ref[...].astype(_BF)

    for hd in range(MLA_HEADS):
        s0, d0 = hd * QK_DIM, hd * HEAD_PAD
        r0 = s0 + NOPE_DIM
        wuq_o[:, d0:d0 + QK_DIM] = wuq_ref[:, s0:s0 + QK_DIM].astype(_BF)
        wuq_o[:, d0 + QK_DIM:d0 + QK_DIM + AXIS_DIM] = (-wuq_ref[:, r0 + AXIS_DIM:r0 + ROPE_DIM]).astype(_BF)
        wuq_o[:, d0 + QK_DIM + AXIS_DIM:d0 + HEAD_PAD] = wuq_ref[:, r0:r0 + AXIS_DIM].astype(_BF)
    for hd in range(MLA_HEADS):
        k0 = hd * KV_HEAD
        wukt_o[hd * NOPE_DIM:(hd + 1) * NOPE_DIM, :] = wukv_ref[:, k0:k0 + NOPE_DIM].T.astype(_BF)
        wuv_o[:, hd * V_DIM:(hd + 1) * V_DIM] = wukv_ref[:, k0 + NOPE_DIM:k0 + KV_HEAD].astype(_BF)
    poolw_o[...] = jnp.zeros((POOL_WIDTH, POOL_WIDTH), _BF)
    for g in range(POOL_GROUPS):
        sl = slice(g * POOL_GROUP_DIM, (g + 1) * POOL_GROUP_DIM)
        poolw_o[sl, sl] = poolw_ref[g].astype(_BF)


def _prep_weights(w_in_t, w_out, w_uq, w_ukv, pool_w):
    per_layer = lambda *shape: pl.BlockSpec((None,) + shape, lambda l: (l,) + (0,) * len(shape))
    return pl.pallas_call(
        _prep_kernel,
        grid=(DEPTH,),
        in_specs=[
            per_layer(IN_WIDTH, D_MODEL),
            per_layer(MIX_WIDTH, D_MODEL),
            per_layer(Q_LORA, MLA_HEADS * QK_DIM),
            per_layer(KV_LORA, MLA_HEADS * KV_HEAD),
            per_layer(POOL_GROUPS, POOL_GROUP_DIM, POOL_GROUP_DIM),
        ],
        out_specs=[
            per_layer(D_MODEL, _C_END),
            per_layer(MIX_WIDTH, D_MODEL),
            per_layer(Q_LORA, MLA_HEADS * HEAD_PAD),
            per_layer(MLA_HEADS * NOPE_DIM, KV_LORA),
            per_layer(KV_LORA, MLA_WIDTH),
            per_layer(POOL_WIDTH, POOL_WIDTH),
        ],
        out_shape=[
            jax.ShapeDtypeStruct((DEPTH, D_MODEL, _C_END), _BF),
            jax.ShapeDtypeStruct((DEPTH, MIX_WIDTH, D_MODEL), _BF),
            jax.ShapeDtypeStruct((DEPTH, Q_LORA, MLA_HEADS * HEAD_PAD), _BF),
            jax.ShapeDtypeStruct((DEPTH, MLA_HEADS * NOPE_DIM, KV_LORA), _BF),
            jax.ShapeDtypeStruct((DEPTH, KV_LORA, MLA_WIDTH), _BF),
            jax.ShapeDtypeStruct((DEPTH, POOL_WIDTH, POOL_WIDTH), _BF),
        ],
        compiler_params=pltpu.CompilerParams(dimension_semantics=("arbitrary",),
                                             vmem_limit_bytes=VMEM_LIMIT_BYTES),
        name="weight_prep",
    )(w_in_t, w_out, w_uq, w_ukv, pool_w)


def _pool_mix(win, rcnt):
    n = win.shape[0]
    rows = n - 2 * HALO
    lane = lax.broadcasted_iota(jnp.int32, (1, POOL_WIDTH), 1)
    sums = [win + pltpu.roll(win, 1, axis=0)]
    for k in (1, 2, 4):
        sums.append(pltpu.roll(sums[-1], k, axis=0) + pltpu.roll(sums[-1], n - k, axis=0))
    sel = sums[-1]
    for g in range(POOL_GROUPS - 2, -1, -1):
        sel = jnp.where(lane < (g + 1) * POOL_GROUP_DIM, sums[g], sel)
    return sel[HALO:HALO + rows] * rcnt - win[HALO:HALO + rows]


def _short_conv(win, w):
    rows = win.shape[0] - 2 * HALO
    return (win[HALO - 1:HALO - 1 + rows] * w[0] + win[HALO:HALO + rows] * w[1]
            + win[HALO + 1:HALO + 1 + rows] * w[2])


def _pass_kernel(*refs, n_seq, seq_len, cache_len, has_rope, emit_state, rc, roll_chunks):
    it = iter(refs)
    x_ref, mod_ref = next(it), next(it)
    if cache_len:
        cckv_all_ref, ckr_all_ref = next(it), next(it)
    if has_rope:
        cs_ref = next(it)
    rcnt_ref, gn_ref, gq_ref, gkv_ref, pools_ref, convw_ref, gfin_ref = (next(it) for _ in range(7))
    w_hbm = [next(it) for _ in range(len(_WEIGHTS))]
    y_ref = next(it)
    if emit_state:
        ckv_out_ref, kr_out_ref = next(it), next(it)
    h_s, q_s, kt_s, v_s, px_s, prod_s, mixed_s = (next(it) for _ in range(7))
    w_vm = [next(it) for _ in range(len(_WEIGHTS))]
    w_sem = next(it)
    win_all_ref, wuq_all_ref, wukt_all_ref, wuv_all_ref, poolw_all_ref, wout_all_ref = w_vm

    def weight_copy(l, k):
        return pltpu.make_async_copy(w_hbm[k].at[l], w_vm[k].at[l], w_sem.at[l, k])

    first_step = pl.program_id(0) == 0

    @pl.when(first_step)
    def _():
        for l in range(DEPTH):
            for k in range(len(_WEIGHTS)):
                weight_copy(l, k).start()

    m = n_seq * seq_len
    keys = cache_len + seq_len
    padded = seq_len + 2 * HALO
    lane = lax.broadcasted_iota(jnp.int32, (1, ROPE_PAD), 1)
    mrow = (1 + pl.program_id(0)) if has_rope else 0
    piece = min(seq_len, rc)
    n_sub = rc // piece
    pieces = [(s, slice(s * piece, (s + 1) * piece)) for s in range(n_sub)]
    chunks = range(m // rc)

    for c in chunks:
        y_ref[c * rc:(c + 1) * rc, :] = x_ref[c * rc:(c + 1) * rc, :]

    zeros_halo = jnp.zeros((HALO, POOL_WIDTH), _F32)
    for buf in (px_s, prod_s):
        for s in range(n_seq):
            buf[s * padded:s * padded + HALO, :] = zeros_halo
            buf[(s + 1) * padded - HALO:(s + 1) * padded, :] = zeros_halo
    ones_col = jnp.where(lax.broadcasted_iota(jnp.int32, (n_seq * keys, V_DIM), 1) == 0, 1.0, 0.0).astype(_BF)
    for hd in range(MLA_HEADS):
        v_s[hd, :, V_DIM:] = ones_col

    def layer_view(l):
        lw = dict(l=l, row=pl.ds(l, 1), win=win_all_ref.at[l], wuq=wuq_all_ref.at[l], wukt=wukt_all_ref.at[l],
                  wuv=wuv_all_ref.at[l], poolw=poolw_all_ref.at[l], wout=wout_all_ref.at[l])
        lw["shift"], lw["scale"], lw["gate"] = (mod_ref[l, k, pl.ds(mrow, 1), :] for k in range(3))
        if cache_len:
            lw["cckv"], lw["ckr"] = cckv_all_ref.at[l], ckr_all_ref.at[l]
        return lw

    def rotate(v, r):
        v = v * cs_ref[pl.ds(r, rc), :]
        return v + pltpu.roll(v, ROPE_DIM, axis=1)

    def put_kv(lw, ckv, kr_t, kc):
        knt = _dot(lw["wukt"][...], ckv.T.astype(_BF))
        v4 = _dot(ckv.astype(_BF), lw["wuv"][...])
        kr_bf = kr_t.astype(_BF)
        rows_k = pl.ds(_hint(kc * rc, rc), rc)
        for hd in range(MLA_HEADS):
            kt_s[hd, kc, :NOPE_DIM, :] = knt[hd * NOPE_DIM:(hd + 1) * NOPE_DIM, :].astype(_BF)
            kt_s[hd, kc, NOPE_DIM:, :] = kr_bf
            v_s[hd, rows_k, :V_DIM] = v4[:, hd * V_DIM:(hd + 1) * V_DIM].astype(_BF)

    def cached_keys(lw):
        ckr_t = lw["ckr"][...]
        ckr_pad = jnp.concatenate([ckr_t, jnp.zeros_like(ckr_t)], axis=0)
        for cc in range(cache_len // rc):
            rows_k = slice(cc * rc, (cc + 1) * rc)
            put_kv(lw, lw["cckv"][rows_k, :], ckr_pad[:, rows_k], cc)

    def halo_base(c, s):
        return _hint((c * n_sub + s) * padded if seq_len <= rc else c * rc, HALO)

    def project(lw, c):
        r = _hint(c * rc, rc)
        rows = pl.ds(r, rc)
        row, win_ref = lw["row"], lw["win"]
        h = (_rms(y_ref[rows, :], gn_ref[row, :]) * (1.0 + lw["scale"]) + lw["shift"]).astype(_BF)
        h_s[rows, :] = h
        a = _dot(h, win_ref[:, _C_CQ:_C_GMLA])
        cq = _rms(a[:, :Q_LORA], gq_ref[row, :]).astype(_BF)
        ckv = _rms(a[:, _C_CKV:_C_KR], gkv_ref[row, :])
        kr = a[:, _C_KR:]
        if has_rope:
            kr = rotate(kr, r)
        kr_t = jnp.where(lane < ROPE_DIM, kr, 0.0).T
        if emit_state:
            for s, ps in pieces:
                ckv_out_ref[c * n_sub + s, lw["l"]] = ckv[ps, :]
                kr_out_ref[c * n_sub + s, lw["l"]] = kr_t[:ROPE_DIM, ps]
        put_kv(lw, ckv, kr_t, cache_len // rc + c)
        for hp in range(MLA_HEADS // 2):
            q2 = _dot(cq, lw["wuq"][:, hp * 2 * HEAD_PAD:(hp + 1) * 2 * HEAD_PAD]) * Q_SCALE
            for j in range(2):
                hd = 2 * hp + j
                c0 = j * HEAD_PAD
                q_s[hd, rows, :NOPE_DIM] = q2[:, c0:c0 + NOPE_DIM].astype(_BF)
                qr = q2[:, c0 + NOPE_DIM:c0 + HEAD_PAD]
                if has_rope:
                    qr = rotate(qr, r)
                q_s[hd, rows, NOPE_DIM:] = qr.astype(_BF)
        pe = _dot(h, win_ref[:, _C_PX:_C_CB])
        for s, ps in pieces:
            rows_h = pl.ds(_hint(halo_base(c, s) + HALO, HALO), piece)
            px_s[rows_h, :] = pe[ps, :POOL_WIDTH]
            prod_s[rows_h, :] = pe[ps, POOL_WIDTH:POOL_WIDTH + CONV_WIDTH] * pe[ps, POOL_WIDTH + CONV_WIDTH:]

    def mix(lw, c):
        r = _hint(c * rc, rc)
        rows = pl.ds(r, rc)
        row, win_ref = lw["row"], lw["win"]
        h = h_s[rows, :]
        for hp in range(MLA_HEADS // 2):
            g2 = _silu(_dot(h, win_ref[:, _C_GMLA + hp * 2 * V_DIM:_C_GMLA + (hp + 1) * 2 * V_DIM]))
            for j in range(2):
                hd = 2 * hp + j
                for s, ps in pieces:
                    rows_q = pl.ds(_hint(r + s * piece, piece), piece)
                    if n_seq == 1:
                        ks = [kt_s[hd, kc] for kc in range(keys // rc)]
                        rows_k = pl.ds(0, keys)
                    else:
                        ks = [kt_s[hd, c, :, ps]]
                        rows_k = rows_q
                    q = q_s[hd, rows_q, :]
                    sc = jnp.concatenate([_dot(q, k) for k in ks], axis=-1)
                    p = jnp.exp2(sc - jnp.max(sc, axis=-1, keepdims=True))
                    ov = _dot(p.astype(_BF), v_s[hd, rows_k, :])
                    o = ov[:, :V_DIM] / ov[:, V_DIM:V_DIM + 1]
                    mixed_s[rows_q, hd * V_DIM:(hd + 1) * V_DIM] = (g2[ps, j * V_DIM:(j + 1) * V_DIM] * o).astype(_BF)

        windows = [pl.ds(halo_base(c, s), piece + 2 * HALO) for s, _ in pieces]
        rcnt = rcnt_ref[pl.ds(0 if seq_len <= rc else r, piece), :]
        pooled = jnp.concatenate([_pool_mix(px_s[w, :], rcnt) for w in windows], axis=0)
        pool = _dot(pooled.astype(_BF), lw["poolw"][...]) * pools_ref[row, :]
        gp = _silu(_dot(h, win_ref[:, _C_GPOOL:_C_PX]))
        mixed_s[rows, MLA_WIDTH:MLA_WIDTH + POOL_WIDTH] = (gp * pool).astype(_BF)

        convw = [convw_ref[k, row, :] for k in range(3)]
        conv = jnp.concatenate([_short_conv(prod_s[w, :], convw) for w in windows], axis=0)
        e2 = _dot(h, win_ref[:, _C_CB:_C_END])
        mixed_s[rows, MLA_WIDTH + POOL_WIDTH:] = (_silu(e2[:, CONV_WIDTH:]) * (e2[:, :CONV_WIDTH] * conv)).astype(_BF)

        y_ref[rows, :] = y_ref[rows, :] + lw["gate"] * _dot(mixed_s[rows, :], lw["wout"][...])

    def one_layer(l, carry):
        @pl.when(first_step)
        def _():
            for k in range(len(_WEIGHTS)):
                weight_copy(l, k).wait()

        lw = layer_view(l)
        if cache_len:
            cached_keys(lw)
        if roll_chunks:
            lax.fori_loop(0, len(chunks), lambda c, _: project(lw, c), None)
        else:
            for c in chunks:
                project(lw, c)
        for c in chunks:
            mix(lw, c)
        return carry

    lax.fori_loop(0, DEPTH, one_layer, 0)
    for c in chunks:
        rows = slice(c * rc, (c + 1) * rc)
        y_ref[rows, :] = _rms(y_ref[rows, :], gfin_ref[...])


def _mixer_pass(x2d, mod_all, wts, *, seq_len, cache=None, rope=None, emit_state):
    rows = x2d.shape[0]
    m = ROWS_PER_STEP
    n_seq = m // seq_len
    n_steps = rows // m
    n_all = rows // seq_len
    cache_len = 0 if cache is None else cache[0].shape[2]
    keys = cache_len + seq_len
    has_rope = rope is not None
    rc = ROW_CHUNK
    assert m % rc == 0 and cache_len % rc == 0
    assert (rc % seq_len == 0 and not cache_len) or (n_seq == 1 and seq_len % rc == 0 and not emit_state)

    def const(*shape):
        return pl.BlockSpec(shape, lambda i: (0,) * len(shape), pipeline_mode=pl.Buffered(1))

    args = [x2d, mod_all]
    in_specs = [pl.BlockSpec((m, D_MODEL), lambda i: (i, 0)), const(DEPTH, 3, MOD_ROWS, D_MODEL)]
    if cache_len:
        args += [cache[0], cache[1]]
        in_specs += [pl.BlockSpec((None, DEPTH, cache_len, KV_LORA), lambda i: (i, 0, 0, 0)),
                     pl.BlockSpec((None, DEPTH, ROPE_DIM, cache_len), lambda i: (i, 0, 0, 0))]
    if has_rope:
        args += [rope]
        in_specs += [const(seq_len, ROPE_PAD)]
    args += [_pool_rcnt(seq_len), wts["g_norm"], wts["g_q"], wts["g_kv"], wts["pool_s"], wts["conv_w"],
             wts["g_final"]]
    in_specs += [const(seq_len, POOL_WIDTH), const(DEPTH, D_MODEL), const(DEPTH, Q_LORA), const(DEPTH, KV_LORA),
                 const(DEPTH, POOL_WIDTH), const(3, DEPTH, CONV_WIDTH), const(1, D_MODEL)]
    weights = [wts[name] for name in _WEIGHTS]
    args += weights
    in_specs += [pl.BlockSpec(memory_space=pl.ANY)] * len(weights)

    out_shape = [jax.ShapeDtypeStruct((rows, D_MODEL), _F32)]
    out_specs = [pl.BlockSpec((m, D_MODEL), lambda i: (i, 0))]
    if emit_state:
        out_shape += [jax.ShapeDtypeStruct((n_all, DEPTH, seq_len, KV_LORA), _F32),
                      jax.ShapeDtypeStruct((n_all, DEPTH, ROPE_DIM, seq_len), _F32)]
        out_specs += [pl.BlockSpec((n_seq, DEPTH, seq_len, KV_LORA), lambda i: (i, 0, 0, 0)),
                      pl.BlockSpec((n_seq, DEPTH, ROPE_DIM, seq_len), lambda i: (i, 0, 0, 0))]

    kern = functools.partial(_pass_kernel, n_seq=n_seq, seq_len=seq_len, cache_len=cache_len,
                             has_rope=has_rope, emit_state=emit_state, rc=rc, roll_chunks=n_seq > 1)
    return pl.pallas_call(
        kern,
        grid=(n_steps,),
        in_specs=in_specs,
        out_specs=out_specs,
        out_shape=out_shape,
        scratch_shapes=[
            pltpu.VMEM((m, D_MODEL), _BF),
            pltpu.VMEM((MLA_HEADS, m, HEAD_PAD), _BF),
            pltpu.VMEM((MLA_HEADS, n_seq * keys // rc, HEAD_PAD, rc), _BF),
            pltpu.VMEM((MLA_HEADS, n_seq * keys, 2 * V_DIM), _BF),
            pltpu.VMEM((n_seq * (seq_len + 2 * HALO), POOL_WIDTH), _F32),
            pltpu.VMEM((n_seq * (seq_len + 2 * HALO), CONV_WIDTH), _F32),
            pltpu.VMEM((m, MIX_WIDTH), _BF),
        ] + [pltpu.VMEM(w.shape, w.dtype) for w in weights] + [pltpu.SemaphoreType.DMA((DEPTH, len(weights)))],
        compiler_params=pltpu.CompilerParams(dimension_semantics=("arbitrary",),
                                             vmem_limit_bytes=VMEM_LIMIT_BYTES),
        name="mixer_pass_latent" if has_rope else "mixer_pass_context",
    )(*args)


def _pool_rcnt(seq_len):
    t = np.arange(seq_len)[:, None]
    half = np.repeat(1 << np.arange(POOL_GROUPS), POOL_GROUP_DIM)[None, :]
    cnt = np.minimum(t + half, seq_len) - np.maximum(t - half, 0)
    return jnp.asarray((1.0 / cnt).astype(np.float32))


def _rope_table(seq_len):
    rows = seq_len // GRID_W
    row = np.repeat(np.arange(rows), GRID_W).astype(np.float32)
    col = np.tile(np.arange(GRID_W), rows).astype(np.float32)
    inv = (1.0 / (np.float32(ROPE_BASE) ** (np.arange(0, AXIS_DIM, 2, dtype=np.float32) / np.float32(AXIS_DIM))))
    inv = inv.astype(np.float32)
    ang = np.concatenate([row[:, None] * inv, col[:, None] * inv], axis=-1).astype(np.float64)
    cos, sin = np.cos(ang).astype(np.float32), np.sin(ang).astype(np.float32)
    return jnp.asarray(np.concatenate([cos, cos, sin, sin], axis=-1))


def kernel(x_prompt, x_sample, cache_ckv, cache_krope, c, c_ctx, w_mod, b_mod, g_norm, w_in, g_q, w_uq,
           g_kv, w_ukv, pool_w, pool_s, conv_w, w_out, g_final):
    batch, seq, _ = x_prompt.shape
    dec_batch, dec_seq, _ = x_sample.shape
    assert 1 + dec_batch <= MOD_ROWS and dec_seq == ROWS_PER_STEP and ROWS_PER_STEP % seq == 0

    c_all = jnp.concatenate([c_ctx[None], c, jnp.zeros((MOD_ROWS - 1 - dec_batch, D_MODEL), _F32)], axis=0)
    mod_all = _modulation(c_all, w_mod, b_mod)
    w_in_r, w_out_r, w_uq_r, w_ukt, w_uv, pool_w_r = _prep_weights(
        jnp.transpose(w_in, (0, 2, 1)), w_out, w_uq, w_ukv, pool_w)
    wts = {
        "g_norm": g_norm, "w_in": w_in_r, "g_q": g_q, "w_uq": w_uq_r, "g_kv": g_kv, "w_ukt": w_ukt,
        "w_uv": w_uv, "pool_w": pool_w_r, "pool_s": pool_s, "conv_w": jnp.transpose(conv_w, (1, 0, 2)),
        "w_out": w_out_r, "g_final": g_final.reshape(1, D_MODEL),
    }

    rope = _rope_table(dec_seq)
    cache = (cache_ckv, jnp.transpose(cache_krope, (0, 1, 3, 2)))
    xp = x_prompt.reshape(batch * seq, D_MODEL)
    xs = x_sample.reshape(dec_batch * dec_seq, D_MODEL)
    xp, state_ckv, state_krope_t = _mixer_pass(xp, mod_all, wts, seq_len=seq, emit_state=True)
    (xs,) = _mixer_pass(xs, mod_all, wts, seq_len=dec_seq, cache=cache, rope=rope, emit_state=False)
    y_prompt = xp.reshape(batch, seq, D_MODEL)
    state_krope = jnp.transpose(state_krope_t, (0, 1, 3, 2))
    y_sample = xs.reshape(dec_batch, dec_seq, D_MODEL)
    return (y_prompt, y_sample, state_ckv, state_krope)
```

```python
import functools

import numpy as np
import jax
import jax.numpy as jnp
from jax import lax
from jax.experimental import pallas as pl
from jax.experimental.pallas import tpu as pltpu

D_MODEL = 1024
DEPTH = 2
GRID_W = 64
MLA_HEADS = 4
NOPE_DIM = 128
ROPE_DIM = 64
V_DIM = 128
QK_DIM = NOPE_DIM + ROPE_DIM
MLA_WIDTH = MLA_HEADS * V_DIM
Q_LORA = 384
KV_LORA = 256
POOL_GROUPS = 4
POOL_GROUP_DIM = 64
POOL_WIDTH = POOL_GROUPS * POOL_GROUP_DIM
CONV_WIDTH = 256
MIX_WIDTH = MLA_WIDTH + POOL_WIDTH + CONV_WIDTH
ROPE_BASE = 10000.0
AXIS_DIM = ROPE_DIM // 2
ATTN_SCALE = QK_DIM ** -0.5
Q_SCALE = ATTN_SCALE * float(np.log2(np.e))
EPS = 1e-6

_SPLITS = (Q_LORA, KV_LORA, ROPE_DIM, MLA_WIDTH, POOL_WIDTH, POOL_WIDTH,
           CONV_WIDTH, CONV_WIDTH, CONV_WIDTH, CONV_WIDTH)
_OFFS = [sum(_SPLITS[:i]) for i in range(len(_SPLITS) + 1)]
IN_WIDTH = _OFFS[-1]

V7X_SUBLANES = 8
V7X_MXU_DEPTH = 256
V7X_VMEM_BYTES = 64 * 1024 * 1024
HEAD_PAD = V7X_MXU_DEPTH
ROPE_PAD = HEAD_PAD - NOPE_DIM
KV_HEAD = NOPE_DIM + V_DIM

_C_CQ = 0
_C_CKV = _C_CQ + Q_LORA
_C_KR = _C_CKV + KV_LORA
_C_GMLA = _C_KR + ROPE_PAD
_C_GPOOL = _C_GMLA + MLA_WIDTH
_C_PX = _C_GPOOL + POOL_WIDTH
_C_CC = _C_PX + POOL_WIDTH
_C_CB = _C_CC + 2 * CONV_WIDTH
_C_END = _C_CB + 2 * CONV_WIDTH

MOD_ROWS = 8
ROWS_PER_STEP = 1024
ROW_CHUNK = 512
SWEEP_ROWS = 64
HALO = V7X_SUBLANES
PREP_COLS = 256
VMEM_LIMIT_BYTES = V7X_VMEM_BYTES - 4 * 1024 * 1024

_WEIGHTS = ("w_in", "w_uq", "w_ukt", "w_uv", "pool_w", "w_out")

_BF = jnp.bfloat16
_F32 = jnp.float32


def _dot(a, b):
    return jnp.dot(a, b, preferred_element_type=_F32)


def _rms(x, g):
    return x * lax.rsqrt(jnp.mean(x * x, axis=-1, keepdims=True) + EPS) * g


def _silu(x):
    return x * jax.nn.sigmoid(x)


def _hint(v, k):
    return v if isinstance(v, int) else pl.multiple_of(v, k)


def _mod_kernel(c_ref, w_ref, b_ref, o_ref):
    s = _silu(c_ref[...]).astype(_BF)
    for k in range(3):
        cols = slice(k * D_MODEL, (k + 1) * D_MODEL)
        bias = b_ref[pl.ds(pl.program_id(0), 1), cols]
        o_ref[k] = _dot(s, w_ref[:, cols].astype(_BF)) + bias


def _modulation(c_all, w_mod, b_mod):
    return pl.pallas_call(
        _mod_kernel,
        grid=(DEPTH,),
        in_specs=[
            pl.BlockSpec((MOD_ROWS, D_MODEL), lambda l: (0, 0)),
            pl.BlockSpec((None, D_MODEL, 3 * D_MODEL), lambda l: (l, 0, 0)),
            pl.BlockSpec((DEPTH, 3 * D_MODEL), lambda l: (0, 0)),
        ],
        out_specs=pl.BlockSpec((None, 3, MOD_ROWS, D_MODEL), lambda l: (l, 0, 0, 0)),
        out_shape=jax.ShapeDtypeStruct((DEPTH, 3, MOD_ROWS, D_MODEL), _F32),
        compiler_params=pltpu.CompilerParams(dimension_semantics=("arbitrary",),
                                             vmem_limit_bytes=VMEM_LIMIT_BYTES),
        name="modulation",
    )(c_all, w_mod, b_mod)


def _prep_kernel(wint_ref, wout_ref, wuq_ref, wukv_ref, poolw_ref,
                 win_o, wout_o, wuq_o, wukt_o, wuv_o, poolw_o):
    def put(dst, src, width):
        for c in range(0, width, PREP_COLS):
            n = min(PREP_COLS, width - c)
            win_o[:, dst + c:dst + c + n] = wint_ref[src + c:src + c + n, :].T.astype(_BF)

    put(_C_CQ, _OFFS[0], Q_LORA + KV_LORA)
    kr = wint_ref[_OFFS[2]:_OFFS[3], :]
    krb = jnp.concatenate([kr, -kr[AXIS_DIM:], kr[:AXIS_DIM]], axis=0)
    win_o[:, _C_KR:_C_GMLA] = krb.T.astype(_BF)
    put(_C_GMLA, _OFFS[3], MLA_WIDTH)
    put(_C_GPOOL, _OFFS[5], POOL_WIDTH)
    put(_C_PX, _OFFS[4], POOL_WIDTH)
    put(_C_CC, _OFFS[7], 2 * CONV_WIDTH)
    put(_C_CB, _OFFS[6], CONV_WIDTH)
    put(_C_CB + CONV_WIDTH, _OFFS[9], CONV_WIDTH)
    wout_o[...] = wout_ref[...].astype(_BF)

    for hd in range(MLA_HEADS):
        s0, d0 = hd * QK_DIM, hd * HEAD_PAD
        r0 = s0 + NOPE_DIM
        wuq_o[:, d0:d0 + QK_DIM] = wuq_ref[:, s0:s0 + QK_DIM].astype(_BF)
        wuq_o[:, d0 + QK_DIM:d0 + QK_DIM + AXIS_DIM] = (-wuq_ref[:, r0 + AXIS_DIM:r0 + ROPE_DIM]).astype(_BF)
        wuq_o[:, d0 + QK_DIM + AXIS_DIM:d0 + HEAD_PAD] = wuq_ref[:, r0:r0 + AXIS_DIM].astype(_BF)
    for hd in range(MLA_HEADS):
        k0 = hd * KV_HEAD
        wukt_o[hd * NOPE_DIM:(hd + 1) * NOPE_DIM, :] = wukv_ref[:, k0:k0 + NOPE_DIM].T.astype(_BF)
        wuv_o[:, hd * V_DIM:(hd + 1) * V_DIM] = wukv_ref[:, k0 + NOPE_DIM:k0 + KV_HEAD].astype(_BF)
    poolw_o[...] = jnp.zeros((POOL_WIDTH, POOL_WIDTH), _BF)
    for g in range(POOL_GROUPS):
        sl = slice(g * POOL_GROUP_DIM, (g + 1) * POOL_GROUP_DIM)
        poolw_o[sl, sl] = poolw_ref[g].astype(_BF)


def _prep_weights(w_in_t, w_out, w_uq, w_ukv, pool_w):
    per_layer = lambda *shape: pl.BlockSpec((None,) + shape, lambda l: (l,) + (0,) * len(shape))
    return pl.pallas_call(
        _prep_kernel,
        grid=(DEPTH,),
        in_specs=[
            per_layer(IN_WIDTH, D_MODEL),
            per_layer(MIX_WIDTH, D_MODEL),
            per_layer(Q_LORA, MLA_HEADS * QK_DIM),
            per_layer(KV_LORA, MLA_HEADS * KV_HEAD),
            per_layer(POOL_GROUPS, POOL_GROUP_DIM, POOL_GROUP_DIM),
        ],
        out_specs=[
            per_layer(D_MODEL, _C_END),
            per_layer(MIX_WIDTH, D_MODEL),
            per_layer(Q_LORA, MLA_HEADS * HEAD_PAD),
            per_layer(MLA_HEADS * NOPE_DIM, KV_LORA),
            per_layer(KV_LORA, MLA_WIDTH),
            per_layer(POOL_WIDTH, POOL_WIDTH),
        ],
        out_shape=[
            jax.ShapeDtypeStruct((DEPTH, D_MODEL, _C_END), _BF),
            jax.ShapeDtypeStruct((DEPTH, MIX_WIDTH, D_MODEL), _BF),
            jax.ShapeDtypeStruct((DEPTH, Q_LORA, MLA_HEADS * HEAD_PAD), _BF),
            jax.ShapeDtypeStruct((DEPTH, MLA_HEADS * NOPE_DIM, KV_LORA), _BF),
            jax.ShapeDtypeStruct((DEPTH, KV_LORA, MLA_WIDTH), _BF),
            jax.ShapeDtypeStruct((DEPTH, POOL_WIDTH, POOL_WIDTH), _BF),
        ],
        compiler_params=pltpu.CompilerParams(dimension_semantics=("arbitrary",),
                                             vmem_limit_bytes=VMEM_LIMIT_BYTES),
        name="weight_prep",
    )(w_in_t, w_out, w_uq, w_ukv, pool_w)


def _pool_mix(win, rcnt):
    n = win.shape[0]
    rows = n - 2 * HALO
    lane = lax.broadcasted_iota(jnp.int32, (1, POOL_WIDTH), 1)
    sums = [win + pltpu.roll(win, 1, axis=0)]
    for k in (1, 2, 4):
        sums.append(pltpu.roll(sums[-1], k, axis=0) + pltpu.roll(sums[-1], n - k, axis=0))
    sel = sums[-1]
    for g in range(POOL_GROUPS - 2, -1, -1):
        sel = jnp.where(lane < (g + 1) * POOL_GROUP_DIM, sums[g], sel)
    return sel[HALO:HALO + rows] * rcnt - win[HALO:HALO + rows]


def _short_conv(win, w):
    rows = win.shape[0] - 2 * HALO
    return (win[HALO - 1:HALO - 1 + rows] * w[0] + win[HALO:HALO + rows] * w[1]
            + win[HALO + 1:HALO + 1 + rows] * w[2])


def _pass_kernel(*refs, n_seq, seq_len, cache_len, has_rope, emit_state, rc):
    it = iter(refs)
    x_ref, mod_ref = next(it), next(it)
    if cache_len:
        cckv_all_ref, ckr_all_ref = next(it), next(it)
    if has_rope:
        cs_ref = next(it)
    rcnt_ref, gn_ref, gq_ref, gkv_ref, pools_ref, convw_ref, gfin_ref = (next(it) for _ in range(7))
    w_hbm = [next(it) for _ in range(len(_WEIGHTS))]
    y_ref = next(it)
    if emit_state:
        ckv_out_ref, kr_out_ref = next(it), next(it)
    h_s, q_s, kt_s, v_s, px_s, prod_s, mixed_s = (next(it) for _ in range(7))
    w_vm = [next(it) for _ in range(len(_WEIGHTS))]
    w_sem = next(it)
    win_all_ref, wuq_all_ref, wukt_all_ref, wuv_all_ref, poolw_all_ref, wout_all_ref = w_vm

    def weight_copy(l, k):
        return pltpu.make_async_copy(w_hbm[k].at[l], w_vm[k].at[l], w_sem.at[l, k])

    first_step = pl.program_id(0) == 0

    @pl.when(first_step)
    def _():
        for l in range(DEPTH):
            for k in range(len(_WEIGHTS)):
                weight_copy(l, k).start()

    m = n_seq * seq_len
    keys = cache_len + seq_len
    padded = seq_len + 2 * HALO
    lane = lax.broadcasted_iota(jnp.int32, (1, ROPE_PAD), 1)
    mrow = (1 + pl.program_id(0)) if has_rope else 0
    piece = min(seq_len, rc)
    n_sub = rc // piece
    pieces = [(s, slice(s * piece, (s + 1) * piece)) for s in range(n_sub)]
    chunks = range(m // rc)

    def copy_in(b, carry):
        rows = pl.ds(_hint(b * SWEEP_ROWS, SWEEP_ROWS), SWEEP_ROWS)
        y_ref[rows, :] = x_ref[rows, :]
        return carry

    lax.fori_loop(0, m // SWEEP_ROWS, copy_in, 0)

    zeros_halo = jnp.zeros((HALO, POOL_WIDTH), _F32)
    for buf in (px_s, prod_s):
        for s in range(n_seq):
            buf[s * padded:s * padded + HALO, :] = zeros_halo
            buf[(s + 1) * padded - HALO:(s + 1) * padded, :] = zeros_halo
    ones_col = jnp.where(lax.broadcasted_iota(jnp.int32, (n_seq * keys, V_DIM), 1) == 0, 1.0, 0.0).astype(_BF)
    for hd in range(MLA_HEADS):
        v_s[hd, :, V_DIM:] = ones_col

    def layer_view(l):
        lw = dict(l=l, row=pl.ds(l, 1), win=win_all_ref.at[l], wuq=wuq_all_ref.at[l], wukt=wukt_all_ref.at[l],
                  wuv=wuv_all_ref.at[l], poolw=poolw_all_ref.at[l], wout=wout_all_ref.at[l])
        lw["shift"], lw["scale"], lw["gate"] = (mod_ref[l, k, pl.ds(mrow, 1), :] for k in range(3))
        if cache_len:
            lw["cckv"], lw["ckr"] = cckv_all_ref.at[l], ckr_all_ref.at[l]
        return lw

    def rotate(v, r):
        v = v * cs_ref[pl.ds(r, rc), :]
        return v + pltpu.roll(v, ROPE_DIM, axis=1)

    def put_kv(lw, ckv, kr_t, kc):
        knt = _dot(lw["wukt"][...], ckv.T.astype(_BF))
        v4 = _dot(ckv.astype(_BF), lw["wuv"][...])
        kr_bf = kr_t.astype(_BF)
        rows_k = pl.ds(_hint(kc * rc, rc), rc)
        for hd in range(MLA_HEADS):
            kt_s[hd, kc, :NOPE_DIM, :] = knt[hd * NOPE_DIM:(hd + 1) * NOPE_DIM, :].astype(_BF)
            kt_s[hd, kc, NOPE_DIM:, :] = kr_bf
            v_s[hd, rows_k, :V_DIM] = v4[:, hd * V_DIM:(hd + 1) * V_DIM].astype(_BF)

    def cached_keys(lw):
        ckr_t = lw["ckr"][...]
        ckr_pad = jnp.concatenate([ckr_t, jnp.zeros_like(ckr_t)], axis=0)
        for cc in range(cache_len // rc):
            rows_k = slice(cc * rc, (cc + 1) * rc)
            put_kv(lw, lw["cckv"][rows_k, :], ckr_pad[:, rows_k], cc)

    def halo_base(c, s):
        return _hint((c * n_sub + s) * padded if seq_len <= rc else c * rc, HALO)

    def project(lw, c):
        r = _hint(c * rc, rc)
        rows = pl.ds(r, rc)
        row, win_ref = lw["row"], lw["win"]
        h = (_rms(y_ref[rows, :], gn_ref[row, :]) * (1.0 + lw["scale"]) + lw["shift"]).astype(_BF)
        h_s[rows, :] = h
        a = _dot(h, win_ref[:, _C_CQ:_C_GMLA])
        cq = _rms(a[:, :Q_LORA], gq_ref[row, :]).astype(_BF)
        ckv = _rms(a[:, _C_CKV:_C_KR], gkv_ref[row, :])
        kr = a[:, _C_KR:]
        if has_rope:
            kr = rotate(kr, r)
        kr_t = jnp.where(lane < ROPE_DIM, kr, 0.0).T
        if emit_state:
            for s, ps in pieces:
                ckv_out_ref[c * n_sub + s, lw["l"]] = ckv[ps, :]
                kr_out_ref[c * n_sub + s, lw["l"]] = kr_t[:ROPE_DIM, ps]
        put_kv(lw, ckv, kr_t, cache_len // rc + c)
        for hp in range(MLA_HEADS // 2):
            q2 = _dot(cq, lw["wuq"][:, hp * 2 * HEAD_PAD:(hp + 1) * 2 * HEAD_PAD]) * Q_SCALE
            for j in range(2):
                hd = 2 * hp + j
                c0 = j * HEAD_PAD
                q_s[hd, rows, :NOPE_DIM] = q2[:, c0:c0 + NOPE_DIM].astype(_BF)
                qr = q2[:, c0 + NOPE_DIM:c0 + HEAD_PAD]
                if has_rope:
                    qr = rotate(qr, r)
                q_s[hd, rows, NOPE_DIM:] = qr.astype(_BF)
        pe = _dot(h, win_ref[:, _C_PX:_C_CB])
        for s, ps in pieces:
            rows_h = pl.ds(_hint(halo_base(c, s) + HALO, HALO), piece)
            px_s[rows_h, :] = pe[ps, :POOL_WIDTH]
            prod_s[rows_h, :] = pe[ps, POOL_WIDTH:POOL_WIDTH + CONV_WIDTH] * pe[ps, POOL_WIDTH + CONV_WIDTH:]

    def mix(lw, c):
        r = _hint(c * rc, rc)
        rows = pl.ds(r, rc)
        row, win_ref = lw["row"], lw["win"]
        h = h_s[rows, :]
        for hp in range(MLA_HEADS // 2):
            g2 = _silu(_dot(h, win_ref[:, _C_GMLA + hp * 2 * V_DIM:_C_GMLA + (hp + 1) * 2 * V_DIM]))
            for j in range(2):
                hd = 2 * hp + j
                for s, ps in pieces:
                    rows_q = pl.ds(_hint(r + s * piece, piece), piece)
                    if n_seq == 1:
                        ks = [kt_s[hd, kc] for kc in range(keys // rc)]
                        rows_k = pl.ds(0, keys)
                    else:
                        ks = [kt_s[hd, c, :, ps]]
                        rows_k = rows_q
                    q = q_s[hd, rows_q, :]
                    sc = jnp.concatenate([_dot(q, k) for k in ks], axis=-1)
                    p = jnp.exp2(sc - jnp.max(sc, axis=-1, keepdims=True))
                    ov = _dot(p.astype(_BF), v_s[hd, rows_k, :])
                    o = ov[:, :V_DIM] / ov[:, V_DIM:V_DIM + 1]
                    mixed_s[rows_q, hd * V_DIM:(hd + 1) * V_DIM] = (g2[ps, j * V_DIM:(j + 1) * V_DIM] * o).astype(_BF)

        windows = [pl.ds(halo_base(c, s), piece + 2 * HALO) for s, _ in pieces]
        rcnt = rcnt_ref[pl.ds(0 if seq_len <= rc else r, piece), :]
        pooled = jnp.concatenate([_pool_mix(px_s[w, :], rcnt) for w in windows], axis=0)
        pool = _dot(pooled.astype(_BF), lw["poolw"][...]) * pools_ref[row, :]
        gp = _silu(_dot(h, win_ref[:, _C_GPOOL:_C_PX]))
        mixed_s[rows, MLA_WIDTH:MLA_WIDTH + POOL_WIDTH] = (gp * pool).astype(_BF)

        convw = [convw_ref[k, row, :] for k in range(3)]
        conv = jnp.concatenate([_short_conv(prod_s[w, :], convw) for w in windows], axis=0)
        e2 = _dot(h, win_ref[:, _C_CB:_C_END])
        mixed_s[rows, MLA_WIDTH + POOL_WIDTH:] = (_silu(e2[:, CONV_WIDTH:]) * (e2[:, :CONV_WIDTH] * conv)).astype(_BF)

        y_ref[rows, :] = y_ref[rows, :] + lw["gate"] * _dot(mixed_s[rows, :], lw["wout"][...])

    def one_layer(l, carry):
        @pl.when(first_step)
        def _():
            for k in range(len(_WEIGHTS)):
                weight_copy(l, k).wait()

        lw = layer_view(l)
        if cache_len:
            cached_keys(lw)
        for c in chunks:
            project(lw, c)
        for c in chunks:
            mix(lw, c)
        return carry

    lax.fori_loop(0, DEPTH, one_layer, 0)

    def final_norm(b, carry):
        rows = pl.ds(_hint(b * SWEEP_ROWS, SWEEP_ROWS), SWEEP_ROWS)
        y_ref[rows, :] = _rms(y_ref[rows, :], gfin_ref[...])
        return carry

    lax.fori_loop(0, m // SWEEP_ROWS, final_norm, 0)


def _mixer_pass(x2d, mod_all, wts, *, seq_len, cache=None, rope=None, emit_state):
    rows = x2d.shape[0]
    m = ROWS_PER_STEP
    n_seq = m // seq_len
    n_steps = rows // m
    n_all = rows // seq_len
    cache_len = 0 if cache is None else cache[0].shape[2]
    keys = cache_len + seq_len
    has_rope = rope is not None
    rc = ROW_CHUNK
    assert m % rc == 0 and cache_len % rc == 0
    assert (rc % seq_len == 0 and not cache_len) or (n_seq == 1 and seq_len % rc == 0 and not emit_state)

    def const(*shape):
        return pl.BlockSpec(shape, lambda i: (0,) * len(shape), pipeline_mode=pl.Buffered(1))

    args = [x2d, mod_all]
    in_specs = [pl.BlockSpec((m, D_MODEL), lambda i: (i, 0)), const(DEPTH, 3, MOD_ROWS, D_MODEL)]
    if cache_len:
        args += [cache[0], cache[1]]
        in_specs += [pl.BlockSpec((None, DEPTH, cache_len, KV_LORA), lambda i: (i, 0, 0, 0)),
                     pl.BlockSpec((None, DEPTH, ROPE_DIM, cache_len), lambda i: (i, 0, 0, 0))]
    if has_rope:
        args += [rope]
        in_specs += [const(seq_len, ROPE_PAD)]
    args += [_pool_rcnt(seq_len), wts["g_norm"], wts["g_q"], wts["g_kv"], wts["pool_s"], wts["conv_w"],
             wts["g_final"]]
    in_specs += [const(seq_len, POOL_WIDTH), const(DEPTH, D_MODEL), const(DEPTH, Q_LORA), const(DEPTH, KV_LORA),
                 const(DEPTH, POOL_WIDTH), const(3, DEPTH, CONV_WIDTH), const(1, D_MODEL)]
    weights = [wts[name] for name in _WEIGHTS]
    args += weights
    in_specs += [pl.BlockSpec(memory_space=pl.ANY)] * len(weights)

    out_shape = [jax.ShapeDtypeStruct((rows, D_MODEL), _F32)]
    out_specs = [pl.BlockSpec((m, D_MODEL), lambda i: (i, 0))]
    if emit_state:
        out_shape += [jax.ShapeDtypeStruct((n_all, DEPTH, seq_len, KV_LORA), _F32),
                      jax.ShapeDtypeStruct((n_all, DEPTH, ROPE_DIM, seq_len), _F32)]
        out_specs += [pl.BlockSpec((n_seq, DEPTH, seq_len, KV_LORA), lambda i: (i, 0, 0, 0)),
                      pl.BlockSpec((n_seq, DEPTH, ROPE_DIM, seq_len), lambda i: (i, 0, 0, 0))]

    kern = functools.partial(_pass_kernel, n_seq=n_seq, seq_len=seq_len, cache_len=cache_len,
                             has_rope=has_rope, emit_state=emit_state, rc=rc)
    return pl.pallas_call(
        kern,
        grid=(n_steps,),
        in_specs=in_specs,
        out_specs=out_specs,
        out_shape=out_shape,
        scratch_shapes=[
            pltpu.VMEM((m, D_MODEL), _BF),
            pltpu.VMEM((MLA_HEADS, m, HEAD_PAD), _BF),
            pltpu.VMEM((MLA_HEADS, n_seq * keys // rc, HEAD_PAD, rc), _BF),
            pltpu.VMEM((MLA_HEADS, n_seq * keys, 2 * V_DIM), _BF),
            pltpu.VMEM((n_seq * (seq_len + 2 * HALO), POOL_WIDTH), _F32),
            pltpu.VMEM((n_seq * (seq_len + 2 * HALO), CONV_WIDTH), _F32),
            pltpu.VMEM((m, MIX_WIDTH), _BF),
        ] + [pltpu.VMEM(w.shape, w.dtype) for w in weights] + [pltpu.SemaphoreType.DMA((DEPTH, len(weights)))],
        compiler_params=pltpu.CompilerParams(dimension_semantics=("arbitrary",),
                                             vmem_limit_bytes=VMEM_LIMIT_BYTES),
        name="mixer_pass_latent" if has_rope else "mixer_pass_context",
    )(*args)


def _pool_rcnt(seq_len):
    t = np.arange(seq_len)[:, None]
    half = np.repeat(1 << np.arange(POOL_GROUPS), POOL_GROUP_DIM)[None, :]
    cnt = np.minimum(t + half, seq_len) - np.maximum(t - half, 0)
    return jnp.asarray((1.0 / cnt).astype(np.float32))


def _rope_table(seq_len):
    rows = seq_len // GRID_W
    row = np.repeat(np.arange(rows), GRID_W).astype(np.float32)
    col = np.tile(np.arange(GRID_W), rows).astype(np.float32)
    inv = (1.0 / (np.float32(ROPE_BASE) ** (np.arange(0, AXIS_DIM, 2, dtype=np.float32) / np.float32(AXIS_DIM))))
    inv = inv.astype(np.float32)
    ang = np.concatenate([row[:, None] * inv, col[:, None] * inv], axis=-1).astype(np.float64)
    cos, sin = np.cos(ang).astype(np.float32), np.sin(ang).astype(np.float32)
    return jnp.asarray(np.concatenate([cos, cos, sin, sin], axis=-1))


def kernel(x_prompt, x_sample, cache_ckv, cache_krope, c, c_ctx, w_mod, b_mod, g_norm, w_in, g_q, w_uq,
           g_kv, w_ukv, pool_w, pool_s, conv_w, w_out, g_final):
    batch, seq, _ = x_prompt.shape
    dec_batch, dec_seq, _ = x_sample.shape
    assert 1 + dec_batch <= MOD_ROWS and dec_seq == ROWS_PER_STEP and ROWS_PER_STEP % seq == 0

    c_all = jnp.concatenate([c_ctx[None], c, jnp.zeros((MOD_ROWS - 1 - dec_batch, D_MODEL), _F32)], axis=0)
    mod_all = _modulation(c_all, w_mod, b_mod)
    w_in_r, w_out_r, w_uq_r, w_ukt, w_uv, pool_w_r = _prep_weights(
        jnp.transpose(w_in, (0, 2, 1)), w_out, w_uq, w_ukv, pool_w)
    wts = {
        "g_norm": g_norm, "w_in": w_in_r, "g_q": g_q, "w_uq": w_uq_r, "g_kv": g_kv, "w_ukt": w_ukt,
        "w_uv": w_uv, "pool_w": pool_w_r, "pool_s": pool_s, "conv_w": jnp.transpose(conv_w, (1, 0, 2)),
        "w_out": w_out_r, "g_final": g_final.reshape(1, D_MODEL),
    }

    rope = _rope_table(dec_seq)
    cache = (cache_ckv, jnp.transpose(cache_krope, (0, 1, 3, 2)))
    xp = x_prompt.reshape(batch * seq, D_MODEL)
    xs = x_sample.reshape(dec_batch * dec_seq, D_MODEL)
    xp, state_ckv, state_krope_t = _mixer_pass(xp, mod_all, wts, seq_len=seq, emit_state=True)
    (xs,) = _mixer_pass(xs, mod_all, wts, seq_len=dec_seq, cache=cache, rope=rope, emit_state=False)
    y_prompt = xp.reshape(batch, seq, D_MODEL)
    state_krope = jnp.transpose(state_krope_t, (0, 1, 3, 2))
    y_sample = xs.reshape(dec_batch, dec_seq, D_MODEL)
    return (y_prompt, y_sample, state_ckv, state_krope)
```

```python
import functools

import numpy as np
import jax
import jax.numpy as jnp
from jax import lax
from jax.experimental import pallas as pl
from jax.experimental.pallas import tpu as pltpu

D_MODEL = 1024
DEPTH = 2
GRID_W = 64
MLA_HEADS = 4
NOPE_DIM = 128
ROPE_DIM = 64
V_DIM = 128
QK_DIM = NOPE_DIM + ROPE_DIM
MLA_WIDTH = MLA_HEADS * V_DIM
Q_LORA = 384
KV_LORA = 256
POOL_GROUPS = 4
POOL_GROUP_DIM = 64
POOL_WIDTH = POOL_GROUPS * POOL_GROUP_DIM
CONV_WIDTH = 256
MIX_WIDTH = MLA_WIDTH + POOL_WIDTH + CONV_WIDTH
ROPE_BASE = 10000.0
AXIS_DIM = ROPE_DIM // 2
ATTN_SCALE = QK_DIM ** -0.5
Q_SCALE = ATTN_SCALE * float(np.log2(np.e))
EPS = 1e-6

_SPLITS = (Q_LORA, KV_LORA, ROPE_DIM, MLA_WIDTH, POOL_WIDTH, POOL_WIDTH,
           CONV_WIDTH, CONV_WIDTH, CONV_WIDTH, CONV_WIDTH)
_OFFS = [sum(_SPLITS[:i]) for i in range(len(_SPLITS) + 1)]
IN_WIDTH = _OFFS[-1]

V7X_SUBLANES = 8
V7X_MXU_DEPTH = 256
V7X_VMEM_BYTES = 64 * 1024 * 1024
HEAD_PAD = V7X_MXU_DEPTH
ROPE_PAD = HEAD_PAD - NOPE_DIM
KV_HEAD = NOPE_DIM + V_DIM

_C_CQ = 0
_C_CKV = _C_CQ + Q_LORA
_C_KR = _C_CKV + KV_LORA
_C_GMLA = _C_KR + ROPE_PAD
_C_GPOOL = _C_GMLA + MLA_WIDTH
_C_PX = _C_GPOOL + POOL_WIDTH
_C_CC = _C_PX + POOL_WIDTH
_C_CB = _C_CC + 2 * CONV_WIDTH
_C_END = _C_CB + 2 * CONV_WIDTH

MOD_ROWS = 8
ROWS_PER_STEP = 1024
ROW_CHUNK = 512
HALO = V7X_SUBLANES
PREP_COLS = 256
VMEM_LIMIT_BYTES = V7X_VMEM_BYTES - 4 * 1024 * 1024

_WEIGHTS = ("w_in", "w_uq", "w_ukt", "w_uv", "pool_w", "w_out")

_BF = jnp.bfloat16
_F32 = jnp.float32


def _dot(a, b):
    return jnp.dot(a, b, preferred_element_type=_F32)


def _rms(x, g):
    return x * lax.rsqrt(jnp.mean(x * x, axis=-1, keepdims=True) + EPS) * g


def _silu(x):
    return x * jax.nn.sigmoid(x)


def _mod_kernel(c_ref, w_ref, b_ref, o_ref):
    s = _silu(c_ref[...]).astype(_BF)
    for k in range(3):
        cols = slice(k * D_MODEL, (k + 1) * D_MODEL)
        bias = b_ref[pl.ds(pl.program_id(0), 1), cols]
        o_ref[k] = _dot(s, w_ref[:, cols].astype(_BF)) + bias


def _modulation(c_all, w_mod, b_mod):
    return pl.pallas_call(
        _mod_kernel,
        grid=(DEPTH,),
        in_specs=[
            pl.BlockSpec((MOD_ROWS, D_MODEL), lambda l: (0, 0)),
            pl.BlockSpec((None, D_MODEL, 3 * D_MODEL), lambda l: (l, 0, 0)),
            pl.BlockSpec((DEPTH, 3 * D_MODEL), lambda l: (0, 0)),
        ],
        out_specs=pl.BlockSpec((None, 3, MOD_ROWS, D_MODEL), lambda l: (l, 0, 0, 0)),
        out_shape=jax.ShapeDtypeStruct((DEPTH, 3, MOD_ROWS, D_MODEL), _F32),
        compiler_params=pltpu.CompilerParams(dimension_semantics=("arbitrary",),
                                             vmem_limit_bytes=VMEM_LIMIT_BYTES),
        name="modulation",
    )(c_all, w_mod, b_mod)


def _prep_kernel(*refs):
    n_src = 5
    src_hbm, outs = refs[:n_src], refs[n_src:n_src + 6]
    stage, sem = refs[n_src + 6:2 * n_src + 6], refs[2 * n_src + 6]
    win_o, wout_o, wuq_o, wukt_o, wuv_o, poolw_o = outs
    layer = pl.program_id(0)

    def src_copy(l, k):
        return pltpu.make_async_copy(src_hbm[k].at[l], stage[k].at[l], sem.at[l, k])

    @pl.when(layer == 0)
    def _():
        for l in range(DEPTH):
            for k in range(n_src):
                src_copy(l, k).start()

    def arrived(k):
        src_copy(layer, k).wait()
        return stage[k].at[layer]

    wint_ref = arrived(0)

    def put(dst, src, width):
        for c in range(0, width, PREP_COLS):
            n = min(PREP_COLS, width - c)
            win_o[:, dst + c:dst + c + n] = wint_ref[src + c:src + c + n, :].T.astype(_BF)

    put(_C_CQ, _OFFS[0], Q_LORA + KV_LORA)
    kr = wint_ref[_OFFS[2]:_OFFS[3], :]
    krb = jnp.concatenate([kr, -kr[AXIS_DIM:], kr[:AXIS_DIM]], axis=0)
    win_o[:, _C_KR:_C_GMLA] = krb.T.astype(_BF)
    put(_C_GMLA, _OFFS[3], MLA_WIDTH)
    put(_C_GPOOL, _OFFS[5], POOL_WIDTH)
    put(_C_PX, _OFFS[4], POOL_WIDTH)
    put(_C_CC, _OFFS[7], 2 * CONV_WIDTH)
    put(_C_CB, _OFFS[6], CONV_WIDTH)
    put(_C_CB + CONV_WIDTH, _OFFS[9], CONV_WIDTH)
    wout_o[...] = arrived(1)[...].astype(_BF)

    wuq_ref, wukv_ref, poolw_ref = arrived(2), arrived(3), arrived(4)
    for hd in range(MLA_HEADS):
        s0, d0 = hd * QK_DIM, hd * HEAD_PAD
        r0 = s0 + NOPE_DIM
        wuq_o[:, d0:d0 + QK_DIM] = wuq_ref[:, s0:s0 + QK_DIM].astype(_BF)
        wuq_o[:, d0 + QK_DIM:d0 + QK_DIM + AXIS_DIM] = (-wuq_ref[:, r0 + AXIS_DIM:r0 + ROPE_DIM]).astype(_BF)
        wuq_o[:, d0 + QK_DIM + AXIS_DIM:d0 + HEAD_PAD] = wuq_ref[:, r0:r0 + AXIS_DIM].astype(_BF)
    for hd in range(MLA_HEADS):
        k0 = hd * KV_HEAD
        wukt_o[hd * NOPE_DIM:(hd + 1) * NOPE_DIM, :] = wukv_ref[:, k0:k0 + NOPE_DIM].T.astype(_BF)
        wuv_o[:, hd * V_DIM:(hd + 1) * V_DIM] = wukv_ref[:, k0 + NOPE_DIM:k0 + KV_HEAD].astype(_BF)
    poolw_o[...] = jnp.zeros((POOL_WIDTH, POOL_WIDTH), _BF)
    for g in range(POOL_GROUPS):
        sl = slice(g * POOL_GROUP_DIM, (g + 1) * POOL_GROUP_DIM)
        poolw_o[sl, sl] = poolw_ref[g].astype(_BF)


def _prep_weights(w_in_t, w_out, w_uq, w_ukv, pool_w):
    per_layer = lambda *shape: pl.BlockSpec((None,) + shape, lambda l: (l,) + (0,) * len(shape))
    sources = (w_in_t, w_out, w_uq, w_ukv, pool_w)
    return pl.pallas_call(
        _prep_kernel,
        grid=(DEPTH,),
        in_specs=[pl.BlockSpec(memory_space=pl.ANY)] * len(sources),
        out_specs=[
            per_layer(D_MODEL, _C_END),
            per_layer(MIX_WIDTH, D_MODEL),
            per_layer(Q_LORA, MLA_HEADS * HEAD_PAD),
            per_layer(MLA_HEADS * NOPE_DIM, KV_LORA),
            per_layer(KV_LORA, MLA_WIDTH),
            per_layer(POOL_WIDTH, POOL_WIDTH),
        ],
        out_shape=[
            jax.ShapeDtypeStruct((DEPTH, D_MODEL, _C_END), _BF),
            jax.ShapeDtypeStruct((DEPTH, MIX_WIDTH, D_MODEL), _BF),
            jax.ShapeDtypeStruct((DEPTH, Q_LORA, MLA_HEADS * HEAD_PAD), _BF),
            jax.ShapeDtypeStruct((DEPTH, MLA_HEADS * NOPE_DIM, KV_LORA), _BF),
            jax.ShapeDtypeStruct((DEPTH, KV_LORA, MLA_WIDTH), _BF),
            jax.ShapeDtypeStruct((DEPTH, POOL_WIDTH, POOL_WIDTH), _BF),
        ],
        scratch_shapes=[pltpu.VMEM(w.shape, w.dtype) for w in sources]
        + [pltpu.SemaphoreType.DMA((DEPTH, len(sources)))],
        compiler_params=pltpu.CompilerParams(dimension_semantics=("arbitrary",),
                                             vmem_limit_bytes=VMEM_LIMIT_BYTES),
        name="weight_prep",
    )(*sources)


def _pool_mix(win, rcnt):
    n = win.shape[0]
    rows = n - 2 * HALO
    lane = lax.broadcasted_iota(jnp.int32, (1, POOL_WIDTH), 1)
    sums = [win + pltpu.roll(win, 1, axis=0)]
    for k in (1, 2, 4):
        sums.append(pltpu.roll(sums[-1], k, axis=0) + pltpu.roll(sums[-1], n - k, axis=0))
    sel = sums[-1]
    for g in range(POOL_GROUPS - 2, -1, -1):
        sel = jnp.where(lane < (g + 1) * POOL_GROUP_DIM, sums[g], sel)
    return sel[HALO:HALO + rows] * rcnt - win[HALO:HALO + rows]


def _short_conv(win, w):
    rows = win.shape[0] - 2 * HALO
    return (win[HALO - 1:HALO - 1 + rows] * w[0] + win[HALO:HALO + rows] * w[1]
            + win[HALO + 1:HALO + 1 + rows] * w[2])


def _pass_kernel(*refs, n_seq, seq_len, cache_len, has_rope, emit_state, rc):
    it = iter(refs)
    x_ref, mod_ref = next(it), next(it)
    if cache_len:
        cckv_all_ref, ckr_all_ref = next(it), next(it)
    if has_rope:
        cs_ref = next(it)
    rcnt_ref, gn_ref, gq_ref, gkv_ref, pools_ref, convw_ref, gfin_ref = (next(it) for _ in range(7))
    w_hbm = [next(it) for _ in range(len(_WEIGHTS))]
    y_ref = next(it)
    if emit_state:
        ckv_out_ref, kr_out_ref = next(it), next(it)
    h_s, q_s, kt_s, v_s, px_s, prod_s, mixed_s = (next(it) for _ in range(7))
    w_vm = [next(it) for _ in range(len(_WEIGHTS))]
    w_sem = next(it)
    win_all_ref, wuq_all_ref, wukt_all_ref, wuv_all_ref, poolw_all_ref, wout_all_ref = w_vm

    def weight_copy(l, k):
        return pltpu.make_async_copy(w_hbm[k].at[l], w_vm[k].at[l], w_sem.at[l, k])

    first_step = pl.program_id(0) == 0

    @pl.when(first_step)
    def _():
        for l in range(DEPTH):
            for k in range(len(_WEIGHTS)):
                weight_copy(l, k).start()

    m = n_seq * seq_len
    keys = cache_len + seq_len
    padded = seq_len + 2 * HALO
    lane = lax.broadcasted_iota(jnp.int32, (1, ROPE_PAD), 1)
    mrow = (1 + pl.program_id(0)) if has_rope else 0
    piece = min(seq_len, rc)
    n_sub = rc // piece
    pieces = [(s, slice(s * piece, (s + 1) * piece)) for s in range(n_sub)]
    chunks = range(m // rc)

    for c in chunks:
        y_ref[c * rc:(c + 1) * rc, :] = x_ref[c * rc:(c + 1) * rc, :]

    zeros_halo = jnp.zeros((HALO, POOL_WIDTH), _F32)
    for buf in (px_s, prod_s):
        for s in range(n_seq):
            buf[s * padded:s * padded + HALO, :] = zeros_halo
            buf[(s + 1) * padded - HALO:(s + 1) * padded, :] = zeros_halo
    ones_col = jnp.where(lax.broadcasted_iota(jnp.int32, (n_seq * keys, V_DIM), 1) == 0, 1.0, 0.0).astype(_BF)
    for hd in range(MLA_HEADS):
        v_s[hd, :, V_DIM:] = ones_col

    def layer_view(l):
        lw = dict(l=l, row=pl.ds(l, 1), win=win_all_ref.at[l], wuq=wuq_all_ref.at[l], wukt=wukt_all_ref.at[l],
                  wuv=wuv_all_ref.at[l], poolw=poolw_all_ref.at[l], wout=wout_all_ref.at[l])
        lw["shift"], lw["scale"], lw["gate"] = (mod_ref[l, k, pl.ds(mrow, 1), :] for k in range(3))
        if cache_len:
            lw["cckv"], lw["ckr"] = cckv_all_ref.at[l], ckr_all_ref.at[l]
        return lw

    def rotate(v, r):
        v = v * cs_ref[pl.ds(r, rc), :]
        return v + pltpu.roll(v, ROPE_DIM, axis=1)

    def put_kv(lw, ckv, kr_t, kc):
        knt = _dot(lw["wukt"][...], ckv.T.astype(_BF))
        v4 = _dot(ckv.astype(_BF), lw["wuv"][...])
        kr_bf = kr_t.astype(_BF)
        rows_k = pl.ds(kc * rc, rc)
        for hd in range(MLA_HEADS):
            kt_s[hd, kc, :NOPE_DIM, :] = knt[hd * NOPE_DIM:(hd + 1) * NOPE_DIM, :].astype(_BF)
            kt_s[hd, kc, NOPE_DIM:, :] = kr_bf
            v_s[hd, rows_k, :V_DIM] = v4[:, hd * V_DIM:(hd + 1) * V_DIM].astype(_BF)

    def cached_keys(lw):
        ckr_t = lw["ckr"][...]
        ckr_pad = jnp.concatenate([ckr_t, jnp.zeros_like(ckr_t)], axis=0)
        for cc in range(cache_len // rc):
            rows_k = slice(cc * rc, (cc + 1) * rc)
            put_kv(lw, lw["cckv"][rows_k, :], ckr_pad[:, rows_k], cc)

    def halo_base(c, s):
        return (c * n_sub + s) * padded if seq_len <= rc else c * rc

    def project(lw, c):
        r = c * rc
        rows = pl.ds(r, rc)
        row, win_ref = lw["row"], lw["win"]
        h = (_rms(y_ref[rows, :], gn_ref[row, :]) * (1.0 + lw["scale"]) + lw["shift"]).astype(_BF)
        h_s[rows, :] = h
        a = _dot(h, win_ref[:, _C_CQ:_C_GMLA])
        cq = _rms(a[:, :Q_LORA], gq_ref[row, :]).astype(_BF)
        ckv = _rms(a[:, _C_CKV:_C_KR], gkv_ref[row, :])
        kr = a[:, _C_KR:]
        if has_rope:
            kr = rotate(kr, r)
        kr_t = jnp.where(lane < ROPE_DIM, kr, 0.0).T
        if emit_state:
            for s, ps in pieces:
                ckv_out_ref[c * n_sub + s, lw["l"]] = ckv[ps, :]
                kr_out_ref[c * n_sub + s, lw["l"]] = kr_t[:ROPE_DIM, ps]
        put_kv(lw, ckv, kr_t, cache_len // rc + c)
        for hp in range(MLA_HEADS // 2):
            q2 = _dot(cq, lw["wuq"][:, hp * 2 * HEAD_PAD:(hp + 1) * 2 * HEAD_PAD]) * Q_SCALE
            for j in range(2):
                hd = 2 * hp + j
                c0 = j * HEAD_PAD
                q_s[hd, rows, :NOPE_DIM] = q2[:, c0:c0 + NOPE_DIM].astype(_BF)
                qr = q2[:, c0 + NOPE_DIM:c0 + HEAD_PAD]
                if has_rope:
                    qr = rotate(qr, r)
                q_s[hd, rows, NOPE_DIM:] = qr.astype(_BF)
        pe = _dot(h, win_ref[:, _C_PX:_C_CB])
        for s, ps in pieces:
            rows_h = pl.ds(halo_base(c, s) + HALO, piece)
            px_s[rows_h, :] = pe[ps, :POOL_WIDTH]
            prod_s[rows_h, :] = pe[ps, POOL_WIDTH:POOL_WIDTH + CONV_WIDTH] * pe[ps, POOL_WIDTH + CONV_WIDTH:]

    def mix(lw, c):
        r = c * rc
        rows = pl.ds(r, rc)
        row, win_ref = lw["row"], lw["win"]
        h = h_s[rows, :]
        for hp in range(MLA_HEADS // 2):
            g2 = _silu(_dot(h, win_ref[:, _C_GMLA + hp * 2 * V_DIM:_C_GMLA + (hp + 1) * 2 * V_DIM]))
            for j in range(2):
                hd = 2 * hp + j
                for s, ps in pieces:
                    rows_q = pl.ds(r + s * piece, piece)
                    if n_seq == 1:
                        ks = [kt_s[hd, kc] for kc in range(keys // rc)]
                        rows_k = pl.ds(0, keys)
                    else:
                        ks = [kt_s[hd, c, :, ps]]
                        rows_k = rows_q
                    q = q_s[hd, rows_q, :]
                    sc = jnp.concatenate([_dot(q, k) for k in ks], axis=-1)
                    p = jnp.exp2(sc - jnp.max(sc, axis=-1, keepdims=True))
                    ov = _dot(p.astype(_BF), v_s[hd, rows_k, :])
                    o = ov[:, :V_DIM] / ov[:, V_DIM:V_DIM + 1]
                    mixed_s[rows_q, hd * V_DIM:(hd + 1) * V_DIM] = (g2[ps, j * V_DIM:(j + 1) * V_DIM] * o).astype(_BF)

        windows = [pl.ds(halo_base(c, s), piece + 2 * HALO) for s, _ in pieces]
        rcnt = rcnt_ref[pl.ds(0 if seq_len <= rc else r, piece), :]
        pooled = jnp.concatenate([_pool_mix(px_s[w, :], rcnt) for w in windows], axis=0)
        pool = _dot(pooled.astype(_BF), lw["poolw"][...]) * pools_ref[row, :]
        gp = _silu(_dot(h, win_ref[:, _C_GPOOL:_C_PX]))
        mixed_s[rows, MLA_WIDTH:MLA_WIDTH + POOL_WIDTH] = (gp * pool).astype(_BF)

        convw = [convw_ref[k, row, :] for k in range(3)]
        conv = jnp.concatenate([_short_conv(prod_s[w, :], convw) for w in windows], axis=0)
        e2 = _dot(h, win_ref[:, _C_CB:_C_END])
        mixed_s[rows, MLA_WIDTH + POOL_WIDTH:] = (_silu(e2[:, CONV_WIDTH:]) * (e2[:, :CONV_WIDTH] * conv)).astype(_BF)

        y_ref[rows, :] = y_ref[rows, :] + lw["gate"] * _dot(mixed_s[rows, :], lw["wout"][...])

    def one_layer(l, carry):
        @pl.when(first_step)
        def _():
            for k in range(len(_WEIGHTS)):
                weight_copy(l, k).wait()

        lw = layer_view(l)
        if cache_len:
            cached_keys(lw)
        for c in chunks:
            project(lw, c)
        for c in chunks:
            mix(lw, c)
        return carry

    lax.fori_loop(0, DEPTH, one_layer, 0)
    for c in chunks:
        rows = slice(c * rc, (c + 1) * rc)
        y_ref[rows, :] = _rms(y_ref[rows, :], gfin_ref[...])


def _mixer_pass(x2d, mod_all, wts, *, seq_len, cache=None, rope=None, emit_state):
    rows = x2d.shape[0]
    m = ROWS_PER_STEP
    n_seq = m // seq_len
    n_steps = rows // m
    n_all = rows // seq_len
    cache_len = 0 if cache is None else cache[0].shape[2]
    keys = cache_len + seq_len
    has_rope = rope is not None
    rc = ROW_CHUNK
    assert m % rc == 0 and cache_len % rc == 0
    assert (rc % seq_len == 0 and not cache_len) or (n_seq == 1 and seq_len % rc == 0 and not emit_state)

    def const(*shape):
        return pl.BlockSpec(shape, lambda i: (0,) * len(shape), pipeline_mode=pl.Buffered(1))

    args = [x2d, mod_all]
    in_specs = [pl.BlockSpec((m, D_MODEL), lambda i: (i, 0)), const(DEPTH, 3, MOD_ROWS, D_MODEL)]
    if cache_len:
        args += [cache[0], cache[1]]
        in_specs += [pl.BlockSpec((None, DEPTH, cache_len, KV_LORA), lambda i: (i, 0, 0, 0)),
                     pl.BlockSpec((None, DEPTH, ROPE_DIM, cache_len), lambda i: (i, 0, 0, 0))]
    if has_rope:
        args += [rope]
        in_specs += [const(seq_len, ROPE_PAD)]
    args += [_pool_rcnt(seq_len), wts["g_norm"], wts["g_q"], wts["g_kv"], wts["pool_s"], wts["conv_w"],
             wts["g_final"]]
    in_specs += [const(seq_len, POOL_WIDTH), const(DEPTH, D_MODEL), const(DEPTH, Q_LORA), const(DEPTH, KV_LORA),
                 const(DEPTH, POOL_WIDTH), const(3, DEPTH, CONV_WIDTH), const(1, D_MODEL)]
    weights = [wts[name] for name in _WEIGHTS]
    args += weights
    in_specs += [pl.BlockSpec(memory_space=pl.ANY)] * len(weights)

    out_shape = [jax.ShapeDtypeStruct((rows, D_MODEL), _F32)]
    out_specs = [pl.BlockSpec((m, D_MODEL), lambda i: (i, 0))]
    if emit_state:
        out_shape += [jax.ShapeDtypeStruct((n_all, DEPTH, seq_len, KV_LORA), _F32),
                      jax.ShapeDtypeStruct((n_all, DEPTH, ROPE_DIM, seq_len), _F32)]
        out_specs += [pl.BlockSpec((n_seq, DEPTH, seq_len, KV_LORA), lambda i: (i, 0, 0, 0)),
                      pl.BlockSpec((n_seq, DEPTH, ROPE_DIM, seq_len), lambda i: (i, 0, 0, 0))]

    kern = functools.partial(_pass_kernel, n_seq=n_seq, seq_len=seq_len, cache_len=cache_len,
                             has_rope=has_rope, emit_state=emit_state, rc=rc)
    return pl.pallas_call(
        kern,
        grid=(n_steps,),
        in_specs=in_specs,
        out_specs=out_specs,
        out_shape=out_shape,
        scratch_shapes=[
            pltpu.VMEM((m, D_MODEL), _BF),
            pltpu.VMEM((MLA_HEADS, m, HEAD_PAD), _BF),
            pltpu.VMEM((MLA_HEADS, n_seq * keys // rc, HEAD_PAD, rc), _BF),
            pltpu.VMEM((MLA_HEADS, n_seq * keys, 2 * V_DIM), _BF),
            pltpu.VMEM((n_seq * (seq_len + 2 * HALO), POOL_WIDTH), _F32),
            pltpu.VMEM((n_seq * (seq_len + 2 * HALO), CONV_WIDTH), _F32),
            pltpu.VMEM((m, MIX_WIDTH), _BF),
        ] + [pltpu.VMEM(w.shape, w.dtype) for w in weights] + [pltpu.SemaphoreType.DMA((DEPTH, len(weights)))],
        compiler_params=pltpu.CompilerParams(dimension_semantics=("arbitrary",),
                                             vmem_limit_bytes=VMEM_LIMIT_BYTES),
        name="mixer_pass_latent" if has_rope else "mixer_pass_context",
    )(*args)


def _pool_rcnt(seq_len):
    t = np.arange(seq_len)[:, None]
    half = np.repeat(1 << np.arange(POOL_GROUPS), POOL_GROUP_DIM)[None, :]
    cnt = np.minimum(t + half, seq_len) - np.maximum(t - half, 0)
    return jnp.asarray((1.0 / cnt).astype(np.float32))


def _rope_table(seq_len):
    rows = seq_len // GRID_W
    row = np.repeat(np.arange(rows), GRID_W).astype(np.float32)
    col = np.tile(np.arange(GRID_W), rows).astype(np.float32)
    inv = (1.0 / (np.float32(ROPE_BASE) ** (np.arange(0, AXIS_DIM, 2, dtype=np.float32) / np.float32(AXIS_DIM))))
    inv = inv.astype(np.float32)
    ang = np.concatenate([row[:, None] * inv, col[:, None] * inv], axis=-1).astype(np.float64)
    cos, sin = np.cos(ang).astype(np.float32), np.sin(ang).astype(np.float32)
    return jnp.asarray(np.concatenate([cos, cos, sin, sin], axis=-1))


def kernel(x_prompt, x_sample, cache_ckv, cache_krope, c, c_ctx, w_mod, b_mod, g_norm, w_in, g_q, w_uq,
           g_kv, w_ukv, pool_w, pool_s, conv_w, w_out, g_final):
    batch, seq, _ = x_prompt.shape
    dec_batch, dec_seq, _ = x_sample.shape
    assert 1 + dec_batch <= MOD_ROWS and dec_seq == ROWS_PER_STEP and ROWS_PER_STEP % seq == 0

    c_all = jnp.concatenate([c_ctx[None], c, jnp.zeros((MOD_ROWS - 1 - dec_batch, D_MODEL), _F32)], axis=0)
    mod_all = _modulation(c_all, w_mod, b_mod)
    w_in_r, w_out_r, w_uq_r, w_ukt, w_uv, pool_w_r = _prep_weights(
        jnp.transpose(w_in, (0, 2, 1)), w_out, w_uq, w_ukv, pool_w)
    wts = {
        "g_norm": g_norm, "w_in": w_in_r, "g_q": g_q, "w_uq": w_uq_r, "g_kv": g_kv, "w_ukt": w_ukt,
        "w_uv": w_uv, "pool_w": pool_w_r, "pool_s": pool_s, "conv_w": jnp.transpose(conv_w, (1, 0, 2)),
        "w_out": w_out_r, "g_final": g_final.reshape(1, D_MODEL),
    }

    rope = _rope_table(dec_seq)
    cache = (cache_ckv, jnp.transpose(cache_krope, (0, 1, 3, 2)))
    xp = x_prompt.reshape(batch * seq, D_MODEL)
    xs = x_sample.reshape(dec_batch * dec_seq, D_MODEL)
    xp, state_ckv, state_krope_t = _mixer_pass(xp, mod_all, wts, seq_len=seq, emit_state=True)
    (xs,) = _mixer_pass(xs, mod_all, wts, seq_len=dec_seq, cache=cache, rope=rope, emit_state=False)
    y_prompt = xp.reshape(batch, seq, D_MODEL)
    state_krope = jnp.transpose(state_krope_t, (0, 1, 3, 2))
    y_sample = xs.reshape(dec_batch, dec_seq, D_MODEL)
    return (y_prompt, y_sample, state_ckv, state_krope)
```

```python
import functools

import numpy as np
import jax
import jax.numpy as jnp
from jax import lax
from jax.experimental import pallas as pl
from jax.experimental.pallas import tpu as pltpu

D_MODEL = 1024
DEPTH = 2
GRID_W = 64
MLA_HEADS = 4
NOPE_DIM = 128
ROPE_DIM = 64
V_DIM = 128
QK_DIM = NOPE_DIM + ROPE_DIM
MLA_WIDTH = MLA_HEADS * V_DIM
Q_LORA = 384
KV_LORA = 256
POOL_GROUPS = 4
POOL_GROUP_DIM = 64
POOL_WIDTH = POOL_GROUPS * POOL_GROUP_DIM
CONV_WIDTH = 256
MIX_WIDTH = MLA_WIDTH + POOL_WIDTH + CONV_WIDTH
ROPE_BASE = 10000.0
AXIS_DIM = ROPE_DIM // 2
ATTN_SCALE = QK_DIM ** -0.5
Q_SCALE = ATTN_SCALE * float(np.log2(np.e))
EPS = 1e-6

_SPLITS = (Q_LORA, KV_LORA, ROPE_DIM, MLA_WIDTH, POOL_WIDTH, POOL_WIDTH,
           CONV_WIDTH, CONV_WIDTH, CONV_WIDTH, CONV_WIDTH)
_OFFS = [sum(_SPLITS[:i]) for i in range(len(_SPLITS) + 1)]
IN_WIDTH = _OFFS[-1]

V7X_SUBLANES = 8
V7X_MXU_DEPTH = 256
V7X_VMEM_BYTES = 64 * 1024 * 1024
HEAD_PAD = V7X_MXU_DEPTH
ROPE_PAD = HEAD_PAD - NOPE_DIM
KV_HEAD = NOPE_DIM + V_DIM

_C_CQ = 0
_C_CKV = _C_CQ + Q_LORA
_C_KR = _C_CKV + KV_LORA
_C_GMLA = _C_KR + ROPE_PAD
_C_GPOOL = _C_GMLA + MLA_WIDTH
_C_PX = _C_GPOOL + POOL_WIDTH
_C_CC = _C_PX + POOL_WIDTH
_C_CB = _C_CC + 2 * CONV_WIDTH
_C_END = _C_CB + 2 * CONV_WIDTH

MOD_ROWS = 8
ROWS_PER_STEP = 1024
ROW_CHUNK = 512
HALO = V7X_SUBLANES
PREP_COLS = 256
VMEM_LIMIT_BYTES = V7X_VMEM_BYTES - 4 * 1024 * 1024

_WEIGHTS = ("w_in", "w_uq", "w_ukt", "w_uv", "pool_w", "w_out")

_BF = jnp.bfloat16
_F32 = jnp.float32


def _dot(a, b):
    return jnp.dot(a, b, preferred_element_type=_F32)


def _rms(x, g):
    return x * lax.rsqrt(jnp.mean(x * x, axis=-1, keepdims=True) + EPS) * g


def _silu(x):
    return x * jax.nn.sigmoid(x)


def _mod_kernel(c_ref, w_ref, b_ref, o_ref):
    s = _silu(c_ref[...]).astype(_BF)
    for k in range(3):
        cols = slice(k * D_MODEL, (k + 1) * D_MODEL)
        bias = b_ref[pl.ds(pl.program_id(0), 1), cols]
        o_ref[k] = _dot(s, w_ref[:, cols].astype(_BF)) + bias


def _modulation(c_all, w_mod, b_mod):
    return pl.pallas_call(
        _mod_kernel,
        grid=(DEPTH,),
        in_specs=[
            pl.BlockSpec((MOD_ROWS, D_MODEL), lambda l: (0, 0)),
            pl.BlockSpec((None, D_MODEL, 3 * D_MODEL), lambda l: (l, 0, 0)),
            pl.BlockSpec((DEPTH, 3 * D_MODEL), lambda l: (0, 0)),
        ],
        out_specs=pl.BlockSpec((None, 3, MOD_ROWS, D_MODEL), lambda l: (l, 0, 0, 0)),
        out_shape=jax.ShapeDtypeStruct((DEPTH, 3, MOD_ROWS, D_MODEL), _F32),
        compiler_params=pltpu.CompilerParams(dimension_semantics=("arbitrary",),
                                             vmem_limit_bytes=VMEM_LIMIT_BYTES),
        name="modulation",
    )(c_all, w_mod, b_mod)


def _prep_kernel(wint_ref, wout_ref, wuq_ref, wukv_ref, poolw_ref,
                 win_o, wout_o, wuq_o, wukt_o, wuv_o, poolw_o):
    def put(dst, src, width):
        for c in range(0, width, PREP_COLS):
            n = min(PREP_COLS, width - c)
            win_o[:, dst + c:dst + c + n] = wint_ref[src + c:src + c + n, :].astype(_BF).T

    put(_C_CQ, _OFFS[0], Q_LORA + KV_LORA)
    kr = wint_ref[_OFFS[2]:_OFFS[3], :]
    krb = jnp.concatenate([kr, -kr[AXIS_DIM:], kr[:AXIS_DIM]], axis=0)
    win_o[:, _C_KR:_C_GMLA] = krb.T.astype(_BF)
    put(_C_GMLA, _OFFS[3], MLA_WIDTH)
    put(_C_GPOOL, _OFFS[5], POOL_WIDTH)
    put(_C_PX, _OFFS[4], POOL_WIDTH)
    put(_C_CC, _OFFS[7], 2 * CONV_WIDTH)
    put(_C_CB, _OFFS[6], CONV_WIDTH)
    put(_C_CB + CONV_WIDTH, _OFFS[9], CONV_WIDTH)
    wout_o[...] = wout_ref[...].astype(_BF)

    for hd in range(MLA_HEADS):
        s0, d0 = hd * QK_DIM, hd * HEAD_PAD
        r0 = s0 + NOPE_DIM
        wuq_o[:, d0:d0 + QK_DIM] = wuq_ref[:, s0:s0 + QK_DIM].astype(_BF)
        wuq_o[:, d0 + QK_DIM:d0 + QK_DIM + AXIS_DIM] = (-wuq_ref[:, r0 + AXIS_DIM:r0 + ROPE_DIM]).astype(_BF)
        wuq_o[:, d0 + QK_DIM + AXIS_DIM:d0 + HEAD_PAD] = wuq_ref[:, r0:r0 + AXIS_DIM].astype(_BF)
    for hd in range(MLA_HEADS):
        k0 = hd * KV_HEAD
        wukt_o[hd * NOPE_DIM:(hd + 1) * NOPE_DIM, :] = wukv_ref[:, k0:k0 + NOPE_DIM].T.astype(_BF)
        wuv_o[:, hd * V_DIM:(hd + 1) * V_DIM] = wukv_ref[:, k0 + NOPE_DIM:k0 + KV_HEAD].astype(_BF)
    poolw_o[...] = jnp.zeros((POOL_WIDTH, POOL_WIDTH), _BF)
    for g in range(POOL_GROUPS):
        sl = slice(g * POOL_GROUP_DIM, (g + 1) * POOL_GROUP_DIM)
        poolw_o[sl, sl] = poolw_ref[g].astype(_BF)


def _prep_weights(w_in_t, w_out, w_uq, w_ukv, pool_w):
    per_layer = lambda *shape: pl.BlockSpec((None,) + shape, lambda l: (l,) + (0,) * len(shape))
    return pl.pallas_call(
        _prep_kernel,
        grid=(DEPTH,),
        in_specs=[
            per_layer(IN_WIDTH, D_MODEL),
            per_layer(MIX_WIDTH, D_MODEL),
            per_layer(Q_LORA, MLA_HEADS * QK_DIM),
            per_layer(KV_LORA, MLA_HEADS * KV_HEAD),
            per_layer(POOL_GROUPS, POOL_GROUP_DIM, POOL_GROUP_DIM),
        ],
        out_specs=[
            per_layer(D_MODEL, _C_END),
            per_layer(MIX_WIDTH, D_MODEL),
            per_layer(Q_LORA, MLA_HEADS * HEAD_PAD),
            per_layer(MLA_HEADS * NOPE_DIM, KV_LORA),
            per_layer(KV_LORA, MLA_WIDTH),
            per_layer(POOL_WIDTH, POOL_WIDTH),
        ],
        out_shape=[
            jax.ShapeDtypeStruct((DEPTH, D_MODEL, _C_END), _BF),
            jax.ShapeDtypeStruct((DEPTH, MIX_WIDTH, D_MODEL), _BF),
            jax.ShapeDtypeStruct((DEPTH, Q_LORA, MLA_HEADS * HEAD_PAD), _BF),
            jax.ShapeDtypeStruct((DEPTH, MLA_HEADS * NOPE_DIM, KV_LORA), _BF),
            jax.ShapeDtypeStruct((DEPTH, KV_LORA, MLA_WIDTH), _BF),
            jax.ShapeDtypeStruct((DEPTH, POOL_WIDTH, POOL_WIDTH), _BF),
        ],
        compiler_params=pltpu.CompilerParams(dimension_semantics=("arbitrary",),
                                             vmem_limit_bytes=VMEM_LIMIT_BYTES),
        name="weight_prep",
    )(w_in_t, w_out, w_uq, w_ukv, pool_w)


def _pool_mix(win, rcnt):
    n = win.shape[0]
    rows = n - 2 * HALO
    lane = lax.broadcasted_iota(jnp.int32, (1, POOL_WIDTH), 1)
    sums = [win + pltpu.roll(win, 1, axis=0)]
    for k in (1, 2, 4):
        sums.append(pltpu.roll(sums[-1], k, axis=0) + pltpu.roll(sums[-1], n - k, axis=0))
    sel = sums[-1]
    for g in range(POOL_GROUPS - 2, -1, -1):
        sel = jnp.where(lane < (g + 1) * POOL_GROUP_DIM, sums[g], sel)
    return sel[HALO:HALO + rows] * rcnt - win[HALO:HALO + rows]


def _short_conv(win, w):
    rows = win.shape[0] - 2 * HALO
    return (win[HALO - 1:HALO - 1 + rows] * w[0] + win[HALO:HALO + rows] * w[1]
            + win[HALO + 1:HALO + 1 + rows] * w[2])


def _pass_kernel(*refs, n_seq, seq_len, cache_len, has_rope, emit_state, rc):
    it = iter(refs)
    x_ref, mod_ref = next(it), next(it)
    if cache_len:
        cckv_all_ref, ckr_all_ref = next(it), next(it)
    if has_rope:
        cs_ref = next(it)
    rcnt_ref, gn_ref, gq_ref, gkv_ref, pools_ref, convw_ref, gfin_ref = (next(it) for _ in range(7))
    w_hbm = [next(it) for _ in range(len(_WEIGHTS))]
    y_ref = next(it)
    if emit_state:
        ckv_out_ref, kr_out_ref = next(it), next(it)
    h_s, q_s, kt_s, v_s, px_s, prod_s, mixed_s = (next(it) for _ in range(7))
    w_vm = [next(it) for _ in range(len(_WEIGHTS))]
    w_sem = next(it)
    win_all_ref, wuq_all_ref, wukt_all_ref, wuv_all_ref, poolw_all_ref, wout_all_ref = w_vm

    def weight_copy(l, k):
        return pltpu.make_async_copy(w_hbm[k].at[l], w_vm[k].at[l], w_sem.at[l, k])

    first_step = pl.program_id(0) == 0

    @pl.when(first_step)
    def _():
        for l in range(DEPTH):
            for k in range(len(_WEIGHTS)):
                weight_copy(l, k).start()

    m = n_seq * seq_len
    keys = cache_len + seq_len
    padded = seq_len + 2 * HALO
    lane = lax.broadcasted_iota(jnp.int32, (1, ROPE_PAD), 1)
    mrow = (1 + pl.program_id(0)) if has_rope else 0
    piece = min(seq_len, rc)
    n_sub = rc // piece
    pieces = [(s, slice(s * piece, (s + 1) * piece)) for s in range(n_sub)]
    chunks = range(m // rc)

    for c in chunks:
        y_ref[c * rc:(c + 1) * rc, :] = x_ref[c * rc:(c + 1) * rc, :]

    zeros_halo = jnp.zeros((HALO, POOL_WIDTH), _F32)
    for buf in (px_s, prod_s):
        for s in range(n_seq):
            buf[s * padded:s * padded + HALO, :] = zeros_halo
            buf[(s + 1) * padded - HALO:(s + 1) * padded, :] = zeros_halo
    ones_col = jnp.where(lax.broadcasted_iota(jnp.int32, (n_seq * keys, V_DIM), 1) == 0, 1.0, 0.0).astype(_BF)
    for hd in range(MLA_HEADS):
        v_s[hd, :, V_DIM:] = ones_col

    def layer_view(l):
        lw = dict(l=l, row=pl.ds(l, 1), win=win_all_ref.at[l], wuq=wuq_all_ref.at[l], wukt=wukt_all_ref.at[l],
                  wuv=wuv_all_ref.at[l], poolw=poolw_all_ref.at[l], wout=wout_all_ref.at[l])
        lw["shift"], lw["scale"], lw["gate"] = (mod_ref[l, k, pl.ds(mrow, 1), :] for k in range(3))
        if cache_len:
            lw["cckv"], lw["ckr"] = cckv_all_ref.at[l], ckr_all_ref.at[l]
        return lw

    def rotate(v, r):
        v = v * cs_ref[pl.ds(r, rc), :]
        return v + pltpu.roll(v, ROPE_DIM, axis=1)

    def put_kv(lw, ckv, kr_t, kc):
        knt = _dot(lw["wukt"][...], ckv.T.astype(_BF))
        v4 = _dot(ckv.astype(_BF), lw["wuv"][...])
        kr_bf = kr_t.astype(_BF)
        rows_k = pl.ds(kc * rc, rc)
        for hd in range(MLA_HEADS):
            kt_s[hd, kc, :NOPE_DIM, :] = knt[hd * NOPE_DIM:(hd + 1) * NOPE_DIM, :].astype(_BF)
            kt_s[hd, kc, NOPE_DIM:, :] = kr_bf
            v_s[hd, rows_k, :V_DIM] = v4[:, hd * V_DIM:(hd + 1) * V_DIM].astype(_BF)

    def cached_keys(lw):
        ckr_t = lw["ckr"][...]
        ckr_pad = jnp.concatenate([ckr_t, jnp.zeros_like(ckr_t)], axis=0)
        for cc in range(cache_len // rc):
            rows_k = slice(cc * rc, (cc + 1) * rc)
            put_kv(lw, lw["cckv"][rows_k, :], ckr_pad[:, rows_k], cc)

    def halo_base(c, s):
        return (c * n_sub + s) * padded if seq_len <= rc else c * rc

    def project(lw, c):
        r = c * rc
        rows = pl.ds(r, rc)
        row, win_ref = lw["row"], lw["win"]
        h = (_rms(y_ref[rows, :], gn_ref[row, :]) * (1.0 + lw["scale"]) + lw["shift"]).astype(_BF)
        h_s[rows, :] = h
        a = _dot(h, win_ref[:, _C_CQ:_C_GMLA])
        cq = _rms(a[:, :Q_LORA], gq_ref[row, :]).astype(_BF)
        ckv = _rms(a[:, _C_CKV:_C_KR], gkv_ref[row, :])
        kr = a[:, _C_KR:]
        if has_rope:
            kr = rotate(kr, r)
        kr_t = jnp.where(lane < ROPE_DIM, kr, 0.0).T
        if emit_state:
            for s, ps in pieces:
                ckv_out_ref[c * n_sub + s, lw["l"]] = ckv[ps, :]
                kr_out_ref[c * n_sub + s, lw["l"]] = kr_t[:ROPE_DIM, ps]
        put_kv(lw, ckv, kr_t, cache_len // rc + c)
        for hp in range(MLA_HEADS // 2):
            q2 = _dot(cq, lw["wuq"][:, hp * 2 * HEAD_PAD:(hp + 1) * 2 * HEAD_PAD]) * Q_SCALE
            for j in range(2):
                hd = 2 * hp + j
                c0 = j * HEAD_PAD
                q_s[hd, rows, :NOPE_DIM] = q2[:, c0:c0 + NOPE_DIM].astype(_BF)
                qr = q2[:, c0 + NOPE_DIM:c0 + HEAD_PAD]
                if has_rope:
                    qr = rotate(qr, r)
                q_s[hd, rows, NOPE_DIM:] = qr.astype(_BF)
        pe = _dot(h, win_ref[:, _C_PX:_C_CB])
        for s, ps in pieces:
            rows_h = pl.ds(halo_base(c, s) + HALO, piece)
            px_s[rows_h, :] = pe[ps, :POOL_WIDTH]
            prod_s[rows_h, :] = pe[ps, POOL_WIDTH:POOL_WIDTH + CONV_WIDTH] * pe[ps, POOL_WIDTH + CONV_WIDTH:]

    def mix(lw, c):
        r = c * rc
        rows = pl.ds(r, rc)
        row, win_ref = lw["row"], lw["win"]
        h = h_s[rows, :]
        for hp in range(MLA_HEADS // 2):
            g2 = _silu(_dot(h, win_ref[:, _C_GMLA + hp * 2 * V_DIM:_C_GMLA + (hp + 1) * 2 * V_DIM]))
            for j in range(2):
                hd = 2 * hp + j
                for s, ps in pieces:
                    rows_q = pl.ds(r + s * piece, piece)
                    if n_seq == 1:
                        ks = [kt_s[hd, kc] for kc in range(keys // rc)]
                        rows_k = pl.ds(0, keys)
                    else:
                        ks = [kt_s[hd, c, :, ps]]
                        rows_k = rows_q
                    q = q_s[hd, rows_q, :]
                    sc = jnp.concatenate([_dot(q, k) for k in ks], axis=-1)
                    p = jnp.exp2(sc - jnp.max(sc, axis=-1, keepdims=True))
                    ov = _dot(p.astype(_BF), v_s[hd, rows_k, :])
                    o = ov[:, :V_DIM] / ov[:, V_DIM:V_DIM + 1]
                    mixed_s[rows_q, hd * V_DIM:(hd + 1) * V_DIM] = (g2[ps, j * V_DIM:(j + 1) * V_DIM] * o).astype(_BF)

        windows = [pl.ds(halo_base(c, s), piece + 2 * HALO) for s, _ in pieces]
        rcnt = rcnt_ref[pl.ds(0 if seq_len <= rc else r, piece), :]
        pooled = jnp.concatenate([_pool_mix(px_s[w, :], rcnt) for w in windows], axis=0)
        pool = _dot(pooled.astype(_BF), lw["poolw"][...]) * pools_ref[row, :]
        gp = _silu(_dot(h, win_ref[:, _C_GPOOL:_C_PX]))
        mixed_s[rows, MLA_WIDTH:MLA_WIDTH + POOL_WIDTH] = (gp * pool).astype(_BF)

        convw = [convw_ref[k, row, :] for k in range(3)]
        conv = jnp.concatenate([_short_conv(prod_s[w, :], convw) for w in windows], axis=0)
        e2 = _dot(h, win_ref[:, _C_CB:_C_END])
        mixed_s[rows, MLA_WIDTH + POOL_WIDTH:] = (_silu(e2[:, CONV_WIDTH:]) * (e2[:, :CONV_WIDTH] * conv)).astype(_BF)

        y_ref[rows, :] = y_ref[rows, :] + lw["gate"] * _dot(mixed_s[rows, :], lw["wout"][...])

    def one_layer(l, carry):
        @pl.when(first_step)
        def _():
            for k in range(len(_WEIGHTS)):
                weight_copy(l, k).wait()

        lw = layer_view(l)
        if cache_len:
            cached_keys(lw)
        for c in chunks:
            project(lw, c)
        for c in chunks:
            mix(lw, c)
        return carry

    lax.fori_loop(0, DEPTH, one_layer, 0)
    for c in chunks:
        rows = slice(c * rc, (c + 1) * rc)
        y_ref[rows, :] = _rms(y_ref[rows, :], gfin_ref[...])


def _mixer_pass(x2d, mod_all, wts, *, seq_len, cache=None, rope=None, emit_state):
    rows = x2d.shape[0]
    m = ROWS_PER_STEP
    n_seq = m // seq_len
    n_steps = rows // m
    n_all = rows // seq_len
    cache_len = 0 if cache is None else cache[0].shape[2]
    keys = cache_len + seq_len
    has_rope = rope is not None
    rc = ROW_CHUNK
    assert m % rc == 0 and cache_len % rc == 0
    assert (rc % seq_len == 0 and not cache_len) or (n_seq == 1 and seq_len % rc == 0 and not emit_state)

    def const(*shape):
        return pl.BlockSpec(shape, lambda i: (0,) * len(shape), pipeline_mode=pl.Buffered(1))

    args = [x2d, mod_all]
    in_specs = [pl.BlockSpec((m, D_MODEL), lambda i: (i, 0)), const(DEPTH, 3, MOD_ROWS, D_MODEL)]
    if cache_len:
        args += [cache[0], cache[1]]
        in_specs += [pl.BlockSpec((None, DEPTH, cache_len, KV_LORA), lambda i: (i, 0, 0, 0)),
                     pl.BlockSpec((None, DEPTH, ROPE_DIM, cache_len), lambda i: (i, 0, 0, 0))]
    if has_rope:
        args += [rope]
        in_specs += [const(seq_len, ROPE_PAD)]
    args += [_pool_rcnt(seq_len), wts["g_norm"], wts["g_q"], wts["g_kv"], wts["pool_s"], wts["conv_w"],
             wts["g_final"]]
    in_specs += [const(seq_len, POOL_WIDTH), const(DEPTH, D_MODEL), const(DEPTH, Q_LORA), const(DEPTH, KV_LORA),
                 const(DEPTH, POOL_WIDTH), const(3, DEPTH, CONV_WIDTH), const(1, D_MODEL)]
    weights = [wts[name] for name in _WEIGHTS]
    args += weights
    in_specs += [pl.BlockSpec(memory_space=pl.ANY)] * len(weights)

    out_shape = [jax.ShapeDtypeStruct((rows, D_MODEL), _F32)]
    out_specs = [pl.BlockSpec((m, D_MODEL), lambda i: (i, 0))]
    if emit_state:
        out_shape += [jax.ShapeDtypeStruct((n_all, DEPTH, seq_len, KV_LORA), _F32),
                      jax.ShapeDtypeStruct((n_all, DEPTH, ROPE_DIM, seq_len), _F32)]
        out_specs += [pl.BlockSpec((n_seq, DEPTH, seq_len, KV_LORA), lambda i: (i, 0, 0, 0)),
                      pl.BlockSpec((n_seq, DEPTH, ROPE_DIM, seq_len), lambda i: (i, 0, 0, 0))]

    kern = functools.partial(_pass_kernel, n_seq=n_seq, seq_len=seq_len, cache_len=cache_len,
                             has_rope=has_rope, emit_state=emit_state, rc=rc)
    return pl.pallas_call(
        kern,
        grid=(n_steps,),
        in_specs=in_specs,
        out_specs=out_specs,
        out_shape=out_shape,
        scratch_shapes=[
            pltpu.VMEM((m, D_MODEL), _BF),
            pltpu.VMEM((MLA_HEADS, m, HEAD_PAD), _BF),
            pltpu.VMEM((MLA_HEADS, n_seq * keys // rc, HEAD_PAD, rc), _BF),
            pltpu.VMEM((MLA_HEADS, n_seq * keys, 2 * V_DIM), _BF),
            pltpu.VMEM((n_seq * (seq_len + 2 * HALO), POOL_WIDTH), _F32),
            pltpu.VMEM((n_seq * (seq_len + 2 * HALO), CONV_WIDTH), _F32),
            pltpu.VMEM((m, MIX_WIDTH), _BF),
        ] + [pltpu.VMEM(w.shape, w.dtype) for w in weights] + [pltpu.SemaphoreType.DMA((DEPTH, len(weights)))],
        compiler_params=pltpu.CompilerParams(dimension_semantics=("arbitrary",),
                                             vmem_limit_bytes=VMEM_LIMIT_BYTES),
        name="mixer_pass_latent" if has_rope else "mixer_pass_context",
    )(*args)


def _pool_rcnt(seq_len):
    t = np.arange(seq_len)[:, None]
    half = np.repeat(1 << np.arange(POOL_GROUPS), POOL_GROUP_DIM)[None, :]
    cnt = np.minimum(t + half, seq_len) - np.maximum(t - half, 0)
    return jnp.asarray((1.0 / cnt).astype(np.float32))


def _rope_table(seq_len):
    rows = seq_len // GRID_W
    row = np.repeat(np.arange(rows), GRID_W).astype(np.float32)
    col = np.tile(np.arange(GRID_W), rows).astype(np.float32)
    inv = (1.0 / (np.float32(ROPE_BASE) ** (np.arange(0, AXIS_DIM, 2, dtype=np.float32) / np.float32(AXIS_DIM))))
    inv = inv.astype(np.float32)
    ang = np.concatenate([row[:, None] * inv, col[:, None] * inv], axis=-1).astype(np.float64)
    cos, sin = np.cos(ang).astype(np.float32), np.sin(ang).astype(np.float32)
    return jnp.asarray(np.concatenate([cos, cos, sin, sin], axis=-1))


def kernel(x_prompt, x_sample, cache_ckv, cache_krope, c, c_ctx, w_mod, b_mod, g_norm, w_in, g_q, w_uq,
           g_kv, w_ukv, pool_w, pool_s, conv_w, w_out, g_final):
    batch, seq, _ = x_prompt.shape
    dec_batch, dec_seq, _ = x_sample.shape
    assert 1 + dec_batch <= MOD_ROWS and dec_seq == ROWS_PER_STEP and ROWS_PER_STEP % seq == 0

    c_all = jnp.concatenate([c_ctx[None], c, jnp.zeros((MOD_ROWS - 1 - dec_batch, D_MODEL), _F32)], axis=0)
    mod_all = _modulation(c_all, w_mod, b_mod)
    w_in_r, w_out_r, w_uq_r, w_ukt, w_uv, pool_w_r = _prep_weights(
        jnp.transpose(w_in, (0, 2, 1)), w_out, w_uq, w_ukv, pool_w)
    wts = {
        "g_norm": g_norm, "w_in": w_in_r, "g_q": g_q, "w_uq": w_uq_r, "g_kv": g_kv, "w_ukt": w_ukt,
        "w_uv": w_uv, "pool_w": pool_w_r, "pool_s": pool_s, "conv_w": jnp.transpose(conv_w, (1, 0, 2)),
        "w_out": w_out_r, "g_final": g_final.reshape(1, D_MODEL),
    }

    rope = _rope_table(dec_seq)
    cache = (cache_ckv, jnp.transpose(cache_krope, (0, 1, 3, 2)))
    xp = x_prompt.reshape(batch * seq, D_MODEL)
    xs = x_sample.reshape(dec_batch * dec_seq, D_MODEL)
    xp, state_ckv, state_krope_t = _mixer_pass(xp, mod_all, wts, seq_len=seq, emit_state=True)
    (xs,) = _mixer_pass(xs, mod_all, wts, seq_len=dec_seq, cache=cache, rope=rope, emit_state=False)
    y_prompt = xp.reshape(batch, seq, D_MODEL)
    state_krope = jnp.transpose(state_krope_t, (0, 1, 3, 2))
    y_sample = xs.reshape(dec_batch, dec_seq, D_MODEL)
    return (y_prompt, y_sample, state_ckv, state_krope)
```

```python
import functools

import numpy as np
import jax
import jax.numpy as jnp
from jax import lax
from jax.experimental import pallas as pl
from jax.experimental.pallas import tpu as pltpu

D_MODEL = 1024
DEPTH = 2
GRID_W = 64
MLA_HEADS = 4
NOPE_DIM = 128
ROPE_DIM = 64
V_DIM = 128
QK_DIM = NOPE_DIM + ROPE_DIM
MLA_WIDTH = MLA_HEADS * V_DIM
Q_LORA = 384
KV_LORA = 256
POOL_GROUPS = 4
POOL_GROUP_DIM = 64
POOL_WIDTH = POOL_GROUPS * POOL_GROUP_DIM
CONV_WIDTH = 256
MIX_WIDTH = MLA_WIDTH + POOL_WIDTH + CONV_WIDTH
ROPE_BASE = 10000.0
AXIS_DIM = ROPE_DIM // 2
ATTN_SCALE = QK_DIM ** -0.5
Q_SCALE = ATTN_SCALE * float(np.log2(np.e))
EPS = 1e-6

_SPLITS = (Q_LORA, KV_LORA, ROPE_DIM, MLA_WIDTH, POOL_WIDTH, POOL_WIDTH,
           CONV_WIDTH, CONV_WIDTH, CONV_WIDTH, CONV_WIDTH)
_OFFS = [sum(_SPLITS[:i]) for i in range(len(_SPLITS) + 1)]
IN_WIDTH = _OFFS[-1]

V7X_SUBLANES = 8
V7X_MXU_DEPTH = 256
V7X_VMEM_BYTES = 64 * 1024 * 1024
HEAD_PAD = V7X_MXU_DEPTH
ROPE_PAD = HEAD_PAD - NOPE_DIM
KV_HEAD = NOPE_DIM + V_DIM

_C_CQ = 0
_C_CKV = _C_CQ + Q_LORA
_C_KR = _C_CKV + KV_LORA
_C_GMLA = _C_KR + ROPE_PAD
_C_GPOOL = _C_GMLA + MLA_WIDTH
_C_PX = _C_GPOOL + POOL_WIDTH
_C_CC = _C_PX + POOL_WIDTH
_C_CB = _C_CC + 2 * CONV_WIDTH
_C_END = _C_CB + 2 * CONV_WIDTH

MOD_ROWS = 8
ROWS_PER_STEP = 1024
ROW_CHUNK = 512
HALO = V7X_SUBLANES
PREP_COLS = 256
VMEM_LIMIT_BYTES = V7X_VMEM_BYTES - 4 * 1024 * 1024

_WEIGHTS = ("w_in", "w_uq", "w_ukt", "w_uv", "pool_w", "w_out")

_BF = jnp.bfloat16
_F32 = jnp.float32


def _dot(a, b):
    return jnp.dot(a, b, preferred_element_type=_F32)


def _rms(x, g):
    return x * lax.rsqrt(jnp.mean(x * x, axis=-1, keepdims=True) + EPS) * g


def _silu(x):
    return x * jax.nn.sigmoid(x)


def _mod_kernel(c_ref, w_ref, b_ref, o_ref):
    s = _silu(c_ref[...]).astype(_BF)
    for k in range(3):
        cols = slice(k * D_MODEL, (k + 1) * D_MODEL)
        bias = b_ref[pl.ds(pl.program_id(0), 1), cols]
        o_ref[k] = _dot(s, w_ref[:, cols].astype(_BF)) + bias


def _modulation(c_all, w_mod, b_mod):
    return pl.pallas_call(
        _mod_kernel,
        grid=(DEPTH,),
        in_specs=[
            pl.BlockSpec((MOD_ROWS, D_MODEL), lambda l: (0, 0)),
            pl.BlockSpec((None, D_MODEL, 3 * D_MODEL), lambda l: (l, 0, 0)),
            pl.BlockSpec((DEPTH, 3 * D_MODEL), lambda l: (0, 0)),
        ],
        out_specs=pl.BlockSpec((None, 3, MOD_ROWS, D_MODEL), lambda l: (l, 0, 0, 0)),
        out_shape=jax.ShapeDtypeStruct((DEPTH, 3, MOD_ROWS, D_MODEL), _F32),
        compiler_params=pltpu.CompilerParams(dimension_semantics=("arbitrary",),
                                             vmem_limit_bytes=VMEM_LIMIT_BYTES),
        name="modulation",
    )(c_all, w_mod, b_mod)


def _prep_kernel(wint_ref, wout_ref, wuq_ref, wukv_ref, poolw_ref,
                 win_o, wout_o, wuq_o, wukt_o, wuv_o, poolw_o):
    def put(dst, src, width):
        for c in range(0, width, PREP_COLS):
            n = min(PREP_COLS, width - c)
            win_o[:, dst + c:dst + c + n] = wint_ref[src + c:src + c + n, :].T.astype(_BF)

    put(_C_CQ, _OFFS[0], Q_LORA + KV_LORA)
    kr = wint_ref[_OFFS[2]:_OFFS[3], :]
    krb = jnp.concatenate([kr, -kr[AXIS_DIM:], kr[:AXIS_DIM]], axis=0)
    win_o[:, _C_KR:_C_GMLA] = krb.T.astype(_BF)
    put(_C_GMLA, _OFFS[3], MLA_WIDTH)
    put(_C_GPOOL, _OFFS[5], POOL_WIDTH)
    put(_C_PX, _OFFS[4], POOL_WIDTH)
    put(_C_CC, _OFFS[7], 2 * CONV_WIDTH)
    put(_C_CB, _OFFS[6], CONV_WIDTH)
    put(_C_CB + CONV_WIDTH, _OFFS[9], CONV_WIDTH)
    wout_o[...] = wout_ref[...].astype(_BF)

    for hd in range(MLA_HEADS):
        s0, d0 = hd * QK_DIM, hd * HEAD_PAD
        r0 = s0 + NOPE_DIM
        wuq_o[:, d0:d0 + QK_DIM] = wuq_ref[:, s0:s0 + QK_DIM].astype(_BF)
        wuq_o[:, d0 + QK_DIM:d0 + QK_DIM + AXIS_DIM] = (-wuq_ref[:, r0 + AXIS_DIM:r0 + ROPE_DIM]).astype(_BF)
        wuq_o[:, d0 + QK_DIM + AXIS_DIM:d0 + HEAD_PAD] = wuq_ref[:, r0:r0 + AXIS_DIM].astype(_BF)
    for hd in range(MLA_HEADS):
        k0 = hd * KV_HEAD
        wukt_o[hd * NOPE_DIM:(hd + 1) * NOPE_DIM, :] = wukv_ref[:, k0:k0 + NOPE_DIM].T.astype(_BF)
        wuv_o[:, hd * V_DIM:(hd + 1) * V_DIM] = wukv_ref[:, k0 + NOPE_DIM:k0 + KV_HEAD].astype(_BF)
    poolw_o[...] = jnp.zeros((POOL_WIDTH, POOL_WIDTH), _BF)
    for g in range(POOL_GROUPS):
        sl = slice(g * POOL_GROUP_DIM, (g + 1) * POOL_GROUP_DIM)
        poolw_o[sl, sl] = poolw_ref[g].astype(_BF)


def _prep_weights(w_in_t, w_out, w_uq, w_ukv, pool_w):
    per_layer = lambda *shape: pl.BlockSpec((None,) + shape, lambda l: (l,) + (0,) * len(shape))
    return pl.pallas_call(
        _prep_kernel,
        grid=(DEPTH,),
        in_specs=[
            per_layer(IN_WIDTH, D_MODEL),
            per_layer(MIX_WIDTH, D_MODEL),
            per_layer(Q_LORA, MLA_HEADS * QK_DIM),
            per_layer(KV_LORA, MLA_HEADS * KV_HEAD),
            per_layer(POOL_GROUPS, POOL_GROUP_DIM, POOL_GROUP_DIM),
        ],
        out_specs=[
            per_layer(D_MODEL, _C_END),
            per_layer(MIX_WIDTH, D_MODEL),
            per_layer(Q_LORA, MLA_HEADS * HEAD_PAD),
            per_layer(MLA_HEADS * NOPE_DIM, KV_LORA),
            per_layer(KV_LORA, MLA_WIDTH),
            per_layer(POOL_WIDTH, POOL_WIDTH),
        ],
        out_shape=[
            jax.ShapeDtypeStruct((DEPTH, D_MODEL, _C_END), _BF),
            jax.ShapeDtypeStruct((DEPTH, MIX_WIDTH, D_MODEL), _BF),
            jax.ShapeDtypeStruct((DEPTH, Q_LORA, MLA_HEADS * HEAD_PAD), _BF),
            jax.ShapeDtypeStruct((DEPTH, MLA_HEADS * NOPE_DIM, KV_LORA), _BF),
            jax.ShapeDtypeStruct((DEPTH, KV_LORA, MLA_WIDTH), _BF),
            jax.ShapeDtypeStruct((DEPTH, POOL_WIDTH, POOL_WIDTH), _BF),
        ],
        compiler_params=pltpu.CompilerParams(dimension_semantics=("arbitrary",),
                                             vmem_limit_bytes=VMEM_LIMIT_BYTES),
        name="weight_prep",
    )(w_in_t, w_out, w_uq, w_ukv, pool_w)


def _pool_mix(win, rcnt):
    n = win.shape[0]
    rows = n - 2 * HALO
    lane = lax.broadcasted_iota(jnp.int32, (1, POOL_WIDTH), 1)
    sums = [win + pltpu.roll(win, 1, axis=0)]
    for k in (1, 2, 4):
        sums.append(pltpu.roll(sums[-1], k, axis=0) + pltpu.roll(sums[-1], n - k, axis=0))
    sel = sums[-1]
    for g in range(POOL_GROUPS - 2, -1, -1):
        sel = jnp.where(lane < (g + 1) * POOL_GROUP_DIM, sums[g], sel)
    return sel[HALO:HALO + rows] * rcnt - win[HALO:HALO + rows]


def _short_conv(win, w):
    rows = win.shape[0] - 2 * HALO
    return (win[HALO - 1:HALO - 1 + rows] * w[0] + win[HALO:HALO + rows] * w[1]
            + win[HALO + 1:HALO + 1 + rows] * w[2])


def _pass_kernel(*refs, n_seq, seq_len, cache_len, has_rope, emit_state, rc):
    it = iter(refs)
    x_ref, mod_ref = next(it), next(it)
    if cache_len:
        cckv_all_ref, ckr_all_ref = next(it), next(it)
    if has_rope:
        cs_ref = next(it)
    rcnt_ref, gn_ref, gq_ref, gkv_ref, pools_ref, convw_ref, gfin_ref = (next(it) for _ in range(7))
    w_hbm = [next(it) for _ in range(len(_WEIGHTS))]
    y_ref = next(it)
    if emit_state:
        ckv_out_ref, kr_out_ref = next(it), next(it)
    h_s, q_s, kt_s, v_s, px_s, prod_s, mixed_s = (next(it) for _ in range(7))
    w_vm = [next(it) for _ in range(len(_WEIGHTS))]
    w_sem = next(it)
    win_all_ref, wuq_all_ref, wukt_all_ref, wuv_all_ref, poolw_all_ref, wout_all_ref = w_vm

    def weight_copy(l, k):
        return pltpu.make_async_copy(w_hbm[k].at[l], w_vm[k].at[l], w_sem.at[l, k])

    first_step = pl.program_id(0) == 0

    @pl.when(first_step)
    def _():
        for l in range(DEPTH):
            for k in range(len(_WEIGHTS)):
                weight_copy(l, k).start()

    m = n_seq * seq_len
    keys = cache_len + seq_len
    padded = seq_len + 2 * HALO
    lane = lax.broadcasted_iota(jnp.int32, (1, ROPE_PAD), 1)
    mrow = (1 + pl.program_id(0)) if has_rope else 0
    piece = min(seq_len, rc)
    n_sub = rc // piece
    pieces = [(s, slice(s * piece, (s + 1) * piece)) for s in range(n_sub)]
    chunks = range(m // rc)

    for c in chunks:
        y_ref[c * rc:(c + 1) * rc, :] = x_ref[c * rc:(c + 1) * rc, :]

    zeros_halo = jnp.zeros((HALO, POOL_WIDTH), _F32)
    for buf in (px_s, prod_s):
        for s in range(n_seq):
            buf[s * padded:s * padded + HALO, :] = zeros_halo
            buf[(s + 1) * padded - HALO:(s + 1) * padded, :] = zeros_halo
    ones_col = jnp.where(lax.broadcasted_iota(jnp.int32, (n_seq * keys, V_DIM), 1) == 0, 1.0, 0.0).astype(_BF)
    for hd in range(MLA_HEADS):
        v_s[hd, :, V_DIM:] = ones_col

    def layer_view(l):
        lw = dict(l=l, row=pl.ds(l, 1), win=win_all_ref.at[l], wuq=wuq_all_ref.at[l], wukt=wukt_all_ref.at[l],
                  wuv=wuv_all_ref.at[l], poolw=poolw_all_ref.at[l], wout=wout_all_ref.at[l])
        lw["shift"], lw["scale"], lw["gate"] = (mod_ref[l, k, pl.ds(mrow, 1), :] for k in range(3))
        if cache_len:
            lw["cckv"], lw["ckr"] = cckv_all_ref.at[l], ckr_all_ref.at[l]
        return lw

    def rotate(v, r):
        v = v * cs_ref[pl.ds(r, rc), :]
        return v + pltpu.roll(v, ROPE_DIM, axis=1)

    def put_kv(lw, ckv, kr_t, kc):
        knt = _dot(lw["wukt"][...], ckv.T.astype(_BF))
        v4 = _dot(ckv.astype(_BF), lw["wuv"][...])
        kr_bf = kr_t.astype(_BF)
        rows_k = pl.ds(kc * rc, rc)
        for hd in range(MLA_HEADS):
            kt_s[hd, kc, :NOPE_DIM, :] = knt[hd * NOPE_DIM:(hd + 1) * NOPE_DIM, :].astype(_BF)
            kt_s[hd, kc, NOPE_DIM:, :] = kr_bf
            v_s[hd, rows_k, :V_DIM] = v4[:, hd * V_DIM:(hd + 1) * V_DIM].astype(_BF)

    def cached_keys(lw):
        ckr_t = lw["ckr"][...]
        ckr_pad = jnp.concatenate([ckr_t, jnp.zeros_like(ckr_t)], axis=0)
        for cc in range(cache_len // rc):
            rows_k = slice(cc * rc, (cc + 1) * rc)
            put_kv(lw, lw["cckv"][rows_k, :], ckr_pad[:, rows_k], cc)

    def halo_base(c, s):
        return (c * n_sub + s) * padded if seq_len <= rc else c * rc

    def project(lw, c):
        r = c * rc
        rows = pl.ds(r, rc)
        row, win_ref = lw["row"], lw["win"]
        h = (_rms(y_ref[rows, :], gn_ref[row, :]) * (1.0 + lw["scale"]) + lw["shift"]).astype(_BF)
        h_s[rows, :] = h
        a = _dot(h, win_ref[:, _C_CQ:_C_GMLA])
        pe = _dot(h, win_ref[:, _C_PX:_C_CB])
        for s, ps in pieces:
            rows_h = pl.ds(halo_base(c, s) + HALO, piece)
            px_s[rows_h, :] = pe[ps, :POOL_WIDTH]
            prod_s[rows_h, :] = pe[ps, POOL_WIDTH:POOL_WIDTH + CONV_WIDTH] * pe[ps, POOL_WIDTH + CONV_WIDTH:]
        cq = _rms(a[:, :Q_LORA], gq_ref[row, :]).astype(_BF)
        ckv = _rms(a[:, _C_CKV:_C_KR], gkv_ref[row, :])
        kr = a[:, _C_KR:]
        if has_rope:
            kr = rotate(kr, r)
        kr_t = jnp.where(lane < ROPE_DIM, kr, 0.0).T
        if emit_state:
            for s, ps in pieces:
                ckv_out_ref[c * n_sub + s, lw["l"]] = ckv[ps, :]
                kr_out_ref[c * n_sub + s, lw["l"]] = kr_t[:ROPE_DIM, ps]
        put_kv(lw, ckv, kr_t, cache_len // rc + c)
        for hp in range(MLA_HEADS // 2):
            q2 = _dot(cq, lw["wuq"][:, hp * 2 * HEAD_PAD:(hp + 1) * 2 * HEAD_PAD]) * Q_SCALE
            for j in range(2):
                hd = 2 * hp + j
                c0 = j * HEAD_PAD
                q_s[hd, rows, :NOPE_DIM] = q2[:, c0:c0 + NOPE_DIM].astype(_BF)
                qr = q2[:, c0 + NOPE_DIM:c0 + HEAD_PAD]
                if has_rope:
                    qr = rotate(qr, r)
                q_s[hd, rows, NOPE_DIM:] = qr.astype(_BF)

    def mix(lw, c):
        r = c * rc
        rows = pl.ds(r, rc)
        row, win_ref = lw["row"], lw["win"]
        h = h_s[rows, :]
        for hp in range(MLA_HEADS // 2):
            g2 = _silu(_dot(h, win_ref[:, _C_GMLA + hp * 2 * V_DIM:_C_GMLA + (hp + 1) * 2 * V_DIM]))
            for j in range(2):
                hd = 2 * hp + j
                for s, ps in pieces:
                    rows_q = pl.ds(r + s * piece, piece)
                    if n_seq == 1:
                        ks = [kt_s[hd, kc] for kc in range(keys // rc)]
                        rows_k = pl.ds(0, keys)
                    else:
                        ks = [kt_s[hd, c, :, ps]]
                        rows_k = rows_q
                    q = q_s[hd, rows_q, :]
                    sc = jnp.concatenate([_dot(q, k) for k in ks], axis=-1)
                    p = jnp.exp2(sc - jnp.max(sc, axis=-1, keepdims=True))
                    ov = _dot(p.astype(_BF), v_s[hd, rows_k, :])
                    o = ov[:, :V_DIM] / ov[:, V_DIM:V_DIM + 1]
                    mixed_s[rows_q, hd * V_DIM:(hd + 1) * V_DIM] = (g2[ps, j * V_DIM:(j + 1) * V_DIM] * o).astype(_BF)

        windows = [pl.ds(halo_base(c, s), piece + 2 * HALO) for s, _ in pieces]
        rcnt = rcnt_ref[pl.ds(0 if seq_len <= rc else r, piece), :]
        pooled = jnp.concatenate([_pool_mix(px_s[w, :], rcnt) for w in windows], axis=0)
        pool = _dot(pooled.astype(_BF), lw["poolw"][...]) * pools_ref[row, :]
        gp = _silu(_dot(h, win_ref[:, _C_GPOOL:_C_PX]))
        mixed_s[rows, MLA_WIDTH:MLA_WIDTH + POOL_WIDTH] = (gp * pool).astype(_BF)

        convw = [convw_ref[k, row, :] for k in range(3)]
        conv = jnp.concatenate([_short_conv(prod_s[w, :], convw) for w in windows], axis=0)
        e2 = _dot(h, win_ref[:, _C_CB:_C_END])
        mixed_s[rows, MLA_WIDTH + POOL_WIDTH:] = (_silu(e2[:, CONV_WIDTH:]) * (e2[:, :CONV_WIDTH] * conv)).astype(_BF)

        y_ref[rows, :] = y_ref[rows, :] + lw["gate"] * _dot(mixed_s[rows, :], lw["wout"][...])

    def one_layer(l, carry):
        @pl.when(first_step)
        def _():
            for k in range(len(_WEIGHTS)):
                weight_copy(l, k).wait()

        lw = layer_view(l)
        if cache_len:
            cached_keys(lw)
        for c in chunks:
            project(lw, c)
        for c in chunks:
            mix(lw, c)
        return carry

    lax.fori_loop(0, DEPTH, one_layer, 0)
    for c in chunks:
        rows = slice(c * rc, (c + 1) * rc)
        y_ref[rows, :] = _rms(y_ref[rows, :], gfin_ref[...])


def _mixer_pass(x2d, mod_all, wts, *, seq_len, cache=None, rope=None, emit_state):
    rows = x2d.shape[0]
    m = ROWS_PER_STEP
    n_seq = m // seq_len
    n_steps = rows // m
    n_all = rows // seq_len
    cache_len = 0 if cache is None else cache[0].shape[2]
    keys = cache_len + seq_len
    has_rope = rope is not None
    rc = ROW_CHUNK
    assert m % rc == 0 and cache_len % rc == 0
    assert (rc % seq_len == 0 and not cache_len) or (n_seq == 1 and seq_len % rc == 0 and not emit_state)

    def const(*shape):
        return pl.BlockSpec(shape, lambda i: (0,) * len(shape), pipeline_mode=pl.Buffered(1))

    args = [x2d, mod_all]
    in_specs = [pl.BlockSpec((m, D_MODEL), lambda i: (i, 0)), const(DEPTH, 3, MOD_ROWS, D_MODEL)]
    if cache_len:
        args += [cache[0], cache[1]]
        in_specs += [pl.BlockSpec((None, DEPTH, cache_len, KV_LORA), lambda i: (i, 0, 0, 0)),
                     pl.BlockSpec((None, DEPTH, ROPE_DIM, cache_len), lambda i: (i, 0, 0, 0))]
    if has_rope:
        args += [rope]
        in_specs += [const(seq_len, ROPE_PAD)]
    args += [_pool_rcnt(seq_len), wts["g_norm"], wts["g_q"], wts["g_kv"], wts["pool_s"], wts["conv_w"],
             wts["g_final"]]
    in_specs += [const(seq_len, POOL_WIDTH), const(DEPTH, D_MODEL), const(DEPTH, Q_LORA), const(DEPTH, KV_LORA),
                 const(DEPTH, POOL_WIDTH), const(3, DEPTH, CONV_WIDTH), const(1, D_MODEL)]
    weights = [wts[name] for name in _WEIGHTS]
    args += weights
    in_specs += [pl.BlockSpec(memory_space=pl.ANY)] * len(weights)

    out_shape = [jax.ShapeDtypeStruct((rows, D_MODEL), _F32)]
    out_specs = [pl.BlockSpec((m, D_MODEL), lambda i: (i, 0))]
    if emit_state:
        out_shape += [jax.ShapeDtypeStruct((n_all, DEPTH, seq_len, KV_LORA), _F32),
                      jax.ShapeDtypeStruct((n_all, DEPTH, ROPE_DIM, seq_len), _F32)]
        out_specs += [pl.BlockSpec((n_seq, DEPTH, seq_len, KV_LORA), lambda i: (i, 0, 0, 0)),
                      pl.BlockSpec((n_seq, DEPTH, ROPE_DIM, seq_len), lambda i: (i, 0, 0, 0))]

    kern = functools.partial(_pass_kernel, n_seq=n_seq, seq_len=seq_len, cache_len=cache_len,
                             has_rope=has_rope, emit_state=emit_state, rc=rc)
    return pl.pallas_call(
        kern,
        grid=(n_steps,),
        in_specs=in_specs,
        out_specs=out_specs,
        out_shape=out_shape,
        scratch_shapes=[
            pltpu.VMEM((m, D_MODEL), _BF),
            pltpu.VMEM((MLA_HEADS, m, HEAD_PAD), _BF),
            pltpu.VMEM((MLA_HEADS, n_seq * keys // rc, HEAD_PAD, rc), _BF),
            pltpu.VMEM((MLA_HEADS, n_seq * keys, 2 * V_DIM), _BF),
            pltpu.VMEM((n_seq * (seq_len + 2 * HALO), POOL_WIDTH), _F32),
            pltpu.VMEM((n_seq * (seq_len + 2 * HALO), CONV_WIDTH), _F32),
            pltpu.VMEM((m, MIX_WIDTH), _BF),
        ] + [pltpu.VMEM(w.shape, w.dtype) for w in weights] + [pltpu.SemaphoreType.DMA((DEPTH, len(weights)))],
        compiler_params=pltpu.CompilerParams(dimension_semantics=("arbitrary",),
                                             vmem_limit_bytes=VMEM_LIMIT_BYTES),
        name="mixer_pass_latent" if has_rope else "mixer_pass_context",
    )(*args)


def _pool_rcnt(seq_len):
    t = np.arange(seq_len)[:, None]
    half = np.repeat(1 << np.arange(POOL_GROUPS), POOL_GROUP_DIM)[None, :]
    cnt = np.minimum(t + half, seq_len) - np.maximum(t - half, 0)
    return jnp.asarray((1.0 / cnt).astype(np.float32))


def _rope_table(seq_len):
    rows = seq_len // GRID_W
    row = np.repeat(np.arange(rows), GRID_W).astype(np.float32)
    col = np.tile(np.arange(GRID_W), rows).astype(np.float32)
    inv = (1.0 / (np.float32(ROPE_BASE) ** (np.arange(0, AXIS_DIM, 2, dtype=np.float32) / np.float32(AXIS_DIM))))
    inv = inv.astype(np.float32)
    ang = np.concatenate([row[:, None] * inv, col[:, None] * inv], axis=-1).astype(np.float64)
    cos, sin = np.cos(ang).astype(np.float32), np.sin(ang).astype(np.float32)
    return jnp.asarray(np.concatenate([cos, cos, sin, sin], axis=-1))


def kernel(x_prompt, x_sample, cache_ckv, cache_krope, c, c_ctx, w_mod, b_mod, g_norm, w_in, g_q, w_uq,
           g_kv, w_ukv, pool_w, pool_s, conv_w, w_out, g_final):
    batch, seq, _ = x_prompt.shape
    dec_batch, dec_seq, _ = x_sample.shape
    assert 1 + dec_batch <= MOD_ROWS and dec_seq == ROWS_PER_STEP and ROWS_PER_STEP % seq == 0

    c_all = jnp.concatenate([c_ctx[None], c, jnp.zeros((MOD_ROWS - 1 - dec_batch, D_MODEL), _F32)], axis=0)
    mod_all = _modulation(c_all, w_mod, b_mod)
    w_in_r, w_out_r, w_uq_r, w_ukt, w_uv, pool_w_r = _prep_weights(
        jnp.transpose(w_in, (0, 2, 1)), w_out, w_uq, w_ukv, pool_w)
    wts = {
        "g_norm": g_norm, "w_in": w_in_r, "g_q": g_q, "w_uq": w_uq_r, "g_kv": g_kv, "w_ukt": w_ukt,
        "w_uv": w_uv, "pool_w": pool_w_r, "pool_s": pool_s, "conv_w": jnp.transpose(conv_w, (1, 0, 2)),
        "w_out": w_out_r, "g_final": g_final.reshape(1, D_MODEL),
    }

    rope = _rope_table(dec_seq)
    cache = (cache_ckv, jnp.transpose(cache_krope, (0, 1, 3, 2)))
    xp = x_prompt.reshape(batch * seq, D_MODEL)
    xs = x_sample.reshape(dec_batch * dec_seq, D_MODEL)
    xp, state_ckv, state_krope_t = _mixer_pass(xp, mod_all, wts, seq_len=seq, emit_state=True)
    (xs,) = _mixer_pass(xs, mod_all, wts, seq_len=dec_seq, cache=cache, rope=rope, emit_state=False)
    y_prompt = xp.reshape(batch, seq, D_MODEL)
    state_krope = jnp.transpose(state_krope_t, (0, 1, 3, 2))
    y_sample = xs.reshape(dec_batch, dec_seq, D_MODEL)
    return (y_prompt, y_sample, state_ckv, state_krope)
```

```python
import functools

import numpy as np
import jax
import jax.numpy as jnp
from jax import lax
from jax.experimental import pallas as pl
from jax.experimental.pallas import tpu as pltpu

D_MODEL = 1024
DEPTH = 2
GRID_W = 64
MLA_HEADS = 4
NOPE_DIM = 128
ROPE_DIM = 64
V_DIM = 128
QK_DIM = NOPE_DIM + ROPE_DIM
MLA_WIDTH = MLA_HEADS * V_DIM
Q_LORA = 384
KV_LORA = 256
POOL_GROUPS = 4
POOL_GROUP_DIM = 64
POOL_WIDTH = POOL_GROUPS * POOL_GROUP_DIM
CONV_WIDTH = 256
MIX_WIDTH = MLA_WIDTH + POOL_WIDTH + CONV_WIDTH
ROPE_BASE = 10000.0
AXIS_DIM = ROPE_DIM // 2
ATTN_SCALE = QK_DIM ** -0.5
Q_SCALE = ATTN_SCALE * float(np.log2(np.e))
EPS = 1e-6

_SPLITS = (Q_LORA, KV_LORA, ROPE_DIM, MLA_WIDTH, POOL_WIDTH, POOL_WIDTH,
           CONV_WIDTH, CONV_WIDTH, CONV_WIDTH, CONV_WIDTH)
_OFFS = [sum(_SPLITS[:i]) for i in range(len(_SPLITS) + 1)]
IN_WIDTH = _OFFS[-1]

V7X_SUBLANES = 8
V7X_MXU_DEPTH = 256
V7X_VMEM_BYTES = 64 * 1024 * 1024
HEAD_PAD = V7X_MXU_DEPTH
ROPE_PAD = HEAD_PAD - NOPE_DIM
KV_HEAD = NOPE_DIM + V_DIM

_C_CQ = 0
_C_CKV = _C_CQ + Q_LORA
_C_KR = _C_CKV + KV_LORA
_C_GMLA = _C_KR + ROPE_PAD
_C_GPOOL = _C_GMLA + MLA_WIDTH
_C_PX = _C_GPOOL + POOL_WIDTH
_C_CC = _C_PX + POOL_WIDTH
_C_CB = _C_CC + 2 * CONV_WIDTH
_C_END = _C_CB + 2 * CONV_WIDTH

MOD_ROWS = 8
ROWS_PER_STEP = 1024
ROW_CHUNK = 512
HALO = V7X_SUBLANES
PREP_COLS = 256
VMEM_LIMIT_BYTES = V7X_VMEM_BYTES - 4 * 1024 * 1024

_WEIGHTS = ("w_in", "w_uq", "w_ukt", "w_uv", "pool_w", "w_out")

_BF = jnp.bfloat16
_F32 = jnp.float32


def _dot(a, b):
    return jnp.dot(a, b, preferred_element_type=_F32)


def _rms(x, g):
    return x * lax.rsqrt(jnp.mean(x * x, axis=-1, keepdims=True) + EPS) * g


def _silu(x):
    return x * jax.nn.sigmoid(x)


def _mod_kernel(c_ref, w_ref, b_ref, o_ref):
    s = _silu(c_ref[...]).astype(_BF)
    for k in range(3):
        cols = slice(k * D_MODEL, (k + 1) * D_MODEL)
        bias = b_ref[pl.ds(pl.program_id(0), 1), cols]
        o_ref[k] = _dot(s, w_ref[:, cols].astype(_BF)) + bias


def _modulation(c_all, w_mod, b_mod):
    return pl.pallas_call(
        _mod_kernel,
        grid=(DEPTH,),
        in_specs=[
            pl.BlockSpec((MOD_ROWS, D_MODEL), lambda l: (0, 0)),
            pl.BlockSpec((None, D_MODEL, 3 * D_MODEL), lambda l: (l, 0, 0)),
            pl.BlockSpec((DEPTH, 3 * D_MODEL), lambda l: (0, 0)),
        ],
        out_specs=pl.BlockSpec((None, 3, MOD_ROWS, D_MODEL), lambda l: (l, 0, 0, 0)),
        out_shape=jax.ShapeDtypeStruct((DEPTH, 3, MOD_ROWS, D_MODEL), _F32),
        compiler_params=pltpu.CompilerParams(dimension_semantics=("arbitrary",),
                                             vmem_limit_bytes=VMEM_LIMIT_BYTES),
        name="modulation",
    )(c_all, w_mod, b_mod)


def _prep_kernel(wint_ref, wout_ref, wuq_ref, wukv_ref, poolw_ref,
                 win_o, wout_o, wuq_o, wukt_o, wuv_o, poolw_o):
    def put(dst, src, width):
        for c in range(0, width, PREP_COLS):
            n = min(PREP_COLS, width - c)
            win_o[:, dst + c:dst + c + n] = wint_ref[src + c:src + c + n, :].T.astype(_BF)

    put(_C_CQ, _OFFS[0], Q_LORA + KV_LORA)
    kr = wint_ref[_OFFS[2]:_OFFS[3], :]
    krb = jnp.concatenate([kr, -kr[AXIS_DIM:], kr[:AXIS_DIM]], axis=0)
    win_o[:, _C_KR:_C_GMLA] = krb.T.astype(_BF)
    put(_C_GMLA, _OFFS[3], MLA_WIDTH)
    put(_C_GPOOL, _OFFS[5], POOL_WIDTH)
    put(_C_PX, _OFFS[4], POOL_WIDTH)
    put(_C_CC, _OFFS[7], 2 * CONV_WIDTH)
    put(_C_CB, _OFFS[6], CONV_WIDTH)
    put(_C_CB + CONV_WIDTH, _OFFS[9], CONV_WIDTH)
    wout_o[...] = wout_ref[...].astype(_BF)

    for hd in range(MLA_HEADS):
        s0, d0 = hd * QK_DIM, hd * HEAD_PAD
        r0 = s0 + NOPE_DIM
        wuq_o[:, d0:d0 + QK_DIM] = wuq_ref[:, s0:s0 + QK_DIM].astype(_BF)
        wuq_o[:, d0 + QK_DIM:d0 + QK_DIM + AXIS_DIM] = (-wuq_ref[:, r0 + AXIS_DIM:r0 + ROPE_DIM]).astype(_BF)
        wuq_o[:, d0 + QK_DIM + AXIS_DIM:d0 + HEAD_PAD] = wuq_ref[:, r0:r0 + AXIS_DIM].astype(_BF)
    for hd in range(MLA_HEADS):
        k0 = hd * KV_HEAD
        wukt_o[hd * NOPE_DIM:(hd + 1) * NOPE_DIM, :] = wukv_ref[:, k0:k0 + NOPE_DIM].T.astype(_BF)
        wuv_o[:, hd * V_DIM:(hd + 1) * V_DIM] = wukv_ref[:, k0 + NOPE_DIM:k0 + KV_HEAD].astype(_BF)
    poolw_o[...] = jnp.zeros((POOL_WIDTH, POOL_WIDTH), _BF)
    for g in range(POOL_GROUPS):
        sl = slice(g * POOL_GROUP_DIM, (g + 1) * POOL_GROUP_DIM)
        poolw_o[sl, sl] = poolw_ref[g].astype(_BF)


def _prep_weights(w_in_t, w_out, w_uq, w_ukv, pool_w):
    per_layer = lambda *shape: pl.BlockSpec((None,) + shape, lambda l: (l,) + (0,) * len(shape))
    return pl.pallas_call(
        _prep_kernel,
        grid=(DEPTH,),
        in_specs=[
            per_layer(IN_WIDTH, D_MODEL),
            per_layer(MIX_WIDTH, D_MODEL),
            per_layer(Q_LORA, MLA_HEADS * QK_DIM),
            per_layer(KV_LORA, MLA_HEADS * KV_HEAD),
            per_layer(POOL_GROUPS, POOL_GROUP_DIM, POOL_GROUP_DIM),
        ],
        out_specs=[
            per_layer(D_MODEL, _C_END),
            per_layer(MIX_WIDTH, D_MODEL),
            per_layer(Q_LORA, MLA_HEADS * HEAD_PAD),
            per_layer(MLA_HEADS * NOPE_DIM, KV_LORA),
            per_layer(KV_LORA, MLA_WIDTH),
            per_layer(POOL_WIDTH, POOL_WIDTH),
        ],
        out_shape=[
            jax.ShapeDtypeStruct((DEPTH, D_MODEL, _C_END), _BF),
            jax.ShapeDtypeStruct((DEPTH, MIX_WIDTH, D_MODEL), _BF),
            jax.ShapeDtypeStruct((DEPTH, Q_LORA, MLA_HEADS * HEAD_PAD), _BF),
            jax.ShapeDtypeStruct((DEPTH, MLA_HEADS * NOPE_DIM, KV_LORA), _BF),
            jax.ShapeDtypeStruct((DEPTH, KV_LORA, MLA_WIDTH), _BF),
            jax.ShapeDtypeStruct((DEPTH, POOL_WIDTH, POOL_WIDTH), _BF),
        ],
        compiler_params=pltpu.CompilerParams(dimension_semantics=("arbitrary",),
                                             vmem_limit_bytes=VMEM_LIMIT_BYTES),
        name="weight_prep",
    )(w_in_t, w_out, w_uq, w_ukv, pool_w)


def _pool_mix(win, rcnt):
    n = win.shape[0]
    rows = n - 2 * HALO
    lane = lax.broadcasted_iota(jnp.int32, (1, POOL_WIDTH), 1)
    sums = [win + pltpu.roll(win, 1, axis=0)]
    for k in (1, 2, 4):
        sums.append(pltpu.roll(sums[-1], k, axis=0) + pltpu.roll(sums[-1], n - k, axis=0))
    sel = sums[-1]
    for g in range(POOL_GROUPS - 2, -1, -1):
        sel = jnp.where(lane < (g + 1) * POOL_GROUP_DIM, sums[g], sel)
    return sel[HALO:HALO + rows] * rcnt - win[HALO:HALO + rows]


def _short_conv(win, w):
    rows = win.shape[0] - 2 * HALO
    return (win[HALO - 1:HALO - 1 + rows] * w[0] + win[HALO:HALO + rows] * w[1]
            + win[HALO + 1:HALO + 1 + rows] * w[2])


def _pass_kernel(*refs, n_seq, seq_len, cache_len, has_rope, emit_state, rc):
    it = iter(refs)
    x_ref, mod_ref = next(it), next(it)
    if cache_len:
        cckv_all_ref, ckr_all_ref = next(it), next(it)
    if has_rope:
        cs_ref = next(it)
    rcnt_ref, gn_ref, gq_ref, gkv_ref, pools_ref, convw_ref, gfin_ref = (next(it) for _ in range(7))
    w_hbm = [next(it) for _ in range(len(_WEIGHTS))]
    y_ref = next(it)
    if emit_state:
        ckv_out_ref, kr_out_ref = next(it), next(it)
    h_s, q_s, kt_s, v_s, px_s, prod_s, mixed_s = (next(it) for _ in range(7))
    w_vm = [next(it) for _ in range(len(_WEIGHTS))]
    w_sem = next(it)
    win_all_ref, wuq_all_ref, wukt_all_ref, wuv_all_ref, poolw_all_ref, wout_all_ref = w_vm

    def weight_copy(l, k):
        return pltpu.make_async_copy(w_hbm[k].at[l], w_vm[k].at[l], w_sem.at[l, k])

    first_step = pl.program_id(0) == 0

    @pl.when(first_step)
    def _():
        for l in range(DEPTH):
            for k in range(len(_WEIGHTS)):
                weight_copy(l, k).start()

    m = n_seq * seq_len
    keys = cache_len + seq_len
    padded = seq_len + 2 * HALO
    lane = lax.broadcasted_iota(jnp.int32, (1, ROPE_PAD), 1)
    mrow = (1 + pl.program_id(0)) if has_rope else 0
    piece = min(seq_len, rc)
    n_sub = rc // piece
    pieces = [(s, slice(s * piece, (s + 1) * piece)) for s in range(n_sub)]
    chunks = range(m // rc)

    for c in chunks:
        y_ref[c * rc:(c + 1) * rc, :] = x_ref[c * rc:(c + 1) * rc, :]

    zeros_halo = jnp.zeros((HALO, POOL_WIDTH), _F32)
    for buf in (px_s, prod_s):
        for s in range(n_seq):
            buf[s * padded:s * padded + HALO, :] = zeros_halo
            buf[(s + 1) * padded - HALO:(s + 1) * padded, :] = zeros_halo
    ones_col = jnp.where(lax.broadcasted_iota(jnp.int32, (n_seq * keys, V_DIM), 1) == 0, 1.0, 0.0).astype(_BF)
    for hd in range(MLA_HEADS):
        v_s[hd, :, V_DIM:] = ones_col

    def layer_view(l):
        lw = dict(l=l, row=pl.ds(l, 1), win=win_all_ref.at[l], wuq=wuq_all_ref.at[l], wukt=wukt_all_ref.at[l],
                  wuv=wuv_all_ref.at[l], poolw=poolw_all_ref.at[l], wout=wout_all_ref.at[l])
        lw["shift"], lw["scale"], lw["gate"] = (mod_ref[l, k, pl.ds(mrow, 1), :] for k in range(3))
        if cache_len:
            lw["cckv"], lw["ckr"] = cckv_all_ref.at[l], ckr_all_ref.at[l]
        return lw

    def rotate(v, r):
        v = v * cs_ref[pl.ds(r, rc), :]
        return v + pltpu.roll(v, ROPE_DIM, axis=1)

    def put_kv(lw, ckv, kr_t, kc):
        knt = _dot(lw["wukt"][...], ckv.T.astype(_BF))
        v4 = _dot(ckv.astype(_BF), lw["wuv"][...])
        kr_bf = kr_t.astype(_BF)
        rows_k = pl.ds(kc * rc, rc)
        for hd in range(MLA_HEADS):
            kt_s[hd, kc, :NOPE_DIM, :] = knt[hd * NOPE_DIM:(hd + 1) * NOPE_DIM, :].astype(_BF)
            kt_s[hd, kc, NOPE_DIM:, :] = kr_bf
            v_s[hd, rows_k, :V_DIM] = v4[:, hd * V_DIM:(hd + 1) * V_DIM].astype(_BF)

    def cached_keys(lw):
        ckr_t = lw["ckr"][...]
        ckr_pad = jnp.concatenate([ckr_t, jnp.zeros_like(ckr_t)], axis=0)
        for cc in range(cache_len // rc):
            rows_k = slice(cc * rc, (cc + 1) * rc)
            put_kv(lw, lw["cckv"][rows_k, :], ckr_pad[:, rows_k], cc)

    def halo_base(c, s):
        return (c * n_sub + s) * padded if seq_len <= rc else c * rc

    def project(lw, c):
        r = c * rc
        rows = pl.ds(r, rc)
        row, win_ref = lw["row"], lw["win"]
        h = (_rms(y_ref[rows, :], gn_ref[row, :]) * (1.0 + lw["scale"]) + lw["shift"]).astype(_BF)
        h_s[rows, :] = h
        a = _dot(h, win_ref[:, _C_CQ:_C_GMLA])
        pe = _dot(h, win_ref[:, _C_PX:_C_CB])
        for s, ps in pieces:
            rows_h = pl.ds(halo_base(c, s) + HALO, piece)
            px_s[rows_h, :] = pe[ps, :POOL_WIDTH]
            prod_s[rows_h, :] = pe[ps, POOL_WIDTH:POOL_WIDTH + CONV_WIDTH] * pe[ps, POOL_WIDTH + CONV_WIDTH:]
        return a

    def project_heads(lw, c, a):
        r = c * rc
        rows = pl.ds(r, rc)
        row = lw["row"]
        cq = _rms(a[:, :Q_LORA], gq_ref[row, :]).astype(_BF)
        ckv = _rms(a[:, _C_CKV:_C_KR], gkv_ref[row, :])
        kr = a[:, _C_KR:]
        if has_rope:
            kr = rotate(kr, r)
        kr_t = jnp.where(lane < ROPE_DIM, kr, 0.0).T
        if emit_state:
            for s, ps in pieces:
                ckv_out_ref[c * n_sub + s, lw["l"]] = ckv[ps, :]
                kr_out_ref[c * n_sub + s, lw["l"]] = kr_t[:ROPE_DIM, ps]
        put_kv(lw, ckv, kr_t, cache_len // rc + c)
        for hp in range(MLA_HEADS // 2):
            q2 = _dot(cq, lw["wuq"][:, hp * 2 * HEAD_PAD:(hp + 1) * 2 * HEAD_PAD]) * Q_SCALE
            for j in range(2):
                hd = 2 * hp + j
                c0 = j * HEAD_PAD
                q_s[hd, rows, :NOPE_DIM] = q2[:, c0:c0 + NOPE_DIM].astype(_BF)
                qr = q2[:, c0 + NOPE_DIM:c0 + HEAD_PAD]
                if has_rope:
                    qr = rotate(qr, r)
                q_s[hd, rows, NOPE_DIM:] = qr.astype(_BF)

    def mix(lw, c):
        r = c * rc
        rows = pl.ds(r, rc)
        row, win_ref = lw["row"], lw["win"]
        h = h_s[rows, :]
        for hp in range(MLA_HEADS // 2):
            g2 = _silu(_dot(h, win_ref[:, _C_GMLA + hp * 2 * V_DIM:_C_GMLA + (hp + 1) * 2 * V_DIM]))
            for j in range(2):
                hd = 2 * hp + j
                for s, ps in pieces:
                    rows_q = pl.ds(r + s * piece, piece)
                    if n_seq == 1:
                        ks = [kt_s[hd, kc] for kc in range(keys // rc)]
                        rows_k = pl.ds(0, keys)
                    else:
                        ks = [kt_s[hd, c, :, ps]]
                        rows_k = rows_q
                    q = q_s[hd, rows_q, :]
                    sc = jnp.concatenate([_dot(q, k) for k in ks], axis=-1)
                    p = jnp.exp2(sc - jnp.max(sc, axis=-1, keepdims=True))
                    ov = _dot(p.astype(_BF), v_s[hd, rows_k, :])
                    o = ov[:, :V_DIM] / ov[:, V_DIM:V_DIM + 1]
                    mixed_s[rows_q, hd * V_DIM:(hd + 1) * V_DIM] = (g2[ps, j * V_DIM:(j + 1) * V_DIM] * o).astype(_BF)

        windows = [pl.ds(halo_base(c, s), piece + 2 * HALO) for s, _ in pieces]
        rcnt = rcnt_ref[pl.ds(0 if seq_len <= rc else r, piece), :]
        pooled = jnp.concatenate([_pool_mix(px_s[w, :], rcnt) for w in windows], axis=0)
        pool = _dot(pooled.astype(_BF), lw["poolw"][...]) * pools_ref[row, :]
        gp = _silu(_dot(h, win_ref[:, _C_GPOOL:_C_PX]))
        mixed_s[rows, MLA_WIDTH:MLA_WIDTH + POOL_WIDTH] = (gp * pool).astype(_BF)

        convw = [convw_ref[k, row, :] for k in range(3)]
        conv = jnp.concatenate([_short_conv(prod_s[w, :], convw) for w in windows], axis=0)
        e2 = _dot(h, win_ref[:, _C_CB:_C_END])
        mixed_s[rows, MLA_WIDTH + POOL_WIDTH:] = (_silu(e2[:, CONV_WIDTH:]) * (e2[:, :CONV_WIDTH] * conv)).astype(_BF)

        y_ref[rows, :] = y_ref[rows, :] + lw["gate"] * _dot(mixed_s[rows, :], lw["wout"][...])

    def one_layer(l, carry):
        @pl.when(first_step)
        def _():
            for k in range(len(_WEIGHTS)):
                weight_copy(l, k).wait()

        lw = layer_view(l)
        if cache_len:
            cached_keys(lw)
        firsts = [project(lw, c) for c in chunks]
        for c in chunks:
            project_heads(lw, c, firsts[c])
        for c in chunks:
            mix(lw, c)
        return carry

    lax.fori_loop(0, DEPTH, one_layer, 0)
    for c in chunks:
        rows = slice(c * rc, (c + 1) * rc)
        y_ref[rows, :] = _rms(y_ref[rows, :], gfin_ref[...])


def _mixer_pass(x2d, mod_all, wts, *, seq_len, cache=None, rope=None, emit_state):
    rows = x2d.shape[0]
    m = ROWS_PER_STEP
    n_seq = m // seq_len
    n_steps = rows // m
    n_all = rows // seq_len
    cache_len = 0 if cache is None else cache[0].shape[2]
    keys = cache_len + seq_len
    has_rope = rope is not None
    rc = ROW_CHUNK
    assert m % rc == 0 and cache_len % rc == 0
    assert (rc % seq_len == 0 and not cache_len) or (n_seq == 1 and seq_len % rc == 0 and not emit_state)

    def const(*shape):
        return pl.BlockSpec(shape, lambda i: (0,) * len(shape), pipeline_mode=pl.Buffered(1))

    args = [x2d, mod_all]
    in_specs = [pl.BlockSpec((m, D_MODEL), lambda i: (i, 0)), const(DEPTH, 3, MOD_ROWS, D_MODEL)]
    if cache_len:
        args += [cache[0], cache[1]]
        in_specs += [pl.BlockSpec((None, DEPTH, cache_len, KV_LORA), lambda i: (i, 0, 0, 0)),
                     pl.BlockSpec((None, DEPTH, ROPE_DIM, cache_len), lambda i: (i, 0, 0, 0))]
    if has_rope:
        args += [rope]
        in_specs += [const(seq_len, ROPE_PAD)]
    args += [_pool_rcnt(seq_len), wts["g_norm"], wts["g_q"], wts["g_kv"], wts["pool_s"], wts["conv_w"],
             wts["g_final"]]
    in_specs += [const(seq_len, POOL_WIDTH), const(DEPTH, D_MODEL), const(DEPTH, Q_LORA), const(DEPTH, KV_LORA),
                 const(DEPTH, POOL_WIDTH), const(3, DEPTH, CONV_WIDTH), const(1, D_MODEL)]
    weights = [wts[name] for name in _WEIGHTS]
    args += weights
    in_specs += [pl.BlockSpec(memory_space=pl.ANY)] * len(weights)

    out_shape = [jax.ShapeDtypeStruct((rows, D_MODEL), _F32)]
    out_specs = [pl.BlockSpec((m, D_MODEL), lambda i: (i, 0))]
    if emit_state:
        out_shape += [jax.ShapeDtypeStruct((n_all, DEPTH, seq_len, KV_LORA), _F32),
                      jax.ShapeDtypeStruct((n_all, DEPTH, ROPE_DIM, seq_len), _F32)]
        out_specs += [pl.BlockSpec((n_seq, DEPTH, seq_len, KV_LORA), lambda i: (i, 0, 0, 0)),
                      pl.BlockSpec((n_seq, DEPTH, ROPE_DIM, seq_len), lambda i: (i, 0, 0, 0))]

    kern = functools.partial(_pass_kernel, n_seq=n_seq, seq_len=seq_len, cache_len=cache_len,
                             has_rope=has_rope, emit_state=emit_state, rc=rc)
    return pl.pallas_call(
        kern,
        grid=(n_steps,),
        in_specs=in_specs,
        out_specs=out_specs,
        out_shape=out_shape,
        scratch_shapes=[
            pltpu.VMEM((m, D_MODEL), _BF),
            pltpu.VMEM((MLA_HEADS, m, HEAD_PAD), _BF),
            pltpu.VMEM((MLA_HEADS, n_seq * keys // rc, HEAD_PAD, rc), _BF),
            pltpu.VMEM((MLA_HEADS, n_seq * keys, 2 * V_DIM), _BF),
            pltpu.VMEM((n_seq * (seq_len + 2 * HALO), POOL_WIDTH), _F32),
            pltpu.VMEM((n_seq * (seq_len + 2 * HALO), CONV_WIDTH), _F32),
            pltpu.VMEM((m, MIX_WIDTH), _BF),
        ] + [pltpu.VMEM(w.shape, w.dtype) for w in weights] + [pltpu.SemaphoreType.DMA((DEPTH, len(weights)))],
        compiler_params=pltpu.CompilerParams(dimension_semantics=("arbitrary",),
                                             vmem_limit_bytes=VMEM_LIMIT_BYTES),
        name="mixer_pass_latent" if has_rope else "mixer_pass_context",
    )(*args)


def _pool_rcnt(seq_len):
    t = np.arange(seq_len)[:, None]
    half = np.repeat(1 << np.arange(POOL_GROUPS), POOL_GROUP_DIM)[None, :]
    cnt = np.minimum(t + half, seq_len) - np.maximum(t - half, 0)
    return jnp.asarray((1.0 / cnt).astype(np.float32))


def _rope_table(seq_len):
    rows = seq_len // GRID_W
    row = np.repeat(np.arange(rows), GRID_W).astype(np.float32)
    col = np.tile(np.arange(GRID_W), rows).astype(np.float32)
    inv = (1.0 / (np.float32(ROPE_BASE) ** (np.arange(0, AXIS_DIM, 2, dtype=np.float32) / np.float32(AXIS_DIM))))
    inv = inv.astype(np.float32)
    ang = np.concatenate([row[:, None] * inv, col[:, None] * inv], axis=-1).astype(np.float64)
    cos, sin = np.cos(ang).astype(np.float32), np.sin(ang).astype(np.float32)
    return jnp.asarray(np.concatenate([cos, cos, sin, sin], axis=-1))


def kernel(x_prompt, x_sample, cache_ckv, cache_krope, c, c_ctx, w_mod, b_mod, g_norm, w_in, g_q, w_uq,
           g_kv, w_ukv, pool_w, pool_s, conv_w, w_out, g_final):
    batch, seq, _ = x_prompt.shape
    dec_batch, dec_seq, _ = x_sample.shape
    assert 1 + dec_batch <= MOD_ROWS and dec_seq == ROWS_PER_STEP and ROWS_PER_STEP % seq == 0

    c_all = jnp.concatenate([c_ctx[None], c, jnp.zeros((MOD_ROWS - 1 - dec_batch, D_MODEL), _F32)], axis=0)
    mod_all = _modulation(c_all, w_mod, b_mod)
    w_in_r, w_out_r, w_uq_r, w_ukt, w_uv, pool_w_r = _prep_weights(
        jnp.transpose(w_in, (0, 2, 1)), w_out, w_uq, w_ukv, pool_w)
    wts = {
        "g_norm": g_norm, "w_in": w_in_r, "g_q": g_q, "w_uq": w_uq_r, "g_kv": g_kv, "w_ukt": w_ukt,
        "w_uv": w_uv, "pool_w": pool_w_r, "pool_s": pool_s, "conv_w": jnp.transpose(conv_w, (1, 0, 2)),
        "w_out": w_out_r, "g_final": g_final.reshape(1, D_MODEL),
    }

    rope = _rope_table(dec_seq)
    cache = (cache_ckv, jnp.transpose(cache_krope, (0, 1, 3, 2)))
    xp = x_prompt.reshape(batch * seq, D_MODEL)
    xs = x_sample.reshape(dec_batch * dec_seq, D_MODEL)
    xp, state_ckv, state_krope_t = _mixer_pass(xp, mod_all, wts, seq_len=seq, emit_state=True)
    (xs,) = _mixer_pass(xs, mod_all, wts, seq_len=dec_seq, cache=cache, rope=rope, emit_state=False)
    y_prompt = xp.reshape(batch, seq, D_MODEL)
    state_krope = jnp.transpose(state_krope_t, (0, 1, 3, 2))
    y_sample = xs.reshape(dec_batch, dec_seq, D_MODEL)
    return (y_prompt, y_sample, state_ckv, state_krope)
```

```python
import functools

import numpy as np
import jax
import jax.numpy as jnp
from jax import lax
from jax.experimental import pallas as pl
from jax.experimental.pallas import tpu as pltpu

D_MODEL = 1024
DEPTH = 2
GRID_W = 64
MLA_HEADS = 4
NOPE_DIM = 128
ROPE_DIM = 64
V_DIM = 128
QK_DIM = NOPE_DIM + ROPE_DIM
MLA_WIDTH = MLA_HEADS * V_DIM
Q_LORA = 384
KV_LORA = 256
POOL_GROUPS = 4
POOL_GROUP_DIM = 64
POOL_WIDTH = POOL_GROUPS * POOL_GROUP_DIM
CONV_WIDTH = 256
MIX_WIDTH = MLA_WIDTH + POOL_WIDTH + CONV_WIDTH
ROPE_BASE = 10000.0
AXIS_DIM = ROPE_DIM // 2
ATTN_SCALE = QK_DIM ** -0.5
Q_SCALE = ATTN_SCALE * float(np.log2(np.e))
EPS = 1e-6

_SPLITS = (Q_LORA, KV_LORA, ROPE_DIM, MLA_WIDTH, POOL_WIDTH, POOL_WIDTH,
           CONV_WIDTH, CONV_WIDTH, CONV_WIDTH, CONV_WIDTH)
_OFFS = [sum(_SPLITS[:i]) for i in range(len(_SPLITS) + 1)]
IN_WIDTH = _OFFS[-1]

V7X_SUBLANES = 8
V7X_MXU_DEPTH = 256
V7X_VMEM_BYTES = 64 * 1024 * 1024
HEAD_PAD = V7X_MXU_DEPTH
ROPE_PAD = HEAD_PAD - NOPE_DIM
KV_HEAD = NOPE_DIM + V_DIM

_C_CQ = 0
_C_CKV = _C_CQ + Q_LORA
_C_KR = _C_CKV + KV_LORA
_C_GMLA = _C_KR + ROPE_PAD
_C_GPOOL = _C_GMLA + MLA_WIDTH
_C_PX = _C_GPOOL + POOL_WIDTH
_C_CC = _C_PX + POOL_WIDTH
_C_CB = _C_CC + 2 * CONV_WIDTH
_C_END = _C_CB + 2 * CONV_WIDTH

MOD_ROWS = 8
ROWS_PER_STEP = 1024
ROW_CHUNK = 512
HALO = V7X_SUBLANES
PREP_COLS = 256
VMEM_LIMIT_BYTES = V7X_VMEM_BYTES - 4 * 1024 * 1024

_WEIGHTS = ("w_in", "w_uq", "w_ukt", "w_uv", "pool_w", "w_out")

_BF = jnp.bfloat16
_F32 = jnp.float32


def _dot(a, b):
    return jnp.dot(a, b, preferred_element_type=_F32)


def _rms(x, g):
    return x * lax.rsqrt(jnp.mean(x * x, axis=-1, keepdims=True) + EPS) * g


def _silu(x):
    return x * jax.nn.sigmoid(x)


def _mod_kernel(c_ref, w_ref, b_ref, o_ref):
    s = _silu(c_ref[...]).astype(_BF)
    for k in range(3):
        cols = slice(k * D_MODEL, (k + 1) * D_MODEL)
        bias = b_ref[pl.ds(pl.program_id(0), 1), cols]
        o_ref[k] = _dot(s, w_ref[:, cols].astype(_BF)) + bias


def _modulation(c_all, w_mod, b_mod):
    return pl.pallas_call(
        _mod_kernel,
        grid=(DEPTH,),
        in_specs=[
            pl.BlockSpec((MOD_ROWS, D_MODEL), lambda l: (0, 0)),
            pl.BlockSpec((None, D_MODEL, 3 * D_MODEL), lambda l: (l, 0, 0)),
            pl.BlockSpec((DEPTH, 3 * D_MODEL), lambda l: (0, 0)),
        ],
        out_specs=pl.BlockSpec((None, 3, MOD_ROWS, D_MODEL), lambda l: (l, 0, 0, 0)),
        out_shape=jax.ShapeDtypeStruct((DEPTH, 3, MOD_ROWS, D_MODEL), _F32),
        compiler_params=pltpu.CompilerParams(dimension_semantics=("arbitrary",),
                                             vmem_limit_bytes=VMEM_LIMIT_BYTES),
        name="modulation",
    )(c_all, w_mod, b_mod)


def _prep_kernel(wint_ref, wout_ref, wuq_ref, wukv_ref, poolw_ref,
                 win_o, wout_o, wuq_o, wukt_o, wuv_o, poolw_o):
    def put(dst, src, width):
        for c in range(0, width, PREP_COLS):
            n = min(PREP_COLS, width - c)
            win_o[:, dst + c:dst + c + n] = wint_ref[src + c:src + c + n, :].T.astype(_BF)

    put(_C_CQ, _OFFS[0], Q_LORA + KV_LORA)
    kr = wint_ref[_OFFS[2]:_OFFS[3], :]
    krb = jnp.concatenate([kr, -kr[AXIS_DIM:], kr[:AXIS_DIM]], axis=0)
    win_o[:, _C_KR:_C_GMLA] = krb.T.astype(_BF)
    put(_C_GMLA, _OFFS[3], MLA_WIDTH)
    put(_C_GPOOL, _OFFS[5], POOL_WIDTH)
    put(_C_PX, _OFFS[4], POOL_WIDTH)
    put(_C_CC, _OFFS[7], 2 * CONV_WIDTH)
    put(_C_CB, _OFFS[6], CONV_WIDTH)
    put(_C_CB + CONV_WIDTH, _OFFS[9], CONV_WIDTH)
    wout_o[...] = wout_ref[...].astype(_BF)

    for hd in range(MLA_HEADS):
        s0, d0 = hd * QK_DIM, hd * HEAD_PAD
        r0 = s0 + NOPE_DIM
        wuq_o[:, d0:d0 + QK_DIM] = wuq_ref[:, s0:s0 + QK_DIM].astype(_BF)
        wuq_o[:, d0 + QK_DIM:d0 + QK_DIM + AXIS_DIM] = (-wuq_ref[:, r0 + AXIS_DIM:r0 + ROPE_DIM]).astype(_BF)
        wuq_o[:, d0 + QK_DIM + AXIS_DIM:d0 + HEAD_PAD] = wuq_ref[:, r0:r0 + AXIS_DIM].astype(_BF)
    for hd in range(MLA_HEADS):
        k0 = hd * KV_HEAD
        wukt_o[hd * NOPE_DIM:(hd + 1) * NOPE_DIM, :] = wukv_ref[:, k0:k0 + NOPE_DIM].T.astype(_BF)
        wuv_o[:, hd * V_DIM:(hd + 1) * V_DIM] = wukv_ref[:, k0 + NOPE_DIM:k0 + KV_HEAD].astype(_BF)
    poolw_o[...] = jnp.zeros((POOL_WIDTH, POOL_WIDTH), _BF)
    for g in range(POOL_GROUPS):
        sl = slice(g * POOL_GROUP_DIM, (g + 1) * POOL_GROUP_DIM)
        poolw_o[sl, sl] = poolw_ref[g].astype(_BF)


def _prep_weights(w_in_t, w_out, w_uq, w_ukv, pool_w):
    per_layer = lambda *shape: pl.BlockSpec((None,) + shape, lambda l: (l,) + (0,) * len(shape))
    return pl.pallas_call(
        _prep_kernel,
        grid=(DEPTH,),
        in_specs=[
            per_layer(IN_WIDTH, D_MODEL),
            per_layer(MIX_WIDTH, D_MODEL),
            per_layer(Q_LORA, MLA_HEADS * QK_DIM),
            per_layer(KV_LORA, MLA_HEADS * KV_HEAD),
            per_layer(POOL_GROUPS, POOL_GROUP_DIM, POOL_GROUP_DIM),
        ],
        out_specs=[
            per_layer(D_MODEL, _C_END),
            per_layer(MIX_WIDTH, D_MODEL),
            per_layer(Q_LORA, MLA_HEADS * HEAD_PAD),
            per_layer(MLA_HEADS * NOPE_DIM, KV_LORA),
            per_layer(KV_LORA, MLA_WIDTH),
            per_layer(POOL_WIDTH, POOL_WIDTH),
        ],
        out_shape=[
            jax.ShapeDtypeStruct((DEPTH, D_MODEL, _C_END), _BF),
            jax.ShapeDtypeStruct((DEPTH, MIX_WIDTH, D_MODEL), _BF),
            jax.ShapeDtypeStruct((DEPTH, Q_LORA, MLA_HEADS * HEAD_PAD), _BF),
            jax.ShapeDtypeStruct((DEPTH, MLA_HEADS * NOPE_DIM, KV_LORA), _BF),
            jax.ShapeDtypeStruct((DEPTH, KV_LORA, MLA_WIDTH), _BF),
            jax.ShapeDtypeStruct((DEPTH, POOL_WIDTH, POOL_WIDTH), _BF),
        ],
        compiler_params=pltpu.CompilerParams(dimension_semantics=("arbitrary",),
                                             vmem_limit_bytes=VMEM_LIMIT_BYTES),
        name="weight_prep",
    )(w_in_t, w_out, w_uq, w_ukv, pool_w)


def _pool_mix(win, rcnt):
    n = win.shape[0]
    rows = n - 2 * HALO
    lane = lax.broadcasted_iota(jnp.int32, (1, POOL_WIDTH), 1)
    sums = [win + pltpu.roll(win, 1, axis=0)]
    for k in (1, 2, 4):
        sums.append(pltpu.roll(sums[-1], k, axis=0) + pltpu.roll(sums[-1], n - k, axis=0))
    sel = sums[-1]
    for g in range(POOL_GROUPS - 2, -1, -1):
        sel = jnp.where(lane < (g + 1) * POOL_GROUP_DIM, sums[g], sel)
    return sel[HALO:HALO + rows] * rcnt - win[HALO:HALO + rows]


def _short_conv(win, w):
    rows = win.shape[0] - 2 * HALO
    return (win[HALO - 1:HALO - 1 + rows] * w[0] + win[HALO:HALO + rows] * w[1]
            + win[HALO + 1:HALO + 1 + rows] * w[2])


def _pass_kernel(*refs, n_seq, seq_len, cache_len, has_rope, emit_state, rc):
    it = iter(refs)
    x_ref, mod_ref = next(it), next(it)
    if cache_len:
        cckv_all_ref, ckr_all_ref = next(it), next(it)
    if has_rope:
        cs_ref = next(it)
    rcnt_ref, gn_ref, gq_ref, gkv_ref, pools_ref, convw_ref, gfin_ref = (next(it) for _ in range(7))
    w_hbm = [next(it) for _ in range(len(_WEIGHTS))]
    y_ref = next(it)
    if emit_state:
        ckv_out_ref, kr_out_ref = next(it), next(it)
    h_s, q_s, kt_s, v_s, px_s, prod_s, mixed_s = (next(it) for _ in range(7))
    w_vm = [next(it) for _ in range(len(_WEIGHTS))]
    w_sem = next(it)
    win_all_ref, wuq_all_ref, wukt_all_ref, wuv_all_ref, poolw_all_ref, wout_all_ref = w_vm

    def weight_copy(l, k):
        return pltpu.make_async_copy(w_hbm[k].at[l], w_vm[k].at[l], w_sem.at[l, k])

    first_step = pl.program_id(0) == 0

    @pl.when(first_step)
    def _():
        for l in range(DEPTH):
            for k in range(len(_WEIGHTS)):
                weight_copy(l, k).start()

    m = n_seq * seq_len
    keys = cache_len + seq_len
    padded = seq_len + 2 * HALO
    lane = lax.broadcasted_iota(jnp.int32, (1, ROPE_PAD), 1)
    mrow = (1 + pl.program_id(0)) if has_rope else 0
    piece = min(seq_len, rc)
    n_sub = rc // piece
    pieces = [(s, slice(s * piece, (s + 1) * piece)) for s in range(n_sub)]
    chunks = range(m // rc)

    for c in chunks:
        y_ref[c * rc:(c + 1) * rc, :] = x_ref[c * rc:(c + 1) * rc, :]

    zeros_halo = jnp.zeros((HALO, POOL_WIDTH), _F32)
    for buf in (px_s, prod_s):
        for s in range(n_seq):
            buf[s * padded:s * padded + HALO, :] = zeros_halo
            buf[(s + 1) * padded - HALO:(s + 1) * padded, :] = zeros_halo
    ones_col = jnp.where(lax.broadcasted_iota(jnp.int32, (n_seq * keys, V_DIM), 1) == 0, 1.0, 0.0).astype(_BF)
    for hd in range(MLA_HEADS):
        v_s[hd, :, V_DIM:] = ones_col

    def layer_view(l):
        lw = dict(l=l, row=pl.ds(l, 1), win=win_all_ref.at[l], wuq=wuq_all_ref.at[l], wukt=wukt_all_ref.at[l],
                  wuv=wuv_all_ref.at[l], poolw=poolw_all_ref.at[l], wout=wout_all_ref.at[l])
        lw["shift"], lw["scale"], lw["gate"] = (mod_ref[l, k, pl.ds(mrow, 1), :] for k in range(3))
        if cache_len:
            lw["cckv"], lw["ckr"] = cckv_all_ref.at[l], ckr_all_ref.at[l]
        return lw

    def rotate(v, r):
        v = v * cs_ref[pl.ds(r, rc), :]
        return v + pltpu.roll(v, ROPE_DIM, axis=1)

    def put_kv(lw, ckv, kr_t, kc):
        knt = _dot(lw["wukt"][...], ckv.T.astype(_BF))
        v4 = _dot(ckv.astype(_BF), lw["wuv"][...])
        kr_bf = kr_t.astype(_BF)
        rows_k = pl.ds(kc * rc, rc)
        for hd in range(MLA_HEADS):
            kt_s[hd, kc, :NOPE_DIM, :] = knt[hd * NOPE_DIM:(hd + 1) * NOPE_DIM, :].astype(_BF)
            kt_s[hd, kc, NOPE_DIM:, :] = kr_bf
            v_s[hd, rows_k, :V_DIM] = v4[:, hd * V_DIM:(hd + 1) * V_DIM].astype(_BF)

    def cached_keys(lw):
        ckr_t = lw["ckr"][...]
        ckr_pad = jnp.concatenate([ckr_t, jnp.zeros_like(ckr_t)], axis=0)
        for cc in range(cache_len // rc):
            rows_k = slice(cc * rc, (cc + 1) * rc)
            put_kv(lw, lw["cckv"][rows_k, :], ckr_pad[:, rows_k], cc)

    def halo_base(c, s):
        return (c * n_sub + s) * padded if seq_len <= rc else c * rc

    def project(lw, c):
        r = c * rc
        rows = pl.ds(r, rc)
        row, win_ref = lw["row"], lw["win"]
        h = (_rms(y_ref[rows, :], gn_ref[row, :]) * (1.0 + lw["scale"]) + lw["shift"]).astype(_BF)
        h_s[rows, :] = h
        a = _dot(h, win_ref[:, _C_CQ:_C_GMLA])
        pe = _dot(h, win_ref[:, _C_PX:_C_CB])
        for s, ps in pieces:
            rows_h = pl.ds(halo_base(c, s) + HALO, piece)
            px_s[rows_h, :] = pe[ps, :POOL_WIDTH]
            prod_s[rows_h, :] = pe[ps, POOL_WIDTH:POOL_WIDTH + CONV_WIDTH] * pe[ps, POOL_WIDTH + CONV_WIDTH:]
        return a

    def project_heads(lw, c, a):
        r = c * rc
        rows = pl.ds(r, rc)
        row = lw["row"]
        cq = _rms(a[:, :Q_LORA], gq_ref[row, :]).astype(_BF)
        ckv = _rms(a[:, _C_CKV:_C_KR], gkv_ref[row, :])
        kr = a[:, _C_KR:]
        if has_rope:
            kr = rotate(kr, r)
        kr_t = jnp.where(lane < ROPE_DIM, kr, 0.0).T
        if emit_state:
            for s, ps in pieces:
                ckv_out_ref[c * n_sub + s, lw["l"]] = ckv[ps, :]
                kr_out_ref[c * n_sub + s, lw["l"]] = kr_t[:ROPE_DIM, ps]
        put_kv(lw, ckv, kr_t, cache_len // rc + c)
        for hp in range(MLA_HEADS // 2):
            q2 = _dot(cq, lw["wuq"][:, hp * 2 * HEAD_PAD:(hp + 1) * 2 * HEAD_PAD]) * Q_SCALE
            for j in range(2):
                hd = 2 * hp + j
                c0 = j * HEAD_PAD
                q_s[hd, rows, :NOPE_DIM] = q2[:, c0:c0 + NOPE_DIM].astype(_BF)
                qr = q2[:, c0 + NOPE_DIM:c0 + HEAD_PAD]
                if has_rope:
                    qr = rotate(qr, r)
                q_s[hd, rows, NOPE_DIM:] = qr.astype(_BF)

    def mix(lw, c):
        r = c * rc
        rows = pl.ds(r, rc)
        row, win_ref = lw["row"], lw["win"]
        h = h_s[rows, :]
        for hp in range(MLA_HEADS // 2):
            g2 = _silu(_dot(h, win_ref[:, _C_GMLA + hp * 2 * V_DIM:_C_GMLA + (hp + 1) * 2 * V_DIM]))
            for j in range(2):
                hd = 2 * hp + j
                for s, ps in pieces:
                    rows_q = pl.ds(r + s * piece, piece)
                    if n_seq == 1:
                        ks = [kt_s[hd, kc] for kc in range(keys // rc)]
                        rows_k = pl.ds(0, keys)
                    else:
                        ks = [kt_s[hd, c, :, ps]]
                        rows_k = rows_q
                    q = q_s[hd, rows_q, :]
                    sc = jnp.concatenate([_dot(q, k) for k in ks], axis=-1)
                    p = jnp.exp2(sc - jnp.max(sc, axis=-1, keepdims=True))
                    ov = _dot(p.astype(_BF), v_s[hd, rows_k, :])
                    o = ov[:, :V_DIM] / ov[:, V_DIM:V_DIM + 1]
                    mixed_s[rows_q, hd * V_DIM:(hd + 1) * V_DIM] = (g2[ps, j * V_DIM:(j + 1) * V_DIM] * o).astype(_BF)

        windows = [pl.ds(halo_base(c, s), piece + 2 * HALO) for s, _ in pieces]
        rcnt = rcnt_ref[pl.ds(0 if seq_len <= rc else r, piece), :]
        pooled = jnp.concatenate([_pool_mix(px_s[w, :], rcnt) for w in windows], axis=0)
        pool = _dot(pooled.astype(_BF), lw["poolw"][...]) * pools_ref[row, :]
        gp = _silu(_dot(h, win_ref[:, _C_GPOOL:_C_PX]))
        mixed_s[rows, MLA_WIDTH:MLA_WIDTH + POOL_WIDTH] = (gp * pool).astype(_BF)

        convw = [convw_ref[k, row, :] for k in range(3)]
        conv = jnp.concatenate([_short_conv(prod_s[w, :], convw) for w in windows], axis=0)
        e2 = _dot(h, win_ref[:, _C_CB:_C_END])
        mixed_s[rows, MLA_WIDTH + POOL_WIDTH:] = (_silu(e2[:, CONV_WIDTH:]) * (e2[:, :CONV_WIDTH] * conv)).astype(_BF)

        y_ref[rows, :] = y_ref[rows, :] + lw["gate"] * _dot(mixed_s[rows, :], lw["wout"][...])

    def one_layer(l, carry):
        @pl.when(first_step)
        def _():
            for k in range(len(_WEIGHTS)):
                weight_copy(l, k).wait()

        lw = layer_view(l)
        firsts = [project(lw, c) for c in chunks]
        if cache_len:
            cached_keys(lw)
        for c in chunks:
            project_heads(lw, c, firsts[c])
        for c in chunks:
            mix(lw, c)
        return carry

    lax.fori_loop(0, DEPTH, one_layer, 0)
    for c in chunks:
        rows = slice(c * rc, (c + 1) * rc)
        y_ref[rows, :] = _rms(y_ref[rows, :], gfin_ref[...])


def _mixer_pass(x2d, mod_all, wts, *, seq_len, cache=None, rope=None, emit_state):
    rows = x2d.shape[0]
    m = ROWS_PER_STEP
    n_seq = m // seq_len
    n_steps = rows // m
    n_all = rows // seq_len
    cache_len = 0 if cache is None else cache[0].shape[2]
    keys = cache_len + seq_len
    has_rope = rope is not None
    rc = ROW_CHUNK
    assert m % rc == 0 and cache_len % rc == 0
    assert (rc % seq_len == 0 and not cache_len) or (n_seq == 1 and seq_len % rc == 0 and not emit_state)

    def const(*shape):
        return pl.BlockSpec(shape, lambda i: (0,) * len(shape), pipeline_mode=pl.Buffered(1))

    args = [x2d, mod_all]
    in_specs = [pl.BlockSpec((m, D_MODEL), lambda i: (i, 0)), const(DEPTH, 3, MOD_ROWS, D_MODEL)]
    if cache_len:
        args += [cache[0], cache[1]]
        in_specs += [pl.BlockSpec((None, DEPTH, cache_len, KV_LORA), lambda i: (i, 0, 0, 0)),
                     pl.BlockSpec((None, DEPTH, ROPE_DIM, cache_len), lambda i: (i, 0, 0, 0))]
    if has_rope:
        args += [rope]
        in_specs += [const(seq_len, ROPE_PAD)]
    args += [_pool_rcnt(seq_len), wts["g_norm"], wts["g_q"], wts["g_kv"], wts["pool_s"], wts["conv_w"],
             wts["g_final"]]
    in_specs += [const(seq_len, POOL_WIDTH), const(DEPTH, D_MODEL), const(DEPTH, Q_LORA), const(DEPTH, KV_LORA),
                 const(DEPTH, POOL_WIDTH), const(3, DEPTH, CONV_WIDTH), const(1, D_MODEL)]
    weights = [wts[name] for name in _WEIGHTS]
    args += weights
    in_specs += [pl.BlockSpec(memory_space=pl.ANY)] * len(weights)

    out_shape = [jax.ShapeDtypeStruct((rows, D_MODEL), _F32)]
    out_specs = [pl.BlockSpec((m, D_MODEL), lambda i: (i, 0))]
    if emit_state:
        out_shape += [jax.ShapeDtypeStruct((n_all, DEPTH, seq_len, KV_LORA), _F32),
                      jax.ShapeDtypeStruct((n_all, DEPTH, ROPE_DIM, seq_len), _F32)]
        out_specs += [pl.BlockSpec((n_seq, DEPTH, seq_len, KV_LORA), lambda i: (i, 0, 0, 0)),
                      pl.BlockSpec((n_seq, DEPTH, ROPE_DIM, seq_len), lambda i: (i, 0, 0, 0))]

    kern = functools.partial(_pass_kernel, n_seq=n_seq, seq_len=seq_len, cache_len=cache_len,
                             has_rope=has_rope, emit_state=emit_state, rc=rc)
    return pl.pallas_call(
        kern,
        grid=(n_steps,),
        in_specs=in_specs,
        out_specs=out_specs,
        out_shape=out_shape,
        scratch_shapes=[
            pltpu.VMEM((m, D_MODEL), _BF),
            pltpu.VMEM((MLA_HEADS, m, HEAD_PAD), _BF),
            pltpu.VMEM((MLA_HEADS, n_seq * keys // rc, HEAD_PAD, rc), _BF),
            pltpu.VMEM((MLA_HEADS, n_seq * keys, 2 * V_DIM), _BF),
            pltpu.VMEM((n_seq * (seq_len + 2 * HALO), POOL_WIDTH), _F32),
            pltpu.VMEM((n_seq * (seq_len + 2 * HALO), CONV_WIDTH), _F32),
            pltpu.VMEM((m, MIX_WIDTH), _BF),
        ] + [pltpu.VMEM(w.shape, w.dtype) for w in weights] + [pltpu.SemaphoreType.DMA((DEPTH, len(weights)))],
        compiler_params=pltpu.CompilerParams(dimension_semantics=("arbitrary",),
                                             vmem_limit_bytes=VMEM_LIMIT_BYTES),
        name="mixer_pass_latent" if has_rope else "mixer_pass_context",
    )(*args)


def _pool_rcnt(seq_len):
    t = np.arange(seq_len)[:, None]
    half = np.repeat(1 << np.arange(POOL_GROUPS), POOL_GROUP_DIM)[None, :]
    cnt = np.minimum(t + half, seq_len) - np.maximum(t - half, 0)
    return jnp.asarray((1.0 / cnt).astype(np.float32))


def _rope_table(seq_len):
    rows = seq_len // GRID_W
    row = np.repeat(np.arange(rows), GRID_W).astype(np.float32)
    col = np.tile(np.arange(GRID_W), rows).astype(np.float32)
    inv = (1.0 / (np.float32(ROPE_BASE) ** (np.arange(0, AXIS_DIM, 2, dtype=np.float32) / np.float32(AXIS_DIM))))
    inv = inv.astype(np.float32)
    ang = np.concatenate([row[:, None] * inv, col[:, None] * inv], axis=-1).astype(np.float64)
    cos, sin = np.cos(ang).astype(np.float32), np.sin(ang).astype(np.float32)
    return jnp.asarray(np.concatenate([cos, cos, sin, sin], axis=-1))


def kernel(x_prompt, x_sample, cache_ckv, cache_krope, c, c_ctx, w_mod, b_mod, g_norm, w_in, g_q, w_uq,
           g_kv, w_ukv, pool_w, pool_s, conv_w, w_out, g_final):
    batch, seq, _ = x_prompt.shape
    dec_batch, dec_seq, _ = x_sample.shape
    assert 1 + dec_batch <= MOD_ROWS and dec_seq == ROWS_PER_STEP and ROWS_PER_STEP % seq == 0

    c_all = jnp.concatenate([c_ctx[None], c, jnp.zeros((MOD_ROWS - 1 - dec_batch, D_MODEL), _F32)], axis=0)
    mod_all = _modulation(c_all, w_mod, b_mod)
    w_in_r, w_out_r, w_uq_r, w_ukt, w_uv, pool_w_r = _prep_weights(
        jnp.transpose(w_in, (0, 2, 1)), w_out, w_uq, w_ukv, pool_w)
    wts = {
        "g_norm": g_norm, "w_in": w_in_r, "g_q": g_q, "w_uq": w_uq_r, "g_kv": g_kv, "w_ukt": w_ukt,
        "w_uv": w_uv, "pool_w": pool_w_r, "pool_s": pool_s, "conv_w": jnp.transpose(conv_w, (1, 0, 2)),
        "w_out": w_out_r, "g_final": g_final.reshape(1, D_MODEL),
    }

    rope = _rope_table(dec_seq)
    cache = (cache_ckv, jnp.transpose(cache_krope, (0, 1, 3, 2)))
    xp = x_prompt.reshape(batch * seq, D_MODEL)
    xs = x_sample.reshape(dec_batch * dec_seq, D_MODEL)
    xp, state_ckv, state_krope_t = _mixer_pass(xp, mod_all, wts, seq_len=seq, emit_state=True)
    (xs,) = _mixer_pass(xs, mod_all, wts, seq_len=dec_seq, cache=cache, rope=rope, emit_state=False)
    y_prompt = xp.reshape(batch, seq, D_MODEL)
    state_krope = jnp.transpose(state_krope_t, (0, 1, 3, 2))
    y_sample = xs.reshape(dec_batch, dec_seq, D_MODEL)
    return (y_prompt, y_sample, state_ckv, state_krope)
```

```python
import functools

import numpy as np
import jax
import jax.numpy as jnp
from jax import lax
from jax.experimental import pallas as pl
from jax.experimental.pallas import tpu as pltpu

D_MODEL = 1024
DEPTH = 2
GRID_W = 64
MLA_HEADS = 4
NOPE_DIM = 128
ROPE_DIM = 64
V_DIM = 128
QK_DIM = NOPE_DIM + ROPE_DIM
MLA_WIDTH = MLA_HEADS * V_DIM
Q_LORA = 384
KV_LORA = 256
POOL_GROUPS = 4
POOL_GROUP_DIM = 64
POOL_WIDTH = POOL_GROUPS * POOL_GROUP_DIM
CONV_WIDTH = 256
MIX_WIDTH = MLA_WIDTH + POOL_WIDTH + CONV_WIDTH
ROPE_BASE = 10000.0
AXIS_DIM = ROPE_DIM // 2
ATTN_SCALE = QK_DIM ** -0.5
Q_SCALE = ATTN_SCALE * float(np.log2(np.e))
EPS = 1e-6

_SPLITS = (Q_LORA, KV_LORA, ROPE_DIM, MLA_WIDTH, POOL_WIDTH, POOL_WIDTH,
           CONV_WIDTH, CONV_WIDTH, CONV_WIDTH, CONV_WIDTH)
_OFFS = [sum(_SPLITS[:i]) for i in range(len(_SPLITS) + 1)]
IN_WIDTH = _OFFS[-1]

V7X_SUBLANES = 8
V7X_MXU_DEPTH = 256
V7X_VMEM_BYTES = 64 * 1024 * 1024
HEAD_PAD = V7X_MXU_DEPTH
ROPE_PAD = HEAD_PAD - NOPE_DIM
KV_HEAD = NOPE_DIM + V_DIM

_C_CQ = 0
_C_CKV = _C_CQ + Q_LORA
_C_KR = _C_CKV + KV_LORA
_C_GMLA = _C_KR + ROPE_PAD
_C_GPOOL = _C_GMLA + MLA_WIDTH
_C_PX = _C_GPOOL + POOL_WIDTH
_C_CC = _C_PX + POOL_WIDTH
_C_CB = _C_CC + 2 * CONV_WIDTH
_C_END = _C_CB + 2 * CONV_WIDTH

MOD_ROWS = 8
ROWS_PER_STEP = 1024
ROW_CHUNK = 512
HALO = V7X_SUBLANES
PREP_COLS = 256
VMEM_LIMIT_BYTES = V7X_VMEM_BYTES - 4 * 1024 * 1024

_WEIGHTS = ("w_in", "w_uq", "w_ukt", "w_uv", "pool_w", "w_out")

_BF = jnp.bfloat16
_F32 = jnp.float32


def _dot(a, b):
    return jnp.dot(a, b, preferred_element_type=_F32)


def _rms(x, g):
    return x * lax.rsqrt(jnp.mean(x * x, axis=-1, keepdims=True) + EPS) * g


def _silu(x):
    return x * jax.nn.sigmoid(x)


def _mod_kernel(c_ref, w_ref, b_ref, o_ref):
    s = _silu(c_ref[...]).astype(_BF)
    for k in range(3):
        cols = slice(k * D_MODEL, (k + 1) * D_MODEL)
        bias = b_ref[pl.ds(pl.program_id(0), 1), cols]
        o_ref[k] = _dot(s, w_ref[:, cols].astype(_BF)) + bias


def _modulation(c_all, w_mod, b_mod):
    return pl.pallas_call(
        _mod_kernel,
        grid=(DEPTH,),
        in_specs=[
            pl.BlockSpec((MOD_ROWS, D_MODEL), lambda l: (0, 0)),
            pl.BlockSpec((None, D_MODEL, 3 * D_MODEL), lambda l: (l, 0, 0)),
            pl.BlockSpec((DEPTH, 3 * D_MODEL), lambda l: (0, 0)),
        ],
        out_specs=pl.BlockSpec((None, 3, MOD_ROWS, D_MODEL), lambda l: (l, 0, 0, 0)),
        out_shape=jax.ShapeDtypeStruct((DEPTH, 3, MOD_ROWS, D_MODEL), _F32),
        compiler_params=pltpu.CompilerParams(dimension_semantics=("arbitrary",),
                                             vmem_limit_bytes=VMEM_LIMIT_BYTES),
        name="modulation",
    )(c_all, w_mod, b_mod)


def _prep_kernel(wint_ref, wout_ref, wuq_ref, wukv_ref, poolw_ref,
                 win_o, wout_o, wuq_o, wukt_o, wuv_o, poolw_o):
    def put(dst, src, width):
        for c in range(0, width, PREP_COLS):
            n = min(PREP_COLS, width - c)
            win_o[:, dst + c:dst + c + n] = wint_ref[src + c:src + c + n, :].T.astype(_BF)

    put(_C_CQ, _OFFS[0], Q_LORA + KV_LORA)
    kr = wint_ref[_OFFS[2]:_OFFS[3], :]
    krb = jnp.concatenate([kr, -kr[AXIS_DIM:], kr[:AXIS_DIM]], axis=0)
    win_o[:, _C_KR:_C_GMLA] = krb.T.astype(_BF)
    put(_C_GMLA, _OFFS[3], MLA_WIDTH)
    put(_C_GPOOL, _OFFS[5], POOL_WIDTH)
    put(_C_PX, _OFFS[4], POOL_WIDTH)
    put(_C_CC, _OFFS[7], 2 * CONV_WIDTH)
    put(_C_CB, _OFFS[6], CONV_WIDTH)
    put(_C_CB + CONV_WIDTH, _OFFS[9], CONV_WIDTH)
    wout_o[...] = wout_ref[...].astype(_BF)

    for hd in range(MLA_HEADS):
        s0, d0 = hd * QK_DIM, hd * HEAD_PAD
        r0 = s0 + NOPE_DIM
        wuq_o[:, d0:d0 + QK_DIM] = wuq_ref[:, s0:s0 + QK_DIM].astype(_BF)
        wuq_o[:, d0 + QK_DIM:d0 + QK_DIM + AXIS_DIM] = (-wuq_ref[:, r0 + AXIS_DIM:r0 + ROPE_DIM]).astype(_BF)
        wuq_o[:, d0 + QK_DIM + AXIS_DIM:d0 + HEAD_PAD] = wuq_ref[:, r0:r0 + AXIS_DIM].astype(_BF)
    for hd in range(MLA_HEADS):
        k0 = hd * KV_HEAD
        wukt_o[hd * NOPE_DIM:(hd + 1) * NOPE_DIM, :] = wukv_ref[:, k0:k0 + NOPE_DIM].T.astype(_BF)
        wuv_o[:, hd * V_DIM:(hd + 1) * V_DIM] = wukv_ref[:, k0 + NOPE_DIM:k0 + KV_HEAD].astype(_BF)
    poolw_o[...] = jnp.zeros((POOL_WIDTH, POOL_WIDTH), _BF)
    for g in range(POOL_GROUPS):
        sl = slice(g * POOL_GROUP_DIM, (g + 1) * POOL_GROUP_DIM)
        poolw_o[sl, sl] = poolw_ref[g].astype(_BF)


def _prep_weights(w_in_t, w_out, w_uq, w_ukv, pool_w):
    per_layer = lambda *shape: pl.BlockSpec((None,) + shape, lambda l: (l,) + (0,) * len(shape))
    return pl.pallas_call(
        _prep_kernel,
        grid=(DEPTH,),
        in_specs=[
            per_layer(IN_WIDTH, D_MODEL),
            per_layer(MIX_WIDTH, D_MODEL),
            per_layer(Q_LORA, MLA_HEADS * QK_DIM),
            per_layer(KV_LORA, MLA_HEADS * KV_HEAD),
            per_layer(POOL_GROUPS, POOL_GROUP_DIM, POOL_GROUP_DIM),
        ],
        out_specs=[
            per_layer(D_MODEL, _C_END),
            per_layer(MIX_WIDTH, D_MODEL),
            per_layer(Q_LORA, MLA_HEADS * HEAD_PAD),
            per_layer(MLA_HEADS * NOPE_DIM, KV_LORA),
            per_layer(KV_LORA, MLA_WIDTH),
            per_layer(POOL_WIDTH, POOL_WIDTH),
        ],
        out_shape=[
            jax.ShapeDtypeStruct((DEPTH, D_MODEL, _C_END), _BF),
            jax.ShapeDtypeStruct((DEPTH, MIX_WIDTH, D_MODEL), _BF),
            jax.ShapeDtypeStruct((DEPTH, Q_LORA, MLA_HEADS * HEAD_PAD), _BF),
            jax.ShapeDtypeStruct((DEPTH, MLA_HEADS * NOPE_DIM, KV_LORA), _BF),
            jax.ShapeDtypeStruct((DEPTH, KV_LORA, MLA_WIDTH), _BF),
            jax.ShapeDtypeStruct((DEPTH, POOL_WIDTH, POOL_WIDTH), _BF),
        ],
        compiler_params=pltpu.CompilerParams(dimension_semantics=("arbitrary",),
                                             vmem_limit_bytes=VMEM_LIMIT_BYTES),
        name="weight_prep",
    )(w_in_t, w_out, w_uq, w_ukv, pool_w)


def _pool_mix(win, rcnt):
    n = win.shape[0]
    rows = n - 2 * HALO
    lane = lax.broadcasted_iota(jnp.int32, (1, POOL_WIDTH), 1)
    sums = [win + pltpu.roll(win, 1, axis=0)]
    for k in (1, 2, 4):
        sums.append(pltpu.roll(sums[-1], k, axis=0) + pltpu.roll(sums[-1], n - k, axis=0))
    sel = sums[-1]
    for g in range(POOL_GROUPS - 2, -1, -1):
        sel = jnp.where(lane < (g + 1) * POOL_GROUP_DIM, sums[g], sel)
    return sel[HALO:HALO + rows] * rcnt - win[HALO:HALO + rows]


def _short_conv(win, w):
    rows = win.shape[0] - 2 * HALO
    return (win[HALO - 1:HALO - 1 + rows] * w[0] + win[HALO:HALO + rows] * w[1]
            + win[HALO + 1:HALO + 1 + rows] * w[2])


def _pass_kernel(*refs, n_seq, seq_len, cache_len, has_rope, emit_state, rc):
    it = iter(refs)
    x_ref, mod_ref = next(it), next(it)
    if cache_len:
        cckv_all_ref, ckr_all_ref = next(it), next(it)
    if has_rope:
        cs_ref = next(it)
    rcnt_ref, gn_ref, gq_ref, gkv_ref, pools_ref, convw_ref, gfin_ref = (next(it) for _ in range(7))
    w_hbm = [next(it) for _ in range(len(_WEIGHTS))]
    y_ref = next(it)
    if emit_state:
        ckv_out_ref, kr_out_ref = next(it), next(it)
    h_s, q_s, kt_s, v_s, px_s, prod_s, mixed_s = (next(it) for _ in range(7))
    w_vm = [next(it) for _ in range(len(_WEIGHTS))]
    w_sem = next(it)
    win_all_ref, wuq_all_ref, wukt_all_ref, wuv_all_ref, poolw_all_ref, wout_all_ref = w_vm

    def weight_copy(l, k):
        return pltpu.make_async_copy(w_hbm[k].at[l], w_vm[k].at[l], w_sem.at[l, k])

    first_step = pl.program_id(0) == 0

    @pl.when(first_step)
    def _():
        for l in range(DEPTH):
            for k in range(len(_WEIGHTS)):
                weight_copy(l, k).start()

    m = n_seq * seq_len
    keys = cache_len + seq_len
    padded = seq_len + 2 * HALO
    lane = lax.broadcasted_iota(jnp.int32, (1, ROPE_PAD), 1)
    mrow = (1 + pl.program_id(0)) if has_rope else 0
    piece = min(seq_len, rc)
    n_sub = rc // piece
    pieces = [(s, slice(s * piece, (s + 1) * piece)) for s in range(n_sub)]
    chunks = range(m // rc)

    for c in chunks:
        y_ref[c * rc:(c + 1) * rc, :] = x_ref[c * rc:(c + 1) * rc, :]

    zeros_halo = jnp.zeros((HALO, POOL_WIDTH), _F32)
    for buf in (px_s, prod_s):
        for s in range(n_seq):
            buf[s * padded:s * padded + HALO, :] = zeros_halo
            buf[(s + 1) * padded - HALO:(s + 1) * padded, :] = zeros_halo
    ones_col = jnp.where(lax.broadcasted_iota(jnp.int32, (n_seq * keys, V_DIM), 1) == 0, 1.0, 0.0).astype(_BF)
    for hd in range(MLA_HEADS):
        v_s[hd, :, V_DIM:] = ones_col

    def layer_view(l):
        lw = dict(l=l, row=pl.ds(l, 1), win=win_all_ref.at[l], wuq=wuq_all_ref.at[l], wukt=wukt_all_ref.at[l],
                  wuv=wuv_all_ref.at[l], poolw=poolw_all_ref.at[l], wout=wout_all_ref.at[l])
        lw["shift"], lw["scale"], lw["gate"] = (mod_ref[l, k, pl.ds(mrow, 1), :] for k in range(3))
        if cache_len:
            lw["cckv"], lw["ckr"] = cckv_all_ref.at[l], ckr_all_ref.at[l]
        return lw

    def rotate(v, r):
        v = v * cs_ref[pl.ds(r, rc), :]
        return v + pltpu.roll(v, ROPE_DIM, axis=1)

    def put_kv(lw, ckv, kr_t, kc):
        knt = _dot(lw["wukt"][...], ckv.T.astype(_BF))
        v4 = _dot(ckv.astype(_BF), lw["wuv"][...])
        kr_bf = kr_t.astype(_BF)
        rows_k = pl.ds(kc * rc, rc)
        for hd in range(MLA_HEADS):
            kt_s[hd, kc, :NOPE_DIM, :] = knt[hd * NOPE_DIM:(hd + 1) * NOPE_DIM, :].astype(_BF)
            kt_s[hd, kc, NOPE_DIM:, :] = kr_bf
            v_s[hd, rows_k, :V_DIM] = v4[:, hd * V_DIM:(hd + 1) * V_DIM].astype(_BF)

    def cached_keys(lw):
        ckr_t = lw["ckr"][...]
        ckr_pad = jnp.concatenate([ckr_t, jnp.zeros_like(ckr_t)], axis=0)
        for cc in range(cache_len // rc):
            rows_k = slice(cc * rc, (cc + 1) * rc)
            put_kv(lw, lw["cckv"][rows_k, :], ckr_pad[:, rows_k], cc)

    def halo_base(c, s):
        return (c * n_sub + s) * padded if seq_len <= rc else c * rc

    def project(lw, c):
        r = c * rc
        rows = pl.ds(r, rc)
        row, win_ref = lw["row"], lw["win"]
        h = (_rms(y_ref[rows, :], gn_ref[row, :]) * (1.0 + lw["scale"]) + lw["shift"]).astype(_BF)
        h_s[rows, :] = h
        a = _dot(h, win_ref[:, _C_CQ:_C_GMLA])
        pe = _dot(h, win_ref[:, _C_PX:_C_CB])
        for s, ps in pieces:
            rows_h = pl.ds(halo_base(c, s) + HALO, piece)
            px_s[rows_h, :] = pe[ps, :POOL_WIDTH]
            prod_s[rows_h, :] = pe[ps, POOL_WIDTH:POOL_WIDTH + CONV_WIDTH] * pe[ps, POOL_WIDTH + CONV_WIDTH:]
        return a

    def project_heads(lw, c, a):
        r = c * rc
        rows = pl.ds(r, rc)
        row = lw["row"]
        cq = _rms(a[:, :Q_LORA], gq_ref[row, :]).astype(_BF)
        ckv = _rms(a[:, _C_CKV:_C_KR], gkv_ref[row, :])
        kr = a[:, _C_KR:]
        if has_rope:
            kr = rotate(kr, r)
        kr_t = jnp.where(lane < ROPE_DIM, kr, 0.0).T
        if emit_state:
            for s, ps in pieces:
                ckv_out_ref[c * n_sub + s, lw["l"]] = ckv[ps, :]
                kr_out_ref[c * n_sub + s, lw["l"]] = kr_t[:ROPE_DIM, ps]
        put_kv(lw, ckv, kr_t, cache_len // rc + c)
        for hp in range(MLA_HEADS // 2):
            q2 = _dot(cq, lw["wuq"][:, hp * 2 * HEAD_PAD:(hp + 1) * 2 * HEAD_PAD]) * Q_SCALE
            for j in range(2):
                hd = 2 * hp + j
                c0 = j * HEAD_PAD
                q_s[hd, rows, :NOPE_DIM] = q2[:, c0:c0 + NOPE_DIM].astype(_BF)
                qr = q2[:, c0 + NOPE_DIM:c0 + HEAD_PAD]
                if has_rope:
                    qr = rotate(qr, r)
                q_s[hd, rows, NOPE_DIM:] = qr.astype(_BF)

    def mix_attention(lw, c):
        r = c * rc
        rows = pl.ds(r, rc)
        win_ref = lw["win"]
        h = h_s[rows, :]
        for hp in range(MLA_HEADS // 2):
            g2 = _silu(_dot(h, win_ref[:, _C_GMLA + hp * 2 * V_DIM:_C_GMLA + (hp + 1) * 2 * V_DIM]))
            for j in range(2):
                hd = 2 * hp + j
                for s, ps in pieces:
                    rows_q = pl.ds(r + s * piece, piece)
                    if n_seq == 1:
                        ks = [kt_s[hd, kc] for kc in range(keys // rc)]
                        rows_k = pl.ds(0, keys)
                    else:
                        ks = [kt_s[hd, c, :, ps]]
                        rows_k = rows_q
                    q = q_s[hd, rows_q, :]
                    sc = jnp.concatenate([_dot(q, k) for k in ks], axis=-1)
                    p = jnp.exp2(sc - jnp.max(sc, axis=-1, keepdims=True))
                    ov = _dot(p.astype(_BF), v_s[hd, rows_k, :])
                    o = ov[:, :V_DIM] / ov[:, V_DIM:V_DIM + 1]
                    mixed_s[rows_q, hd * V_DIM:(hd + 1) * V_DIM] = (g2[ps, j * V_DIM:(j + 1) * V_DIM] * o).astype(_BF)

    def mix_rest(lw, c):
        r = c * rc
        rows = pl.ds(r, rc)
        row, win_ref = lw["row"], lw["win"]
        h = h_s[rows, :]
        windows = [pl.ds(halo_base(c, s), piece + 2 * HALO) for s, _ in pieces]
        rcnt = rcnt_ref[pl.ds(0 if seq_len <= rc else r, piece), :]
        pooled = jnp.concatenate([_pool_mix(px_s[w, :], rcnt) for w in windows], axis=0)
        pool = _dot(pooled.astype(_BF), lw["poolw"][...]) * pools_ref[row, :]
        gp = _silu(_dot(h, win_ref[:, _C_GPOOL:_C_PX]))
        mixed_s[rows, MLA_WIDTH:MLA_WIDTH + POOL_WIDTH] = (gp * pool).astype(_BF)

        convw = [convw_ref[k, row, :] for k in range(3)]
        conv = jnp.concatenate([_short_conv(prod_s[w, :], convw) for w in windows], axis=0)
        e2 = _dot(h, win_ref[:, _C_CB:_C_END])
        mixed_s[rows, MLA_WIDTH + POOL_WIDTH:] = (_silu(e2[:, CONV_WIDTH:]) * (e2[:, :CONV_WIDTH] * conv)).astype(_BF)

        y_ref[rows, :] = y_ref[rows, :] + lw["gate"] * _dot(mixed_s[rows, :], lw["wout"][...])

    def one_layer(l, carry):
        @pl.when(first_step)
        def _():
            for k in range(len(_WEIGHTS)):
                weight_copy(l, k).wait()

        lw = layer_view(l)
        firsts = [project(lw, c) for c in chunks]
        if cache_len:
            cached_keys(lw)
        for c in chunks:
            project_heads(lw, c, firsts[c])
        for c in chunks:
            mix_attention(lw, c)
        for c in chunks:
            mix_rest(lw, c)
        return carry

    lax.fori_loop(0, DEPTH, one_layer, 0)
    for c in chunks:
        rows = slice(c * rc, (c + 1) * rc)
        y_ref[rows, :] = _rms(y_ref[rows, :], gfin_ref[...])


def _mixer_pass(x2d, mod_all, wts, *, seq_len, cache=None, rope=None, emit_state):
    rows = x2d.shape[0]
    m = ROWS_PER_STEP
    n_seq = m // seq_len
    n_steps = rows // m
    n_all = rows // seq_len
    cache_len = 0 if cache is None else cache[0].shape[2]
    keys = cache_len + seq_len
    has_rope = rope is not None
    rc = ROW_CHUNK
    assert m % rc == 0 and cache_len % rc == 0
    assert (rc % seq_len == 0 and not cache_len) or (n_seq == 1 and seq_len % rc == 0 and not emit_state)

    def const(*shape):
        return pl.BlockSpec(shape, lambda i: (0,) * len(shape), pipeline_mode=pl.Buffered(1))

    args = [x2d, mod_all]
    in_specs = [pl.BlockSpec((m, D_MODEL), lambda i: (i, 0)), const(DEPTH, 3, MOD_ROWS, D_MODEL)]
    if cache_len:
        args += [cache[0], cache[1]]
        in_specs += [pl.BlockSpec((None, DEPTH, cache_len, KV_LORA), lambda i: (i, 0, 0, 0)),
                     pl.BlockSpec((None, DEPTH, ROPE_DIM, cache_len), lambda i: (i, 0, 0, 0))]
    if has_rope:
        args += [rope]
        in_specs += [const(seq_len, ROPE_PAD)]
    args += [_pool_rcnt(seq_len), wts["g_norm"], wts["g_q"], wts["g_kv"], wts["pool_s"], wts["conv_w"],
             wts["g_final"]]
    in_specs += [const(seq_len, POOL_WIDTH), const(DEPTH, D_MODEL), const(DEPTH, Q_LORA), const(DEPTH, KV_LORA),
                 const(DEPTH, POOL_WIDTH), const(3, DEPTH, CONV_WIDTH), const(1, D_MODEL)]
    weights = [wts[name] for name in _WEIGHTS]
    args += weights
    in_specs += [pl.BlockSpec(memory_space=pl.ANY)] * len(weights)

    out_shape = [jax.ShapeDtypeStruct((rows, D_MODEL), _F32)]
    out_specs = [pl.BlockSpec((m, D_MODEL), lambda i: (i, 0))]
    if emit_state:
        out_shape += [jax.ShapeDtypeStruct((n_all, DEPTH, seq_len, KV_LORA), _F32),
                      jax.ShapeDtypeStruct((n_all, DEPTH, ROPE_DIM, seq_len), _F32)]
        out_specs += [pl.BlockSpec((n_seq, DEPTH, seq_len, KV_LORA), lambda i: (i, 0, 0, 0)),
                      pl.BlockSpec((n_seq, DEPTH, ROPE_DIM, seq_len), lambda i: (i, 0, 0, 0))]

    kern = functools.partial(_pass_kernel, n_seq=n_seq, seq_len=seq_len, cache_len=cache_len,
                             has_rope=has_rope, emit_state=emit_state, rc=rc)
    return pl.pallas_call(
        kern,
        grid=(n_steps,),
        in_specs=in_specs,
        out_specs=out_specs,
        out_shape=out_shape,
        scratch_shapes=[
            pltpu.VMEM((m, D_MODEL), _BF),
            pltpu.VMEM((MLA_HEADS, m, HEAD_PAD), _BF),
            pltpu.VMEM((MLA_HEADS, n_seq * keys // rc, HEAD_PAD, rc), _BF),
            pltpu.VMEM((MLA_HEADS, n_seq * keys, 2 * V_DIM), _BF),
            pltpu.VMEM((n_seq * (seq_len + 2 * HALO), POOL_WIDTH), _F32),
            pltpu.VMEM((n_seq * (seq_len + 2 * HALO), CONV_WIDTH), _F32),
            pltpu.VMEM((m, MIX_WIDTH), _BF),
        ] + [pltpu.VMEM(w.shape, w.dtype) for w in weights] + [pltpu.SemaphoreType.DMA((DEPTH, len(weights)))],
        compiler_params=pltpu.CompilerParams(dimension_semantics=("arbitrary",),
                                             vmem_limit_bytes=VMEM_LIMIT_BYTES),
        name="mixer_pass_latent" if has_rope else "mixer_pass_context",
    )(*args)


def _pool_rcnt(seq_len):
    t = np.arange(seq_len)[:, None]
    half = np.repeat(1 << np.arange(POOL_GROUPS), POOL_GROUP_DIM)[None, :]
    cnt = np.minimum(t + half, seq_len) - np.maximum(t - half, 0)
    return jnp.asarray((1.0 / cnt).astype(np.float32))


def _rope_table(seq_len):
    rows = seq_len // GRID_W
    row = np.repeat(np.arange(rows), GRID_W).astype(np.float32)
    col = np.tile(np.arange(GRID_W), rows).astype(np.float32)
    inv = (1.0 / (np.float32(ROPE_BASE) ** (np.arange(0, AXIS_DIM, 2, dtype=np.float32) / np.float32(AXIS_DIM))))
    inv = inv.astype(np.float32)
    ang = np.concatenate([row[:, None] * inv, col[:, None] * inv], axis=-1).astype(np.float64)
    cos, sin = np.cos(ang).astype(np.float32), np.sin(ang).astype(np.float32)
    return jnp.asarray(np.concatenate([cos, cos, sin, sin], axis=-1))


def kernel(x_prompt, x_sample, cache_ckv, cache_krope, c, c_ctx, w_mod, b_mod, g_norm, w_in, g_q, w_uq,
           g_kv, w_ukv, pool_w, pool_s, conv_w, w_out, g_final):
    batch, seq, _ = x_prompt.shape
    dec_batch, dec_seq, _ = x_sample.shape
    assert 1 + dec_batch <= MOD_ROWS and dec_seq == ROWS_PER_STEP and ROWS_PER_STEP % seq == 0

    c_all = jnp.concatenate([c_ctx[None], c, jnp.zeros((MOD_ROWS - 1 - dec_batch, D_MODEL), _F32)], axis=0)
    mod_all = _modulation(c_all, w_mod, b_mod)
    w_in_r, w_out_r, w_uq_r, w_ukt, w_uv, pool_w_r = _prep_weights(
        jnp.transpose(w_in, (0, 2, 1)), w_out, w_uq, w_ukv, pool_w)
    wts = {
        "g_norm": g_norm, "w_in": w_in_r, "g_q": g_q, "w_uq": w_uq_r, "g_kv": g_kv, "w_ukt": w_ukt,
        "w_uv": w_uv, "pool_w": pool_w_r, "pool_s": pool_s, "conv_w": jnp.transpose(conv_w, (1, 0, 2)),
        "w_out": w_out_r, "g_final": g_final.reshape(1, D_MODEL),
    }

    rope = _rope_table(dec_seq)
    cache = (cache_ckv, jnp.transpose(cache_krope, (0, 1, 3, 2)))
    xp = x_prompt.reshape(batch * seq, D_MODEL)
    xs = x_sample.reshape(dec_batch * dec_seq, D_MODEL)
    xp, state_ckv, state_krope_t = _mixer_pass(xp, mod_all, wts, seq_len=seq, emit_state=True)
    (xs,) = _mixer_pass(xs, mod_all, wts, seq_len=dec_seq, cache=cache, rope=rope, emit_state=False)
    y_prompt = xp.reshape(batch, seq, D_MODEL)
    state_krope = jnp.transpose(state_krope_t, (0, 1, 3, 2))
    y_sample = xs.reshape(dec_batch, dec_seq, D_MODEL)
    return (y_prompt, y_sample, state_ckv, state_krope)
```

```python
import functools

import numpy as np
import jax
import jax.numpy as jnp
from jax import lax
from jax.experimental import pallas as pl
from jax.experimental.pallas import tpu as pltpu

D_MODEL = 1024
DEPTH = 2
GRID_W = 64
MLA_HEADS = 4
NOPE_DIM = 128
ROPE_DIM = 64
V_DIM = 128
QK_DIM = NOPE_DIM + ROPE_DIM
MLA_WIDTH = MLA_HEADS * V_DIM
Q_LORA = 384
KV_LORA = 256
POOL_GROUPS = 4
POOL_GROUP_DIM = 64
POOL_WIDTH = POOL_GROUPS * POOL_GROUP_DIM
CONV_WIDTH = 256
MIX_WIDTH = MLA_WIDTH + POOL_WIDTH + CONV_WIDTH
ROPE_BASE = 10000.0
AXIS_DIM = ROPE_DIM // 2
ATTN_SCALE = QK_DIM ** -0.5
Q_SCALE = ATTN_SCALE * float(np.log2(np.e))
EPS = 1e-6

_SPLITS = (Q_LORA, KV_LORA, ROPE_DIM, MLA_WIDTH, POOL_WIDTH, POOL_WIDTH,
           CONV_WIDTH, CONV_WIDTH, CONV_WIDTH, CONV_WIDTH)
_OFFS = [sum(_SPLITS[:i]) for i in range(len(_SPLITS) + 1)]
IN_WIDTH = _OFFS[-1]

V7X_SUBLANES = 8
V7X_MXU_DEPTH = 256
V7X_VMEM_BYTES = 64 * 1024 * 1024
HEAD_PAD = V7X_MXU_DEPTH
ROPE_PAD = HEAD_PAD - NOPE_DIM
KV_HEAD = NOPE_DIM + V_DIM

_C_CQ = 0
_C_CKV = _C_CQ + Q_LORA
_C_KR = _C_CKV + KV_LORA
_C_GMLA = _C_KR + ROPE_PAD
_C_GPOOL = _C_GMLA + MLA_WIDTH
_C_PX = _C_GPOOL + POOL_WIDTH
_C_CC = _C_PX + POOL_WIDTH
_C_CB = _C_CC + 2 * CONV_WIDTH
_C_END = _C_CB + 2 * CONV_WIDTH

MOD_ROWS = 8
ROWS_PER_STEP = 1024
ROW_CHUNK = 512
HALO = V7X_SUBLANES
PREP_COLS = 256
VMEM_LIMIT_BYTES = V7X_VMEM_BYTES - 4 * 1024 * 1024

_WEIGHTS = ("w_in", "w_uq", "w_ukt", "w_uv", "pool_w", "w_out")

_BF = jnp.bfloat16
_F32 = jnp.float32


def _dot(a, b):
    return jnp.dot(a, b, preferred_element_type=_F32)


def _rms(x, g):
    return x * lax.rsqrt(jnp.mean(x * x, axis=-1, keepdims=True) + EPS) * g


def _silu(x):
    return x * jax.nn.sigmoid(x)


def _mod_kernel(c_ref, w_ref, b_ref, o_ref):
    s = _silu(c_ref[...]).astype(_BF)
    for k in range(3):
        cols = slice(k * D_MODEL, (k + 1) * D_MODEL)
        bias = b_ref[pl.ds(pl.program_id(0), 1), cols]
        o_ref[k] = _dot(s, w_ref[:, cols].astype(_BF)) + bias


def _modulation(c_all, w_mod, b_mod):
    return pl.pallas_call(
        _mod_kernel,
        grid=(DEPTH,),
        in_specs=[
            pl.BlockSpec((MOD_ROWS, D_MODEL), lambda l: (0, 0)),
            pl.BlockSpec((None, D_MODEL, 3 * D_MODEL), lambda l: (l, 0, 0)),
            pl.BlockSpec((DEPTH, 3 * D_MODEL), lambda l: (0, 0)),
        ],
        out_specs=pl.BlockSpec((None, 3, MOD_ROWS, D_MODEL), lambda l: (l, 0, 0, 0)),
        out_shape=jax.ShapeDtypeStruct((DEPTH, 3, MOD_ROWS, D_MODEL), _F32),
        compiler_params=pltpu.CompilerParams(dimension_semantics=("arbitrary",),
                                             vmem_limit_bytes=VMEM_LIMIT_BYTES),
        name="modulation",
    )(c_all, w_mod, b_mod)


def _prep_kernel(wint_ref, wout_ref, wuq_ref, wukv_ref, poolw_ref,
                 win_o, wout_o, wuq_o, wukt_o, wuv_o, poolw_o):
    def put(dst, src, width):
        for c in range(0, width, PREP_COLS):
            n = min(PREP_COLS, width - c)
            win_o[:, dst + c:dst + c + n] = wint_ref[src + c:src + c + n, :].T.astype(_BF)

    put(_C_CQ, _OFFS[0], Q_LORA + KV_LORA)
    kr = wint_ref[_OFFS[2]:_OFFS[3], :]
    krb = jnp.concatenate([kr, -kr[AXIS_DIM:], kr[:AXIS_DIM]], axis=0)
    win_o[:, _C_KR:_C_GMLA] = krb.T.astype(_BF)
    put(_C_GMLA, _OFFS[3], MLA_WIDTH)
    put(_C_GPOOL, _OFFS[5], POOL_WIDTH)
    put(_C_PX, _OFFS[4], POOL_WIDTH)
    put(_C_CC, _OFFS[7], 2 * CONV_WIDTH)
    put(_C_CB, _OFFS[6], CONV_WIDTH)
    put(_C_CB + CONV_WIDTH, _OFFS[9], CONV_WIDTH)
    wout_o[...] = wout_ref[...].astype(_BF)

    for hd in range(MLA_HEADS):
        s0, d0 = hd * QK_DIM, hd * HEAD_PAD
        r0 = s0 + NOPE_DIM
        wuq_o[:, d0:d0 + QK_DIM] = wuq_ref[:, s0:s0 + QK_DIM].astype(_BF)
        wuq_o[:, d0 + QK_DIM:d0 + QK_DIM + AXIS_DIM] = (-wuq_ref[:, r0 + AXIS_DIM:r0 + ROPE_DIM]).astype(_BF)
        wuq_o[:, d0 + QK_DIM + AXIS_DIM:d0 + HEAD_PAD] = wuq_ref[:, r0:r0 + AXIS_DIM].astype(_BF)
    for hd in range(MLA_HEADS):
        k0 = hd * KV_HEAD
        wukt_o[hd * NOPE_DIM:(hd + 1) * NOPE_DIM, :] = wukv_ref[:, k0:k0 + NOPE_DIM].T.astype(_BF)
        wuv_o[:, hd * V_DIM:(hd + 1) * V_DIM] = wukv_ref[:, k0 + NOPE_DIM:k0 + KV_HEAD].astype(_BF)
    poolw_o[...] = jnp.zeros((POOL_WIDTH, POOL_WIDTH), _BF)
    for g in range(POOL_GROUPS):
        sl = slice(g * POOL_GROUP_DIM, (g + 1) * POOL_GROUP_DIM)
        poolw_o[sl, sl] = poolw_ref[g].astype(_BF)


def _prep_weights(w_in_t, w_out, w_uq, w_ukv, pool_w):
    per_layer = lambda *shape: pl.BlockSpec((None,) + shape, lambda l: (l,) + (0,) * len(shape))
    return pl.pallas_call(
        _prep_kernel,
        grid=(DEPTH,),
        in_specs=[
            per_layer(IN_WIDTH, D_MODEL),
            per_layer(MIX_WIDTH, D_MODEL),
            per_layer(Q_LORA, MLA_HEADS * QK_DIM),
            per_layer(KV_LORA, MLA_HEADS * KV_HEAD),
            per_layer(POOL_GROUPS, POOL_GROUP_DIM, POOL_GROUP_DIM),
        ],
        out_specs=[
            per_layer(D_MODEL, _C_END),
            per_layer(MIX_WIDTH, D_MODEL),
            per_layer(Q_LORA, MLA_HEADS * HEAD_PAD),
            per_layer(MLA_HEADS * NOPE_DIM, KV_LORA),
            per_layer(KV_LORA, MLA_WIDTH),
            per_layer(POOL_WIDTH, POOL_WIDTH),
        ],
        out_shape=[
            jax.ShapeDtypeStruct((DEPTH, D_MODEL, _C_END), _BF),
            jax.ShapeDtypeStruct((DEPTH, MIX_WIDTH, D_MODEL), _BF),
            jax.ShapeDtypeStruct((DEPTH, Q_LORA, MLA_HEADS * HEAD_PAD), _BF),
            jax.ShapeDtypeStruct((DEPTH, MLA_HEADS * NOPE_DIM, KV_LORA), _BF),
            jax.ShapeDtypeStruct((DEPTH, KV_LORA, MLA_WIDTH), _BF),
            jax.ShapeDtypeStruct((DEPTH, POOL_WIDTH, POOL_WIDTH), _BF),
        ],
        compiler_params=pltpu.CompilerParams(dimension_semantics=("arbitrary",),
                                             vmem_limit_bytes=VMEM_LIMIT_BYTES),
        name="weight_prep",
    )(w_in_t, w_out, w_uq, w_ukv, pool_w)


def _pool_mix(win, rcnt):
    n = win.shape[0]
    rows = n - 2 * HALO
    lane = lax.broadcasted_iota(jnp.int32, (1, POOL_WIDTH), 1)
    sums = [win + pltpu.roll(win, 1, axis=0)]
    for k in (1, 2, 4):
        sums.append(pltpu.roll(sums[-1], k, axis=0) + pltpu.roll(sums[-1], n - k, axis=0))
    sel = sums[-1]
    for g in range(POOL_GROUPS - 2, -1, -1):
        sel = jnp.where(lane < (g + 1) * POOL_GROUP_DIM, sums[g], sel)
    return sel[HALO:HALO + rows] * rcnt - win[HALO:HALO + rows]


def _short_conv(win, w):
    rows = win.shape[0] - 2 * HALO
    return (win[HALO - 1:HALO - 1 + rows] * w[0] + win[HALO:HALO + rows] * w[1]
            + win[HALO + 1:HALO + 1 + rows] * w[2])


def _pass_kernel(*refs, n_seq, seq_len, cache_len, has_rope, emit_state, rc):
    it = iter(refs)
    x_ref, mod_ref = next(it), next(it)
    if cache_len:
        cckv_all_ref, ckr_all_ref = next(it), next(it)
    if has_rope:
        cs_ref = next(it)
    rcnt_ref, gn_ref, gq_ref, gkv_ref, pools_ref, convw_ref, gfin_ref = (next(it) for _ in range(7))
    w_hbm = [next(it) for _ in range(len(_WEIGHTS))]
    y_ref = next(it)
    if emit_state:
        ckv_out_ref, kr_out_ref = next(it), next(it)
    h_s, q_s, kt_s, v_s, px_s, prod_s, mixed_s = (next(it) for _ in range(7))
    w_vm = [next(it) for _ in range(len(_WEIGHTS))]
    w_sem = next(it)
    win_all_ref, wuq_all_ref, wukt_all_ref, wuv_all_ref, poolw_all_ref, wout_all_ref = w_vm

    def weight_copy(l, k):
        return pltpu.make_async_copy(w_hbm[k].at[l], w_vm[k].at[l], w_sem.at[l, k])

    first_step = pl.program_id(0) == 0

    @pl.when(first_step)
    def _():
        for l in range(DEPTH):
            for k in range(len(_WEIGHTS)):
                weight_copy(l, k).start()

    m = n_seq * seq_len
    keys = cache_len + seq_len
    padded = seq_len + 2 * HALO
    lane = lax.broadcasted_iota(jnp.int32, (1, ROPE_PAD), 1)
    mrow = (1 + pl.program_id(0)) if has_rope else 0
    piece = min(seq_len, rc)
    n_sub = rc // piece
    pieces = [(s, slice(s * piece, (s + 1) * piece)) for s in range(n_sub)]
    chunks = range(m // rc)

    for c in chunks:
        y_ref[c * rc:(c + 1) * rc, :] = x_ref[c * rc:(c + 1) * rc, :]

    zeros_halo = jnp.zeros((HALO, POOL_WIDTH), _F32)
    for buf in (px_s, prod_s):
        for s in range(n_seq):
            buf[s * padded:s * padded + HALO, :] = zeros_halo
            buf[(s + 1) * padded - HALO:(s + 1) * padded, :] = zeros_halo
    ones_col = jnp.where(lax.broadcasted_iota(jnp.int32, (n_seq * keys, V_DIM), 1) == 0, 1.0, 0.0).astype(_BF)
    for hd in range(MLA_HEADS):
        v_s[hd, :, V_DIM:] = ones_col

    def layer_view(l):
        lw = dict(l=l, row=pl.ds(l, 1), win=win_all_ref.at[l], wuq=wuq_all_ref.at[l], wukt=wukt_all_ref.at[l],
                  wuv=wuv_all_ref.at[l], poolw=poolw_all_ref.at[l], wout=wout_all_ref.at[l])
        lw["shift"], lw["scale"], lw["gate"] = (mod_ref[l, k, pl.ds(mrow, 1), :] for k in range(3))
        if cache_len:
            lw["cckv"], lw["ckr"] = cckv_all_ref.at[l], ckr_all_ref.at[l]
        return lw

    def rotate(v, r):
        v = v * cs_ref[pl.ds(r, rc), :]
        return v + pltpu.roll(v, ROPE_DIM, axis=1)

    def put_kv(lw, ckv, kr_t, kc):
        knt = _dot(lw["wukt"][...], ckv.T.astype(_BF))
        v4 = _dot(ckv.astype(_BF), lw["wuv"][...])
        kr_bf = kr_t.astype(_BF)
        rows_k = pl.ds(kc * rc, rc)
        for hd in range(MLA_HEADS):
            kt_s[hd, kc, :NOPE_DIM, :] = knt[hd * NOPE_DIM:(hd + 1) * NOPE_DIM, :].astype(_BF)
            kt_s[hd, kc, NOPE_DIM:, :] = kr_bf
            v_s[hd, rows_k, :V_DIM] = v4[:, hd * V_DIM:(hd + 1) * V_DIM].astype(_BF)

    def cached_keys(lw):
        ckr_t = lw["ckr"][...]
        ckr_pad = jnp.concatenate([ckr_t, jnp.zeros_like(ckr_t)], axis=0)
        for cc in range(cache_len // rc):
            rows_k = slice(cc * rc, (cc + 1) * rc)
            put_kv(lw, lw["cckv"][rows_k, :], ckr_pad[:, rows_k], cc)

    def halo_base(c, s):
        return (c * n_sub + s) * padded if seq_len <= rc else c * rc

    def project(lw, c):
        r = c * rc
        rows = pl.ds(r, rc)
        row, win_ref = lw["row"], lw["win"]
        h = (_rms(y_ref[rows, :], gn_ref[row, :]) * (1.0 + lw["scale"]) + lw["shift"]).astype(_BF)
        h_s[rows, :] = h
        a = _dot(h, win_ref[:, _C_CQ:_C_GMLA])
        pe = _dot(h, win_ref[:, _C_PX:_C_CB])
        for s, ps in pieces:
            rows_h = pl.ds(halo_base(c, s) + HALO, piece)
            px_s[rows_h, :] = pe[ps, :POOL_WIDTH]
            prod_s[rows_h, :] = pe[ps, POOL_WIDTH:POOL_WIDTH + CONV_WIDTH] * pe[ps, POOL_WIDTH + CONV_WIDTH:]
        return a

    def project_heads(lw, c, a):
        r = c * rc
        rows = pl.ds(r, rc)
        row = lw["row"]
        cq = _rms(a[:, :Q_LORA], gq_ref[row, :]).astype(_BF)
        ckv = _rms(a[:, _C_CKV:_C_KR], gkv_ref[row, :])
        kr = a[:, _C_KR:]
        if has_rope:
            kr = rotate(kr, r)
        kr_t = jnp.where(lane < ROPE_DIM, kr, 0.0).T
        if emit_state:
            for s, ps in pieces:
                ckv_out_ref[c * n_sub + s, lw["l"]] = ckv[ps, :]
                kr_out_ref[c * n_sub + s, lw["l"]] = kr_t[:ROPE_DIM, ps]
        put_kv(lw, ckv, kr_t, cache_len // rc + c)
        for hp in range(MLA_HEADS // 2):
            q2 = _dot(cq, lw["wuq"][:, hp * 2 * HEAD_PAD:(hp + 1) * 2 * HEAD_PAD]) * Q_SCALE
            for j in range(2):
                hd = 2 * hp + j
                c0 = j * HEAD_PAD
                q_s[hd, rows, :NOPE_DIM] = q2[:, c0:c0 + NOPE_DIM].astype(_BF)
                qr = q2[:, c0 + NOPE_DIM:c0 + HEAD_PAD]
                if has_rope:
                    qr = rotate(qr, r)
                q_s[hd, rows, NOPE_DIM:] = qr.astype(_BF)

    def mix_attention(lw, c):
        r = c * rc
        rows = pl.ds(r, rc)
        win_ref = lw["win"]
        h = h_s[rows, :]
        for hp in range(MLA_HEADS // 2):
            g2 = _silu(_dot(h, win_ref[:, _C_GMLA + hp * 2 * V_DIM:_C_GMLA + (hp + 1) * 2 * V_DIM]))
            for j in range(2):
                hd = 2 * hp + j
                for s, ps in pieces:
                    rows_q = pl.ds(r + s * piece, piece)
                    if n_seq == 1:
                        ks = [kt_s[hd, kc] for kc in range(keys // rc)]
                        rows_k = pl.ds(0, keys)
                    else:
                        ks = [kt_s[hd, c, :, ps]]
                        rows_k = rows_q
                    q = q_s[hd, rows_q, :]
                    sc = jnp.concatenate([_dot(q, k) for k in ks], axis=-1)
                    p = jnp.exp2(sc - jnp.max(sc, axis=-1, keepdims=True))
                    ov = _dot(p.astype(_BF), v_s[hd, rows_k, :])
                    o = ov[:, :V_DIM] / ov[:, V_DIM:V_DIM + 1]
                    mixed_s[rows_q, hd * V_DIM:(hd + 1) * V_DIM] = (g2[ps, j * V_DIM:(j + 1) * V_DIM] * o).astype(_BF)

    def mix_rest(lw, c):
        r = c * rc
        rows = pl.ds(r, rc)
        row, win_ref = lw["row"], lw["win"]
        h = h_s[rows, :]
        windows = [pl.ds(halo_base(c, s), piece + 2 * HALO) for s, _ in pieces]
        rcnt = rcnt_ref[pl.ds(0 if seq_len <= rc else r, piece), :]
        pooled = jnp.concatenate([_pool_mix(px_s[w, :], rcnt) for w in windows], axis=0)
        pool = _dot(pooled.astype(_BF), lw["poolw"][...]) * pools_ref[row, :]
        gp = _silu(_dot(h, win_ref[:, _C_GPOOL:_C_PX]))
        mixed_s[rows, MLA_WIDTH:MLA_WIDTH + POOL_WIDTH] = (gp * pool).astype(_BF)

        convw = [convw_ref[k, row, :] for k in range(3)]
        conv = jnp.concatenate([_short_conv(prod_s[w, :], convw) for w in windows], axis=0)
        e2 = _dot(h, win_ref[:, _C_CB:_C_END])
        mixed_s[rows, MLA_WIDTH + POOL_WIDTH:] = (_silu(e2[:, CONV_WIDTH:]) * (e2[:, :CONV_WIDTH] * conv)).astype(_BF)

    def mix_out(lw, c):
        rows = pl.ds(c * rc, rc)
        y_ref[rows, :] = y_ref[rows, :] + lw["gate"] * _dot(mixed_s[rows, :], lw["wout"][...])

    def one_layer(l, carry):
        @pl.when(first_step)
        def _():
            for k in range(len(_WEIGHTS)):
                weight_copy(l, k).wait()

        lw = layer_view(l)
        firsts = [project(lw, c) for c in chunks]
        if cache_len:
            cached_keys(lw)
        for c in chunks:
            project_heads(lw, c, firsts[c])
        for c in chunks:
            mix_rest(lw, c)
            mix_attention(lw, c)
            mix_out(lw, c)
        return carry

    lax.fori_loop(0, DEPTH, one_layer, 0)
    for c in chunks:
        rows = slice(c * rc, (c + 1) * rc)
        y_ref[rows, :] = _rms(y_ref[rows, :], gfin_ref[...])


def _mixer_pass(x2d, mod_all, wts, *, seq_len, cache=None, rope=None, emit_state):
    rows = x2d.shape[0]
    m = ROWS_PER_STEP
    n_seq = m // seq_len
    n_steps = rows // m
    n_all = rows // seq_len
    cache_len = 0 if cache is None else cache[0].shape[2]
    keys = cache_len + seq_len
    has_rope = rope is not None
    rc = ROW_CHUNK
    assert m % rc == 0 and cache_len % rc == 0
    assert (rc % seq_len == 0 and not cache_len) or (n_seq == 1 and seq_len % rc == 0 and not emit_state)

    def const(*shape):
        return pl.BlockSpec(shape, lambda i: (0,) * len(shape), pipeline_mode=pl.Buffered(1))

    args = [x2d, mod_all]
    in_specs = [pl.BlockSpec((m, D_MODEL), lambda i: (i, 0)), const(DEPTH, 3, MOD_ROWS, D_MODEL)]
    if cache_len:
        args += [cache[0], cache[1]]
        in_specs += [pl.BlockSpec((None, DEPTH, cache_len, KV_LORA), lambda i: (i, 0, 0, 0)),
                     pl.BlockSpec((None, DEPTH, ROPE_DIM, cache_len), lambda i: (i, 0, 0, 0))]
    if has_rope:
        args += [rope]
        in_specs += [const(seq_len, ROPE_PAD)]
    args += [_pool_rcnt(seq_len), wts["g_norm"], wts["g_q"], wts["g_kv"], wts["pool_s"], wts["conv_w"],
             wts["g_final"]]
    in_specs += [const(seq_len, POOL_WIDTH), const(DEPTH, D_MODEL), const(DEPTH, Q_LORA), const(DEPTH, KV_LORA),
                 const(DEPTH, POOL_WIDTH), const(3, DEPTH, CONV_WIDTH), const(1, D_MODEL)]
    weights = [wts[name] for name in _WEIGHTS]
    args += weights
    in_specs += [pl.BlockSpec(memory_space=pl.ANY)] * len(weights)

    out_shape = [jax.ShapeDtypeStruct((rows, D_MODEL), _F32)]
    out_specs = [pl.BlockSpec((m, D_MODEL), lambda i: (i, 0))]
    if emit_state:
        out_shape += [jax.ShapeDtypeStruct((n_all, DEPTH, seq_len, KV_LORA), _F32),
                      jax.ShapeDtypeStruct((n_all, DEPTH, ROPE_DIM, seq_len), _F32)]
        out_specs += [pl.BlockSpec((n_seq, DEPTH, seq_len, KV_LORA), lambda i: (i, 0, 0, 0)),
                      pl.BlockSpec((n_seq, DEPTH, ROPE_DIM, seq_len), lambda i: (i, 0, 0, 0))]

    kern = functools.partial(_pass_kernel, n_seq=n_seq, seq_len=seq_len, cache_len=cache_len,
                             has_rope=has_rope, emit_state=emit_state, rc=rc)
    return pl.pallas_call(
        kern,
        grid=(n_steps,),
        in_specs=in_specs,
        out_specs=out_specs,
        out_shape=out_shape,
        scratch_shapes=[
            pltpu.VMEM((m, D_MODEL), _BF),
            pltpu.VMEM((MLA_HEADS, m, HEAD_PAD), _BF),
            pltpu.VMEM((MLA_HEADS, n_seq * keys // rc, HEAD_PAD, rc), _BF),
            pltpu.VMEM((MLA_HEADS, n_seq * keys, 2 * V_DIM), _BF),
            pltpu.VMEM((n_seq * (seq_len + 2 * HALO), POOL_WIDTH), _F32),
            pltpu.VMEM((n_seq * (seq_len + 2 * HALO), CONV_WIDTH), _F32),
            pltpu.VMEM((m, MIX_WIDTH), _BF),
        ] + [pltpu.VMEM(w.shape, w.dtype) for w in weights] + [pltpu.SemaphoreType.DMA((DEPTH, len(weights)))],
        compiler_params=pltpu.CompilerParams(dimension_semantics=("arbitrary",),
                                             vmem_limit_bytes=VMEM_LIMIT_BYTES),
        name="mixer_pass_latent" if has_rope else "mixer_pass_context",
    )(*args)


def _pool_rcnt(seq_len):
    t = np.arange(seq_len)[:, None]
    half = np.repeat(1 << np.arange(POOL_GROUPS), POOL_GROUP_DIM)[None, :]
    cnt = np.minimum(t + half, seq_len) - np.maximum(t - half, 0)
    return jnp.asarray((1.0 / cnt).astype(np.float32))


def _rope_table(seq_len):
    rows = seq_len // GRID_W
    row = np.repeat(np.arange(rows), GRID_W).astype(np.float32)
    col = np.tile(np.arange(GRID_W), rows).astype(np.float32)
    inv = (1.0 / (np.float32(ROPE_BASE) ** (np.arange(0, AXIS_DIM, 2, dtype=np.float32) / np.float32(AXIS_DIM))))
    inv = inv.astype(np.float32)
    ang = np.concatenate([row[:, None] * inv, col[:, None] * inv], axis=-1).astype(np.float64)
    cos, sin = np.cos(ang).astype(np.float32), np.sin(ang).astype(np.float32)
    return jnp.asarray(np.concatenate([cos, cos, sin, sin], axis=-1))


def kernel(x_prompt, x_sample, cache_ckv, cache_krope, c, c_ctx, w_mod, b_mod, g_norm, w_in, g_q, w_uq,
           g_kv, w_ukv, pool_w, pool_s, conv_w, w_out, g_final):
    batch, seq, _ = x_prompt.shape
    dec_batch, dec_seq, _ = x_sample.shape
    assert 1 + dec_batch <= MOD_ROWS and dec_seq == ROWS_PER_STEP and ROWS_PER_STEP % seq == 0

    c_all = jnp.concatenate([c_ctx[None], c, jnp.zeros((MOD_ROWS - 1 - dec_batch, D_MODEL), _F32)], axis=0)
    mod_all = _modulation(c_all, w_mod, b_mod)
    w_in_r, w_out_r, w_uq_r, w_ukt, w_uv, pool_w_r = _prep_weights(
        jnp.transpose(w_in, (0, 2, 1)), w_out, w_uq, w_ukv, pool_w)
    wts = {
        "g_norm": g_norm, "w_in": w_in_r, "g_q": g_q, "w_uq": w_uq_r, "g_kv": g_kv, "w_ukt": w_ukt,
        "w_uv": w_uv, "pool_w": pool_w_r, "pool_s": pool_s, "conv_w": jnp.transpose(conv_w, (1, 0, 2)),
        "w_out": w_out_r, "g_final": g_final.reshape(1, D_MODEL),
    }

    rope = _rope_table(dec_seq)
    cache = (cache_ckv, jnp.transpose(cache_krope, (0, 1, 3, 2)))
    xp = x_prompt.reshape(batch * seq, D_MODEL)
    xs = x_sample.reshape(dec_batch * dec_seq, D_MODEL)
    xp, state_ckv, state_krope_t = _mixer_pass(xp, mod_all, wts, seq_len=seq, emit_state=True)
    (xs,) = _mixer_pass(xs, mod_all, wts, seq_len=dec_seq, cache=cache, rope=rope, emit_state=False)
    y_prompt = xp.reshape(batch, seq, D_MODEL)
    state_krope = jnp.transpose(state_krope_t, (0, 1, 3, 2))
    y_sample = xs.reshape(dec_batch, dec_seq, D_MODEL)
    return (y_prompt, y_sample, state_ckv, state_krope)
```

```python
import functools

import numpy as np
import jax
import jax.numpy as jnp
from jax import lax
from jax.experimental import pallas as pl
from jax.experimental.pallas import tpu as pltpu

D_MODEL = 1024
DEPTH = 2
GRID_W = 64
MLA_HEADS = 4
NOPE_DIM = 128
ROPE_DIM = 64
V_DIM = 128
QK_DIM = NOPE_DIM + ROPE_DIM
MLA_WIDTH = MLA_HEADS * V_DIM
Q_LORA = 384
KV_LORA = 256
POOL_GROUPS = 4
POOL_GROUP_DIM = 64
POOL_WIDTH = POOL_GROUPS * POOL_GROUP_DIM
CONV_WIDTH = 256
MIX_WIDTH = MLA_WIDTH + POOL_WIDTH + CONV_WIDTH
ROPE_BASE = 10000.0
AXIS_DIM = ROPE_DIM // 2
ATTN_SCALE = QK_DIM ** -0.5
Q_SCALE = ATTN_SCALE * float(np.log2(np.e))
EPS = 1e-6

_SPLITS = (Q_LORA, KV_LORA, ROPE_DIM, MLA_WIDTH, POOL_WIDTH, POOL_WIDTH,
           CONV_WIDTH, CONV_WIDTH, CONV_WIDTH, CONV_WIDTH)
_OFFS = [sum(_SPLITS[:i]) for i in range(len(_SPLITS) + 1)]
IN_WIDTH = _OFFS[-1]

V7X_SUBLANES = 8
V7X_MXU_DEPTH = 256
V7X_VMEM_BYTES = 64 * 1024 * 1024
HEAD_PAD = V7X_MXU_DEPTH
ROPE_PAD = HEAD_PAD - NOPE_DIM
KV_HEAD = NOPE_DIM + V_DIM

_C_CQ = 0
_C_CKV = _C_CQ + Q_LORA
_C_KR = _C_CKV + KV_LORA
_C_GMLA = _C_KR + ROPE_PAD
_C_GPOOL = _C_GMLA + MLA_WIDTH
_C_PX = _C_GPOOL + POOL_WIDTH
_C_CC = _C_PX + POOL_WIDTH
_C_CB = _C_CC + 2 * CONV_WIDTH
_C_END = _C_CB + 2 * CONV_WIDTH

MOD_ROWS = 8
ROWS_PER_STEP = 1024
ROW_CHUNK = 512
HALO = V7X_SUBLANES
PREP_COLS = 256
VMEM_LIMIT_BYTES = V7X_VMEM_BYTES - 4 * 1024 * 1024

_WEIGHTS = ("w_in", "w_uq", "w_ukt", "w_uv", "pool_w", "w_out")

_BF = jnp.bfloat16
_F32 = jnp.float32


def _dot(a, b):
    return jnp.dot(a, b, preferred_element_type=_F32)


def _rms(x, g):
    return x * lax.rsqrt(jnp.mean(x * x, axis=-1, keepdims=True) + EPS) * g


def _silu(x):
    return x * jax.nn.sigmoid(x)


def _mod_kernel(c_ref, w_ref, b_ref, o_ref):
    s = _silu(c_ref[...]).astype(_BF)
    for k in range(3):
        cols = slice(k * D_MODEL, (k + 1) * D_MODEL)
        bias = b_ref[pl.ds(pl.program_id(0), 1), cols]
        o_ref[k] = _dot(s, w_ref[:, cols].astype(_BF)) + bias


def _modulation(c_all, w_mod, b_mod):
    return pl.pallas_call(
        _mod_kernel,
        grid=(DEPTH,),
        in_specs=[
            pl.BlockSpec((MOD_ROWS, D_MODEL), lambda l: (0, 0)),
            pl.BlockSpec((None, D_MODEL, 3 * D_MODEL), lambda l: (l, 0, 0)),
            pl.BlockSpec((DEPTH, 3 * D_MODEL), lambda l: (0, 0)),
        ],
        out_specs=pl.BlockSpec((None, 3, MOD_ROWS, D_MODEL), lambda l: (l, 0, 0, 0)),
        out_shape=jax.ShapeDtypeStruct((DEPTH, 3, MOD_ROWS, D_MODEL), _F32),
        compiler_params=pltpu.CompilerParams(dimension_semantics=("arbitrary",),
                                             vmem_limit_bytes=VMEM_LIMIT_BYTES),
        name="modulation",
    )(c_all, w_mod, b_mod)


def _prep_kernel(wint_ref, wout_ref, wuq_ref, wukv_ref, poolw_ref,
                 win_o, wout_o, wuq_o, wukt_o, wuv_o, poolw_o):
    def put(dst, src, width):
        for c in range(0, width, PREP_COLS):
            n = min(PREP_COLS, width - c)
            win_o[:, dst + c:dst + c + n] = wint_ref[src + c:src + c + n, :].T.astype(_BF)

    put(_C_CQ, _OFFS[0], Q_LORA + KV_LORA)
    kr = wint_ref[_OFFS[2]:_OFFS[3], :]
    krb = jnp.concatenate([kr, -kr[AXIS_DIM:], kr[:AXIS_DIM]], axis=0)
    win_o[:, _C_KR:_C_GMLA] = krb.T.astype(_BF)
    put(_C_GMLA, _OFFS[3], MLA_WIDTH)
    put(_C_GPOOL, _OFFS[5], POOL_WIDTH)
    put(_C_PX, _OFFS[4], POOL_WIDTH)
    put(_C_CC, _OFFS[7], 2 * CONV_WIDTH)
    put(_C_CB, _OFFS[6], CONV_WIDTH)
    put(_C_CB + CONV_WIDTH, _OFFS[9], CONV_WIDTH)
    wout_o[...] = wout_ref[...].astype(_BF)

    for hd in range(MLA_HEADS):
        s0, d0 = hd * QK_DIM, hd * HEAD_PAD
        r0 = s0 + NOPE_DIM
        wuq_o[:, d0:d0 + QK_DIM] = wuq_ref[:, s0:s0 + QK_DIM].astype(_BF)
        wuq_o[:, d0 + QK_DIM:d0 + QK_DIM + AXIS_DIM] = (-wuq_ref[:, r0 + AXIS_DIM:r0 + ROPE_DIM]).astype(_BF)
        wuq_o[:, d0 + QK_DIM + AXIS_DIM:d0 + HEAD_PAD] = wuq_ref[:, r0:r0 + AXIS_DIM].astype(_BF)
    for hd in range(MLA_HEADS):
        k0 = hd * KV_HEAD
        wukt_o[hd * NOPE_DIM:(hd + 1) * NOPE_DIM, :] = wukv_ref[:, k0:k0 + NOPE_DIM].T.astype(_BF)
        wuv_o[:, hd * V_DIM:(hd + 1) * V_DIM] = wukv_ref[:, k0 + NOPE_DIM:k0 + KV_HEAD].astype(_BF)
    poolw_o[...] = jnp.zeros((POOL_WIDTH, POOL_WIDTH), _BF)
    for g in range(POOL_GROUPS):
        sl = slice(g * POOL_GROUP_DIM, (g + 1) * POOL_GROUP_DIM)
        poolw_o[sl, sl] = poolw_ref[g].astype(_BF)


def _prep_weights(w_in_t, w_out, w_uq, w_ukv, pool_w):
    per_layer = lambda *shape: pl.BlockSpec((None,) + shape, lambda l: (l,) + (0,) * len(shape))
    return pl.pallas_call(
        _prep_kernel,
        grid=(DEPTH,),
        in_specs=[
            per_layer(IN_WIDTH, D_MODEL),
            per_layer(MIX_WIDTH, D_MODEL),
            per_layer(Q_LORA, MLA_HEADS * QK_DIM),
            per_layer(KV_LORA, MLA_HEADS * KV_HEAD),
            per_layer(POOL_GROUPS, POOL_GROUP_DIM, POOL_GROUP_DIM),
        ],
        out_specs=[
            per_layer(D_MODEL, _C_END),
            per_layer(MIX_WIDTH, D_MODEL),
            per_layer(Q_LORA, MLA_HEADS * HEAD_PAD),
            per_layer(MLA_HEADS * NOPE_DIM, KV_LORA),
            per_layer(KV_LORA, MLA_WIDTH),
            per_layer(POOL_WIDTH, POOL_WIDTH),
        ],
        out_shape=[
            jax.ShapeDtypeStruct((DEPTH, D_MODEL, _C_END), _BF),
            jax.ShapeDtypeStruct((DEPTH, MIX_WIDTH, D_MODEL), _BF),
            jax.ShapeDtypeStruct((DEPTH, Q_LORA, MLA_HEADS * HEAD_PAD), _BF),
            jax.ShapeDtypeStruct((DEPTH, MLA_HEADS * NOPE_DIM, KV_LORA), _BF),
            jax.ShapeDtypeStruct((DEPTH, KV_LORA, MLA_WIDTH), _BF),
            jax.ShapeDtypeStruct((DEPTH, POOL_WIDTH, POOL_WIDTH), _BF),
        ],
        compiler_params=pltpu.CompilerParams(dimension_semantics=("arbitrary",),
                                             vmem_limit_bytes=VMEM_LIMIT_BYTES),
        name="weight_prep",
    )(w_in_t, w_out, w_uq, w_ukv, pool_w)


def _pool_mix(win, rcnt):
    n = win.shape[0]
    rows = n - 2 * HALO
    lane = lax.broadcasted_iota(jnp.int32, (1, POOL_WIDTH), 1)
    sums = [win + pltpu.roll(win, 1, axis=0)]
    for k in (1, 2, 4):
        sums.append(pltpu.roll(sums[-1], k, axis=0) + pltpu.roll(sums[-1], n - k, axis=0))
    sel = sums[-1]
    for g in range(POOL_GROUPS - 2, -1, -1):
        sel = jnp.where(lane < (g + 1) * POOL_GROUP_DIM, sums[g], sel)
    return sel[HALO:HALO + rows] * rcnt - win[HALO:HALO + rows]


def _short_conv(win, w):
    rows = win.shape[0] - 2 * HALO
    return (win[HALO - 1:HALO - 1 + rows] * w[0] + win[HALO:HALO + rows] * w[1]
            + win[HALO + 1:HALO + 1 + rows] * w[2])


def _pass_kernel(*refs, n_seq, seq_len, cache_len, has_rope, emit_state, rc):
    it = iter(refs)
    x_ref, mod_ref = next(it), next(it)
    if cache_len:
        cckv_all_ref, ckr_all_ref = next(it), next(it)
    if has_rope:
        cs_ref = next(it)
    rcnt_ref, gn_ref, gq_ref, gkv_ref, pools_ref, convw_ref, gfin_ref = (next(it) for _ in range(7))
    w_hbm = [next(it) for _ in range(len(_WEIGHTS))]
    y_ref = next(it)
    if emit_state:
        ckv_out_ref, kr_out_ref = next(it), next(it)
    h_s, q_s, kt_s, v_s, px_s, prod_s, mixed_s = (next(it) for _ in range(7))
    w_vm = [next(it) for _ in range(len(_WEIGHTS))]
    w_sem = next(it)
    win_all_ref, wuq_all_ref, wukt_all_ref, wuv_all_ref, poolw_all_ref, wout_all_ref = w_vm

    def weight_copy(l, k):
        return pltpu.make_async_copy(w_hbm[k].at[l], w_vm[k].at[l], w_sem.at[l, k])

    first_step = pl.program_id(0) == 0

    @pl.when(first_step)
    def _():
        for l in range(DEPTH):
            for k in range(len(_WEIGHTS)):
                weight_copy(l, k).start()

    m = n_seq * seq_len
    keys = cache_len + seq_len
    padded = seq_len + 2 * HALO
    lane = lax.broadcasted_iota(jnp.int32, (1, ROPE_PAD), 1)
    mrow = (1 + pl.program_id(0)) if has_rope else 0
    piece = min(seq_len, rc)
    n_sub = rc // piece
    pieces = [(s, slice(s * piece, (s + 1) * piece)) for s in range(n_sub)]
    chunks = range(m // rc)

    for c in chunks:
        y_ref[c * rc:(c + 1) * rc, :] = x_ref[c * rc:(c + 1) * rc, :]

    zeros_halo = jnp.zeros((HALO, POOL_WIDTH), _F32)
    for buf in (px_s, prod_s):
        for s in range(n_seq):
            buf[s * padded:s * padded + HALO, :] = zeros_halo
            buf[(s + 1) * padded - HALO:(s + 1) * padded, :] = zeros_halo
    ones_col = jnp.where(lax.broadcasted_iota(jnp.int32, (n_seq * keys, V_DIM), 1) == 0, 1.0, 0.0).astype(_BF)
    for hd in range(MLA_HEADS):
        v_s[hd, :, V_DIM:] = ones_col

    def layer_view(l):
        lw = dict(l=l, row=pl.ds(l, 1), win=win_all_ref.at[l], wuq=wuq_all_ref.at[l], wukt=wukt_all_ref.at[l],
                  wuv=wuv_all_ref.at[l], poolw=poolw_all_ref.at[l], wout=wout_all_ref.at[l])
        lw["shift"], lw["scale"], lw["gate"] = (mod_ref[l, k, pl.ds(mrow, 1), :] for k in range(3))
        if cache_len:
            lw["cckv"], lw["ckr"] = cckv_all_ref.at[l], ckr_all_ref.at[l]
        return lw

    def rotate(v, r):
        v = v * cs_ref[pl.ds(r, rc), :]
        return v + pltpu.roll(v, ROPE_DIM, axis=1)

    def put_kv(lw, ckv, kr_t, kc):
        knt = _dot(lw["wukt"][...], ckv.T.astype(_BF))
        v4 = _dot(ckv.astype(_BF), lw["wuv"][...])
        kr_bf = kr_t.astype(_BF)
        rows_k = pl.ds(kc * rc, rc)
        for hd in range(MLA_HEADS):
            kt_s[hd, kc, :NOPE_DIM, :] = knt[hd * NOPE_DIM:(hd + 1) * NOPE_DIM, :].astype(_BF)
            kt_s[hd, kc, NOPE_DIM:, :] = kr_bf
            v_s[hd, rows_k, :V_DIM] = v4[:, hd * V_DIM:(hd + 1) * V_DIM].astype(_BF)

    def cached_keys(lw):
        ckr_t = lw["ckr"][...]
        ckr_pad = jnp.concatenate([ckr_t, jnp.zeros_like(ckr_t)], axis=0)
        for cc in range(cache_len // rc):
            rows_k = slice(cc * rc, (cc + 1) * rc)
            put_kv(lw, lw["cckv"][rows_k, :], ckr_pad[:, rows_k], cc)

    def halo_base(c, s):
        return (c * n_sub + s) * padded if seq_len <= rc else c * rc

    def project(lw, c):
        r = c * rc
        rows = pl.ds(r, rc)
        row, win_ref = lw["row"], lw["win"]
        h = (_rms(y_ref[rows, :], gn_ref[row, :]) * (1.0 + lw["scale"]) + lw["shift"]).astype(_BF)
        h_s[rows, :] = h
        a = _dot(h, win_ref[:, _C_CQ:_C_GMLA])
        pe = _dot(h, win_ref[:, _C_PX:_C_CB])
        for s, ps in pieces:
            rows_h = pl.ds(halo_base(c, s) + HALO, piece)
            px_s[rows_h, :] = pe[ps, :POOL_WIDTH]
            prod_s[rows_h, :] = pe[ps, POOL_WIDTH:POOL_WIDTH + CONV_WIDTH] * pe[ps, POOL_WIDTH + CONV_WIDTH:]
        return a

    def project_heads(lw, c, a):
        r = c * rc
        rows = pl.ds(r, rc)
        row = lw["row"]
        cq = _rms(a[:, :Q_LORA], gq_ref[row, :]).astype(_BF)
        ckv = _rms(a[:, _C_CKV:_C_KR], gkv_ref[row, :])
        kr = a[:, _C_KR:]
        if has_rope:
            kr = rotate(kr, r)
        kr_t = jnp.where(lane < ROPE_DIM, kr, 0.0).T
        if emit_state:
            for s, ps in pieces:
                ckv_out_ref[c * n_sub + s, lw["l"]] = ckv[ps, :]
                kr_out_ref[c * n_sub + s, lw["l"]] = kr_t[:ROPE_DIM, ps]
        put_kv(lw, ckv, kr_t, cache_len // rc + c)
        for hp in range(MLA_HEADS // 2):
            q2 = _dot(cq, lw["wuq"][:, hp * 2 * HEAD_PAD:(hp + 1) * 2 * HEAD_PAD]) * Q_SCALE
            for j in range(2):
                hd = 2 * hp + j
                c0 = j * HEAD_PAD
                q_s[hd, rows, :NOPE_DIM] = q2[:, c0:c0 + NOPE_DIM].astype(_BF)
                qr = q2[:, c0 + NOPE_DIM:c0 + HEAD_PAD]
                if has_rope:
                    qr = rotate(qr, r)
                q_s[hd, rows, NOPE_DIM:] = qr.astype(_BF)

    def mix_attention(lw, c):
        r = c * rc
        rows = pl.ds(r, rc)
        win_ref = lw["win"]
        h = h_s[rows, :]
        for hp in range(MLA_HEADS // 2):
            g2 = _silu(_dot(h, win_ref[:, _C_GMLA + hp * 2 * V_DIM:_C_GMLA + (hp + 1) * 2 * V_DIM]))
            for j in range(2):
                hd = 2 * hp + j
                for s, ps in pieces:
                    rows_q = pl.ds(r + s * piece, piece)
                    if n_seq == 1:
                        ks = [kt_s[hd, kc] for kc in range(keys // rc)]
                        rows_k = pl.ds(0, keys)
                    else:
                        ks = [kt_s[hd, c, :, ps]]
                        rows_k = rows_q
                    q = q_s[hd, rows_q, :]
                    sc = jnp.concatenate([_dot(q, k) for k in ks], axis=-1)
                    p = jnp.exp2(sc - jnp.max(sc, axis=-1, keepdims=True))
                    ov = _dot(p.astype(_BF), v_s[hd, rows_k, :])
                    o = ov[:, :V_DIM] / ov[:, V_DIM:V_DIM + 1]
                    mixed_s[rows_q, hd * V_DIM:(hd + 1) * V_DIM] = (g2[ps, j * V_DIM:(j + 1) * V_DIM] * o).astype(_BF)

    def mix_rest(lw, c):
        r = c * rc
        rows = pl.ds(r, rc)
        row, win_ref = lw["row"], lw["win"]
        h = h_s[rows, :]
        windows = [pl.ds(halo_base(c, s), piece + 2 * HALO) for s, _ in pieces]
        rcnt = rcnt_ref[pl.ds(0 if seq_len <= rc else r, piece), :]
        pooled = jnp.concatenate([_pool_mix(px_s[w, :], rcnt) for w in windows], axis=0)
        pool = _dot(pooled.astype(_BF), lw["poolw"][...]) * pools_ref[row, :]
        gp = _silu(_dot(h, win_ref[:, _C_GPOOL:_C_PX]))
        mixed_s[rows, MLA_WIDTH:MLA_WIDTH + POOL_WIDTH] = (gp * pool).astype(_BF)

        convw = [convw_ref[k, row, :] for k in range(3)]
        conv = jnp.concatenate([_short_conv(prod_s[w, :], convw) for w in windows], axis=0)
        e2 = _dot(h, win_ref[:, _C_CB:_C_END])
        mixed_s[rows, MLA_WIDTH + POOL_WIDTH:] = (_silu(e2[:, CONV_WIDTH:]) * (e2[:, :CONV_WIDTH] * conv)).astype(_BF)

    def mix_out(lw, c):
        rows = pl.ds(c * rc, rc)
        y_ref[rows, :] = y_ref[rows, :] + lw["gate"] * _dot(mixed_s[rows, :], lw["wout"][...])

    def one_layer(l, carry):
        @pl.when(first_step)
        def _():
            for k in range(len(_WEIGHTS)):
                weight_copy(l, k).wait()

        lw = layer_view(l)
        firsts = [project(lw, c) for c in chunks]
        if cache_len:
            cached_keys(lw)
        for c in chunks:
            project_heads(lw, c, firsts[c])
        for c in chunks:
            steps = (mix_rest, mix_attention) if n_seq > 1 else (mix_attention, mix_rest)
            for step in steps + (mix_out,):
                step(lw, c)
        return carry

    lax.fori_loop(0, DEPTH, one_layer, 0)
    for c in chunks:
        rows = slice(c * rc, (c + 1) * rc)
        y_ref[rows, :] = _rms(y_ref[rows, :], gfin_ref[...])


def _mixer_pass(x2d, mod_all, wts, *, seq_len, cache=None, rope=None, emit_state):
    rows = x2d.shape[0]
    m = ROWS_PER_STEP
    n_seq = m // seq_len
    n_steps = rows // m
    n_all = rows // seq_len
    cache_len = 0 if cache is None else cache[0].shape[2]
    keys = cache_len + seq_len
    has_rope = rope is not None
    rc = ROW_CHUNK
    assert m % rc == 0 and cache_len % rc == 0
    assert (rc % seq_len == 0 and not cache_len) or (n_seq == 1 and seq_len % rc == 0 and not emit_state)

    def const(*shape):
        return pl.BlockSpec(shape, lambda i: (0,) * len(shape), pipeline_mode=pl.Buffered(1))

    args = [x2d, mod_all]
    in_specs = [pl.BlockSpec((m, D_MODEL), lambda i: (i, 0)), const(DEPTH, 3, MOD_ROWS, D_MODEL)]
    if cache_len:
        args += [cache[0], cache[1]]
        in_specs += [pl.BlockSpec((None, DEPTH, cache_len, KV_LORA), lambda i: (i, 0, 0, 0)),
                     pl.BlockSpec((None, DEPTH, ROPE_DIM, cache_len), lambda i: (i, 0, 0, 0))]
    if has_rope:
        args += [rope]
        in_specs += [const(seq_len, ROPE_PAD)]
    args += [_pool_rcnt(seq_len), wts["g_norm"], wts["g_q"], wts["g_kv"], wts["pool_s"], wts["conv_w"],
             wts["g_final"]]
    in_specs += [const(seq_len, POOL_WIDTH), const(DEPTH, D_MODEL), const(DEPTH, Q_LORA), const(DEPTH, KV_LORA),
                 const(DEPTH, POOL_WIDTH), const(3, DEPTH, CONV_WIDTH), const(1, D_MODEL)]
    weights = [wts[name] for name in _WEIGHTS]
    args += weights
    in_specs += [pl.BlockSpec(memory_space=pl.ANY)] * len(weights)

    out_shape = [jax.ShapeDtypeStruct((rows, D_MODEL), _F32)]
    out_specs = [pl.BlockSpec((m, D_MODEL), lambda i: (i, 0))]
    if emit_state:
        out_shape += [jax.ShapeDtypeStruct((n_all, DEPTH, seq_len, KV_LORA), _F32),
                      jax.ShapeDtypeStruct((n_all, DEPTH, ROPE_DIM, seq_len), _F32)]
        out_specs += [pl.BlockSpec((n_seq, DEPTH, seq_len, KV_LORA), lambda i: (i, 0, 0, 0)),
                      pl.BlockSpec((n_seq, DEPTH, ROPE_DIM, seq_len), lambda i: (i, 0, 0, 0))]

    kern = functools.partial(_pass_kernel, n_seq=n_seq, seq_len=seq_len, cache_len=cache_len,
                             has_rope=has_rope, emit_state=emit_state, rc=rc)
    return pl.pallas_call(
        kern,
        grid=(n_steps,),
        in_specs=in_specs,
        out_specs=out_specs,
        out_shape=out_shape,
        scratch_shapes=[
            pltpu.VMEM((m, D_MODEL), _BF),
            pltpu.VMEM((MLA_HEADS, m, HEAD_PAD), _BF),
            pltpu.VMEM((MLA_HEADS, n_seq * keys // rc, HEAD_PAD, rc), _BF),
            pltpu.VMEM((MLA_HEADS, n_seq * keys, 2 * V_DIM), _BF),
            pltpu.VMEM((n_seq * (seq_len + 2 * HALO), POOL_WIDTH), _F32),
            pltpu.VMEM((n_seq * (seq_len + 2 * HALO), CONV_WIDTH), _F32),
            pltpu.VMEM((m, MIX_WIDTH), _BF),
        ] + [pltpu.VMEM(w.shape, w.dtype) for w in weights] + [pltpu.SemaphoreType.DMA((DEPTH, len(weights)))],
        compiler_params=pltpu.CompilerParams(dimension_semantics=("arbitrary",),
                                             vmem_limit_bytes=VMEM_LIMIT_BYTES),
        name="mixer_pass_latent" if has_rope else "mixer_pass_context",
    )(*args)


def _pool_rcnt(seq_len):
    t = np.arange(seq_len)[:, None]
    half = np.repeat(1 << np.arange(POOL_GROUPS), POOL_GROUP_DIM)[None, :]
    cnt = np.minimum(t + half, seq_len) - np.maximum(t - half, 0)
    return jnp.asarray((1.0 / cnt).astype(np.float32))


def _rope_table(seq_len):
    rows = seq_len // GRID_W
    row = np.repeat(np.arange(rows), GRID_W).astype(np.float32)
    col = np.tile(np.arange(GRID_W), rows).astype(np.float32)
    inv = (1.0 / (np.float32(ROPE_BASE) ** (np.arange(0, AXIS_DIM, 2, dtype=np.float32) / np.float32(AXIS_DIM))))
    inv = inv.astype(np.float32)
    ang = np.concatenate([row[:, None] * inv, col[:, None] * inv], axis=-1).astype(np.float64)
    cos, sin = np.cos(ang).astype(np.float32), np.sin(ang).astype(np.float32)
    return jnp.asarray(np.concatenate([cos, cos, sin, sin], axis=-1))


def kernel(x_prompt, x_sample, cache_ckv, cache_krope, c, c_ctx, w_mod, b_mod, g_norm, w_in, g_q, w_uq,
           g_kv, w_ukv, pool_w, pool_s, conv_w, w_out, g_final):
    batch, seq, _ = x_prompt.shape
    dec_batch, dec_seq, _ = x_sample.shape
    assert 1 + dec_batch <= MOD_ROWS and dec_seq == ROWS_PER_STEP and ROWS_PER_STEP % seq == 0

    c_all = jnp.concatenate([c_ctx[None], c, jnp.zeros((MOD_ROWS - 1 - dec_batch, D_MODEL), _F32)], axis=0)
    mod_all = _modulation(c_all, w_mod, b_mod)
    w_in_r, w_out_r, w_uq_r, w_ukt, w_uv, pool_w_r = _prep_weights(
        jnp.transpose(w_in, (0, 2, 1)), w_out, w_uq, w_ukv, pool_w)
    wts = {
        "g_norm": g_norm, "w_in": w_in_r, "g_q": g_q, "w_uq": w_uq_r, "g_kv": g_kv, "w_ukt": w_ukt,
        "w_uv": w_uv, "pool_w": pool_w_r, "pool_s": pool_s, "conv_w": jnp.transpose(conv_w, (1, 0, 2)),
        "w_out": w_out_r, "g_final": g_final.reshape(1, D_MODEL),
    }

    rope = _rope_table(dec_seq)
    cache = (cache_ckv, jnp.transpose(cache_krope, (0, 1, 3, 2)))
    xp = x_prompt.reshape(batch * seq, D_MODEL)
    xs = x_sample.reshape(dec_batch * dec_seq, D_MODEL)
    xp, state_ckv, state_krope_t = _mixer_pass(xp, mod_all, wts, seq_len=seq, emit_state=True)
    (xs,) = _mixer_pass(xs, mod_all, wts, seq_len=dec_seq, cache=cache, rope=rope, emit_state=False)
    y_prompt = xp.reshape(batch, seq, D_MODEL)
    state_krope = jnp.transpose(state_krope_t, (0, 1, 3, 2))
    y_sample = xs.reshape(dec_batch, dec_seq, D_MODEL)
    return (y_prompt, y_sample, state_ckv, state_krope)
```

```python
import functools

import numpy as np
import jax
import jax.numpy as jnp
from jax import lax
from jax.experimental import pallas as pl
from jax.experimental.pallas import tpu as pltpu

D_MODEL = 1024
DEPTH = 2
GRID_W = 64
MLA_HEADS = 4
NOPE_DIM = 128
ROPE_DIM = 64
V_DIM = 128
QK_DIM = NOPE_DIM + ROPE_DIM
MLA_WIDTH = MLA_HEADS * V_DIM
Q_LORA = 384
KV_LORA = 256
POOL_GROUPS = 4
POOL_GROUP_DIM = 64
POOL_WIDTH = POOL_GROUPS * POOL_GROUP_DIM
CONV_WIDTH = 256
MIX_WIDTH = MLA_WIDTH + POOL_WIDTH + CONV_WIDTH
ROPE_BASE = 10000.0
AXIS_DIM = ROPE_DIM // 2
ATTN_SCALE = QK_DIM ** -0.5
Q_SCALE = ATTN_SCALE * float(np.log2(np.e))
EPS = 1e-6

_SPLITS = (Q_LORA, KV_LORA, ROPE_DIM, MLA_WIDTH, POOL_WIDTH, POOL_WIDTH,
           CONV_WIDTH, CONV_WIDTH, CONV_WIDTH, CONV_WIDTH)
_OFFS = [sum(_SPLITS[:i]) for i in range(len(_SPLITS) + 1)]
IN_WIDTH = _OFFS[-1]

V7X_SUBLANES = 8
V7X_MXU_DEPTH = 256
V7X_VMEM_BYTES = 64 * 1024 * 1024
HEAD_PAD = V7X_MXU_DEPTH
ROPE_PAD = HEAD_PAD - NOPE_DIM
KV_HEAD = NOPE_DIM + V_DIM

_C_CQ = 0
_C_CKV = _C_CQ + Q_LORA
_C_KR = _C_CKV + KV_LORA
_C_GMLA = _C_KR + ROPE_PAD
_C_GPOOL = _C_GMLA + MLA_WIDTH
_C_PX = _C_GPOOL + POOL_WIDTH
_C_CC = _C_PX + POOL_WIDTH
_C_CB = _C_CC + 2 * CONV_WIDTH
_C_END = _C_CB + 2 * CONV_WIDTH

MOD_ROWS = 8
ROWS_PER_STEP = 1024
ROW_CHUNK = 512
HALO = V7X_SUBLANES
PREP_COLS = 256
VMEM_LIMIT_BYTES = V7X_VMEM_BYTES - 4 * 1024 * 1024

_WEIGHTS = ("w_in", "w_uq", "w_ukt", "w_uv", "pool_w", "w_out")

_BF = jnp.bfloat16
_F32 = jnp.float32


def _dot(a, b):
    return jnp.dot(a, b, preferred_element_type=_F32)


def _rms(x, g):
    return x * lax.rsqrt(jnp.mean(x * x, axis=-1, keepdims=True) + EPS) * g


def _silu(x):
    return x * jax.nn.sigmoid(x)


def _mod_kernel(c_ref, w_ref, b_ref, o_ref):
    s = _silu(c_ref[...]).astype(_BF)
    for k in range(3):
        cols = slice(k * D_MODEL, (k + 1) * D_MODEL)
        bias = b_ref[pl.ds(pl.program_id(0), 1), cols]
        o_ref[k] = _dot(s, w_ref[:, cols].astype(_BF)) + bias


def _modulation(c_all, w_mod, b_mod):
    return pl.pallas_call(
        _mod_kernel,
        grid=(DEPTH,),
        in_specs=[
            pl.BlockSpec((MOD_ROWS, D_MODEL), lambda l: (0, 0)),
            pl.BlockSpec((None, D_MODEL, 3 * D_MODEL), lambda l: (l, 0, 0)),
            pl.BlockSpec((DEPTH, 3 * D_MODEL), lambda l: (0, 0)),
        ],
        out_specs=pl.BlockSpec((None, 3, MOD_ROWS, D_MODEL), lambda l: (l, 0, 0, 0)),
        out_shape=jax.ShapeDtypeStruct((DEPTH, 3, MOD_ROWS, D_MODEL), _F32),
        compiler_params=pltpu.CompilerParams(dimension_semantics=("arbitrary",),
                                             vmem_limit_bytes=VMEM_LIMIT_BYTES),
        name="modulation",
    )(c_all, w_mod, b_mod)


def _prep_kernel(wint_ref, wout_ref, wuq_ref, wukv_ref, poolw_ref,
                 win_o, wout_o, wuq_o, wukt_o, wuv_o, poolw_o):
    def put(dst, src, width):
        for c in range(0, width, PREP_COLS):
            n = min(PREP_COLS, width - c)
            win_o[:, dst + c:dst + c + n] = wint_ref[src + c:src + c + n, :].T.astype(_BF)

    put(_C_CQ, _OFFS[0], Q_LORA + KV_LORA)
    kr = wint_ref[_OFFS[2]:_OFFS[3], :]
    krb = jnp.concatenate([kr, -kr[AXIS_DIM:], kr[:AXIS_DIM]], axis=0)
    win_o[:, _C_KR:_C_GMLA] = krb.T.astype(_BF)
    put(_C_GMLA, _OFFS[3], MLA_WIDTH)
    put(_C_GPOOL, _OFFS[5], POOL_WIDTH)
    put(_C_PX, _OFFS[4], POOL_WIDTH)
    put(_C_CC, _OFFS[7], 2 * CONV_WIDTH)
    put(_C_CB, _OFFS[6], CONV_WIDTH)
    put(_C_CB + CONV_WIDTH, _OFFS[9], CONV_WIDTH)
    wout_o[...] = wout_ref[...].astype(_BF)

    for hd in range(MLA_HEADS):
        s0, d0 = hd * QK_DIM, hd * HEAD_PAD
        r0 = s0 + NOPE_DIM
        wuq_o[:, d0:d0 + QK_DIM] = wuq_ref[:, s0:s0 + QK_DIM].astype(_BF)
        wuq_o[:, d0 + QK_DIM:d0 + QK_DIM + AXIS_DIM] = (-wuq_ref[:, r0 + AXIS_DIM:r0 + ROPE_DIM]).astype(_BF)
        wuq_o[:, d0 + QK_DIM + AXIS_DIM:d0 + HEAD_PAD] = wuq_ref[:, r0:r0 + AXIS_DIM].astype(_BF)
    for hd in range(MLA_HEADS):
        k0 = hd * KV_HEAD
        wukt_o[hd * NOPE_DIM:(hd + 1) * NOPE_DIM, :] = wukv_ref[:, k0:k0 + NOPE_DIM].T.astype(_BF)
        wuv_o[:, hd * V_DIM:(hd + 1) * V_DIM] = wukv_ref[:, k0 + NOPE_DIM:k0 + KV_HEAD].astype(_BF)
    poolw_o[...] = jnp.zeros((POOL_WIDTH, POOL_WIDTH), _BF)
    for g in range(POOL_GROUPS):
        sl = slice(g * POOL_GROUP_DIM, (g + 1) * POOL_GROUP_DIM)
        poolw_o[sl, sl] = poolw_ref[g].astype(_BF)


def _prep_weights(w_in_t, w_out, w_uq, w_ukv, pool_w):
    per_layer = lambda *shape: pl.BlockSpec((None,) + shape, lambda l: (l,) + (0,) * len(shape))
    return pl.pallas_call(
        _prep_kernel,
        grid=(DEPTH,),
        in_specs=[
            per_layer(IN_WIDTH, D_MODEL),
            per_layer(MIX_WIDTH, D_MODEL),
            per_layer(Q_LORA, MLA_HEADS * QK_DIM),
            per_layer(KV_LORA, MLA_HEADS * KV_HEAD),
            per_layer(POOL_GROUPS, POOL_GROUP_DIM, POOL_GROUP_DIM),
        ],
        out_specs=[
            per_layer(D_MODEL, _C_END),
            per_layer(MIX_WIDTH, D_MODEL),
            per_layer(Q_LORA, MLA_HEADS * HEAD_PAD),
            per_layer(MLA_HEADS * NOPE_DIM, KV_LORA),
            per_layer(KV_LORA, MLA_WIDTH),
            per_layer(POOL_WIDTH, POOL_WIDTH),
        ],
        out_shape=[
            jax.ShapeDtypeStruct((DEPTH, D_MODEL, _C_END), _BF),
            jax.ShapeDtypeStruct((DEPTH, MIX_WIDTH, D_MODEL), _BF),
            jax.ShapeDtypeStruct((DEPTH, Q_LORA, MLA_HEADS * HEAD_PAD), _BF),
            jax.ShapeDtypeStruct((DEPTH, MLA_HEADS * NOPE_DIM, KV_LORA), _BF),
            jax.ShapeDtypeStruct((DEPTH, KV_LORA, MLA_WIDTH), _BF),
            jax.ShapeDtypeStruct((DEPTH, POOL_WIDTH, POOL_WIDTH), _BF),
        ],
        compiler_params=pltpu.CompilerParams(dimension_semantics=("arbitrary",),
                                             vmem_limit_bytes=VMEM_LIMIT_BYTES),
        name="weight_prep",
    )(w_in_t, w_out, w_uq, w_ukv, pool_w)


def _pool_mix(win, rcnt):
    n = win.shape[0]
    rows = n - 2 * HALO
    lane = lax.broadcasted_iota(jnp.int32, (1, POOL_WIDTH), 1)
    sums = [win + pltpu.roll(win, 1, axis=0)]
    for k in (1, 2, 4):
        sums.append(pltpu.roll(sums[-1], k, axis=0) + pltpu.roll(sums[-1], n - k, axis=0))
    sel = sums[-1]
    for g in range(POOL_GROUPS - 2, -1, -1):
        sel = jnp.where(lane < (g + 1) * POOL_GROUP_DIM, sums[g], sel)
    return sel[HALO:HALO + rows] * rcnt - win[HALO:HALO + rows]


def _short_conv(win, w):
    rows = win.shape[0] - 2 * HALO
    return (win[HALO - 1:HALO - 1 + rows] * w[0] + win[HALO:HALO + rows] * w[1]
            + win[HALO + 1:HALO + 1 + rows] * w[2])


def _pass_kernel(*refs, n_seq, seq_len, cache_len, has_rope, emit_state, rc):
    it = iter(refs)
    x_ref, mod_ref = next(it), next(it)
    if cache_len:
        cckv_all_ref, ckr_all_ref = next(it), next(it)
    if has_rope:
        cs_ref = next(it)
    rcnt_ref, gn_ref, gq_ref, gkv_ref, pools_ref, convw_ref, gfin_ref = (next(it) for _ in range(7))
    w_hbm = [next(it) for _ in range(len(_WEIGHTS))]
    y_ref = next(it)
    if emit_state:
        ckv_out_ref, kr_out_ref = next(it), next(it)
    h_s, q_s, kt_s, v_s, px_s, prod_s, mixed_s = (next(it) for _ in range(7))
    w_vm = [next(it) for _ in range(len(_WEIGHTS))]
    w_sem = next(it)
    win_all_ref, wuq_all_ref, wukt_all_ref, wuv_all_ref, poolw_all_ref, wout_all_ref = w_vm

    def weight_copy(l, k):
        return pltpu.make_async_copy(w_hbm[k].at[l], w_vm[k].at[l], w_sem.at[l, k])

    first_step = pl.program_id(0) == 0

    @pl.when(first_step)
    def _():
        for l in range(DEPTH):
            for k in range(len(_WEIGHTS)):
                weight_copy(l, k).start()

    m = n_seq * seq_len
    keys = cache_len + seq_len
    padded = seq_len + 2 * HALO
    lane = lax.broadcasted_iota(jnp.int32, (1, ROPE_PAD), 1)
    mrow = (1 + pl.program_id(0)) if has_rope else 0
    piece = min(seq_len, rc)
    n_sub = rc // piece
    pieces = [(s, slice(s * piece, (s + 1) * piece)) for s in range(n_sub)]
    chunks = range(m // rc)

    for c in chunks:
        y_ref[c * rc:(c + 1) * rc, :] = x_ref[c * rc:(c + 1) * rc, :]

    zeros_halo = jnp.zeros((HALO, POOL_WIDTH), _F32)
    for buf in (px_s, prod_s):
        for s in range(n_seq):
            buf[s * padded:s * padded + HALO, :] = zeros_halo
            buf[(s + 1) * padded - HALO:(s + 1) * padded, :] = zeros_halo
    ones_col = jnp.where(lax.broadcasted_iota(jnp.int32, (n_seq * keys, V_DIM), 1) == 0, 1.0, 0.0).astype(_BF)
    for hd in range(MLA_HEADS):
        v_s[hd, :, V_DIM:] = ones_col

    def layer_view(l):
        lw = dict(l=l, row=pl.ds(l, 1), win=win_all_ref.at[l], wuq=wuq_all_ref.at[l], wukt=wukt_all_ref.at[l],
                  wuv=wuv_all_ref.at[l], poolw=poolw_all_ref.at[l], wout=wout_all_ref.at[l])
        lw["shift"], lw["scale"], lw["gate"] = (mod_ref[l, k, pl.ds(mrow, 1), :] for k in range(3))
        if cache_len:
            lw["cckv"], lw["ckr"] = cckv_all_ref.at[l], ckr_all_ref.at[l]
        return lw

    def rotate(v, r):
        v = v * cs_ref[pl.ds(r, rc), :]
        return v + pltpu.roll(v, ROPE_DIM, axis=1)

    def put_kv(lw, ckv, kr_t, kc):
        knt = _dot(lw["wukt"][...], ckv.T.astype(_BF))
        v4 = _dot(ckv.astype(_BF), lw["wuv"][...])
        kr_bf = kr_t.astype(_BF)
        rows_k = pl.ds(kc * rc, rc)
        for hd in range(MLA_HEADS):
            kt_s[hd, kc, :NOPE_DIM, :] = knt[hd * NOPE_DIM:(hd + 1) * NOPE_DIM, :].astype(_BF)
            kt_s[hd, kc, NOPE_DIM:, :] = kr_bf
            v_s[hd, rows_k, :V_DIM] = v4[:, hd * V_DIM:(hd + 1) * V_DIM].astype(_BF)

    def cached_keys(lw):
        ckr_t = lw["ckr"][...]
        ckr_pad = jnp.concatenate([ckr_t, jnp.zeros_like(ckr_t)], axis=0)
        for cc in range(cache_len // rc):
            rows_k = slice(cc * rc, (cc + 1) * rc)
            put_kv(lw, lw["cckv"][rows_k, :], ckr_pad[:, rows_k], cc)

    def halo_base(c, s):
        return (c * n_sub + s) * padded if seq_len <= rc else c * rc

    def project(lw, c):
        r = c * rc
        rows = pl.ds(r, rc)
        row, win_ref = lw["row"], lw["win"]
        h = (_rms(y_ref[rows, :], gn_ref[row, :]) * (1.0 + lw["scale"]) + lw["shift"]).astype(_BF)
        h_s[rows, :] = h
        a = _dot(h, win_ref[:, _C_CQ:_C_GMLA])
        pe = _dot(h, win_ref[:, _C_PX:_C_CB])
        for s, ps in pieces:
            rows_h = pl.ds(halo_base(c, s) + HALO, piece)
            px_s[rows_h, :] = pe[ps, :POOL_WIDTH]
            prod_s[rows_h, :] = pe[ps, POOL_WIDTH:POOL_WIDTH + CONV_WIDTH] * pe[ps, POOL_WIDTH + CONV_WIDTH:]
        return a

    def project_heads(lw, c, a):
        r = c * rc
        rows = pl.ds(r, rc)
        row = lw["row"]
        cq = _rms(a[:, :Q_LORA], gq_ref[row, :]).astype(_BF)
        ckv = _rms(a[:, _C_CKV:_C_KR], gkv_ref[row, :])
        kr = a[:, _C_KR:]
        if has_rope:
            kr = rotate(kr, r)
        kr_t = jnp.where(lane < ROPE_DIM, kr, 0.0).T
        if emit_state:
            for s, ps in pieces:
                ckv_out_ref[c * n_sub + s, lw["l"]] = ckv[ps, :]
                kr_out_ref[c * n_sub + s, lw["l"]] = kr_t[:ROPE_DIM, ps]
        put_kv(lw, ckv, kr_t, cache_len // rc + c)
        for hp in range(MLA_HEADS // 2):
            q2 = _dot(cq, lw["wuq"][:, hp * 2 * HEAD_PAD:(hp + 1) * 2 * HEAD_PAD]) * Q_SCALE
            for j in range(2):
                hd = 2 * hp + j
                c0 = j * HEAD_PAD
                q_s[hd, rows, :NOPE_DIM] = q2[:, c0:c0 + NOPE_DIM].astype(_BF)
                qr = q2[:, c0 + NOPE_DIM:c0 + HEAD_PAD]
                if has_rope:
                    qr = rotate(qr, r)
                q_s[hd, rows, NOPE_DIM:] = qr.astype(_BF)

    def mix_attention(lw, c):
        r = c * rc
        rows = pl.ds(r, rc)
        win_ref = lw["win"]
        h = h_s[rows, :]
        for hp in range(MLA_HEADS // 2):
            g2 = _silu(_dot(h, win_ref[:, _C_GMLA + hp * 2 * V_DIM:_C_GMLA + (hp + 1) * 2 * V_DIM]))
            for j in range(2):
                hd = 2 * hp + j
                for s, ps in pieces:
                    rows_q = pl.ds(r + s * piece, piece)
                    if n_seq == 1:
                        ks = [kt_s[hd, kc] for kc in range(keys // rc)]
                        rows_k = pl.ds(0, keys)
                    else:
                        ks = [kt_s[hd, c, :, ps]]
                        rows_k = rows_q
                    q = q_s[hd, rows_q, :]
                    sc = jnp.concatenate([_dot(q, k) for k in ks], axis=-1)
                    p = jnp.exp2(sc - jnp.max(sc, axis=-1, keepdims=True))
                    ov = _dot(p.astype(_BF), v_s[hd, rows_k, :])
                    o = ov[:, :V_DIM] / ov[:, V_DIM:V_DIM + 1]
                    mixed_s[rows_q, hd * V_DIM:(hd + 1) * V_DIM] = (g2[ps, j * V_DIM:(j + 1) * V_DIM] * o).astype(_BF)

    def mix_rest(lw, c):
        r = c * rc
        rows = pl.ds(r, rc)
        row, win_ref = lw["row"], lw["win"]
        h = h_s[rows, :]
        windows = [pl.ds(halo_base(c, s), piece + 2 * HALO) for s, _ in pieces]
        rcnt = rcnt_ref[pl.ds(0 if seq_len <= rc else r, piece), :]
        gp = _silu(_dot(h, win_ref[:, _C_GPOOL:_C_PX]))
        e2 = _dot(h, win_ref[:, _C_CB:_C_END])
        pooled = jnp.concatenate([_pool_mix(px_s[w, :], rcnt) for w in windows], axis=0)
        pool = _dot(pooled.astype(_BF), lw["poolw"][...]) * pools_ref[row, :]
        mixed_s[rows, MLA_WIDTH:MLA_WIDTH + POOL_WIDTH] = (gp * pool).astype(_BF)

        convw = [convw_ref[k, row, :] for k in range(3)]
        conv = jnp.concatenate([_short_conv(prod_s[w, :], convw) for w in windows], axis=0)
        mixed_s[rows, MLA_WIDTH + POOL_WIDTH:] = (_silu(e2[:, CONV_WIDTH:]) * (e2[:, :CONV_WIDTH] * conv)).astype(_BF)

    def mix_out(lw, c):
        rows = pl.ds(c * rc, rc)
        y_ref[rows, :] = y_ref[rows, :] + lw["gate"] * _dot(mixed_s[rows, :], lw["wout"][...])

    def one_layer(l, carry):
        @pl.when(first_step)
        def _():
            for k in range(len(_WEIGHTS)):
                weight_copy(l, k).wait()

        lw = layer_view(l)
        firsts = [project(lw, c) for c in chunks]
        if cache_len:
            cached_keys(lw)
        for c in chunks:
            project_heads(lw, c, firsts[c])
        for c in chunks:
            steps = (mix_rest, mix_attention) if n_seq > 1 else (mix_attention, mix_rest)
            for step in steps + (mix_out,):
                step(lw, c)
        return carry

    lax.fori_loop(0, DEPTH, one_layer, 0)
    for c in chunks:
        rows = slice(c * rc, (c + 1) * rc)
        y_ref[rows, :] = _rms(y_ref[rows, :], gfin_ref[...])


def _mixer_pass(x2d, mod_all, wts, *, seq_len, cache=None, rope=None, emit_state):
    rows = x2d.shape[0]
    m = ROWS_PER_STEP
    n_seq = m // seq_len
    n_steps = rows // m
    n_all = rows // seq_len
    cache_len = 0 if cache is None else cache[0].shape[2]
    keys = cache_len + seq_len
    has_rope = rope is not None
    rc = ROW_CHUNK
    assert m % rc == 0 and cache_len % rc == 0
    assert (rc % seq_len == 0 and not cache_len) or (n_seq == 1 and seq_len % rc == 0 and not emit_state)

    def const(*shape):
        return pl.BlockSpec(shape, lambda i: (0,) * len(shape), pipeline_mode=pl.Buffered(1))

    args = [x2d, mod_all]
    in_specs = [pl.BlockSpec((m, D_MODEL), lambda i: (i, 0)), const(DEPTH, 3, MOD_ROWS, D_MODEL)]
    if cache_len:
        args += [cache[0], cache[1]]
        in_specs += [pl.BlockSpec((None, DEPTH, cache_len, KV_LORA), lambda i: (i, 0, 0, 0)),
                     pl.BlockSpec((None, DEPTH, ROPE_DIM, cache_len), lambda i: (i, 0, 0, 0))]
    if has_rope:
        args += [rope]
        in_specs += [const(seq_len, ROPE_PAD)]
    args += [_pool_rcnt(seq_len), wts["g_norm"], wts["g_q"], wts["g_kv"], wts["pool_s"], wts["conv_w"],
             wts["g_final"]]
    in_specs += [const(seq_len, POOL_WIDTH), const(DEPTH, D_MODEL), const(DEPTH, Q_LORA), const(DEPTH, KV_LORA),
                 const(DEPTH, POOL_WIDTH), const(3, DEPTH, CONV_WIDTH), const(1, D_MODEL)]
    weights = [wts[name] for name in _WEIGHTS]
    args += weights
    in_specs += [pl.BlockSpec(memory_space=pl.ANY)] * len(weights)

    out_shape = [jax.ShapeDtypeStruct((rows, D_MODEL), _F32)]
    out_specs = [pl.BlockSpec((m, D_MODEL), lambda i: (i, 0))]
    if emit_state:
        out_shape += [jax.ShapeDtypeStruct((n_all, DEPTH, seq_len, KV_LORA), _F32),
                      jax.ShapeDtypeStruct((n_all, DEPTH, ROPE_DIM, seq_len), _F32)]
        out_specs += [pl.BlockSpec((n_seq, DEPTH, seq_len, KV_LORA), lambda i: (i, 0, 0, 0)),
                      pl.BlockSpec((n_seq, DEPTH, ROPE_DIM, seq_len), lambda i: (i, 0, 0, 0))]

    kern = functools.partial(_pass_kernel, n_seq=n_seq, seq_len=seq_len, cache_len=cache_len,
                             has_rope=has_rope, emit_state=emit_state, rc=rc)
    return pl.pallas_call(
        kern,
        grid=(n_steps,),
        in_specs=in_specs,
        out_specs=out_specs,
        out_shape=out_shape,
        scratch_shapes=[
            pltpu.VMEM((m, D_MODEL), _BF),
            pltpu.VMEM((MLA_HEADS, m, HEAD_PAD), _BF),
            pltpu.VMEM((MLA_HEADS, n_seq * keys // rc, HEAD_PAD, rc), _BF),
            pltpu.VMEM((MLA_HEADS, n_seq * keys, 2 * V_DIM), _BF),
            pltpu.VMEM((n_seq * (seq_len + 2 * HALO), POOL_WIDTH), _F32),
            pltpu.VMEM((n_seq * (seq_len + 2 * HALO), CONV_WIDTH), _F32),
            pltpu.VMEM((m, MIX_WIDTH), _BF),
        ] + [pltpu.VMEM(w.shape, w.dtype) for w in weights] + [pltpu.SemaphoreType.DMA((DEPTH, len(weights)))],
        compiler_params=pltpu.CompilerParams(dimension_semantics=("arbitrary",),
                                             vmem_limit_bytes=VMEM_LIMIT_BYTES),
        name="mixer_pass_latent" if has_rope else "mixer_pass_context",
    )(*args)


def _pool_rcnt(seq_len):
    t = np.arange(seq_len)[:, None]
    half = np.repeat(1 << np.arange(POOL_GROUPS), POOL_GROUP_DIM)[None, :]
    cnt = np.minimum(t + half, seq_len) - np.maximum(t - half, 0)
    return jnp.asarray((1.0 / cnt).astype(np.float32))


def _rope_table(seq_len):
    rows = seq_len // GRID_W
    row = np.repeat(np.arange(rows), GRID_W).astype(np.float32)
    col = np.tile(np.arange(GRID_W), rows).astype(np.float32)
    inv = (1.0 / (np.float32(ROPE_BASE) ** (np.arange(0, AXIS_DIM, 2, dtype=np.float32) / np.float32(AXIS_DIM))))
    inv = inv.astype(np.float32)
    ang = np.concatenate([row[:, None] * inv, col[:, None] * inv], axis=-1).astype(np.float64)
    cos, sin = np.cos(ang).astype(np.float32), np.sin(ang).astype(np.float32)
    return jnp.asarray(np.concatenate([cos, cos, sin, sin], axis=-1))


def kernel(x_prompt, x_sample, cache_ckv, cache_krope, c, c_ctx, w_mod, b_mod, g_norm, w_in, g_q, w_uq,
           g_kv, w_ukv, pool_w, pool_s, conv_w, w_out, g_final):
    batch, seq, _ = x_prompt.shape
    dec_batch, dec_seq, _ = x_sample.shape
    assert 1 + dec_batch <= MOD_ROWS and dec_seq == ROWS_PER_STEP and ROWS_PER_STEP % seq == 0

    c_all = jnp.concatenate([c_ctx[None], c, jnp.zeros((MOD_ROWS - 1 - dec_batch, D_MODEL), _F32)], axis=0)
    mod_all = _modulation(c_all, w_mod, b_mod)
    w_in_r, w_out_r, w_uq_r, w_ukt, w_uv, pool_w_r = _prep_weights(
        jnp.transpose(w_in, (0, 2, 1)), w_out, w_uq, w_ukv, pool_w)
    wts = {
        "g_norm": g_norm, "w_in": w_in_r, "g_q": g_q, "w_uq": w_uq_r, "g_kv": g_kv, "w_ukt": w_ukt,
        "w_uv": w_uv, "pool_w": pool_w_r, "pool_s": pool_s, "conv_w": jnp.transpose(conv_w, (1, 0, 2)),
        "w_out": w_out_r, "g_final": g_final.reshape(1, D_MODEL),
    }

    rope = _rope_table(dec_seq)
    cache = (cache_ckv, jnp.transpose(cache_krope, (0, 1, 3, 2)))
    xp = x_prompt.reshape(batch * seq, D_MODEL)
    xs = x_sample.reshape(dec_batch * dec_seq, D_MODEL)
    xp, state_ckv, state_krope_t = _mixer_pass(xp, mod_all, wts, seq_len=seq, emit_state=True)
    (xs,) = _mixer_pass(xs, mod_all, wts, seq_len=dec_seq, cache=cache, rope=rope, emit_state=False)
    y_prompt = xp.reshape(batch, seq, D_MODEL)
    state_krope = jnp.transpose(state_krope_t, (0, 1, 3, 2))
    y_sample = xs.reshape(dec_batch, dec_seq, D_MODEL)
    return (y_prompt, y_sample, state_ckv, state_krope)
```

```python
import functools

import numpy as np
import jax
import jax.numpy as jnp
from jax import lax
from jax.experimental import pallas as pl
from jax.experimental.pallas import tpu as pltpu

D_MODEL = 1024
DEPTH = 2
GRID_W = 64
MLA_HEADS = 4
NOPE_DIM = 128
ROPE_DIM = 64
V_DIM = 128
QK_DIM = NOPE_DIM + ROPE_DIM
MLA_WIDTH = MLA_HEADS * V_DIM
Q_LORA = 384
KV_LORA = 256
POOL_GROUPS = 4
POOL_GROUP_DIM = 64
POOL_WIDTH = POOL_GROUPS * POOL_GROUP_DIM
CONV_WIDTH = 256
MIX_WIDTH = MLA_WIDTH + POOL_WIDTH + CONV_WIDTH
ROPE_BASE = 10000.0
AXIS_DIM = ROPE_DIM // 2
ATTN_SCALE = QK_DIM ** -0.5
Q_SCALE = ATTN_SCALE * float(np.log2(np.e))
EPS = 1e-6

_SPLITS = (Q_LORA, KV_LORA, ROPE_DIM, MLA_WIDTH, POOL_WIDTH, POOL_WIDTH,
           CONV_WIDTH, CONV_WIDTH, CONV_WIDTH, CONV_WIDTH)
_OFFS = [sum(_SPLITS[:i]) for i in range(len(_SPLITS) + 1)]
IN_WIDTH = _OFFS[-1]

V7X_SUBLANES = 8
V7X_MXU_DEPTH = 256
V7X_VMEM_BYTES = 64 * 1024 * 1024
HEAD_PAD = V7X_MXU_DEPTH
ROPE_PAD = HEAD_PAD - NOPE_DIM
KV_HEAD = NOPE_DIM + V_DIM

_C_CQ = 0
_C_CKV = _C_CQ + Q_LORA
_C_KR = _C_CKV + KV_LORA
_C_GMLA = _C_KR + ROPE_PAD
_C_GPOOL = _C_GMLA + MLA_WIDTH
_C_PX = _C_GPOOL + POOL_WIDTH
_C_CC = _C_PX + POOL_WIDTH
_C_CB = _C_CC + 2 * CONV_WIDTH
_C_END = _C_CB + 2 * CONV_WIDTH

MOD_ROWS = 8
ROWS_PER_STEP = 1024
ROW_CHUNK = 512
HALO = V7X_SUBLANES
PREP_COLS = 256
VMEM_LIMIT_BYTES = V7X_VMEM_BYTES - 4 * 1024 * 1024

_WEIGHTS = ("w_in", "w_uq", "w_ukt", "w_uv", "pool_w", "w_out")

_BF = jnp.bfloat16
_F32 = jnp.float32


def _dot(a, b):
    return jnp.dot(a, b, preferred_element_type=_F32)


def _rms(x, g):
    return x * lax.rsqrt(jnp.mean(x * x, axis=-1, keepdims=True) + EPS) * g


def _silu(x):
    return x * jax.nn.sigmoid(x)


def _mod_kernel(c_ref, w_ref, b_ref, o_ref):
    s = _silu(c_ref[...]).astype(_BF)
    for k in range(3):
        cols = slice(k * D_MODEL, (k + 1) * D_MODEL)
        bias = b_ref[pl.ds(pl.program_id(0), 1), cols]
        o_ref[k] = _dot(s, w_ref[:, cols].astype(_BF)) + bias


def _modulation(c_all, w_mod, b_mod):
    return pl.pallas_call(
        _mod_kernel,
        grid=(DEPTH,),
        in_specs=[
            pl.BlockSpec((MOD_ROWS, D_MODEL), lambda l: (0, 0)),
            pl.BlockSpec((None, D_MODEL, 3 * D_MODEL), lambda l: (l, 0, 0)),
            pl.BlockSpec((DEPTH, 3 * D_MODEL), lambda l: (0, 0)),
        ],
        out_specs=pl.BlockSpec((None, 3, MOD_ROWS, D_MODEL), lambda l: (l, 0, 0, 0)),
        out_shape=jax.ShapeDtypeStruct((DEPTH, 3, MOD_ROWS, D_MODEL), _F32),
        compiler_params=pltpu.CompilerParams(dimension_semantics=("arbitrary",),
                                             vmem_limit_bytes=VMEM_LIMIT_BYTES),
        name="modulation",
    )(c_all, w_mod, b_mod)


def _prep_kernel(wint_ref, wout_ref, wuq_ref, wukv_ref, poolw_ref,
                 win_o, wout_o, wuq_o, wukt_o, wuv_o, poolw_o):
    def put(dst, src, width):
        for c in range(0, width, PREP_COLS):
            n = min(PREP_COLS, width - c)
            win_o[:, dst + c:dst + c + n] = wint_ref[src + c:src + c + n, :].T.astype(_BF)

    put(_C_CQ, _OFFS[0], Q_LORA + KV_LORA)
    kr = wint_ref[_OFFS[2]:_OFFS[3], :]
    krb = jnp.concatenate([kr, -kr[AXIS_DIM:], kr[:AXIS_DIM]], axis=0)
    win_o[:, _C_KR:_C_GMLA] = krb.T.astype(_BF)
    put(_C_GMLA, _OFFS[3], MLA_WIDTH)
    put(_C_GPOOL, _OFFS[5], POOL_WIDTH)
    put(_C_PX, _OFFS[4], POOL_WIDTH)
    put(_C_CC, _OFFS[7], 2 * CONV_WIDTH)
    put(_C_CB, _OFFS[6], CONV_WIDTH)
    put(_C_CB + CONV_WIDTH, _OFFS[9], CONV_WIDTH)
    wout_o[...] = wout_ref[...].astype(_BF)

    for hd in range(MLA_HEADS):
        s0, d0 = hd * QK_DIM, hd * HEAD_PAD
        r0 = s0 + NOPE_DIM
        wuq_o[:, d0:d0 + QK_DIM] = wuq_ref[:, s0:s0 + QK_DIM].astype(_BF)
        wuq_o[:, d0 + QK_DIM:d0 + QK_DIM + AXIS_DIM] = (-wuq_ref[:, r0 + AXIS_DIM:r0 + ROPE_DIM]).astype(_BF)
        wuq_o[:, d0 + QK_DIM + AXIS_DIM:d0 + HEAD_PAD] = wuq_ref[:, r0:r0 + AXIS_DIM].astype(_BF)
    for hd in range(MLA_HEADS):
        k0 = hd * KV_HEAD
        wukt_o[hd * NOPE_DIM:(hd + 1) * NOPE_DIM, :] = wukv_ref[:, k0:k0 + NOPE_DIM].T.astype(_BF)
        wuv_o[:, hd * V_DIM:(hd + 1) * V_DIM] = wukv_ref[:, k0 + NOPE_DIM:k0 + KV_HEAD].astype(_BF)
    poolw_o[...] = jnp.zeros((POOL_WIDTH, POOL_WIDTH), _BF)
    for g in range(POOL_GROUPS):
        sl = slice(g * POOL_GROUP_DIM, (g + 1) * POOL_GROUP_DIM)
        poolw_o[sl, sl] = poolw_ref[g].astype(_BF)


def _prep_weights(w_in_t, w_out, w_uq, w_ukv, pool_w):
    per_layer = lambda *shape: pl.BlockSpec((None,) + shape, lambda l: (l,) + (0,) * len(shape))
    return pl.pallas_call(
        _prep_kernel,
        grid=(DEPTH,),
        in_specs=[
            per_layer(IN_WIDTH, D_MODEL),
            per_layer(MIX_WIDTH, D_MODEL),
            per_layer(Q_LORA, MLA_HEADS * QK_DIM),
            per_layer(KV_LORA, MLA_HEADS * KV_HEAD),
            per_layer(POOL_GROUPS, POOL_GROUP_DIM, POOL_GROUP_DIM),
        ],
        out_specs=[
            per_layer(D_MODEL, _C_END),
            per_layer(MIX_WIDTH, D_MODEL),
            per_layer(Q_LORA, MLA_HEADS * HEAD_PAD),
            per_layer(MLA_HEADS * NOPE_DIM, KV_LORA),
            per_layer(KV_LORA, MLA_WIDTH),
            per_layer(POOL_WIDTH, POOL_WIDTH),
        ],
        out_shape=[
            jax.ShapeDtypeStruct((DEPTH, D_MODEL, _C_END), _BF),
            jax.ShapeDtypeStruct((DEPTH, MIX_WIDTH, D_MODEL), _BF),
            jax.ShapeDtypeStruct((DEPTH, Q_LORA, MLA_HEADS * HEAD_PAD), _BF),
            jax.ShapeDtypeStruct((DEPTH, MLA_HEADS * NOPE_DIM, KV_LORA), _BF),
            jax.ShapeDtypeStruct((DEPTH, KV_LORA, MLA_WIDTH), _BF),
            jax.ShapeDtypeStruct((DEPTH, POOL_WIDTH, POOL_WIDTH), _BF),
        ],
        compiler_params=pltpu.CompilerParams(dimension_semantics=("arbitrary",),
                                             vmem_limit_bytes=VMEM_LIMIT_BYTES),
        name="weight_prep",
    )(w_in_t, w_out, w_uq, w_ukv, pool_w)


def _pool_mix(win, rcnt):
    n = win.shape[0]
    rows = n - 2 * HALO
    lane = lax.broadcasted_iota(jnp.int32, (1, POOL_WIDTH), 1)
    sums = [win + pltpu.roll(win, 1, axis=0)]
    for k in (1, 2, 4):
        sums.append(pltpu.roll(sums[-1], k, axis=0) + pltpu.roll(sums[-1], n - k, axis=0))
    sel = sums[-1]
    for g in range(POOL_GROUPS - 2, -1, -1):
        sel = jnp.where(lane < (g + 1) * POOL_GROUP_DIM, sums[g], sel)
    return sel[HALO:HALO + rows] * rcnt - win[HALO:HALO + rows]


def _short_conv(win, w):
    rows = win.shape[0] - 2 * HALO
    return (win[HALO - 1:HALO - 1 + rows] * w[0] + win[HALO:HALO + rows] * w[1]
            + win[HALO + 1:HALO + 1 + rows] * w[2])


def _pass_kernel(*refs, n_seq, seq_len, cache_len, has_rope, emit_state, rc):
    it = iter(refs)
    x_ref, mod_ref = next(it), next(it)
    if cache_len:
        cckv_all_ref, ckr_all_ref = next(it), next(it)
    if has_rope:
        cs_ref = next(it)
    rcnt_ref, gn_ref, gq_ref, gkv_ref, pools_ref, convw_ref, gfin_ref = (next(it) for _ in range(7))
    w_hbm = [next(it) for _ in range(len(_WEIGHTS))]
    y_ref = next(it)
    if emit_state:
        ckv_out_ref, kr_out_ref = next(it), next(it)
    h_s, q_s, kt_s, v_s, px_s, prod_s, mixed_s = (next(it) for _ in range(7))
    w_vm = [next(it) for _ in range(len(_WEIGHTS))]
    w_sem = next(it)
    win_all_ref, wuq_all_ref, wukt_all_ref, wuv_all_ref, poolw_all_ref, wout_all_ref = w_vm

    def weight_copy(l, k):
        return pltpu.make_async_copy(w_hbm[k].at[l], w_vm[k].at[l], w_sem.at[l, k])

    first_step = pl.program_id(0) == 0

    @pl.when(first_step)
    def _():
        for l in range(DEPTH):
            for k in range(len(_WEIGHTS)):
                weight_copy(l, k).start()

    m = n_seq * seq_len
    keys = cache_len + seq_len
    padded = seq_len + 2 * HALO
    lane = lax.broadcasted_iota(jnp.int32, (1, ROPE_PAD), 1)
    mrow = (1 + pl.program_id(0)) if has_rope else 0
    piece = min(seq_len, rc)
    n_sub = rc // piece
    pieces = [(s, slice(s * piece, (s + 1) * piece)) for s in range(n_sub)]
    chunks = range(m // rc)

    for c in chunks:
        y_ref[c * rc:(c + 1) * rc, :] = x_ref[c * rc:(c + 1) * rc, :]

    zeros_halo = jnp.zeros((HALO, POOL_WIDTH), _F32)
    for buf in (px_s, prod_s):
        for s in range(n_seq):
            buf[s * padded:s * padded + HALO, :] = zeros_halo
            buf[(s + 1) * padded - HALO:(s + 1) * padded, :] = zeros_halo
    ones_col = jnp.where(lax.broadcasted_iota(jnp.int32, (n_seq * keys, V_DIM), 1) == 0, 1.0, 0.0).astype(_BF)
    for hd in range(MLA_HEADS):
        v_s[hd, :, V_DIM:] = ones_col

    def layer_view(l):
        lw = dict(l=l, row=pl.ds(l, 1), win=win_all_ref.at[l], wuq=wuq_all_ref.at[l], wukt=wukt_all_ref.at[l],
                  wuv=wuv_all_ref.at[l], poolw=poolw_all_ref.at[l], wout=wout_all_ref.at[l])
        lw["shift"], lw["scale"], lw["gate"] = (mod_ref[l, k, pl.ds(mrow, 1), :] for k in range(3))
        if cache_len:
            lw["cckv"], lw["ckr"] = cckv_all_ref.at[l], ckr_all_ref.at[l]
        return lw

    def rotate(v, r):
        v = v * cs_ref[pl.ds(r, rc), :]
        return v + pltpu.roll(v, ROPE_DIM, axis=1)

    def put_kv(lw, ckv, kr_t, kc):
        knt = _dot(lw["wukt"][...], ckv.T.astype(_BF))
        v4 = _dot(ckv.astype(_BF), lw["wuv"][...])
        kr_bf = kr_t.astype(_BF)
        rows_k = pl.ds(kc * rc, rc)
        for hd in range(MLA_HEADS):
            kt_s[hd, kc, :NOPE_DIM, :] = knt[hd * NOPE_DIM:(hd + 1) * NOPE_DIM, :].astype(_BF)
            kt_s[hd, kc, NOPE_DIM:, :] = kr_bf
            v_s[hd, rows_k, :V_DIM] = v4[:, hd * V_DIM:(hd + 1) * V_DIM].astype(_BF)

    def cached_keys(lw):
        ckr_t = lw["ckr"][...]
        ckr_pad = jnp.concatenate([ckr_t, jnp.zeros_like(ckr_t)], axis=0)
        for cc in range(cache_len // rc):
            rows_k = slice(cc * rc, (cc + 1) * rc)
            put_kv(lw, lw["cckv"][rows_k, :], ckr_pad[:, rows_k], cc)

    def halo_base(c, s):
        return (c * n_sub + s) * padded if seq_len <= rc else c * rc

    def project(lw, c):
        r = c * rc
        rows = pl.ds(r, rc)
        row, win_ref = lw["row"], lw["win"]
        h = (_rms(y_ref[rows, :], gn_ref[row, :]) * (1.0 + lw["scale"]) + lw["shift"]).astype(_BF)
        h_s[rows, :] = h
        a = _dot(h, win_ref[:, _C_CQ:_C_GMLA])
        pe = _dot(h, win_ref[:, _C_PX:_C_CB])
        for s, ps in pieces:
            rows_h = pl.ds(halo_base(c, s) + HALO, piece)
            px_s[rows_h, :] = pe[ps, :POOL_WIDTH]
            prod_s[rows_h, :] = pe[ps, POOL_WIDTH:POOL_WIDTH + CONV_WIDTH] * pe[ps, POOL_WIDTH + CONV_WIDTH:]
        return a

    def project_heads(lw, c, a):
        r = c * rc
        rows = pl.ds(r, rc)
        row = lw["row"]
        cq = _rms(a[:, :Q_LORA], gq_ref[row, :]).astype(_BF)
        for hp in range(MLA_HEADS // 2):
            q2 = _dot(cq, lw["wuq"][:, hp * 2 * HEAD_PAD:(hp + 1) * 2 * HEAD_PAD]) * Q_SCALE
            for j in range(2):
                hd = 2 * hp + j
                c0 = j * HEAD_PAD
                q_s[hd, rows, :NOPE_DIM] = q2[:, c0:c0 + NOPE_DIM].astype(_BF)
                qr = q2[:, c0 + NOPE_DIM:c0 + HEAD_PAD]
                if has_rope:
                    qr = rotate(qr, r)
                q_s[hd, rows, NOPE_DIM:] = qr.astype(_BF)
        ckv = _rms(a[:, _C_CKV:_C_KR], gkv_ref[row, :])
        kr = a[:, _C_KR:]
        if has_rope:
            kr = rotate(kr, r)
        kr_t = jnp.where(lane < ROPE_DIM, kr, 0.0).T
        if emit_state:
            for s, ps in pieces:
                ckv_out_ref[c * n_sub + s, lw["l"]] = ckv[ps, :]
                kr_out_ref[c * n_sub + s, lw["l"]] = kr_t[:ROPE_DIM, ps]
        put_kv(lw, ckv, kr_t, cache_len // rc + c)

    def mix_attention(lw, c):
        r = c * rc
        rows = pl.ds(r, rc)
        win_ref = lw["win"]
        h = h_s[rows, :]
        for hp in range(MLA_HEADS // 2):
            g2 = _silu(_dot(h, win_ref[:, _C_GMLA + hp * 2 * V_DIM:_C_GMLA + (hp + 1) * 2 * V_DIM]))
            for j in range(2):
                hd = 2 * hp + j
                for s, ps in pieces:
                    rows_q = pl.ds(r + s * piece, piece)
                    if n_seq == 1:
                        ks = [kt_s[hd, kc] for kc in range(keys // rc)]
                        rows_k = pl.ds(0, keys)
                    else:
                        ks = [kt_s[hd, c, :, ps]]
                        rows_k = rows_q
                    q = q_s[hd, rows_q, :]
                    sc = jnp.concatenate([_dot(q, k) for k in ks], axis=-1)
                    p = jnp.exp2(sc - jnp.max(sc, axis=-1, keepdims=True))
                    ov = _dot(p.astype(_BF), v_s[hd, rows_k, :])
                    o = ov[:, :V_DIM] / ov[:, V_DIM:V_DIM + 1]
                    mixed_s[rows_q, hd * V_DIM:(hd + 1) * V_DIM] = (g2[ps, j * V_DIM:(j + 1) * V_DIM] * o).astype(_BF)

    def mix_rest(lw, c):
        r = c * rc
        rows = pl.ds(r, rc)
        row, win_ref = lw["row"], lw["win"]
        h = h_s[rows, :]
        windows = [pl.ds(halo_base(c, s), piece + 2 * HALO) for s, _ in pieces]
        rcnt = rcnt_ref[pl.ds(0 if seq_len <= rc else r, piece), :]
        pooled = jnp.concatenate([_pool_mix(px_s[w, :], rcnt) for w in windows], axis=0)
        pool = _dot(pooled.astype(_BF), lw["poolw"][...]) * pools_ref[row, :]
        gp = _silu(_dot(h, win_ref[:, _C_GPOOL:_C_PX]))
        mixed_s[rows, MLA_WIDTH:MLA_WIDTH + POOL_WIDTH] = (gp * pool).astype(_BF)

        convw = [convw_ref[k, row, :] for k in range(3)]
        conv = jnp.concatenate([_short_conv(prod_s[w, :], convw) for w in windows], axis=0)
        e2 = _dot(h, win_ref[:, _C_CB:_C_END])
        mixed_s[rows, MLA_WIDTH + POOL_WIDTH:] = (_silu(e2[:, CONV_WIDTH:]) * (e2[:, :CONV_WIDTH] * conv)).astype(_BF)

    def mix_out(lw, c):
        rows = pl.ds(c * rc, rc)
        y_ref[rows, :] = y_ref[rows, :] + lw["gate"] * _dot(mixed_s[rows, :], lw["wout"][...])

    def one_layer(l, carry):
        @pl.when(first_step)
        def _():
            for k in range(len(_WEIGHTS)):
                weight_copy(l, k).wait()

        lw = layer_view(l)
        firsts = [project(lw, c) for c in chunks]
        if cache_len:
            cached_keys(lw)
        for c in chunks:
            project_heads(lw, c, firsts[c])
        for c in chunks:
            steps = (mix_rest, mix_attention) if n_seq > 1 else (mix_attention, mix_rest)
            for step in steps + (mix_out,):
                step(lw, c)
        return carry

    lax.fori_loop(0, DEPTH, one_layer, 0)
    for c in chunks:
        rows = slice(c * rc, (c + 1) * rc)
        y_ref[rows, :] = _rms(y_ref[rows, :], gfin_ref[...])


def _mixer_pass(x2d, mod_all, wts, *, seq_len, cache=None, rope=None, emit_state):
    rows = x2d.shape[0]
    m = ROWS_PER_STEP
    n_seq = m // seq_len
    n_steps = rows // m
    n_all = rows // seq_len
    cache_len = 0 if cache is None else cache[0].shape[2]
    keys = cache_len + seq_len
    has_rope = rope is not None
    rc = ROW_CHUNK
    assert m % rc == 0 and cache_len % rc == 0
    assert (rc % seq_len == 0 and not cache_len) or (n_seq == 1 and seq_len % rc == 0 and not emit_state)

    def const(*shape):
        return pl.BlockSpec(shape, lambda i: (0,) * len(shape), pipeline_mode=pl.Buffered(1))

    args = [x2d, mod_all]
    in_specs = [pl.BlockSpec((m, D_MODEL), lambda i: (i, 0)), const(DEPTH, 3, MOD_ROWS, D_MODEL)]
    if cache_len:
        args += [cache[0], cache[1]]
        in_specs += [pl.BlockSpec((None, DEPTH, cache_len, KV_LORA), lambda i: (i, 0, 0, 0)),
                     pl.BlockSpec((None, DEPTH, ROPE_DIM, cache_len), lambda i: (i, 0, 0, 0))]
    if has_rope:
        args += [rope]
        in_specs += [const(seq_len, ROPE_PAD)]
    args += [_pool_rcnt(seq_len), wts["g_norm"], wts["g_q"], wts["g_kv"], wts["pool_s"], wts["conv_w"],
             wts["g_final"]]
    in_specs += [const(seq_len, POOL_WIDTH), const(DEPTH, D_MODEL), const(DEPTH, Q_LORA), const(DEPTH, KV_LORA),
                 const(DEPTH, POOL_WIDTH), const(3, DEPTH, CONV_WIDTH), const(1, D_MODEL)]
    weights = [wts[name] for name in _WEIGHTS]
    args += weights
    in_specs += [pl.BlockSpec(memory_space=pl.ANY)] * len(weights)

    out_shape = [jax.ShapeDtypeStruct((rows, D_MODEL), _F32)]
    out_specs = [pl.BlockSpec((m, D_MODEL), lambda i: (i, 0))]
    if emit_state:
        out_shape += [jax.ShapeDtypeStruct((n_all, DEPTH, seq_len, KV_LORA), _F32),
                      jax.ShapeDtypeStruct((n_all, DEPTH, ROPE_DIM, seq_len), _F32)]
        out_specs += [pl.BlockSpec((n_seq, DEPTH, seq_len, KV_LORA), lambda i: (i, 0, 0, 0)),
                      pl.BlockSpec((n_seq, DEPTH, ROPE_DIM, seq_len), lambda i: (i, 0, 0, 0))]

    kern = functools.partial(_pass_kernel, n_seq=n_seq, seq_len=seq_len, cache_len=cache_len,
                             has_rope=has_rope, emit_state=emit_state, rc=rc)
    return pl.pallas_call(
        kern,
        grid=(n_steps,),
        in_specs=in_specs,
        out_specs=out_specs,
        out_shape=out_shape,
        scratch_shapes=[
            pltpu.VMEM((m, D_MODEL), _BF),
            pltpu.VMEM((MLA_HEADS, m, HEAD_PAD), _BF),
            pltpu.VMEM((MLA_HEADS, n_seq * keys // rc, HEAD_PAD, rc), _BF),
            pltpu.VMEM((MLA_HEADS, n_seq * keys, 2 * V_DIM), _BF),
            pltpu.VMEM((n_seq * (seq_len + 2 * HALO), POOL_WIDTH), _F32),
            pltpu.VMEM((n_seq * (seq_len + 2 * HALO), CONV_WIDTH), _F32),
            pltpu.VMEM((m, MIX_WIDTH), _BF),
        ] + [pltpu.VMEM(w.shape, w.dtype) for w in weights] + [pltpu.SemaphoreType.DMA((DEPTH, len(weights)))],
        compiler_params=pltpu.CompilerParams(dimension_semantics=("arbitrary",),
                                             vmem_limit_bytes=VMEM_LIMIT_BYTES),
        name="mixer_pass_latent" if has_rope else "mixer_pass_context",
    )(*args)


def _pool_rcnt(seq_len):
    t = np.arange(seq_len)[:, None]
    half = np.repeat(1 << np.arange(POOL_GROUPS), POOL_GROUP_DIM)[None, :]
    cnt = np.minimum(t + half, seq_len) - np.maximum(t - half, 0)
    return jnp.asarray((1.0 / cnt).astype(np.float32))


def _rope_table(seq_len):
    rows = seq_len // GRID_W
    row = np.repeat(np.arange(rows), GRID_W).astype(np.float32)
    col = np.tile(np.arange(GRID_W), rows).astype(np.float32)
    inv = (1.0 / (np.float32(ROPE_BASE) ** (np.arange(0, AXIS_DIM, 2, dtype=np.float32) / np.float32(AXIS_DIM))))
    inv = inv.astype(np.float32)
    ang = np.concatenate([row[:, None] * inv, col[:, None] * inv], axis=-1).astype(np.float64)
    cos, sin = np.cos(ang).astype(np.float32), np.sin(ang).astype(np.float32)
    return jnp.asarray(np.concatenate([cos, cos, sin, sin], axis=-1))


def kernel(x_prompt, x_sample, cache_ckv, cache_krope, c, c_ctx, w_mod, b_mod, g_norm, w_in, g_q, w_uq,
           g_kv, w_ukv, pool_w, pool_s, conv_w, w_out, g_final):
    batch, seq, _ = x_prompt.shape
    dec_batch, dec_seq, _ = x_sample.shape
    assert 1 + dec_batch <= MOD_ROWS and dec_seq == ROWS_PER_STEP and ROWS_PER_STEP % seq == 0

    c_all = jnp.concatenate([c_ctx[None], c, jnp.zeros((MOD_ROWS - 1 - dec_batch, D_MODEL), _F32)], axis=0)
    mod_all = _modulation(c_all, w_mod, b_mod)
    w_in_r, w_out_r, w_uq_r, w_ukt, w_uv, pool_w_r = _prep_weights(
        jnp.transpose(w_in, (0, 2, 1)), w_out, w_uq, w_ukv, pool_w)
    wts = {
        "g_norm": g_norm, "w_in": w_in_r, "g_q": g_q, "w_uq": w_uq_r, "g_kv": g_kv, "w_ukt": w_ukt,
        "w_uv": w_uv, "pool_w": pool_w_r, "pool_s": pool_s, "conv_w": jnp.transpose(conv_w, (1, 0, 2)),
        "w_out": w_out_r, "g_final": g_final.reshape(1, D_MODEL),
    }

    rope = _rope_table(dec_seq)
    cache = (cache_ckv, jnp.transpose(cache_krope, (0, 1, 3, 2)))
    xp = x_prompt.reshape(batch * seq, D_MODEL)
    xs = x_sample.reshape(dec_batch * dec_seq, D_MODEL)
    xp, state_ckv, state_krope_t = _mixer_pass(xp, mod_all, wts, seq_len=seq, emit_state=True)
    (xs,) = _mixer_pass(xs, mod_all, wts, seq_len=dec_seq, cache=cache, rope=rope, emit_state=False)
    y_prompt = xp.reshape(batch, seq, D_MODEL)
    state_krope = jnp.transpose(state_krope_t, (0, 1, 3, 2))
    y_sample = xs.reshape(dec_batch, dec_seq, D_MODEL)
    return (y_prompt, y_sample, state_ckv, state_krope)
```

```python
import functools

import numpy as np
import jax
import jax.numpy as jnp
from jax import lax
from jax.experimental import pallas as pl
from jax.experimental.pallas import tpu as pltpu

D_MODEL = 1024
DEPTH = 2
GRID_W = 64
MLA_HEADS = 4
NOPE_DIM = 128
ROPE_DIM = 64
V_DIM = 128
QK_DIM = NOPE_DIM + ROPE_DIM
MLA_WIDTH = MLA_HEADS * V_DIM
Q_LORA = 384
KV_LORA = 256
POOL_GROUPS = 4
POOL_GROUP_DIM = 64
POOL_WIDTH = POOL_GROUPS * POOL_GROUP_DIM
CONV_WIDTH = 256
MIX_WIDTH = MLA_WIDTH + POOL_WIDTH + CONV_WIDTH
ROPE_BASE = 10000.0
AXIS_DIM = ROPE_DIM // 2
ATTN_SCALE = QK_DIM ** -0.5
Q_SCALE = ATTN_SCALE * float(np.log2(np.e))
EPS = 1e-6

_SPLITS = (Q_LORA, KV_LORA, ROPE_DIM, MLA_WIDTH, POOL_WIDTH, POOL_WIDTH,
           CONV_WIDTH, CONV_WIDTH, CONV_WIDTH, CONV_WIDTH)
_OFFS = [sum(_SPLITS[:i]) for i in range(len(_SPLITS) + 1)]
IN_WIDTH = _OFFS[-1]

V7X_SUBLANES = 8
V7X_MXU_DEPTH = 256
V7X_VMEM_BYTES = 64 * 1024 * 1024
HEAD_PAD = V7X_MXU_DEPTH
ROPE_PAD = HEAD_PAD - NOPE_DIM
KV_HEAD = NOPE_DIM + V_DIM

_C_CQ = 0
_C_CKV = _C_CQ + Q_LORA
_C_KR = _C_CKV + KV_LORA
_C_GMLA = _C_KR + ROPE_PAD
_C_GPOOL = _C_GMLA + MLA_WIDTH
_C_PX = _C_GPOOL + POOL_WIDTH
_C_CC = _C_PX + POOL_WIDTH
_C_CB = _C_CC + 2 * CONV_WIDTH
_C_END = _C_CB + 2 * CONV_WIDTH

MOD_ROWS = 8
ROWS_PER_STEP = 1024
ROW_CHUNK = 512
HALO = V7X_SUBLANES
PREP_COLS = 256
VMEM_LIMIT_BYTES = V7X_VMEM_BYTES - 4 * 1024 * 1024

_WEIGHTS = ("w_in", "w_uq", "w_ukt", "w_uv", "pool_w", "w_out")

_BF = jnp.bfloat16
_F32 = jnp.float32


def _dot(a, b):
    return jnp.dot(a, b, preferred_element_type=_F32)


def _rms(x, g):
    return x * lax.rsqrt(jnp.mean(x * x, axis=-1, keepdims=True) + EPS) * g


def _silu(x):
    return x * jax.nn.sigmoid(x)


def _mod_kernel(c_ref, w_ref, b_ref, o_ref):
    s = _silu(c_ref[...]).astype(_BF)
    for k in range(3):
        cols = slice(k * D_MODEL, (k + 1) * D_MODEL)
        bias = b_ref[pl.ds(pl.program_id(0), 1), cols]
        o_ref[k] = _dot(s, w_ref[:, cols].astype(_BF)) + bias


def _modulation(c_all, w_mod, b_mod):
    return pl.pallas_call(
        _mod_kernel,
        grid=(DEPTH,),
        in_specs=[
            pl.BlockSpec((MOD_ROWS, D_MODEL), lambda l: (0, 0)),
            pl.BlockSpec((None, D_MODEL, 3 * D_MODEL), lambda l: (l, 0, 0)),
            pl.BlockSpec((DEPTH, 3 * D_MODEL), lambda l: (0, 0)),
        ],
        out_specs=pl.BlockSpec((None, 3, MOD_ROWS, D_MODEL), lambda l: (l, 0, 0, 0)),
        out_shape=jax.ShapeDtypeStruct((DEPTH, 3, MOD_ROWS, D_MODEL), _F32),
        compiler_params=pltpu.CompilerParams(dimension_semantics=("arbitrary",),
                                             vmem_limit_bytes=VMEM_LIMIT_BYTES),
        name="modulation",
    )(c_all, w_mod, b_mod)


def _prep_kernel(wint_ref, wout_ref, wuq_ref, wukv_ref, poolw_ref,
                 win_o, wout_o, wuq_o, wukt_o, wuv_o, poolw_o):
    def put(dst, src, width):
        for c in range(0, width, PREP_COLS):
            n = min(PREP_COLS, width - c)
            win_o[:, dst + c:dst + c + n] = wint_ref[src + c:src + c + n, :].T.astype(_BF)

    put(_C_CQ, _OFFS[0], Q_LORA + KV_LORA)
    kr = wint_ref[_OFFS[2]:_OFFS[3], :]
    krb = jnp.concatenate([kr, -kr[AXIS_DIM:], kr[:AXIS_DIM]], axis=0)
    win_o[:, _C_KR:_C_GMLA] = krb.T.astype(_BF)
    put(_C_GMLA, _OFFS[3], MLA_WIDTH)
    put(_C_GPOOL, _OFFS[5], POOL_WIDTH)
    put(_C_PX, _OFFS[4], POOL_WIDTH)
    put(_C_CC, _OFFS[7], 2 * CONV_WIDTH)
    put(_C_CB, _OFFS[6], CONV_WIDTH)
    put(_C_CB + CONV_WIDTH, _OFFS[9], CONV_WIDTH)
    wout_o[...] = wout_ref[...].astype(_BF)

    for hd in range(MLA_HEADS):
        s0, d0 = hd * QK_DIM, hd * HEAD_PAD
        r0 = s0 + NOPE_DIM
        wuq_o[:, d0:d0 + QK_DIM] = wuq_ref[:, s0:s0 + QK_DIM].astype(_BF)
        wuq_o[:, d0 + QK_DIM:d0 + QK_DIM + AXIS_DIM] = (-wuq_ref[:, r0 + AXIS_DIM:r0 + ROPE_DIM]).astype(_BF)
        wuq_o[:, d0 + QK_DIM + AXIS_DIM:d0 + HEAD_PAD] = wuq_ref[:, r0:r0 + AXIS_DIM].astype(_BF)
    for hd in range(MLA_HEADS):
        k0 = hd * KV_HEAD
        wukt_o[hd * NOPE_DIM:(hd + 1) * NOPE_DIM, :] = wukv_ref[:, k0:k0 + NOPE_DIM].T.astype(_BF)
        wuv_o[:, hd * V_DIM:(hd + 1) * V_DIM] = wukv_ref[:, k0 + NOPE_DIM:k0 + KV_HEAD].astype(_BF)
    poolw_o[...] = jnp.zeros((POOL_WIDTH, POOL_WIDTH), _BF)
    for g in range(POOL_GROUPS):
        sl = slice(g * POOL_GROUP_DIM, (g + 1) * POOL_GROUP_DIM)
        poolw_o[sl, sl] = poolw_ref[g].astype(_BF)


def _prep_weights(w_in_t, w_out, w_uq, w_ukv, pool_w):
    per_layer = lambda *shape: pl.BlockSpec((None,) + shape, lambda l: (l,) + (0,) * len(shape))
    return pl.pallas_call(
        _prep_kernel,
        grid=(DEPTH,),
        in_specs=[
            per_layer(IN_WIDTH, D_MODEL),
            per_layer(MIX_WIDTH, D_MODEL),
            per_layer(Q_LORA, MLA_HEADS * QK_DIM),
            per_layer(KV_LORA, MLA_HEADS * KV_HEAD),
            per_layer(POOL_GROUPS, POOL_GROUP_DIM, POOL_GROUP_DIM),
        ],
        out_specs=[
            per_layer(D_MODEL, _C_END),
            per_layer(MIX_WIDTH, D_MODEL),
            per_layer(Q_LORA, MLA_HEADS * HEAD_PAD),
            per_layer(MLA_HEADS * NOPE_DIM, KV_LORA),
            per_layer(KV_LORA, MLA_WIDTH),
            per_layer(POOL_WIDTH, POOL_WIDTH),
        ],
        out_shape=[
            jax.ShapeDtypeStruct((DEPTH, D_MODEL, _C_END), _BF),
            jax.ShapeDtypeStruct((DEPTH, MIX_WIDTH, D_MODEL), _BF),
            jax.ShapeDtypeStruct((DEPTH, Q_LORA, MLA_HEADS * HEAD_PAD), _BF),
            jax.ShapeDtypeStruct((DEPTH, MLA_HEADS * NOPE_DIM, KV_LORA), _BF),
            jax.ShapeDtypeStruct((DEPTH, KV_LORA, MLA_WIDTH), _BF),
            jax.ShapeDtypeStruct((DEPTH, POOL_WIDTH, POOL_WIDTH), _BF),
        ],
        compiler_params=pltpu.CompilerParams(dimension_semantics=("arbitrary",),
                                             vmem_limit_bytes=VMEM_LIMIT_BYTES),
        name="weight_prep",
    )(w_in_t, w_out, w_uq, w_ukv, pool_w)


def _pool_mix(win, rcnt):
    n = win.shape[0]
    rows = n - 2 * HALO
    lane = lax.broadcasted_iota(jnp.int32, (1, POOL_WIDTH), 1)
    sums = [win + pltpu.roll(win, 1, axis=0)]
    for k in (1, 2, 4):
        sums.append(pltpu.roll(sums[-1], k, axis=0) + pltpu.roll(sums[-1], n - k, axis=0))
    sel = sums[-1]
    for g in range(POOL_GROUPS - 2, -1, -1):
        sel = jnp.where(lane < (g + 1) * POOL_GROUP_DIM, sums[g], sel)
    return sel[HALO:HALO + rows] * rcnt - win[HALO:HALO + rows]


def _short_conv(win, w):
    rows = win.shape[0] - 2 * HALO
    return (win[HALO - 1:HALO - 1 + rows] * w[0] + win[HALO:HALO + rows] * w[1]
            + win[HALO + 1:HALO + 1 + rows] * w[2])


def _pass_kernel(*refs, n_seq, seq_len, cache_len, has_rope, emit_state, rc):
    it = iter(refs)
    x_ref, mod_ref = next(it), next(it)
    if cache_len:
        cckv_all_ref, ckr_all_ref = next(it), next(it)
    if has_rope:
        cs_ref = next(it)
    rcnt_ref, gn_ref, gq_ref, gkv_ref, pools_ref, convw_ref, gfin_ref = (next(it) for _ in range(7))
    w_hbm = [next(it) for _ in range(len(_WEIGHTS))]
    y_ref = next(it)
    if emit_state:
        ckv_out_ref, kr_out_ref = next(it), next(it)
    h_s, q_s, kt_s, v_s, px_s, prod_s, mixed_s = (next(it) for _ in range(7))
    w_vm = [next(it) for _ in range(len(_WEIGHTS))]
    w_sem = next(it)
    win_all_ref, wuq_all_ref, wukt_all_ref, wuv_all_ref, poolw_all_ref, wout_all_ref = w_vm

    def weight_copy(l, k):
        return pltpu.make_async_copy(w_hbm[k].at[l], w_vm[k].at[l], w_sem.at[l, k])

    first_step = pl.program_id(0) == 0

    @pl.when(first_step)
    def _():
        for l in range(DEPTH):
            for k in range(len(_WEIGHTS)):
                weight_copy(l, k).start()

    m = n_seq * seq_len
    keys = cache_len + seq_len
    padded = seq_len + 2 * HALO
    lane = lax.broadcasted_iota(jnp.int32, (1, ROPE_PAD), 1)
    mrow = (1 + pl.program_id(0)) if has_rope else 0
    piece = min(seq_len, rc)
    n_sub = rc // piece
    pieces = [(s, slice(s * piece, (s + 1) * piece)) for s in range(n_sub)]
    chunks = range(m // rc)

    for c in chunks:
        y_ref[c * rc:(c + 1) * rc, :] = x_ref[c * rc:(c + 1) * rc, :]

    zeros_halo = jnp.zeros((HALO, POOL_WIDTH), _F32)
    for buf in (px_s, prod_s):
        for s in range(n_seq):
            buf[s * padded:s * padded + HALO, :] = zeros_halo
            buf[(s + 1) * padded - HALO:(s + 1) * padded, :] = zeros_halo
    ones_col = jnp.where(lax.broadcasted_iota(jnp.int32, (n_seq * keys, V_DIM), 1) == 0, 1.0, 0.0).astype(_BF)
    for hd in range(MLA_HEADS):
        v_s[hd, :, V_DIM:] = ones_col

    def layer_view(l):
        lw = dict(l=l, row=pl.ds(l, 1), win=win_all_ref.at[l], wuq=wuq_all_ref.at[l], wukt=wukt_all_ref.at[l],
                  wuv=wuv_all_ref.at[l], poolw=poolw_all_ref.at[l], wout=wout_all_ref.at[l])
        lw["shift"], lw["scale"], lw["gate"] = (mod_ref[l, k, pl.ds(mrow, 1), :] for k in range(3))
        if cache_len:
            lw["cckv"], lw["ckr"] = cckv_all_ref.at[l], ckr_all_ref.at[l]
        return lw

    def rotate(v, r):
        v = v * cs_ref[pl.ds(r, rc), :]
        return v + pltpu.roll(v, ROPE_DIM, axis=1)

    def put_kv(lw, ckv, kr_t, kc):
        knt = _dot(lw["wukt"][...], ckv.T.astype(_BF))
        v4 = _dot(ckv.astype(_BF), lw["wuv"][...])
        kr_bf = kr_t.astype(_BF)
        rows_k = pl.ds(kc * rc, rc)
        for hd in range(MLA_HEADS):
            kt_s[hd, kc, :NOPE_DIM, :] = knt[hd * NOPE_DIM:(hd + 1) * NOPE_DIM, :].astype(_BF)
            kt_s[hd, kc, NOPE_DIM:, :] = kr_bf
            v_s[hd, rows_k, :V_DIM] = v4[:, hd * V_DIM:(hd + 1) * V_DIM].astype(_BF)

    def cached_keys(lw):
        ckr_t = lw["ckr"][...]
        ckr_pad = jnp.concatenate([ckr_t, jnp.zeros_like(ckr_t)], axis=0)
        for cc in range(cache_len // rc):
            rows_k = slice(cc * rc, (cc + 1) * rc)
            put_kv(lw, lw["cckv"][rows_k, :], ckr_pad[:, rows_k], cc)

    def halo_base(c, s):
        return (c * n_sub + s) * padded if seq_len <= rc else c * rc

    def normalise(lw, c):
        rows = pl.ds(c * rc, rc)
        h = _rms(y_ref[rows, :], gn_ref[lw["row"], :]) * (1.0 + lw["scale"]) + lw["shift"]
        h_s[rows, :] = h.astype(_BF)

    def project(lw, c):
        r = c * rc
        rows = pl.ds(r, rc)
        win_ref = lw["win"]
        h = h_s[rows, :]
        a = _dot(h, win_ref[:, _C_CQ:_C_GMLA])
        pe = _dot(h, win_ref[:, _C_PX:_C_CB])
        for s, ps in pieces:
            rows_h = pl.ds(halo_base(c, s) + HALO, piece)
            px_s[rows_h, :] = pe[ps, :POOL_WIDTH]
            prod_s[rows_h, :] = pe[ps, POOL_WIDTH:POOL_WIDTH + CONV_WIDTH] * pe[ps, POOL_WIDTH + CONV_WIDTH:]
        return a

    def project_heads(lw, c, a):
        r = c * rc
        rows = pl.ds(r, rc)
        row = lw["row"]
        cq = _rms(a[:, :Q_LORA], gq_ref[row, :]).astype(_BF)
        for hp in range(MLA_HEADS // 2):
            q2 = _dot(cq, lw["wuq"][:, hp * 2 * HEAD_PAD:(hp + 1) * 2 * HEAD_PAD]) * Q_SCALE
            for j in range(2):
                hd = 2 * hp + j
                c0 = j * HEAD_PAD
                q_s[hd, rows, :NOPE_DIM] = q2[:, c0:c0 + NOPE_DIM].astype(_BF)
                qr = q2[:, c0 + NOPE_DIM:c0 + HEAD_PAD]
                if has_rope:
                    qr = rotate(qr, r)
                q_s[hd, rows, NOPE_DIM:] = qr.astype(_BF)
        ckv = _rms(a[:, _C_CKV:_C_KR], gkv_ref[row, :])
        kr = a[:, _C_KR:]
        if has_rope:
            kr = rotate(kr, r)
        kr_t = jnp.where(lane < ROPE_DIM, kr, 0.0).T
        if emit_state:
            for s, ps in pieces:
                ckv_out_ref[c * n_sub + s, lw["l"]] = ckv[ps, :]
                kr_out_ref[c * n_sub + s, lw["l"]] = kr_t[:ROPE_DIM, ps]
        put_kv(lw, ckv, kr_t, cache_len // rc + c)

    def mix_attention(lw, c):
        r = c * rc
        rows = pl.ds(r, rc)
        win_ref = lw["win"]
        h = h_s[rows, :]
        for hp in range(MLA_HEADS // 2):
            g2 = _silu(_dot(h, win_ref[:, _C_GMLA + hp * 2 * V_DIM:_C_GMLA + (hp + 1) * 2 * V_DIM]))
            for j in range(2):
                hd = 2 * hp + j
                for s, ps in pieces:
                    rows_q = pl.ds(r + s * piece, piece)
                    if n_seq == 1:
                        ks = [kt_s[hd, kc] for kc in range(keys // rc)]
                        rows_k = pl.ds(0, keys)
                    else:
                        ks = [kt_s[hd, c, :, ps]]
                        rows_k = rows_q
                    q = q_s[hd, rows_q, :]
                    sc = jnp.concatenate([_dot(q, k) for k in ks], axis=-1)
                    p = jnp.exp2(sc - jnp.max(sc, axis=-1, keepdims=True))
                    ov = _dot(p.astype(_BF), v_s[hd, rows_k, :])
                    o = ov[:, :V_DIM] / ov[:, V_DIM:V_DIM + 1]
                    mixed_s[rows_q, hd * V_DIM:(hd + 1) * V_DIM] = (g2[ps, j * V_DIM:(j + 1) * V_DIM] * o).astype(_BF)

    def mix_rest(lw, c):
        r = c * rc
        rows = pl.ds(r, rc)
        row, win_ref = lw["row"], lw["win"]
        h = h_s[rows, :]
        windows = [pl.ds(halo_base(c, s), piece + 2 * HALO) for s, _ in pieces]
        rcnt = rcnt_ref[pl.ds(0 if seq_len <= rc else r, piece), :]
        pooled = jnp.concatenate([_pool_mix(px_s[w, :], rcnt) for w in windows], axis=0)
        pool = _dot(pooled.astype(_BF), lw["poolw"][...]) * pools_ref[row, :]
        gp = _silu(_dot(h, win_ref[:, _C_GPOOL:_C_PX]))
        mixed_s[rows, MLA_WIDTH:MLA_WIDTH + POOL_WIDTH] = (gp * pool).astype(_BF)

        convw = [convw_ref[k, row, :] for k in range(3)]
        conv = jnp.concatenate([_short_conv(prod_s[w, :], convw) for w in windows], axis=0)
        e2 = _dot(h, win_ref[:, _C_CB:_C_END])
        mixed_s[rows, MLA_WIDTH + POOL_WIDTH:] = (_silu(e2[:, CONV_WIDTH:]) * (e2[:, :CONV_WIDTH] * conv)).astype(_BF)

    def mix_out(lw, c):
        rows = pl.ds(c * rc, rc)
        y_ref[rows, :] = y_ref[rows, :] + lw["gate"] * _dot(mixed_s[rows, :], lw["wout"][...])

    def one_layer(l, carry):
        @pl.when(first_step)
        def _():
            for k in range(len(_WEIGHTS)):
                weight_copy(l, k).wait()

        lw = layer_view(l)
        for c in chunks:
            normalise(lw, c)
        firsts = [project(lw, c) for c in chunks]
        if cache_len:
            cached_keys(lw)
        for c in chunks:
            project_heads(lw, c, firsts[c])
        for c in chunks:
            steps = (mix_rest, mix_attention) if n_seq > 1 else (mix_attention, mix_rest)
            for step in steps + (mix_out,):
                step(lw, c)
        return carry

    lax.fori_loop(0, DEPTH, one_layer, 0)
    for c in chunks:
        rows = slice(c * rc, (c + 1) * rc)
        y_ref[rows, :] = _rms(y_ref[rows, :], gfin_ref[...])


def _mixer_pass(x2d, mod_all, wts, *, seq_len, cache=None, rope=None, emit_state):
    rows = x2d.shape[0]
    m = ROWS_PER_STEP
    n_seq = m // seq_len
    n_steps = rows // m
    n_all = rows // seq_len
    cache_len = 0 if cache is None else cache[0].shape[2]
    keys = cache_len + seq_len
    has_rope = rope is not None
    rc = ROW_CHUNK
    assert m % rc == 0 and cache_len % rc == 0
    assert (rc % seq_len == 0 and not cache_len) or (n_seq == 1 and seq_len % rc == 0 and not emit_state)

    def const(*shape):
        return pl.BlockSpec(shape, lambda i: (0,) * len(shape), pipeline_mode=pl.Buffered(1))

    args = [x2d, mod_all]
    in_specs = [pl.BlockSpec((m, D_MODEL), lambda i: (i, 0)), const(DEPTH, 3, MOD_ROWS, D_MODEL)]
    if cache_len:
        args += [cache[0], cache[1]]
        in_specs += [pl.BlockSpec((None, DEPTH, cache_len, KV_LORA), lambda i: (i, 0, 0, 0)),
                     pl.BlockSpec((None, DEPTH, ROPE_DIM, cache_len), lambda i: (i, 0, 0, 0))]
    if has_rope:
        args += [rope]
        in_specs += [const(seq_len, ROPE_PAD)]
    args += [_pool_rcnt(seq_len), wts["g_norm"], wts["g_q"], wts["g_kv"], wts["pool_s"], wts["conv_w"],
             wts["g_final"]]
    in_specs += [const(seq_len, POOL_WIDTH), const(DEPTH, D_MODEL), const(DEPTH, Q_LORA), const(DEPTH, KV_LORA),
                 const(DEPTH, POOL_WIDTH), const(3, DEPTH, CONV_WIDTH), const(1, D_MODEL)]
    weights = [wts[name] for name in _WEIGHTS]
    args += weights
    in_specs += [pl.BlockSpec(memory_space=pl.ANY)] * len(weights)

    out_shape = [jax.ShapeDtypeStruct((rows, D_MODEL), _F32)]
    out_specs = [pl.BlockSpec((m, D_MODEL), lambda i: (i, 0))]
    if emit_state:
        out_shape += [jax.ShapeDtypeStruct((n_all, DEPTH, seq_len, KV_LORA), _F32),
                      jax.ShapeDtypeStruct((n_all, DEPTH, ROPE_DIM, seq_len), _F32)]
        out_specs += [pl.BlockSpec((n_seq, DEPTH, seq_len, KV_LORA), lambda i: (i, 0, 0, 0)),
                      pl.BlockSpec((n_seq, DEPTH, ROPE_DIM, seq_len), lambda i: (i, 0, 0, 0))]

    kern = functools.partial(_pass_kernel, n_seq=n_seq, seq_len=seq_len, cache_len=cache_len,
                             has_rope=has_rope, emit_state=emit_state, rc=rc)
    return pl.pallas_call(
        kern,
        grid=(n_steps,),
        in_specs=in_specs,
        out_specs=out_specs,
        out_shape=out_shape,
        scratch_shapes=[
            pltpu.VMEM((m, D_MODEL), _BF),
            pltpu.VMEM((MLA_HEADS, m, HEAD_PAD), _BF),
            pltpu.VMEM((MLA_HEADS, n_seq * keys // rc, HEAD_PAD, rc), _BF),
            pltpu.VMEM((MLA_HEADS, n_seq * keys, 2 * V_DIM), _BF),
            pltpu.VMEM((n_seq * (seq_len + 2 * HALO), POOL_WIDTH), _F32),
            pltpu.VMEM((n_seq * (seq_len + 2 * HALO), CONV_WIDTH), _F32),
            pltpu.VMEM((m, MIX_WIDTH), _BF),
        ] + [pltpu.VMEM(w.shape, w.dtype) for w in weights] + [pltpu.SemaphoreType.DMA((DEPTH, len(weights)))],
        compiler_params=pltpu.CompilerParams(dimension_semantics=("arbitrary",),
                                             vmem_limit_bytes=VMEM_LIMIT_BYTES),
        name="mixer_pass_latent" if has_rope else "mixer_pass_context",
    )(*args)


def _pool_rcnt(seq_len):
    t = np.arange(seq_len)[:, None]
    half = np.repeat(1 << np.arange(POOL_GROUPS), POOL_GROUP_DIM)[None, :]
    cnt = np.minimum(t + half, seq_len) - np.maximum(t - half, 0)
    return jnp.asarray((1.0 / cnt).astype(np.float32))


def _rope_table(seq_len):
    rows = seq_len // GRID_W
    row = np.repeat(np.arange(rows), GRID_W).astype(np.float32)
    col = np.tile(np.arange(GRID_W), rows).astype(np.float32)
    inv = (1.0 / (np.float32(ROPE_BASE) ** (np.arange(0, AXIS_DIM, 2, dtype=np.float32) / np.float32(AXIS_DIM))))
    inv = inv.astype(np.float32)
    ang = np.concatenate([row[:, None] * inv, col[:, None] * inv], axis=-1).astype(np.float64)
    cos, sin = np.cos(ang).astype(np.float32), np.sin(ang).astype(np.float32)
    return jnp.asarray(np.concatenate([cos, cos, sin, sin], axis=-1))


def kernel(x_prompt, x_sample, cache_ckv, cache_krope, c, c_ctx, w_mod, b_mod, g_norm, w_in, g_q, w_uq,
           g_kv, w_ukv, pool_w, pool_s, conv_w, w_out, g_final):
    batch, seq, _ = x_prompt.shape
    dec_batch, dec_seq, _ = x_sample.shape
    assert 1 + dec_batch <= MOD_ROWS and dec_seq == ROWS_PER_STEP and ROWS_PER_STEP % seq == 0

    c_all = jnp.concatenate([c_ctx[None], c, jnp.zeros((MOD_ROWS - 1 - dec_batch, D_MODEL), _F32)], axis=0)
    mod_all = _modulation(c_all, w_mod, b_mod)
    w_in_r, w_out_r, w_uq_r, w_ukt, w_uv, pool_w_r = _prep_weights(
        jnp.transpose(w_in, (0, 2, 1)), w_out, w_uq, w_ukv, pool_w)
    wts = {
        "g_norm": g_norm, "w_in": w_in_r, "g_q": g_q, "w_uq": w_uq_r, "g_kv": g_kv, "w_ukt": w_ukt,
        "w_uv": w_uv, "pool_w": pool_w_r, "pool_s": pool_s, "conv_w": jnp.transpose(conv_w, (1, 0, 2)),
        "w_out": w_out_r, "g_final": g_final.reshape(1, D_MODEL),
    }

    rope = _rope_table(dec_seq)
    cache = (cache_ckv, jnp.transpose(cache_krope, (0, 1, 3, 2)))
    xp = x_prompt.reshape(batch * seq, D_MODEL)
    xs = x_sample.reshape(dec_batch * dec_seq, D_MODEL)
    xp, state_ckv, state_krope_t = _mixer_pass(xp, mod_all, wts, seq_len=seq, emit_state=True)
    (xs,) = _mixer_pass(xs, mod_all, wts, seq_len=dec_seq, cache=cache, rope=rope, emit_state=False)
    y_prompt = xp.reshape(batch, seq, D_MODEL)
    state_krope = jnp.transpose(state_krope_t, (0, 1, 3, 2))
    y_sample = xs.reshape(dec_batch, dec_seq, D_MODEL)
    return (y_prompt, y_sample, state_ckv, state_krope)
```

```python
import functools

import numpy as np
import jax
import jax.numpy as jnp
from jax import lax
from jax.experimental import pallas as pl
from jax.experimental.pallas import tpu as pltpu

D_MODEL = 1024
DEPTH = 2
GRID_W = 64
MLA_HEADS = 4
NOPE_DIM = 128
ROPE_DIM = 64
V_DIM = 128
QK_DIM = NOPE_DIM + ROPE_DIM
MLA_WIDTH = MLA_HEADS * V_DIM
Q_LORA = 384
KV_LORA = 256
POOL_GROUPS = 4
POOL_GROUP_DIM = 64
POOL_WIDTH = POOL_GROUPS * POOL_GROUP_DIM
CONV_WIDTH = 256
MIX_WIDTH = MLA_WIDTH + POOL_WIDTH + CONV_WIDTH
ROPE_BASE = 10000.0
AXIS_DIM = ROPE_DIM // 2
ATTN_SCALE = QK_DIM ** -0.5
Q_SCALE = ATTN_SCALE * float(np.log2(np.e))
EPS = 1e-6

_SPLITS = (Q_LORA, KV_LORA, ROPE_DIM, MLA_WIDTH, POOL_WIDTH, POOL_WIDTH,
           CONV_WIDTH, CONV_WIDTH, CONV_WIDTH, CONV_WIDTH)
_OFFS = [sum(_SPLITS[:i]) for i in range(len(_SPLITS) + 1)]
IN_WIDTH = _OFFS[-1]

V7X_SUBLANES = 8
V7X_MXU_DEPTH = 256
V7X_VMEM_BYTES = 64 * 1024 * 1024
HEAD_PAD = V7X_MXU_DEPTH
ROPE_PAD = HEAD_PAD - NOPE_DIM
KV_HEAD = NOPE_DIM + V_DIM

_C_CQ = 0
_C_CKV = _C_CQ + Q_LORA
_C_KR = _C_CKV + KV_LORA
_C_GMLA = _C_KR + ROPE_PAD
_C_GPOOL = _C_GMLA + MLA_WIDTH
_C_PX = _C_GPOOL + POOL_WIDTH
_C_CC = _C_PX + POOL_WIDTH
_C_CB = _C_CC + 2 * CONV_WIDTH
_C_END = _C_CB + 2 * CONV_WIDTH

MOD_ROWS = 8
ROWS_PER_STEP = 1024
ROW_CHUNK = 512
HALO = V7X_SUBLANES
PREP_COLS = 256
VMEM_LIMIT_BYTES = V7X_VMEM_BYTES - 4 * 1024 * 1024

_WEIGHTS = ("w_in", "w_uq", "w_ukt", "w_uv", "pool_w", "w_out")

_BF = jnp.bfloat16
_F32 = jnp.float32


def _dot(a, b):
    return jnp.dot(a, b, preferred_element_type=_F32)


def _rms(x, g):
    return x * lax.rsqrt(jnp.mean(x * x, axis=-1, keepdims=True) + EPS) * g


def _silu(x):
    return x * jax.nn.sigmoid(x)


def _mod_kernel(c_ref, w_ref, b_ref, o_ref):
    s = _silu(c_ref[...]).astype(_BF)
    for k in range(3):
        cols = slice(k * D_MODEL, (k + 1) * D_MODEL)
        bias = b_ref[pl.ds(pl.program_id(0), 1), cols]
        o_ref[k] = _dot(s, w_ref[:, cols].astype(_BF)) + bias


def _modulation(c_all, w_mod, b_mod):
    return pl.pallas_call(
        _mod_kernel,
        grid=(DEPTH,),
        in_specs=[
            pl.BlockSpec((MOD_ROWS, D_MODEL), lambda l: (0, 0)),
            pl.BlockSpec((None, D_MODEL, 3 * D_MODEL), lambda l: (l, 0, 0)),
            pl.BlockSpec((DEPTH, 3 * D_MODEL), lambda l: (0, 0)),
        ],
        out_specs=pl.BlockSpec((None, 3, MOD_ROWS, D_MODEL), lambda l: (l, 0, 0, 0)),
        out_shape=jax.ShapeDtypeStruct((DEPTH, 3, MOD_ROWS, D_MODEL), _F32),
        compiler_params=pltpu.CompilerParams(dimension_semantics=("arbitrary",),
                                             vmem_limit_bytes=VMEM_LIMIT_BYTES),
        name="modulation",
    )(c_all, w_mod, b_mod)


def _prep_kernel(wint_ref, wout_ref, wuq_ref, wukv_ref, poolw_ref,
                 win_o, wout_o, wuq_o, wukt_o, wuv_o, poolw_o):
    def put(dst, src, width):
        for c in range(0, width, PREP_COLS):
            n = min(PREP_COLS, width - c)
            win_o[:, dst + c:dst + c + n] = wint_ref[src + c:src + c + n, :].T.astype(_BF)

    put(_C_CQ, _OFFS[0], Q_LORA + KV_LORA)
    kr = wint_ref[_OFFS[2]:_OFFS[3], :]
    krb = jnp.concatenate([kr, -kr[AXIS_DIM:], kr[:AXIS_DIM]], axis=0)
    win_o[:, _C_KR:_C_GMLA] = krb.T.astype(_BF)
    put(_C_GMLA, _OFFS[3], MLA_WIDTH)
    put(_C_GPOOL, _OFFS[5], POOL_WIDTH)
    put(_C_PX, _OFFS[4], POOL_WIDTH)
    put(_C_CC, _OFFS[7], 2 * CONV_WIDTH)
    put(_C_CB, _OFFS[6], CONV_WIDTH)
    put(_C_CB + CONV_WIDTH, _OFFS[9], CONV_WIDTH)
    wout_o[...] = wout_ref[...].astype(_BF)

    for hd in range(MLA_HEADS):
        s0, d0 = hd * QK_DIM, hd * HEAD_PAD
        r0 = s0 + NOPE_DIM
        wuq_o[:, d0:d0 + QK_DIM] = wuq_ref[:, s0:s0 + QK_DIM].astype(_BF)
        wuq_o[:, d0 + QK_DIM:d0 + QK_DIM + AXIS_DIM] = (-wuq_ref[:, r0 + AXIS_DIM:r0 + ROPE_DIM]).astype(_BF)
        wuq_o[:, d0 + QK_DIM + AXIS_DIM:d0 + HEAD_PAD] = wuq_ref[:, r0:r0 + AXIS_DIM].astype(_BF)
    for hd in range(MLA_HEADS):
        k0 = hd * KV_HEAD
        wukt_o[hd * NOPE_DIM:(hd + 1) * NOPE_DIM, :] = wukv_ref[:, k0:k0 + NOPE_DIM].T.astype(_BF)
        wuv_o[:, hd * V_DIM:(hd + 1) * V_DIM] = wukv_ref[:, k0 + NOPE_DIM:k0 + KV_HEAD].astype(_BF)
    poolw_o[...] = jnp.zeros((POOL_WIDTH, POOL_WIDTH), _BF)
    for g in range(POOL_GROUPS):
        sl = slice(g * POOL_GROUP_DIM, (g + 1) * POOL_GROUP_DIM)
        poolw_o[sl, sl] = poolw_ref[g].astype(_BF)


def _prep_weights(w_in_t, w_out, w_uq, w_ukv, pool_w):
    per_layer = lambda *shape: pl.BlockSpec((None,) + shape, lambda l: (l,) + (0,) * len(shape))
    return pl.pallas_call(
        _prep_kernel,
        grid=(DEPTH,),
        in_specs=[
            per_layer(IN_WIDTH, D_MODEL),
            per_layer(MIX_WIDTH, D_MODEL),
            per_layer(Q_LORA, MLA_HEADS * QK_DIM),
            per_layer(KV_LORA, MLA_HEADS * KV_HEAD),
            per_layer(POOL_GROUPS, POOL_GROUP_DIM, POOL_GROUP_DIM),
        ],
        out_specs=[
            per_layer(D_MODEL, _C_END),
            per_layer(MIX_WIDTH, D_MODEL),
            per_layer(Q_LORA, MLA_HEADS * HEAD_PAD),
            per_layer(MLA_HEADS * NOPE_DIM, KV_LORA),
            per_layer(KV_LORA, MLA_WIDTH),
            per_layer(POOL_WIDTH, POOL_WIDTH),
        ],
        out_shape=[
            jax.ShapeDtypeStruct((DEPTH, D_MODEL, _C_END), _BF),
            jax.ShapeDtypeStruct((DEPTH, MIX_WIDTH, D_MODEL), _BF),
            jax.ShapeDtypeStruct((DEPTH, Q_LORA, MLA_HEADS * HEAD_PAD), _BF),
            jax.ShapeDtypeStruct((DEPTH, MLA_HEADS * NOPE_DIM, KV_LORA), _BF),
            jax.ShapeDtypeStruct((DEPTH, KV_LORA, MLA_WIDTH), _BF),
            jax.ShapeDtypeStruct((DEPTH, POOL_WIDTH, POOL_WIDTH), _BF),
        ],
        compiler_params=pltpu.CompilerParams(dimension_semantics=("arbitrary",),
                                             vmem_limit_bytes=VMEM_LIMIT_BYTES),
        name="weight_prep",
    )(w_in_t, w_out, w_uq, w_ukv, pool_w)


def _pool_mix(win, rcnt):
    n = win.shape[0]
    rows = n - 2 * HALO
    lane = lax.broadcasted_iota(jnp.int32, (1, POOL_WIDTH), 1)
    sums = [win + pltpu.roll(win, 1, axis=0)]
    for k in (1, 2, 4):
        sums.append(pltpu.roll(sums[-1], k, axis=0) + pltpu.roll(sums[-1], n - k, axis=0))
    sel = sums[-1]
    for g in range(POOL_GROUPS - 2, -1, -1):
        sel = jnp.where(lane < (g + 1) * POOL_GROUP_DIM, sums[g], sel)
    return sel[HALO:HALO + rows] * rcnt - win[HALO:HALO + rows]


def _short_conv(win, w):
    rows = win.shape[0] - 2 * HALO
    return (win[HALO - 1:HALO - 1 + rows] * w[0] + win[HALO:HALO + rows] * w[1]
            + win[HALO + 1:HALO + 1 + rows] * w[2])


def _pass_kernel(*refs, n_seq, seq_len, cache_len, has_rope, emit_state, rc):
    it = iter(refs)
    x_ref, mod_ref = next(it), next(it)
    if cache_len:
        cckv_all_ref, ckr_all_ref = next(it), next(it)
    if has_rope:
        cs_ref = next(it)
    rcnt_ref, gn_ref, gq_ref, gkv_ref, pools_ref, convw_ref, gfin_ref = (next(it) for _ in range(7))
    w_hbm = [next(it) for _ in range(len(_WEIGHTS))]
    y_ref = next(it)
    if emit_state:
        ckv_out_ref, kr_out_ref = next(it), next(it)
    h_s, q_s, kt_s, v_s, px_s, prod_s, mixed_s = (next(it) for _ in range(7))
    w_vm = [next(it) for _ in range(len(_WEIGHTS))]
    w_sem = next(it)
    win_all_ref, wuq_all_ref, wukt_all_ref, wuv_all_ref, poolw_all_ref, wout_all_ref = w_vm

    def weight_copy(l, k):
        return pltpu.make_async_copy(w_hbm[k].at[l], w_vm[k].at[l], w_sem.at[l, k])

    first_step = pl.program_id(0) == 0

    @pl.when(first_step)
    def _():
        for l in range(DEPTH):
            for k in range(len(_WEIGHTS)):
                weight_copy(l, k).start()

    m = n_seq * seq_len
    keys = cache_len + seq_len
    padded = seq_len + 2 * HALO
    lane = lax.broadcasted_iota(jnp.int32, (1, ROPE_PAD), 1)
    mrow = (1 + pl.program_id(0)) if has_rope else 0
    piece = min(seq_len, rc)
    n_sub = rc // piece
    pieces = [(s, slice(s * piece, (s + 1) * piece)) for s in range(n_sub)]
    chunks = range(m // rc)

    for c in chunks:
        y_ref[c * rc:(c + 1) * rc, :] = x_ref[c * rc:(c + 1) * rc, :]

    zeros_halo = jnp.zeros((HALO, POOL_WIDTH), _F32)
    for buf in (px_s, prod_s):
        for s in range(n_seq):
            buf[s * padded:s * padded + HALO, :] = zeros_halo
            buf[(s + 1) * padded - HALO:(s + 1) * padded, :] = zeros_halo
    ones_col = jnp.where(lax.broadcasted_iota(jnp.int32, (n_seq * keys, V_DIM), 1) == 0, 1.0, 0.0).astype(_BF)
    for hd in range(MLA_HEADS):
        v_s[hd, :, V_DIM:] = ones_col

    def layer_view(l):
        lw = dict(l=l, row=pl.ds(l, 1), win=win_all_ref.at[l], wuq=wuq_all_ref.at[l], wukt=wukt_all_ref.at[l],
                  wuv=wuv_all_ref.at[l], poolw=poolw_all_ref.at[l], wout=wout_all_ref.at[l])
        lw["shift"], lw["scale"], lw["gate"] = (mod_ref[l, k, pl.ds(mrow, 1), :] for k in range(3))
        if cache_len:
            lw["cckv"], lw["ckr"] = cckv_all_ref.at[l], ckr_all_ref.at[l]
        return lw

    def rotate(v, r):
        v = v * cs_ref[pl.ds(r, rc), :]
        return v + pltpu.roll(v, ROPE_DIM, axis=1)

    def put_kv(lw, ckv, kr_t, kc):
        knt = _dot(lw["wukt"][...], ckv.T.astype(_BF))
        v4 = _dot(ckv.astype(_BF), lw["wuv"][...])
        kr_bf = kr_t.astype(_BF)
        rows_k = pl.ds(kc * rc, rc)
        for hd in range(MLA_HEADS):
            kt_s[hd, kc, :NOPE_DIM, :] = knt[hd * NOPE_DIM:(hd + 1) * NOPE_DIM, :].astype(_BF)
            kt_s[hd, kc, NOPE_DIM:, :] = kr_bf
            v_s[hd, rows_k, :V_DIM] = v4[:, hd * V_DIM:(hd + 1) * V_DIM].astype(_BF)

    def cached_keys(lw):
        ckr_t = lw["ckr"][...]
        ckr_pad = jnp.concatenate([ckr_t, jnp.zeros_like(ckr_t)], axis=0)
        for cc in range(cache_len // rc):
            rows_k = slice(cc * rc, (cc + 1) * rc)
            put_kv(lw, lw["cckv"][rows_k, :], ckr_pad[:, rows_k], cc)

    def halo_base(c, s):
        return (c * n_sub + s) * padded if seq_len <= rc else c * rc

    def project(lw, c):
        r = c * rc
        rows = pl.ds(r, rc)
        row, win_ref = lw["row"], lw["win"]
        h = (_rms(y_ref[rows, :], gn_ref[row, :]) * (1.0 + lw["scale"]) + lw["shift"]).astype(_BF)
        h_s[rows, :] = h
        a = _dot(h, win_ref[:, _C_CQ:_C_GMLA])
        pe = _dot(h, win_ref[:, _C_PX:_C_CB])
        for s, ps in pieces:
            rows_h = pl.ds(halo_base(c, s) + HALO, piece)
            px_s[rows_h, :] = pe[ps, :POOL_WIDTH]
            prod_s[rows_h, :] = pe[ps, POOL_WIDTH:POOL_WIDTH + CONV_WIDTH] * pe[ps, POOL_WIDTH + CONV_WIDTH:]
        return a

    def project_heads(lw, c, a):
        r = c * rc
        rows = pl.ds(r, rc)
        row = lw["row"]
        cq = _rms(a[:, :Q_LORA], gq_ref[row, :]).astype(_BF)
        for hp in range(MLA_HEADS // 2):
            q2 = _dot(cq, lw["wuq"][:, hp * 2 * HEAD_PAD:(hp + 1) * 2 * HEAD_PAD]) * Q_SCALE
            for j in range(2):
                hd = 2 * hp + j
                c0 = j * HEAD_PAD
                q_s[hd, rows, :NOPE_DIM] = q2[:, c0:c0 + NOPE_DIM].astype(_BF)
                qr = q2[:, c0 + NOPE_DIM:c0 + HEAD_PAD]
                if has_rope:
                    qr = rotate(qr, r)
                q_s[hd, rows, NOPE_DIM:] = qr.astype(_BF)
        ckv = _rms(a[:, _C_CKV:_C_KR], gkv_ref[row, :])
        kr = a[:, _C_KR:]
        if has_rope:
            kr = rotate(kr, r)
        kr_t = jnp.where(lane < ROPE_DIM, kr, 0.0).T
        if emit_state:
            for s, ps in pieces:
                ckv_out_ref[c * n_sub + s, lw["l"]] = ckv[ps, :]
                kr_out_ref[c * n_sub + s, lw["l"]] = kr_t[:ROPE_DIM, ps]
        put_kv(lw, ckv, kr_t, cache_len // rc + c)

    def mix_attention(lw, c):
        r = c * rc
        rows = pl.ds(r, rc)
        win_ref = lw["win"]
        h = h_s[rows, :]
        for hp in range(MLA_HEADS // 2):
            outs = {}
            for j in range(2):
                hd = 2 * hp + j
                for s, ps in pieces:
                    rows_q = pl.ds(r + s * piece, piece)
                    if n_seq == 1:
                        ks = [kt_s[hd, kc] for kc in range(keys // rc)]
                        rows_k = pl.ds(0, keys)
                    else:
                        ks = [kt_s[hd, c, :, ps]]
                        rows_k = rows_q
                    q = q_s[hd, rows_q, :]
                    sc = jnp.concatenate([_dot(q, k) for k in ks], axis=-1)
                    p = jnp.exp2(sc - jnp.max(sc, axis=-1, keepdims=True))
                    ov = _dot(p.astype(_BF), v_s[hd, rows_k, :])
                    outs[j, s] = ov[:, :V_DIM] / ov[:, V_DIM:V_DIM + 1]
            g2 = _silu(_dot(h, win_ref[:, _C_GMLA + hp * 2 * V_DIM:_C_GMLA + (hp + 1) * 2 * V_DIM]))
            for j in range(2):
                hd = 2 * hp + j
                for s, ps in pieces:
                    rows_q = pl.ds(r + s * piece, piece)
                    mixed_s[rows_q, hd * V_DIM:(hd + 1) * V_DIM] = (
                        g2[ps, j * V_DIM:(j + 1) * V_DIM] * outs[j, s]).astype(_BF)

    def mix_rest(lw, c):
        r = c * rc
        rows = pl.ds(r, rc)
        row, win_ref = lw["row"], lw["win"]
        h = h_s[rows, :]
        windows = [pl.ds(halo_base(c, s), piece + 2 * HALO) for s, _ in pieces]
        rcnt = rcnt_ref[pl.ds(0 if seq_len <= rc else r, piece), :]
        pooled = jnp.concatenate([_pool_mix(px_s[w, :], rcnt) for w in windows], axis=0)
        pool = _dot(pooled.astype(_BF), lw["poolw"][...]) * pools_ref[row, :]
        gp = _silu(_dot(h, win_ref[:, _C_GPOOL:_C_PX]))
        mixed_s[rows, MLA_WIDTH:MLA_WIDTH + POOL_WIDTH] = (gp * pool).astype(_BF)

        convw = [convw_ref[k, row, :] for k in range(3)]
        conv = jnp.concatenate([_short_conv(prod_s[w, :], convw) for w in windows], axis=0)
        e2 = _dot(h, win_ref[:, _C_CB:_C_END])
        mixed_s[rows, MLA_WIDTH + POOL_WIDTH:] = (_silu(e2[:, CONV_WIDTH:]) * (e2[:, :CONV_WIDTH] * conv)).astype(_BF)

    def mix_out(lw, c):
        rows = pl.ds(c * rc, rc)
        y_ref[rows, :] = y_ref[rows, :] + lw["gate"] * _dot(mixed_s[rows, :], lw["wout"][...])

    def one_layer(l, carry):
        @pl.when(first_step)
        def _():
            for k in range(len(_WEIGHTS)):
                weight_copy(l, k).wait()

        lw = layer_view(l)
        firsts = [project(lw, c) for c in chunks]
        if cache_len:
            cached_keys(lw)
        for c in chunks:
            project_heads(lw, c, firsts[c])
        for c in chunks:
            steps = (mix_rest, mix_attention) if n_seq > 1 else (mix_attention, mix_rest)
            for step in steps + (mix_out,):
                step(lw, c)
        return carry

    lax.fori_loop(0, DEPTH, one_layer, 0)
    for c in chunks:
        rows = slice(c * rc, (c + 1) * rc)
        y_ref[rows, :] = _rms(y_ref[rows, :], gfin_ref[...])


def _mixer_pass(x2d, mod_all, wts, *, seq_len, cache=None, rope=None, emit_state):
    rows = x2d.shape[0]
    m = ROWS_PER_STEP
    n_seq = m // seq_len
    n_steps = rows // m
    n_all = rows // seq_len
    cache_len = 0 if cache is None else cache[0].shape[2]
    keys = cache_len + seq_len
    has_rope = rope is not None
    rc = ROW_CHUNK
    assert m % rc == 0 and cache_len % rc == 0
    assert (rc % seq_len == 0 and not cache_len) or (n_seq == 1 and seq_len % rc == 0 and not emit_state)

    def const(*shape):
        return pl.BlockSpec(shape, lambda i: (0,) * len(shape), pipeline_mode=pl.Buffered(1))

    args = [x2d, mod_all]
    in_specs = [pl.BlockSpec((m, D_MODEL), lambda i: (i, 0)), const(DEPTH, 3, MOD_ROWS, D_MODEL)]
    if cache_len:
        args += [cache[0], cache[1]]
        in_specs += [pl.BlockSpec((None, DEPTH, cache_len, KV_LORA), lambda i: (i, 0, 0, 0)),
                     pl.BlockSpec((None, DEPTH, ROPE_DIM, cache_len), lambda i: (i, 0, 0, 0))]
    if has_rope:
        args += [rope]
        in_specs += [const(seq_len, ROPE_PAD)]
    args += [_pool_rcnt(seq_len), wts["g_norm"], wts["g_q"], wts["g_kv"], wts["pool_s"], wts["conv_w"],
             wts["g_final"]]
    in_specs += [const(seq_len, POOL_WIDTH), const(DEPTH, D_MODEL), const(DEPTH, Q_LORA), const(DEPTH, KV_LORA),
                 const(DEPTH, POOL_WIDTH), const(3, DEPTH, CONV_WIDTH), const(1, D_MODEL)]
    weights = [wts[name] for name in _WEIGHTS]
    args += weights
    in_specs += [pl.BlockSpec(memory_space=pl.ANY)] * len(weights)

    out_shape = [jax.ShapeDtypeStruct((rows, D_MODEL), _F32)]
    out_specs = [pl.BlockSpec((m, D_MODEL), lambda i: (i, 0))]
    if emit_state:
        out_shape += [jax.ShapeDtypeStruct((n_all, DEPTH, seq_len, KV_LORA), _F32),
                      jax.ShapeDtypeStruct((n_all, DEPTH, ROPE_DIM, seq_len), _F32)]
        out_specs += [pl.BlockSpec((n_seq, DEPTH, seq_len, KV_LORA), lambda i: (i, 0, 0, 0)),
                      pl.BlockSpec((n_seq, DEPTH, ROPE_DIM, seq_len), lambda i: (i, 0, 0, 0))]

    kern = functools.partial(_pass_kernel, n_seq=n_seq, seq_len=seq_len, cache_len=cache_len,
                             has_rope=has_rope, emit_state=emit_state, rc=rc)
    return pl.pallas_call(
        kern,
        grid=(n_steps,),
        in_specs=in_specs,
        out_specs=out_specs,
        out_shape=out_shape,
        scratch_shapes=[
            pltpu.VMEM((m, D_MODEL), _BF),
            pltpu.VMEM((MLA_HEADS, m, HEAD_PAD), _BF),
            pltpu.VMEM((MLA_HEADS, n_seq * keys // rc, HEAD_PAD, rc), _BF),
            pltpu.VMEM((MLA_HEADS, n_seq * keys, 2 * V_DIM), _BF),
            pltpu.VMEM((n_seq * (seq_len + 2 * HALO), POOL_WIDTH), _F32),
            pltpu.VMEM((n_seq * (seq_len + 2 * HALO), CONV_WIDTH), _F32),
            pltpu.VMEM((m, MIX_WIDTH), _BF),
        ] + [pltpu.VMEM(w.shape, w.dtype) for w in weights] + [pltpu.SemaphoreType.DMA((DEPTH, len(weights)))],
        compiler_params=pltpu.CompilerParams(dimension_semantics=("arbitrary",),
                                             vmem_limit_bytes=VMEM_LIMIT_BYTES),
        name="mixer_pass_latent" if has_rope else "mixer_pass_context",
    )(*args)


def _pool_rcnt(seq_len):
    t = np.arange(seq_len)[:, None]
    half = np.repeat(1 << np.arange(POOL_GROUPS), POOL_GROUP_DIM)[None, :]
    cnt = np.minimum(t + half, seq_len) - np.maximum(t - half, 0)
    return jnp.asarray((1.0 / cnt).astype(np.float32))


def _rope_table(seq_len):
    rows = seq_len // GRID_W
    row = np.repeat(np.arange(rows), GRID_W).astype(np.float32)
    col = np.tile(np.arange(GRID_W), rows).astype(np.float32)
    inv = (1.0 / (np.float32(ROPE_BASE) ** (np.arange(0, AXIS_DIM, 2, dtype=np.float32) / np.float32(AXIS_DIM))))
    inv = inv.astype(np.float32)
    ang = np.concatenate([row[:, None] * inv, col[:, None] * inv], axis=-1).astype(np.float64)
    cos, sin = np.cos(ang).astype(np.float32), np.sin(ang).astype(np.float32)
    return jnp.asarray(np.concatenate([cos, cos, sin, sin], axis=-1))


def kernel(x_prompt, x_sample, cache_ckv, cache_krope, c, c_ctx, w_mod, b_mod, g_norm, w_in, g_q, w_uq,
           g_kv, w_ukv, pool_w, pool_s, conv_w, w_out, g_final):
    batch, seq, _ = x_prompt.shape
    dec_batch, dec_seq, _ = x_sample.shape
    assert 1 + dec_batch <= MOD_ROWS and dec_seq == ROWS_PER_STEP and ROWS_PER_STEP % seq == 0

    c_all = jnp.concatenate([c_ctx[None], c, jnp.zeros((MOD_ROWS - 1 - dec_batch, D_MODEL), _F32)], axis=0)
    mod_all = _modulation(c_all, w_mod, b_mod)
    w_in_r, w_out_r, w_uq_r, w_ukt, w_uv, pool_w_r = _prep_weights(
        jnp.transpose(w_in, (0, 2, 1)), w_out, w_uq, w_ukv, pool_w)
    wts = {
        "g_norm": g_norm, "w_in": w_in_r, "g_q": g_q, "w_uq": w_uq_r, "g_kv": g_kv, "w_ukt": w_ukt,
        "w_uv": w_uv, "pool_w": pool_w_r, "pool_s": pool_s, "conv_w": jnp.transpose(conv_w, (1, 0, 2)),
        "w_out": w_out_r, "g_final": g_final.reshape(1, D_MODEL),
    }

    rope = _rope_table(dec_seq)
    cache = (cache_ckv, jnp.transpose(cache_krope, (0, 1, 3, 2)))
    xp = x_prompt.reshape(batch * seq, D_MODEL)
    xs = x_sample.reshape(dec_batch * dec_seq, D_MODEL)
    xp, state_ckv, state_krope_t = _mixer_pass(xp, mod_all, wts, seq_len=seq, emit_state=True)
    (xs,) = _mixer_pass(xs, mod_all, wts, seq_len=dec_seq, cache=cache, rope=rope, emit_state=False)
    y_prompt = xp.reshape(batch, seq, D_MODEL)
    state_krope = jnp.transpose(state_krope_t, (0, 1, 3, 2))
    y_sample = xs.reshape(dec_batch, dec_seq, D_MODEL)
    return (y_prompt, y_sample, state_ckv, state_krope)
```

```python
import functools

import numpy as np
import jax
import jax.numpy as jnp
from jax import lax
from jax.experimental import pallas as pl
from jax.experimental.pallas import tpu as pltpu

D_MODEL = 1024
DEPTH = 2
GRID_W = 64
MLA_HEADS = 4
NOPE_DIM = 128
ROPE_DIM = 64
V_DIM = 128
QK_DIM = NOPE_DIM + ROPE_DIM
MLA_WIDTH = MLA_HEADS * V_DIM
Q_LORA = 384
KV_LORA = 256
POOL_GROUPS = 4
POOL_GROUP_DIM = 64
POOL_WIDTH = POOL_GROUPS * POOL_GROUP_DIM
CONV_WIDTH = 256
MIX_WIDTH = MLA_WIDTH + POOL_WIDTH + CONV_WIDTH
ROPE_BASE = 10000.0
AXIS_DIM = ROPE_DIM // 2
ATTN_SCALE = QK_DIM ** -0.5
Q_SCALE = ATTN_SCALE * float(np.log2(np.e))
EPS = 1e-6

_SPLITS = (Q_LORA, KV_LORA, ROPE_DIM, MLA_WIDTH, POOL_WIDTH, POOL_WIDTH,
           CONV_WIDTH, CONV_WIDTH, CONV_WIDTH, CONV_WIDTH)
_OFFS = [sum(_SPLITS[:i]) for i in range(len(_SPLITS) + 1)]
IN_WIDTH = _OFFS[-1]

V7X_SUBLANES = 8
V7X_MXU_DEPTH = 256
V7X_VMEM_BYTES = 64 * 1024 * 1024
HEAD_PAD = V7X_MXU_DEPTH
ROPE_PAD = HEAD_PAD - NOPE_DIM
KV_HEAD = NOPE_DIM + V_DIM

_C_CQ = 0
_C_CKV = _C_CQ + Q_LORA
_C_KR = _C_CKV + KV_LORA
_C_GMLA = _C_KR + ROPE_PAD
_C_GPOOL = _C_GMLA + MLA_WIDTH
_C_PX = _C_GPOOL + POOL_WIDTH
_C_CC = _C_PX + POOL_WIDTH
_C_CB = _C_CC + 2 * CONV_WIDTH
_C_END = _C_CB + 2 * CONV_WIDTH

MOD_ROWS = 8
ROWS_PER_STEP = 1024
ROW_CHUNK = 512
HALO = V7X_SUBLANES
PREP_COLS = 256
VMEM_LIMIT_BYTES = V7X_VMEM_BYTES - 4 * 1024 * 1024

_WEIGHTS = ("w_in", "w_uq", "w_ukt", "w_uv", "pool_w", "w_out")

_BF = jnp.bfloat16
_F32 = jnp.float32


def _dot(a, b):
    return jnp.dot(a, b, preferred_element_type=_F32)


def _rms(x, g):
    return x * lax.rsqrt(jnp.mean(x * x, axis=-1, keepdims=True) + EPS) * g


def _silu(x):
    return x * jax.nn.sigmoid(x)


def _mod_kernel(c_ref, w_ref, b_ref, o_ref):
    s = _silu(c_ref[...]).astype(_BF)
    for k in range(3):
        cols = slice(k * D_MODEL, (k + 1) * D_MODEL)
        bias = b_ref[pl.ds(pl.program_id(0), 1), cols]
        o_ref[k] = _dot(s, w_ref[:, cols].astype(_BF)) + bias


def _modulation(c_all, w_mod, b_mod):
    return pl.pallas_call(
        _mod_kernel,
        grid=(DEPTH,),
        in_specs=[
            pl.BlockSpec((MOD_ROWS, D_MODEL), lambda l: (0, 0)),
            pl.BlockSpec((None, D_MODEL, 3 * D_MODEL), lambda l: (l, 0, 0)),
            pl.BlockSpec((DEPTH, 3 * D_MODEL), lambda l: (0, 0)),
        ],
        out_specs=pl.BlockSpec((None, 3, MOD_ROWS, D_MODEL), lambda l: (l, 0, 0, 0)),
        out_shape=jax.ShapeDtypeStruct((DEPTH, 3, MOD_ROWS, D_MODEL), _F32),
        compiler_params=pltpu.CompilerParams(dimension_semantics=("arbitrary",),
                                             vmem_limit_bytes=VMEM_LIMIT_BYTES),
        name="modulation",
    )(c_all, w_mod, b_mod)


def _prep_kernel(wint_ref, wout_ref, wuq_ref, wukv_ref, poolw_ref,
                 win_o, wout_o, wuq_o, wukt_o, wuv_o, poolw_o):
    def put(dst, src, width):
        for c in range(0, width, PREP_COLS):
            n = min(PREP_COLS, width - c)
            win_o[:, dst + c:dst + c + n] = wint_ref[src + c:src + c + n, :].T.astype(_BF)

    put(_C_CQ, _OFFS[0], Q_LORA + KV_LORA)
    kr = wint_ref[_OFFS[2]:_OFFS[3], :]
    krb = jnp.concatenate([kr, -kr[AXIS_DIM:], kr[:AXIS_DIM]], axis=0)
    win_o[:, _C_KR:_C_GMLA] = krb.T.astype(_BF)
    put(_C_GMLA, _OFFS[3], MLA_WIDTH)
    put(_C_GPOOL, _OFFS[5], POOL_WIDTH)
    put(_C_PX, _OFFS[4], POOL_WIDTH)
    put(_C_CC, _OFFS[7], 2 * CONV_WIDTH)
    put(_C_CB, _OFFS[6], CONV_WIDTH)
    put(_C_CB + CONV_WIDTH, _OFFS[9], CONV_WIDTH)
    wout_o[...] = wout_ref[...].astype(_BF)

    for hd in range(MLA_HEADS):
        s0, d0 = hd * QK_DIM, hd * HEAD_PAD
        r0 = s0 + NOPE_DIM
        wuq_o[:, d0:d0 + QK_DIM] = wuq_ref[:, s0:s0 + QK_DIM].astype(_BF)
        wuq_o[:, d0 + QK_DIM:d0 + QK_DIM + AXIS_DIM] = (-wuq_ref[:, r0 + AXIS_DIM:r0 + ROPE_DIM]).astype(_BF)
        wuq_o[:, d0 + QK_DIM + AXIS_DIM:d0 + HEAD_PAD] = wuq_ref[:, r0:r0 + AXIS_DIM].astype(_BF)
    for hd in range(MLA_HEADS):
        k0 = hd * KV_HEAD
        wukt_o[hd * NOPE_DIM:(hd + 1) * NOPE_DIM, :] = wukv_ref[:, k0:k0 + NOPE_DIM].T.astype(_BF)
        wuv_o[:, hd * V_DIM:(hd + 1) * V_DIM] = wukv_ref[:, k0 + NOPE_DIM:k0 + KV_HEAD].astype(_BF)
    poolw_o[...] = jnp.zeros((POOL_WIDTH, POOL_WIDTH), _BF)
    for g in range(POOL_GROUPS):
        sl = slice(g * POOL_GROUP_DIM, (g + 1) * POOL_GROUP_DIM)
        poolw_o[sl, sl] = poolw_ref[g].astype(_BF)


def _prep_weights(w_in_t, w_out, w_uq, w_ukv, pool_w):
    per_layer = lambda *shape: pl.BlockSpec((None,) + shape, lambda l: (l,) + (0,) * len(shape))
    return pl.pallas_call(
        _prep_kernel,
        grid=(DEPTH,),
        in_specs=[
            per_layer(IN_WIDTH, D_MODEL),
            per_layer(MIX_WIDTH, D_MODEL),
            per_layer(Q_LORA, MLA_HEADS * QK_DIM),
            per_layer(KV_LORA, MLA_HEADS * KV_HEAD),
            per_layer(POOL_GROUPS, POOL_GROUP_DIM, POOL_GROUP_DIM),
        ],
        out_specs=[
            per_layer(D_MODEL, _C_END),
            per_layer(MIX_WIDTH, D_MODEL),
            per_layer(Q_LORA, MLA_HEADS * HEAD_PAD),
            per_layer(MLA_HEADS * NOPE_DIM, KV_LORA),
            per_layer(KV_LORA, MLA_WIDTH),
            per_layer(POOL_WIDTH, POOL_WIDTH),
        ],
        out_shape=[
            jax.ShapeDtypeStruct((DEPTH, D_MODEL, _C_END), _BF),
            jax.ShapeDtypeStruct((DEPTH, MIX_WIDTH, D_MODEL), _BF),
            jax.ShapeDtypeStruct((DEPTH, Q_LORA, MLA_HEADS * HEAD_PAD), _BF),
            jax.ShapeDtypeStruct((DEPTH, MLA_HEADS * NOPE_DIM, KV_LORA), _BF),
            jax.ShapeDtypeStruct((DEPTH, KV_LORA, MLA_WIDTH), _BF),
            jax.ShapeDtypeStruct((DEPTH, POOL_WIDTH, POOL_WIDTH), _BF),
        ],
        compiler_params=pltpu.CompilerParams(dimension_semantics=("arbitrary",),
                                             vmem_limit_bytes=VMEM_LIMIT_BYTES),
        name="weight_prep",
    )(w_in_t, w_out, w_uq, w_ukv, pool_w)


def _pool_mix(win, rcnt):
    n = win.shape[0]
    rows = n - 2 * HALO
    lane = lax.broadcasted_iota(jnp.int32, (1, POOL_WIDTH), 1)
    sums = [win + pltpu.roll(win, 1, axis=0)]
    for k in (1, 2, 4):
        sums.append(pltpu.roll(sums[-1], k, axis=0) + pltpu.roll(sums[-1], n - k, axis=0))
    sel = sums[-1]
    for g in range(POOL_GROUPS - 2, -1, -1):
        sel = jnp.where(lane < (g + 1) * POOL_GROUP_DIM, sums[g], sel)
    return sel[HALO:HALO + rows] * rcnt - win[HALO:HALO + rows]


def _short_conv(win, w):
    rows = win.shape[0] - 2 * HALO
    return (win[HALO - 1:HALO - 1 + rows] * w[0] + win[HALO:HALO + rows] * w[1]
            + win[HALO + 1:HALO + 1 + rows] * w[2])


def _pass_kernel(*refs, n_seq, seq_len, cache_len, has_rope, emit_state, rc):
    it = iter(refs)
    x_ref, mod_ref = next(it), next(it)
    if cache_len:
        cckv_all_ref, ckr_all_ref = next(it), next(it)
    if has_rope:
        cs_ref = next(it)
    rcnt_ref, gn_ref, gq_ref, gkv_ref, pools_ref, convw_ref, gfin_ref = (next(it) for _ in range(7))
    w_hbm = [next(it) for _ in range(len(_WEIGHTS))]
    y_ref = next(it)
    if emit_state:
        ckv_out_ref, kr_out_ref = next(it), next(it)
    h_s, q_s, kt_s, v_s, px_s, prod_s, mixed_s = (next(it) for _ in range(7))
    w_vm = [next(it) for _ in range(len(_WEIGHTS))]
    w_sem = next(it)
    win_all_ref, wuq_all_ref, wukt_all_ref, wuv_all_ref, poolw_all_ref, wout_all_ref = w_vm

    def weight_copy(l, k):
        return pltpu.make_async_copy(w_hbm[k].at[l], w_vm[k].at[l], w_sem.at[l, k])

    first_step = pl.program_id(0) == 0

    @pl.when(first_step)
    def _():
        for l in range(DEPTH):
            for k in range(len(_WEIGHTS)):
                weight_copy(l, k).start()

    m = n_seq * seq_len
    keys = cache_len + seq_len
    padded = seq_len + 2 * HALO
    lane = lax.broadcasted_iota(jnp.int32, (1, ROPE_PAD), 1)
    mrow = (1 + pl.program_id(0)) if has_rope else 0
    piece = min(seq_len, rc)
    n_sub = rc // piece
    pieces = [(s, slice(s * piece, (s + 1) * piece)) for s in range(n_sub)]
    chunks = range(m // rc)

    for c in chunks:
        y_ref[c * rc:(c + 1) * rc, :] = x_ref[c * rc:(c + 1) * rc, :]

    zeros_halo = jnp.zeros((HALO, POOL_WIDTH), _F32)
    for buf in (px_s, prod_s):
        for s in range(n_seq):
            buf[s * padded:s * padded + HALO, :] = zeros_halo
            buf[(s + 1) * padded - HALO:(s + 1) * padded, :] = zeros_halo
    ones_col = jnp.where(lax.broadcasted_iota(jnp.int32, (n_seq * keys, V_DIM), 1) == 0, 1.0, 0.0).astype(_BF)
    for hd in range(MLA_HEADS):
        v_s[hd, :, V_DIM:] = ones_col

    def layer_view(l):
        lw = dict(l=l, row=pl.ds(l, 1), win=win_all_ref.at[l], wuq=wuq_all_ref.at[l], wukt=wukt_all_ref.at[l],
                  wuv=wuv_all_ref.at[l], poolw=poolw_all_ref.at[l], wout=wout_all_ref.at[l])
        lw["shift"], lw["scale"], lw["gate"] = (mod_ref[l, k, pl.ds(mrow, 1), :] for k in range(3))
        if cache_len:
            lw["cckv"], lw["ckr"] = cckv_all_ref.at[l], ckr_all_ref.at[l]
        return lw

    def rotate(v, r):
        v = v * cs_ref[pl.ds(r, rc), :]
        return v + pltpu.roll(v, ROPE_DIM, axis=1)

    def put_kv(lw, ckv, kr_t, kc):
        knt = _dot(lw["wukt"][...], ckv.T.astype(_BF))
        v4 = _dot(ckv.astype(_BF), lw["wuv"][...])
        kr_bf = kr_t.astype(_BF)
        rows_k = pl.ds(kc * rc, rc)
        for hd in range(MLA_HEADS):
            kt_s[hd, kc, :NOPE_DIM, :] = knt[hd * NOPE_DIM:(hd + 1) * NOPE_DIM, :].astype(_BF)
            kt_s[hd, kc, NOPE_DIM:, :] = kr_bf
            v_s[hd, rows_k, :V_DIM] = v4[:, hd * V_DIM:(hd + 1) * V_DIM].astype(_BF)

    def cached_keys(lw):
        ckr_t = lw["ckr"][...]
        ckr_pad = jnp.concatenate([ckr_t, jnp.zeros_like(ckr_t)], axis=0)
        for cc in range(cache_len // rc):
            rows_k = slice(cc * rc, (cc + 1) * rc)
            put_kv(lw, lw["cckv"][rows_k, :], ckr_pad[:, rows_k], cc)

    def halo_base(c, s):
        return (c * n_sub + s) * padded if seq_len <= rc else c * rc

    def project(lw, c):
        r = c * rc
        rows = pl.ds(r, rc)
        row, win_ref = lw["row"], lw["win"]
        h = (_rms(y_ref[rows, :], gn_ref[row, :]) * (1.0 + lw["scale"]) + lw["shift"]).astype(_BF)
        h_s[rows, :] = h
        a = _dot(h, win_ref[:, _C_CQ:_C_GMLA])
        pe = _dot(h, win_ref[:, _C_PX:_C_CB])
        for s, ps in pieces:
            rows_h = pl.ds(halo_base(c, s) + HALO, piece)
            px_s[rows_h, :] = pe[ps, :POOL_WIDTH]
            prod_s[rows_h, :] = pe[ps, POOL_WIDTH:POOL_WIDTH + CONV_WIDTH] * pe[ps, POOL_WIDTH + CONV_WIDTH:]
        return a

    def project_heads(lw, c, a):
        r = c * rc
        rows = pl.ds(r, rc)
        row = lw["row"]
        cq = _rms(a[:, :Q_LORA], gq_ref[row, :]).astype(_BF)
        for hp in range(MLA_HEADS // 2):
            q2 = _dot(cq, lw["wuq"][:, hp * 2 * HEAD_PAD:(hp + 1) * 2 * HEAD_PAD]) * Q_SCALE
            for j in range(2):
                hd = 2 * hp + j
                c0 = j * HEAD_PAD
                q_s[hd, rows, :NOPE_DIM] = q2[:, c0:c0 + NOPE_DIM].astype(_BF)
                qr = q2[:, c0 + NOPE_DIM:c0 + HEAD_PAD]
                if has_rope:
                    qr = rotate(qr, r)
                q_s[hd, rows, NOPE_DIM:] = qr.astype(_BF)
        ckv = _rms(a[:, _C_CKV:_C_KR], gkv_ref[row, :])
        kr = a[:, _C_KR:]
        if has_rope:
            kr = rotate(kr, r)
        kr_t = jnp.where(lane < ROPE_DIM, kr, 0.0).T
        if emit_state:
            for s, ps in pieces:
                ckv_out_ref[c * n_sub + s, lw["l"]] = ckv[ps, :]
                kr_out_ref[c * n_sub + s, lw["l"]] = kr_t[:ROPE_DIM, ps]
        put_kv(lw, ckv, kr_t, cache_len // rc + c)

    def mix_attention(lw, c):
        r = c * rc
        rows = pl.ds(r, rc)
        win_ref = lw["win"]
        h = h_s[rows, :]
        gate_last = n_seq > 1
        for hp in range(MLA_HEADS // 2):
            gate = lambda: _silu(_dot(h, win_ref[:, _C_GMLA + hp * 2 * V_DIM:_C_GMLA + (hp + 1) * 2 * V_DIM]))
            g2 = None if gate_last else gate()
            outs = {}
            for j in range(2):
                hd = 2 * hp + j
                for s, ps in pieces:
                    rows_q = pl.ds(r + s * piece, piece)
                    if n_seq == 1:
                        ks = [kt_s[hd, kc] for kc in range(keys // rc)]
                        rows_k = pl.ds(0, keys)
                    else:
                        ks = [kt_s[hd, c, :, ps]]
                        rows_k = rows_q
                    q = q_s[hd, rows_q, :]
                    sc = jnp.concatenate([_dot(q, k) for k in ks], axis=-1)
                    p = jnp.exp2(sc - jnp.max(sc, axis=-1, keepdims=True))
                    ov = _dot(p.astype(_BF), v_s[hd, rows_k, :])
                    outs[j, s] = (rows_q, ov[:, :V_DIM] / ov[:, V_DIM:V_DIM + 1])
                    if not gate_last:
                        mixed_s[rows_q, hd * V_DIM:(hd + 1) * V_DIM] = (
                            g2[ps, j * V_DIM:(j + 1) * V_DIM] * outs[j, s][1]).astype(_BF)
            if gate_last:
                g2 = gate()
                for j in range(2):
                    for s, ps in pieces:
                        rows_q, o = outs[j, s]
                        mixed_s[rows_q, (2 * hp + j) * V_DIM:(2 * hp + j + 1) * V_DIM] = (
                            g2[ps, j * V_DIM:(j + 1) * V_DIM] * o).astype(_BF)

    def mix_rest(lw, c):
        r = c * rc
        rows = pl.ds(r, rc)
        row, win_ref = lw["row"], lw["win"]
        h = h_s[rows, :]
        windows = [pl.ds(halo_base(c, s), piece + 2 * HALO) for s, _ in pieces]
        rcnt = rcnt_ref[pl.ds(0 if seq_len <= rc else r, piece), :]
        pooled = jnp.concatenate([_pool_mix(px_s[w, :], rcnt) for w in windows], axis=0)
        pool = _dot(pooled.astype(_BF), lw["poolw"][...]) * pools_ref[row, :]
        gp = _silu(_dot(h, win_ref[:, _C_GPOOL:_C_PX]))
        mixed_s[rows, MLA_WIDTH:MLA_WIDTH + POOL_WIDTH] = (gp * pool).astype(_BF)

        convw = [convw_ref[k, row, :] for k in range(3)]
        conv = jnp.concatenate([_short_conv(prod_s[w, :], convw) for w in windows], axis=0)
        e2 = _dot(h, win_ref[:, _C_CB:_C_END])
        mixed_s[rows, MLA_WIDTH + POOL_WIDTH:] = (_silu(e2[:, CONV_WIDTH:]) * (e2[:, :CONV_WIDTH] * conv)).astype(_BF)

    def mix_out(lw, c):
        rows = pl.ds(c * rc, rc)
        y_ref[rows, :] = y_ref[rows, :] + lw["gate"] * _dot(mixed_s[rows, :], lw["wout"][...])

    def one_layer(l, carry):
        @pl.when(first_step)
        def _():
            for k in range(len(_WEIGHTS)):
                weight_copy(l, k).wait()

        lw = layer_view(l)
        firsts = [project(lw, c) for c in chunks]
        if cache_len:
            cached_keys(lw)
        for c in chunks:
            project_heads(lw, c, firsts[c])
        for c in chunks:
            steps = (mix_rest, mix_attention) if n_seq > 1 else (mix_attention, mix_rest)
            for step in steps + (mix_out,):
                step(lw, c)
        return carry

    lax.fori_loop(0, DEPTH, one_layer, 0)
    for c in chunks:
        rows = slice(c * rc, (c + 1) * rc)
        y_ref[rows, :] = _rms(y_ref[rows, :], gfin_ref[...])


def _mixer_pass(x2d, mod_all, wts, *, seq_len, cache=None, rope=None, emit_state):
    rows = x2d.shape[0]
    m = ROWS_PER_STEP
    n_seq = m // seq_len
    n_steps = rows // m
    n_all = rows // seq_len
    cache_len = 0 if cache is None else cache[0].shape[2]
    keys = cache_len + seq_len
    has_rope = rope is not None
    rc = ROW_CHUNK
    assert m % rc == 0 and cache_len % rc == 0
    assert (rc % seq_len == 0 and not cache_len) or (n_seq == 1 and seq_len % rc == 0 and not emit_state)

    def const(*shape):
        return pl.BlockSpec(shape, lambda i: (0,) * len(shape), pipeline_mode=pl.Buffered(1))

    args = [x2d, mod_all]
    in_specs = [pl.BlockSpec((m, D_MODEL), lambda i: (i, 0)), const(DEPTH, 3, MOD_ROWS, D_MODEL)]
    if cache_len:
        args += [cache[0], cache[1]]
        in_specs += [pl.BlockSpec((None, DEPTH, cache_len, KV_LORA), lambda i: (i, 0, 0, 0)),
                     pl.BlockSpec((None, DEPTH, ROPE_DIM, cache_len), lambda i: (i, 0, 0, 0))]
    if has_rope:
        args += [rope]
        in_specs += [const(seq_len, ROPE_PAD)]
    args += [_pool_rcnt(seq_len), wts["g_norm"], wts["g_q"], wts["g_kv"], wts["pool_s"], wts["conv_w"],
             wts["g_final"]]
    in_specs += [const(seq_len, POOL_WIDTH), const(DEPTH, D_MODEL), const(DEPTH, Q_LORA), const(DEPTH, KV_LORA),
                 const(DEPTH, POOL_WIDTH), const(3, DEPTH, CONV_WIDTH), const(1, D_MODEL)]
    weights = [wts[name] for name in _WEIGHTS]
    args += weights
    in_specs += [pl.BlockSpec(memory_space=pl.ANY)] * len(weights)

    out_shape = [jax.ShapeDtypeStruct((rows, D_MODEL), _F32)]
    out_specs = [pl.BlockSpec((m, D_MODEL), lambda i: (i, 0))]
    if emit_state:
        out_shape += [jax.ShapeDtypeStruct((n_all, DEPTH, seq_len, KV_LORA), _F32),
                      jax.ShapeDtypeStruct((n_all, DEPTH, ROPE_DIM, seq_len), _F32)]
        out_specs += [pl.BlockSpec((n_seq, DEPTH, seq_len, KV_LORA), lambda i: (i, 0, 0, 0)),
                      pl.BlockSpec((n_seq, DEPTH, ROPE_DIM, seq_len), lambda i: (i, 0, 0, 0))]

    kern = functools.partial(_pass_kernel, n_seq=n_seq, seq_len=seq_len, cache_len=cache_len,
                             has_rope=has_rope, emit_state=emit_state, rc=rc)
    return pl.pallas_call(
        kern,
        grid=(n_steps,),
        in_specs=in_specs,
        out_specs=out_specs,
        out_shape=out_shape,
        scratch_shapes=[
            pltpu.VMEM((m, D_MODEL), _BF),
            pltpu.VMEM((MLA_HEADS, m, HEAD_PAD), _BF),
            pltpu.VMEM((MLA_HEADS, n_seq * keys // rc, HEAD_PAD, rc), _BF),
            pltpu.VMEM((MLA_HEADS, n_seq * keys, 2 * V_DIM), _BF),
            pltpu.VMEM((n_seq * (seq_len + 2 * HALO), POOL_WIDTH), _F32),
            pltpu.VMEM((n_seq * (seq_len + 2 * HALO), CONV_WIDTH), _F32),
            pltpu.VMEM((m, MIX_WIDTH), _BF),
        ] + [pltpu.VMEM(w.shape, w.dtype) for w in weights] + [pltpu.SemaphoreType.DMA((DEPTH, len(weights)))],
        compiler_params=pltpu.CompilerParams(dimension_semantics=("arbitrary",),
                                             vmem_limit_bytes=VMEM_LIMIT_BYTES),
        name="mixer_pass_latent" if has_rope else "mixer_pass_context",
    )(*args)


def _pool_rcnt(seq_len):
    t = np.arange(seq_len)[:, None]
    half = np.repeat(1 << np.arange(POOL_GROUPS), POOL_GROUP_DIM)[None, :]
    cnt = np.minimum(t + half, seq_len) - np.maximum(t - half, 0)
    return jnp.asarray((1.0 / cnt).astype(np.float32))


def _rope_table(seq_len):
    rows = seq_len // GRID_W
    row = np.repeat(np.arange(rows), GRID_W).astype(np.float32)
    col = np.tile(np.arange(GRID_W), rows).astype(np.float32)
    inv = (1.0 / (np.float32(ROPE_BASE) ** (np.arange(0, AXIS_DIM, 2, dtype=np.float32) / np.float32(AXIS_DIM))))
    inv = inv.astype(np.float32)
    ang = np.concatenate([row[:, None] * inv, col[:, None] * inv], axis=-1).astype(np.float64)
    cos, sin = np.cos(ang).astype(np.float32), np.sin(ang).astype(np.float32)
    return jnp.asarray(np.concatenate([cos, cos, sin, sin], axis=-1))


def kernel(x_prompt, x_sample, cache_ckv, cache_krope, c, c_ctx, w_mod, b_mod, g_norm, w_in, g_q, w_uq,
           g_kv, w_ukv, pool_w, pool_s, conv_w, w_out, g_final):
    batch, seq, _ = x_prompt.shape
    dec_batch, dec_seq, _ = x_sample.shape
    assert 1 + dec_batch <= MOD_ROWS and dec_seq == ROWS_PER_STEP and ROWS_PER_STEP % seq == 0

    c_all = jnp.concatenate([c_ctx[None], c, jnp.zeros((MOD_ROWS - 1 - dec_batch, D_MODEL), _F32)], axis=0)
    mod_all = _modulation(c_all, w_mod, b_mod)
    w_in_r, w_out_r, w_uq_r, w_ukt, w_uv, pool_w_r = _prep_weights(
        jnp.transpose(w_in, (0, 2, 1)), w_out, w_uq, w_ukv, pool_w)
    wts = {
        "g_norm": g_norm, "w_in": w_in_r, "g_q": g_q, "w_uq": w_uq_r, "g_kv": g_kv, "w_ukt": w_ukt,
        "w_uv": w_uv, "pool_w": pool_w_r, "pool_s": pool_s, "conv_w": jnp.transpose(conv_w, (1, 0, 2)),
        "w_out": w_out_r, "g_final": g_final.reshape(1, D_MODEL),
    }

    rope = _rope_table(dec_seq)
    cache = (cache_ckv, jnp.transpose(cache_krope, (0, 1, 3, 2)))
    xp = x_prompt.reshape(batch * seq, D_MODEL)
    xs = x_sample.reshape(dec_batch * dec_seq, D_MODEL)
    xp, state_ckv, state_krope_t = _mixer_pass(xp, mod_all, wts, seq_len=seq, emit_state=True)
    (xs,) = _mixer_pass(xs, mod_all, wts, seq_len=dec_seq, cache=cache, rope=rope, emit_state=False)
    y_prompt = xp.reshape(batch, seq, D_MODEL)
    state_krope = jnp.transpose(state_krope_t, (0, 1, 3, 2))
    y_sample = xs.reshape(dec_batch, dec_seq, D_MODEL)
    return (y_prompt, y_sample, state_ckv, state_krope)
```

```python
import functools

import numpy as np
import jax
import jax.numpy as jnp
from jax import lax
from jax.experimental import pallas as pl
from jax.experimental.pallas import tpu as pltpu

D_MODEL = 1024
DEPTH = 2
GRID_W = 64
MLA_HEADS = 4
NOPE_DIM = 128
ROPE_DIM = 64
V_DIM = 128
QK_DIM = NOPE_DIM + ROPE_DIM
MLA_WIDTH = MLA_HEADS * V_DIM
Q_LORA = 384
KV_LORA = 256
POOL_GROUPS = 4
POOL_GROUP_DIM = 64
POOL_WIDTH = POOL_GROUPS * POOL_GROUP_DIM
CONV_WIDTH = 256
MIX_WIDTH = MLA_WIDTH + POOL_WIDTH + CONV_WIDTH
ROPE_BASE = 10000.0
AXIS_DIM = ROPE_DIM // 2
ATTN_SCALE = QK_DIM ** -0.5
Q_SCALE = ATTN_SCALE * float(np.log2(np.e))
EPS = 1e-6

_SPLITS = (Q_LORA, KV_LORA, ROPE_DIM, MLA_WIDTH, POOL_WIDTH, POOL_WIDTH,
           CONV_WIDTH, CONV_WIDTH, CONV_WIDTH, CONV_WIDTH)
_OFFS = [sum(_SPLITS[:i]) for i in range(len(_SPLITS) + 1)]
IN_WIDTH = _OFFS[-1]

V7X_SUBLANES = 8
V7X_MXU_DEPTH = 256
V7X_VMEM_BYTES = 64 * 1024 * 1024
HEAD_PAD = V7X_MXU_DEPTH
ROPE_PAD = HEAD_PAD - NOPE_DIM
KV_HEAD = NOPE_DIM + V_DIM

_C_CQ = 0
_C_CKV = _C_CQ + Q_LORA
_C_KR = _C_CKV + KV_LORA
_C_GMLA = _C_KR + ROPE_PAD
_C_GPOOL = _C_GMLA + MLA_WIDTH
_C_PX = _C_GPOOL + POOL_WIDTH
_C_CC = _C_PX + POOL_WIDTH
_C_CB = _C_CC + 2 * CONV_WIDTH
_C_END = _C_CB + 2 * CONV_WIDTH

MOD_ROWS = 8
ROWS_PER_STEP = 1024
ROW_CHUNK = 512
HALO = V7X_SUBLANES
PREP_COLS = 256
VMEM_LIMIT_BYTES = V7X_VMEM_BYTES - 4 * 1024 * 1024

_WEIGHTS = ("w_in", "w_uq", "w_ukt", "w_uv", "pool_w", "w_out")

_BF = jnp.bfloat16
_F32 = jnp.float32


def _dot(a, b):
    return jnp.dot(a, b, preferred_element_type=_F32)


def _rms(x, g):
    return x * lax.rsqrt(jnp.mean(x * x, axis=-1, keepdims=True) + EPS) * g


def _silu(x):
    return x * jax.nn.sigmoid(x)


def _mod_kernel(c_ref, w_hbm, b_ref, o_ref, w_vm, w_sem):
    def block_copy(l, k):
        cols = slice(k * D_MODEL, (k + 1) * D_MODEL)
        return pltpu.make_async_copy(w_hbm.at[l, :, cols], w_vm.at[l, k], w_sem.at[l, k])

    blocks = [(l, k) for l in range(DEPTH) for k in range(3)]
    for l, k in blocks:
        block_copy(l, k).start()
    s = _silu(c_ref[...]).astype(_BF)
    for l, k in blocks:
        block_copy(l, k).wait()
        bias = b_ref[l:l + 1, k * D_MODEL:(k + 1) * D_MODEL]
        o_ref[l, k] = _dot(s, w_vm[l, k].astype(_BF)) + bias


def _modulation(c_all, w_mod, b_mod):
    return pl.pallas_call(
        _mod_kernel,
        in_specs=[
            pl.BlockSpec(memory_space=pltpu.VMEM),
            pl.BlockSpec(memory_space=pl.ANY),
            pl.BlockSpec(memory_space=pltpu.VMEM),
        ],
        out_specs=pl.BlockSpec(memory_space=pltpu.VMEM),
        out_shape=jax.ShapeDtypeStruct((DEPTH, 3, MOD_ROWS, D_MODEL), _F32),
        scratch_shapes=[pltpu.VMEM((DEPTH, 3, D_MODEL, D_MODEL), _F32),
                        pltpu.SemaphoreType.DMA((DEPTH, 3))],
        compiler_params=pltpu.CompilerParams(vmem_limit_bytes=VMEM_LIMIT_BYTES),
        name="modulation",
    )(c_all, w_mod, b_mod)


def _prep_kernel(wint_ref, wout_ref, wuq_ref, wukv_ref, poolw_ref,
                 win_o, wout_o, wuq_o, wukt_o, wuv_o, poolw_o):
    def put(dst, src, width):
        for c in range(0, width, PREP_COLS):
            n = min(PREP_COLS, width - c)
            win_o[:, dst + c:dst + c + n] = wint_ref[src + c:src + c + n, :].T.astype(_BF)

    put(_C_CQ, _OFFS[0], Q_LORA + KV_LORA)
    kr = wint_ref[_OFFS[2]:_OFFS[3], :]
    krb = jnp.concatenate([kr, -kr[AXIS_DIM:], kr[:AXIS_DIM]], axis=0)
    win_o[:, _C_KR:_C_GMLA] = krb.T.astype(_BF)
    put(_C_GMLA, _OFFS[3], MLA_WIDTH)
    put(_C_GPOOL, _OFFS[5], POOL_WIDTH)
    put(_C_PX, _OFFS[4], POOL_WIDTH)
    put(_C_CC, _OFFS[7], 2 * CONV_WIDTH)
    put(_C_CB, _OFFS[6], CONV_WIDTH)
    put(_C_CB + CONV_WIDTH, _OFFS[9], CONV_WIDTH)
    wout_o[...] = wout_ref[...].astype(_BF)

    for hd in range(MLA_HEADS):
        s0, d0 = hd * QK_DIM, hd * HEAD_PAD
        r0 = s0 + NOPE_DIM
        wuq_o[:, d0:d0 + QK_DIM] = wuq_ref[:, s0:s0 + QK_DIM].astype(_BF)
        wuq_o[:, d0 + QK_DIM:d0 + QK_DIM + AXIS_DIM] = (-wuq_ref[:, r0 + AXIS_DIM:r0 + ROPE_DIM]).astype(_BF)
        wuq_o[:, d0 + QK_DIM + AXIS_DIM:d0 + HEAD_PAD] = wuq_ref[:, r0:r0 + AXIS_DIM].astype(_BF)
    for hd in range(MLA_HEADS):
        k0 = hd * KV_HEAD
        wukt_o[hd * NOPE_DIM:(hd + 1) * NOPE_DIM, :] = wukv_ref[:, k0:k0 + NOPE_DIM].T.astype(_BF)
        wuv_o[:, hd * V_DIM:(hd + 1) * V_DIM] = wukv_ref[:, k0 + NOPE_DIM:k0 + KV_HEAD].astype(_BF)
    poolw_o[...] = jnp.zeros((POOL_WIDTH, POOL_WIDTH), _BF)
    for g in range(POOL_GROUPS):
        sl = slice(g * POOL_GROUP_DIM, (g + 1) * POOL_GROUP_DIM)
        poolw_o[sl, sl] = poolw_ref[g].astype(_BF)


def _prep_weights(w_in_t, w_out, w_uq, w_ukv, pool_w):
    per_layer = lambda *shape: pl.BlockSpec((None,) + shape, lambda l: (l,) + (0,) * len(shape))
    return pl.pallas_call(
        _prep_kernel,
        grid=(DEPTH,),
        in_specs=[
            per_layer(IN_WIDTH, D_MODEL),
            per_layer(MIX_WIDTH, D_MODEL),
            per_layer(Q_LORA, MLA_HEADS * QK_DIM),
            per_layer(KV_LORA, MLA_HEADS * KV_HEAD),
            per_layer(POOL_GROUPS, POOL_GROUP_DIM, POOL_GROUP_DIM),
        ],
        out_specs=[
            per_layer(D_MODEL, _C_END),
            per_layer(MIX_WIDTH, D_MODEL),
            per_layer(Q_LORA, MLA_HEADS * HEAD_PAD),
            per_layer(MLA_HEADS * NOPE_DIM, KV_LORA),
            per_layer(KV_LORA, MLA_WIDTH),
            per_layer(POOL_WIDTH, POOL_WIDTH),
        ],
        out_shape=[
            jax.ShapeDtypeStruct((DEPTH, D_MODEL, _C_END), _BF),
            jax.ShapeDtypeStruct((DEPTH, MIX_WIDTH, D_MODEL), _BF),
            jax.ShapeDtypeStruct((DEPTH, Q_LORA, MLA_HEADS * HEAD_PAD), _BF),
            jax.ShapeDtypeStruct((DEPTH, MLA_HEADS * NOPE_DIM, KV_LORA), _BF),
            jax.ShapeDtypeStruct((DEPTH, KV_LORA, MLA_WIDTH), _BF),
            jax.ShapeDtypeStruct((DEPTH, POOL_WIDTH, POOL_WIDTH), _BF),
        ],
        compiler_params=pltpu.CompilerParams(dimension_semantics=("arbitrary",),
                                             vmem_limit_bytes=VMEM_LIMIT_BYTES),
        name="weight_prep",
    )(w_in_t, w_out, w_uq, w_ukv, pool_w)


def _pool_mix(win, rcnt):
    n = win.shape[0]
    rows = n - 2 * HALO
    lane = lax.broadcasted_iota(jnp.int32, (1, POOL_WIDTH), 1)
    sums = [win + pltpu.roll(win, 1, axis=0)]
    for k in (1, 2, 4):
        sums.append(pltpu.roll(sums[-1], k, axis=0) + pltpu.roll(sums[-1], n - k, axis=0))
    sel = sums[-1]
    for g in range(POOL_GROUPS - 2, -1, -1):
        sel = jnp.where(lane < (g + 1) * POOL_GROUP_DIM, sums[g], sel)
    return sel[HALO:HALO + rows] * rcnt - win[HALO:HALO + rows]


def _short_conv(win, w):
    rows = win.shape[0] - 2 * HALO
    return (win[HALO - 1:HALO - 1 + rows] * w[0] + win[HALO:HALO + rows] * w[1]
            + win[HALO + 1:HALO + 1 + rows] * w[2])


def _pass_kernel(*refs, n_seq, seq_len, cache_len, has_rope, emit_state, rc):
    it = iter(refs)
    x_ref, mod_ref = next(it), next(it)
    if cache_len:
        cckv_all_ref, ckr_all_ref = next(it), next(it)
    if has_rope:
        cs_ref = next(it)
    rcnt_ref, gn_ref, gq_ref, gkv_ref, pools_ref, convw_ref, gfin_ref = (next(it) for _ in range(7))
    w_hbm = [next(it) for _ in range(len(_WEIGHTS))]
    y_ref = next(it)
    if emit_state:
        ckv_out_ref, kr_out_ref = next(it), next(it)
    h_s, q_s, kt_s, v_s, px_s, prod_s, mixed_s = (next(it) for _ in range(7))
    w_vm = [next(it) for _ in range(len(_WEIGHTS))]
    w_sem = next(it)
    win_all_ref, wuq_all_ref, wukt_all_ref, wuv_all_ref, poolw_all_ref, wout_all_ref = w_vm

    def weight_copy(l, k):
        return pltpu.make_async_copy(w_hbm[k].at[l], w_vm[k].at[l], w_sem.at[l, k])

    first_step = pl.program_id(0) == 0

    @pl.when(first_step)
    def _():
        for l in range(DEPTH):
            for k in range(len(_WEIGHTS)):
                weight_copy(l, k).start()

    m = n_seq * seq_len
    keys = cache_len + seq_len
    padded = seq_len + 2 * HALO
    lane = lax.broadcasted_iota(jnp.int32, (1, ROPE_PAD), 1)
    mrow = (1 + pl.program_id(0)) if has_rope else 0
    piece = min(seq_len, rc)
    n_sub = rc // piece
    pieces = [(s, slice(s * piece, (s + 1) * piece)) for s in range(n_sub)]
    chunks = range(m // rc)

    for c in chunks:
        y_ref[c * rc:(c + 1) * rc, :] = x_ref[c * rc:(c + 1) * rc, :]

    zeros_halo = jnp.zeros((HALO, POOL_WIDTH), _F32)
    for buf in (px_s, prod_s):
        for s in range(n_seq):
            buf[s * padded:s * padded + HALO, :] = zeros_halo
            buf[(s + 1) * padded - HALO:(s + 1) * padded, :] = zeros_halo
    ones_col = jnp.where(lax.broadcasted_iota(jnp.int32, (n_seq * keys, V_DIM), 1) == 0, 1.0, 0.0).astype(_BF)
    for hd in range(MLA_HEADS):
        v_s[hd, :, V_DIM:] = ones_col

    def layer_view(l):
        lw = dict(l=l, row=pl.ds(l, 1), win=win_all_ref.at[l], wuq=wuq_all_ref.at[l], wukt=wukt_all_ref.at[l],
                  wuv=wuv_all_ref.at[l], poolw=poolw_all_ref.at[l], wout=wout_all_ref.at[l])
        lw["shift"], lw["scale"], lw["gate"] = (mod_ref[l, k, pl.ds(mrow, 1), :] for k in range(3))
        if cache_len:
            lw["cckv"], lw["ckr"] = cckv_all_ref.at[l], ckr_all_ref.at[l]
        return lw

    def rotate(v, r):
        v = v * cs_ref[pl.ds(r, rc), :]
        return v + pltpu.roll(v, ROPE_DIM, axis=1)

    def put_kv(lw, ckv, kr_t, kc):
        knt = _dot(lw["wukt"][...], ckv.T.astype(_BF))
        v4 = _dot(ckv.astype(_BF), lw["wuv"][...])
        kr_bf = kr_t.astype(_BF)
        rows_k = pl.ds(kc * rc, rc)
        for hd in range(MLA_HEADS):
            kt_s[hd, kc, :NOPE_DIM, :] = knt[hd * NOPE_DIM:(hd + 1) * NOPE_DIM, :].astype(_BF)
            kt_s[hd, kc, NOPE_DIM:, :] = kr_bf
            v_s[hd, rows_k, :V_DIM] = v4[:, hd * V_DIM:(hd + 1) * V_DIM].astype(_BF)

    def cached_keys(lw):
        ckr_t = lw["ckr"][...]
        ckr_pad = jnp.concatenate([ckr_t, jnp.zeros_like(ckr_t)], axis=0)
        for cc in range(cache_len // rc):
            rows_k = slice(cc * rc, (cc + 1) * rc)
            put_kv(lw, lw["cckv"][rows_k, :], ckr_pad[:, rows_k], cc)

    def halo_base(c, s):
        return (c * n_sub + s) * padded if seq_len <= rc else c * rc

    def project(lw, c):
        r = c * rc
        rows = pl.ds(r, rc)
        row, win_ref = lw["row"], lw["win"]
        h = (_rms(y_ref[rows, :], gn_ref[row, :]) * (1.0 + lw["scale"]) + lw["shift"]).astype(_BF)
        h_s[rows, :] = h
        a = _dot(h, win_ref[:, _C_CQ:_C_GMLA])
        pe = _dot(h, win_ref[:, _C_PX:_C_CB])
        for s, ps in pieces:
            rows_h = pl.ds(halo_base(c, s) + HALO, piece)
            px_s[rows_h, :] = pe[ps, :POOL_WIDTH]
            prod_s[rows_h, :] = pe[ps, POOL_WIDTH:POOL_WIDTH + CONV_WIDTH] * pe[ps, POOL_WIDTH + CONV_WIDTH:]
        return a

    def project_heads(lw, c, a):
        r = c * rc
        rows = pl.ds(r, rc)
        row = lw["row"]
        cq = _rms(a[:, :Q_LORA], gq_ref[row, :]).astype(_BF)
        for hp in range(MLA_HEADS // 2):
            q2 = _dot(cq, lw["wuq"][:, hp * 2 * HEAD_PAD:(hp + 1) * 2 * HEAD_PAD]) * Q_SCALE
            for j in range(2):
                hd = 2 * hp + j
                c0 = j * HEAD_PAD
                q_s[hd, rows, :NOPE_DIM] = q2[:, c0:c0 + NOPE_DIM].astype(_BF)
                qr = q2[:, c0 + NOPE_DIM:c0 + HEAD_PAD]
                if has_rope:
                    qr = rotate(qr, r)
                q_s[hd, rows, NOPE_DIM:] = qr.astype(_BF)
        ckv = _rms(a[:, _C_CKV:_C_KR], gkv_ref[row, :])
        kr = a[:, _C_KR:]
        if has_rope:
            kr = rotate(kr, r)
        kr_t = jnp.where(lane < ROPE_DIM, kr, 0.0).T
        if emit_state:
            for s, ps in pieces:
                ckv_out_ref[c * n_sub + s, lw["l"]] = ckv[ps, :]
                kr_out_ref[c * n_sub + s, lw["l"]] = kr_t[:ROPE_DIM, ps]
        put_kv(lw, ckv, kr_t, cache_len // rc + c)

    def mix_attention(lw, c):
        r = c * rc
        rows = pl.ds(r, rc)
        win_ref = lw["win"]
        h = h_s[rows, :]
        for hp in range(MLA_HEADS // 2):
            g2 = _silu(_dot(h, win_ref[:, _C_GMLA + hp * 2 * V_DIM:_C_GMLA + (hp + 1) * 2 * V_DIM]))
            for j in range(2):
                hd = 2 * hp + j
                for s, ps in pieces:
                    rows_q = pl.ds(r + s * piece, piece)
                    if n_seq == 1:
                        ks = [kt_s[hd, kc] for kc in range(keys // rc)]
                        rows_k = pl.ds(0, keys)
                    else:
                        ks = [kt_s[hd, c, :, ps]]
                        rows_k = rows_q
                    q = q_s[hd, rows_q, :]
                    sc = jnp.concatenate([_dot(q, k) for k in ks], axis=-1)
                    p = jnp.exp2(sc - jnp.max(sc, axis=-1, keepdims=True))
                    ov = _dot(p.astype(_BF), v_s[hd, rows_k, :])
                    o = ov[:, :V_DIM] / ov[:, V_DIM:V_DIM + 1]
                    mixed_s[rows_q, hd * V_DIM:(hd + 1) * V_DIM] = (g2[ps, j * V_DIM:(j + 1) * V_DIM] * o).astype(_BF)

    def mix_rest(lw, c):
        r = c * rc
        rows = pl.ds(r, rc)
        row, win_ref = lw["row"], lw["win"]
        h = h_s[rows, :]
        windows = [pl.ds(halo_base(c, s), piece + 2 * HALO) for s, _ in pieces]
        rcnt = rcnt_ref[pl.ds(0 if seq_len <= rc else r, piece), :]
        pooled = jnp.concatenate([_pool_mix(px_s[w, :], rcnt) for w in windows], axis=0)
        pool = _dot(pooled.astype(_BF), lw["poolw"][...]) * pools_ref[row, :]
        gp = _silu(_dot(h, win_ref[:, _C_GPOOL:_C_PX]))
        mixed_s[rows, MLA_WIDTH:MLA_WIDTH + POOL_WIDTH] = (gp * pool).astype(_BF)

        convw = [convw_ref[k, row, :] for k in range(3)]
        conv = jnp.concatenate([_short_conv(prod_s[w, :], convw) for w in windows], axis=0)
        e2 = _dot(h, win_ref[:, _C_CB:_C_END])
        mixed_s[rows, MLA_WIDTH + POOL_WIDTH:] = (_silu(e2[:, CONV_WIDTH:]) * (e2[:, :CONV_WIDTH] * conv)).astype(_BF)

    def mix_out(lw, c):
        rows = pl.ds(c * rc, rc)
        y_ref[rows, :] = y_ref[rows, :] + lw["gate"] * _dot(mixed_s[rows, :], lw["wout"][...])

    def one_layer(l, carry):
        @pl.when(first_step)
        def _():
            for k in range(len(_WEIGHTS)):
                weight_copy(l, k).wait()

        lw = layer_view(l)
        firsts = [project(lw, c) for c in chunks]
        if cache_len:
            cached_keys(lw)
        for c in chunks:
            project_heads(lw, c, firsts[c])
        for c in chunks:
            steps = (mix_rest, mix_attention) if n_seq > 1 else (mix_attention, mix_rest)
            for step in steps + (mix_out,):
                step(lw, c)
        return carry

    lax.fori_loop(0, DEPTH, one_layer, 0)
    for c in chunks:
        rows = slice(c * rc, (c + 1) * rc)
        y_ref[rows, :] = _rms(y_ref[rows, :], gfin_ref[...])


def _mixer_pass(x2d, mod_all, wts, *, seq_len, cache=None, rope=None, emit_state):
    rows = x2d.shape[0]
    m = ROWS_PER_STEP
    n_seq = m // seq_len
    n_steps = rows // m
    n_all = rows // seq_len
    cache_len = 0 if cache is None else cache[0].shape[2]
    keys = cache_len + seq_len
    has_rope = rope is not None
    rc = ROW_CHUNK
    assert m % rc == 0 and cache_len % rc == 0
    assert (rc % seq_len == 0 and not cache_len) or (n_seq == 1 and seq_len % rc == 0 and not emit_state)

    def const(*shape):
        return pl.BlockSpec(shape, lambda i: (0,) * len(shape), pipeline_mode=pl.Buffered(1))

    args = [x2d, mod_all]
    in_specs = [pl.BlockSpec((m, D_MODEL), lambda i: (i, 0)), const(DEPTH, 3, MOD_ROWS, D_MODEL)]
    if cache_len:
        args += [cache[0], cache[1]]
        in_specs += [pl.BlockSpec((None, DEPTH, cache_len, KV_LORA), lambda i: (i, 0, 0, 0)),
                     pl.BlockSpec((None, DEPTH, ROPE_DIM, cache_len), lambda i: (i, 0, 0, 0))]
    if has_rope:
        args += [rope]
        in_specs += [const(seq_len, ROPE_PAD)]
    args += [_pool_rcnt(seq_len), wts["g_norm"], wts["g_q"], wts["g_kv"], wts["pool_s"], wts["conv_w"],
             wts["g_final"]]
    in_specs += [const(seq_len, POOL_WIDTH), const(DEPTH, D_MODEL), const(DEPTH, Q_LORA), const(DEPTH, KV_LORA),
                 const(DEPTH, POOL_WIDTH), const(3, DEPTH, CONV_WIDTH), const(1, D_MODEL)]
    weights = [wts[name] for name in _WEIGHTS]
    args += weights
    in_specs += [pl.BlockSpec(memory_space=pl.ANY)] * len(weights)

    out_shape = [jax.ShapeDtypeStruct((rows, D_MODEL), _F32)]
    out_specs = [pl.BlockSpec((m, D_MODEL), lambda i: (i, 0))]
    if emit_state:
        out_shape += [jax.ShapeDtypeStruct((n_all, DEPTH, seq_len, KV_LORA), _F32),
                      jax.ShapeDtypeStruct((n_all, DEPTH, ROPE_DIM, seq_len), _F32)]
        out_specs += [pl.BlockSpec((n_seq, DEPTH, seq_len, KV_LORA), lambda i: (i, 0, 0, 0)),
                      pl.BlockSpec((n_seq, DEPTH, ROPE_DIM, seq_len), lambda i: (i, 0, 0, 0))]

    kern = functools.partial(_pass_kernel, n_seq=n_seq, seq_len=seq_len, cache_len=cache_len,
                             has_rope=has_rope, emit_state=emit_state, rc=rc)
    return pl.pallas_call(
        kern,
        grid=(n_steps,),
        in_specs=in_specs,
        out_specs=out_specs,
        out_shape=out_shape,
        scratch_shapes=[
            pltpu.VMEM((m, D_MODEL), _BF),
            pltpu.VMEM((MLA_HEADS, m, HEAD_PAD), _BF),
            pltpu.VMEM((MLA_HEADS, n_seq * keys // rc, HEAD_PAD, rc), _BF),
            pltpu.VMEM((MLA_HEADS, n_seq * keys, 2 * V_DIM), _BF),
            pltpu.VMEM((n_seq * (seq_len + 2 * HALO), POOL_WIDTH), _F32),
            pltpu.VMEM((n_seq * (seq_len + 2 * HALO), CONV_WIDTH), _F32),
            pltpu.VMEM((m, MIX_WIDTH), _BF),
        ] + [pltpu.VMEM(w.shape, w.dtype) for w in weights] + [pltpu.SemaphoreType.DMA((DEPTH, len(weights)))],
        compiler_params=pltpu.CompilerParams(dimension_semantics=("arbitrary",),
                                             vmem_limit_bytes=VMEM_LIMIT_BYTES),
        name="mixer_pass_latent" if has_rope else "mixer_pass_context",
    )(*args)


def _pool_rcnt(seq_len):
    t = np.arange(seq_len)[:, None]
    half = np.repeat(1 << np.arange(POOL_GROUPS), POOL_GROUP_DIM)[None, :]
    cnt = np.minimum(t + half, seq_len) - np.maximum(t - half, 0)
    return jnp.asarray((1.0 / cnt).astype(np.float32))


def _rope_table(seq_len):
    rows = seq_len // GRID_W
    row = np.repeat(np.arange(rows), GRID_W).astype(np.float32)
    col = np.tile(np.arange(GRID_W), rows).astype(np.float32)
    inv = (1.0 / (np.float32(ROPE_BASE) ** (np.arange(0, AXIS_DIM, 2, dtype=np.float32) / np.float32(AXIS_DIM))))
    inv = inv.astype(np.float32)
    ang = np.concatenate([row[:, None] * inv, col[:, None] * inv], axis=-1).astype(np.float64)
    cos, sin = np.cos(ang).astype(np.float32), np.sin(ang).astype(np.float32)
    return jnp.asarray(np.concatenate([cos, cos, sin, sin], axis=-1))


def kernel(x_prompt, x_sample, cache_ckv, cache_krope, c, c_ctx, w_mod, b_mod, g_norm, w_in, g_q, w_uq,
           g_kv, w_ukv, pool_w, pool_s, conv_w, w_out, g_final):
    batch, seq, _ = x_prompt.shape
    dec_batch, dec_seq, _ = x_sample.shape
    assert 1 + dec_batch <= MOD_ROWS and dec_seq == ROWS_PER_STEP and ROWS_PER_STEP % seq == 0

    c_all = jnp.concatenate([c_ctx[None], c, jnp.zeros((MOD_ROWS - 1 - dec_batch, D_MODEL), _F32)], axis=0)
    mod_all = _modulation(c_all, w_mod, b_mod)
    w_in_r, w_out_r, w_uq_r, w_ukt, w_uv, pool_w_r = _prep_weights(
        jnp.transpose(w_in, (0, 2, 1)), w_out, w_uq, w_ukv, pool_w)
    wts = {
        "g_norm": g_norm, "w_in": w_in_r, "g_q": g_q, "w_uq": w_uq_r, "g_kv": g_kv, "w_ukt": w_ukt,
        "w_uv": w_uv, "pool_w": pool_w_r, "pool_s": pool_s, "conv_w": jnp.transpose(conv_w, (1, 0, 2)),
        "w_out": w_out_r, "g_final": g_final.reshape(1, D_MODEL),
    }

    rope = _rope_table(dec_seq)
    cache = (cache_ckv, jnp.transpose(cache_krope, (0, 1, 3, 2)))
    xp = x_prompt.reshape(batch * seq, D_MODEL)
    xs = x_sample.reshape(dec_batch * dec_seq, D_MODEL)
    xp, state_ckv, state_krope_t = _mixer_pass(xp, mod_all, wts, seq_len=seq, emit_state=True)
    (xs,) = _mixer_pass(xs, mod_all, wts, seq_len=dec_seq, cache=cache, rope=rope, emit_state=False)
    y_prompt = xp.reshape(batch, seq, D_MODEL)
    state_krope = jnp.transpose(state_krope_t, (0, 1, 3, 2))
    y_sample = xs.reshape(dec_batch, dec_seq, D_MODEL)
    return (y_prompt, y_sample, state_ckv, state_krope)
```

```python
import functools

import numpy as np
import jax
import jax.numpy as jnp
from jax import lax
from jax.experimental import pallas as pl
from jax.experimental.pallas import tpu as pltpu

D_MODEL = 1024
DEPTH = 2
GRID_W = 64
MLA_HEADS = 4
NOPE_DIM = 128
ROPE_DIM = 64
V_DIM = 128
QK_DIM = NOPE_DIM + ROPE_DIM
MLA_WIDTH = MLA_HEADS * V_DIM
Q_LORA = 384
KV_LORA = 256
POOL_GROUPS = 4
POOL_GROUP_DIM = 64
POOL_WIDTH = POOL_GROUPS * POOL_GROUP_DIM
CONV_WIDTH = 256
MIX_WIDTH = MLA_WIDTH + POOL_WIDTH + CONV_WIDTH
ROPE_BASE = 10000.0
AXIS_DIM = ROPE_DIM // 2
ATTN_SCALE = QK_DIM ** -0.5
Q_SCALE = ATTN_SCALE * float(np.log2(np.e))
EPS = 1e-6

_SPLITS = (Q_LORA, KV_LORA, ROPE_DIM, MLA_WIDTH, POOL_WIDTH, POOL_WIDTH,
           CONV_WIDTH, CONV_WIDTH, CONV_WIDTH, CONV_WIDTH)
_OFFS = [sum(_SPLITS[:i]) for i in range(len(_SPLITS) + 1)]
IN_WIDTH = _OFFS[-1]

V7X_SUBLANES = 8
V7X_MXU_DEPTH = 256
V7X_VMEM_BYTES = 64 * 1024 * 1024
HEAD_PAD = V7X_MXU_DEPTH
ROPE_PAD = HEAD_PAD - NOPE_DIM
KV_HEAD = NOPE_DIM + V_DIM

_C_CQ = 0
_C_CKV = _C_CQ + Q_LORA
_C_KR = _C_CKV + KV_LORA
_C_GMLA = _C_KR + ROPE_PAD
_C_GPOOL = _C_GMLA + MLA_WIDTH
_C_PX = _C_GPOOL + POOL_WIDTH
_C_CC = _C_PX + POOL_WIDTH
_C_CB = _C_CC + 2 * CONV_WIDTH
_C_END = _C_CB + 2 * CONV_WIDTH

MOD_ROWS = 8
ROWS_PER_STEP = 1024
ROW_CHUNK = 512
HALO = V7X_SUBLANES
PREP_COLS = 256
VMEM_LIMIT_BYTES = V7X_VMEM_BYTES - 4 * 1024 * 1024

_WEIGHTS = ("w_in", "w_uq", "w_ukt", "w_uv", "pool_w", "w_out")

_BF = jnp.bfloat16
_F32 = jnp.float32


def _dot(a, b):
    return jnp.dot(a, b, preferred_element_type=_F32)


def _rms(x, g):
    return x * lax.rsqrt(jnp.mean(x * x, axis=-1, keepdims=True) + EPS) * g


def _silu(x):
    return x * jax.nn.sigmoid(x)


def _mod_kernel(c_ref, w_ref, b_ref, o_ref):
    s = _silu(c_ref[...]).astype(_BF)
    bias = b_ref[pl.ds(pl.program_id(0) * 3 + pl.program_id(1), 1), :]
    o_ref[...] = _dot(s, w_ref[...].astype(_BF)) + bias


def _modulation(c_all, w_mod, b_mod):
    return pl.pallas_call(
        _mod_kernel,
        grid=(DEPTH, 3),
        in_specs=[
            pl.BlockSpec((MOD_ROWS, D_MODEL), lambda l, k: (0, 0)),
            pl.BlockSpec((None, D_MODEL, D_MODEL), lambda l, k: (l, 0, k)),
            pl.BlockSpec((DEPTH * 3, D_MODEL), lambda l, k: (0, 0)),
        ],
        out_specs=pl.BlockSpec((None, None, MOD_ROWS, D_MODEL), lambda l, k: (l, k, 0, 0)),
        out_shape=jax.ShapeDtypeStruct((DEPTH, 3, MOD_ROWS, D_MODEL), _F32),
        compiler_params=pltpu.CompilerParams(dimension_semantics=("arbitrary", "arbitrary"),
                                             vmem_limit_bytes=VMEM_LIMIT_BYTES),
        name="modulation",
    )(c_all, w_mod, b_mod.reshape(DEPTH * 3, D_MODEL))


def _prep_kernel(wint_ref, wout_ref, wuq_ref, wukv_ref, poolw_ref,
                 win_o, wout_o, wuq_o, wukt_o, wuv_o, poolw_o):
    def put(dst, src, width):
        for c in range(0, width, PREP_COLS):
            n = min(PREP_COLS, width - c)
            win_o[:, dst + c:dst + c + n] = wint_ref[src + c:src + c + n, :].T.astype(_BF)

    put(_C_CQ, _OFFS[0], Q_LORA + KV_LORA)
    kr = wint_ref[_OFFS[2]:_OFFS[3], :]
    krb = jnp.concatenate([kr, -kr[AXIS_DIM:], kr[:AXIS_DIM]], axis=0)
    win_o[:, _C_KR:_C_GMLA] = krb.T.astype(_BF)
    put(_C_GMLA, _OFFS[3], MLA_WIDTH)
    put(_C_GPOOL, _OFFS[5], POOL_WIDTH)
    put(_C_PX, _OFFS[4], POOL_WIDTH)
    put(_C_CC, _OFFS[7], 2 * CONV_WIDTH)
    put(_C_CB, _OFFS[6], CONV_WIDTH)
    put(_C_CB + CONV_WIDTH, _OFFS[9], CONV_WIDTH)
    wout_o[...] = wout_ref[...].astype(_BF)

    for hd in range(MLA_HEADS):
        s0, d0 = hd * QK_DIM, hd * HEAD_PAD
        r0 = s0 + NOPE_DIM
        wuq_o[:, d0:d0 + QK_DIM] = wuq_ref[:, s0:s0 + QK_DIM].astype(_BF)
        wuq_o[:, d0 + QK_DIM:d0 + QK_DIM + AXIS_DIM] = (-wuq_ref[:, r0 + AXIS_DIM:r0 + ROPE_DIM]).astype(_BF)
        wuq_o[:, d0 + QK_DIM + AXIS_DIM:d0 + HEAD_PAD] = wuq_ref[:, r0:r0 + AXIS_DIM].astype(_BF)
    for hd in range(MLA_HEADS):
        k0 = hd * KV_HEAD
        wukt_o[hd * NOPE_DIM:(hd + 1) * NOPE_DIM, :] = wukv_ref[:, k0:k0 + NOPE_DIM].T.astype(_BF)
        wuv_o[:, hd * V_DIM:(hd + 1) * V_DIM] = wukv_ref[:, k0 + NOPE_DIM:k0 + KV_HEAD].astype(_BF)
    poolw_o[...] = jnp.zeros((POOL_WIDTH, POOL_WIDTH), _BF)
    for g in range(POOL_GROUPS):
        sl = slice(g * POOL_GROUP_DIM, (g + 1) * POOL_GROUP_DIM)
        poolw_o[sl, sl] = poolw_ref[g].astype(_BF)


def _prep_weights(w_in_t, w_out, w_uq, w_ukv, pool_w):
    per_layer = lambda *shape: pl.BlockSpec((None,) + shape, lambda l: (l,) + (0,) * len(shape))
    return pl.pallas_call(
        _prep_kernel,
        grid=(DEPTH,),
        in_specs=[
            per_layer(IN_WIDTH, D_MODEL),
            per_layer(MIX_WIDTH, D_MODEL),
            per_layer(Q_LORA, MLA_HEADS * QK_DIM),
            per_layer(KV_LORA, MLA_HEADS * KV_HEAD),
            per_layer(POOL_GROUPS, POOL_GROUP_DIM, POOL_GROUP_DIM),
        ],
        out_specs=[
            per_layer(D_MODEL, _C_END),
            per_layer(MIX_WIDTH, D_MODEL),
            per_layer(Q_LORA, MLA_HEADS * HEAD_PAD),
            per_layer(MLA_HEADS * NOPE_DIM, KV_LORA),
            per_layer(KV_LORA, MLA_WIDTH),
            per_layer(POOL_WIDTH, POOL_WIDTH),
        ],
        out_shape=[
            jax.ShapeDtypeStruct((DEPTH, D_MODEL, _C_END), _BF),
            jax.ShapeDtypeStruct((DEPTH, MIX_WIDTH, D_MODEL), _BF),
            jax.ShapeDtypeStruct((DEPTH, Q_LORA, MLA_HEADS * HEAD_PAD), _BF),
            jax.ShapeDtypeStruct((DEPTH, MLA_HEADS * NOPE_DIM, KV_LORA), _BF),
            jax.ShapeDtypeStruct((DEPTH, KV_LORA, MLA_WIDTH), _BF),
            jax.ShapeDtypeStruct((DEPTH, POOL_WIDTH, POOL_WIDTH), _BF),
        ],
        compiler_params=pltpu.CompilerParams(dimension_semantics=("arbitrary",),
                                             vmem_limit_bytes=VMEM_LIMIT_BYTES),
        name="weight_prep",
    )(w_in_t, w_out, w_uq, w_ukv, pool_w)


def _pool_mix(win, rcnt):
    n = win.shape[0]
    rows = n - 2 * HALO
    lane = lax.broadcasted_iota(jnp.int32, (1, POOL_WIDTH), 1)
    sums = [win + pltpu.roll(win, 1, axis=0)]
    for k in (1, 2, 4):
        sums.append(pltpu.roll(sums[-1], k, axis=0) + pltpu.roll(sums[-1], n - k, axis=0))
    sel = sums[-1]
    for g in range(POOL_GROUPS - 2, -1, -1):
        sel = jnp.where(lane < (g + 1) * POOL_GROUP_DIM, sums[g], sel)
    return sel[HALO:HALO + rows] * rcnt - win[HALO:HALO + rows]


def _short_conv(win, w):
    rows = win.shape[0] - 2 * HALO
    return (win[HALO - 1:HALO - 1 + rows] * w[0] + win[HALO:HALO + rows] * w[1]
            + win[HALO + 1:HALO + 1 + rows] * w[2])


def _pass_kernel(*refs, n_seq, seq_len, cache_len, has_rope, emit_state, rc):
    it = iter(refs)
    x_ref, mod_ref = next(it), next(it)
    if cache_len:
        cckv_all_ref, ckr_all_ref = next(it), next(it)
    if has_rope:
        cs_ref = next(it)
    rcnt_ref, gn_ref, gq_ref, gkv_ref, pools_ref, convw_ref, gfin_ref = (next(it) for _ in range(7))
    w_hbm = [next(it) for _ in range(len(_WEIGHTS))]
    y_ref = next(it)
    if emit_state:
        ckv_out_ref, kr_out_ref = next(it), next(it)
    h_s, q_s, kt_s, v_s, px_s, prod_s, mixed_s = (next(it) for _ in range(7))
    w_vm = [next(it) for _ in range(len(_WEIGHTS))]
    w_sem = next(it)
    win_all_ref, wuq_all_ref, wukt_all_ref, wuv_all_ref, poolw_all_ref, wout_all_ref = w_vm

    def weight_copy(l, k):
        return pltpu.make_async_copy(w_hbm[k].at[l], w_vm[k].at[l], w_sem.at[l, k])

    first_step = pl.program_id(0) == 0

    @pl.when(first_step)
    def _():
        for l in range(DEPTH):
            for k in range(len(_WEIGHTS)):
                weight_copy(l, k).start()

    m = n_seq * seq_len
    keys = cache_len + seq_len
    padded = seq_len + 2 * HALO
    lane = lax.broadcasted_iota(jnp.int32, (1, ROPE_PAD), 1)
    mrow = (1 + pl.program_id(0)) if has_rope else 0
    piece = min(seq_len, rc)
    n_sub = rc // piece
    pieces = [(s, slice(s * piece, (s + 1) * piece)) for s in range(n_sub)]
    chunks = range(m // rc)

    for c in chunks:
        y_ref[c * rc:(c + 1) * rc, :] = x_ref[c * rc:(c + 1) * rc, :]

    zeros_halo = jnp.zeros((HALO, POOL_WIDTH), _F32)
    for buf in (px_s, prod_s):
        for s in range(n_seq):
            buf[s * padded:s * padded + HALO, :] = zeros_halo
            buf[(s + 1) * padded - HALO:(s + 1) * padded, :] = zeros_halo
    ones_col = jnp.where(lax.broadcasted_iota(jnp.int32, (n_seq * keys, V_DIM), 1) == 0, 1.0, 0.0).astype(_BF)
    for hd in range(MLA_HEADS):
        v_s[hd, :, V_DIM:] = ones_col

    def layer_view(l):
        lw = dict(l=l, row=pl.ds(l, 1), win=win_all_ref.at[l], wuq=wuq_all_ref.at[l], wukt=wukt_all_ref.at[l],
                  wuv=wuv_all_ref.at[l], poolw=poolw_all_ref.at[l], wout=wout_all_ref.at[l])
        lw["shift"], lw["scale"], lw["gate"] = (mod_ref[l, k, pl.ds(mrow, 1), :] for k in range(3))
        if cache_len:
            lw["cckv"], lw["ckr"] = cckv_all_ref.at[l], ckr_all_ref.at[l]
        return lw

    def rotate(v, r):
        v = v * cs_ref[pl.ds(r, rc), :]
        return v + pltpu.roll(v, ROPE_DIM, axis=1)

    def put_kv(lw, ckv, kr_t, kc):
        knt = _dot(lw["wukt"][...], ckv.T.astype(_BF))
        v4 = _dot(ckv.astype(_BF), lw["wuv"][...])
        kr_bf = kr_t.astype(_BF)
        rows_k = pl.ds(kc * rc, rc)
        for hd in range(MLA_HEADS):
            kt_s[hd, kc, :NOPE_DIM, :] = knt[hd * NOPE_DIM:(hd + 1) * NOPE_DIM, :].astype(_BF)
            kt_s[hd, kc, NOPE_DIM:, :] = kr_bf
            v_s[hd, rows_k, :V_DIM] = v4[:, hd * V_DIM:(hd + 1) * V_DIM].astype(_BF)

    def cached_keys(lw):
        ckr_t = lw["ckr"][...]
        ckr_pad = jnp.concatenate([ckr_t, jnp.zeros_like(ckr_t)], axis=0)
        for cc in range(cache_len // rc):
            rows_k = slice(cc * rc, (cc + 1) * rc)
            put_kv(lw, lw["cckv"][rows_k, :], ckr_pad[:, rows_k], cc)

    def halo_base(c, s):
        return (c * n_sub + s) * padded if seq_len <= rc else c * rc

    def project(lw, c):
        r = c * rc
        rows = pl.ds(r, rc)
        row, win_ref = lw["row"], lw["win"]
        h = (_rms(y_ref[rows, :], gn_ref[row, :]) * (1.0 + lw["scale"]) + lw["shift"]).astype(_BF)
        h_s[rows, :] = h
        a = _dot(h, win_ref[:, _C_CQ:_C_GMLA])
        pe = _dot(h, win_ref[:, _C_PX:_C_CB])
        for s, ps in pieces:
            rows_h = pl.ds(halo_base(c, s) + HALO, piece)
            px_s[rows_h, :] = pe[ps, :POOL_WIDTH]
            prod_s[rows_h, :] = pe[ps, POOL_WIDTH:POOL_WIDTH + CONV_WIDTH] * pe[ps, POOL_WIDTH + CONV_WIDTH:]
        return a

    def project_heads(lw, c, a):
        r = c * rc
        rows = pl.ds(r, rc)
        row = lw["row"]
        cq = _rms(a[:, :Q_LORA], gq_ref[row, :]).astype(_BF)
        for hp in range(MLA_HEADS // 2):
            q2 = _dot(cq, lw["wuq"][:, hp * 2 * HEAD_PAD:(hp + 1) * 2 * HEAD_PAD]) * Q_SCALE
            for j in range(2):
                hd = 2 * hp + j
                c0 = j * HEAD_PAD
                q_s[hd, rows, :NOPE_DIM] = q2[:, c0:c0 + NOPE_DIM].astype(_BF)
                qr = q2[:, c0 + NOPE_DIM:c0 + HEAD_PAD]
                if has_rope:
                    qr = rotate(qr, r)
                q_s[hd, rows, NOPE_DIM:] = qr.astype(_BF)
        ckv = _rms(a[:, _C_CKV:_C_KR], gkv_ref[row, :])
        kr = a[:, _C_KR:]
        if has_rope:
            kr = rotate(kr, r)
        kr_t = jnp.where(lane < ROPE_DIM, kr, 0.0).T
        if emit_state:
            for s, ps in pieces:
                ckv_out_ref[c * n_sub + s, lw["l"]] = ckv[ps, :]
                kr_out_ref[c * n_sub + s, lw["l"]] = kr_t[:ROPE_DIM, ps]
        put_kv(lw, ckv, kr_t, cache_len // rc + c)

    def mix_attention(lw, c):
        r = c * rc
        rows = pl.ds(r, rc)
        win_ref = lw["win"]
        h = h_s[rows, :]
        for hp in range(MLA_HEADS // 2):
            g2 = _silu(_dot(h, win_ref[:, _C_GMLA + hp * 2 * V_DIM:_C_GMLA + (hp + 1) * 2 * V_DIM]))
            for j in range(2):
                hd = 2 * hp + j
                for s, ps in pieces:
                    rows_q = pl.ds(r + s * piece, piece)
                    if n_seq == 1:
                        ks = [kt_s[hd, kc] for kc in range(keys // rc)]
                        rows_k = pl.ds(0, keys)
                    else:
                        ks = [kt_s[hd, c, :, ps]]
                        rows_k = rows_q
                    q = q_s[hd, rows_q, :]
                    sc = jnp.concatenate([_dot(q, k) for k in ks], axis=-1)
                    p = jnp.exp2(sc - jnp.max(sc, axis=-1, keepdims=True))
                    ov = _dot(p.astype(_BF), v_s[hd, rows_k, :])
                    o = ov[:, :V_DIM] / ov[:, V_DIM:V_DIM + 1]
                    mixed_s[rows_q, hd * V_DIM:(hd + 1) * V_DIM] = (g2[ps, j * V_DIM:(j + 1) * V_DIM] * o).astype(_BF)

    def mix_rest(lw, c):
        r = c * rc
        rows = pl.ds(r, rc)
        row, win_ref = lw["row"], lw["win"]
        h = h_s[rows, :]
        windows = [pl.ds(halo_base(c, s), piece + 2 * HALO) for s, _ in pieces]
        rcnt = rcnt_ref[pl.ds(0 if seq_len <= rc else r, piece), :]
        pooled = jnp.concatenate([_pool_mix(px_s[w, :], rcnt) for w in windows], axis=0)
        pool = _dot(pooled.astype(_BF), lw["poolw"][...]) * pools_ref[row, :]
        gp = _silu(_dot(h, win_ref[:, _C_GPOOL:_C_PX]))
        mixed_s[rows, MLA_WIDTH:MLA_WIDTH + POOL_WIDTH] = (gp * pool).astype(_BF)

        convw = [convw_ref[k, row, :] for k in range(3)]
        conv = jnp.concatenate([_short_conv(prod_s[w, :], convw) for w in windows], axis=0)
        e2 = _dot(h, win_ref[:, _C_CB:_C_END])
        mixed_s[rows, MLA_WIDTH + POOL_WIDTH:] = (_silu(e2[:, CONV_WIDTH:]) * (e2[:, :CONV_WIDTH] * conv)).astype(_BF)

    def mix_out(lw, c):
        rows = pl.ds(c * rc, rc)
        y_ref[rows, :] = y_ref[rows, :] + lw["gate"] * _dot(mixed_s[rows, :], lw["wout"][...])

    def one_layer(l, carry):
        @pl.when(first_step)
        def _():
            for k in range(len(_WEIGHTS)):
                weight_copy(l, k).wait()

        lw = layer_view(l)
        firsts = [project(lw, c) for c in chunks]
        if cache_len:
            cached_keys(lw)
        for c in chunks:
            project_heads(lw, c, firsts[c])
        for c in chunks:
            steps = (mix_rest, mix_attention) if n_seq > 1 else (mix_attention, mix_rest)
            for step in steps + (mix_out,):
                step(lw, c)
        return carry

    lax.fori_loop(0, DEPTH, one_layer, 0)
    for c in chunks:
        rows = slice(c * rc, (c + 1) * rc)
        y_ref[rows, :] = _rms(y_ref[rows, :], gfin_ref[...])


def _mixer_pass(x2d, mod_all, wts, *, seq_len, cache=None, rope=None, emit_state):
    rows = x2d.shape[0]
    m = ROWS_PER_STEP
    n_seq = m // seq_len
    n_steps = rows // m
    n_all = rows // seq_len
    cache_len = 0 if cache is None else cache[0].shape[2]
    keys = cache_len + seq_len
    has_rope = rope is not None
    rc = ROW_CHUNK
    assert m % rc == 0 and cache_len % rc == 0
    assert (rc % seq_len == 0 and not cache_len) or (n_seq == 1 and seq_len % rc == 0 and not emit_state)

    def const(*shape):
        return pl.BlockSpec(shape, lambda i: (0,) * len(shape), pipeline_mode=pl.Buffered(1))

    args = [x2d, mod_all]
    in_specs = [pl.BlockSpec((m, D_MODEL), lambda i: (i, 0)), const(DEPTH, 3, MOD_ROWS, D_MODEL)]
    if cache_len:
        args += [cache[0], cache[1]]
        in_specs += [pl.BlockSpec((None, DEPTH, cache_len, KV_LORA), lambda i: (i, 0, 0, 0)),
                     pl.BlockSpec((None, DEPTH, ROPE_DIM, cache_len), lambda i: (i, 0, 0, 0))]
    if has_rope:
        args += [rope]
        in_specs += [const(seq_len, ROPE_PAD)]
    args += [_pool_rcnt(seq_len), wts["g_norm"], wts["g_q"], wts["g_kv"], wts["pool_s"], wts["conv_w"],
             wts["g_final"]]
    in_specs += [const(seq_len, POOL_WIDTH), const(DEPTH, D_MODEL), const(DEPTH, Q_LORA), const(DEPTH, KV_LORA),
                 const(DEPTH, POOL_WIDTH), const(3, DEPTH, CONV_WIDTH), const(1, D_MODEL)]
    weights = [wts[name] for name in _WEIGHTS]
    args += weights
    in_specs += [pl.BlockSpec(memory_space=pl.ANY)] * len(weights)

    out_shape = [jax.ShapeDtypeStruct((rows, D_MODEL), _F32)]
    out_specs = [pl.BlockSpec((m, D_MODEL), lambda i: (i, 0))]
    if emit_state:
        out_shape += [jax.ShapeDtypeStruct((n_all, DEPTH, seq_len, KV_LORA), _F32),
                      jax.ShapeDtypeStruct((n_all, DEPTH, ROPE_DIM, seq_len), _F32)]
        out_specs += [pl.BlockSpec((n_seq, DEPTH, seq_len, KV_LORA), lambda i: (i, 0, 0, 0)),
                      pl.BlockSpec((n_seq, DEPTH, ROPE_DIM, seq_len), lambda i: (i, 0, 0, 0))]

    kern = functools.partial(_pass_kernel, n_seq=n_seq, seq_len=seq_len, cache_len=cache_len,
                             has_rope=has_rope, emit_state=emit_state, rc=rc)
    return pl.pallas_call(
        kern,
        grid=(n_steps,),
        in_specs=in_specs,
        out_specs=out_specs,
        out_shape=out_shape,
        scratch_shapes=[
            pltpu.VMEM((m, D_MODEL), _BF),
            pltpu.VMEM((MLA_HEADS, m, HEAD_PAD), _BF),
            pltpu.VMEM((MLA_HEADS, n_seq * keys // rc, HEAD_PAD, rc), _BF),
            pltpu.VMEM((MLA_HEADS, n_seq * keys, 2 * V_DIM), _BF),
            pltpu.VMEM((n_seq * (seq_len + 2 * HALO), POOL_WIDTH), _F32),
            pltpu.VMEM((n_seq * (seq_len + 2 * HALO), CONV_WIDTH), _F32),
            pltpu.VMEM((m, MIX_WIDTH), _BF),
        ] + [pltpu.VMEM(w.shape, w.dtype) for w in weights] + [pltpu.SemaphoreType.DMA((DEPTH, len(weights)))],
        compiler_params=pltpu.CompilerParams(dimension_semantics=("arbitrary",),
                                             vmem_limit_bytes=VMEM_LIMIT_BYTES),
        name="mixer_pass_latent" if has_rope else "mixer_pass_context",
    )(*args)


def _pool_rcnt(seq_len):
    t = np.arange(seq_len)[:, None]
    half = np.repeat(1 << np.arange(POOL_GROUPS), POOL_GROUP_DIM)[None, :]
    cnt = np.minimum(t + half, seq_len) - np.maximum(t - half, 0)
    return jnp.asarray((1.0 / cnt).astype(np.float32))


def _rope_table(seq_len):
    rows = seq_len // GRID_W
    row = np.repeat(np.arange(rows), GRID_W).astype(np.float32)
    col = np.tile(np.arange(GRID_W), rows).astype(np.float32)
    inv = (1.0 / (np.float32(ROPE_BASE) ** (np.arange(0, AXIS_DIM, 2, dtype=np.float32) / np.float32(AXIS_DIM))))
    inv = inv.astype(np.float32)
    ang = np.concatenate([row[:, None] * inv, col[:, None] * inv], axis=-1).astype(np.float64)
    cos, sin = np.cos(ang).astype(np.float32), np.sin(ang).astype(np.float32)
    return jnp.asarray(np.concatenate([cos, cos, sin, sin], axis=-1))


def kernel(x_prompt, x_sample, cache_ckv, cache_krope, c, c_ctx, w_mod, b_mod, g_norm, w_in, g_q, w_uq,
           g_kv, w_ukv, pool_w, pool_s, conv_w, w_out, g_final):
    batch, seq, _ = x_prompt.shape
    dec_batch, dec_seq, _ = x_sample.shape
    assert 1 + dec_batch <= MOD_ROWS and dec_seq == ROWS_PER_STEP and ROWS_PER_STEP % seq == 0

    c_all = jnp.concatenate([c_ctx[None], c, jnp.zeros((MOD_ROWS - 1 - dec_batch, D_MODEL), _F32)], axis=0)
    mod_all = _modulation(c_all, w_mod, b_mod)
    w_in_r, w_out_r, w_uq_r, w_ukt, w_uv, pool_w_r = _prep_weights(
        jnp.transpose(w_in, (0, 2, 1)), w_out, w_uq, w_ukv, pool_w)
    wts = {
        "g_norm": g_norm, "w_in": w_in_r, "g_q": g_q, "w_uq": w_uq_r, "g_kv": g_kv, "w_ukt": w_ukt,
        "w_uv": w_uv, "pool_w": pool_w_r, "pool_s": pool_s, "conv_w": jnp.transpose(conv_w, (1, 0, 2)),
        "w_out": w_out_r, "g_final": g_final.reshape(1, D_MODEL),
    }

    rope = _rope_table(dec_seq)
    cache = (cache_ckv, jnp.transpose(cache_krope, (0, 1, 3, 2)))
    xp = x_prompt.reshape(batch * seq, D_MODEL)
    xs = x_sample.reshape(dec_batch * dec_seq, D_MODEL)
    xp, state_ckv, state_krope_t = _mixer_pass(xp, mod_all, wts, seq_len=seq, emit_state=True)
    (xs,) = _mixer_pass(xs, mod_all, wts, seq_len=dec_seq, cache=cache, rope=rope, emit_state=False)
    y_prompt = xp.reshape(batch, seq, D_MODEL)
    state_krope = jnp.transpose(state_krope_t, (0, 1, 3, 2))
    y_sample = xs.reshape(dec_batch, dec_seq, D_MODEL)
    return (y_prompt, y_sample, state_ckv, state_krope)
```
